```python
import jax, jax.numpy as jnp
from jax import lax
import numpy as np

D_MODEL = 1024
BATCH = 8
SEQ = 2048
DEPTH = 2

CHUNK = 64
CONV_WIDTH = 512
CONV_KERNEL = 31
RWKV_WIDTH = 512
RWKV_HEAD = 64
RWKV_HEADS = RWKV_WIDTH // RWKV_HEAD
LORA_W = 64
LORA_A = 64
LORA_G = 128
N_EXPERTS = 64
TOP_K = 8
N_GROUPS = 8
TOPK_GROUPS = 4
D_EXPERT = 256
D_SHARED = 256
ROUTED_SCALE = 2.5
EXPERT_BLOCK = 256
RMS_EPS = 1e-6
LN_EPS = 1e-5
GN_EPS = 64e-5

COL_CONV = 2 * CONV_WIDTH
COL_RWKV = 3 * RWKV_WIDTH + LORA_W + LORA_A + LORA_G
COL_GATE = 2 * D_MODEL
D_IN = COL_CONV + COL_RWKV + COL_GATE

kernel_name = 'hybrid_conv_rwkv7_moe_stream_encoder'


def rmsnorm(x, g):
    x32 = x.astype(jnp.float32)
    y = x32 * lax.rsqrt(jnp.mean(x32 * x32, axis=-1, keepdims=True) + RMS_EPS)
    return (y * g.astype(jnp.float32)).astype(x.dtype)


def layernorm(x, g, b):
    x32 = x.astype(jnp.float32)
    mu = jnp.mean(x32, axis=-1, keepdims=True)
    var = jnp.mean(jnp.square(x32 - mu), axis=-1, keepdims=True)
    y = (x32 - mu) * lax.rsqrt(var + LN_EPS) * g.astype(jnp.float32) + b.astype(jnp.float32)
    return y.astype(x.dtype)


def conformer_conv(zc, conv_w, conv_b, ln_g, ln_b, p_conv):
    u = zc[..., :CONV_WIDTH] * jax.nn.sigmoid(zc[..., CONV_WIDTH:])
    u = lax.conv_general_dilated(
        u, conv_w[:, None, :], window_strides=(1,),
        padding=[(CONV_KERNEL - 1, 0)],
        dimension_numbers=('NWC', 'WIO', 'NWC'),
        feature_group_count=CONV_WIDTH) + conv_b
    u = jax.nn.silu(layernorm(u, ln_g, ln_b))
    return u @ p_conv


def rwkv7_time_mix(zr, mu, w0, w2, a0, a2, g2, k_k, k_a, r_k, gn_g, gn_b, p_rwkv):
    B, S, _ = zr.shape
    f32 = jnp.float32
    prev = jnp.pad(zr, ((0, 0), (1, 0), (0, 0)))[:, :S]
    zr = zr + (prev - zr) * mu
    RW = RWKV_WIDTH
    r, k, v, wd, ad, gd = jnp.split(
        zr, [RW, 2 * RW, 3 * RW, 3 * RW + LORA_W, 3 * RW + LORA_W + LORA_A], axis=-1)
    w = -jax.nn.softplus(-(w0 + jnp.tanh(wd) @ w2)) - 0.5
    decay = jnp.exp(-jnp.exp(w.astype(f32)))
    a = jax.nn.sigmoid(a0 + ad @ a2)
    g = jax.nn.sigmoid(gd) @ g2
    kk = k * k_k
    k = k * (1.0 + (a - 1.0) * k_a)

    def heads(t):
        return t.astype(f32).reshape(B, S, RWKV_HEADS, RWKV_HEAD)

    r, k, v, kk, a, decay = (heads(t) for t in (r, k, v, kk, a, decay))
    kk = kk * lax.rsqrt(jnp.maximum(jnp.sum(kk * kk, axis=-1, keepdims=True), 1e-24))

    def step(state, inp):
        r_t, w_t, k_t, v_t, kk_t, a_t = inp
        sa = jnp.einsum('bhvk,bhk->bhv', state, -kk_t)
        state = (state * w_t[:, :, None, :]
                 + sa[..., None] * (kk_t * a_t)[:, :, None, :]
                 + v_t[..., None] * k_t[:, :, None, :])
        y_t = jnp.einsum('bhvk,bhk->bhv', state, r_t)
        return state, y_t

    xs = tuple(jnp.moveaxis(t, 1, 0) for t in (r, decay, k, v, kk, a))
    state0 = jnp.zeros((B, RWKV_HEADS, RWKV_HEAD, RWKV_HEAD), f32)
    _, y = lax.scan(step, state0, xs)
    y = jnp.moveaxis(y, 0, 1)
    mu_y = jnp.mean(y, axis=-1, keepdims=True)
    var_y = jnp.mean(jnp.square(y - mu_y), axis=-1, keepdims=True)
    y = ((y - mu_y) * lax.rsqrt(var_y + GN_EPS)).reshape(B, S, RW)
    y = y * gn_g.astype(f32) + gn_b.astype(f32)
    y = y + (jnp.sum(r * k * r_k.astype(f32), axis=-1, keepdims=True) * v).reshape(B, S, RW)
    return (y.astype(zr.dtype) * g) @ p_rwkv


def mixing_sublayer(h, w_in, mu_shift, conv_w, conv_b, conv_ln_g, conv_ln_b, p_conv,
                    w0, w2, a0, a2, g2, k_k, k_a, r_k, gn_g, gn_b, p_rwkv, w_o):
    z = h @ w_in
    zc, zr, zg = jnp.split(z, [COL_CONV, COL_CONV + COL_RWKV], axis=-1)
    y_conv = conformer_conv(zc, conv_w, conv_b, conv_ln_g, conv_ln_b, p_conv)
    y_rwkv = rwkv7_time_mix(zr, mu_shift, w0, w2, a0, a2, g2, k_k, k_a, r_k, gn_g, gn_b, p_rwkv)
    g_conv, g_rwkv = jnp.split(jax.nn.sigmoid(zg), 2, axis=-1)
    return (g_conv * y_conv + g_rwkv * y_rwkv) @ w_o


def moe_ffn(h, w_router, b_router, we_gate, we_up, we_down, ws_gate, ws_up, ws_down):
    B, S, D = h.shape
    t = h.reshape(-1, D)
    n = t.shape[0]
    f32 = jnp.float32
    per_grp = N_EXPERTS // N_GROUPS
    scores = jax.nn.sigmoid((t @ w_router).astype(f32))
    biased = scores + b_router.astype(f32)
    grp_score = lax.top_k(biased.reshape(n, N_GROUPS, per_grp), 2)[0].sum(-1)
    _, top_grp = lax.top_k(grp_score, TOPK_GROUPS)
    grp_mask = jnp.any(top_grp[:, :, None] == jnp.arange(N_GROUPS), axis=1)
    masked = jnp.where(jnp.repeat(grp_mask, per_grp, axis=1), biased, -jnp.inf)
    _, idx = lax.top_k(masked, TOP_K)
    sel = jnp.take_along_axis(scores, idx, axis=1)
    gate = sel / jnp.sum(sel, axis=-1, keepdims=True) * ROUTED_SCALE

    n_assign = n * TOP_K
    flat_e = idx.reshape(-1)
    order = jnp.argsort(flat_e)
    e_sorted = flat_e[order]
    tok = order // TOP_K
    counts = jnp.bincount(flat_e, length=N_EXPERTS)
    padded = (counts + EXPERT_BLOCK - 1) // EXPERT_BLOCK * EXPERT_BLOCK
    pad_end = jnp.cumsum(padded)
    pad_start = pad_end - padded
    start = jnp.cumsum(counts) - counts
    dest = pad_start[e_sorted] + jnp.arange(n_assign) - start[e_sorted]
    n_blocks = -(-n_assign // EXPERT_BLOCK) + N_EXPERTS
    buf = jnp.zeros((n_blocks * EXPERT_BLOCK, D), t.dtype).at[dest].set(t[tok])
    blk_expert = jnp.minimum(
        jnp.searchsorted(pad_end, jnp.arange(n_blocks) * EXPERT_BLOCK, side='right'),
        N_EXPERTS - 1)

    def expert_block(args):
        xb, e = args
        return (jax.nn.silu(xb @ we_gate[e]) * (xb @ we_up[e])) @ we_down[e]

    yb = lax.map(expert_block, (buf.reshape(n_blocks, EXPERT_BLOCK, D), blk_expert))
    y = yb.reshape(-1, D)[dest] * gate.reshape(-1)[order][:, None].astype(t.dtype)
    routed = jax.ops.segment_sum(y, tok, num_segments=n)
    shared = (jax.nn.silu(t @ ws_gate) * (t @ ws_up)) @ ws_down
    return (routed + shared).reshape(B, S, D)


def setup_inputs(seed: int = 0) -> dict:
    key = jax.random.key(seed)
    keys = jax.random.split(key, 40)
    counter = [0]

    def nk():
        k = keys[counter[0]]
        counter[0] += 1
        return k

    def nrm(shape, scale):
        return jax.random.normal(nk(), shape, jnp.float32) * scale

    def gain(shape):
        return 1.0 + nrm(shape, 0.05)

    L, D, CW, RW = DEPTH, D_MODEL, CONV_WIDTH, RWKV_WIDTH
    return {
        'x': nrm((BATCH, SEQ, D), 1.0),
        'c': nrm((BATCH, D), 1.0),
        'w_ada': nrm((L, D, 6 * D), 0.5 * D ** -0.5),
        'b_ada': nrm((L, 6 * D), 0.02),
        'norm_mix_pre': gain((L, D)),
        'norm_mix_post': gain((L, D)),
        'norm_ffn_pre': gain((L, D)),
        'norm_ffn_post': gain((L, D)),
        'w_in': nrm((L, D, D_IN), D ** -0.5),
        'mu_shift': jax.random.uniform(nk(), (L, COL_RWKV), jnp.float32),
        'conv_w': nrm((L, CONV_KERNEL, CW), CONV_KERNEL ** -0.5),
        'conv_b': nrm((L, CW), 0.02),
        'conv_ln_g': gain((L, CW)),
        'conv_ln_b': nrm((L, CW), 0.02),
        'p_conv': nrm((L, CW, D), CW ** -0.5),
        'w0': nrm((L, RW), 1.0) - 1.5,
        'w2': nrm((L, LORA_W, RW), LORA_W ** -0.5),
        'a0': nrm((L, RW), 0.5),
        'a2': nrm((L, LORA_A, RW), 0.5 * LORA_A ** -0.5),
        'g2': nrm((L, LORA_G, RW), LORA_G ** -0.5),
        'k_k': 0.85 + nrm((L, RW), 0.05),
        'k_a': gain((L, RW)),
        'r_k': nrm((L, RWKV_HEADS, RWKV_HEAD), 0.1),
        'gn_g': gain((L, RW)),
        'gn_b': nrm((L, RW), 0.02),
        'p_rwkv': nrm((L, RW, D), RW ** -0.5),
        'w_o': nrm((L, D, D), D ** -0.5),
        'w_router': nrm((L, D, N_EXPERTS), D ** -0.5),
        'b_router': nrm((L, N_EXPERTS), 0.01),
        'we_gate': nrm((L, N_EXPERTS, D, D_EXPERT), D ** -0.5),
        'we_up': nrm((L, N_EXPERTS, D, D_EXPERT), D ** -0.5),
        'we_down': nrm((L, N_EXPERTS, D_EXPERT, D), D_EXPERT ** -0.5),
        'ws_gate': nrm((L, D, D_SHARED), D ** -0.5),
        'ws_up': nrm((L, D, D_SHARED), D ** -0.5),
        'ws_down': nrm((L, D_SHARED, D), D_SHARED ** -0.5),
    }


def reference(x, c, w_ada, b_ada, norm_mix_pre, norm_mix_post, norm_ffn_pre, norm_ffn_post,
              w_in, mu_shift, conv_w, conv_b, conv_ln_g, conv_ln_b, p_conv,
              w0, w2, a0, a2, g2, k_k, k_a, r_k, gn_g, gn_b, p_rwkv, w_o,
              w_router, b_router, we_gate, we_up, we_down, ws_gate, ws_up, ws_down):
    for l in range(DEPTH):
        mod = jax.nn.silu(c) @ w_ada[l] + b_ada[l]
        sh1, sc1, gt1, sh2, sc2, gt2 = (m[:, None, :] for m in jnp.split(mod, 6, axis=-1))
        h = rmsnorm(x, norm_mix_pre[l]) * (1.0 + sc1) + sh1
        y = mixing_sublayer(h, w_in[l], mu_shift[l], conv_w[l], conv_b[l], conv_ln_g[l],
                            conv_ln_b[l], p_conv[l], w0[l], w2[l], a0[l], a2[l], g2[l],
                            k_k[l], k_a[l], r_k[l], gn_g[l], gn_b[l], p_rwkv[l], w_o[l])
        x = x + gt1 * rmsnorm(y, norm_mix_post[l])
        h = rmsnorm(x, norm_ffn_pre[l]) * (1.0 + sc2) + sh2
        y = moe_ffn(h, w_router[l], b_router[l], we_gate[l], we_up[l], we_down[l],
                    ws_gate[l], ws_up[l], ws_down[l])
        x = x + gt2 * rmsnorm(y, norm_ffn_post[l])
    return x
```

```python
import functools

import jax
import jax.numpy as jnp
from jax import lax
from jax.experimental import pallas as pl
from jax.experimental.pallas import tpu as pltpu

F32 = jnp.float32
BF16 = jnp.bfloat16
HI = lax.Precision.HIGHEST

D_MODEL = 1024
CONV_WIDTH = 512
CONV_KERNEL = 31
RWKV_WIDTH = 512
HEAD = 64
N_HEADS = RWKV_WIDTH // HEAD
LORA_W = 64
LORA_A = 64
LORA_G = 128
N_EXPERTS = 64
TOP_K = 8
N_GROUPS = 8
TOPK_GROUPS = 4
GROUP_SIZE = N_EXPERTS // N_GROUPS
D_EXPERT = 256
ROUTED_SCALE = 2.5
RMS_EPS = 1e-6
LN_EPS = 1e-5
GN_EPS = 64e-5
COL_CONV = 2 * CONV_WIDTH
COL_RWKV = 3 * RWKV_WIDTH + LORA_W + LORA_A + LORA_G
COL_GATE = 2 * D_MODEL
D_IN = COL_CONV + COL_RWKV + COL_GATE

CHUNK = 64
QUAD = 4 * HEAD
CONV_HALO = 32
SHIFT_HALO = 8
GATE_LANES = 128
EXPERTS_PER_STEP = 5
VMEM_LIMIT = 52 * 1024 * 1024


def _params(*sem):
    return pltpu.CompilerParams(dimension_semantics=sem, vmem_limit_bytes=VMEM_LIMIT)


def _sigmoid(x):
    return 1.0 / (1.0 + jnp.exp(-x))


def _softplus(x):
    return jnp.maximum(x, 0.0) + jnp.log(1.0 + jnp.exp(-jnp.abs(x)))


def _rms(x, g):
    return x * lax.rsqrt(jnp.mean(x * x, axis=-1, keepdims=True) + RMS_EPS) * g


def _ada_kernel(c_ref, w_ref, b_ref, o_ref):
    c = c_ref[...]
    o_ref[...] = jnp.dot(c * _sigmoid(c), w_ref[...], precision=HI,
                         preferred_element_type=F32) + b_ref[...]


def _ada(c, w_ada, b_ada):
    nl, d, _ = w_ada.shape
    nb = c.shape[0]
    out = pl.pallas_call(
        _ada_kernel,
        grid=(nl, 6),
        in_specs=[
            pl.BlockSpec((nb, d), lambda l, k: (0, 0)),
            pl.BlockSpec((None, d, d), lambda l, k: (l, 0, k)),
            pl.BlockSpec((None, None, 1, d), lambda l, k: (l, k, 0, 0)),
        ],
        out_specs=pl.BlockSpec((None, None, nb, d), lambda l, k: (l, k, 0, 0)),
        out_shape=jax.ShapeDtypeStruct((nl, 6, nb, d), F32),
        compiler_params=_params("parallel", "parallel"),
        name="ada",
    )(c, w_ada, b_ada.reshape(nl, 6, 1, d))
    return out.reshape(nl, 6, nb, 1, d)


def _mod_spec(layer, piece, rows_per_batch, tm):
    return pl.BlockSpec((None, None, None, 1, D_MODEL),
                        lambda i, *_: (layer, piece, (i * tm) // rows_per_batch, 0, 0))


def _win_kernel(x_ref, g_ref, sc_ref, sh_ref, w_ref, zc_ref, zr_ref, zg_ref):
    h = _rms(x_ref[...], g_ref[...]) * (1.0 + sc_ref[...]) + sh_ref[...]
    hb = h.astype(BF16)
    zc_ref[...] = jnp.dot(hb, w_ref[:, :COL_CONV], preferred_element_type=F32)
    zr_ref[...] = jnp.dot(hb, w_ref[:, COL_CONV:COL_CONV + COL_RWKV], preferred_element_type=F32)
    zg_ref[...] = jnp.dot(hb, w_ref[:, COL_CONV + COL_RWKV:], preferred_element_type=F32)


def _win(x2, g, mod, layer, w_in_b, seq, tm=256):
    n, d = x2.shape
    row = lambda i: (i, 0)
    return pl.pallas_call(
        _win_kernel,
        grid=(n // tm,),
        in_specs=[
            pl.BlockSpec((tm, d), row),
            pl.BlockSpec((1, d), lambda i: (0, 0)),
            _mod_spec(layer, 1, seq, tm),
            _mod_spec(layer, 0, seq, tm),
            pl.BlockSpec((d, D_IN), lambda i: (0, 0)),
        ],
        out_specs=[pl.BlockSpec((tm, COL_CONV), row), pl.BlockSpec((tm, COL_RWKV), row),
                   pl.BlockSpec((tm, COL_GATE), row)],
        out_shape=[jax.ShapeDtypeStruct((n, COL_CONV), F32), jax.ShapeDtypeStruct((n, COL_RWKV), F32),
                   jax.ShapeDtypeStruct((n, COL_GATE), F32)],
        compiler_params=_params("parallel"),
        name="win",
    )(x2, g, mod, mod, w_in_b)


def _conv_kernel(z_ref, halo_ref, w_ref, cb_ref, g_ref, b_ref, o_ref, ubuf, *, ts):
    j = pl.program_id(1)
    z = z_ref[...]
    zh = halo_ref[...]
    uh = zh[:, :CONV_WIDTH] * _sigmoid(zh[:, CONV_WIDTH:])
    ubuf[0:CONV_HALO, :] = jnp.where(j > 0, uh, 0.0)
    ubuf[CONV_HALO:CONV_HALO + ts, :] = z[:, :CONV_WIDTH] * _sigmoid(z[:, CONV_WIDTH:])
    sub = 64
    first = CONV_HALO - (CONV_KERNEL - 1)
    for r in range(ts // sub):
        acc = jnp.zeros((sub, CONV_WIDTH), F32) + cb_ref[...]
        for k in range(CONV_KERNEL):
            acc = acc + w_ref[k:k + 1, :] * ubuf[r * sub + first + k:r * sub + first + k + sub, :]
        mu = jnp.mean(acc, axis=-1, keepdims=True)
        dlt = acc - mu
        var = jnp.mean(dlt * dlt, axis=-1, keepdims=True)
        y = dlt * lax.rsqrt(var + LN_EPS) * g_ref[...] + b_ref[...]
        o_ref[r * sub:(r + 1) * sub, :] = (y * _sigmoid(y)).astype(o_ref.dtype)


def _conv(zc, conv_w, conv_b, ln_g, ln_b, nb, seq, ts=256):
    n = zc.shape[0]
    tps = seq // ts
    hpt = ts // CONV_HALO
    vec = pl.BlockSpec((1, CONV_WIDTH), lambda b, j: (0, 0))
    return pl.pallas_call(
        functools.partial(_conv_kernel, ts=ts),
        grid=(nb, tps),
        in_specs=[
            pl.BlockSpec((ts, COL_CONV), lambda b, j: (b * tps + j, 0)),
            pl.BlockSpec((CONV_HALO, COL_CONV), lambda b, j: (jnp.maximum((b * tps + j) * hpt - 1, 0), 0)),
            pl.BlockSpec((CONV_KERNEL, CONV_WIDTH), lambda b, j: (0, 0)),
            vec, vec, vec,
        ],
        out_specs=pl.BlockSpec((ts, CONV_WIDTH), lambda b, j: (b * tps + j, 0)),
        out_shape=jax.ShapeDtypeStruct((n, CONV_WIDTH), BF16),
        scratch_shapes=[pltpu.VMEM((CONV_HALO + ts, CONV_WIDTH), F32)],
        compiler_params=_params("parallel", "parallel"),
        name="conv",
    )(zc, zc, conv_w, conv_b, ln_g, ln_b)


def _prep_kernel(z_ref, halo_ref, mu_ref, w0_ref, a0_ref, wa_ref, g2_ref, kk_ref, ka_ref, rk_ref,
                 ones_ref, tri_ref,
                 at_ref, bt_ref, kt_ref, rt_ref, v_ref, bts_ref, kts_ref, pc_ref, bonus_ref, g_ref):
    j = pl.program_id(1)
    z = z_ref[...]
    last = jnp.where(j > 0, halo_ref[SHIFT_HALO - 1:SHIFT_HALO, :], 0.0)
    row = lax.broadcasted_iota(jnp.int32, z.shape, 0)
    prev = jnp.where(row == 0, last, pltpu.roll(z, 1, axis=0))
    zs = z + (prev - z) * mu_ref[...]
    rw = RWKV_WIDTH
    r = zs[:, :rw]
    k = zs[:, rw:2 * rw]
    v = zs[:, 2 * rw:3 * rw]
    wad = zs[:, 3 * rw:3 * rw + LORA_W + LORA_A]
    gd = zs[:, 3 * rw + LORA_W + LORA_A:]
    lane = lax.broadcasted_iota(jnp.int32, wad.shape, 1)
    lora = jnp.dot(jnp.where(lane < LORA_W, jnp.tanh(wad), wad), wa_ref[...], precision=HI,
                   preferred_element_type=F32)
    w = -_softplus(-(w0_ref[...] + lora[:, :rw])) - 0.5
    lw = -jnp.exp(w)
    a = _sigmoid(a0_ref[...] + lora[:, rw:])
    g_ref[...] = jnp.dot(_sigmoid(gd), g2_ref[...], precision=HI, preferred_element_type=F32)
    kk = k * kk_ref[...]
    k2 = k * (1.0 + (a - 1.0) * ka_ref[...])
    ones = ones_ref[...]
    ss = jnp.dot(kk * kk, ones, precision=HI, preferred_element_type=F32)
    kk = kk * lax.rsqrt(jnp.maximum(ss, 1e-24))
    bonus_ref[...] = jnp.dot(r * k2 * rk_ref[...], ones, precision=HI, preferred_element_type=F32) * v
    ts = z.shape[0]
    cs = jnp.dot(tri_ref[...], lw, precision=HI, preferred_element_type=F32)
    lc = cs[:ts]
    tot = cs[ts:]
    e_neg = jnp.exp(-lc)
    e_end = jnp.exp(tot - lc)
    at_ref[...] = -kk * jnp.exp(lc - lw)
    bt_ref[...] = kk * a * e_neg
    kt_ref[...] = k2 * e_neg
    rt_ref[...] = r * jnp.exp(lc)
    v_ref[...] = v
    bts_ref[...] = kk * a * e_end
    kts_ref[...] = k2 * e_end
    pc_ref[...] = jnp.exp(tot)


def _prep(zr, mu, w0, a0, wa, g2, k_k, k_a, r_k, nb, seq, ts=256):
    n = zr.shape[0]
    tps = seq // ts
    hpt = ts // SHIFT_HALO
    rw = RWKV_WIDTH
    ones = jnp.kron(jnp.eye(N_HEADS, dtype=F32), jnp.ones((HEAD, HEAD), F32))
    nchunk = ts // CHUNK
    blk = jnp.kron(jnp.eye(nchunk, dtype=F32), jnp.ones((CHUNK, CHUNK), F32))
    tri = jnp.concatenate([jnp.tril(blk), blk], axis=0)
    vec = pl.BlockSpec((1, rw), lambda b, j: (0, 0))
    full = lambda shape: pl.BlockSpec(shape, lambda b, j: (0, 0))
    rows = pl.BlockSpec((ts, rw), lambda b, j: (b * tps + j, 0))
    return pl.pallas_call(
        _prep_kernel,
        grid=(nb, tps),
        in_specs=[
            pl.BlockSpec((ts, COL_RWKV), lambda b, j: (b * tps + j, 0)),
            pl.BlockSpec((SHIFT_HALO, COL_RWKV), lambda b, j: (jnp.maximum((b * tps + j) * hpt - 1, 0), 0)),
            full((1, COL_RWKV)), vec, vec, full((LORA_W + LORA_A, 2 * rw)), full((LORA_G, rw)),
            vec, vec, vec, full((rw, rw)), full((2 * ts, ts)),
        ],
        out_specs=[rows] * 10,
        out_shape=[jax.ShapeDtypeStruct((n, rw), F32)] * 10,
        compiler_params=_params("parallel", "parallel"),
        name="prep",
    )(zr, zr, mu, w0, a0, wa, g2, k_k, k_a, r_k, ones, tri)


def _mm(a, b):
    return jnp.dot(a.astype(BF16), b.astype(BF16), preferred_element_type=F32)


def _mm_nt(a, b):
    return lax.dot_general(a.astype(BF16), b.astype(BF16), (((1,), (1,)), ((), ())),
                           preferred_element_type=F32)


def _mm_tn(a, b):
    return lax.dot_general(a.astype(BF16), b.astype(BF16), (((0,), (0,)), ((), ())),
                           preferred_element_type=F32)


def _scan_kernel(at_ref, bt_ref, kt_ref, rt_ref, v_ref, bts_ref, kts_ref, pc_ref, bonus_ref, g_ref,
                 gng_ref, gnb_ref, ones_ref, o_ref, s_ref, y_ref, *, ts):
    @pl.when(pl.program_id(2) == 0)
    def _():
        s_ref[...] = jnp.zeros_like(s_ref)

    row_q = lax.broadcasted_iota(jnp.int32, (QUAD, QUAD), 0)
    col_q = lax.broadcasted_iota(jnp.int32, (QUAD, QUAD), 1)
    bdmask = (row_q // HEAD) == (col_q // HEAD)
    row_c = lax.broadcasted_iota(jnp.int32, (CHUNK, QUAD), 0)
    col_c = lax.broadcasted_iota(jnp.int32, (CHUNK, QUAD), 1) % CHUNK
    strict = col_c < row_c
    incl = col_c <= row_c
    eye = jnp.where(col_c == row_c, 1.0, 0.0).astype(F32)

    def bd(x):
        xb = x.astype(BF16)
        return jnp.where(bdmask, jnp.concatenate([xb] * (QUAD // CHUNK), axis=0), jnp.zeros((), BF16))

    def chunk(c, carry):
        sl = pl.ds(pl.multiple_of(c * CHUNK, CHUNK), CHUNK)
        at, bt, kt, rt, v = at_ref[sl, :], bt_ref[sl, :], kt_ref[sl, :], rt_ref[sl, :], v_ref[sl, :]
        bts, kts = bts_ref[sl, :], kts_ref[sl, :]
        pc = pc_ref[sl, :][0:1, :]
        ar = jnp.concatenate([at, rt], axis=0)
        pb = _mm_nt(ar, bd(bt))
        pk = _mm_nt(ar, bd(kt))
        l_ab = jnp.where(strict, pb[:CHUNK], 0.0)
        a_rb = jnp.where(incl, pb[CHUNK:], 0.0)
        l_ak = jnp.where(strict, pk[:CHUNK], 0.0)
        a_rk = jnp.where(incl, pk[CHUNK:], 0.0)
        tm = eye + l_ab
        lp = _mm(l_ab, bd(l_ab))
        for _ in range(4):
            both = _mm(jnp.concatenate([tm, lp], axis=0), bd(lp))
            tm = tm + both[:CHUNK]
            lp = both[CHUNK:]
        tm = tm + _mm(tm, bd(lp))
        bdv = bd(v)
        x0 = _mm(l_ak, bdv)
        tu = _mm(tm, jnp.concatenate([bd(x0), bd(at)], axis=1))
        u0, w = tu[:, :QUAD], tu[:, QUAD:]
        qy = _mm(a_rb, jnp.concatenate([bd(w), bd(u0)], axis=1))
        q = rt + qy[:, :QUAD]
        y0 = qy[:, QUAD:] + _mm(a_rk, bdv)
        s = s_ref[...]
        y_ref[sl, :] = y0 + _mm_nt(q, s)
        m_p = jnp.where(bdmask, _mm_tn(w, bts), 0.0)
        n_p = jnp.where(bdmask, _mm_tn(jnp.concatenate([u0, v], axis=0),
                                       jnp.concatenate([bts, kts], axis=0)), 0.0)
        s_ref[...] = s * pc + _mm(s, m_p) + n_p
        return carry

    lax.fori_loop(0, ts // CHUNK, chunk, 0)

    y = y_ref[...]
    ones = ones_ref[...]
    mean = jnp.dot(y, ones, precision=HI, preferred_element_type=F32) * (1.0 / HEAD)
    dlt = y - mean
    var = jnp.dot(dlt * dlt, ones, precision=HI, preferred_element_type=F32) * (1.0 / HEAD)
    yn = dlt * lax.rsqrt(var + GN_EPS) * gng_ref[...] + gnb_ref[...]
    o_ref[...] = ((yn + bonus_ref[...]) * g_ref[...]).astype(o_ref.dtype)


def _scan(prep_out, gn_g, gn_b, nb, seq, ts=512):
    n = prep_out[0].shape[0]
    tps = seq // ts
    nq = RWKV_WIDTH // QUAD
    ones = jnp.kron(jnp.eye(QUAD // HEAD, dtype=F32), jnp.ones((HEAD, HEAD), F32))
    rows = pl.BlockSpec((ts, QUAD), lambda b, q, t: (b * tps + t, q))
    vec = pl.BlockSpec((1, QUAD), lambda b, q, t: (0, q))
    return pl.pallas_call(
        functools.partial(_scan_kernel, ts=ts),
        grid=(nb, nq, tps),
        in_specs=[rows] * 10 + [vec, vec, pl.BlockSpec((QUAD, QUAD), lambda b, q, t: (0, 0))],
        out_specs=rows,
        out_shape=jax.ShapeDtypeStruct((n, RWKV_WIDTH), BF16),
        scratch_shapes=[pltpu.VMEM((QUAD, QUAD), F32), pltpu.VMEM((ts, QUAD), F32)],
        compiler_params=_params("parallel", "parallel", "arbitrary"),
        name="scan",
    )(*prep_out, gn_g, gn_b, ones)


def _merge_kernel(uc_ref, ur_ref, zg_ref, x_ref, pc_ref, pr_ref, wo_ref, gpost_ref, gt_ref,
                  gpre_ref, sc_ref, sh_ref, wrt_ref, xo_ref, h_ref, lg_ref):
    yc = jnp.dot(uc_ref[...], pc_ref[...], preferred_element_type=F32)
    yr = jnp.dot(ur_ref[...], pr_ref[...], preferred_element_type=F32)
    zg = zg_ref[...]
    m = _sigmoid(zg[:, :D_MODEL]) * yc + _sigmoid(zg[:, D_MODEL:]) * yr
    y = jnp.dot(m.astype(BF16), wo_ref[...], preferred_element_type=F32)
    xn = x_ref[...] + gt_ref[...] * _rms(y, gpost_ref[...])
    xo_ref[...] = xn
    h = _rms(xn, gpre_ref[...]) * (1.0 + sc_ref[...]) + sh_ref[...]
    h_ref[...] = h.astype(h_ref.dtype)
    lg_ref[...] = lax.dot_general(wrt_ref[...], h, (((1,), (1,)), ((), ())), precision=HI,
                                  preferred_element_type=F32)


def _merge(uc, ur, zg, x2, p_conv_b, p_rwkv_b, w_o_b, g_post, g_pre, mod, layer, w_router_t, seq, tm=256):
    n, d = x2.shape
    row = lambda i: (i, 0)
    full = lambda shape: pl.BlockSpec(shape, lambda i: (0, 0))
    return pl.pallas_call(
        _merge_kernel,
        grid=(n // tm,),
        in_specs=[
            pl.BlockSpec((tm, CONV_WIDTH), row), pl.BlockSpec((tm, RWKV_WIDTH), row),
            pl.BlockSpec((tm, COL_GATE), row), pl.BlockSpec((tm, d), row),
            full((CONV_WIDTH, d)), full((RWKV_WIDTH, d)), full((d, d)),
            full((1, d)), _mod_spec(layer, 2, seq, tm),
            full((1, d)), _mod_spec(layer, 4, seq, tm), _mod_spec(layer, 3, seq, tm),
            full((N_EXPERTS, d)),
        ],
        out_specs=[pl.BlockSpec((tm, d), row), pl.BlockSpec((tm, d), row),
                   pl.BlockSpec((N_EXPERTS, tm), lambda i: (0, i))],
        out_shape=[jax.ShapeDtypeStruct((n, d), F32), jax.ShapeDtypeStruct((n, d), BF16),
                   jax.ShapeDtypeStruct((N_EXPERTS, n), F32)],
        compiler_params=_params("parallel"),
        name="merge",
    )(uc, ur, zg, x2, p_conv_b, p_rwkv_b, w_o_b, g_post, mod, g_pre, mod, mod, w_router_t)


def _route_kernel(lg_ref, b_ref, o_ref):
    s = _sigmoid(lg_ref[...])
    biased = s + b_ref[...]
    t = s.shape[1]
    member = lax.broadcasted_iota(jnp.int32, (GROUP_SIZE, t), 0)
    grp = []
    for g in range(N_GROUPS):
        bg = biased[g * GROUP_SIZE:(g + 1) * GROUP_SIZE, :]
        m1 = jnp.max(bg, axis=0, keepdims=True)
        first = jnp.min(jnp.where(bg == m1, member, GROUP_SIZE), axis=0, keepdims=True)
        m2 = jnp.max(jnp.where(member == first, -jnp.inf, bg), axis=0, keepdims=True)
        grp.append(m1 + m2)
    masked = []
    for g in range(N_GROUPS):
        rank = jnp.zeros((1, t), jnp.int32)
        for o in range(N_GROUPS):
            if o == g:
                continue
            ahead = (grp[o] > grp[g]) if o > g else (grp[o] >= grp[g])
            rank = rank + jnp.where(ahead, 1, 0)
        keep = rank < TOPK_GROUPS
        masked.append(jnp.where(keep, biased[g * GROUP_SIZE:(g + 1) * GROUP_SIZE, :], -jnp.inf))
    masked = jnp.concatenate(masked, axis=0)
    eidx = lax.broadcasted_iota(jnp.int32, masked.shape, 0)
    rank = jnp.zeros(masked.shape, jnp.int32)
    for o in range(N_EXPERTS):
        ro = masked[o:o + 1, :]
        rank = rank + jnp.where(ro > masked, 1, jnp.where((ro == masked) & (eidx > o), 1, 0))
    sel = jnp.where(rank < TOP_K, s, 0.0)
    gate = sel / jnp.sum(sel, axis=0, keepdims=True) * ROUTED_SCALE
    pad = lax.broadcasted_iota(jnp.int32, (GATE_LANES - N_EXPERTS, t), 0)
    full = jnp.concatenate([gate, jnp.where(pad == 0, 1.0, 0.0).astype(F32)], axis=0)
    o_ref[...] = full.T


def _route(logits_t, b_router, tt=512):
    e, n = logits_t.shape
    return pl.pallas_call(
        _route_kernel,
        grid=(n // tt,),
        in_specs=[pl.BlockSpec((e, tt), lambda i: (0, i)), pl.BlockSpec((e, 1), lambda i: (0, 0))],
        out_specs=pl.BlockSpec((tt, GATE_LANES), lambda i: (i, 0)),
        out_shape=jax.ShapeDtypeStruct((n, GATE_LANES), F32),
        compiler_params=_params("parallel"),
        name="route",
    )(logits_t, b_router)


def _moe_kernel(h_ref, gate_ref, wg_ref, wu_ref, wd_ref, x_ref, gpost_ref, gt_ref, o_ref, acc_ref):
    e = pl.program_id(1)

    @pl.when(e == 0)
    def _():
        acc_ref[...] = jnp.zeros_like(acc_ref)

    h = h_ref[...]
    parts = []
    for i in range(EXPERTS_PER_STEP):
        gte = jnp.dot(h, wg_ref[i], preferred_element_type=F32)
        up = jnp.dot(h, wu_ref[i], preferred_element_type=F32)
        parts.append((gte * _sigmoid(gte) * up * gate_ref[:, i:i + 1]).astype(BF16))
    act = jnp.concatenate(parts, axis=1)
    wd = wd_ref[...].reshape(EXPERTS_PER_STEP * D_EXPERT, D_MODEL)
    acc_ref[...] += jnp.dot(act, wd, preferred_element_type=F32)

    @pl.when(e == pl.num_programs(1) - 1)
    def _():
        o_ref[...] = x_ref[...] + gt_ref[...] * _rms(acc_ref[...], gpost_ref[...])


def _moe(h2, gates3, wg, wu, wd, x2, g_post, mod, layer, seq, tm=512):
    n, d = x2.shape
    ne = wg.shape[0]
    eb = EXPERTS_PER_STEP
    return pl.pallas_call(
        _moe_kernel,
        grid=(n // tm, ne // eb),
        in_specs=[
            pl.BlockSpec((tm, d), lambda i, e: (i, 0)),
            pl.BlockSpec((None, tm, eb), lambda i, e: (e, i, 0)),
            pl.BlockSpec((eb, d, D_EXPERT), lambda i, e: (e, 0, 0)),
            pl.BlockSpec((eb, d, D_EXPERT), lambda i, e: (e, 0, 0)),
            pl.BlockSpec((eb, D_EXPERT, d), lambda i, e: (e, 0, 0)),
            pl.BlockSpec((tm, d), lambda i, e: (i, 0)),
            pl.BlockSpec((1, d), lambda i, e: (0, 0)),
            _mod_spec(layer, 5, seq, tm),
        ],
        out_specs=pl.BlockSpec((tm, d), lambda i, e: (i, 0)),
        out_shape=jax.ShapeDtypeStruct((n, d), F32),
        scratch_shapes=[pltpu.VMEM((tm, d), F32)],
        compiler_params=_params("parallel", "arbitrary"),
        name="moe",
    )(h2, gates3, wg, wu, wd, x2, g_post, mod)


def kernel(x, c, w_ada, b_ada, norm_mix_pre, norm_mix_post, norm_ffn_pre, norm_ffn_post, w_in, mu_shift, conv_w, conv_b, conv_ln_g, conv_ln_b, p_conv, w0, w2, a0, a2, g2, k_k, k_a, r_k, gn_g, gn_b, p_rwkv, w_o, w_router, b_router, we_gate, we_up, we_down, ws_gate, ws_up, ws_down):
    nb, seq, d = x.shape
    depth = w_ada.shape[0]
    n = nb * seq
    rw = RWKV_WIDTH
    mod = _ada(c, w_ada, b_ada)
    x2 = x.reshape(n, d)
    row = lambda a: a.reshape(1, -1)
    for l in range(depth):
        zc, zr, zg = _win(x2, row(norm_mix_pre[l]), mod, l, w_in[l].astype(BF16), seq)
        uc = _conv(zc, conv_w[l], row(conv_b[l]), row(conv_ln_g[l]), row(conv_ln_b[l]), nb, seq)
        wa = jnp.zeros((LORA_W + LORA_A, 2 * rw), F32)
        wa = wa.at[:LORA_W, :rw].set(w2[l]).at[LORA_W:, rw:].set(a2[l])
        prep_out = _prep(zr, row(mu_shift[l]), row(w0[l]), row(a0[l]), wa, g2[l], row(k_k[l]),
                         row(k_a[l]), row(r_k[l]), nb, seq)
        ur = _scan(prep_out, row(gn_g[l]), row(gn_b[l]), nb, seq)
        x2, h2, logits_t = _merge(uc, ur, zg, x2, p_conv[l].astype(BF16), p_rwkv[l].astype(BF16),
                                  w_o[l].astype(BF16), row(norm_mix_post[l]), row(norm_ffn_pre[l]),
                                  mod, l, w_router[l].T, seq)
        gates = _route(logits_t, b_router[l].reshape(-1, 1))
        ne = N_EXPERTS + 1
        gates3 = gates[:, :ne].reshape(n, ne // EXPERTS_PER_STEP, EXPERTS_PER_STEP).transpose(1, 0, 2)
        wg = jnp.concatenate([we_gate[l], ws_gate[l][None]], axis=0).astype(BF16)
        wu = jnp.concatenate([we_up[l], ws_up[l][None]], axis=0).astype(BF16)
        wd = jnp.concatenate([we_down[l], ws_down[l][None]], axis=0).astype(BF16)
        x2 = _moe(h2, gates3, wg, wu, wd, x2, row(norm_ffn_post[l]), mod, l, seq)
    return x2.reshape(nb, seq, d)
```

```python
import functools

import jax
import jax.numpy as jnp
from jax import lax
from jax.experimental import pallas as pl
from jax.experimental.pallas import tpu as pltpu

F32 = jnp.float32
BF16 = jnp.bfloat16
HI = lax.Precision.HIGHEST

D_MODEL = 1024
CONV_WIDTH = 512
CONV_KERNEL = 31
RWKV_WIDTH = 512
HEAD = 64
N_HEADS = RWKV_WIDTH // HEAD
LORA_W = 64
LORA_A = 64
LORA_G = 128
N_EXPERTS = 64
TOP_K = 8
N_GROUPS = 8
TOPK_GROUPS = 4
GROUP_SIZE = N_EXPERTS // N_GROUPS
D_EXPERT = 256
ROUTED_SCALE = 2.5
RMS_EPS = 1e-6
LN_EPS = 1e-5
GN_EPS = 64e-5
COL_CONV = 2 * CONV_WIDTH
COL_RWKV = 3 * RWKV_WIDTH + LORA_W + LORA_A + LORA_G
COL_GATE = 2 * D_MODEL
D_IN = COL_CONV + COL_RWKV + COL_GATE

CHUNK = 64
QUAD = 4 * HEAD
N_QUADS = RWKV_WIDTH // QUAD
CUM_BLOCK = 256
CONV_HALO = 32
SHIFT_HALO = 8
GATE_LANES = 128
EXPERTS_PER_STEP = 4
VMEM_LIMIT = 52 * 1024 * 1024


def _params(*sem):
    return pltpu.CompilerParams(dimension_semantics=sem, vmem_limit_bytes=VMEM_LIMIT)


def _sigmoid(x):
    return 1.0 / (1.0 + jnp.exp(-x))


def _softplus(x):
    return jnp.maximum(x, 0.0) + jnp.log(1.0 + jnp.exp(-jnp.abs(x)))


def _rms(x, g):
    return x * lax.rsqrt(jnp.mean(x * x, axis=-1, keepdims=True) + RMS_EPS) * g


def _split(x):
    hi = x.astype(BF16)
    return hi, (x - hi.astype(F32)).astype(BF16)


def _dot(a, b):
    return jnp.dot(a, b, preferred_element_type=F32)


def _dot_exact_rhs(x, w):
    hi, lo = _split(x)
    return _dot(hi, w) + _dot(lo, w)


def _dot_exact_lhs(w, x):
    hi, lo = _split(x)
    return _dot(w, hi) + _dot(w, lo)


def _ada_kernel(c_ref, w_ref, b_ref, o_ref):
    c = c_ref[...]
    o_ref[...] = jnp.dot(c * _sigmoid(c), w_ref[...], precision=HI,
                         preferred_element_type=F32) + b_ref[...]


def _ada(c, w_ada, b_ada):
    nl, d, _ = w_ada.shape
    nb = c.shape[0]
    out = pl.pallas_call(
        _ada_kernel,
        grid=(nl, 6),
        in_specs=[
            pl.BlockSpec((nb, d), lambda l, k: (0, 0)),
            pl.BlockSpec((None, d, d), lambda l, k: (l, 0, k)),
            pl.BlockSpec((None, None, 1, d), lambda l, k: (l, k, 0, 0)),
        ],
        out_specs=pl.BlockSpec((None, None, nb, d), lambda l, k: (l, k, 0, 0)),
        out_shape=jax.ShapeDtypeStruct((nl, 6, nb, d), F32),
        compiler_params=_params("parallel", "parallel"),
        name="ada",
    )(c, w_ada, b_ada.reshape(nl, 6, 1, d))
    return out.reshape(nl, 6, nb, 1, d)


def _mod_spec(layer, piece, rows_per_batch, tm):
    return pl.BlockSpec((None, None, None, 1, D_MODEL),
                        lambda i, *_: (layer, piece, (i * tm) // rows_per_batch, 0, 0))


def _win_kernel(x_ref, g_ref, sc_ref, sh_ref, w_ref, zc_ref, zr_ref, zg_ref):
    h = _rms(x_ref[...], g_ref[...]) * (1.0 + sc_ref[...]) + sh_ref[...]
    hb = h.astype(BF16)
    zc_ref[...] = _dot(hb, w_ref[:, :COL_CONV]).astype(zc_ref.dtype)
    zr_ref[...] = _dot(hb, w_ref[:, COL_CONV:COL_CONV + COL_RWKV])
    zg_ref[...] = _dot(hb, w_ref[:, COL_CONV + COL_RWKV:]).astype(zg_ref.dtype)


def _win(x2, g, mod, layer, w_in_b, seq, tm=512):
    n, d = x2.shape
    row = lambda i: (i, 0)
    return pl.pallas_call(
        _win_kernel,
        grid=(n // tm,),
        in_specs=[
            pl.BlockSpec((tm, d), row),
            pl.BlockSpec((1, d), lambda i: (0, 0)),
            _mod_spec(layer, 1, seq, tm),
            _mod_spec(layer, 0, seq, tm),
            pl.BlockSpec((d, D_IN), lambda i: (0, 0)),
        ],
        out_specs=[pl.BlockSpec((tm, COL_CONV), row), pl.BlockSpec((tm, COL_RWKV), row),
                   pl.BlockSpec((tm, COL_GATE), row)],
        out_shape=[jax.ShapeDtypeStruct((n, COL_CONV), BF16), jax.ShapeDtypeStruct((n, COL_RWKV), F32),
                   jax.ShapeDtypeStruct((n, COL_GATE), BF16)],
        compiler_params=_params("parallel"),
        name="win",
    )(x2, g, mod, mod, w_in_b)


def _conv_kernel(z_ref, halo_ref, w_ref, cb_ref, g_ref, b_ref, o_ref, ubuf, *, ts):
    j = pl.program_id(1)
    z = z_ref[...].astype(F32)
    zh = halo_ref[...].astype(F32)
    uh = zh[:, :CONV_WIDTH] * _sigmoid(zh[:, CONV_WIDTH:])
    ubuf[0:CONV_HALO, :] = jnp.where(j > 0, uh, 0.0)
    ubuf[CONV_HALO:CONV_HALO + ts, :] = z[:, :CONV_WIDTH] * _sigmoid(z[:, CONV_WIDTH:])
    sub = 64
    first = CONV_HALO - (CONV_KERNEL - 1)
    for r in range(ts // sub):
        acc = jnp.zeros((sub, CONV_WIDTH), F32) + cb_ref[...]
        for k in range(CONV_KERNEL):
            acc = acc + w_ref[k:k + 1, :] * ubuf[r * sub + first + k:r * sub + first + k + sub, :]
        mu = jnp.mean(acc, axis=-1, keepdims=True)
        dlt = acc - mu
        var = jnp.mean(dlt * dlt, axis=-1, keepdims=True)
        y = dlt * lax.rsqrt(var + LN_EPS) * g_ref[...] + b_ref[...]
        o_ref[r * sub:(r + 1) * sub, :] = (y * _sigmoid(y)).astype(o_ref.dtype)


def _conv(zc, conv_w, conv_b, ln_g, ln_b, nb, seq, ts=256):
    n = zc.shape[0]
    tps = seq // ts
    hpt = ts // CONV_HALO
    vec = pl.BlockSpec((1, CONV_WIDTH), lambda b, j: (0, 0))
    return pl.pallas_call(
        functools.partial(_conv_kernel, ts=ts),
        grid=(nb, tps),
        in_specs=[
            pl.BlockSpec((ts, COL_CONV), lambda b, j: (b * tps + j, 0)),
            pl.BlockSpec((CONV_HALO, COL_CONV), lambda b, j: (jnp.maximum((b * tps + j) * hpt - 1, 0), 0)),
            pl.BlockSpec((CONV_KERNEL, CONV_WIDTH), lambda b, j: (0, 0)),
            vec, vec, vec,
        ],
        out_specs=pl.BlockSpec((ts, CONV_WIDTH), lambda b, j: (b * tps + j, 0)),
        out_shape=jax.ShapeDtypeStruct((n, CONV_WIDTH), BF16),
        scratch_shapes=[pltpu.VMEM((CONV_HALO + ts, CONV_WIDTH), F32)],
        compiler_params=_params("parallel", "parallel"),
        name="conv",
    )(zc, zc, conv_w, conv_b, ln_g, ln_b)


def _prep_kernel(z_ref, halo_ref, mu_ref, w0_ref, a0_ref, wah_ref, wal_ref, g2_ref, kk_ref, ka_ref, rk_ref,
                 ones_ref, tri_ref, csel_ref,
                 at_ref, bt_ref, kt_ref, rt_ref, v_ref, bts_ref, kts_ref, bonus_ref, g_ref, pc_ref):
    j = pl.program_id(1)
    z = z_ref[...]
    ts = z.shape[0]
    last = jnp.where(j > 0, halo_ref[SHIFT_HALO - 1:SHIFT_HALO, :], 0.0)
    row = lax.broadcasted_iota(jnp.int32, z.shape, 0)
    prev = jnp.where(row == 0, last, pltpu.roll(z, 1, axis=0))
    zs = z + (prev - z) * mu_ref[...]
    rw = RWKV_WIDTH
    r = zs[:, :rw]
    k = zs[:, rw:2 * rw]
    v = zs[:, 2 * rw:3 * rw]
    wad = zs[:, 3 * rw:3 * rw + LORA_W + LORA_A]
    gd = zs[:, 3 * rw + LORA_W + LORA_A:]
    lane = lax.broadcasted_iota(jnp.int32, wad.shape, 1)
    xh, xl = _split(jnp.where(lane < LORA_W, jnp.tanh(wad), wad))
    wah = wah_ref[...]
    lora = _dot(xh, wah) + _dot(xl, wah) + _dot(xh, wal_ref[...])
    w = -_softplus(-(w0_ref[...] + lora[:, :rw])) - 0.5
    lw = -jnp.exp(w)
    a = _sigmoid(a0_ref[...] + lora[:, rw:])
    g_ref[...] = _dot(_sigmoid(gd).astype(BF16), g2_ref[...]).astype(g_ref.dtype)
    kk = k * kk_ref[...]
    k2 = k * (1.0 + (a - 1.0) * ka_ref[...])
    ones = ones_ref[...]
    kk = kk * lax.rsqrt(jnp.maximum(_dot_exact_rhs(kk * kk, ones), 1e-24))
    bonus_ref[...] = (_dot_exact_rhs(r * k2 * rk_ref[...], ones) * v).astype(bonus_ref.dtype)
    tri = tri_ref[...]
    lc, tot = [], []
    for h in range(ts // CUM_BLOCK):
        cs = _dot_exact_lhs(tri, lw[h * CUM_BLOCK:(h + 1) * CUM_BLOCK])
        lc.append(cs[:CUM_BLOCK])
        tot.append(cs[CUM_BLOCK:])
    lc = jnp.concatenate(lc, axis=0)
    tot = jnp.concatenate(tot, axis=0)
    e_neg = jnp.exp(-lc)
    e_end = jnp.exp(tot - lc)
    kka = kk * a
    dt = at_ref.dtype
    at_ref[...] = (-kk * jnp.exp(lc - lw)).astype(dt)
    bt_ref[...] = (kka * e_neg).astype(dt)
    kt_ref[...] = (k2 * e_neg).astype(dt)
    rt_ref[...] = (r * jnp.exp(lc)).astype(dt)
    v_ref[...] = v.astype(dt)
    bts_ref[...] = (kka * e_end).astype(dt)
    kts_ref[...] = (k2 * e_end).astype(dt)
    pc_ref[...] = jnp.exp(_dot_exact_lhs(csel_ref[...], lw))


def _prep(zr, mu, w0, a0, wa, g2, k_k, k_a, r_k, nb, seq, ts=512):
    n = zr.shape[0]
    tps = seq // ts
    hpt = ts // SHIFT_HALO
    rw = RWKV_WIDTH
    cpt = ts // CHUNK
    ones = jnp.kron(jnp.eye(N_HEADS, dtype=F32), jnp.ones((HEAD, HEAD), F32)).astype(BF16)
    blk = jnp.kron(jnp.eye(CUM_BLOCK // CHUNK, dtype=F32), jnp.ones((CHUNK, CHUNK), F32))
    tri = jnp.concatenate([jnp.tril(blk), blk], axis=0).astype(BF16)
    csel = jnp.kron(jnp.eye(cpt, dtype=F32), jnp.ones((1, CHUNK), F32)).astype(BF16)
    wa_hi, wa_lo = _split(wa)
    vec = pl.BlockSpec((1, rw), lambda b, j: (0, 0))
    full = lambda shape: pl.BlockSpec(shape, lambda b, j: (0, 0))
    rows = pl.BlockSpec((ts, rw), lambda b, j: (b * tps + j, 0))
    return pl.pallas_call(
        _prep_kernel,
        grid=(nb, tps),
        in_specs=[
            pl.BlockSpec((ts, COL_RWKV), lambda b, j: (b * tps + j, 0)),
            pl.BlockSpec((SHIFT_HALO, COL_RWKV), lambda b, j: (jnp.maximum((b * tps + j) * hpt - 1, 0), 0)),
            full((1, COL_RWKV)), vec, vec,
            full((LORA_W + LORA_A, 2 * rw)), full((LORA_W + LORA_A, 2 * rw)), full((LORA_G, rw)),
            vec, vec, vec, full((rw, rw)), full((2 * CUM_BLOCK, CUM_BLOCK)), full((cpt, ts)),
        ],
        out_specs=[rows] * 9 + [pl.BlockSpec((cpt, rw), lambda b, j: (b * tps + j, 0))],
        out_shape=[jax.ShapeDtypeStruct((n, rw), BF16)] * 9 + [jax.ShapeDtypeStruct((n // CHUNK, rw), F32)],
        compiler_params=_params("parallel", "parallel"),
        name="prep",
    )(zr, zr, mu, w0, a0, wa_hi, wa_lo, g2.astype(BF16), k_k, k_a, r_k, ones, tri, csel)


def _mm(a, b):
    return jnp.dot(a.astype(BF16), b.astype(BF16), preferred_element_type=F32)


def _mm_nt(a, b):
    return lax.dot_general(a.astype(BF16), b.astype(BF16), (((1,), (1,)), ((), ())),
                           preferred_element_type=F32)


def _mm_tn(a, b):
    return lax.dot_general(a.astype(BF16), b.astype(BF16), (((0,), (0,)), ((), ())),
                           preferred_element_type=F32)


def _scan_kernel(at_ref, bt_ref, kt_ref, rt_ref, v_ref, bts_ref, kts_ref, bonus_ref, g_ref, pc_ref,
                 gng_ref, gnb_ref, ones_ref, o_ref, s_ref, y_ref, *, ts, gb):
    @pl.when(pl.program_id(1) == 0)
    def _():
        s_ref[...] = jnp.zeros_like(s_ref)

    row_q = lax.broadcasted_iota(jnp.int32, (QUAD, QUAD), 0)
    col_q = lax.broadcasted_iota(jnp.int32, (QUAD, QUAD), 1)
    bdmask = (row_q // HEAD) == (col_q // HEAD)
    row_c = lax.broadcasted_iota(jnp.int32, (CHUNK, QUAD), 0)
    col_c = lax.broadcasted_iota(jnp.int32, (CHUNK, QUAD), 1) % CHUNK
    strict = col_c < row_c
    incl = col_c <= row_c
    eye = jnp.where(col_c == row_c, 1.0, 0.0).astype(F32)

    def bd(x):
        xb = x.astype(BF16)
        return jnp.where(bdmask, jnp.concatenate([xb] * (QUAD // CHUNK), axis=0), jnp.zeros((), BF16))

    chains = [(b, qd) for b in range(gb) for qd in range(N_QUADS)]

    def chunk(c, carry):
        sl = pl.ds(pl.multiple_of(c * CHUNK, CHUNK), CHUNK)
        lanes = [slice(qd * QUAD, (qd + 1) * QUAD) for _, qd in chains]
        ld = lambda ref: [ref[b, sl, ln] for (b, _), ln in zip(chains, lanes)]
        at, bt, kt, rt, v, bts, kts = (ld(r) for r in (at_ref, bt_ref, kt_ref, rt_ref, v_ref, bts_ref, kts_ref))
        each = range(len(chains))
        ar = [jnp.concatenate([at[i], rt[i]], axis=0) for i in each]
        pb = [_mm_nt(ar[i], bd(bt[i])) for i in each]
        pk = [_mm_nt(ar[i], bd(kt[i])) for i in each]
        l_ab = [jnp.where(strict, p[:CHUNK], 0.0) for p in pb]
        a_rb = [jnp.where(incl, p[CHUNK:], 0.0) for p in pb]
        l_ak = [jnp.where(strict, p[:CHUNK], 0.0) for p in pk]
        a_rk = [jnp.where(incl, p[CHUNK:], 0.0) for p in pk]
        tm = [eye + l for l in l_ab]
        lp = [_mm(l, bd(l)) for l in l_ab]
        for _ in range(4):
            both = [_mm(jnp.concatenate([tm[i], lp[i]], axis=0), bd(lp[i])) for i in each]
            tm = [tm[i] + both[i][:CHUNK] for i in each]
            lp = [both[i][CHUNK:] for i in each]
        tm = [tm[i] + _mm(tm[i], bd(lp[i])) for i in each]
        bdv = [bd(x) for x in v]
        x0 = [_mm(l_ak[i], bdv[i]) for i in each]
        tu = [_mm(tm[i], jnp.concatenate([bd(x0[i]), bd(at[i])], axis=1)) for i in each]
        u0 = [t[:, :QUAD] for t in tu]
        w = [t[:, QUAD:] for t in tu]
        qy = [_mm(a_rb[i], jnp.concatenate([bd(w[i]), bd(u0[i])], axis=1)) for i in each]
        y0 = [qy[i][:, QUAD:] + _mm(a_rk[i], bdv[i]) for i in each]
        q = [rt[i].astype(F32) + qy[i][:, :QUAD] for i in each]
        s = [s_ref[i] for i in each]
        for i, ((b, _), ln) in enumerate(zip(chains, lanes)):
            y_ref[b, sl, ln] = y0[i] + _mm_nt(q[i], s[i])
        m_p = [jnp.where(bdmask, _mm_tn(w[i], bts[i]), 0.0) for i in each]
        n_p = [jnp.where(bdmask, _mm_tn(jnp.concatenate([u0[i].astype(BF16), v[i]], axis=0),
                                        jnp.concatenate([bts[i], kts[i]], axis=0)), 0.0) for i in each]
        for i, ((b, _), ln) in enumerate(zip(chains, lanes)):
            s_ref[i] = s[i] * pc_ref[b, pl.ds(c, 1), ln] + _mm(s[i], m_p[i]) + n_p[i]
        return carry

    lax.fori_loop(0, ts // CHUNK, chunk, 0)

    ones = ones_ref[...]
    for b in range(gb):
        for qd in range(N_QUADS):
            ln = slice(qd * QUAD, (qd + 1) * QUAD)
            y = y_ref[b, :, ln]
            dlt = y - _dot_exact_rhs(y, ones) * (1.0 / HEAD)
            var = _dot_exact_rhs(dlt * dlt, ones) * (1.0 / HEAD)
            yn = dlt * lax.rsqrt(var + GN_EPS) * gng_ref[:, ln] + gnb_ref[:, ln]
            o_ref[b, :, ln] = ((yn + bonus_ref[b, :, ln].astype(F32)) * g_ref[b, :, ln].astype(F32)).astype(o_ref.dtype)


def _scan(prep_out, gn_g, gn_b, nb, seq, ts=512, gb=2):
    n = prep_out[0].shape[0]
    rw = RWKV_WIDTH
    cpt = ts // CHUNK
    ones = jnp.kron(jnp.eye(QUAD // HEAD, dtype=F32), jnp.ones((HEAD, HEAD), F32)).astype(BF16)
    rows = pl.BlockSpec((gb, ts, rw), lambda i, t: (i, t, 0))
    vec = pl.BlockSpec((1, rw), lambda i, t: (0, 0))
    args = [a.reshape(nb, seq, rw) for a in prep_out[:9]] + [prep_out[9].reshape(nb, seq // CHUNK, rw)]
    out = pl.pallas_call(
        functools.partial(_scan_kernel, ts=ts, gb=gb),
        grid=(nb // gb, seq // ts),
        in_specs=[rows] * 9 + [pl.BlockSpec((gb, cpt, rw), lambda i, t: (i, t, 0)), vec, vec,
                               pl.BlockSpec((QUAD, QUAD), lambda i, t: (0, 0))],
        out_specs=rows,
        out_shape=jax.ShapeDtypeStruct((nb, seq, rw), BF16),
        scratch_shapes=[pltpu.VMEM((gb * N_QUADS, QUAD, QUAD), F32), pltpu.VMEM((gb, ts, rw), F32)],
        compiler_params=_params("parallel", "arbitrary"),
        name="scan",
    )(*args, gn_g, gn_b, ones)
    return out.reshape(n, rw)


def _merge_kernel(uc_ref, ur_ref, zg_ref, x_ref, pc_ref, pr_ref, wo_ref, gpost_ref, gt_ref,
                  gpre_ref, sc_ref, sh_ref, wrt_ref, xo_ref, h_ref, lg_ref):
    yc = _dot(uc_ref[...], pc_ref[...])
    yr = _dot(ur_ref[...], pr_ref[...])
    zg = zg_ref[...].astype(F32)
    m = _sigmoid(zg[:, :D_MODEL]) * yc + _sigmoid(zg[:, D_MODEL:]) * yr
    y = _dot(m.astype(BF16), wo_ref[...])
    xn = x_ref[...] + gt_ref[...] * _rms(y, gpost_ref[...])
    xo_ref[...] = xn
    h = _rms(xn, gpre_ref[...]) * (1.0 + sc_ref[...]) + sh_ref[...]
    h_ref[...] = h.astype(h_ref.dtype)
    lg_ref[...] = lax.dot_general(wrt_ref[...], h, (((1,), (1,)), ((), ())), precision=HI,
                                  preferred_element_type=F32)


def _merge(uc, ur, zg, x2, p_conv_b, p_rwkv_b, w_o_b, g_post, g_pre, mod, layer, w_router_t, seq, tm=512):
    n, d = x2.shape
    row = lambda i: (i, 0)
    full = lambda shape: pl.BlockSpec(shape, lambda i: (0, 0))
    return pl.pallas_call(
        _merge_kernel,
        grid=(n // tm,),
        in_specs=[
            pl.BlockSpec((tm, CONV_WIDTH), row), pl.BlockSpec((tm, RWKV_WIDTH), row),
            pl.BlockSpec((tm, COL_GATE), row), pl.BlockSpec((tm, d), row),
            full((CONV_WIDTH, d)), full((RWKV_WIDTH, d)), full((d, d)),
            full((1, d)), _mod_spec(layer, 2, seq, tm),
            full((1, d)), _mod_spec(layer, 4, seq, tm), _mod_spec(layer, 3, seq, tm),
            full((N_EXPERTS, d)),
        ],
        out_specs=[pl.BlockSpec((tm, d), row), pl.BlockSpec((tm, d), row),
                   pl.BlockSpec((N_EXPERTS, tm), lambda i: (0, i))],
        out_shape=[jax.ShapeDtypeStruct((n, d), F32), jax.ShapeDtypeStruct((n, d), BF16),
                   jax.ShapeDtypeStruct((N_EXPERTS, n), F32)],
        compiler_params=_params("parallel"),
        name="merge",
    )(uc, ur, zg, x2, p_conv_b, p_rwkv_b, w_o_b, g_post, mod, g_pre, mod, mod, w_router_t)


def _route_kernel(lg_ref, b_ref, o_ref):
    s = _sigmoid(lg_ref[...])
    biased = s + b_ref[...]
    t = s.shape[1]
    member = lax.broadcasted_iota(jnp.int32, (GROUP_SIZE, t), 0)
    grp = []
    for g in range(N_GROUPS):
        bg = biased[g * GROUP_SIZE:(g + 1) * GROUP_SIZE, :]
        m1 = jnp.max(bg, axis=0, keepdims=True)
        first = jnp.min(jnp.where(bg == m1, member, GROUP_SIZE), axis=0, keepdims=True)
        m2 = jnp.max(jnp.where(member == first, -jnp.inf, bg), axis=0, keepdims=True)
        grp.append(m1 + m2)
    masked = []
    for g in range(N_GROUPS):
        rank = jnp.zeros((1, t), jnp.int32)
        for o in range(N_GROUPS):
            if o == g:
                continue
            ahead = (grp[o] > grp[g]) if o > g else (grp[o] >= grp[g])
            rank = rank + jnp.where(ahead, 1, 0)
        keep = rank < TOPK_GROUPS
        masked.append(jnp.where(keep, biased[g * GROUP_SIZE:(g + 1) * GROUP_SIZE, :], -jnp.inf))
    masked = jnp.concatenate(masked, axis=0)
    eidx = lax.broadcasted_iota(jnp.int32, masked.shape, 0)
    rank = jnp.zeros(masked.shape, jnp.int32)
    for o in range(N_EXPERTS):
        ro = masked[o:o + 1, :]
        rank = rank + jnp.where(ro > masked, 1, jnp.where((ro == masked) & (eidx > o), 1, 0))
    sel = jnp.where(rank < TOP_K, s, 0.0)
    gate = sel / jnp.sum(sel, axis=0, keepdims=True) * ROUTED_SCALE
    full = jnp.concatenate([gate, jnp.zeros((GATE_LANES - N_EXPERTS, t), F32)], axis=0)
    o_ref[...] = full.T


def _route(logits_t, b_router, tt=512):
    e, n = logits_t.shape
    return pl.pallas_call(
        _route_kernel,
        grid=(n // tt,),
        in_specs=[pl.BlockSpec((e, tt), lambda i: (0, i)), pl.BlockSpec((e, 1), lambda i: (0, 0))],
        out_specs=pl.BlockSpec((tt, GATE_LANES), lambda i: (i, 0)),
        out_shape=jax.ShapeDtypeStruct((n, GATE_LANES), F32),
        compiler_params=_params("parallel"),
        name="route",
    )(logits_t, b_router)


def _swiglu(h, wg, wu):
    gte = _dot(h, wg)
    return gte * _sigmoid(gte) * _dot(h, wu)


def _moe_kernel(h_ref, gate_ref, wg_ref, wu_ref, wd_ref, sg_ref, su_ref, sd_ref, x_ref, gpost_ref, gt_ref,
                o_ref, acc_ref):
    e = pl.program_id(1)
    h = h_ref[...]

    @pl.when(e == 0)
    def _():
        acc_ref[...] = _dot(_swiglu(h, sg_ref[...], su_ref[...]).astype(BF16), sd_ref[...])

    gates = gate_ref[...]
    lane = lax.broadcasted_iota(jnp.int32, gates.shape, 1)
    parts = []
    for i in range(EXPERTS_PER_STEP):
        gcol = jnp.sum(jnp.where(lane == e * EXPERTS_PER_STEP + i, gates, 0.0), axis=1, keepdims=True)
        parts.append((_swiglu(h, wg_ref[i], wu_ref[i]) * gcol).astype(BF16))
    act = jnp.concatenate(parts, axis=1)
    wd = wd_ref[...].reshape(EXPERTS_PER_STEP * D_EXPERT, D_MODEL)
    acc_ref[...] += _dot(act, wd)

    @pl.when(e == pl.num_programs(1) - 1)
    def _():
        o_ref[...] = x_ref[...] + gt_ref[...] * _rms(acc_ref[...], gpost_ref[...])


def _moe(h2, gates, wg, wu, wd, sg, su, sd, x2, g_post, mod, layer, seq, tm=1024):
    n, d = x2.shape
    ne = wg.shape[0]
    eb = EXPERTS_PER_STEP
    full = lambda shape: pl.BlockSpec(shape, lambda i, e: (0, 0))
    return pl.pallas_call(
        _moe_kernel,
        grid=(n // tm, ne // eb),
        in_specs=[
            pl.BlockSpec((tm, d), lambda i, e: (i, 0)),
            pl.BlockSpec((tm, GATE_LANES), lambda i, e: (i, 0)),
            pl.BlockSpec((eb, d, D_EXPERT), lambda i, e: (e, 0, 0)),
            pl.BlockSpec((eb, d, D_EXPERT), lambda i, e: (e, 0, 0)),
            pl.BlockSpec((eb, D_EXPERT, d), lambda i, e: (e, 0, 0)),
            full((d, D_EXPERT)), full((d, D_EXPERT)), full((D_EXPERT, d)),
            pl.BlockSpec((tm, d), lambda i, e: (i, 0)),
            full((1, d)),
            _mod_spec(layer, 5, seq, tm),
        ],
        out_specs=pl.BlockSpec((tm, d), lambda i, e: (i, 0)),
        out_shape=jax.ShapeDtypeStruct((n, d), F32),
        scratch_shapes=[pltpu.VMEM((tm, d), F32)],
        compiler_params=_params("parallel", "arbitrary"),
        name="moe",
    )(h2, gates, wg, wu, wd, sg, su, sd, x2, g_post, mod)


def kernel(x, c, w_ada, b_ada, norm_mix_pre, norm_mix_post, norm_ffn_pre, norm_ffn_post, w_in, mu_shift, conv_w, conv_b, conv_ln_g, conv_ln_b, p_conv, w0, w2, a0, a2, g2, k_k, k_a, r_k, gn_g, gn_b, p_rwkv, w_o, w_router, b_router, we_gate, we_up, we_down, ws_gate, ws_up, ws_down):
    nb, seq, d = x.shape
    depth = w_ada.shape[0]
    n = nb * seq
    rw = RWKV_WIDTH
    mod = _ada(c, w_ada, b_ada)
    x2 = x.reshape(n, d)
    row = lambda a: a.reshape(1, -1)
    bf = lambda a: a.astype(BF16)
    for l in range(depth):
        zc, zr, zg = _win(x2, row(norm_mix_pre[l]), mod, l, bf(w_in[l]), seq)
        uc = _conv(zc, conv_w[l], row(conv_b[l]), row(conv_ln_g[l]), row(conv_ln_b[l]), nb, seq)
        wa = jnp.zeros((LORA_W + LORA_A, 2 * rw), F32)
        wa = wa.at[:LORA_W, :rw].set(w2[l]).at[LORA_W:, rw:].set(a2[l])
        prep_out = _prep(zr, row(mu_shift[l]), row(w0[l]), row(a0[l]), wa, g2[l], row(k_k[l]),
                         row(k_a[l]), row(r_k[l]), nb, seq)
        ur = _scan(prep_out, row(gn_g[l]), row(gn_b[l]), nb, seq)
        x2, h2, logits_t = _merge(uc, ur, zg, x2, bf(p_conv[l]), bf(p_rwkv[l]), bf(w_o[l]),
                                  row(norm_mix_post[l]), row(norm_ffn_pre[l]), mod, l, w_router[l].T, seq)
        gates = _route(logits_t, b_router[l].reshape(-1, 1))
        x2 = _moe(h2, gates, bf(we_gate[l]), bf(we_up[l]), bf(we_down[l]), bf(ws_gate[l]), bf(ws_up[l]),
                  bf(ws_down[l]), x2, row(norm_ffn_post[l]), mod, l, seq)
    return x2.reshape(nb, seq, d)
```

```python
import functools

import jax
import jax.numpy as jnp
from jax import lax
from jax.experimental import pallas as pl
from jax.experimental.pallas import tpu as pltpu

F32 = jnp.float32
BF16 = jnp.bfloat16
HI = lax.Precision.HIGHEST

D_MODEL = 1024
CONV_WIDTH = 512
CONV_KERNEL = 31
RWKV_WIDTH = 512
HEAD = 64
N_HEADS = RWKV_WIDTH // HEAD
LORA_W = 64
LORA_A = 64
LORA_G = 128
N_EXPERTS = 64
TOP_K = 8
N_GROUPS = 8
TOPK_GROUPS = 4
GROUP_SIZE = N_EXPERTS // N_GROUPS
D_EXPERT = 256
ROUTED_SCALE = 2.5
RMS_EPS = 1e-6
LN_EPS = 1e-5
GN_EPS = 64e-5
COL_CONV = 2 * CONV_WIDTH
COL_RWKV = 3 * RWKV_WIDTH + LORA_W + LORA_A + LORA_G
COL_GATE = 2 * D_MODEL
D_IN = COL_CONV + COL_RWKV + COL_GATE

CHUNK = 64
QUAD = 4 * HEAD
N_QUADS = RWKV_WIDTH // QUAD
CUM_BLOCK = 256
CONV_HALO = 32
SUBLANES = 8
SHIFT_HALO = SUBLANES
GATE_LANES = 128
EXPERTS_PER_STEP = 4
VMEM_LIMIT = 52 * 1024 * 1024


def _params(*sem):
    return pltpu.CompilerParams(dimension_semantics=sem, vmem_limit_bytes=VMEM_LIMIT)


def _sigmoid(x):
    return 1.0 / (1.0 + jnp.exp(-x))


def _softplus(x):
    return jnp.maximum(x, 0.0) + jnp.log(1.0 + jnp.exp(-jnp.abs(x)))


def _rms(x, g):
    return x * lax.rsqrt(jnp.mean(x * x, axis=-1, keepdims=True) + RMS_EPS) * g


def _split(x):
    hi = x.astype(BF16)
    return hi, (x - hi.astype(F32)).astype(BF16)


def _dot(a, b):
    return jnp.dot(a, b, preferred_element_type=F32)


def _dot_exact_rhs(x, w):
    hi, lo = _split(x)
    return _dot(hi, w) + _dot(lo, w)


def _dot_exact_lhs(w, x):
    hi, lo = _split(x)
    return _dot(w, hi) + _dot(w, lo)


def _ada_kernel(c_ref, w_ref, b_ref, o_ref):
    c = c_ref[...]
    o_ref[...] = jnp.dot(c * _sigmoid(c), w_ref[...], precision=HI,
                         preferred_element_type=F32) + b_ref[...]


def _ada(c, w_ada, b_ada):
    nl, d, _ = w_ada.shape
    nb = c.shape[0]
    out = pl.pallas_call(
        _ada_kernel,
        grid=(nl, 6),
        in_specs=[
            pl.BlockSpec((nb, d), lambda l, k: (0, 0)),
            pl.BlockSpec((None, d, d), lambda l, k: (l, 0, k)),
            pl.BlockSpec((None, None, 1, d), lambda l, k: (l, k, 0, 0)),
        ],
        out_specs=pl.BlockSpec((None, None, nb, d), lambda l, k: (l, k, 0, 0)),
        out_shape=jax.ShapeDtypeStruct((nl, 6, nb, d), F32),
        compiler_params=_params("parallel", "parallel"),
        name="ada",
    )(c, w_ada, b_ada.reshape(nl, 6, 1, d))
    return out.reshape(nl, 6, nb, 1, d)


def _mod_spec(layer, piece, rows_per_batch, tm):
    return pl.BlockSpec((None, None, None, 1, D_MODEL),
                        lambda i, *_: (layer, piece, (i * tm) // rows_per_batch, 0, 0))


def _win_kernel(x_ref, g_ref, sc_ref, sh_ref, w_ref, zc_ref, zr_ref, zg_ref):
    h = _rms(x_ref[...], g_ref[...]) * (1.0 + sc_ref[...]) + sh_ref[...]
    hb = h.astype(BF16)
    zc_ref[...] = _dot(hb, w_ref[:, :COL_CONV]).astype(zc_ref.dtype)
    zr_ref[...] = _dot(hb, w_ref[:, COL_CONV:COL_CONV + COL_RWKV])
    zg_ref[...] = _dot(hb, w_ref[:, COL_CONV + COL_RWKV:]).astype(zg_ref.dtype)


def _win(x2, g, mod, layer, w_in_b, seq, tm=512):
    n, d = x2.shape
    row = lambda i: (i, 0)
    return pl.pallas_call(
        _win_kernel,
        grid=(n // tm,),
        in_specs=[
            pl.BlockSpec((tm, d), row),
            pl.BlockSpec((1, d), lambda i: (0, 0)),
            _mod_spec(layer, 1, seq, tm),
            _mod_spec(layer, 0, seq, tm),
            pl.BlockSpec((d, D_IN), lambda i: (0, 0)),
        ],
        out_specs=[pl.BlockSpec((tm, COL_CONV), row), pl.BlockSpec((tm, COL_RWKV), row),
                   pl.BlockSpec((tm, COL_GATE), row)],
        out_shape=[jax.ShapeDtypeStruct((n, COL_CONV), BF16), jax.ShapeDtypeStruct((n, COL_RWKV), F32),
                   jax.ShapeDtypeStruct((n, COL_GATE), BF16)],
        compiler_params=_params("parallel"),
        name="win",
    )(x2, g, mod, mod, w_in_b)


def _conv_kernel(z_ref, halo_ref, w_ref, cb_ref, g_ref, b_ref, o_ref, ubuf, *, ts):
    j = pl.program_id(1)
    z = z_ref[...].astype(F32)
    zh = halo_ref[...].astype(F32)
    uh = zh[:, :CONV_WIDTH] * _sigmoid(zh[:, CONV_WIDTH:])
    rows = CONV_HALO + ts
    u = jnp.concatenate([jnp.where(j > 0, uh, 0.0),
                         z[:, :CONV_WIDTH] * _sigmoid(z[:, CONV_WIDTH:])], axis=0)
    ubuf[0] = u
    for s in range(1, SUBLANES):
        ubuf[s] = pltpu.roll(u, rows - s, axis=0)
    sub = 64
    first = CONV_HALO - (CONV_KERNEL - 1)
    for r in range(ts // sub):
        acc = jnp.zeros((sub, CONV_WIDTH), F32) + cb_ref[...]
        for k in range(CONV_KERNEL):
            s = (first + k) % SUBLANES
            base = r * sub + first + k - s
            acc = acc + w_ref[k:k + 1, :] * ubuf[s, base:base + sub, :]
        mu = jnp.mean(acc, axis=-1, keepdims=True)
        dlt = acc - mu
        var = jnp.mean(dlt * dlt, axis=-1, keepdims=True)
        y = dlt * lax.rsqrt(var + LN_EPS) * g_ref[...] + b_ref[...]
        o_ref[r * sub:(r + 1) * sub, :] = (y * _sigmoid(y)).astype(o_ref.dtype)


def _conv(zc, conv_w, conv_b, ln_g, ln_b, nb, seq, ts=256):
    n = zc.shape[0]
    tps = seq // ts
    hpt = ts // CONV_HALO
    vec = pl.BlockSpec((1, CONV_WIDTH), lambda b, j: (0, 0))
    return pl.pallas_call(
        functools.partial(_conv_kernel, ts=ts),
        grid=(nb, tps),
        in_specs=[
            pl.BlockSpec((ts, COL_CONV), lambda b, j: (b * tps + j, 0)),
            pl.BlockSpec((CONV_HALO, COL_CONV), lambda b, j: (jnp.maximum((b * tps + j) * hpt - 1, 0), 0)),
            pl.BlockSpec((CONV_KERNEL, CONV_WIDTH), lambda b, j: (0, 0)),
            vec, vec, vec,
        ],
        out_specs=pl.BlockSpec((ts, CONV_WIDTH), lambda b, j: (b * tps + j, 0)),
        out_shape=jax.ShapeDtypeStruct((n, CONV_WIDTH), BF16),
        scratch_shapes=[pltpu.VMEM((SUBLANES, CONV_HALO + ts, CONV_WIDTH), F32)],
        compiler_params=_params("parallel", "parallel"),
        name="conv",
    )(zc, zc, conv_w, conv_b, ln_g, ln_b)


def _prep_kernel(z_ref, halo_ref, mu_ref, w0_ref, a0_ref, wah_ref, wal_ref, g2_ref, kk_ref, ka_ref, rk_ref,
                 ones_ref, tri_ref, csel_ref,
                 at_ref, bt_ref, kt_ref, rt_ref, v_ref, bts_ref, kts_ref, bonus_ref, g_ref, pc_ref):
    j = pl.program_id(1)
    z = z_ref[...]
    ts = z.shape[0]
    last = jnp.where(j > 0, halo_ref[SHIFT_HALO - 1:SHIFT_HALO, :], 0.0)
    row = lax.broadcasted_iota(jnp.int32, z.shape, 0)
    prev = jnp.where(row == 0, last, pltpu.roll(z, 1, axis=0))
    zs = z + (prev - z) * mu_ref[...]
    rw = RWKV_WIDTH
    r = zs[:, :rw]
    k = zs[:, rw:2 * rw]
    v = zs[:, 2 * rw:3 * rw]
    wad = zs[:, 3 * rw:3 * rw + LORA_W + LORA_A]
    gd = zs[:, 3 * rw + LORA_W + LORA_A:]
    lane = lax.broadcasted_iota(jnp.int32, wad.shape, 1)
    xh, xl = _split(jnp.where(lane < LORA_W, jnp.tanh(wad), wad))
    wah = wah_ref[...]
    lora = _dot(xh, wah) + _dot(xl, wah) + _dot(xh, wal_ref[...])
    w = -_softplus(-(w0_ref[...] + lora[:, :rw])) - 0.5
    lw = -jnp.exp(w)
    a = _sigmoid(a0_ref[...] + lora[:, rw:])
    g_ref[...] = _dot(_sigmoid(gd).astype(BF16), g2_ref[...]).astype(g_ref.dtype)
    kk = k * kk_ref[...]
    k2 = k * (1.0 + (a - 1.0) * ka_ref[...])
    ones = ones_ref[...]
    kk = kk * lax.rsqrt(jnp.maximum(_dot_exact_rhs(kk * kk, ones), 1e-24))
    bonus_ref[...] = (_dot_exact_rhs(r * k2 * rk_ref[...], ones) * v).astype(bonus_ref.dtype)
    tri = tri_ref[...]
    lc, tot = [], []
    for h in range(ts // CUM_BLOCK):
        cs = _dot_exact_lhs(tri, lw[h * CUM_BLOCK:(h + 1) * CUM_BLOCK])
        lc.append(cs[:CUM_BLOCK])
        tot.append(cs[CUM_BLOCK:])
    lc = jnp.concatenate(lc, axis=0)
    tot = jnp.concatenate(tot, axis=0)
    e_neg = jnp.exp(-lc)
    e_end = jnp.exp(tot - lc)
    kka = kk * a
    dt = at_ref.dtype
    at_ref[...] = (-kk * jnp.exp(lc - lw)).astype(dt)
    bt_ref[...] = (kka * e_neg).astype(dt)
    kt_ref[...] = (k2 * e_neg).astype(dt)
    rt_ref[...] = (r * jnp.exp(lc)).astype(dt)
    v_ref[...] = v.astype(dt)
    bts_ref[...] = (kka * e_end).astype(dt)
    kts_ref[...] = (k2 * e_end).astype(dt)
    pc_ref[...] = jnp.exp(_dot_exact_lhs(csel_ref[...], lw))


def _prep(zr, mu, w0, a0, wa, g2, k_k, k_a, r_k, nb, seq, ts=512):
    n = zr.shape[0]
    tps = seq // ts
    hpt = ts // SHIFT_HALO
    rw = RWKV_WIDTH
    cpt = ts // CHUNK
    ones = jnp.kron(jnp.eye(N_HEADS, dtype=F32), jnp.ones((HEAD, HEAD), F32)).astype(BF16)
    blk = jnp.kron(jnp.eye(CUM_BLOCK // CHUNK, dtype=F32), jnp.ones((CHUNK, CHUNK), F32))
    tri = jnp.concatenate([jnp.tril(blk), blk], axis=0).astype(BF16)
    csel = jnp.kron(jnp.eye(cpt, dtype=F32), jnp.ones((1, CHUNK), F32)).astype(BF16)
    wa_hi, wa_lo = _split(wa)
    vec = pl.BlockSpec((1, rw), lambda b, j: (0, 0))
    full = lambda shape: pl.BlockSpec(shape, lambda b, j: (0, 0))
    rows = pl.BlockSpec((ts, rw), lambda b, j: (b * tps + j, 0))
    return pl.pallas_call(
        _prep_kernel,
        grid=(nb, tps),
        in_specs=[
            pl.BlockSpec((ts, COL_RWKV), lambda b, j: (b * tps + j, 0)),
            pl.BlockSpec((SHIFT_HALO, COL_RWKV), lambda b, j: (jnp.maximum((b * tps + j) * hpt - 1, 0), 0)),
            full((1, COL_RWKV)), vec, vec,
            full((LORA_W + LORA_A, 2 * rw)), full((LORA_W + LORA_A, 2 * rw)), full((LORA_G, rw)),
            vec, vec, vec, full((rw, rw)), full((2 * CUM_BLOCK, CUM_BLOCK)), full((cpt, ts)),
        ],
        out_specs=[rows] * 9 + [pl.BlockSpec((cpt, rw), lambda b, j: (b * tps + j, 0))],
        out_shape=[jax.ShapeDtypeStruct((n, rw), BF16)] * 9 + [jax.ShapeDtypeStruct((n // CHUNK, rw), F32)],
        compiler_params=_params("parallel", "parallel"),
        name="prep",
    )(zr, zr, mu, w0, a0, wa_hi, wa_lo, g2.astype(BF16), k_k, k_a, r_k, ones, tri, csel)


def _mm(a, b):
    return jnp.dot(a.astype(BF16), b.astype(BF16), preferred_element_type=F32)


def _mm_nt(a, b):
    return lax.dot_general(a.astype(BF16), b.astype(BF16), (((1,), (1,)), ((), ())),
                           preferred_element_type=F32)


def _mm_tn(a, b):
    return lax.dot_general(a.astype(BF16), b.astype(BF16), (((0,), (0,)), ((), ())),
                           preferred_element_type=F32)


def _scan_kernel(at_ref, bt_ref, kt_ref, rt_ref, v_ref, bts_ref, kts_ref, bonus_ref, g_ref, pc_ref,
                 gng_ref, gnb_ref, ones_ref, o_ref, s_ref, y_ref, *, ts, gb):
    @pl.when(pl.program_id(1) == 0)
    def _():
        s_ref[...] = jnp.zeros_like(s_ref)

    row_q = lax.broadcasted_iota(jnp.int32, (QUAD, QUAD), 0)
    col_q = lax.broadcasted_iota(jnp.int32, (QUAD, QUAD), 1)
    bdmask = (row_q // HEAD) == (col_q // HEAD)
    row_c = lax.broadcasted_iota(jnp.int32, (CHUNK, QUAD), 0)
    col_c = lax.broadcasted_iota(jnp.int32, (CHUNK, QUAD), 1) % CHUNK
    strict = col_c < row_c
    incl = col_c <= row_c
    eye = jnp.where(col_c == row_c, 1.0, 0.0).astype(F32)

    def bd(x):
        xb = x.astype(BF16)
        return jnp.where(bdmask, jnp.concatenate([xb] * (QUAD // CHUNK), axis=0), jnp.zeros((), BF16))

    chains = [(b, qd) for b in range(gb) for qd in range(N_QUADS)]

    def chunk(c, carry):
        sl = pl.ds(pl.multiple_of(c * CHUNK, CHUNK), CHUNK)
        lanes = [slice(qd * QUAD, (qd + 1) * QUAD) for _, qd in chains]
        ld = lambda ref: [ref[b, sl, ln] for (b, _), ln in zip(chains, lanes)]
        at, bt, kt, rt, v, bts, kts = (ld(r) for r in (at_ref, bt_ref, kt_ref, rt_ref, v_ref, bts_ref, kts_ref))
        each = range(len(chains))
        ar = [jnp.concatenate([at[i], rt[i]], axis=0) for i in each]
        pb = [_mm_nt(ar[i], bd(bt[i])) for i in each]
        pk = [_mm_nt(ar[i], bd(kt[i])) for i in each]
        l_ab = [jnp.where(strict, p[:CHUNK], 0.0) for p in pb]
        a_rb = [jnp.where(incl, p[CHUNK:], 0.0) for p in pb]
        l_ak = [jnp.where(strict, p[:CHUNK], 0.0) for p in pk]
        a_rk = [jnp.where(incl, p[CHUNK:], 0.0) for p in pk]
        tm = [eye + l for l in l_ab]
        lp = [_mm(l, bd(l)) for l in l_ab]
        for _ in range(4):
            both = [_mm(jnp.concatenate([tm[i], lp[i]], axis=0), bd(lp[i])) for i in each]
            tm = [tm[i] + both[i][:CHUNK] for i in each]
            lp = [both[i][CHUNK:] for i in each]
        tm = [tm[i] + _mm(tm[i], bd(lp[i])) for i in each]
        bdv = [bd(x) for x in v]
        x0 = [_mm(l_ak[i], bdv[i]) for i in each]
        tu = [_mm(tm[i], jnp.concatenate([bd(x0[i]), bd(at[i])], axis=1)) for i in each]
        u0 = [t[:, :QUAD] for t in tu]
        w = [t[:, QUAD:] for t in tu]
        qy = [_mm(a_rb[i], jnp.concatenate([bd(w[i]), bd(u0[i])], axis=1)) for i in each]
        y0 = [qy[i][:, QUAD:] + _mm(a_rk[i], bdv[i]) for i in each]
        q = [rt[i].astype(F32) + qy[i][:, :QUAD] for i in each]
        s = [s_ref[i] for i in each]
        qw_s = [_mm_nt(jnp.concatenate([q[i], w[i]], axis=0), s[i]) for i in each]
        for i, ((b, _), ln) in enumerate(zip(chains, lanes)):
            y_ref[b, sl, ln] = y0[i] + qw_s[i][:CHUNK]
        u = [(u0[i] + qw_s[i][CHUNK:]).astype(BF16) for i in each]
        n_p = [jnp.where(bdmask, _mm_tn(jnp.concatenate([u[i], v[i]], axis=0),
                                        jnp.concatenate([bts[i], kts[i]], axis=0)), 0.0) for i in each]
        for i, ((b, _), ln) in enumerate(zip(chains, lanes)):
            s_ref[i] = s[i] * pc_ref[b, pl.ds(c, 1), ln] + n_p[i]
        return carry

    lax.fori_loop(0, ts // CHUNK, chunk, 0)

    ones = ones_ref[...]
    for b in range(gb):
        for qd in range(N_QUADS):
            ln = slice(qd * QUAD, (qd + 1) * QUAD)
            y = y_ref[b, :, ln]
            dlt = y - _dot_exact_rhs(y, ones) * (1.0 / HEAD)
            var = _dot_exact_rhs(dlt * dlt, ones) * (1.0 / HEAD)
            yn = dlt * lax.rsqrt(var + GN_EPS) * gng_ref[:, ln] + gnb_ref[:, ln]
            o_ref[b, :, ln] = ((yn + bonus_ref[b, :, ln].astype(F32)) * g_ref[b, :, ln].astype(F32)).astype(o_ref.dtype)


def _scan(prep_out, gn_g, gn_b, nb, seq, ts=256, gb=4):
    n = prep_out[0].shape[0]
    rw = RWKV_WIDTH
    cpt = ts // CHUNK
    ones = jnp.kron(jnp.eye(QUAD // HEAD, dtype=F32), jnp.ones((HEAD, HEAD), F32)).astype(BF16)
    rows = pl.BlockSpec((gb, ts, rw), lambda i, t: (i, t, 0))
    vec = pl.BlockSpec((1, rw), lambda i, t: (0, 0))
    args = [a.reshape(nb, seq, rw) for a in prep_out[:9]] + [prep_out[9].reshape(nb, seq // ts, cpt, rw)]
    out = pl.pallas_call(
        functools.partial(_scan_kernel, ts=ts, gb=gb),
        grid=(nb // gb, seq // ts),
        in_specs=[rows] * 9 + [pl.BlockSpec((gb, None, cpt, rw), lambda i, t: (i, t, 0, 0)), vec, vec,
                               pl.BlockSpec((QUAD, QUAD), lambda i, t: (0, 0))],
        out_specs=rows,
        out_shape=jax.ShapeDtypeStruct((nb, seq, rw), BF16),
        scratch_shapes=[pltpu.VMEM((gb * N_QUADS, QUAD, QUAD), F32), pltpu.VMEM((gb, ts, rw), F32)],
        compiler_params=_params("parallel", "arbitrary"),
        name="scan",
    )(*args, gn_g, gn_b, ones)
    return out.reshape(n, rw)


def _merge_kernel(uc_ref, ur_ref, zg_ref, x_ref, pc_ref, pr_ref, wo_ref, gpost_ref, gt_ref,
                  gpre_ref, sc_ref, sh_ref, wrt_ref, xo_ref, h_ref, lg_ref):
    yc = _dot(uc_ref[...], pc_ref[...])
    yr = _dot(ur_ref[...], pr_ref[...])
    zg = zg_ref[...].astype(F32)
    m = _sigmoid(zg[:, :D_MODEL]) * yc + _sigmoid(zg[:, D_MODEL:]) * yr
    y = _dot(m.astype(BF16), wo_ref[...])
    xn = x_ref[...] + gt_ref[...] * _rms(y, gpost_ref[...])
    xo_ref[...] = xn
    h = _rms(xn, gpre_ref[...]) * (1.0 + sc_ref[...]) + sh_ref[...]
    h_ref[...] = h.astype(h_ref.dtype)
    lg_ref[...] = lax.dot_general(wrt_ref[...], h, (((1,), (1,)), ((), ())), precision=HI,
                                  preferred_element_type=F32)


def _merge(uc, ur, zg, x2, p_conv_b, p_rwkv_b, w_o_b, g_post, g_pre, mod, layer, w_router_t, seq, tm=512):
    n, d = x2.shape
    row = lambda i: (i, 0)
    full = lambda shape: pl.BlockSpec(shape, lambda i: (0, 0))
    return pl.pallas_call(
        _merge_kernel,
        grid=(n // tm,),
        in_specs=[
            pl.BlockSpec((tm, CONV_WIDTH), row), pl.BlockSpec((tm, RWKV_WIDTH), row),
            pl.BlockSpec((tm, COL_GATE), row), pl.BlockSpec((tm, d), row),
            full((CONV_WIDTH, d)), full((RWKV_WIDTH, d)), full((d, d)),
            full((1, d)), _mod_spec(layer, 2, seq, tm),
            full((1, d)), _mod_spec(layer, 4, seq, tm), _mod_spec(layer, 3, seq, tm),
            full((N_EXPERTS, d)),
        ],
        out_specs=[pl.BlockSpec((tm, d), row), pl.BlockSpec((tm, d), row),
                   pl.BlockSpec((N_EXPERTS, tm), lambda i: (0, i))],
        out_shape=[jax.ShapeDtypeStruct((n, d), F32), jax.ShapeDtypeStruct((n, d), BF16),
                   jax.ShapeDtypeStruct((N_EXPERTS, n), F32)],
        compiler_params=_params("parallel"),
        name="merge",
    )(uc, ur, zg, x2, p_conv_b, p_rwkv_b, w_o_b, g_post, mod, g_pre, mod, mod, w_router_t)


def _route_kernel(lg_ref, b_ref, o_ref):
    s = _sigmoid(lg_ref[...])
    biased = s + b_ref[...]
    t = s.shape[1]
    member = lax.broadcasted_iota(jnp.int32, (GROUP_SIZE, t), 0)
    grp = []
    for g in range(N_GROUPS):
        bg = biased[g * GROUP_SIZE:(g + 1) * GROUP_SIZE, :]
        m1 = jnp.max(bg, axis=0, keepdims=True)
        first = jnp.min(jnp.where(bg == m1, member, GROUP_SIZE), axis=0, keepdims=True)
        m2 = jnp.max(jnp.where(member == first, -jnp.inf, bg), axis=0, keepdims=True)
        grp.append(m1 + m2)
    masked = []
    for g in range(N_GROUPS):
        rank = jnp.zeros((1, t), jnp.int32)
        for o in range(N_GROUPS):
            if o == g:
                continue
            ahead = (grp[o] > grp[g]) if o > g else (grp[o] >= grp[g])
            rank = rank + jnp.where(ahead, 1, 0)
        keep = rank < TOPK_GROUPS
        masked.append(jnp.where(keep, biased[g * GROUP_SIZE:(g + 1) * GROUP_SIZE, :], -jnp.inf))
    masked = jnp.concatenate(masked, axis=0)
    eidx = lax.broadcasted_iota(jnp.int32, masked.shape, 0)
    rank = jnp.zeros(masked.shape, jnp.int32)
    for o in range(N_EXPERTS):
        ro = masked[o:o + 1, :]
        rank = rank + jnp.where(ro > masked, 1, jnp.where((ro == masked) & (eidx > o), 1, 0))
    sel = jnp.where(rank < TOP_K, s, 0.0)
    gate = sel / jnp.sum(sel, axis=0, keepdims=True) * ROUTED_SCALE
    full = jnp.concatenate([gate, jnp.zeros((GATE_LANES - N_EXPERTS, t), F32)], axis=0)
    o_ref[...] = full.T


def _route(logits_t, b_router, tt=512):
    e, n = logits_t.shape
    return pl.pallas_call(
        _route_kernel,
        grid=(n // tt,),
        in_specs=[pl.BlockSpec((e, tt), lambda i: (0, i)), pl.BlockSpec((e, 1), lambda i: (0, 0))],
        out_specs=pl.BlockSpec((tt, GATE_LANES), lambda i: (i, 0)),
        out_shape=jax.ShapeDtypeStruct((n, GATE_LANES), F32),
        compiler_params=_params("parallel"),
        name="route",
    )(logits_t, b_router)


def _swiglu(h, wg, wu):
    gte = _dot(h, wg)
    return gte * _sigmoid(gte) * _dot(h, wu)


def _moe_kernel(h_ref, gate_ref, wg_ref, wu_ref, wd_ref, sg_ref, su_ref, sd_ref, x_ref, gpost_ref, gt_ref,
                o_ref, acc_ref):
    e = pl.program_id(1)
    h = h_ref[...]

    @pl.when(e == 0)
    def _():
        acc_ref[...] = _dot(_swiglu(h, sg_ref[...], su_ref[...]).astype(BF16), sd_ref[...])

    gates = gate_ref[...]
    lane = lax.broadcasted_iota(jnp.int32, gates.shape, 1)
    parts = []
    for i in range(EXPERTS_PER_STEP):
        gcol = jnp.sum(jnp.where(lane == e * EXPERTS_PER_STEP + i, gates, 0.0), axis=1, keepdims=True)
        parts.append((_swiglu(h, wg_ref[i], wu_ref[i]) * gcol).astype(BF16))
    act = jnp.concatenate(parts, axis=1)
    wd = wd_ref[...].reshape(EXPERTS_PER_STEP * D_EXPERT, D_MODEL)
    acc_ref[...] += _dot(act, wd)

    @pl.when(e == pl.num_programs(1) - 1)
    def _():
        o_ref[...] = x_ref[...] + gt_ref[...] * _rms(acc_ref[...], gpost_ref[...])


def _moe(h2, gates, wg, wu, wd, sg, su, sd, x2, g_post, mod, layer, seq, tm=1024):
    n, d = x2.shape
    ne = wg.shape[0]
    eb = EXPERTS_PER_STEP
    full = lambda shape: pl.BlockSpec(shape, lambda i, e: (0, 0))
    return pl.pallas_call(
        _moe_kernel,
        grid=(n // tm, ne // eb),
        in_specs=[
            pl.BlockSpec((tm, d), lambda i, e: (i, 0)),
            pl.BlockSpec((tm, GATE_LANES), lambda i, e: (i, 0)),
            pl.BlockSpec((eb, d, D_EXPERT), lambda i, e: (e, 0, 0)),
            pl.BlockSpec((eb, d, D_EXPERT), lambda i, e: (e, 0, 0)),
            pl.BlockSpec((eb, D_EXPERT, d), lambda i, e: (e, 0, 0)),
            full((d, D_EXPERT)), full((d, D_EXPERT)), full((D_EXPERT, d)),
            pl.BlockSpec((tm, d), lambda i, e: (i, 0)),
            full((1, d)),
            _mod_spec(layer, 5, seq, tm),
        ],
        out_specs=pl.BlockSpec((tm, d), lambda i, e: (i, 0)),
        out_shape=jax.ShapeDtypeStruct((n, d), F32),
        scratch_shapes=[pltpu.VMEM((tm, d), F32)],
        compiler_params=_params("parallel", "arbitrary"),
        name="moe",
    )(h2, gates, wg, wu, wd, sg, su, sd, x2, g_post, mod)


def kernel(x, c, w_ada, b_ada, norm_mix_pre, norm_mix_post, norm_ffn_pre, norm_ffn_post, w_in, mu_shift, conv_w, conv_b, conv_ln_g, conv_ln_b, p_conv, w0, w2, a0, a2, g2, k_k, k_a, r_k, gn_g, gn_b, p_rwkv, w_o, w_router, b_router, we_gate, we_up, we_down, ws_gate, ws_up, ws_down):
    nb, seq, d = x.shape
    depth = w_ada.shape[0]
    n = nb * seq
    rw = RWKV_WIDTH
    mod = _ada(c, w_ada, b_ada)
    x2 = x.reshape(n, d)
    row = lambda a: a.reshape(1, -1)
    bf = lambda a: a.astype(BF16)
    for l in range(depth):
        zc, zr, zg = _win(x2, row(norm_mix_pre[l]), mod, l, bf(w_in[l]), seq)
        uc = _conv(zc, conv_w[l], row(conv_b[l]), row(conv_ln_g[l]), row(conv_ln_b[l]), nb, seq)
        wa = jnp.zeros((LORA_W + LORA_A, 2 * rw), F32)
        wa = wa.at[:LORA_W, :rw].set(w2[l]).at[LORA_W:, rw:].set(a2[l])
        prep_out = _prep(zr, row(mu_shift[l]), row(w0[l]), row(a0[l]), wa, g2[l], row(k_k[l]),
                         row(k_a[l]), row(r_k[l]), nb, seq)
        ur = _scan(prep_out, row(gn_g[l]), row(gn_b[l]), nb, seq)
        x2, h2, logits_t = _merge(uc, ur, zg, x2, bf(p_conv[l]), bf(p_rwkv[l]), bf(w_o[l]),
                                  row(norm_mix_post[l]), row(norm_ffn_pre[l]), mod, l, w_router[l].T, seq)
        gates = _route(logits_t, b_router[l].reshape(-1, 1))
        x2 = _moe(h2, gates, bf(we_gate[l]), bf(we_up[l]), bf(we_down[l]), bf(ws_gate[l]), bf(ws_up[l]),
                  bf(ws_down[l]), x2, row(norm_ffn_post[l]), mod, l, seq)
    return x2.reshape(nb, seq, d)
```

```python
import functools

import jax
import jax.numpy as jnp
from jax import lax
from jax.experimental import pallas as pl
from jax.experimental.pallas import tpu as pltpu
from jax.experimental.pallas import tpu_sc as plsc

F32 = jnp.float32
BF16 = jnp.bfloat16
HI = lax.Precision.HIGHEST

D_MODEL = 1024
CONV_WIDTH = 512
CONV_KERNEL = 31
RWKV_WIDTH = 512
HEAD = 64
N_HEADS = RWKV_WIDTH // HEAD
LORA_W = 64
LORA_A = 64
LORA_G = 128
N_EXPERTS = 64
TOP_K = 8
N_GROUPS = 8
TOPK_GROUPS = 4
GROUP_SIZE = N_EXPERTS // N_GROUPS
D_EXPERT = 256
ROUTED_SCALE = 2.5
RMS_EPS = 1e-6
LN_EPS = 1e-5
GN_EPS = 64e-5
COL_CONV = 2 * CONV_WIDTH
COL_RWKV = 3 * RWKV_WIDTH + LORA_W + LORA_A + LORA_G
COL_GATE = 2 * D_MODEL
D_IN = COL_CONV + COL_RWKV + COL_GATE

CHUNK = 64
QUAD = 4 * HEAD
N_QUADS = RWKV_WIDTH // QUAD
CUM_BLOCK = 256
CONV_HALO = 32
SUBLANES = 8
SHIFT_HALO = SUBLANES
GATE_LANES = 128
EXPERT_BLOCK = 256
ROW_WORDS = D_MODEL // 2
SC_CORES = 2
SC_SUBCORES = 16
SC_WORKERS = SC_CORES * SC_SUBCORES
SC_CHUNK = 128
VMEM_LIMIT = 52 * 1024 * 1024


def _params(*sem):
    return pltpu.CompilerParams(dimension_semantics=sem, vmem_limit_bytes=VMEM_LIMIT)


def _sigmoid(x):
    return 1.0 / (1.0 + jnp.exp(-x))


def _softplus(x):
    return jnp.maximum(x, 0.0) + jnp.log(1.0 + jnp.exp(-jnp.abs(x)))


def _rms(x, g):
    return x * lax.rsqrt(jnp.mean(x * x, axis=-1, keepdims=True) + RMS_EPS) * g


def _split(x):
    hi = x.astype(BF16)
    return hi, (x - hi.astype(F32)).astype(BF16)


def _dot(a, b):
    return jnp.dot(a, b, preferred_element_type=F32)


def _dot_exact_rhs(x, w):
    hi, lo = _split(x)
    return _dot(hi, w) + _dot(lo, w)


def _pack_rows(y):
    half = y.shape[1] // 2
    hi = lax.bitcast_convert_type(y[:, :half].astype(BF16).astype(F32), jnp.uint32)
    lo = lax.bitcast_convert_type(y[:, half:].astype(BF16).astype(F32), jnp.uint32)
    return lax.bitcast_convert_type((hi & jnp.uint32(0xFFFF0000)) | (lo >> 16), jnp.int32)


def _unpack_rows(u):
    u = lax.bitcast_convert_type(u, jnp.uint32)
    hi = lax.bitcast_convert_type(u & jnp.uint32(0xFFFF0000), F32)
    lo = lax.bitcast_convert_type(u << 16, F32)
    return jnp.concatenate([hi, lo], axis=1)


def _dot_exact_lhs(w, x):
    hi, lo = _split(x)
    return _dot(w, hi) + _dot(w, lo)


def _ada_kernel(c_ref, w_ref, b_ref, o_ref):
    c = c_ref[...]
    o_ref[...] = jnp.dot(c * _sigmoid(c), w_ref[...], precision=HI,
                         preferred_element_type=F32) + b_ref[...]


def _ada(c, w_ada, b_ada):
    nl, d, _ = w_ada.shape
    nb = c.shape[0]
    out = pl.pallas_call(
        _ada_kernel,
        grid=(nl, 6),
        in_specs=[
            pl.BlockSpec((nb, d), lambda l, k: (0, 0)),
            pl.BlockSpec((None, d, d), lambda l, k: (l, 0, k)),
            pl.BlockSpec((None, None, 1, d), lambda l, k: (l, k, 0, 0)),
        ],
        out_specs=pl.BlockSpec((None, None, nb, d), lambda l, k: (l, k, 0, 0)),
        out_shape=jax.ShapeDtypeStruct((nl, 6, nb, d), F32),
        compiler_params=_params("parallel", "parallel"),
        name="ada",
    )(c, w_ada, b_ada.reshape(nl, 6, 1, d))
    return out.reshape(nl, 6, nb, 1, d)


def _mod_spec(layer, piece, rows_per_batch, tm):
    return pl.BlockSpec((None, None, None, 1, D_MODEL),
                        lambda i, *_: (layer, piece, (i * tm) // rows_per_batch, 0, 0))


def _win_kernel(x_ref, g_ref, sc_ref, sh_ref, w_ref, zc_ref, zr_ref, zg_ref):
    h = _rms(x_ref[...], g_ref[...]) * (1.0 + sc_ref[...]) + sh_ref[...]
    hb = h.astype(BF16)
    zc_ref[...] = _dot(hb, w_ref[:, :COL_CONV]).astype(zc_ref.dtype)
    zr_ref[...] = _dot(hb, w_ref[:, COL_CONV:COL_CONV + COL_RWKV])
    zg_ref[...] = _dot(hb, w_ref[:, COL_CONV + COL_RWKV:]).astype(zg_ref.dtype)


def _win(x2, g, mod, layer, w_in_b, seq, tm=512):
    n, d = x2.shape
    row = lambda i: (i, 0)
    return pl.pallas_call(
        _win_kernel,
        grid=(n // tm,),
        in_specs=[
            pl.BlockSpec((tm, d), row),
            pl.BlockSpec((1, d), lambda i: (0, 0)),
            _mod_spec(layer, 1, seq, tm),
            _mod_spec(layer, 0, seq, tm),
            pl.BlockSpec((d, D_IN), lambda i: (0, 0)),
        ],
        out_specs=[pl.BlockSpec((tm, COL_CONV), row), pl.BlockSpec((tm, COL_RWKV), row),
                   pl.BlockSpec((tm, COL_GATE), row)],
        out_shape=[jax.ShapeDtypeStruct((n, COL_CONV), BF16), jax.ShapeDtypeStruct((n, COL_RWKV), F32),
                   jax.ShapeDtypeStruct((n, COL_GATE), BF16)],
        compiler_params=_params("parallel"),
        name="win",
    )(x2, g, mod, mod, w_in_b)


def _conv_kernel(z_ref, halo_ref, w_ref, cb_ref, g_ref, b_ref, o_ref, ubuf, *, ts):
    j = pl.program_id(1)
    z = z_ref[...].astype(F32)
    zh = halo_ref[...].astype(F32)
    uh = zh[:, :CONV_WIDTH] * _sigmoid(zh[:, CONV_WIDTH:])
    rows = CONV_HALO + ts
    u = jnp.concatenate([jnp.where(j > 0, uh, 0.0),
                         z[:, :CONV_WIDTH] * _sigmoid(z[:, CONV_WIDTH:])], axis=0)
    ubuf[0] = u
    for s in range(1, SUBLANES):
        ubuf[s] = pltpu.roll(u, rows - s, axis=0)
    sub = 64
    first = CONV_HALO - (CONV_KERNEL - 1)
    for r in range(ts // sub):
        acc = jnp.zeros((sub, CONV_WIDTH), F32) + cb_ref[...]
        for k in range(CONV_KERNEL):
            s = (first + k) % SUBLANES
            base = r * sub + first + k - s
            acc = acc + w_ref[k:k + 1, :] * ubuf[s, base:base + sub, :]
        mu = jnp.mean(acc, axis=-1, keepdims=True)
        dlt = acc - mu
        var = jnp.mean(dlt * dlt, axis=-1, keepdims=True)
        y = dlt * lax.rsqrt(var + LN_EPS) * g_ref[...] + b_ref[...]
        o_ref[r * sub:(r + 1) * sub, :] = (y * _sigmoid(y)).astype(o_ref.dtype)


def _conv(zc, conv_w, conv_b, ln_g, ln_b, nb, seq, ts=256):
    n = zc.shape[0]
    tps = seq // ts
    hpt = ts // CONV_HALO
    vec = pl.BlockSpec((1, CONV_WIDTH), lambda b, j: (0, 0))
    return pl.pallas_call(
        functools.partial(_conv_kernel, ts=ts),
        grid=(nb, tps),
        in_specs=[
            pl.BlockSpec((ts, COL_CONV), lambda b, j: (b * tps + j, 0)),
            pl.BlockSpec((CONV_HALO, COL_CONV), lambda b, j: (jnp.maximum((b * tps + j) * hpt - 1, 0), 0)),
            pl.BlockSpec((CONV_KERNEL, CONV_WIDTH), lambda b, j: (0, 0)),
            vec, vec, vec,
        ],
        out_specs=pl.BlockSpec((ts, CONV_WIDTH), lambda b, j: (b * tps + j, 0)),
        out_shape=jax.ShapeDtypeStruct((n, CONV_WIDTH), BF16),
        scratch_shapes=[pltpu.VMEM((SUBLANES, CONV_HALO + ts, CONV_WIDTH), F32)],
        compiler_params=_params("parallel", "parallel"),
        name="conv",
    )(zc, zc, conv_w, conv_b, ln_g, ln_b)


def _prep_kernel(z_ref, halo_ref, mu_ref, w0_ref, a0_ref, wah_ref, wal_ref, g2_ref, kk_ref, ka_ref, rk_ref,
                 ones_ref, tri_ref, csel_ref,
                 at_ref, bt_ref, kt_ref, rt_ref, v_ref, bts_ref, kts_ref, bonus_ref, g_ref, pc_ref):
    j = pl.program_id(1)
    z = z_ref[...]
    ts = z.shape[0]
    last = jnp.where(j > 0, halo_ref[SHIFT_HALO - 1:SHIFT_HALO, :], 0.0)
    row = lax.broadcasted_iota(jnp.int32, z.shape, 0)
    prev = jnp.where(row == 0, last, pltpu.roll(z, 1, axis=0))
    zs = z + (prev - z) * mu_ref[...]
    rw = RWKV_WIDTH
    r = zs[:, :rw]
    k = zs[:, rw:2 * rw]
    v = zs[:, 2 * rw:3 * rw]
    wad = zs[:, 3 * rw:3 * rw + LORA_W + LORA_A]
    gd = zs[:, 3 * rw + LORA_W + LORA_A:]
    lane = lax.broadcasted_iota(jnp.int32, wad.shape, 1)
    xh, xl = _split(jnp.where(lane < LORA_W, jnp.tanh(wad), wad))
    wah = wah_ref[...]
    lora = _dot(xh, wah) + _dot(xl, wah) + _dot(xh, wal_ref[...])
    w = -_softplus(-(w0_ref[...] + lora[:, :rw])) - 0.5
    lw = -jnp.exp(w)
    a = _sigmoid(a0_ref[...] + lora[:, rw:])
    g_ref[...] = _dot(_sigmoid(gd).astype(BF16), g2_ref[...]).astype(g_ref.dtype)
    kk = k * kk_ref[...]
    k2 = k * (1.0 + (a - 1.0) * ka_ref[...])
    ones = ones_ref[...]
    kk = kk * lax.rsqrt(jnp.maximum(_dot_exact_rhs(kk * kk, ones), 1e-24))
    bonus_ref[...] = (_dot_exact_rhs(r * k2 * rk_ref[...], ones) * v).astype(bonus_ref.dtype)
    tri = tri_ref[...]
    lc, tot = [], []
    for h in range(ts // CUM_BLOCK):
        cs = _dot_exact_lhs(tri, lw[h * CUM_BLOCK:(h + 1) * CUM_BLOCK])
        lc.append(cs[:CUM_BLOCK])
        tot.append(cs[CUM_BLOCK:])
    lc = jnp.concatenate(lc, axis=0)
    tot = jnp.concatenate(tot, axis=0)
    e_neg = jnp.exp(-lc)
    e_end = jnp.exp(tot - lc)
    kka = kk * a
    dt = at_ref.dtype
    at_ref[...] = (-kk * jnp.exp(lc - lw)).astype(dt)
    bt_ref[...] = (kka * e_neg).astype(dt)
    kt_ref[...] = (k2 * e_neg).astype(dt)
    rt_ref[...] = (r * jnp.exp(lc)).astype(dt)
    v_ref[...] = v.astype(dt)
    bts_ref[...] = (kka * e_end).astype(dt)
    kts_ref[...] = (k2 * e_end).astype(dt)
    pc_ref[...] = jnp.exp(_dot_exact_lhs(csel_ref[...], lw))


def _prep(zr, mu, w0, a0, wa, g2, k_k, k_a, r_k, nb, seq, ts=512):
    n = zr.shape[0]
    tps = seq // ts
    hpt = ts // SHIFT_HALO
    rw = RWKV_WIDTH
    cpt = ts // CHUNK
    ones = jnp.kron(jnp.eye(N_HEADS, dtype=F32), jnp.ones((HEAD, HEAD), F32)).astype(BF16)
    blk = jnp.kron(jnp.eye(CUM_BLOCK // CHUNK, dtype=F32), jnp.ones((CHUNK, CHUNK), F32))
    tri = jnp.concatenate([jnp.tril(blk), blk], axis=0).astype(BF16)
    csel = jnp.kron(jnp.eye(cpt, dtype=F32), jnp.ones((1, CHUNK), F32)).astype(BF16)
    wa_hi, wa_lo = _split(wa)
    vec = pl.BlockSpec((1, rw), lambda b, j: (0, 0))
    full = lambda shape: pl.BlockSpec(shape, lambda b, j: (0, 0))
    rows = pl.BlockSpec((ts, rw), lambda b, j: (b * tps + j, 0))
    return pl.pallas_call(
        _prep_kernel,
        grid=(nb, tps),
        in_specs=[
            pl.BlockSpec((ts, COL_RWKV), lambda b, j: (b * tps + j, 0)),
            pl.BlockSpec((SHIFT_HALO, COL_RWKV), lambda b, j: (jnp.maximum((b * tps + j) * hpt - 1, 0), 0)),
            full((1, COL_RWKV)), vec, vec,
            full((LORA_W + LORA_A, 2 * rw)), full((LORA_W + LORA_A, 2 * rw)), full((LORA_G, rw)),
            vec, vec, vec, full((rw, rw)), full((2 * CUM_BLOCK, CUM_BLOCK)), full((cpt, ts)),
        ],
        out_specs=[rows] * 9 + [pl.BlockSpec((cpt, rw), lambda b, j: (b * tps + j, 0))],
        out_shape=[jax.ShapeDtypeStruct((n, rw), BF16)] * 9 + [jax.ShapeDtypeStruct((n // CHUNK, rw), F32)],
        compiler_params=_params("parallel", "parallel"),
        name="prep",
    )(zr, zr, mu, w0, a0, wa_hi, wa_lo, g2.astype(BF16), k_k, k_a, r_k, ones, tri, csel)


def _mm(a, b):
    return jnp.dot(a.astype(BF16), b.astype(BF16), preferred_element_type=F32)


def _mm_nt(a, b):
    return lax.dot_general(a.astype(BF16), b.astype(BF16), (((1,), (1,)), ((), ())),
                           preferred_element_type=F32)


def _mm_tn(a, b):
    return lax.dot_general(a.astype(BF16), b.astype(BF16), (((0,), (0,)), ((), ())),
                           preferred_element_type=F32)


def _scan_kernel(at_ref, bt_ref, kt_ref, rt_ref, v_ref, bts_ref, kts_ref, bonus_ref, g_ref, pc_ref,
                 gng_ref, gnb_ref, ones_ref, o_ref, s_ref, y_ref, *, ts, gb):
    @pl.when(pl.program_id(1) == 0)
    def _():
        s_ref[...] = jnp.zeros_like(s_ref)

    row_q = lax.broadcasted_iota(jnp.int32, (QUAD, QUAD), 0)
    col_q = lax.broadcasted_iota(jnp.int32, (QUAD, QUAD), 1)
    bdmask = (row_q // HEAD) == (col_q // HEAD)
    row_c = lax.broadcasted_iota(jnp.int32, (CHUNK, QUAD), 0)
    col_c = lax.broadcasted_iota(jnp.int32, (CHUNK, QUAD), 1) % CHUNK
    strict = col_c < row_c
    incl = col_c <= row_c
    eye = jnp.where(col_c == row_c, 1.0, 0.0).astype(F32)

    def bd(x):
        xb = x.astype(BF16)
        return jnp.where(bdmask, jnp.concatenate([xb] * (QUAD // CHUNK), axis=0), jnp.zeros((), BF16))

    chains = [(b, qd) for b in range(gb) for qd in range(N_QUADS)]

    def chunk(c, carry):
        sl = pl.ds(pl.multiple_of(c * CHUNK, CHUNK), CHUNK)
        lanes = [slice(qd * QUAD, (qd + 1) * QUAD) for _, qd in chains]
        ld = lambda ref: [ref[b, sl, ln] for (b, _), ln in zip(chains, lanes)]
        at, bt, kt, rt, v, bts, kts = (ld(r) for r in (at_ref, bt_ref, kt_ref, rt_ref, v_ref, bts_ref, kts_ref))
        each = range(len(chains))
        ar = [jnp.concatenate([at[i], rt[i]], axis=0) for i in each]
        pb = [_mm_nt(ar[i], bd(bt[i])) for i in each]
        pk = [_mm_nt(ar[i], bd(kt[i])) for i in each]
        l_ab = [jnp.where(strict, p[:CHUNK], 0.0) for p in pb]
        a_rb = [jnp.where(incl, p[CHUNK:], 0.0) for p in pb]
        l_ak = [jnp.where(strict, p[:CHUNK], 0.0) for p in pk]
        a_rk = [jnp.where(incl, p[CHUNK:], 0.0) for p in pk]
        tm = [eye + l for l in l_ab]
        lp = [_mm(l, bd(l)) for l in l_ab]
        for _ in range(4):
            both = [_mm(jnp.concatenate([tm[i], lp[i]], axis=0), bd(lp[i])) for i in each]
            tm = [tm[i] + both[i][:CHUNK] for i in each]
            lp = [both[i][CHUNK:] for i in each]
        tm = [tm[i] + _mm(tm[i], bd(lp[i])) for i in each]
        bdv = [bd(x) for x in v]
        x0 = [_mm(l_ak[i], bdv[i]) for i in each]
        tu = [_mm(tm[i], jnp.concatenate([bd(x0[i]), bd(at[i])], axis=1)) for i in each]
        u0 = [t[:, :QUAD] for t in tu]
        w = [t[:, QUAD:] for t in tu]
        qy = [_mm(a_rb[i], jnp.concatenate([bd(w[i]), bd(u0[i])], axis=1)) for i in each]
        y0 = [qy[i][:, QUAD:] + _mm(a_rk[i], bdv[i]) for i in each]
        q = [rt[i].astype(F32) + qy[i][:, :QUAD] for i in each]
        s = [s_ref[i] for i in each]
        qw_s = [_mm_nt(jnp.concatenate([q[i], w[i]], axis=0), s[i]) for i in each]
        for i, ((b, _), ln) in enumerate(zip(chains, lanes)):
            y_ref[b, sl, ln] = y0[i] + qw_s[i][:CHUNK]
        u = [(u0[i] + qw_s[i][CHUNK:]).astype(BF16) for i in each]
        n_p = [jnp.where(bdmask, _mm_tn(jnp.concatenate([u[i], v[i]], axis=0),
                                        jnp.concatenate([bts[i], kts[i]], axis=0)), 0.0) for i in each]
        for i, ((b, _), ln) in enumerate(zip(chains, lanes)):
            s_ref[i] = s[i] * pc_ref[b, pl.ds(c, 1), ln] + n_p[i]
        return carry

    lax.fori_loop(0, ts // CHUNK, chunk, 0)

    ones = ones_ref[...]
    for b in range(gb):
        for qd in range(N_QUADS):
            ln = slice(qd * QUAD, (qd + 1) * QUAD)
            y = y_ref[b, :, ln]
            dlt = y - _dot_exact_rhs(y, ones) * (1.0 / HEAD)
            var = _dot_exact_rhs(dlt * dlt, ones) * (1.0 / HEAD)
            yn = dlt * lax.rsqrt(var + GN_EPS) * gng_ref[:, ln] + gnb_ref[:, ln]
            o_ref[b, :, ln] = ((yn + bonus_ref[b, :, ln].astype(F32)) * g_ref[b, :, ln].astype(F32)).astype(o_ref.dtype)


def _scan(prep_out, gn_g, gn_b, nb, seq, ts=256, gb=4):
    n = prep_out[0].shape[0]
    rw = RWKV_WIDTH
    cpt = ts // CHUNK
    ones = jnp.kron(jnp.eye(QUAD // HEAD, dtype=F32), jnp.ones((HEAD, HEAD), F32)).astype(BF16)
    rows = pl.BlockSpec((gb, ts, rw), lambda i, t: (i, t, 0))
    vec = pl.BlockSpec((1, rw), lambda i, t: (0, 0))
    args = [a.reshape(nb, seq, rw) for a in prep_out[:9]] + [prep_out[9].reshape(nb, seq // ts, cpt, rw)]
    out = pl.pallas_call(
        functools.partial(_scan_kernel, ts=ts, gb=gb),
        grid=(nb // gb, seq // ts),
        in_specs=[rows] * 9 + [pl.BlockSpec((gb, None, cpt, rw), lambda i, t: (i, t, 0, 0)), vec, vec,
                               pl.BlockSpec((QUAD, QUAD), lambda i, t: (0, 0))],
        out_specs=rows,
        out_shape=jax.ShapeDtypeStruct((nb, seq, rw), BF16),
        scratch_shapes=[pltpu.VMEM((gb * N_QUADS, QUAD, QUAD), F32), pltpu.VMEM((gb, ts, rw), F32)],
        compiler_params=_params("parallel", "arbitrary"),
        name="scan",
    )(*args, gn_g, gn_b, ones)
    return out.reshape(n, rw)


def _merge_kernel(uc_ref, ur_ref, zg_ref, x_ref, pc_ref, pr_ref, wo_ref, gpost_ref, gt_ref,
                  gpre_ref, sc_ref, sh_ref, wrt_ref, xo_ref, h_ref, lg_ref):
    yc = _dot(uc_ref[...], pc_ref[...])
    yr = _dot(ur_ref[...], pr_ref[...])
    zg = zg_ref[...].astype(F32)
    m = _sigmoid(zg[:, :D_MODEL]) * yc + _sigmoid(zg[:, D_MODEL:]) * yr
    y = _dot(m.astype(BF16), wo_ref[...])
    xn = x_ref[...] + gt_ref[...] * _rms(y, gpost_ref[...])
    xo_ref[...] = xn
    h = _rms(xn, gpre_ref[...]) * (1.0 + sc_ref[...]) + sh_ref[...]
    h_ref[...] = _pack_rows(h)
    lg_ref[...] = lax.dot_general(wrt_ref[...], h, (((1,), (1,)), ((), ())), precision=HI,
                                  preferred_element_type=F32)


def _merge(uc, ur, zg, x2, p_conv_b, p_rwkv_b, w_o_b, g_post, g_pre, mod, layer, w_router_t, seq, tm=512):
    n, d = x2.shape
    row = lambda i: (i, 0)
    full = lambda shape: pl.BlockSpec(shape, lambda i: (0, 0))
    return pl.pallas_call(
        _merge_kernel,
        grid=(n // tm,),
        in_specs=[
            pl.BlockSpec((tm, CONV_WIDTH), row), pl.BlockSpec((tm, RWKV_WIDTH), row),
            pl.BlockSpec((tm, COL_GATE), row), pl.BlockSpec((tm, d), row),
            full((CONV_WIDTH, d)), full((RWKV_WIDTH, d)), full((d, d)),
            full((1, d)), _mod_spec(layer, 2, seq, tm),
            full((1, d)), _mod_spec(layer, 4, seq, tm), _mod_spec(layer, 3, seq, tm),
            full((N_EXPERTS, d)),
        ],
        out_specs=[pl.BlockSpec((tm, d), row), pl.BlockSpec((tm, ROW_WORDS), row),
                   pl.BlockSpec((N_EXPERTS, tm), lambda i: (0, i))],
        out_shape=[jax.ShapeDtypeStruct((n, d), F32), jax.ShapeDtypeStruct((n, ROW_WORDS), jnp.int32),
                   jax.ShapeDtypeStruct((N_EXPERTS, n), F32)],
        compiler_params=_params("parallel"),
        name="merge",
    )(uc, ur, zg, x2, p_conv_b, p_rwkv_b, w_o_b, g_post, mod, g_pre, mod, mod, w_router_t)


def _route_kernel(lg_ref, b_ref, gate_ref, sel_ref, cnt_ref):
    s = _sigmoid(lg_ref[...])
    biased = s + b_ref[...]
    t = s.shape[1]
    member = lax.broadcasted_iota(jnp.int32, (GROUP_SIZE, t), 0)
    grp = []
    for g in range(N_GROUPS):
        bg = biased[g * GROUP_SIZE:(g + 1) * GROUP_SIZE, :]
        m1 = jnp.max(bg, axis=0, keepdims=True)
        first = jnp.min(jnp.where(bg == m1, member, GROUP_SIZE), axis=0, keepdims=True)
        m2 = jnp.max(jnp.where(member == first, -jnp.inf, bg), axis=0, keepdims=True)
        grp.append(m1 + m2)
    masked = []
    for g in range(N_GROUPS):
        rank = jnp.zeros((1, t), jnp.int32)
        for o in range(N_GROUPS):
            if o == g:
                continue
            ahead = (grp[o] > grp[g]) if o > g else (grp[o] >= grp[g])
            rank = rank + jnp.where(ahead, 1, 0)
        keep = rank < TOPK_GROUPS
        masked.append(jnp.where(keep, biased[g * GROUP_SIZE:(g + 1) * GROUP_SIZE, :], -jnp.inf))
    masked = jnp.concatenate(masked, axis=0)
    eidx = lax.broadcasted_iota(jnp.int32, masked.shape, 0)
    rank = jnp.zeros(masked.shape, jnp.int32)
    for o in range(N_EXPERTS):
        ro = masked[o:o + 1, :]
        rank = rank + jnp.where(ro > masked, 1, jnp.where((ro == masked) & (eidx > o), 1, 0))
    chosen = rank < TOP_K
    sel = jnp.where(chosen, s, 0.0)
    gate_ref[...] = sel / jnp.sum(sel, axis=0, keepdims=True) * ROUTED_SCALE
    mask = jnp.where(chosen, 1.0, 0.0)
    sel_ref[...] = mask.astype(sel_ref.dtype)

    @pl.when(pl.program_id(0) == 0)
    def _():
        cnt_ref[...] = jnp.zeros_like(cnt_ref)

    cnt_ref[...] += jnp.sum(mask, axis=1, keepdims=True)


def _route(logits_t, b_router, tt=512):
    e, n = logits_t.shape
    return pl.pallas_call(
        _route_kernel,
        grid=(n // tt,),
        in_specs=[pl.BlockSpec((e, tt), lambda i: (0, i)), pl.BlockSpec((e, 1), lambda i: (0, 0))],
        out_specs=[pl.BlockSpec((e, tt), lambda i: (0, i)), pl.BlockSpec((e, tt), lambda i: (0, i)),
                   pl.BlockSpec((e, GATE_LANES), lambda i: (0, 0))],
        out_shape=[jax.ShapeDtypeStruct((e, n), F32), jax.ShapeDtypeStruct((e, n), BF16),
                   jax.ShapeDtypeStruct((e, GATE_LANES), F32)],
        compiler_params=_params("arbitrary"),
        name="route",
    )(logits_t, b_router)


def _dest_kernel(sel_ref, gate_ref, start_ref, triu_ref, below_ref, dest_ref, g8_ref, run_ref):
    @pl.when(pl.program_id(0) == 0)
    def _():
        run_ref[...] = jnp.zeros_like(run_ref)

    sel = sel_ref[...]
    t = sel.shape[1]
    self32 = sel.astype(F32)
    incl = _dot(sel, triu_ref[...])
    pos = start_ref[...] + run_ref[...] + incl - self32
    run_ref[...] += incl[:, t - 1:t]
    slot = _dot(below_ref[...], sel)
    gate = gate_ref[...]
    dst, gts = [], []
    for k in range(TOP_K):
        mine = (self32 > 0.0) & (slot == float(k))
        dst.append(jnp.sum(jnp.where(mine, pos, 0.0), axis=0, keepdims=True))
        gts.append(jnp.sum(jnp.where(mine, gate, 0.0), axis=0, keepdims=True))
    dest_ref[...] = jnp.concatenate(dst, axis=0).astype(jnp.int32)
    g8_ref[...] = jnp.concatenate(gts + [jnp.zeros((GATE_LANES - TOP_K, t), F32)], axis=0).T


def _dest(sel, gate, seg_start, tt=512):
    e, n = sel.shape
    triu = jnp.triu(jnp.ones((tt, tt), F32)).astype(BF16)
    below = jnp.tril(jnp.ones((e, e), F32), -1).astype(BF16)
    return pl.pallas_call(
        _dest_kernel,
        grid=(n // tt,),
        in_specs=[pl.BlockSpec((e, tt), lambda i: (0, i)), pl.BlockSpec((e, tt), lambda i: (0, i)),
                  pl.BlockSpec((e, 1), lambda i: (0, 0)), pl.BlockSpec((tt, tt), lambda i: (0, 0)),
                  pl.BlockSpec((e, e), lambda i: (0, 0))],
        out_specs=[pl.BlockSpec((TOP_K, tt), lambda i: (0, i)), pl.BlockSpec((tt, GATE_LANES), lambda i: (i, 0))],
        out_shape=[jax.ShapeDtypeStruct((TOP_K, n), jnp.int32), jax.ShapeDtypeStruct((n, GATE_LANES), F32)],
        scratch_shapes=[pltpu.VMEM((e, 1), F32)],
        compiler_params=_params("arbitrary"),
        name="dest",
    )(sel, gate, seg_start, triu, below)


def _sc_mesh():
    return plsc.VectorSubcoreMesh(core_axis_name="c", subcore_axis_name="s",
                                  num_cores=SC_CORES, num_subcores=SC_SUBCORES)


def _worker_id():
    return lax.axis_index("s") * SC_CORES + lax.axis_index("c")


def _sc_scatter_rows(rows, dest_flat, n_out):
    n, w = rows.shape
    per_worker = n // SC_WORKERS

    def body(rows_hbm, dest_hbm, out_hbm, idx_v, rows_v):
        base = _worker_id() * per_worker

        @pl.loop(0, per_worker // SC_CHUNK)
        def _(c):
            t0 = base + c * SC_CHUNK
            pltpu.sync_copy(rows_hbm.at[pl.ds(t0, SC_CHUNK)], rows_v)
            for k in range(TOP_K):
                pltpu.sync_copy(dest_hbm.at[pl.ds(k * n + t0, SC_CHUNK)], idx_v)
                pltpu.sync_copy(rows_v, out_hbm.at[idx_v])

    return pl.kernel(
        body, out_type=jax.ShapeDtypeStruct((n_out, w), rows.dtype), mesh=_sc_mesh(),
        scratch_types=[pltpu.VMEM((SC_CHUNK,), jnp.int32), pltpu.VMEM((SC_CHUNK, w), rows.dtype)],
        name="sc_scatter",
    )(rows, dest_flat)


def _sc_gather_rows(table, idx_flat):
    m = idx_flat.shape[0]
    w = table.shape[1]
    per_worker = m // SC_WORKERS

    def body(table_hbm, idx_hbm, out_hbm, idx_v, rows_v):
        base = _worker_id() * per_worker

        @pl.loop(0, per_worker // SC_CHUNK)
        def _(c):
            j0 = base + c * SC_CHUNK
            pltpu.sync_copy(idx_hbm.at[pl.ds(j0, SC_CHUNK)], idx_v)
            pltpu.sync_copy(table_hbm.at[idx_v], rows_v)
            pltpu.sync_copy(rows_v, out_hbm.at[pl.ds(j0, SC_CHUNK)])

    return pl.kernel(
        body, out_type=jax.ShapeDtypeStruct((m, w), table.dtype), mesh=_sc_mesh(),
        scratch_types=[pltpu.VMEM((SC_CHUNK,), jnp.int32), pltpu.VMEM((SC_CHUNK, w), table.dtype)],
        name="sc_gather",
    )(table, idx_flat)


def _swiglu(h, wg, wu):
    gte = _dot(h, wg)
    return gte * _sigmoid(gte) * _dot(h, wu)


def _experts_kernel(be_ref, va_ref, xs_ref, wg_ref, wu_ref, wd_ref, ys_ref):
    b = pl.program_id(0)
    valid = va_ref[b]

    @pl.when(valid > 0)
    def _():
        x = _unpack_rows(xs_ref[...])
        row = lax.broadcasted_iota(jnp.int32, x.shape, 0)
        x = jnp.where(row < valid, x, 0.0).astype(BF16)
        act = _swiglu(x, wg_ref[...], wu_ref[...]).astype(BF16)
        ys_ref[...] = _pack_rows(_dot(act, wd_ref[...]))


def _experts(xs, blk_expert, blk_valid, wg, wu, wd):
    n_rows, w = xs.shape
    d = wg.shape[1]
    rows = pl.BlockSpec((EXPERT_BLOCK, w), lambda b, be, va: (b, 0))
    return pl.pallas_call(
        _experts_kernel,
        grid_spec=pltpu.PrefetchScalarGridSpec(
            num_scalar_prefetch=2,
            grid=(n_rows // EXPERT_BLOCK,),
            in_specs=[rows,
                      pl.BlockSpec((None, d, D_EXPERT), lambda b, be, va: (be[b], 0, 0)),
                      pl.BlockSpec((None, d, D_EXPERT), lambda b, be, va: (be[b], 0, 0)),
                      pl.BlockSpec((None, D_EXPERT, d), lambda b, be, va: (be[b], 0, 0))],
            out_specs=rows,
        ),
        out_shape=jax.ShapeDtypeStruct((n_rows, w), xs.dtype),
        compiler_params=_params("arbitrary"),
        name="experts",
    )(blk_expert, blk_valid, xs, wg, wu, wd)


def _combine_kernel(yg_ref, g8_ref, hp_ref, sg_ref, su_ref, sd_ref, x_ref, gpost_ref, gt_ref, o_ref):
    h = _unpack_rows(hp_ref[...]).astype(BF16)
    acc = _dot(_swiglu(h, sg_ref[...], su_ref[...]).astype(BF16), sd_ref[...])
    g8 = g8_ref[...]
    for k in range(TOP_K):
        acc = acc + g8[:, k:k + 1] * _unpack_rows(yg_ref[k])
    o_ref[...] = x_ref[...] + gt_ref[...] * _rms(acc, gpost_ref[...])


def _combine(yg, g8, hp, sg, su, sd, x2, g_post, mod, layer, seq, tm=512):
    n, d = x2.shape
    row = lambda i: (i, 0)
    full = lambda shape: pl.BlockSpec(shape, lambda i: (0, 0))
    return pl.pallas_call(
        _combine_kernel,
        grid=(n // tm,),
        in_specs=[
            pl.BlockSpec((TOP_K, tm, ROW_WORDS), lambda i: (0, i, 0)),
            pl.BlockSpec((tm, GATE_LANES), row),
            pl.BlockSpec((tm, ROW_WORDS), row),
            full((d, D_EXPERT)), full((d, D_EXPERT)), full((D_EXPERT, d)),
            pl.BlockSpec((tm, d), row),
            full((1, d)),
            _mod_spec(layer, 5, seq, tm),
        ],
        out_specs=pl.BlockSpec((tm, d), row),
        out_shape=jax.ShapeDtypeStruct((n, d), F32),
        compiler_params=_params("parallel"),
        name="combine",
    )(yg, g8, hp, sg, su, sd, x2, g_post, mod)


def _moe(hp, logits_t, b_router, wg, wu, wd, sg, su, sd, x2, g_post, mod, layer, seq):
    n = x2.shape[0]
    gate, sel, counts = _route(logits_t, b_router)
    cnt = counts[:, 0].astype(jnp.int32)
    padded = (cnt + EXPERT_BLOCK - 1) // EXPERT_BLOCK * EXPERT_BLOCK
    seg_end = jnp.cumsum(padded)
    seg_start = seg_end - padded
    n_blocks = n * TOP_K // EXPERT_BLOCK + N_EXPERTS
    blk_row = jnp.arange(n_blocks, dtype=jnp.int32) * EXPERT_BLOCK
    blk_expert = jnp.minimum(jnp.searchsorted(seg_end, blk_row, side='right'), N_EXPERTS - 1).astype(jnp.int32)
    blk_valid = jnp.clip(cnt[blk_expert] - (blk_row - seg_start[blk_expert]), 0, EXPERT_BLOCK).astype(jnp.int32)
    dest, g8 = _dest(sel, gate, seg_start.astype(F32).reshape(-1, 1))
    dest_flat = dest.reshape(-1)
    xs = _sc_scatter_rows(hp, dest_flat, n_blocks * EXPERT_BLOCK)
    ys = _experts(xs, blk_expert, blk_valid, wg, wu, wd)
    yg = _sc_gather_rows(ys, dest_flat).reshape(TOP_K, n, ROW_WORDS)
    return _combine(yg, g8, hp, sg, su, sd, x2, g_post, mod, layer, seq)


def kernel(x, c, w_ada, b_ada, norm_mix_pre, norm_mix_post, norm_ffn_pre, norm_ffn_post, w_in, mu_shift, conv_w, conv_b, conv_ln_g, conv_ln_b, p_conv, w0, w2, a0, a2, g2, k_k, k_a, r_k, gn_g, gn_b, p_rwkv, w_o, w_router, b_router, we_gate, we_up, we_down, ws_gate, ws_up, ws_down):
    nb, seq, d = x.shape
    depth = w_ada.shape[0]
    n = nb * seq
    rw = RWKV_WIDTH
    mod = _ada(c, w_ada, b_ada)
    x2 = x.reshape(n, d)
    row = lambda a: a.reshape(1, -1)
    bf = lambda a: a.astype(BF16)
    for l in range(depth):
        zc, zr, zg = _win(x2, row(norm_mix_pre[l]), mod, l, bf(w_in[l]), seq)
        uc = _conv(zc, conv_w[l], row(conv_b[l]), row(conv_ln_g[l]), row(conv_ln_b[l]), nb, seq)
        wa = jnp.zeros((LORA_W + LORA_A, 2 * rw), F32)
        wa = wa.at[:LORA_W, :rw].set(w2[l]).at[LORA_W:, rw:].set(a2[l])
        prep_out = _prep(zr, row(mu_shift[l]), row(w0[l]), row(a0[l]), wa, g2[l], row(k_k[l]),
                         row(k_a[l]), row(r_k[l]), nb, seq)
        ur = _scan(prep_out, row(gn_g[l]), row(gn_b[l]), nb, seq)
        x2, hp, logits_t = _merge(uc, ur, zg, x2, bf(p_conv[l]), bf(p_rwkv[l]), bf(w_o[l]),
                                  row(norm_mix_post[l]), row(norm_ffn_pre[l]), mod, l, w_router[l].T, seq)
        x2 = _moe(hp, logits_t, b_router[l].reshape(-1, 1), bf(we_gate[l]), bf(we_up[l]), bf(we_down[l]),
                  bf(ws_gate[l]), bf(ws_up[l]), bf(ws_down[l]), x2, row(norm_ffn_post[l]), mod, l, seq)
    return x2.reshape(nb, seq, d)
```

```python
import functools

import jax
import jax.numpy as jnp
from jax import lax
from jax.experimental import pallas as pl
from jax.experimental.pallas import tpu as pltpu
from jax.experimental.pallas import tpu_sc as plsc

F32 = jnp.float32
BF16 = jnp.bfloat16
HI = lax.Precision.HIGHEST

D_MODEL = 1024
CONV_WIDTH = 512
CONV_KERNEL = 31
RWKV_WIDTH = 512
HEAD = 64
N_HEADS = RWKV_WIDTH // HEAD
LORA_W = 64
LORA_A = 64
LORA_G = 128
N_EXPERTS = 64
TOP_K = 8
N_GROUPS = 8
TOPK_GROUPS = 4
GROUP_SIZE = N_EXPERTS // N_GROUPS
D_EXPERT = 256
ROUTED_SCALE = 2.5
RMS_EPS = 1e-6
LN_EPS = 1e-5
GN_EPS = 64e-5
COL_CONV = 2 * CONV_WIDTH
COL_RWKV = 3 * RWKV_WIDTH + LORA_W + LORA_A + LORA_G
COL_GATE = 2 * D_MODEL
D_IN = COL_CONV + COL_RWKV + COL_GATE

CHUNK = 64
QUAD = 4 * HEAD
N_QUADS = RWKV_WIDTH // QUAD
CUM_BLOCK = 256
CONV_HALO = 32
SUBLANES = 8
SHIFT_HALO = SUBLANES
GATE_LANES = 128
EXPERT_BLOCK = 256
BLOCKS_PER_STEP = 4
ROW_WORDS = D_MODEL // 2
SC_CORES = 2
SC_SUBCORES = 16
SC_WORKERS = SC_CORES * SC_SUBCORES
SC_CHUNK = 128
VMEM_LIMIT = 52 * 1024 * 1024


def _params(*sem):
    return pltpu.CompilerParams(dimension_semantics=sem, vmem_limit_bytes=VMEM_LIMIT)


def _sigmoid(x):
    return 1.0 / (1.0 + jnp.exp(-x))


def _softplus(x):
    return jnp.maximum(x, 0.0) + jnp.log(1.0 + jnp.exp(-jnp.abs(x)))


def _rms(x, g):
    return x * lax.rsqrt(jnp.mean(x * x, axis=-1, keepdims=True) + RMS_EPS) * g


def _split(x):
    hi = x.astype(BF16)
    return hi, (x - hi.astype(F32)).astype(BF16)


def _dot(a, b):
    return jnp.dot(a, b, preferred_element_type=F32)


def _dot_exact_rhs(x, w):
    hi, lo = _split(x)
    return _dot(hi, w) + _dot(lo, w)


def _pack_rows(y):
    half = y.shape[1] // 2
    hi = lax.bitcast_convert_type(y[:, :half].astype(BF16).astype(F32), jnp.uint32)
    lo = lax.bitcast_convert_type(y[:, half:].astype(BF16).astype(F32), jnp.uint32)
    return lax.bitcast_convert_type((hi & jnp.uint32(0xFFFF0000)) | (lo >> 16), jnp.int32)


def _unpack_rows(u):
    u = lax.bitcast_convert_type(u, jnp.uint32)
    hi = lax.bitcast_convert_type(u & jnp.uint32(0xFFFF0000), F32)
    lo = lax.bitcast_convert_type(u << 16, F32)
    return jnp.concatenate([hi, lo], axis=1)


def _dot_exact_lhs(w, x):
    hi, lo = _split(x)
    return _dot(w, hi) + _dot(w, lo)


def _ada_kernel(c_ref, w_ref, b_ref, o_ref):
    c = c_ref[...]
    o_ref[...] = jnp.dot(c * _sigmoid(c), w_ref[...], precision=HI,
                         preferred_element_type=F32) + b_ref[...]


def _ada(c, w_ada, b_ada):
    nl, d, _ = w_ada.shape
    nb = c.shape[0]
    out = pl.pallas_call(
        _ada_kernel,
        grid=(nl, 6),
        in_specs=[
            pl.BlockSpec((nb, d), lambda l, k: (0, 0)),
            pl.BlockSpec((None, d, d), lambda l, k: (l, 0, k)),
            pl.BlockSpec((None, None, 1, d), lambda l, k: (l, k, 0, 0)),
        ],
        out_specs=pl.BlockSpec((None, None, nb, d), lambda l, k: (l, k, 0, 0)),
        out_shape=jax.ShapeDtypeStruct((nl, 6, nb, d), F32),
        compiler_params=_params("parallel", "parallel"),
        name="ada",
    )(c, w_ada, b_ada.reshape(nl, 6, 1, d))
    return out.reshape(nl, 6, nb, 1, d)


def _mod_spec(layer, piece, rows_per_batch, tm):
    return pl.BlockSpec((None, None, None, 1, D_MODEL),
                        lambda i, *_: (layer, piece, (i * tm) // rows_per_batch, 0, 0))


def _win_kernel(x_ref, g_ref, sc_ref, sh_ref, w_ref, zc_ref, zr_ref, zg_ref):
    h = _rms(x_ref[...], g_ref[...]) * (1.0 + sc_ref[...]) + sh_ref[...]
    hb = h.astype(BF16)
    zc_ref[...] = _dot(hb, w_ref[:, :COL_CONV]).astype(zc_ref.dtype)
    zr_ref[...] = _dot(hb, w_ref[:, COL_CONV:COL_CONV + COL_RWKV])
    zg_ref[...] = _dot(hb, w_ref[:, COL_CONV + COL_RWKV:]).astype(zg_ref.dtype)


def _win(x2, g, mod, layer, w_in_b, seq, tm=512):
    n, d = x2.shape
    row = lambda i: (i, 0)
    return pl.pallas_call(
        _win_kernel,
        grid=(n // tm,),
        in_specs=[
            pl.BlockSpec((tm, d), row),
            pl.BlockSpec((1, d), lambda i: (0, 0)),
            _mod_spec(layer, 1, seq, tm),
            _mod_spec(layer, 0, seq, tm),
            pl.BlockSpec((d, D_IN), lambda i: (0, 0)),
        ],
        out_specs=[pl.BlockSpec((tm, COL_CONV), row), pl.BlockSpec((tm, COL_RWKV), row),
                   pl.BlockSpec((tm, COL_GATE), row)],
        out_shape=[jax.ShapeDtypeStruct((n, COL_CONV), BF16), jax.ShapeDtypeStruct((n, COL_RWKV), F32),
                   jax.ShapeDtypeStruct((n, COL_GATE), BF16)],
        compiler_params=_params("parallel"),
        name="win",
    )(x2, g, mod, mod, w_in_b)


def _conv_kernel(z_ref, halo_ref, w_ref, cb_ref, g_ref, b_ref, o_ref, ubuf, *, ts):
    j = pl.program_id(1)
    z = z_ref[...].astype(F32)
    zh = halo_ref[...].astype(F32)
    uh = zh[:, :CONV_WIDTH] * _sigmoid(zh[:, CONV_WIDTH:])
    rows = CONV_HALO + ts
    u = jnp.concatenate([jnp.where(j > 0, uh, 0.0),
                         z[:, :CONV_WIDTH] * _sigmoid(z[:, CONV_WIDTH:])], axis=0)
    ubuf[0] = u
    for s in range(1, SUBLANES):
        ubuf[s] = pltpu.roll(u, rows - s, axis=0)
    sub = 64
    first = CONV_HALO - (CONV_KERNEL - 1)
    for r in range(ts // sub):
        acc = jnp.zeros((sub, CONV_WIDTH), F32) + cb_ref[...]
        for k in range(CONV_KERNEL):
            s = (first + k) % SUBLANES
            base = r * sub + first + k - s
            acc = acc + w_ref[k:k + 1, :] * ubuf[s, base:base + sub, :]
        mu = jnp.mean(acc, axis=-1, keepdims=True)
        dlt = acc - mu
        var = jnp.mean(dlt * dlt, axis=-1, keepdims=True)
        y = dlt * lax.rsqrt(var + LN_EPS) * g_ref[...] + b_ref[...]
        o_ref[r * sub:(r + 1) * sub, :] = (y * _sigmoid(y)).astype(o_ref.dtype)


def _conv(zc, conv_w, conv_b, ln_g, ln_b, nb, seq, ts=256):
    n = zc.shape[0]
    tps = seq // ts
    hpt = ts // CONV_HALO
    vec = pl.BlockSpec((1, CONV_WIDTH), lambda b, j: (0, 0))
    return pl.pallas_call(
        functools.partial(_conv_kernel, ts=ts),
        grid=(nb, tps),
        in_specs=[
            pl.BlockSpec((ts, COL_CONV), lambda b, j: (b * tps + j, 0)),
            pl.BlockSpec((CONV_HALO, COL_CONV), lambda b, j: (jnp.maximum((b * tps + j) * hpt - 1, 0), 0)),
            pl.BlockSpec((CONV_KERNEL, CONV_WIDTH), lambda b, j: (0, 0)),
            vec, vec, vec,
        ],
        out_specs=pl.BlockSpec((ts, CONV_WIDTH), lambda b, j: (b * tps + j, 0)),
        out_shape=jax.ShapeDtypeStruct((n, CONV_WIDTH), BF16),
        scratch_shapes=[pltpu.VMEM((SUBLANES, CONV_HALO + ts, CONV_WIDTH), F32)],
        compiler_params=_params("parallel", "parallel"),
        name="conv",
    )(zc, zc, conv_w, conv_b, ln_g, ln_b)


def _prep_kernel(z_ref, halo_ref, mu_ref, w0_ref, a0_ref, wah_ref, wal_ref, g2_ref, kk_ref, ka_ref, rk_ref,
                 ones_ref, tri_ref, csel_ref,
                 at_ref, bt_ref, kt_ref, rt_ref, v_ref, bts_ref, kts_ref, bonus_ref, g_ref, pc_ref):
    j = pl.program_id(1)
    z = z_ref[...]
    ts = z.shape[0]
    last = jnp.where(j > 0, halo_ref[SHIFT_HALO - 1:SHIFT_HALO, :], 0.0)
    row = lax.broadcasted_iota(jnp.int32, z.shape, 0)
    prev = jnp.where(row == 0, last, pltpu.roll(z, 1, axis=0))
    zs = z + (prev - z) * mu_ref[...]
    rw = RWKV_WIDTH
    r = zs[:, :rw]
    k = zs[:, rw:2 * rw]
    v = zs[:, 2 * rw:3 * rw]
    wad = zs[:, 3 * rw:3 * rw + LORA_W + LORA_A]
    gd = zs[:, 3 * rw + LORA_W + LORA_A:]
    lane = lax.broadcasted_iota(jnp.int32, wad.shape, 1)
    xh, xl = _split(jnp.where(lane < LORA_W, jnp.tanh(wad), wad))
    wah = wah_ref[...]
    lora = _dot(xh, wah) + _dot(xl, wah) + _dot(xh, wal_ref[...])
    w = -_softplus(-(w0_ref[...] + lora[:, :rw])) - 0.5
    lw = -jnp.exp(w)
    a = _sigmoid(a0_ref[...] + lora[:, rw:])
    g_ref[...] = _dot(_sigmoid(gd).astype(BF16), g2_ref[...]).astype(g_ref.dtype)
    kk = k * kk_ref[...]
    k2 = k * (1.0 + (a - 1.0) * ka_ref[...])
    ones = ones_ref[...]
    kk = kk * lax.rsqrt(jnp.maximum(_dot_exact_rhs(kk * kk, ones), 1e-24))
    bonus_ref[...] = (_dot_exact_rhs(r * k2 * rk_ref[...], ones) * v).astype(bonus_ref.dtype)
    tri = tri_ref[...]
    lc, tot = [], []
    for h in range(ts // CUM_BLOCK):
        cs = _dot_exact_lhs(tri, lw[h * CUM_BLOCK:(h + 1) * CUM_BLOCK])
        lc.append(cs[:CUM_BLOCK])
        tot.append(cs[CUM_BLOCK:])
    lc = jnp.concatenate(lc, axis=0)
    tot = jnp.concatenate(tot, axis=0)
    e_neg = jnp.exp(-lc)
    e_end = jnp.exp(tot - lc)
    kka = kk * a
    dt = at_ref.dtype
    at_ref[...] = (-kk * jnp.exp(lc - lw)).astype(dt)
    bt_ref[...] = (kka * e_neg).astype(dt)
    kt_ref[...] = (k2 * e_neg).astype(dt)
    rt_ref[...] = (r * jnp.exp(lc)).astype(dt)
    v_ref[...] = v.astype(dt)
    bts_ref[...] = (kka * e_end).astype(dt)
    kts_ref[...] = (k2 * e_end).astype(dt)
    pc_ref[...] = jnp.exp(_dot_exact_lhs(csel_ref[...], lw))


def _prep(zr, mu, w0, a0, wa, g2, k_k, k_a, r_k, nb, seq, ts=512):
    n = zr.shape[0]
    tps = seq // ts
    hpt = ts // SHIFT_HALO
    rw = RWKV_WIDTH
    cpt = ts // CHUNK
    ones = jnp.kron(jnp.eye(N_HEADS, dtype=F32), jnp.ones((HEAD, HEAD), F32)).astype(BF16)
    blk = jnp.kron(jnp.eye(CUM_BLOCK // CHUNK, dtype=F32), jnp.ones((CHUNK, CHUNK), F32))
    tri = jnp.concatenate([jnp.tril(blk), blk], axis=0).astype(BF16)
    csel = jnp.kron(jnp.eye(cpt, dtype=F32), jnp.ones((1, CHUNK), F32)).astype(BF16)
    wa_hi, wa_lo = _split(wa)
    vec = pl.BlockSpec((1, rw), lambda b, j: (0, 0))
    full = lambda shape: pl.BlockSpec(shape, lambda b, j: (0, 0))
    rows = pl.BlockSpec((ts, rw), lambda b, j: (b * tps + j, 0))
    return pl.pallas_call(
        _prep_kernel,
        grid=(nb, tps),
        in_specs=[
            pl.BlockSpec((ts, COL_RWKV), lambda b, j: (b * tps + j, 0)),
            pl.BlockSpec((SHIFT_HALO, COL_RWKV), lambda b, j: (jnp.maximum((b * tps + j) * hpt - 1, 0), 0)),
            full((1, COL_RWKV)), vec, vec,
            full((LORA_W + LORA_A, 2 * rw)), full((LORA_W + LORA_A, 2 * rw)), full((LORA_G, rw)),
            vec, vec, vec, full((rw, rw)), full((2 * CUM_BLOCK, CUM_BLOCK)), full((cpt, ts)),
        ],
        out_specs=[rows] * 9 + [pl.BlockSpec((cpt, rw), lambda b, j: (b * tps + j, 0))],
        out_shape=[jax.ShapeDtypeStruct((n, rw), BF16)] * 9 + [jax.ShapeDtypeStruct((n // CHUNK, rw), F32)],
        compiler_params=_params("parallel", "parallel"),
        name="prep",
    )(zr, zr, mu, w0, a0, wa_hi, wa_lo, g2.astype(BF16), k_k, k_a, r_k, ones, tri, csel)


def _mm(a, b):
    return jnp.dot(a.astype(BF16), b.astype(BF16), preferred_element_type=F32)


def _mm_nt(a, b):
    return lax.dot_general(a.astype(BF16), b.astype(BF16), (((1,), (1,)), ((), ())),
                           preferred_element_type=F32)


def _mm_tn(a, b):
    return lax.dot_general(a.astype(BF16), b.astype(BF16), (((0,), (0,)), ((), ())),
                           preferred_element_type=F32)


def _scan_kernel(at_ref, bt_ref, kt_ref, rt_ref, v_ref, bts_ref, kts_ref, bonus_ref, g_ref, pc_ref,
                 gng_ref, gnb_ref, ones_ref, o_ref, s_ref, y_ref, *, ts, gb):
    @pl.when(pl.program_id(1) == 0)
    def _():
        s_ref[...] = jnp.zeros_like(s_ref)

    row_q = lax.broadcasted_iota(jnp.int32, (QUAD, QUAD), 0)
    col_q = lax.broadcasted_iota(jnp.int32, (QUAD, QUAD), 1)
    bdmask = (row_q // HEAD) == (col_q // HEAD)
    row_c = lax.broadcasted_iota(jnp.int32, (CHUNK, QUAD), 0)
    col_c = lax.broadcasted_iota(jnp.int32, (CHUNK, QUAD), 1) % CHUNK
    strict = col_c < row_c
    incl = col_c <= row_c
    eye = jnp.where(col_c == row_c, 1.0, 0.0).astype(F32)

    def bd(x):
        xb = x.astype(BF16)
        return jnp.where(bdmask, jnp.concatenate([xb] * (QUAD // CHUNK), axis=0), jnp.zeros((), BF16))

    chains = [(b, qd) for b in range(gb) for qd in range(N_QUADS)]

    def chunk(c, carry):
        sl = pl.ds(pl.multiple_of(c * CHUNK, CHUNK), CHUNK)
        lanes = [slice(qd * QUAD, (qd + 1) * QUAD) for _, qd in chains]
        ld = lambda ref: [ref[b, sl, ln] for (b, _), ln in zip(chains, lanes)]
        at, bt, kt, rt, v, bts, kts = (ld(r) for r in (at_ref, bt_ref, kt_ref, rt_ref, v_ref, bts_ref, kts_ref))
        each = range(len(chains))
        ar = [jnp.concatenate([at[i], rt[i]], axis=0) for i in each]
        pb = [_mm_nt(ar[i], bd(bt[i])) for i in each]
        pk = [_mm_nt(ar[i], bd(kt[i])) for i in each]
        l_ab = [jnp.where(strict, p[:CHUNK], 0.0) for p in pb]
        a_rb = [jnp.where(incl, p[CHUNK:], 0.0) for p in pb]
        l_ak = [jnp.where(strict, p[:CHUNK], 0.0) for p in pk]
        a_rk = [jnp.where(incl, p[CHUNK:], 0.0) for p in pk]
        tm = [eye + l for l in l_ab]
        lp = [_mm(l, bd(l)) for l in l_ab]
        for _ in range(4):
            both = [_mm(jnp.concatenate([tm[i], lp[i]], axis=0), bd(lp[i])) for i in each]
            tm = [tm[i] + both[i][:CHUNK] for i in each]
            lp = [both[i][CHUNK:] for i in each]
        tm = [tm[i] + _mm(tm[i], bd(lp[i])) for i in each]
        bdv = [bd(x) for x in v]
        x0 = [_mm(l_ak[i], bdv[i]) for i in each]
        tu = [_mm(tm[i], jnp.concatenate([bd(x0[i]), bd(at[i])], axis=1)) for i in each]
        u0 = [t[:, :QUAD] for t in tu]
        w = [t[:, QUAD:] for t in tu]
        qy = [_mm(a_rb[i], jnp.concatenate([bd(w[i]), bd(u0[i])], axis=1)) for i in each]
        y0 = [qy[i][:, QUAD:] + _mm(a_rk[i], bdv[i]) for i in each]
        q = [rt[i].astype(F32) + qy[i][:, :QUAD] for i in each]
        s = [s_ref[i] for i in each]
        qw_s = [_mm_nt(jnp.concatenate([q[i], w[i]], axis=0), s[i]) for i in each]
        for i, ((b, _), ln) in enumerate(zip(chains, lanes)):
            y_ref[b, sl, ln] = y0[i] + qw_s[i][:CHUNK]
        u = [(u0[i] + qw_s[i][CHUNK:]).astype(BF16) for i in each]
        n_p = [jnp.where(bdmask, _mm_tn(jnp.concatenate([u[i], v[i]], axis=0),
                                        jnp.concatenate([bts[i], kts[i]], axis=0)), 0.0) for i in each]
        for i, ((b, _), ln) in enumerate(zip(chains, lanes)):
            s_ref[i] = s[i] * pc_ref[b, pl.ds(c, 1), ln] + n_p[i]
        return carry

    lax.fori_loop(0, ts // CHUNK, chunk, 0)

    ones = ones_ref[...]
    for b in range(gb):
        for qd in range(N_QUADS):
            ln = slice(qd * QUAD, (qd + 1) * QUAD)
            y = y_ref[b, :, ln]
            dlt = y - _dot_exact_rhs(y, ones) * (1.0 / HEAD)
            var = _dot_exact_rhs(dlt * dlt, ones) * (1.0 / HEAD)
            yn = dlt * lax.rsqrt(var + GN_EPS) * gng_ref[:, ln] + gnb_ref[:, ln]
            o_ref[b, :, ln] = ((yn + bonus_ref[b, :, ln].astype(F32)) * g_ref[b, :, ln].astype(F32)).astype(o_ref.dtype)


def _scan(prep_out, gn_g, gn_b, nb, seq, ts=256, gb=4):
    n = prep_out[0].shape[0]
    rw = RWKV_WIDTH
    cpt = ts // CHUNK
    ones = jnp.kron(jnp.eye(QUAD // HEAD, dtype=F32), jnp.ones((HEAD, HEAD), F32)).astype(BF16)
    rows = pl.BlockSpec((gb, ts, rw), lambda i, t: (i, t, 0))
    vec = pl.BlockSpec((1, rw), lambda i, t: (0, 0))
    args = [a.reshape(nb, seq, rw) for a in prep_out[:9]] + [prep_out[9].reshape(nb, seq // ts, cpt, rw)]
    out = pl.pallas_call(
        functools.partial(_scan_kernel, ts=ts, gb=gb),
        grid=(nb // gb, seq // ts),
        in_specs=[rows] * 9 + [pl.BlockSpec((gb, None, cpt, rw), lambda i, t: (i, t, 0, 0)), vec, vec,
                               pl.BlockSpec((QUAD, QUAD), lambda i, t: (0, 0))],
        out_specs=rows,
        out_shape=jax.ShapeDtypeStruct((nb, seq, rw), BF16),
        scratch_shapes=[pltpu.VMEM((gb * N_QUADS, QUAD, QUAD), F32), pltpu.VMEM((gb, ts, rw), F32)],
        compiler_params=_params("parallel", "arbitrary"),
        name="scan",
    )(*args, gn_g, gn_b, ones)
    return out.reshape(n, rw)


def _merge_kernel(uc_ref, ur_ref, zg_ref, x_ref, pc_ref, pr_ref, wo_ref, gpost_ref, gt_ref,
                  gpre_ref, sc_ref, sh_ref, wrt_ref, xo_ref, h_ref, lg_ref):
    yc = _dot(uc_ref[...], pc_ref[...])
    yr = _dot(ur_ref[...], pr_ref[...])
    zg = zg_ref[...].astype(F32)
    m = _sigmoid(zg[:, :D_MODEL]) * yc + _sigmoid(zg[:, D_MODEL:]) * yr
    y = _dot(m.astype(BF16), wo_ref[...])
    xn = x_ref[...] + gt_ref[...] * _rms(y, gpost_ref[...])
    xo_ref[...] = xn
    h = _rms(xn, gpre_ref[...]) * (1.0 + sc_ref[...]) + sh_ref[...]
    h_ref[...] = _pack_rows(h)
    lg_ref[...] = lax.dot_general(wrt_ref[...], h, (((1,), (1,)), ((), ())), precision=HI,
                                  preferred_element_type=F32)


def _merge(uc, ur, zg, x2, p_conv_b, p_rwkv_b, w_o_b, g_post, g_pre, mod, layer, w_router_t, seq, tm=512):
    n, d = x2.shape
    row = lambda i: (i, 0)
    full = lambda shape: pl.BlockSpec(shape, lambda i: (0, 0))
    return pl.pallas_call(
        _merge_kernel,
        grid=(n // tm,),
        in_specs=[
            pl.BlockSpec((tm, CONV_WIDTH), row), pl.BlockSpec((tm, RWKV_WIDTH), row),
            pl.BlockSpec((tm, COL_GATE), row), pl.BlockSpec((tm, d), row),
            full((CONV_WIDTH, d)), full((RWKV_WIDTH, d)), full((d, d)),
            full((1, d)), _mod_spec(layer, 2, seq, tm),
            full((1, d)), _mod_spec(layer, 4, seq, tm), _mod_spec(layer, 3, seq, tm),
            full((N_EXPERTS, d)),
        ],
        out_specs=[pl.BlockSpec((tm, d), row), pl.BlockSpec((tm, ROW_WORDS), row),
                   pl.BlockSpec((N_EXPERTS, tm), lambda i: (0, i))],
        out_shape=[jax.ShapeDtypeStruct((n, d), F32), jax.ShapeDtypeStruct((n, ROW_WORDS), jnp.int32),
                   jax.ShapeDtypeStruct((N_EXPERTS, n), F32)],
        compiler_params=_params("parallel"),
        name="merge",
    )(uc, ur, zg, x2, p_conv_b, p_rwkv_b, w_o_b, g_post, mod, g_pre, mod, mod, w_router_t)


def _route_kernel(lg_ref, b_ref, gate_ref, sel_ref, cnt_ref):
    s = _sigmoid(lg_ref[...])
    biased = s + b_ref[...]
    t = s.shape[1]
    member = lax.broadcasted_iota(jnp.int32, (GROUP_SIZE, t), 0)
    grp = []
    for g in range(N_GROUPS):
        bg = biased[g * GROUP_SIZE:(g + 1) * GROUP_SIZE, :]
        m1 = jnp.max(bg, axis=0, keepdims=True)
        first = jnp.min(jnp.where(bg == m1, member, GROUP_SIZE), axis=0, keepdims=True)
        m2 = jnp.max(jnp.where(member == first, -jnp.inf, bg), axis=0, keepdims=True)
        grp.append(m1 + m2)
    masked = []
    for g in range(N_GROUPS):
        rank = jnp.zeros((1, t), jnp.int32)
        for o in range(N_GROUPS):
            if o == g:
                continue
            ahead = (grp[o] > grp[g]) if o > g else (grp[o] >= grp[g])
            rank = rank + jnp.where(ahead, 1, 0)
        keep = rank < TOPK_GROUPS
        masked.append(jnp.where(keep, biased[g * GROUP_SIZE:(g + 1) * GROUP_SIZE, :], -jnp.inf))
    masked = jnp.concatenate(masked, axis=0)
    eidx = lax.broadcasted_iota(jnp.int32, masked.shape, 0)
    rank = jnp.zeros(masked.shape, jnp.int32)
    for o in range(N_EXPERTS):
        ro = masked[o:o + 1, :]
        rank = rank + jnp.where(ro > masked, 1, jnp.where((ro == masked) & (eidx > o), 1, 0))
    chosen = rank < TOP_K
    sel = jnp.where(chosen, s, 0.0)
    gate_ref[...] = sel / jnp.sum(sel, axis=0, keepdims=True) * ROUTED_SCALE
    mask = jnp.where(chosen, 1.0, 0.0)
    sel_ref[...] = mask.astype(sel_ref.dtype)

    @pl.when(pl.program_id(0) == 0)
    def _():
        cnt_ref[...] = jnp.zeros_like(cnt_ref)

    cnt_ref[...] += jnp.sum(mask, axis=1, keepdims=True)


def _route(logits_t, b_router, tt=512):
    e, n = logits_t.shape
    return pl.pallas_call(
        _route_kernel,
        grid=(n // tt,),
        in_specs=[pl.BlockSpec((e, tt), lambda i: (0, i)), pl.BlockSpec((e, 1), lambda i: (0, 0))],
        out_specs=[pl.BlockSpec((e, tt), lambda i: (0, i)), pl.BlockSpec((e, tt), lambda i: (0, i)),
                   pl.BlockSpec((e, GATE_LANES), lambda i: (0, 0))],
        out_shape=[jax.ShapeDtypeStruct((e, n), F32), jax.ShapeDtypeStruct((e, n), BF16),
                   jax.ShapeDtypeStruct((e, GATE_LANES), F32)],
        compiler_params=_params("arbitrary"),
        name="route",
    )(logits_t, b_router)


def _dest_kernel(sel_ref, gate_ref, start_ref, triu_ref, below_ref, dest_ref, g8_ref, run_ref):
    @pl.when(pl.program_id(0) == 0)
    def _():
        run_ref[...] = jnp.zeros_like(run_ref)

    sel = sel_ref[...]
    t = sel.shape[1]
    self32 = sel.astype(F32)
    incl = _dot(sel, triu_ref[...])
    pos = start_ref[...] + run_ref[...] + incl - self32
    run_ref[...] += incl[:, t - 1:t]
    slot = _dot(below_ref[...], sel)
    gate = gate_ref[...]
    dst, gts = [], []
    for k in range(TOP_K):
        mine = (self32 > 0.0) & (slot == float(k))
        dst.append(jnp.sum(jnp.where(mine, pos, 0.0), axis=0, keepdims=True))
        gts.append(jnp.sum(jnp.where(mine, gate, 0.0), axis=0, keepdims=True))
    dest_ref[...] = jnp.concatenate(dst, axis=0).astype(jnp.int32)
    g8_ref[...] = jnp.concatenate(gts + [jnp.zeros((GATE_LANES - TOP_K, t), F32)], axis=0).T


def _dest(sel, gate, seg_start, tt=512):
    e, n = sel.shape
    triu = jnp.triu(jnp.ones((tt, tt), F32)).astype(BF16)
    below = jnp.tril(jnp.ones((e, e), F32), -1).astype(BF16)
    return pl.pallas_call(
        _dest_kernel,
        grid=(n // tt,),
        in_specs=[pl.BlockSpec((e, tt), lambda i: (0, i)), pl.BlockSpec((e, tt), lambda i: (0, i)),
                  pl.BlockSpec((e, 1), lambda i: (0, 0)), pl.BlockSpec((tt, tt), lambda i: (0, 0)),
                  pl.BlockSpec((e, e), lambda i: (0, 0))],
        out_specs=[pl.BlockSpec((TOP_K, tt), lambda i: (0, i)), pl.BlockSpec((tt, GATE_LANES), lambda i: (i, 0))],
        out_shape=[jax.ShapeDtypeStruct((TOP_K, n), jnp.int32), jax.ShapeDtypeStruct((n, GATE_LANES), F32)],
        scratch_shapes=[pltpu.VMEM((e, 1), F32)],
        compiler_params=_params("arbitrary"),
        name="dest",
    )(sel, gate, seg_start, triu, below)


def _sc_mesh():
    return plsc.VectorSubcoreMesh(core_axis_name="c", subcore_axis_name="s",
                                  num_cores=SC_CORES, num_subcores=SC_SUBCORES)


def _worker_id():
    return lax.axis_index("s") * SC_CORES + lax.axis_index("c")


def _sc_scatter_rows(rows, dest_flat, n_out):
    n, w = rows.shape
    per_worker = n // SC_WORKERS

    def body(rows_hbm, dest_hbm, out_hbm, idx_v, rows_v):
        base = _worker_id() * per_worker

        @pl.loop(0, per_worker // SC_CHUNK)
        def _(c):
            t0 = base + c * SC_CHUNK
            pltpu.sync_copy(rows_hbm.at[pl.ds(t0, SC_CHUNK)], rows_v)
            for k in range(TOP_K):
                pltpu.sync_copy(dest_hbm.at[pl.ds(k * n + t0, SC_CHUNK)], idx_v)
                pltpu.sync_copy(rows_v, out_hbm.at[idx_v])

    return pl.kernel(
        body, out_type=jax.ShapeDtypeStruct((n_out, w), rows.dtype), mesh=_sc_mesh(),
        scratch_types=[pltpu.VMEM((SC_CHUNK,), jnp.int32), pltpu.VMEM((SC_CHUNK, w), rows.dtype)],
        name="sc_scatter",
    )(rows, dest_flat)


def _sc_gather_rows(table, idx_flat):
    m = idx_flat.shape[0]
    w = table.shape[1]
    per_worker = m // SC_WORKERS

    def body(table_hbm, idx_hbm, out_hbm, idx_v, rows_v):
        base = _worker_id() * per_worker

        @pl.loop(0, per_worker // SC_CHUNK)
        def _(c):
            j0 = base + c * SC_CHUNK
            pltpu.sync_copy(idx_hbm.at[pl.ds(j0, SC_CHUNK)], idx_v)
            pltpu.sync_copy(table_hbm.at[idx_v], rows_v)
            pltpu.sync_copy(rows_v, out_hbm.at[pl.ds(j0, SC_CHUNK)])

    return pl.kernel(
        body, out_type=jax.ShapeDtypeStruct((m, w), table.dtype), mesh=_sc_mesh(),
        scratch_types=[pltpu.VMEM((SC_CHUNK,), jnp.int32), pltpu.VMEM((SC_CHUNK, w), table.dtype)],
        name="sc_gather",
    )(table, idx_flat)


def _swiglu(h, wg, wu):
    gte = _dot(h, wg)
    return gte * _sigmoid(gte) * _dot(h, wu)


def _experts_kernel(be_ref, va_ref, xs_ref, *refs):
    nb = BLOCKS_PER_STEP
    wg, wu, wd, ys_ref = refs[:nb], refs[nb:2 * nb], refs[2 * nb:3 * nb], refs[3 * nb]
    first = pl.program_id(0) * nb

    @pl.when(va_ref[first] > 0)
    def _():
        row = lax.broadcasted_iota(jnp.int32, (EXPERT_BLOCK, D_MODEL), 0)
        x = []
        for i in range(nb):
            xi = _unpack_rows(xs_ref[i * EXPERT_BLOCK:(i + 1) * EXPERT_BLOCK, :])
            x.append(jnp.where(row < va_ref[first + i], xi, 0.0).astype(BF16))
        gte = [_dot(x[i], wg[i][...]) for i in range(nb)]
        up = [_dot(x[i], wu[i][...]) for i in range(nb)]
        act = [(gte[i] * _sigmoid(gte[i]) * up[i]).astype(BF16) for i in range(nb)]
        y = [_dot(act[i], wd[i][...]) for i in range(nb)]
        for i in range(nb):
            ys_ref[i * EXPERT_BLOCK:(i + 1) * EXPERT_BLOCK, :] = _pack_rows(y[i])


def _experts(xs, blk_expert, blk_valid, wg, wu, wd, layer):
    n_rows, w = xs.shape
    d = wg.shape[2]
    nb = BLOCKS_PER_STEP
    rows = pl.BlockSpec((nb * EXPERT_BLOCK, w), lambda s, be, va: (s, 0))
    up_spec = lambda i: pl.BlockSpec((None, None, d, D_EXPERT), lambda s, be, va: (layer, be[s * nb + i], 0, 0))
    down_spec = lambda i: pl.BlockSpec((None, None, D_EXPERT, d), lambda s, be, va: (layer, be[s * nb + i], 0, 0))
    return pl.pallas_call(
        _experts_kernel,
        grid_spec=pltpu.PrefetchScalarGridSpec(
            num_scalar_prefetch=2,
            grid=(n_rows // (nb * EXPERT_BLOCK),),
            in_specs=[rows] + [up_spec(i) for i in range(nb)] * 2 + [down_spec(i) for i in range(nb)],
            out_specs=rows,
        ),
        out_shape=jax.ShapeDtypeStruct((n_rows, w), xs.dtype),
        compiler_params=_params("arbitrary"),
        name="experts",
    )(blk_expert, blk_valid, xs, *([wg] * nb), *([wu] * nb), *([wd] * nb))


def _combine_kernel(yg_ref, g8_ref, hp_ref, sg_ref, su_ref, sd_ref, x_ref, gpost_ref, gt_ref, o_ref):
    h = _unpack_rows(hp_ref[...]).astype(BF16)
    acc = _dot(_swiglu(h, sg_ref[...], su_ref[...]).astype(BF16), sd_ref[...])
    g8 = g8_ref[...]
    for k in range(TOP_K):
        acc = acc + g8[:, k:k + 1] * _unpack_rows(yg_ref[k])
    o_ref[...] = x_ref[...] + gt_ref[...] * _rms(acc, gpost_ref[...])


def _combine(yg, g8, hp, sg, su, sd, x2, g_post, mod, layer, seq, tm=512):
    n, d = x2.shape
    row = lambda i: (i, 0)
    full = lambda shape: pl.BlockSpec(shape, lambda i: (0, 0))
    return pl.pallas_call(
        _combine_kernel,
        grid=(n // tm,),
        in_specs=[
            pl.BlockSpec((TOP_K, tm, ROW_WORDS), lambda i: (0, i, 0)),
            pl.BlockSpec((tm, GATE_LANES), row),
            pl.BlockSpec((tm, ROW_WORDS), row),
            full((d, D_EXPERT)), full((d, D_EXPERT)), full((D_EXPERT, d)),
            pl.BlockSpec((tm, d), row),
            full((1, d)),
            _mod_spec(layer, 5, seq, tm),
        ],
        out_specs=pl.BlockSpec((tm, d), row),
        out_shape=jax.ShapeDtypeStruct((n, d), F32),
        compiler_params=_params("parallel"),
        name="combine",
    )(yg, g8, hp, sg, su, sd, x2, g_post, mod)


def _moe(hp, logits_t, b_router, wg, wu, wd, sg, su, sd, x2, g_post, mod, layer, seq):
    n = x2.shape[0]
    gate, sel, counts = _route(logits_t, b_router)
    cnt = counts[:, 0].astype(jnp.int32)
    padded = (cnt + EXPERT_BLOCK - 1) // EXPERT_BLOCK * EXPERT_BLOCK
    seg_end = jnp.cumsum(padded)
    seg_start = seg_end - padded
    n_blocks = n * TOP_K // EXPERT_BLOCK + N_EXPERTS
    blk_row = jnp.arange(n_blocks, dtype=jnp.int32) * EXPERT_BLOCK
    blk_expert = jnp.sum((seg_end[None, :] <= blk_row[:, None]).astype(jnp.int32), axis=1)
    blk_expert = jnp.minimum(blk_expert, N_EXPERTS - 1)
    blk_valid = jnp.clip(cnt[blk_expert] - (blk_row - seg_start[blk_expert]), 0, EXPERT_BLOCK).astype(jnp.int32)
    dest, g8 = _dest(sel, gate, seg_start.astype(F32).reshape(-1, 1))
    dest_flat = dest.reshape(-1)
    xs = _sc_scatter_rows(hp, dest_flat, n_blocks * EXPERT_BLOCK)
    ys = _experts(xs, blk_expert, blk_valid, wg, wu, wd, layer)
    yg = _sc_gather_rows(ys, dest_flat).reshape(TOP_K, n, ROW_WORDS)
    return _combine(yg, g8, hp, sg, su, sd, x2, g_post, mod, layer, seq)


def kernel(x, c, w_ada, b_ada, norm_mix_pre, norm_mix_post, norm_ffn_pre, norm_ffn_post, w_in, mu_shift, conv_w, conv_b, conv_ln_g, conv_ln_b, p_conv, w0, w2, a0, a2, g2, k_k, k_a, r_k, gn_g, gn_b, p_rwkv, w_o, w_router, b_router, we_gate, we_up, we_down, ws_gate, ws_up, ws_down):
    nb, seq, d = x.shape
    depth = w_ada.shape[0]
    n = nb * seq
    rw = RWKV_WIDTH
    mod = _ada(c, w_ada, b_ada)
    x2 = x.reshape(n, d)
    row = lambda a: a.reshape(1, -1)
    bf = lambda a: a.astype(BF16)
    we_gate_b, we_up_b, we_down_b = bf(we_gate), bf(we_up), bf(we_down)
    for l in range(depth):
        zc, zr, zg = _win(x2, row(norm_mix_pre[l]), mod, l, bf(w_in[l]), seq)
        uc = _conv(zc, conv_w[l], row(conv_b[l]), row(conv_ln_g[l]), row(conv_ln_b[l]), nb, seq)
        wa = jnp.zeros((LORA_W + LORA_A, 2 * rw), F32)
        wa = wa.at[:LORA_W, :rw].set(w2[l]).at[LORA_W:, rw:].set(a2[l])
        prep_out = _prep(zr, row(mu_shift[l]), row(w0[l]), row(a0[l]), wa, g2[l], row(k_k[l]),
                         row(k_a[l]), row(r_k[l]), nb, seq)
        ur = _scan(prep_out, row(gn_g[l]), row(gn_b[l]), nb, seq)
        x2, hp, logits_t = _merge(uc, ur, zg, x2, bf(p_conv[l]), bf(p_rwkv[l]), bf(w_o[l]),
                                  row(norm_mix_post[l]), row(norm_ffn_pre[l]), mod, l, w_router[l].T, seq)
        x2 = _moe(hp, logits_t, b_router[l].reshape(-1, 1), we_gate_b, we_up_b, we_down_b,
                  bf(ws_gate[l]), bf(ws_up[l]), bf(ws_down[l]), x2, row(norm_ffn_post[l]), mod, l, seq)
    return x2.reshape(nb, seq, d)
```

```python
import functools

import jax
import jax.numpy as jnp
from jax import lax
from jax.experimental import pallas as pl
from jax.experimental.pallas import tpu as pltpu
from jax.experimental.pallas import tpu_sc as plsc

F32 = jnp.float32
BF16 = jnp.bfloat16
HI = lax.Precision.HIGHEST

D_MODEL = 1024
CONV_WIDTH = 512
CONV_KERNEL = 31
RWKV_WIDTH = 512
HEAD = 64
N_HEADS = RWKV_WIDTH // HEAD
LORA_W = 64
LORA_A = 64
LORA_G = 128
N_EXPERTS = 64
TOP_K = 8
N_GROUPS = 8
TOPK_GROUPS = 4
GROUP_SIZE = N_EXPERTS // N_GROUPS
D_EXPERT = 256
ROUTED_SCALE = 2.5
RMS_EPS = 1e-6
LN_EPS = 1e-5
GN_EPS = 64e-5
COL_CONV = 2 * CONV_WIDTH
COL_RWKV = 3 * RWKV_WIDTH + LORA_W + LORA_A + LORA_G
COL_GATE = 2 * D_MODEL
D_IN = COL_CONV + COL_RWKV + COL_GATE

CHUNK = 64
QUAD = 4 * HEAD
N_QUADS = RWKV_WIDTH // QUAD
CUM_BLOCK = 256
CONV_HALO = 32
SUBLANES = 8
SHIFT_HALO = SUBLANES
GATE_LANES = 128
EXPERT_BLOCK = 512
EXPERT_SUB = 256
MOE_PARTS = 2
ROW_WORDS = D_MODEL // 2
SC_CORES = 2
SC_SUBCORES = 16
SC_WORKERS = SC_CORES * SC_SUBCORES
SC_CHUNK = 128
VMEM_LIMIT = 52 * 1024 * 1024


def _params(*sem):
    return pltpu.CompilerParams(dimension_semantics=sem, vmem_limit_bytes=VMEM_LIMIT)


def _sigmoid(x):
    return 1.0 / (1.0 + jnp.exp(-x))


def _softplus(x):
    return jnp.maximum(x, 0.0) + jnp.log(1.0 + jnp.exp(-jnp.abs(x)))


def _rms(x, g):
    return x * lax.rsqrt(jnp.mean(x * x, axis=-1, keepdims=True) + RMS_EPS) * g


def _split(x):
    hi = x.astype(BF16)
    return hi, (x - hi.astype(F32)).astype(BF16)


def _dot(a, b):
    return jnp.dot(a, b, preferred_element_type=F32)


def _dot_exact_rhs(x, w):
    hi, lo = _split(x)
    return _dot(hi, w) + _dot(lo, w)


def _pack_rows(y):
    half = y.shape[1] // 2
    hi = lax.bitcast_convert_type(y[:, :half].astype(BF16).astype(F32), jnp.uint32)
    lo = lax.bitcast_convert_type(y[:, half:].astype(BF16).astype(F32), jnp.uint32)
    return lax.bitcast_convert_type((hi & jnp.uint32(0xFFFF0000)) | (lo >> 16), jnp.int32)


def _unpack_rows(u):
    u = lax.bitcast_convert_type(u, jnp.uint32)
    hi = lax.bitcast_convert_type(u & jnp.uint32(0xFFFF0000), F32)
    lo = lax.bitcast_convert_type(u << 16, F32)
    return jnp.concatenate([hi, lo], axis=1)


def _dot_exact_lhs(w, x):
    hi, lo = _split(x)
    return _dot(w, hi) + _dot(w, lo)


def _ada_kernel(c_ref, w_ref, b_ref, o_ref):
    c = c_ref[...]
    o_ref[...] = jnp.dot(c * _sigmoid(c), w_ref[...], precision=HI,
                         preferred_element_type=F32) + b_ref[...]


def _ada(c, w_ada, b_ada):
    nl, d, _ = w_ada.shape
    nb = c.shape[0]
    out = pl.pallas_call(
        _ada_kernel,
        grid=(nl, 6),
        in_specs=[
            pl.BlockSpec((nb, d), lambda l, k: (0, 0)),
            pl.BlockSpec((None, d, d), lambda l, k: (l, 0, k)),
            pl.BlockSpec((None, None, 1, d), lambda l, k: (l, k, 0, 0)),
        ],
        out_specs=pl.BlockSpec((None, None, nb, d), lambda l, k: (l, k, 0, 0)),
        out_shape=jax.ShapeDtypeStruct((nl, 6, nb, d), F32),
        compiler_params=_params("parallel", "parallel"),
        name="ada",
    )(c, w_ada, b_ada.reshape(nl, 6, 1, d))
    return out.reshape(nl, 6, nb, 1, d)


def _mod_spec(layer, piece, rows_per_batch, tm, first_tile=0):
    return pl.BlockSpec((None, None, None, 1, D_MODEL),
                        lambda i, *_: (layer, piece, ((i + first_tile) * tm) // rows_per_batch, 0, 0))


def _win_kernel(x_ref, g_ref, sc_ref, sh_ref, w_ref, zc_ref, zr_ref, zg_ref):
    h = _rms(x_ref[...], g_ref[...]) * (1.0 + sc_ref[...]) + sh_ref[...]
    hb = h.astype(BF16)
    zc_ref[...] = _dot(hb, w_ref[:, :COL_CONV]).astype(zc_ref.dtype)
    zr_ref[...] = _dot(hb, w_ref[:, COL_CONV:COL_CONV + COL_RWKV])
    zg_ref[...] = _dot(hb, w_ref[:, COL_CONV + COL_RWKV:]).astype(zg_ref.dtype)


def _win(x2, g, mod, layer, w_in_b, seq, tm=512):
    n, d = x2.shape
    row = lambda i: (i, 0)
    return pl.pallas_call(
        _win_kernel,
        grid=(n // tm,),
        in_specs=[
            pl.BlockSpec((tm, d), row),
            pl.BlockSpec((1, d), lambda i: (0, 0)),
            _mod_spec(layer, 1, seq, tm),
            _mod_spec(layer, 0, seq, tm),
            pl.BlockSpec((d, D_IN), lambda i: (0, 0)),
        ],
        out_specs=[pl.BlockSpec((tm, COL_CONV), row), pl.BlockSpec((tm, COL_RWKV), row),
                   pl.BlockSpec((tm, COL_GATE), row)],
        out_shape=[jax.ShapeDtypeStruct((n, COL_CONV), BF16), jax.ShapeDtypeStruct((n, COL_RWKV), F32),
                   jax.ShapeDtypeStruct((n, COL_GATE), BF16)],
        compiler_params=_params("parallel"),
        name="win",
    )(x2, g, mod, mod, w_in_b)


def _conv_kernel(z_ref, halo_ref, w_ref, cb_ref, g_ref, b_ref, o_ref, ubuf, *, ts):
    j = pl.program_id(1)
    z = z_ref[...].astype(F32)
    zh = halo_ref[...].astype(F32)
    uh = zh[:, :CONV_WIDTH] * _sigmoid(zh[:, CONV_WIDTH:])
    rows = CONV_HALO + ts
    u = jnp.concatenate([jnp.where(j > 0, uh, 0.0),
                         z[:, :CONV_WIDTH] * _sigmoid(z[:, CONV_WIDTH:])], axis=0)
    ubuf[0] = u
    for s in range(1, SUBLANES):
        ubuf[s] = pltpu.roll(u, rows - s, axis=0)
    sub = 64
    first = CONV_HALO - (CONV_KERNEL - 1)
    for r in range(ts // sub):
        acc = jnp.zeros((sub, CONV_WIDTH), F32) + cb_ref[...]
        for k in range(CONV_KERNEL):
            s = (first + k) % SUBLANES
            base = r * sub + first + k - s
            acc = acc + w_ref[k:k + 1, :] * ubuf[s, base:base + sub, :]
        mu = jnp.mean(acc, axis=-1, keepdims=True)
        dlt = acc - mu
        var = jnp.mean(dlt * dlt, axis=-1, keepdims=True)
        y = dlt * lax.rsqrt(var + LN_EPS) * g_ref[...] + b_ref[...]
        o_ref[r * sub:(r + 1) * sub, :] = (y * _sigmoid(y)).astype(o_ref.dtype)


def _conv(zc, conv_w, conv_b, ln_g, ln_b, nb, seq, ts=256):
    n = zc.shape[0]
    tps = seq // ts
    hpt = ts // CONV_HALO
    vec = pl.BlockSpec((1, CONV_WIDTH), lambda b, j: (0, 0))
    return pl.pallas_call(
        functools.partial(_conv_kernel, ts=ts),
        grid=(nb, tps),
        in_specs=[
            pl.BlockSpec((ts, COL_CONV), lambda b, j: (b * tps + j, 0)),
            pl.BlockSpec((CONV_HALO, COL_CONV), lambda b, j: (jnp.maximum((b * tps + j) * hpt - 1, 0), 0)),
            pl.BlockSpec((CONV_KERNEL, CONV_WIDTH), lambda b, j: (0, 0)),
            vec, vec, vec,
        ],
        out_specs=pl.BlockSpec((ts, CONV_WIDTH), lambda b, j: (b * tps + j, 0)),
        out_shape=jax.ShapeDtypeStruct((n, CONV_WIDTH), BF16),
        scratch_shapes=[pltpu.VMEM((SUBLANES, CONV_HALO + ts, CONV_WIDTH), F32)],
        compiler_params=_params("parallel", "parallel"),
        name="conv",
    )(zc, zc, conv_w, conv_b, ln_g, ln_b)


def _prep_kernel(z_ref, halo_ref, mu_ref, w0_ref, a0_ref, wah_ref, wal_ref, g2_ref, kk_ref, ka_ref, rk_ref,
                 ones_ref, tri_ref, csel_ref,
                 at_ref, bt_ref, kt_ref, rt_ref, v_ref, bts_ref, kts_ref, bonus_ref, g_ref, pc_ref):
    j = pl.program_id(1)
    z = z_ref[...]
    ts = z.shape[0]
    last = jnp.where(j > 0, halo_ref[SHIFT_HALO - 1:SHIFT_HALO, :], 0.0)
    row = lax.broadcasted_iota(jnp.int32, z.shape, 0)
    prev = jnp.where(row == 0, last, pltpu.roll(z, 1, axis=0))
    zs = z + (prev - z) * mu_ref[...]
    rw = RWKV_WIDTH
    r = zs[:, :rw]
    k = zs[:, rw:2 * rw]
    v = zs[:, 2 * rw:3 * rw]
    wad = zs[:, 3 * rw:3 * rw + LORA_W + LORA_A]
    gd = zs[:, 3 * rw + LORA_W + LORA_A:]
    lane = lax.broadcasted_iota(jnp.int32, wad.shape, 1)
    xh, xl = _split(jnp.where(lane < LORA_W, jnp.tanh(wad), wad))
    wah = wah_ref[...]
    lora = _dot(xh, wah) + _dot(xl, wah) + _dot(xh, wal_ref[...])
    w = -_softplus(-(w0_ref[...] + lora[:, :rw])) - 0.5
    lw = -jnp.exp(w)
    a = _sigmoid(a0_ref[...] + lora[:, rw:])
    g_ref[...] = _dot(_sigmoid(gd).astype(BF16), g2_ref[...]).astype(g_ref.dtype)
    kk = k * kk_ref[...]
    k2 = k * (1.0 + (a - 1.0) * ka_ref[...])
    ones = ones_ref[...]
    kk = kk * lax.rsqrt(jnp.maximum(_dot_exact_rhs(kk * kk, ones), 1e-24))
    bonus_ref[...] = (_dot_exact_rhs(r * k2 * rk_ref[...], ones) * v).astype(bonus_ref.dtype)
    tri = tri_ref[...]
    lc, tot = [], []
    for h in range(ts // CUM_BLOCK):
        cs = _dot_exact_lhs(tri, lw[h * CUM_BLOCK:(h + 1) * CUM_BLOCK])
        lc.append(cs[:CUM_BLOCK])
        tot.append(cs[CUM_BLOCK:])
    lc = jnp.concatenate(lc, axis=0)
    tot = jnp.concatenate(tot, axis=0)
    e_neg = jnp.exp(-lc)
    e_end = jnp.exp(tot - lc)
    kka = kk * a
    dt = at_ref.dtype
    at_ref[...] = (-kk * jnp.exp(lc - lw)).astype(dt)
    bt_ref[...] = (kka * e_neg).astype(dt)
    kt_ref[...] = (k2 * e_neg).astype(dt)
    rt_ref[...] = (r * jnp.exp(lc)).astype(dt)
    v_ref[...] = v.astype(dt)
    bts_ref[...] = (kka * e_end).astype(dt)
    kts_ref[...] = (k2 * e_end).astype(dt)
    pc_ref[...] = jnp.exp(_dot_exact_lhs(csel_ref[...], lw))


def _prep(zr, mu, w0, a0, wa, g2, k_k, k_a, r_k, nb, seq, ts=512):
    n = zr.shape[0]
    tps = seq // ts
    hpt = ts // SHIFT_HALO
    rw = RWKV_WIDTH
    cpt = ts // CHUNK
    ones = jnp.kron(jnp.eye(N_HEADS, dtype=F32), jnp.ones((HEAD, HEAD), F32)).astype(BF16)
    blk = jnp.kron(jnp.eye(CUM_BLOCK // CHUNK, dtype=F32), jnp.ones((CHUNK, CHUNK), F32))
    tri = jnp.concatenate([jnp.tril(blk), blk], axis=0).astype(BF16)
    csel = jnp.kron(jnp.eye(cpt, dtype=F32), jnp.ones((1, CHUNK), F32)).astype(BF16)
    wa_hi, wa_lo = _split(wa)
    vec = pl.BlockSpec((1, rw), lambda b, j: (0, 0))
    full = lambda shape: pl.BlockSpec(shape, lambda b, j: (0, 0))
    rows = pl.BlockSpec((ts, rw), lambda b, j: (b * tps + j, 0))
    return pl.pallas_call(
        _prep_kernel,
        grid=(nb, tps),
        in_specs=[
            pl.BlockSpec((ts, COL_RWKV), lambda b, j: (b * tps + j, 0)),
            pl.BlockSpec((SHIFT_HALO, COL_RWKV), lambda b, j: (jnp.maximum((b * tps + j) * hpt - 1, 0), 0)),
            full((1, COL_RWKV)), vec, vec,
            full((LORA_W + LORA_A, 2 * rw)), full((LORA_W + LORA_A, 2 * rw)), full((LORA_G, rw)),
            vec, vec, vec, full((rw, rw)), full((2 * CUM_BLOCK, CUM_BLOCK)), full((cpt, ts)),
        ],
        out_specs=[rows] * 9 + [pl.BlockSpec((cpt, rw), lambda b, j: (b * tps + j, 0))],
        out_shape=[jax.ShapeDtypeStruct((n, rw), BF16)] * 9 + [jax.ShapeDtypeStruct((n // CHUNK, rw), F32)],
        compiler_params=_params("parallel", "parallel"),
        name="prep",
    )(zr, zr, mu, w0, a0, wa_hi, wa_lo, g2.astype(BF16), k_k, k_a, r_k, ones, tri, csel)


def _mm(a, b):
    return jnp.dot(a.astype(BF16), b.astype(BF16), preferred_element_type=F32)


def _mm_nt(a, b):
    return lax.dot_general(a.astype(BF16), b.astype(BF16), (((1,), (1,)), ((), ())),
                           preferred_element_type=F32)


def _mm_tn(a, b):
    return lax.dot_general(a.astype(BF16), b.astype(BF16), (((0,), (0,)), ((), ())),
                           preferred_element_type=F32)


def _scan_kernel(at_ref, bt_ref, kt_ref, rt_ref, v_ref, bts_ref, kts_ref, bonus_ref, g_ref, pc_ref,
                 gng_ref, gnb_ref, ones_ref, o_ref, s_ref, y_ref, *, ts, gb):
    @pl.when(pl.program_id(1) == 0)
    def _():
        s_ref[...] = jnp.zeros_like(s_ref)

    row_q = lax.broadcasted_iota(jnp.int32, (QUAD, QUAD), 0)
    col_q = lax.broadcasted_iota(jnp.int32, (QUAD, QUAD), 1)
    bdmask = (row_q // HEAD) == (col_q // HEAD)
    row_c = lax.broadcasted_iota(jnp.int32, (CHUNK, QUAD), 0)
    col_c = lax.broadcasted_iota(jnp.int32, (CHUNK, QUAD), 1) % CHUNK
    strict = col_c < row_c
    incl = col_c <= row_c
    eye = jnp.where(col_c == row_c, 1.0, 0.0).astype(F32)

    def bd(x):
        xb = x.astype(BF16)
        return jnp.where(bdmask, jnp.concatenate([xb] * (QUAD // CHUNK), axis=0), jnp.zeros((), BF16))

    chains = [(b, qd) for b in range(gb) for qd in range(N_QUADS)]

    def chunk(c, carry):
        sl = pl.ds(pl.multiple_of(c * CHUNK, CHUNK), CHUNK)
        lanes = [slice(qd * QUAD, (qd + 1) * QUAD) for _, qd in chains]
        ld = lambda ref: [ref[b, sl, ln] for (b, _), ln in zip(chains, lanes)]
        at, bt, kt, rt, v, bts, kts = (ld(r) for r in (at_ref, bt_ref, kt_ref, rt_ref, v_ref, bts_ref, kts_ref))
        each = range(len(chains))
        ar = [jnp.concatenate([at[i], rt[i]], axis=0) for i in each]
        pb = [_mm_nt(ar[i], bd(bt[i])) for i in each]
        pk = [_mm_nt(ar[i], bd(kt[i])) for i in each]
        l_ab = [jnp.where(strict, p[:CHUNK], 0.0) for p in pb]
        a_rb = [jnp.where(incl, p[CHUNK:], 0.0) for p in pb]
        l_ak = [jnp.where(strict, p[:CHUNK], 0.0) for p in pk]
        a_rk = [jnp.where(incl, p[CHUNK:], 0.0) for p in pk]
        tm = [eye + l for l in l_ab]
        lp = [_mm(l, bd(l)) for l in l_ab]
        for _ in range(4):
            both = [_mm(jnp.concatenate([tm[i], lp[i]], axis=0), bd(lp[i])) for i in each]
            tm = [tm[i] + both[i][:CHUNK] for i in each]
            lp = [both[i][CHUNK:] for i in each]
        tm = [tm[i] + _mm(tm[i], bd(lp[i])) for i in each]
        bdv = [bd(x) for x in v]
        x0 = [_mm(l_ak[i], bdv[i]) for i in each]
        tu = [_mm(tm[i], jnp.concatenate([bd(x0[i]), bd(at[i])], axis=1)) for i in each]
        u0 = [t[:, :QUAD] for t in tu]
        w = [t[:, QUAD:] for t in tu]
        qy = [_mm(a_rb[i], jnp.concatenate([bd(w[i]), bd(u0[i])], axis=1)) for i in each]
        y0 = [qy[i][:, QUAD:] + _mm(a_rk[i], bdv[i]) for i in each]
        q = [rt[i].astype(F32) + qy[i][:, :QUAD] for i in each]
        s = [s_ref[i] for i in each]
        qw_s = [_mm_nt(jnp.concatenate([q[i], w[i]], axis=0), s[i]) for i in each]
        for i, ((b, _), ln) in enumerate(zip(chains, lanes)):
            y_ref[b, sl, ln] = y0[i] + qw_s[i][:CHUNK]
        u = [(u0[i] + qw_s[i][CHUNK:]).astype(BF16) for i in each]
        n_p = [jnp.where(bdmask, _mm_tn(jnp.concatenate([u[i], v[i]], axis=0),
                                        jnp.concatenate([bts[i], kts[i]], axis=0)), 0.0) for i in each]
        for i, ((b, _), ln) in enumerate(zip(chains, lanes)):
            s_ref[i] = s[i] * pc_ref[b, pl.ds(c, 1), ln] + n_p[i]
        return carry

    lax.fori_loop(0, ts // CHUNK, chunk, 0)

    ones = ones_ref[...]
    for b in range(gb):
        for qd in range(N_QUADS):
            ln = slice(qd * QUAD, (qd + 1) * QUAD)
            y = y_ref[b, :, ln]
            dlt = y - _dot_exact_rhs(y, ones) * (1.0 / HEAD)
            var = _dot_exact_rhs(dlt * dlt, ones) * (1.0 / HEAD)
            yn = dlt * lax.rsqrt(var + GN_EPS) * gng_ref[:, ln] + gnb_ref[:, ln]
            o_ref[b, :, ln] = ((yn + bonus_ref[b, :, ln].astype(F32)) * g_ref[b, :, ln].astype(F32)).astype(o_ref.dtype)


def _scan(prep_out, gn_g, gn_b, nb, seq, ts=256, gb=4):
    n = prep_out[0].shape[0]
    rw = RWKV_WIDTH
    cpt = ts // CHUNK
    ones = jnp.kron(jnp.eye(QUAD // HEAD, dtype=F32), jnp.ones((HEAD, HEAD), F32)).astype(BF16)
    rows = pl.BlockSpec((gb, ts, rw), lambda i, t: (i, t, 0))
    vec = pl.BlockSpec((1, rw), lambda i, t: (0, 0))
    args = [a.reshape(nb, seq, rw) for a in prep_out[:9]] + [prep_out[9].reshape(nb, seq // ts, cpt, rw)]
    out = pl.pallas_call(
        functools.partial(_scan_kernel, ts=ts, gb=gb),
        grid=(nb // gb, seq // ts),
        in_specs=[rows] * 9 + [pl.BlockSpec((gb, None, cpt, rw), lambda i, t: (i, t, 0, 0)), vec, vec,
                               pl.BlockSpec((QUAD, QUAD), lambda i, t: (0, 0))],
        out_specs=rows,
        out_shape=jax.ShapeDtypeStruct((nb, seq, rw), BF16),
        scratch_shapes=[pltpu.VMEM((gb * N_QUADS, QUAD, QUAD), F32), pltpu.VMEM((gb, ts, rw), F32)],
        compiler_params=_params("parallel", "arbitrary"),
        name="scan",
    )(*args, gn_g, gn_b, ones)
    return out.reshape(n, rw)


def _merge_kernel(uc_ref, ur_ref, zg_ref, x_ref, pc_ref, pr_ref, wo_ref, gpost_ref, gt_ref,
                  gpre_ref, sc_ref, sh_ref, wrt_ref, xo_ref, h_ref, lg_ref):
    yc = _dot(uc_ref[...], pc_ref[...])
    yr = _dot(ur_ref[...], pr_ref[...])
    zg = zg_ref[...].astype(F32)
    m = _sigmoid(zg[:, :D_MODEL]) * yc + _sigmoid(zg[:, D_MODEL:]) * yr
    y = _dot(m.astype(BF16), wo_ref[...])
    xn = x_ref[...] + gt_ref[...] * _rms(y, gpost_ref[...])
    xo_ref[...] = xn
    h = _rms(xn, gpre_ref[...]) * (1.0 + sc_ref[...]) + sh_ref[...]
    h_ref[...] = _pack_rows(h)
    lg_ref[...] = lax.dot_general(wrt_ref[...], h, (((1,), (1,)), ((), ())), precision=HI,
                                  preferred_element_type=F32)


def _merge(uc, ur, zg, x2, p_conv_b, p_rwkv_b, w_o_b, g_post, g_pre, mod, layer, w_router_t, seq, tm=512):
    n, d = x2.shape
    row = lambda i: (i, 0)
    full = lambda shape: pl.BlockSpec(shape, lambda i: (0, 0))
    return pl.pallas_call(
        _merge_kernel,
        grid=(n // tm,),
        in_specs=[
            pl.BlockSpec((tm, CONV_WIDTH), row), pl.BlockSpec((tm, RWKV_WIDTH), row),
            pl.BlockSpec((tm, COL_GATE), row), pl.BlockSpec((tm, d), row),
            full((CONV_WIDTH, d)), full((RWKV_WIDTH, d)), full((d, d)),
            full((1, d)), _mod_spec(layer, 2, seq, tm),
            full((1, d)), _mod_spec(layer, 4, seq, tm), _mod_spec(layer, 3, seq, tm),
            full((N_EXPERTS, d)),
        ],
        out_specs=[pl.BlockSpec((tm, d), row), pl.BlockSpec((tm, ROW_WORDS), row),
                   pl.BlockSpec((N_EXPERTS, tm), lambda i: (0, i))],
        out_shape=[jax.ShapeDtypeStruct((n, d), F32), jax.ShapeDtypeStruct((n, ROW_WORDS), jnp.int32),
                   jax.ShapeDtypeStruct((N_EXPERTS, n), F32)],
        compiler_params=_params("parallel"),
        name="merge",
    )(uc, ur, zg, x2, p_conv_b, p_rwkv_b, w_o_b, g_post, mod, g_pre, mod, mod, w_router_t)


def _route_kernel(lg_ref, b_ref, gate_ref, sel_ref, cnt_ref):
    s = _sigmoid(lg_ref[...])
    biased = s + b_ref[...]
    t = s.shape[1]
    member = lax.broadcasted_iota(jnp.int32, (GROUP_SIZE, t), 0)
    grp = []
    for g in range(N_GROUPS):
        bg = biased[g * GROUP_SIZE:(g + 1) * GROUP_SIZE, :]
        m1 = jnp.max(bg, axis=0, keepdims=True)
        first = jnp.min(jnp.where(bg == m1, member, GROUP_SIZE), axis=0, keepdims=True)
        m2 = jnp.max(jnp.where(member == first, -jnp.inf, bg), axis=0, keepdims=True)
        grp.append(m1 + m2)
    masked = []
    for g in range(N_GROUPS):
        rank = jnp.zeros((1, t), jnp.int32)
        for o in range(N_GROUPS):
            if o == g:
                continue
            ahead = (grp[o] > grp[g]) if o > g else (grp[o] >= grp[g])
            rank = rank + jnp.where(ahead, 1, 0)
        keep = rank < TOPK_GROUPS
        masked.append(jnp.where(keep, biased[g * GROUP_SIZE:(g + 1) * GROUP_SIZE, :], -jnp.inf))
    masked = jnp.concatenate(masked, axis=0)
    eidx = lax.broadcasted_iota(jnp.int32, masked.shape, 0)
    rank = jnp.zeros(masked.shape, jnp.int32)
    for o in range(N_EXPERTS):
        ro = masked[o:o + 1, :]
        rank = rank + jnp.where(ro > masked, 1, jnp.where((ro == masked) & (eidx > o), 1, 0))
    chosen = rank < TOP_K
    sel = jnp.where(chosen, s, 0.0)
    gate_ref[...] = sel / jnp.sum(sel, axis=0, keepdims=True) * ROUTED_SCALE
    mask = jnp.where(chosen, 1.0, 0.0)
    sel_ref[...] = mask.astype(sel_ref.dtype)

    @pl.when(pl.program_id(0) == 0)
    def _():
        cnt_ref[...] = jnp.zeros_like(cnt_ref)

    cnt_ref[...] += jnp.sum(mask, axis=1, keepdims=True)


def _route(logits_t, b_router, row0, n, tt=512):
    e = logits_t.shape[0]
    first = row0 // tt
    return pl.pallas_call(
        _route_kernel,
        grid=(n // tt,),
        in_specs=[pl.BlockSpec((e, tt), lambda i: (0, i + first)), pl.BlockSpec((e, 1), lambda i: (0, 0))],
        out_specs=[pl.BlockSpec((e, tt), lambda i: (0, i)), pl.BlockSpec((e, tt), lambda i: (0, i)),
                   pl.BlockSpec((e, GATE_LANES), lambda i: (0, 0))],
        out_shape=[jax.ShapeDtypeStruct((e, n), F32), jax.ShapeDtypeStruct((e, n), BF16),
                   jax.ShapeDtypeStruct((e, GATE_LANES), F32)],
        compiler_params=_params("arbitrary"),
        name="route",
    )(logits_t, b_router)


def _dest_kernel(sel_ref, gate_ref, start_ref, triu_ref, below_ref, dest_ref, g8_ref, run_ref):
    @pl.when(pl.program_id(0) == 0)
    def _():
        run_ref[...] = jnp.zeros_like(run_ref)

    sel = sel_ref[...]
    t = sel.shape[1]
    self32 = sel.astype(F32)
    incl = _dot(sel, triu_ref[...])
    pos = start_ref[...] + run_ref[...] + incl - self32
    run_ref[...] += incl[:, t - 1:t]
    slot = _dot(below_ref[...], sel)
    gate = gate_ref[...]
    dst, gts = [], []
    for k in range(TOP_K):
        mine = (self32 > 0.0) & (slot == float(k))
        dst.append(jnp.sum(jnp.where(mine, pos, 0.0), axis=0, keepdims=True))
        gts.append(jnp.sum(jnp.where(mine, gate, 0.0), axis=0, keepdims=True))
    dest_ref[...] = jnp.concatenate(dst, axis=0).astype(jnp.int32)
    g8_ref[...] = jnp.concatenate(gts + [jnp.zeros((GATE_LANES - TOP_K, t), F32)], axis=0).T


def _dest(sel, gate, seg_start, tt=512):
    e, n = sel.shape
    triu = jnp.triu(jnp.ones((tt, tt), F32)).astype(BF16)
    below = jnp.tril(jnp.ones((e, e), F32), -1).astype(BF16)
    return pl.pallas_call(
        _dest_kernel,
        grid=(n // tt,),
        in_specs=[pl.BlockSpec((e, tt), lambda i: (0, i)), pl.BlockSpec((e, tt), lambda i: (0, i)),
                  pl.BlockSpec((e, 1), lambda i: (0, 0)), pl.BlockSpec((tt, tt), lambda i: (0, 0)),
                  pl.BlockSpec((e, e), lambda i: (0, 0))],
        out_specs=[pl.BlockSpec((TOP_K, tt), lambda i: (0, i)), pl.BlockSpec((tt, GATE_LANES), lambda i: (i, 0))],
        out_shape=[jax.ShapeDtypeStruct((TOP_K, n), jnp.int32), jax.ShapeDtypeStruct((n, GATE_LANES), F32)],
        scratch_shapes=[pltpu.VMEM((e, 1), F32)],
        compiler_params=_params("arbitrary"),
        name="dest",
    )(sel, gate, seg_start, triu, below)


def _sc_mesh():
    return plsc.VectorSubcoreMesh(core_axis_name="c", subcore_axis_name="s",
                                  num_cores=SC_CORES, num_subcores=SC_SUBCORES)


def _worker_id():
    return lax.axis_index("s") * SC_CORES + lax.axis_index("c")


def _sc_scatter_rows(rows, row0, dest_flat, n_out):
    n = dest_flat.shape[0] // TOP_K
    w = rows.shape[1]
    per_worker = n // SC_WORKERS

    def body(rows_hbm, dest_hbm, out_hbm, idx_v, rows_v):
        base = _worker_id() * per_worker

        @pl.loop(0, per_worker // SC_CHUNK)
        def _(c):
            t0 = base + c * SC_CHUNK
            pltpu.sync_copy(rows_hbm.at[pl.ds(row0 + t0, SC_CHUNK)], rows_v)
            for k in range(TOP_K):
                pltpu.sync_copy(dest_hbm.at[pl.ds(k * n + t0, SC_CHUNK)], idx_v)
                pltpu.sync_copy(rows_v, out_hbm.at[idx_v])

    return pl.kernel(
        body, out_type=jax.ShapeDtypeStruct((n_out, w), rows.dtype), mesh=_sc_mesh(),
        scratch_types=[pltpu.VMEM((SC_CHUNK,), jnp.int32), pltpu.VMEM((SC_CHUNK, w), rows.dtype)],
        name="sc_scatter",
    )(rows, dest_flat)


def _sc_gather_rows(table, idx_flat):
    m = idx_flat.shape[0]
    w = table.shape[1]
    per_worker = m // SC_WORKERS

    def body(table_hbm, idx_hbm, out_hbm, idx_v, rows_v):
        base = _worker_id() * per_worker

        @pl.loop(0, per_worker // SC_CHUNK)
        def _(c):
            j0 = base + c * SC_CHUNK
            pltpu.sync_copy(idx_hbm.at[pl.ds(j0, SC_CHUNK)], idx_v)
            pltpu.sync_copy(table_hbm.at[idx_v], rows_v)
            pltpu.sync_copy(rows_v, out_hbm.at[pl.ds(j0, SC_CHUNK)])

    return pl.kernel(
        body, out_type=jax.ShapeDtypeStruct((m, w), table.dtype), mesh=_sc_mesh(),
        scratch_types=[pltpu.VMEM((SC_CHUNK,), jnp.int32), pltpu.VMEM((SC_CHUNK, w), table.dtype)],
        name="sc_gather",
    )(table, idx_flat)


def _swiglu(h, wg, wu):
    gte = _dot(h, wg)
    return gte * _sigmoid(gte) * _dot(h, wu)


def _experts_kernel(be_ref, va_ref, xs_ref, wg_ref, wu_ref, wd_ref, ys_ref, wg_b, wu_b, wd_b):
    s = pl.program_id(0)
    valid = va_ref[s]
    fresh = jnp.logical_or(s == 0, be_ref[s] != be_ref[jnp.maximum(s - 1, 0)])

    @pl.when(jnp.logical_and(valid > 0, fresh))
    def _():
        wg_b[...] = wg_ref[...].astype(BF16)
        wu_b[...] = wu_ref[...].astype(BF16)
        wd_b[...] = wd_ref[...].astype(BF16)

    @pl.when(valid > 0)
    def _():
        nsub = EXPERT_BLOCK // EXPERT_SUB
        row = lax.broadcasted_iota(jnp.int32, (EXPERT_SUB, D_MODEL), 0)
        x = []
        for i in range(nsub):
            xi = _unpack_rows(xs_ref[i * EXPERT_SUB:(i + 1) * EXPERT_SUB, :])
            x.append(jnp.where(row < valid - i * EXPERT_SUB, xi, 0.0).astype(BF16))
        gte = [_dot(xi, wg_b[...]) for xi in x]
        up = [_dot(xi, wu_b[...]) for xi in x]
        act = [(g * _sigmoid(g) * u).astype(BF16) for g, u in zip(gte, up)]
        y = [_dot(a, wd_b[...]) for a in act]
        for i in range(nsub):
            ys_ref[i * EXPERT_SUB:(i + 1) * EXPERT_SUB, :] = _pack_rows(y[i])


def _experts(xs, blk_expert, blk_valid, wg, wu, wd, layer):
    n_rows, w = xs.shape
    d = wg.shape[2]
    rows = pl.BlockSpec((EXPERT_BLOCK, w), lambda s, be, va: (s, 0))
    up_spec = pl.BlockSpec((None, None, d, D_EXPERT), lambda s, be, va: (layer, be[s], 0, 0))
    down_spec = pl.BlockSpec((None, None, D_EXPERT, d), lambda s, be, va: (layer, be[s], 0, 0))
    return pl.pallas_call(
        _experts_kernel,
        grid_spec=pltpu.PrefetchScalarGridSpec(
            num_scalar_prefetch=2,
            grid=(n_rows // EXPERT_BLOCK,),
            in_specs=[rows, up_spec, up_spec, down_spec],
            out_specs=rows,
            scratch_shapes=[pltpu.VMEM((d, D_EXPERT), BF16), pltpu.VMEM((d, D_EXPERT), BF16),
                            pltpu.VMEM((D_EXPERT, d), BF16)],
        ),
        out_shape=jax.ShapeDtypeStruct((n_rows, w), xs.dtype),
        compiler_params=_params("arbitrary"),
        name="experts",
    )(blk_expert, blk_valid, xs, wg, wu, wd)


def _combine_kernel(yg_ref, g8_ref, hp_ref, sg_ref, su_ref, sd_ref, x_ref, gpost_ref, gt_ref, *rest):
    o_ref = rest[-1]
    h = _unpack_rows(hp_ref[...]).astype(BF16)
    acc = _dot(_swiglu(h, sg_ref[...], su_ref[...]).astype(BF16), sd_ref[...])
    g8 = g8_ref[...]
    for k in range(TOP_K):
        acc = acc + g8[:, k:k + 1] * _unpack_rows(yg_ref[k])
    o_ref[...] = x_ref[...] + gt_ref[...] * _rms(acc, gpost_ref[...])


def _combine(yg, g8, hp, sg, su, sd, x2, g_post, mod, layer, seq, row0, earlier, tm=512):
    n, d = x2.shape
    part = g8.shape[0]
    first = row0 // tm
    local = lambda i: (i, 0)
    glob = lambda i: (i + first, 0)
    full = lambda shape: pl.BlockSpec(shape, lambda i: (0, 0))
    in_specs = [
        pl.BlockSpec((TOP_K, tm, ROW_WORDS), lambda i: (0, i, 0)),
        pl.BlockSpec((tm, GATE_LANES), local),
        pl.BlockSpec((tm, ROW_WORDS), glob),
        full((d, D_EXPERT)), full((d, D_EXPERT)), full((D_EXPERT, d)),
        pl.BlockSpec((tm, d), glob),
        full((1, d)),
        _mod_spec(layer, 5, seq, tm, first),
    ]
    args = [yg, g8, hp, sg, su, sd, x2, g_post, mod]
    aliases = {}
    if earlier is not None:
        in_specs.append(pl.BlockSpec(memory_space=pl.ANY))
        args.append(earlier)
        aliases = {len(args) - 1: 0}
    return pl.pallas_call(
        _combine_kernel,
        grid=(part // tm,),
        in_specs=in_specs,
        out_specs=pl.BlockSpec((tm, d), glob),
        out_shape=jax.ShapeDtypeStruct((n, d), F32),
        input_output_aliases=aliases,
        compiler_params=_params("parallel"),
        name="combine",
    )(*args)


def _moe(hp, logits_t, b_router, wg, wu, wd, sg, su, sd, x2, g_post, mod, layer, seq):
    n = x2.shape[0]
    part = n // MOE_PARTS
    out = None
    for p in range(MOE_PARTS):
        row0 = p * part
        gate, sel, counts = _route(logits_t, b_router, row0, part)
        cnt = counts[:, 0].astype(jnp.int32)
        padded = (cnt + EXPERT_BLOCK - 1) // EXPERT_BLOCK * EXPERT_BLOCK
        seg_end = jnp.cumsum(padded)
        seg_start = seg_end - padded
        n_blocks = part * TOP_K // EXPERT_BLOCK + N_EXPERTS
        blk_row = jnp.arange(n_blocks, dtype=jnp.int32) * EXPERT_BLOCK
        blk_expert = jnp.sum((seg_end[None, :] <= blk_row[:, None]).astype(jnp.int32), axis=1)
        blk_expert = jnp.minimum(blk_expert, N_EXPERTS - 1)
        blk_valid = jnp.clip(cnt[blk_expert] - (blk_row - seg_start[blk_expert]), 0, EXPERT_BLOCK)
        dest, g8 = _dest(sel, gate, seg_start.astype(F32).reshape(-1, 1))
        dest_flat = dest.reshape(-1)
        xs = _sc_scatter_rows(hp, row0, dest_flat, n_blocks * EXPERT_BLOCK)
        ys = _experts(xs, blk_expert, blk_valid.astype(jnp.int32), wg, wu, wd, layer)
        yg = _sc_gather_rows(ys, dest_flat).reshape(TOP_K, part, ROW_WORDS)
        out = _combine(yg, g8, hp, sg, su, sd, x2, g_post, mod, layer, seq, row0, out)
    return out


def kernel(x, c, w_ada, b_ada, norm_mix_pre, norm_mix_post, norm_ffn_pre, norm_ffn_post, w_in, mu_shift, conv_w, conv_b, conv_ln_g, conv_ln_b, p_conv, w0, w2, a0, a2, g2, k_k, k_a, r_k, gn_g, gn_b, p_rwkv, w_o, w_router, b_router, we_gate, we_up, we_down, ws_gate, ws_up, ws_down):
    nb, seq, d = x.shape
    depth = w_ada.shape[0]
    n = nb * seq
    rw = RWKV_WIDTH
    mod = _ada(c, w_ada, b_ada)
    x2 = x.reshape(n, d)
    row = lambda a: a.reshape(1, -1)
    bf = lambda a: a.astype(BF16)
    for l in range(depth):
        zc, zr, zg = _win(x2, row(norm_mix_pre[l]), mod, l, bf(w_in[l]), seq)
        uc = _conv(zc, conv_w[l], row(conv_b[l]), row(conv_ln_g[l]), row(conv_ln_b[l]), nb, seq)
        wa = jnp.zeros((LORA_W + LORA_A, 2 * rw), F32)
        wa = wa.at[:LORA_W, :rw].set(w2[l]).at[LORA_W:, rw:].set(a2[l])
        prep_out = _prep(zr, row(mu_shift[l]), row(w0[l]), row(a0[l]), wa, g2[l], row(k_k[l]),
                         row(k_a[l]), row(r_k[l]), nb, seq)
        ur = _scan(prep_out, row(gn_g[l]), row(gn_b[l]), nb, seq)
        x2, hp, logits_t = _merge(uc, ur, zg, x2, bf(p_conv[l]), bf(p_rwkv[l]), bf(w_o[l]),
                                  row(norm_mix_post[l]), row(norm_ffn_pre[l]), mod, l, w_router[l].T, seq)
        x2 = _moe(hp, logits_t, b_router[l].reshape(-1, 1), we_gate, we_up, we_down,
                  bf(ws_gate[l]), bf(ws_up[l]), bf(ws_down[l]), x2, row(norm_ffn_post[l]), mod, l, seq)
    return x2.reshape(nb, seq, d)
```

```python
import functools

import jax
import jax.numpy as jnp
from jax import lax
from jax.experimental import pallas as pl
from jax.experimental.pallas import tpu as pltpu
from jax.experimental.pallas import tpu_sc as plsc

F32 = jnp.float32
BF16 = jnp.bfloat16
HI = lax.Precision.HIGHEST

D_MODEL = 1024
CONV_WIDTH = 512
CONV_KERNEL = 31
RWKV_WIDTH = 512
HEAD = 64
N_HEADS = RWKV_WIDTH // HEAD
LORA_W = 64
LORA_A = 64
LORA_G = 128
N_EXPERTS = 64
TOP_K = 8
N_GROUPS = 8
TOPK_GROUPS = 4
GROUP_SIZE = N_EXPERTS // N_GROUPS
D_EXPERT = 256
ROUTED_SCALE = 2.5
RMS_EPS = 1e-6
LN_EPS = 1e-5
GN_EPS = 64e-5
COL_CONV = 2 * CONV_WIDTH
COL_RWKV = 3 * RWKV_WIDTH + LORA_W + LORA_A + LORA_G
COL_GATE = 2 * D_MODEL
D_IN = COL_CONV + COL_RWKV + COL_GATE

CHUNK = 64
QUAD = 4 * HEAD
N_QUADS = RWKV_WIDTH // QUAD
CUM_BLOCK = 256
CONV_HALO = 32
SUBLANES = 8
SHIFT_HALO = SUBLANES
GATE_LANES = 128
EXPERT_BLOCK = 512
EXPERT_SUB = 256
MOE_PARTS = 1
ROW_WORDS = D_MODEL // 2
SC_CORES = 2
SC_SUBCORES = 16
SC_WORKERS = SC_CORES * SC_SUBCORES
SC_CHUNK = 128
VMEM_LIMIT = 52 * 1024 * 1024


def _params(*sem):
    return pltpu.CompilerParams(dimension_semantics=sem, vmem_limit_bytes=VMEM_LIMIT)


def _sigmoid(x):
    return 1.0 / (1.0 + jnp.exp(-x))


def _softplus(x):
    return jnp.maximum(x, 0.0) + jnp.log(1.0 + jnp.exp(-jnp.abs(x)))


def _rms(x, g):
    return x * lax.rsqrt(jnp.mean(x * x, axis=-1, keepdims=True) + RMS_EPS) * g


def _split(x):
    hi = x.astype(BF16)
    return hi, (x - hi.astype(F32)).astype(BF16)


def _dot(a, b):
    return jnp.dot(a, b, preferred_element_type=F32)


def _dot_exact_rhs(x, w):
    hi, lo = _split(x)
    return _dot(hi, w) + _dot(lo, w)


def _pack_rows(y):
    half = y.shape[1] // 2
    packed = pltpu.pack_elementwise([y[:, half:], y[:, :half]], packed_dtype=BF16)
    return lax.bitcast_convert_type(packed, jnp.int32)


def _unpack_rows(u):
    hi = pltpu.unpack_elementwise(u, index=1, packed_dtype=BF16, unpacked_dtype=F32)
    lo = pltpu.unpack_elementwise(u, index=0, packed_dtype=BF16, unpacked_dtype=F32)
    return jnp.concatenate([hi, lo], axis=1)


def _dot_exact_lhs(w, x):
    hi, lo = _split(x)
    return _dot(w, hi) + _dot(w, lo)


def _ada_kernel(c_ref, w_ref, b_ref, o_ref):
    c = c_ref[...]
    o_ref[...] = jnp.dot(c * _sigmoid(c), w_ref[...], precision=HI,
                         preferred_element_type=F32) + b_ref[...]


def _ada(c, w_ada, b_ada):
    nl, d, _ = w_ada.shape
    nb = c.shape[0]
    out = pl.pallas_call(
        _ada_kernel,
        grid=(nl, 6),
        in_specs=[
            pl.BlockSpec((nb, d), lambda l, k: (0, 0)),
            pl.BlockSpec((None, d, d), lambda l, k: (l, 0, k)),
            pl.BlockSpec((None, None, 1, d), lambda l, k: (l, k, 0, 0)),
        ],
        out_specs=pl.BlockSpec((None, None, nb, d), lambda l, k: (l, k, 0, 0)),
        out_shape=jax.ShapeDtypeStruct((nl, 6, nb, d), F32),
        compiler_params=_params("parallel", "parallel"),
        name="ada",
    )(c, w_ada, b_ada.reshape(nl, 6, 1, d))
    return out.reshape(nl, 6, nb, 1, d)


def _mod_spec(layer, piece, rows_per_batch, tm, first_tile=0):
    return pl.BlockSpec((None, None, None, 1, D_MODEL),
                        lambda i, *_: (layer, piece, ((i + first_tile) * tm) // rows_per_batch, 0, 0))


def _win_kernel(x_ref, g_ref, sc_ref, sh_ref, w_ref, zc_ref, zr_ref, zg_ref):
    h = _rms(x_ref[...], g_ref[...]) * (1.0 + sc_ref[...]) + sh_ref[...]
    hb = h.astype(BF16)
    zc_ref[...] = _dot(hb, w_ref[:, :COL_CONV]).astype(zc_ref.dtype)
    zr_ref[...] = _dot(hb, w_ref[:, COL_CONV:COL_CONV + COL_RWKV])
    zg_ref[...] = _dot(hb, w_ref[:, COL_CONV + COL_RWKV:]).astype(zg_ref.dtype)


def _win(x2, g, mod, layer, w_in_b, seq, tm=512):
    n, d = x2.shape
    row = lambda i: (i, 0)
    return pl.pallas_call(
        _win_kernel,
        grid=(n // tm,),
        in_specs=[
            pl.BlockSpec((tm, d), row),
            pl.BlockSpec((1, d), lambda i: (0, 0)),
            _mod_spec(layer, 1, seq, tm),
            _mod_spec(layer, 0, seq, tm),
            pl.BlockSpec((d, D_IN), lambda i: (0, 0)),
        ],
        out_specs=[pl.BlockSpec((tm, COL_CONV), row), pl.BlockSpec((tm, COL_RWKV), row),
                   pl.BlockSpec((tm, COL_GATE), row)],
        out_shape=[jax.ShapeDtypeStruct((n, COL_CONV), BF16), jax.ShapeDtypeStruct((n, COL_RWKV), F32),
                   jax.ShapeDtypeStruct((n, COL_GATE), BF16)],
        compiler_params=_params("parallel"),
        name="win",
    )(x2, g, mod, mod, w_in_b)


def _conv_kernel(z_ref, halo_ref, w_ref, cb_ref, g_ref, b_ref, o_ref, ubuf, *, ts):
    j = pl.program_id(1)
    z = z_ref[...].astype(F32)
    zh = halo_ref[...].astype(F32)
    uh = zh[:, :CONV_WIDTH] * _sigmoid(zh[:, CONV_WIDTH:])
    rows = CONV_HALO + ts
    u = jnp.concatenate([jnp.where(j > 0, uh, 0.0),
                         z[:, :CONV_WIDTH] * _sigmoid(z[:, CONV_WIDTH:])], axis=0)
    ubuf[0] = u
    for s in range(1, SUBLANES):
        ubuf[s] = pltpu.roll(u, rows - s, axis=0)
    sub = 64
    first = CONV_HALO - (CONV_KERNEL - 1)
    for r in range(ts // sub):
        acc = jnp.zeros((sub, CONV_WIDTH), F32) + cb_ref[...]
        for k in range(CONV_KERNEL):
            s = (first + k) % SUBLANES
            base = r * sub + first + k - s
            acc = acc + w_ref[k:k + 1, :] * ubuf[s, base:base + sub, :]
        mu = jnp.mean(acc, axis=-1, keepdims=True)
        dlt = acc - mu
        var = jnp.mean(dlt * dlt, axis=-1, keepdims=True)
        y = dlt * lax.rsqrt(var + LN_EPS) * g_ref[...] + b_ref[...]
        o_ref[r * sub:(r + 1) * sub, :] = (y * _sigmoid(y)).astype(o_ref.dtype)


def _conv(zc, conv_w, conv_b, ln_g, ln_b, nb, seq, ts=256):
    n = zc.shape[0]
    tps = seq // ts
    hpt = ts // CONV_HALO
    vec = pl.BlockSpec((1, CONV_WIDTH), lambda b, j: (0, 0))
    return pl.pallas_call(
        functools.partial(_conv_kernel, ts=ts),
        grid=(nb, tps),
        in_specs=[
            pl.BlockSpec((ts, COL_CONV), lambda b, j: (b * tps + j, 0)),
            pl.BlockSpec((CONV_HALO, COL_CONV), lambda b, j: (jnp.maximum((b * tps + j) * hpt - 1, 0), 0)),
            pl.BlockSpec((CONV_KERNEL, CONV_WIDTH), lambda b, j: (0, 0)),
            vec, vec, vec,
        ],
        out_specs=pl.BlockSpec((ts, CONV_WIDTH), lambda b, j: (b * tps + j, 0)),
        out_shape=jax.ShapeDtypeStruct((n, CONV_WIDTH), BF16),
        scratch_shapes=[pltpu.VMEM((SUBLANES, CONV_HALO + ts, CONV_WIDTH), F32)],
        compiler_params=_params("parallel", "parallel"),
        name="conv",
    )(zc, zc, conv_w, conv_b, ln_g, ln_b)


def _prep_kernel(z_ref, halo_ref, mu_ref, w0_ref, a0_ref, wah_ref, wal_ref, g2_ref, kk_ref, ka_ref, rk_ref,
                 ones_ref, tri_ref, csel_ref,
                 at_ref, bt_ref, kt_ref, rt_ref, v_ref, bts_ref, kts_ref, bonus_ref, g_ref, pc_ref):
    j = pl.program_id(1)
    z = z_ref[...]
    ts = z.shape[0]
    last = jnp.where(j > 0, halo_ref[SHIFT_HALO - 1:SHIFT_HALO, :], 0.0)
    row = lax.broadcasted_iota(jnp.int32, z.shape, 0)
    prev = jnp.where(row == 0, last, pltpu.roll(z, 1, axis=0))
    zs = z + (prev - z) * mu_ref[...]
    rw = RWKV_WIDTH
    r = zs[:, :rw]
    k = zs[:, rw:2 * rw]
    v = zs[:, 2 * rw:3 * rw]
    wad = zs[:, 3 * rw:3 * rw + LORA_W + LORA_A]
    gd = zs[:, 3 * rw + LORA_W + LORA_A:]
    lane = lax.broadcasted_iota(jnp.int32, wad.shape, 1)
    xh, xl = _split(jnp.where(lane < LORA_W, jnp.tanh(wad), wad))
    wah = wah_ref[...]
    lora = _dot(xh, wah) + _dot(xl, wah) + _dot(xh, wal_ref[...])
    w = -_softplus(-(w0_ref[...] + lora[:, :rw])) - 0.5
    lw = -jnp.exp(w)
    a = _sigmoid(a0_ref[...] + lora[:, rw:])
    g_ref[...] = _dot(_sigmoid(gd).astype(BF16), g2_ref[...]).astype(g_ref.dtype)
    kk = k * kk_ref[...]
    k2 = k * (1.0 + (a - 1.0) * ka_ref[...])
    ones = ones_ref[...]
    kk = kk * lax.rsqrt(jnp.maximum(_dot_exact_rhs(kk * kk, ones), 1e-24))
    bonus_ref[...] = (_dot_exact_rhs(r * k2 * rk_ref[...], ones) * v).astype(bonus_ref.dtype)
    tri = tri_ref[...]
    lc, tot = [], []
    for h in range(ts // CUM_BLOCK):
        cs = _dot_exact_lhs(tri, lw[h * CUM_BLOCK:(h + 1) * CUM_BLOCK])
        lc.append(cs[:CUM_BLOCK])
        tot.append(cs[CUM_BLOCK:])
    lc = jnp.concatenate(lc, axis=0)
    tot = jnp.concatenate(tot, axis=0)
    e_neg = jnp.exp(-lc)
    e_end = jnp.exp(tot - lc)
    kka = kk * a
    dt = at_ref.dtype
    at_ref[...] = (-kk * jnp.exp(lc - lw)).astype(dt)
    bt_ref[...] = (kka * e_neg).astype(dt)
    kt_ref[...] = (k2 * e_neg).astype(dt)
    rt_ref[...] = (r * jnp.exp(lc)).astype(dt)
    v_ref[...] = v.astype(dt)
    bts_ref[...] = (kka * e_end).astype(dt)
    kts_ref[...] = (k2 * e_end).astype(dt)
    pc_ref[...] = jnp.exp(_dot_exact_lhs(csel_ref[...], lw))


def _prep(zr, mu, w0, a0, wa, g2, k_k, k_a, r_k, nb, seq, ts=512):
    n = zr.shape[0]
    tps = seq // ts
    hpt = ts // SHIFT_HALO
    rw = RWKV_WIDTH
    cpt = ts // CHUNK
    ones = jnp.kron(jnp.eye(N_HEADS, dtype=F32), jnp.ones((HEAD, HEAD), F32)).astype(BF16)
    blk = jnp.kron(jnp.eye(CUM_BLOCK // CHUNK, dtype=F32), jnp.ones((CHUNK, CHUNK), F32))
    tri = jnp.concatenate([jnp.tril(blk), blk], axis=0).astype(BF16)
    csel = jnp.kron(jnp.eye(cpt, dtype=F32), jnp.ones((1, CHUNK), F32)).astype(BF16)
    wa_hi, wa_lo = _split(wa)
    vec = pl.BlockSpec((1, rw), lambda b, j: (0, 0))
    full = lambda shape: pl.BlockSpec(shape, lambda b, j: (0, 0))
    rows = pl.BlockSpec((ts, rw), lambda b, j: (b * tps + j, 0))
    return pl.pallas_call(
        _prep_kernel,
        grid=(nb, tps),
        in_specs=[
            pl.BlockSpec((ts, COL_RWKV), lambda b, j: (b * tps + j, 0)),
            pl.BlockSpec((SHIFT_HALO, COL_RWKV), lambda b, j: (jnp.maximum((b * tps + j) * hpt - 1, 0), 0)),
            full((1, COL_RWKV)), vec, vec,
            full((LORA_W + LORA_A, 2 * rw)), full((LORA_W + LORA_A, 2 * rw)), full((LORA_G, rw)),
            vec, vec, vec, full((rw, rw)), full((2 * CUM_BLOCK, CUM_BLOCK)), full((cpt, ts)),
        ],
        out_specs=[rows] * 9 + [pl.BlockSpec((cpt, rw), lambda b, j: (b * tps + j, 0))],
        out_shape=[jax.ShapeDtypeStruct((n, rw), BF16)] * 9 + [jax.ShapeDtypeStruct((n // CHUNK, rw), F32)],
        compiler_params=_params("parallel", "parallel"),
        name="prep",
    )(zr, zr, mu, w0, a0, wa_hi, wa_lo, g2.astype(BF16), k_k, k_a, r_k, ones, tri, csel)


def _mm(a, b):
    return jnp.dot(a.astype(BF16), b.astype(BF16), preferred_element_type=F32)


def _mm_nt(a, b):
    return lax.dot_general(a.astype(BF16), b.astype(BF16), (((1,), (1,)), ((), ())),
                           preferred_element_type=F32)


def _mm_tn(a, b):
    return lax.dot_general(a.astype(BF16), b.astype(BF16), (((0,), (0,)), ((), ())),
                           preferred_element_type=F32)


def _scan_kernel(at_ref, bt_ref, kt_ref, rt_ref, v_ref, bts_ref, kts_ref, bonus_ref, g_ref, pc_ref,
                 gng_ref, gnb_ref, ones_ref, o_ref, s_ref, y_ref, *, ts, gb):
    @pl.when(pl.program_id(1) == 0)
    def _():
        s_ref[...] = jnp.zeros_like(s_ref)

    row_q = lax.broadcasted_iota(jnp.int32, (QUAD, QUAD), 0)
    col_q = lax.broadcasted_iota(jnp.int32, (QUAD, QUAD), 1)
    bdmask = (row_q // HEAD) == (col_q // HEAD)
    row_c = lax.broadcasted_iota(jnp.int32, (CHUNK, QUAD), 0)
    col_c = lax.broadcasted_iota(jnp.int32, (CHUNK, QUAD), 1) % CHUNK
    strict = col_c < row_c
    incl = col_c <= row_c
    eye = jnp.where(col_c == row_c, 1.0, 0.0).astype(F32)

    def bd(x):
        xb = x.astype(BF16)
        return jnp.where(bdmask, jnp.concatenate([xb] * (QUAD // CHUNK), axis=0), jnp.zeros((), BF16))

    chains = [(b, qd) for b in range(gb) for qd in range(N_QUADS)]

    def chunk(c, carry):
        sl = pl.ds(pl.multiple_of(c * CHUNK, CHUNK), CHUNK)
        lanes = [slice(qd * QUAD, (qd + 1) * QUAD) for _, qd in chains]
        ld = lambda ref: [ref[b, sl, ln] for (b, _), ln in zip(chains, lanes)]
        at, bt, kt, rt, v, bts, kts = (ld(r) for r in (at_ref, bt_ref, kt_ref, rt_ref, v_ref, bts_ref, kts_ref))
        each = range(len(chains))
        ar = [jnp.concatenate([at[i], rt[i]], axis=0) for i in each]
        pb = [_mm_nt(ar[i], bd(bt[i])) for i in each]
        pk = [_mm_nt(ar[i], bd(kt[i])) for i in each]
        l_ab = [jnp.where(strict, p[:CHUNK], 0.0) for p in pb]
        a_rb = [jnp.where(incl, p[CHUNK:], 0.0) for p in pb]
        l_ak = [jnp.where(strict, p[:CHUNK], 0.0) for p in pk]
        a_rk = [jnp.where(incl, p[CHUNK:], 0.0) for p in pk]
        tm = [eye + l for l in l_ab]
        lp = [_mm(l, bd(l)) for l in l_ab]
        for _ in range(4):
            both = [_mm(jnp.concatenate([tm[i], lp[i]], axis=0), bd(lp[i])) for i in each]
            tm = [tm[i] + both[i][:CHUNK] for i in each]
            lp = [both[i][CHUNK:] for i in each]
        tm = [tm[i] + _mm(tm[i], bd(lp[i])) for i in each]
        bdv = [bd(x) for x in v]
        x0 = [_mm(l_ak[i], bdv[i]) for i in each]
        tu = [_mm(tm[i], jnp.concatenate([bd(x0[i]), bd(at[i])], axis=1)) for i in each]
        u0 = [t[:, :QUAD] for t in tu]
        w = [t[:, QUAD:] for t in tu]
        qy = [_mm(a_rb[i], jnp.concatenate([bd(w[i]), bd(u0[i])], axis=1)) for i in each]
        y0 = [qy[i][:, QUAD:] + _mm(a_rk[i], bdv[i]) for i in each]
        q = [rt[i].astype(F32) + qy[i][:, :QUAD] for i in each]
        s = [s_ref[i] for i in each]
        qw_s = [_mm_nt(jnp.concatenate([q[i], w[i]], axis=0), s[i]) for i in each]
        for i, ((b, _), ln) in enumerate(zip(chains, lanes)):
            y_ref[b, sl, ln] = y0[i] + qw_s[i][:CHUNK]
        u = [(u0[i] + qw_s[i][CHUNK:]).astype(BF16) for i in each]
        n_p = [jnp.where(bdmask, _mm_tn(jnp.concatenate([u[i], v[i]], axis=0),
                                        jnp.concatenate([bts[i], kts[i]], axis=0)), 0.0) for i in each]
        for i, ((b, _), ln) in enumerate(zip(chains, lanes)):
            s_ref[i] = s[i] * pc_ref[b, pl.ds(c, 1), ln] + n_p[i]
        return carry

    lax.fori_loop(0, ts // CHUNK, chunk, 0)

    ones = ones_ref[...]
    for b in range(gb):
        for qd in range(N_QUADS):
            ln = slice(qd * QUAD, (qd + 1) * QUAD)
            y = y_ref[b, :, ln]
            dlt = y - _dot_exact_rhs(y, ones) * (1.0 / HEAD)
            var = _dot_exact_rhs(dlt * dlt, ones) * (1.0 / HEAD)
            yn = dlt * lax.rsqrt(var + GN_EPS) * gng_ref[:, ln] + gnb_ref[:, ln]
            o_ref[b, :, ln] = ((yn + bonus_ref[b, :, ln].astype(F32)) * g_ref[b, :, ln].astype(F32)).astype(o_ref.dtype)


def _scan(prep_out, gn_g, gn_b, nb, seq, ts=256, gb=4):
    n = prep_out[0].shape[0]
    rw = RWKV_WIDTH
    cpt = ts // CHUNK
    ones = jnp.kron(jnp.eye(QUAD // HEAD, dtype=F32), jnp.ones((HEAD, HEAD), F32)).astype(BF16)
    rows = pl.BlockSpec((gb, ts, rw), lambda i, t: (i, t, 0))
    vec = pl.BlockSpec((1, rw), lambda i, t: (0, 0))
    args = [a.reshape(nb, seq, rw) for a in prep_out[:9]] + [prep_out[9].reshape(nb, seq // ts, cpt, rw)]
    out = pl.pallas_call(
        functools.partial(_scan_kernel, ts=ts, gb=gb),
        grid=(nb // gb, seq // ts),
        in_specs=[rows] * 9 + [pl.BlockSpec((gb, None, cpt, rw), lambda i, t: (i, t, 0, 0)), vec, vec,
                               pl.BlockSpec((QUAD, QUAD), lambda i, t: (0, 0))],
        out_specs=rows,
        out_shape=jax.ShapeDtypeStruct((nb, seq, rw), BF16),
        scratch_shapes=[pltpu.VMEM((gb * N_QUADS, QUAD, QUAD), F32), pltpu.VMEM((gb, ts, rw), F32)],
        compiler_params=_params("parallel", "arbitrary"),
        name="scan",
    )(*args, gn_g, gn_b, ones)
    return out.reshape(n, rw)


def _merge_kernel(uc_ref, ur_ref, zg_ref, x_ref, pc_ref, pr_ref, wo_ref, gpost_ref, gt_ref,
                  gpre_ref, sc_ref, sh_ref, wrh_ref, wrl_ref, xo_ref, h_ref, lg_ref):
    yc = _dot(uc_ref[...], pc_ref[...])
    yr = _dot(ur_ref[...], pr_ref[...])
    zg = zg_ref[...].astype(F32)
    m = _sigmoid(zg[:, :D_MODEL]) * yc + _sigmoid(zg[:, D_MODEL:]) * yr
    y = _dot(m.astype(BF16), wo_ref[...])
    xn = x_ref[...] + gt_ref[...] * _rms(y, gpost_ref[...])
    xo_ref[...] = xn
    h = _rms(xn, gpre_ref[...]) * (1.0 + sc_ref[...]) + sh_ref[...]
    h_ref[...] = _pack_rows(h)
    h_hi, h_lo = _split(h)
    nt = lambda a, b: lax.dot_general(a, b, (((1,), (1,)), ((), ())), preferred_element_type=F32)
    w_hi = wrh_ref[...]
    lg_ref[...] = nt(w_hi, h_hi) + nt(w_hi, h_lo) + nt(wrl_ref[...], h_hi)


def _merge(uc, ur, zg, x2, p_conv_b, p_rwkv_b, w_o_b, g_post, g_pre, mod, layer, w_router_t, seq, tm=512):
    n, d = x2.shape
    row = lambda i: (i, 0)
    full = lambda shape: pl.BlockSpec(shape, lambda i: (0, 0))
    return pl.pallas_call(
        _merge_kernel,
        grid=(n // tm,),
        in_specs=[
            pl.BlockSpec((tm, CONV_WIDTH), row), pl.BlockSpec((tm, RWKV_WIDTH), row),
            pl.BlockSpec((tm, COL_GATE), row), pl.BlockSpec((tm, d), row),
            full((CONV_WIDTH, d)), full((RWKV_WIDTH, d)), full((d, d)),
            full((1, d)), _mod_spec(layer, 2, seq, tm),
            full((1, d)), _mod_spec(layer, 4, seq, tm), _mod_spec(layer, 3, seq, tm),
            full((N_EXPERTS, d)), full((N_EXPERTS, d)),
        ],
        out_specs=[pl.BlockSpec((tm, d), row), pl.BlockSpec((tm, ROW_WORDS), row),
                   pl.BlockSpec((N_EXPERTS, tm), lambda i: (0, i))],
        out_shape=[jax.ShapeDtypeStruct((n, d), F32), jax.ShapeDtypeStruct((n, ROW_WORDS), jnp.int32),
                   jax.ShapeDtypeStruct((N_EXPERTS, n), F32)],
        compiler_params=_params("parallel"),
        name="merge",
    )(uc, ur, zg, x2, p_conv_b, p_rwkv_b, w_o_b, g_post, mod, g_pre, mod, mod, *_split(w_router_t))


def _route_kernel(lg_ref, b_ref, gate_ref, sel_ref, cnt_ref):
    s = _sigmoid(lg_ref[...])
    biased = s + b_ref[...]
    t = s.shape[1]
    member = lax.broadcasted_iota(jnp.int32, (GROUP_SIZE, t), 0)
    grp = []
    for g in range(N_GROUPS):
        bg = biased[g * GROUP_SIZE:(g + 1) * GROUP_SIZE, :]
        m1 = jnp.max(bg, axis=0, keepdims=True)
        first = jnp.min(jnp.where(bg == m1, member, GROUP_SIZE), axis=0, keepdims=True)
        m2 = jnp.max(jnp.where(member == first, -jnp.inf, bg), axis=0, keepdims=True)
        grp.append(m1 + m2)
    masked = []
    for g in range(N_GROUPS):
        rank = jnp.zeros((1, t), jnp.int32)
        for o in range(N_GROUPS):
            if o == g:
                continue
            ahead = (grp[o] > grp[g]) if o > g else (grp[o] >= grp[g])
            rank = rank + jnp.where(ahead, 1, 0)
        keep = rank < TOPK_GROUPS
        masked.append(jnp.where(keep, biased[g * GROUP_SIZE:(g + 1) * GROUP_SIZE, :], -jnp.inf))
    masked = jnp.concatenate(masked, axis=0)
    eidx = lax.broadcasted_iota(jnp.int32, masked.shape, 0)
    rank = jnp.zeros(masked.shape, jnp.int32)
    for o in range(N_EXPERTS):
        ro = masked[o:o + 1, :]
        rank = rank + jnp.where(ro > masked, 1, jnp.where((ro == masked) & (eidx > o), 1, 0))
    chosen = rank < TOP_K
    sel = jnp.where(chosen, s, 0.0)
    gate_ref[...] = sel / jnp.sum(sel, axis=0, keepdims=True) * ROUTED_SCALE
    mask = jnp.where(chosen, 1.0, 0.0)
    sel_ref[...] = mask.astype(sel_ref.dtype)

    @pl.when(pl.program_id(0) == 0)
    def _():
        cnt_ref[...] = jnp.zeros_like(cnt_ref)

    cnt_ref[...] += jnp.sum(mask, axis=1, keepdims=True)


def _route(logits_t, b_router, row0, n, tt=512):
    e = logits_t.shape[0]
    first = row0 // tt
    return pl.pallas_call(
        _route_kernel,
        grid=(n // tt,),
        in_specs=[pl.BlockSpec((e, tt), lambda i: (0, i + first)), pl.BlockSpec((e, 1), lambda i: (0, 0))],
        out_specs=[pl.BlockSpec((e, tt), lambda i: (0, i)), pl.BlockSpec((e, tt), lambda i: (0, i)),
                   pl.BlockSpec((e, GATE_LANES), lambda i: (0, 0))],
        out_shape=[jax.ShapeDtypeStruct((e, n), F32), jax.ShapeDtypeStruct((e, n), BF16),
                   jax.ShapeDtypeStruct((e, GATE_LANES), F32)],
        compiler_params=_params("arbitrary"),
        name="route",
    )(logits_t, b_router)


def _dest_kernel(sel_ref, gate_ref, start_ref, triu_ref, below_ref, dest_ref, g8_ref, run_ref):
    @pl.when(pl.program_id(0) == 0)
    def _():
        run_ref[...] = jnp.zeros_like(run_ref)

    sel = sel_ref[...]
    t = sel.shape[1]
    self32 = sel.astype(F32)
    incl = _dot(sel, triu_ref[...])
    pos = start_ref[...] + run_ref[...] + incl - self32
    run_ref[...] += incl[:, t - 1:t]
    slot = _dot(below_ref[...], sel)
    gate = gate_ref[...]
    dst, gts = [], []
    for k in range(TOP_K):
        mine = (self32 > 0.0) & (slot == float(k))
        dst.append(jnp.sum(jnp.where(mine, pos, 0.0), axis=0, keepdims=True))
        gts.append(jnp.sum(jnp.where(mine, gate, 0.0), axis=0, keepdims=True))
    dest_ref[...] = jnp.concatenate(dst, axis=0).astype(jnp.int32)
    g8_ref[...] = jnp.concatenate(gts + [jnp.zeros((GATE_LANES - TOP_K, t), F32)], axis=0).T


def _dest(sel, gate, seg_start, tt=512):
    e, n = sel.shape
    triu = jnp.triu(jnp.ones((tt, tt), F32)).astype(BF16)
    below = jnp.tril(jnp.ones((e, e), F32), -1).astype(BF16)
    return pl.pallas_call(
        _dest_kernel,
        grid=(n // tt,),
        in_specs=[pl.BlockSpec((e, tt), lambda i: (0, i)), pl.BlockSpec((e, tt), lambda i: (0, i)),
                  pl.BlockSpec((e, 1), lambda i: (0, 0)), pl.BlockSpec((tt, tt), lambda i: (0, 0)),
                  pl.BlockSpec((e, e), lambda i: (0, 0))],
        out_specs=[pl.BlockSpec((TOP_K, tt), lambda i: (0, i)), pl.BlockSpec((tt, GATE_LANES), lambda i: (i, 0))],
        out_shape=[jax.ShapeDtypeStruct((TOP_K, n), jnp.int32), jax.ShapeDtypeStruct((n, GATE_LANES), F32)],
        scratch_shapes=[pltpu.VMEM((e, 1), F32)],
        compiler_params=_params("arbitrary"),
        name="dest",
    )(sel, gate, seg_start, triu, below)


def _sc_mesh():
    return plsc.VectorSubcoreMesh(core_axis_name="c", subcore_axis_name="s",
                                  num_cores=SC_CORES, num_subcores=SC_SUBCORES)


def _worker_id():
    return lax.axis_index("s") * SC_CORES + lax.axis_index("c")


def _sc_scatter_rows(rows, row0, dest_flat, n_out):
    n = dest_flat.shape[0] // TOP_K
    w = rows.shape[1]
    per_worker = n // SC_WORKERS

    def body(rows_hbm, dest_hbm, out_hbm, idx_v, rows_v):
        base = _worker_id() * per_worker

        @pl.loop(0, per_worker // SC_CHUNK)
        def _(c):
            t0 = base + c * SC_CHUNK
            pltpu.sync_copy(rows_hbm.at[pl.ds(row0 + t0, SC_CHUNK)], rows_v)
            for k in range(TOP_K):
                pltpu.sync_copy(dest_hbm.at[pl.ds(k * n + t0, SC_CHUNK)], idx_v)
                pltpu.sync_copy(rows_v, out_hbm.at[idx_v])

    return pl.kernel(
        body, out_type=jax.ShapeDtypeStruct((n_out, w), rows.dtype), mesh=_sc_mesh(),
        scratch_types=[pltpu.VMEM((SC_CHUNK,), jnp.int32), pltpu.VMEM((SC_CHUNK, w), rows.dtype)],
        name="sc_scatter",
    )(rows, dest_flat)


def _sc_gather_rows(table, idx_flat):
    m = idx_flat.shape[0]
    w = table.shape[1]
    per_worker = m // SC_WORKERS

    def body(table_hbm, idx_hbm, out_hbm, idx_v, rows_v):
        base = _worker_id() * per_worker

        @pl.loop(0, per_worker // SC_CHUNK)
        def _(c):
            j0 = base + c * SC_CHUNK
            pltpu.sync_copy(idx_hbm.at[pl.ds(j0, SC_CHUNK)], idx_v)
            pltpu.sync_copy(table_hbm.at[idx_v], rows_v)
            pltpu.sync_copy(rows_v, out_hbm.at[pl.ds(j0, SC_CHUNK)])

    return pl.kernel(
        body, out_type=jax.ShapeDtypeStruct((m, w), table.dtype), mesh=_sc_mesh(),
        scratch_types=[pltpu.VMEM((SC_CHUNK,), jnp.int32), pltpu.VMEM((SC_CHUNK, w), table.dtype)],
        name="sc_gather",
    )(table, idx_flat)


def _swiglu(h, wg, wu):
    gte = _dot(h, wg)
    return gte * _sigmoid(gte) * _dot(h, wu)


def _experts_kernel(be_ref, va_ref, xs_ref, wg_ref, wu_ref, wd_ref, ys_ref, wg_b, wu_b, wd_b):
    s = pl.program_id(0)
    valid = va_ref[s]
    fresh = jnp.logical_or(s == 0, be_ref[s] != be_ref[jnp.maximum(s - 1, 0)])

    @pl.when(jnp.logical_and(valid > 0, fresh))
    def _():
        wg_b[...] = wg_ref[...].astype(BF16)
        wu_b[...] = wu_ref[...].astype(BF16)
        wd_b[...] = wd_ref[...].astype(BF16)

    @pl.when(valid > 0)
    def _():
        nsub = EXPERT_BLOCK // EXPERT_SUB
        row = lax.broadcasted_iota(jnp.int32, (EXPERT_SUB, D_MODEL), 0)
        x = []
        for i in range(nsub):
            xi = _unpack_rows(xs_ref[i * EXPERT_SUB:(i + 1) * EXPERT_SUB, :])
            x.append(jnp.where(row < valid - i * EXPERT_SUB, xi, 0.0).astype(BF16))
        gte = [_dot(xi, wg_b[...]) for xi in x]
        up = [_dot(xi, wu_b[...]) for xi in x]
        act = [(g * _sigmoid(g) * u).astype(BF16) for g, u in zip(gte, up)]
        y = [_dot(a, wd_b[...]) for a in act]
        for i in range(nsub):
            ys_ref[i * EXPERT_SUB:(i + 1) * EXPERT_SUB, :] = _pack_rows(y[i])


def _experts(xs, blk_expert, blk_valid, wg, wu, wd, layer):
    n_rows, w = xs.shape
    d = wg.shape[2]
    rows = pl.BlockSpec((EXPERT_BLOCK, w), lambda s, be, va: (s, 0))
    up_spec = pl.BlockSpec((None, None, d, D_EXPERT), lambda s, be, va: (layer, be[s], 0, 0))
    down_spec = pl.BlockSpec((None, None, D_EXPERT, d), lambda s, be, va: (layer, be[s], 0, 0))
    return pl.pallas_call(
        _experts_kernel,
        grid_spec=pltpu.PrefetchScalarGridSpec(
            num_scalar_prefetch=2,
            grid=(n_rows // EXPERT_BLOCK,),
            in_specs=[rows, up_spec, up_spec, down_spec],
            out_specs=rows,
            scratch_shapes=[pltpu.VMEM((d, D_EXPERT), BF16), pltpu.VMEM((d, D_EXPERT), BF16),
                            pltpu.VMEM((D_EXPERT, d), BF16)],
        ),
        out_shape=jax.ShapeDtypeStruct((n_rows, w), xs.dtype),
        compiler_params=_params("arbitrary"),
        name="experts",
    )(blk_expert, blk_valid, xs, wg, wu, wd)


def _combine_kernel(yg_ref, g8_ref, hp_ref, sg_ref, su_ref, sd_ref, x_ref, gpost_ref, gt_ref, *rest):
    o_ref = rest[-1]
    h = _unpack_rows(hp_ref[...]).astype(BF16)
    acc = _dot(_swiglu(h, sg_ref[...], su_ref[...]).astype(BF16), sd_ref[...])
    g8 = g8_ref[...]
    for k in range(TOP_K):
        acc = acc + g8[:, k:k + 1] * _unpack_rows(yg_ref[k])
    o_ref[...] = x_ref[...] + gt_ref[...] * _rms(acc, gpost_ref[...])


def _combine(yg, g8, hp, sg, su, sd, x2, g_post, mod, layer, seq, row0, earlier, tm=512):
    n, d = x2.shape
    part = g8.shape[0]
    first = row0 // tm
    local = lambda i: (i, 0)
    glob = lambda i: (i + first, 0)
    full = lambda shape: pl.BlockSpec(shape, lambda i: (0, 0))
    in_specs = [
        pl.BlockSpec((TOP_K, tm, ROW_WORDS), lambda i: (0, i, 0)),
        pl.BlockSpec((tm, GATE_LANES), local),
        pl.BlockSpec((tm, ROW_WORDS), glob),
        full((d, D_EXPERT)), full((d, D_EXPERT)), full((D_EXPERT, d)),
        pl.BlockSpec((tm, d), glob),
        full((1, d)),
        _mod_spec(layer, 5, seq, tm, first),
    ]
    args = [yg, g8, hp, sg, su, sd, x2, g_post, mod]
    aliases = {}
    if earlier is not None:
        in_specs.append(pl.BlockSpec(memory_space=pl.ANY))
        args.append(earlier)
        aliases = {len(args) - 1: 0}
    return pl.pallas_call(
        _combine_kernel,
        grid=(part // tm,),
        in_specs=in_specs,
        out_specs=pl.BlockSpec((tm, d), glob),
        out_shape=jax.ShapeDtypeStruct((n, d), F32),
        input_output_aliases=aliases,
        compiler_params=_params("parallel"),
        name="combine",
    )(*args)


def _moe(hp, logits_t, b_router, wg, wu, wd, sg, su, sd, x2, g_post, mod, layer, seq):
    n = x2.shape[0]
    part = n // MOE_PARTS
    out = None
    for p in range(MOE_PARTS):
        row0 = p * part
        gate, sel, counts = _route(logits_t, b_router, row0, part)
        cnt = counts[:, 0].astype(jnp.int32)
        padded = (cnt + EXPERT_BLOCK - 1) // EXPERT_BLOCK * EXPERT_BLOCK
        seg_end = jnp.cumsum(padded)
        seg_start = seg_end - padded
        n_blocks = part * TOP_K // EXPERT_BLOCK + N_EXPERTS
        blk_row = jnp.arange(n_blocks, dtype=jnp.int32) * EXPERT_BLOCK
        blk_expert = jnp.sum((seg_end[None, :] <= blk_row[:, None]).astype(jnp.int32), axis=1)
        blk_expert = jnp.minimum(blk_expert, N_EXPERTS - 1)
        blk_valid = jnp.clip(cnt[blk_expert] - (blk_row - seg_start[blk_expert]), 0, EXPERT_BLOCK)
        dest, g8 = _dest(sel, gate, seg_start.astype(F32).reshape(-1, 1))
        dest_flat = dest.reshape(-1)
        xs = _sc_scatter_rows(hp, row0, dest_flat, n_blocks * EXPERT_BLOCK)
        ys = _experts(xs, blk_expert, blk_valid.astype(jnp.int32), wg, wu, wd, layer)
        yg = _sc_gather_rows(ys, dest_flat).reshape(TOP_K, part, ROW_WORDS)
        out = _combine(yg, g8, hp, sg, su, sd, x2, g_post, mod, layer, seq, row0, out)
    return out


def kernel(x, c, w_ada, b_ada, norm_mix_pre, norm_mix_post, norm_ffn_pre, norm_ffn_post, w_in, mu_shift, conv_w, conv_b, conv_ln_g, conv_ln_b, p_conv, w0, w2, a0, a2, g2, k_k, k_a, r_k, gn_g, gn_b, p_rwkv, w_o, w_router, b_router, we_gate, we_up, we_down, ws_gate, ws_up, ws_down):
    nb, seq, d = x.shape
    depth = w_ada.shape[0]
    n = nb * seq
    rw = RWKV_WIDTH
    mod = _ada(c, w_ada, b_ada)
    x2 = x.reshape(n, d)
    row = lambda a: a.reshape(1, -1)
    bf = lambda a: a.astype(BF16)
    for l in range(depth):
        zc, zr, zg = _win(x2, row(norm_mix_pre[l]), mod, l, bf(w_in[l]), seq)
        uc = _conv(zc, conv_w[l], row(conv_b[l]), row(conv_ln_g[l]), row(conv_ln_b[l]), nb, seq)
        wa = jnp.zeros((LORA_W + LORA_A, 2 * rw), F32)
        wa = wa.at[:LORA_W, :rw].set(w2[l]).at[LORA_W:, rw:].set(a2[l])
        prep_out = _prep(zr, row(mu_shift[l]), row(w0[l]), row(a0[l]), wa, g2[l], row(k_k[l]),
                         row(k_a[l]), row(r_k[l]), nb, seq)
        ur = _scan(prep_out, row(gn_g[l]), row(gn_b[l]), nb, seq)
        x2, hp, logits_t = _merge(uc, ur, zg, x2, bf(p_conv[l]), bf(p_rwkv[l]), bf(w_o[l]),
                                  row(norm_mix_post[l]), row(norm_ffn_pre[l]), mod, l, w_router[l].T, seq)
        x2 = _moe(hp, logits_t, b_router[l].reshape(-1, 1), we_gate, we_up, we_down,
                  bf(ws_gate[l]), bf(ws_up[l]), bf(ws_down[l]), x2, row(norm_ffn_post[l]), mod, l, seq)
    return x2.reshape(nb, seq, d)
```

```python
import functools

import jax
import jax.numpy as jnp
from jax import lax
from jax.experimental import pallas as pl
from jax.experimental.pallas import tpu as pltpu
from jax.experimental.pallas import tpu_sc as plsc

F32 = jnp.float32
BF16 = jnp.bfloat16
HI = lax.Precision.HIGHEST

D_MODEL = 1024
CONV_WIDTH = 512
CONV_KERNEL = 31
RWKV_WIDTH = 512
HEAD = 64
N_HEADS = RWKV_WIDTH // HEAD
LORA_W = 64
LORA_A = 64
LORA_G = 128
N_EXPERTS = 64
TOP_K = 8
N_GROUPS = 8
TOPK_GROUPS = 4
GROUP_SIZE = N_EXPERTS // N_GROUPS
D_EXPERT = 256
ROUTED_SCALE = 2.5
RMS_EPS = 1e-6
LN_EPS = 1e-5
GN_EPS = 64e-5
COL_CONV = 2 * CONV_WIDTH
COL_RWKV = 3 * RWKV_WIDTH + LORA_W + LORA_A + LORA_G
COL_GATE = 2 * D_MODEL
D_IN = COL_CONV + COL_RWKV + COL_GATE

CHUNK = 64
QUAD = 4 * HEAD
N_QUADS = RWKV_WIDTH // QUAD
CUM_BLOCK = 256
CONV_HALO = 32
SUBLANES = 8
SHIFT_HALO = SUBLANES
GATE_LANES = 128
EXPERT_BLOCK = 512
EXPERT_SUB = 256
MOE_PARTS = 1
ROW_WORDS = D_MODEL // 2
SC_CORES = 2
SC_SUBCORES = 16
SC_WORKERS = SC_CORES * SC_SUBCORES
SC_CHUNK = 128
VMEM_LIMIT = 52 * 1024 * 1024


def _params(*sem):
    return pltpu.CompilerParams(dimension_semantics=sem, vmem_limit_bytes=VMEM_LIMIT)


def _sigmoid(x):
    return 1.0 / (1.0 + jnp.exp(-x))


def _softplus(x):
    return jnp.maximum(x, 0.0) + jnp.log(1.0 + jnp.exp(-jnp.abs(x)))


def _rms(x, g):
    return x * lax.rsqrt(jnp.mean(x * x, axis=-1, keepdims=True) + RMS_EPS) * g


def _split(x):
    hi = x.astype(BF16)
    return hi, (x - hi.astype(F32)).astype(BF16)


def _dot(a, b):
    return jnp.dot(a, b, preferred_element_type=F32)


def _dot_exact_rhs(x, w):
    hi, lo = _split(x)
    return _dot(hi, w) + _dot(lo, w)


def _pack_rows(y):
    half = y.shape[1] // 2
    packed = pltpu.pack_elementwise([y[:, half:], y[:, :half]], packed_dtype=BF16)
    return lax.bitcast_convert_type(packed, jnp.int32)


def _unpack_rows(u):
    hi = pltpu.unpack_elementwise(u, index=1, packed_dtype=BF16, unpacked_dtype=F32)
    lo = pltpu.unpack_elementwise(u, index=0, packed_dtype=BF16, unpacked_dtype=F32)
    return jnp.concatenate([hi, lo], axis=1)


def _dot_exact_lhs(w, x):
    hi, lo = _split(x)
    return _dot(w, hi) + _dot(w, lo)


def _ada_kernel(c_ref, w_ref, b_ref, o_ref):
    c = c_ref[...]
    o_ref[...] = jnp.dot(c * _sigmoid(c), w_ref[...], precision=HI,
                         preferred_element_type=F32) + b_ref[...]


def _ada(c, w_ada, b_ada):
    nl, d, _ = w_ada.shape
    nb = c.shape[0]
    out = pl.pallas_call(
        _ada_kernel,
        grid=(nl, 6),
        in_specs=[
            pl.BlockSpec((nb, d), lambda l, k: (0, 0)),
            pl.BlockSpec((None, d, d), lambda l, k: (l, 0, k)),
            pl.BlockSpec((None, None, 1, d), lambda l, k: (l, k, 0, 0)),
        ],
        out_specs=pl.BlockSpec((None, None, nb, d), lambda l, k: (l, k, 0, 0)),
        out_shape=jax.ShapeDtypeStruct((nl, 6, nb, d), F32),
        compiler_params=_params("parallel", "parallel"),
        name="ada",
    )(c, w_ada, b_ada.reshape(nl, 6, 1, d))
    return out.reshape(nl, 6, nb, 1, d)


def _mod_spec(layer, piece, rows_per_batch, tm, first_tile=0):
    return pl.BlockSpec((None, None, None, 1, D_MODEL),
                        lambda i, *_: (layer, piece, ((i + first_tile) * tm) // rows_per_batch, 0, 0))


def _win_kernel(x_ref, g_ref, sc_ref, sh_ref, w_ref, zc_ref, zr_ref, zg_ref):
    h = _rms(x_ref[...], g_ref[...]) * (1.0 + sc_ref[...]) + sh_ref[...]
    hb = h.astype(BF16)
    zc_ref[...] = _dot(hb, w_ref[:, :COL_CONV]).astype(zc_ref.dtype)
    zr_ref[...] = _dot(hb, w_ref[:, COL_CONV:COL_CONV + COL_RWKV])
    zg_ref[...] = _dot(hb, w_ref[:, COL_CONV + COL_RWKV:]).astype(zg_ref.dtype)


def _win(x2, g, mod, layer, w_in_b, seq, tm=512):
    n, d = x2.shape
    row = lambda i: (i, 0)
    return pl.pallas_call(
        _win_kernel,
        grid=(n // tm,),
        in_specs=[
            pl.BlockSpec((tm, d), row),
            pl.BlockSpec((1, d), lambda i: (0, 0)),
            _mod_spec(layer, 1, seq, tm),
            _mod_spec(layer, 0, seq, tm),
            pl.BlockSpec((d, D_IN), lambda i: (0, 0)),
        ],
        out_specs=[pl.BlockSpec((tm, COL_CONV), row), pl.BlockSpec((tm, COL_RWKV), row),
                   pl.BlockSpec((tm, COL_GATE), row)],
        out_shape=[jax.ShapeDtypeStruct((n, COL_CONV), BF16), jax.ShapeDtypeStruct((n, COL_RWKV), F32),
                   jax.ShapeDtypeStruct((n, COL_GATE), BF16)],
        compiler_params=_params("parallel"),
        name="win",
    )(x2, g, mod, mod, w_in_b)


def _conv_kernel(z_ref, halo_ref, w_ref, cb_ref, g_ref, b_ref, o_ref, ubuf, *, ts):
    j = pl.program_id(1)
    z = z_ref[...].astype(F32)
    zh = halo_ref[...].astype(F32)
    uh = zh[:, :CONV_WIDTH] * _sigmoid(zh[:, CONV_WIDTH:])
    rows = CONV_HALO + ts
    u = jnp.concatenate([jnp.where(j > 0, uh, 0.0),
                         z[:, :CONV_WIDTH] * _sigmoid(z[:, CONV_WIDTH:])], axis=0)
    ubuf[0] = u
    for s in range(1, SUBLANES):
        ubuf[s] = pltpu.roll(u, rows - s, axis=0)
    sub = 64
    first = CONV_HALO - (CONV_KERNEL - 1)
    for r in range(ts // sub):
        acc = jnp.zeros((sub, CONV_WIDTH), F32) + cb_ref[...]
        for k in range(CONV_KERNEL):
            s = (first + k) % SUBLANES
            base = r * sub + first + k - s
            acc = acc + w_ref[k:k + 1, :] * ubuf[s, base:base + sub, :]
        mu = jnp.mean(acc, axis=-1, keepdims=True)
        dlt = acc - mu
        var = jnp.mean(dlt * dlt, axis=-1, keepdims=True)
        y = dlt * lax.rsqrt(var + LN_EPS) * g_ref[...] + b_ref[...]
        o_ref[r * sub:(r + 1) * sub, :] = (y * _sigmoid(y)).astype(o_ref.dtype)


def _conv(zc, conv_w, conv_b, ln_g, ln_b, nb, seq, ts=256):
    n = zc.shape[0]
    tps = seq // ts
    hpt = ts // CONV_HALO
    vec = pl.BlockSpec((1, CONV_WIDTH), lambda b, j: (0, 0))
    return pl.pallas_call(
        functools.partial(_conv_kernel, ts=ts),
        grid=(nb, tps),
        in_specs=[
            pl.BlockSpec((ts, COL_CONV), lambda b, j: (b * tps + j, 0)),
            pl.BlockSpec((CONV_HALO, COL_CONV), lambda b, j: (jnp.maximum((b * tps + j) * hpt - 1, 0), 0)),
            pl.BlockSpec((CONV_KERNEL, CONV_WIDTH), lambda b, j: (0, 0)),
            vec, vec, vec,
        ],
        out_specs=pl.BlockSpec((ts, CONV_WIDTH), lambda b, j: (b * tps + j, 0)),
        out_shape=jax.ShapeDtypeStruct((n, CONV_WIDTH), BF16),
        scratch_shapes=[pltpu.VMEM((SUBLANES, CONV_HALO + ts, CONV_WIDTH), F32)],
        compiler_params=_params("parallel", "parallel"),
        name="conv",
    )(zc, zc, conv_w, conv_b, ln_g, ln_b)


def _prep_kernel(z_ref, halo_ref, mu_ref, w0_ref, a0_ref, wah_ref, wal_ref, g2_ref, kk_ref, ka_ref, rk_ref,
                 ones_ref, tri_ref, csel_ref,
                 at_ref, bt_ref, kt_ref, rt_ref, v_ref, bts_ref, kts_ref, bonus_ref, g_ref, pc_ref):
    j = pl.program_id(1)
    z = z_ref[...]
    ts = z.shape[0]
    last = jnp.where(j > 0, halo_ref[SHIFT_HALO - 1:SHIFT_HALO, :], 0.0)
    row = lax.broadcasted_iota(jnp.int32, z.shape, 0)
    prev = jnp.where(row == 0, last, pltpu.roll(z, 1, axis=0))
    zs = z + (prev - z) * mu_ref[...]
    rw = RWKV_WIDTH
    r = zs[:, :rw]
    k = zs[:, rw:2 * rw]
    v = zs[:, 2 * rw:3 * rw]
    wad = zs[:, 3 * rw:3 * rw + LORA_W + LORA_A]
    gd = zs[:, 3 * rw + LORA_W + LORA_A:]
    lane = lax.broadcasted_iota(jnp.int32, wad.shape, 1)
    xh, xl = _split(jnp.where(lane < LORA_W, jnp.tanh(wad), wad))
    wah = wah_ref[...]
    lora = _dot(xh, wah) + _dot(xl, wah) + _dot(xh, wal_ref[...])
    w = -_softplus(-(w0_ref[...] + lora[:, :rw])) - 0.5
    lw = -jnp.exp(w)
    a = _sigmoid(a0_ref[...] + lora[:, rw:])
    g_ref[...] = _dot(_sigmoid(gd).astype(BF16), g2_ref[...]).astype(g_ref.dtype)
    kk = k * kk_ref[...]
    k2 = k * (1.0 + (a - 1.0) * ka_ref[...])
    ones = ones_ref[...]
    kk = kk * lax.rsqrt(jnp.maximum(_dot_exact_rhs(kk * kk, ones), 1e-24))
    bonus_ref[...] = (_dot_exact_rhs(r * k2 * rk_ref[...], ones) * v).astype(bonus_ref.dtype)
    tri = tri_ref[...]
    lc, tot = [], []
    for h in range(ts // CUM_BLOCK):
        cs = _dot_exact_lhs(tri, lw[h * CUM_BLOCK:(h + 1) * CUM_BLOCK])
        lc.append(cs[:CUM_BLOCK])
        tot.append(cs[CUM_BLOCK:])
    lc = jnp.concatenate(lc, axis=0)
    tot = jnp.concatenate(tot, axis=0)
    e_neg = jnp.exp(-lc)
    e_end = jnp.exp(tot - lc)
    kka = kk * a
    dt = at_ref.dtype
    at_ref[...] = (-kk * jnp.exp(lc - lw)).astype(dt)
    bt_ref[...] = (kka * e_neg).astype(dt)
    kt_ref[...] = (k2 * e_neg).astype(dt)
    rt_ref[...] = (r * jnp.exp(lc)).astype(dt)
    v_ref[...] = v.astype(dt)
    bts_ref[...] = (kka * e_end).astype(dt)
    kts_ref[...] = (k2 * e_end).astype(dt)
    pc_ref[...] = jnp.exp(_dot_exact_lhs(csel_ref[...], lw))


def _prep(zr, mu, w0, a0, wa, g2, k_k, k_a, r_k, nb, seq, ts=512):
    n = zr.shape[0]
    tps = seq // ts
    hpt = ts // SHIFT_HALO
    rw = RWKV_WIDTH
    cpt = ts // CHUNK
    ones = jnp.kron(jnp.eye(N_HEADS, dtype=F32), jnp.ones((HEAD, HEAD), F32)).astype(BF16)
    blk = jnp.kron(jnp.eye(CUM_BLOCK // CHUNK, dtype=F32), jnp.ones((CHUNK, CHUNK), F32))
    tri = jnp.concatenate([jnp.tril(blk), blk], axis=0).astype(BF16)
    csel = jnp.kron(jnp.eye(cpt, dtype=F32), jnp.ones((1, CHUNK), F32)).astype(BF16)
    wa_hi, wa_lo = _split(wa)
    vec = pl.BlockSpec((1, rw), lambda b, j: (0, 0))
    full = lambda shape: pl.BlockSpec(shape, lambda b, j: (0, 0))
    rows = pl.BlockSpec((ts, rw), lambda b, j: (b * tps + j, 0))
    return pl.pallas_call(
        _prep_kernel,
        grid=(nb, tps),
        in_specs=[
            pl.BlockSpec((ts, COL_RWKV), lambda b, j: (b * tps + j, 0)),
            pl.BlockSpec((SHIFT_HALO, COL_RWKV), lambda b, j: (jnp.maximum((b * tps + j) * hpt - 1, 0), 0)),
            full((1, COL_RWKV)), vec, vec,
            full((LORA_W + LORA_A, 2 * rw)), full((LORA_W + LORA_A, 2 * rw)), full((LORA_G, rw)),
            vec, vec, vec, full((rw, rw)), full((2 * CUM_BLOCK, CUM_BLOCK)), full((cpt, ts)),
        ],
        out_specs=[rows] * 9 + [pl.BlockSpec((cpt, rw), lambda b, j: (b * tps + j, 0))],
        out_shape=[jax.ShapeDtypeStruct((n, rw), BF16)] * 9 + [jax.ShapeDtypeStruct((n // CHUNK, rw), F32)],
        compiler_params=_params("parallel", "parallel"),
        name="prep",
    )(zr, zr, mu, w0, a0, wa_hi, wa_lo, g2.astype(BF16), k_k, k_a, r_k, ones, tri, csel)


def _mm(a, b):
    return jnp.dot(a.astype(BF16), b.astype(BF16), preferred_element_type=F32)


def _mm_nt(a, b):
    return lax.dot_general(a.astype(BF16), b.astype(BF16), (((1,), (1,)), ((), ())),
                           preferred_element_type=F32)


def _mm_tn(a, b):
    return lax.dot_general(a.astype(BF16), b.astype(BF16), (((0,), (0,)), ((), ())),
                           preferred_element_type=F32)


def _scan_kernel(at_ref, bt_ref, kt_ref, rt_ref, v_ref, bts_ref, kts_ref, bonus_ref, g_ref, pc_ref,
                 gng_ref, gnb_ref, ones_ref, o_ref, s_ref, y_ref, *, ts, gb):
    @pl.when(pl.program_id(1) == 0)
    def _():
        s_ref[...] = jnp.zeros_like(s_ref)

    row_q = lax.broadcasted_iota(jnp.int32, (QUAD, QUAD), 0)
    col_q = lax.broadcasted_iota(jnp.int32, (QUAD, QUAD), 1)
    bdmask = (row_q // HEAD) == (col_q // HEAD)
    row_c = lax.broadcasted_iota(jnp.int32, (CHUNK, QUAD), 0)
    col_c = lax.broadcasted_iota(jnp.int32, (CHUNK, QUAD), 1) % CHUNK
    strict = col_c < row_c
    incl = col_c <= row_c
    eye = jnp.where(col_c == row_c, 1.0, 0.0).astype(F32)

    def bd(x):
        xb = x.astype(BF16)
        return jnp.where(bdmask, jnp.concatenate([xb] * (QUAD // CHUNK), axis=0), jnp.zeros((), BF16))

    chains = [(b, qd) for b in range(gb) for qd in range(N_QUADS)]

    def chunk(c, carry):
        sl = pl.ds(pl.multiple_of(c * CHUNK, CHUNK), CHUNK)
        lanes = [slice(qd * QUAD, (qd + 1) * QUAD) for _, qd in chains]
        ld = lambda ref: [ref[b, sl, ln] for (b, _), ln in zip(chains, lanes)]
        at, bt, kt, rt, v, bts, kts = (ld(r) for r in (at_ref, bt_ref, kt_ref, rt_ref, v_ref, bts_ref, kts_ref))
        each = range(len(chains))
        ar = [jnp.concatenate([at[i], rt[i]], axis=0) for i in each]
        pb = [_mm_nt(ar[i], bd(bt[i])) for i in each]
        pk = [_mm_nt(ar[i], bd(kt[i])) for i in each]
        l_ab = [jnp.where(strict, p[:CHUNK], 0.0) for p in pb]
        a_rb = [jnp.where(incl, p[CHUNK:], 0.0) for p in pb]
        l_ak = [jnp.where(strict, p[:CHUNK], 0.0) for p in pk]
        a_rk = [jnp.where(incl, p[CHUNK:], 0.0) for p in pk]
        tm = [eye + l for l in l_ab]
        lp = [_mm(l, bd(l)) for l in l_ab]
        for _ in range(4):
            both = [_mm(jnp.concatenate([tm[i], lp[i]], axis=0), bd(lp[i])) for i in each]
            tm = [tm[i] + both[i][:CHUNK] for i in each]
            lp = [both[i][CHUNK:] for i in each]
        tm = [tm[i] + _mm(tm[i], bd(lp[i])) for i in each]
        bdv = [bd(x) for x in v]
        x0 = [_mm(l_ak[i], bdv[i]) for i in each]
        tu = [_mm(tm[i], jnp.concatenate([bd(x0[i]), bd(at[i])], axis=1)) for i in each]
        u0 = [t[:, :QUAD] for t in tu]
        w = [t[:, QUAD:] for t in tu]
        qy = [_mm(a_rb[i], jnp.concatenate([bd(w[i]), bd(u0[i])], axis=1)) for i in each]
        y0 = [qy[i][:, QUAD:] + _mm(a_rk[i], bdv[i]) for i in each]
        q = [rt[i].astype(F32) + qy[i][:, :QUAD] for i in each]
        s = [s_ref[i] for i in each]
        qw_s = [_mm_nt(jnp.concatenate([q[i], w[i]], axis=0), s[i]) for i in each]
        for i, ((b, _), ln) in enumerate(zip(chains, lanes)):
            y_ref[b, sl, ln] = y0[i] + qw_s[i][:CHUNK]
        u = [(u0[i] + qw_s[i][CHUNK:]).astype(BF16) for i in each]
        n_p = [jnp.where(bdmask, _mm_tn(jnp.concatenate([u[i], v[i]], axis=0),
                                        jnp.concatenate([bts[i], kts[i]], axis=0)), 0.0) for i in each]
        for i, ((b, _), ln) in enumerate(zip(chains, lanes)):
            s_ref[i] = s[i] * pc_ref[b, pl.ds(c, 1), ln] + n_p[i]
        return carry

    lax.fori_loop(0, ts // CHUNK, chunk, 0)

    ones = ones_ref[...]
    for b in range(gb):
        for qd in range(N_QUADS):
            ln = slice(qd * QUAD, (qd + 1) * QUAD)
            y = y_ref[b, :, ln]
            dlt = y - _dot_exact_rhs(y, ones) * (1.0 / HEAD)
            var = _dot_exact_rhs(dlt * dlt, ones) * (1.0 / HEAD)
            yn = dlt * lax.rsqrt(var + GN_EPS) * gng_ref[:, ln] + gnb_ref[:, ln]
            o_ref[b, :, ln] = ((yn + bonus_ref[b, :, ln].astype(F32)) * g_ref[b, :, ln].astype(F32)).astype(o_ref.dtype)


def _scan(prep_out, gn_g, gn_b, nb, seq, ts=256, gb=4):
    n = prep_out[0].shape[0]
    rw = RWKV_WIDTH
    cpt = ts // CHUNK
    ones = jnp.kron(jnp.eye(QUAD // HEAD, dtype=F32), jnp.ones((HEAD, HEAD), F32)).astype(BF16)
    rows = pl.BlockSpec((gb, ts, rw), lambda i, t: (i, t, 0))
    vec = pl.BlockSpec((1, rw), lambda i, t: (0, 0))
    args = [a.reshape(nb, seq, rw) for a in prep_out[:9]] + [prep_out[9].reshape(nb, seq // ts, cpt, rw)]
    out = pl.pallas_call(
        functools.partial(_scan_kernel, ts=ts, gb=gb),
        grid=(nb // gb, seq // ts),
        in_specs=[rows] * 9 + [pl.BlockSpec((gb, None, cpt, rw), lambda i, t: (i, t, 0, 0)), vec, vec,
                               pl.BlockSpec((QUAD, QUAD), lambda i, t: (0, 0))],
        out_specs=rows,
        out_shape=jax.ShapeDtypeStruct((nb, seq, rw), BF16),
        scratch_shapes=[pltpu.VMEM((gb * N_QUADS, QUAD, QUAD), F32), pltpu.VMEM((gb, ts, rw), F32)],
        compiler_params=_params("parallel", "arbitrary"),
        name="scan",
    )(*args, gn_g, gn_b, ones)
    return out.reshape(n, rw)


def _merge_kernel(uc_ref, ur_ref, zg_ref, x_ref, pc_ref, pr_ref, wo_ref, gpost_ref, gt_ref,
                  gpre_ref, sc_ref, sh_ref, wrh_ref, wrl_ref, xo_ref, h_ref, lg_ref):
    yc = _dot(uc_ref[...], pc_ref[...])
    yr = _dot(ur_ref[...], pr_ref[...])
    zg = zg_ref[...].astype(F32)
    m = _sigmoid(zg[:, :D_MODEL]) * yc + _sigmoid(zg[:, D_MODEL:]) * yr
    y = _dot(m.astype(BF16), wo_ref[...])
    xn = x_ref[...] + gt_ref[...] * _rms(y, gpost_ref[...])
    xo_ref[...] = xn
    h = _rms(xn, gpre_ref[...]) * (1.0 + sc_ref[...]) + sh_ref[...]
    h_ref[...] = _pack_rows(h)
    h_hi, h_lo = _split(h)
    nt = lambda a, b: lax.dot_general(a, b, (((1,), (1,)), ((), ())), preferred_element_type=F32)
    w_hi = wrh_ref[...]
    lg_ref[...] = nt(w_hi, h_hi) + nt(w_hi, h_lo) + nt(wrl_ref[...], h_hi)


def _merge(uc, ur, zg, x2, p_conv_b, p_rwkv_b, w_o_b, g_post, g_pre, mod, layer, w_router_t, seq, tm=512):
    n, d = x2.shape
    row = lambda i: (i, 0)
    full = lambda shape: pl.BlockSpec(shape, lambda i: (0, 0))
    return pl.pallas_call(
        _merge_kernel,
        grid=(n // tm,),
        in_specs=[
            pl.BlockSpec((tm, CONV_WIDTH), row), pl.BlockSpec((tm, RWKV_WIDTH), row),
            pl.BlockSpec((tm, COL_GATE), row), pl.BlockSpec((tm, d), row),
            full((CONV_WIDTH, d)), full((RWKV_WIDTH, d)), full((d, d)),
            full((1, d)), _mod_spec(layer, 2, seq, tm),
            full((1, d)), _mod_spec(layer, 4, seq, tm), _mod_spec(layer, 3, seq, tm),
            full((N_EXPERTS, d)), full((N_EXPERTS, d)),
        ],
        out_specs=[pl.BlockSpec((tm, d), row), pl.BlockSpec((tm, ROW_WORDS), row),
                   pl.BlockSpec((N_EXPERTS, tm), lambda i: (0, i))],
        out_shape=[jax.ShapeDtypeStruct((n, d), F32), jax.ShapeDtypeStruct((n, ROW_WORDS), jnp.int32),
                   jax.ShapeDtypeStruct((N_EXPERTS, n), F32)],
        compiler_params=_params("parallel"),
        name="merge",
    )(uc, ur, zg, x2, p_conv_b, p_rwkv_b, w_o_b, g_post, mod, g_pre, mod, mod, *_split(w_router_t))


def _route_kernel(lg_ref, b_ref, gate_ref, sel_ref, cnt_ref):
    s = _sigmoid(lg_ref[...])
    biased = s + b_ref[...]
    t = s.shape[1]
    member = lax.broadcasted_iota(jnp.int32, (GROUP_SIZE, t), 0)
    grp = []
    for g in range(N_GROUPS):
        bg = biased[g * GROUP_SIZE:(g + 1) * GROUP_SIZE, :]
        m1 = jnp.max(bg, axis=0, keepdims=True)
        first = jnp.min(jnp.where(bg == m1, member, GROUP_SIZE), axis=0, keepdims=True)
        m2 = jnp.max(jnp.where(member == first, -jnp.inf, bg), axis=0, keepdims=True)
        grp.append(m1 + m2)
    masked = []
    for g in range(N_GROUPS):
        rank = jnp.zeros((1, t), jnp.int32)
        for o in range(N_GROUPS):
            if o == g:
                continue
            ahead = (grp[o] > grp[g]) if o > g else (grp[o] >= grp[g])
            rank = rank + jnp.where(ahead, 1, 0)
        keep = rank < TOPK_GROUPS
        masked.append(jnp.where(keep, biased[g * GROUP_SIZE:(g + 1) * GROUP_SIZE, :], -jnp.inf))
    masked = jnp.concatenate(masked, axis=0)
    eidx = lax.broadcasted_iota(jnp.int32, masked.shape, 0)
    rank = jnp.zeros(masked.shape, jnp.int32)
    for o in range(N_EXPERTS):
        ro = masked[o:o + 1, :]
        rank = rank + jnp.where(ro > masked, 1, jnp.where((ro == masked) & (eidx > o), 1, 0))
    chosen = rank < TOP_K
    sel = jnp.where(chosen, s, 0.0)
    gate_ref[...] = sel / jnp.sum(sel, axis=0, keepdims=True) * ROUTED_SCALE
    mask = jnp.where(chosen, 1.0, 0.0)
    sel_ref[...] = mask.astype(sel_ref.dtype)

    @pl.when(pl.program_id(0) == 0)
    def _():
        cnt_ref[...] = jnp.zeros_like(cnt_ref)

    cnt_ref[...] += jnp.sum(mask, axis=1, keepdims=True)


def _route(logits_t, b_router, row0, n, tt=512):
    e = logits_t.shape[0]
    first = row0 // tt
    return pl.pallas_call(
        _route_kernel,
        grid=(n // tt,),
        in_specs=[pl.BlockSpec((e, tt), lambda i: (0, i + first)), pl.BlockSpec((e, 1), lambda i: (0, 0))],
        out_specs=[pl.BlockSpec((e, tt), lambda i: (0, i)), pl.BlockSpec((e, tt), lambda i: (0, i)),
                   pl.BlockSpec((e, GATE_LANES), lambda i: (0, 0))],
        out_shape=[jax.ShapeDtypeStruct((e, n), F32), jax.ShapeDtypeStruct((e, n), BF16),
                   jax.ShapeDtypeStruct((e, GATE_LANES), F32)],
        compiler_params=_params("arbitrary"),
        name="route",
    )(logits_t, b_router)


def _dest_kernel(sel_ref, gate_ref, start_ref, triu_ref, below_ref, dest_ref, g8_ref, run_ref):
    @pl.when(pl.program_id(0) == 0)
    def _():
        run_ref[...] = jnp.zeros_like(run_ref)

    sel = sel_ref[...]
    t = sel.shape[1]
    self32 = sel.astype(F32)
    incl = _dot(sel, triu_ref[...])
    pos = start_ref[...] + run_ref[...] + incl - self32
    run_ref[...] += incl[:, t - 1:t]
    slot = _dot(below_ref[...], sel)
    gate = gate_ref[...]
    dst, gts = [], []
    for k in range(TOP_K):
        mine = (self32 > 0.0) & (slot == float(k))
        dst.append(jnp.sum(jnp.where(mine, pos, 0.0), axis=0, keepdims=True))
        gts.append(jnp.sum(jnp.where(mine, gate, 0.0), axis=0, keepdims=True))
    dest_ref[...] = jnp.concatenate(dst, axis=0).astype(jnp.int32)
    g8_ref[...] = jnp.concatenate(gts + [jnp.zeros((GATE_LANES - TOP_K, t), F32)], axis=0).T


def _dest(sel, gate, seg_start, tt=512):
    e, n = sel.shape
    triu = jnp.triu(jnp.ones((tt, tt), F32)).astype(BF16)
    below = jnp.tril(jnp.ones((e, e), F32), -1).astype(BF16)
    return pl.pallas_call(
        _dest_kernel,
        grid=(n // tt,),
        in_specs=[pl.BlockSpec((e, tt), lambda i: (0, i)), pl.BlockSpec((e, tt), lambda i: (0, i)),
                  pl.BlockSpec((e, 1), lambda i: (0, 0)), pl.BlockSpec((tt, tt), lambda i: (0, 0)),
                  pl.BlockSpec((e, e), lambda i: (0, 0))],
        out_specs=[pl.BlockSpec((TOP_K, tt), lambda i: (0, i)), pl.BlockSpec((tt, GATE_LANES), lambda i: (i, 0))],
        out_shape=[jax.ShapeDtypeStruct((TOP_K, n), jnp.int32), jax.ShapeDtypeStruct((n, GATE_LANES), F32)],
        scratch_shapes=[pltpu.VMEM((e, 1), F32)],
        compiler_params=_params("arbitrary"),
        name="dest",
    )(sel, gate, seg_start, triu, below)


def _sc_mesh():
    return plsc.VectorSubcoreMesh(core_axis_name="c", subcore_axis_name="s",
                                  num_cores=SC_CORES, num_subcores=SC_SUBCORES)


def _worker_id():
    return lax.axis_index("s") * SC_CORES + lax.axis_index("c")


def _sc_scatter_rows(rows, row0, dest_flat, n_out):
    n = dest_flat.shape[0] // TOP_K
    w = rows.shape[1]
    per_worker = n // SC_WORKERS

    def body(rows_hbm, dest_hbm, out_hbm, idx_v, rows_v):
        base = _worker_id() * per_worker

        @pl.loop(0, per_worker // SC_CHUNK)
        def _(c):
            t0 = base + c * SC_CHUNK
            pltpu.sync_copy(rows_hbm.at[pl.ds(row0 + t0, SC_CHUNK)], rows_v)
            for k in range(TOP_K):
                pltpu.sync_copy(dest_hbm.at[pl.ds(k * n + t0, SC_CHUNK)], idx_v)
                pltpu.sync_copy(rows_v, out_hbm.at[idx_v])

    return pl.kernel(
        body, out_type=jax.ShapeDtypeStruct((n_out, w), rows.dtype), mesh=_sc_mesh(),
        scratch_types=[pltpu.VMEM((SC_CHUNK,), jnp.int32), pltpu.VMEM((SC_CHUNK, w), rows.dtype)],
        name="sc_scatter",
    )(rows, dest_flat)


def _sc_gather_rows(table, idx_flat):
    m = idx_flat.shape[0]
    w = table.shape[1]
    per_worker = m // SC_WORKERS

    def body(table_hbm, idx_hbm, out_hbm, idx_v, rows_v):
        base = _worker_id() * per_worker

        @pl.loop(0, per_worker // SC_CHUNK)
        def _(c):
            j0 = base + c * SC_CHUNK
            pltpu.sync_copy(idx_hbm.at[pl.ds(j0, SC_CHUNK)], idx_v)
            pltpu.sync_copy(table_hbm.at[idx_v], rows_v)
            pltpu.sync_copy(rows_v, out_hbm.at[pl.ds(j0, SC_CHUNK)])

    return pl.kernel(
        body, out_type=jax.ShapeDtypeStruct((m, w), table.dtype), mesh=_sc_mesh(),
        scratch_types=[pltpu.VMEM((SC_CHUNK,), jnp.int32), pltpu.VMEM((SC_CHUNK, w), table.dtype)],
        name="sc_gather",
    )(table, idx_flat)


def _swiglu(h, wg, wu):
    gte = _dot(h, wg)
    return gte * _sigmoid(gte) * _dot(h, wu)


def _experts_kernel(be_ref, va_ref, xs_ref, wg_b, wu_b, wd_b, ys_ref):
    valid = va_ref[pl.program_id(0)]

    @pl.when(valid > 0)
    def _():
        nsub = EXPERT_BLOCK // EXPERT_SUB
        row = lax.broadcasted_iota(jnp.int32, (EXPERT_SUB, D_MODEL), 0)
        x = []
        for i in range(nsub):
            xi = _unpack_rows(xs_ref[i * EXPERT_SUB:(i + 1) * EXPERT_SUB, :])
            x.append(jnp.where(row < valid - i * EXPERT_SUB, xi, 0.0).astype(BF16))
        gte = [_dot(xi, wg_b[...]) for xi in x]
        up = [_dot(xi, wu_b[...]) for xi in x]
        act = [(g * _sigmoid(g) * u).astype(BF16) for g, u in zip(gte, up)]
        y = [_dot(a, wd_b[...]) for a in act]
        for i in range(nsub):
            ys_ref[i * EXPERT_SUB:(i + 1) * EXPERT_SUB, :] = _pack_rows(y[i])


def _experts(xs, blk_expert, blk_valid, wg, wu, wd, layer):
    n_rows, w = xs.shape
    d = wg.shape[2]
    rows = pl.BlockSpec((EXPERT_BLOCK, w), lambda s, be, va: (s, 0))
    up_spec = pl.BlockSpec((None, None, d, D_EXPERT), lambda s, be, va: (layer, be[s], 0, 0))
    down_spec = pl.BlockSpec((None, None, D_EXPERT, d), lambda s, be, va: (layer, be[s], 0, 0))
    return pl.pallas_call(
        _experts_kernel,
        grid_spec=pltpu.PrefetchScalarGridSpec(
            num_scalar_prefetch=2,
            grid=(n_rows // EXPERT_BLOCK,),
            in_specs=[rows, up_spec, up_spec, down_spec],
            out_specs=rows,
        ),
        out_shape=jax.ShapeDtypeStruct((n_rows, w), xs.dtype),
        compiler_params=_params("arbitrary"),
        name="experts",
    )(blk_expert, blk_valid, xs, wg, wu, wd)


def _combine_kernel(yg_ref, g8_ref, hp_ref, sg_ref, su_ref, sd_ref, x_ref, gpost_ref, gt_ref, *rest):
    o_ref = rest[-1]
    h = _unpack_rows(hp_ref[...]).astype(BF16)
    acc = _dot(_swiglu(h, sg_ref[...], su_ref[...]).astype(BF16), sd_ref[...])
    g8 = g8_ref[...]
    for k in range(TOP_K):
        acc = acc + g8[:, k:k + 1] * _unpack_rows(yg_ref[k])
    o_ref[...] = x_ref[...] + gt_ref[...] * _rms(acc, gpost_ref[...])


def _combine(yg, g8, hp, sg, su, sd, x2, g_post, mod, layer, seq, row0, earlier, tm=512):
    n, d = x2.shape
    part = g8.shape[0]
    first = row0 // tm
    local = lambda i: (i, 0)
    glob = lambda i: (i + first, 0)
    full = lambda shape: pl.BlockSpec(shape, lambda i: (0, 0))
    in_specs = [
        pl.BlockSpec((TOP_K, tm, ROW_WORDS), lambda i: (0, i, 0)),
        pl.BlockSpec((tm, GATE_LANES), local),
        pl.BlockSpec((tm, ROW_WORDS), glob),
        full((d, D_EXPERT)), full((d, D_EXPERT)), full((D_EXPERT, d)),
        pl.BlockSpec((tm, d), glob),
        full((1, d)),
        _mod_spec(layer, 5, seq, tm, first),
    ]
    args = [yg, g8, hp, sg, su, sd, x2, g_post, mod]
    aliases = {}
    if earlier is not None:
        in_specs.append(pl.BlockSpec(memory_space=pl.ANY))
        args.append(earlier)
        aliases = {len(args) - 1: 0}
    return pl.pallas_call(
        _combine_kernel,
        grid=(part // tm,),
        in_specs=in_specs,
        out_specs=pl.BlockSpec((tm, d), glob),
        out_shape=jax.ShapeDtypeStruct((n, d), F32),
        input_output_aliases=aliases,
        compiler_params=_params("parallel"),
        name="combine",
    )(*args)


def _moe(hp, logits_t, b_router, wg, wu, wd, sg, su, sd, x2, g_post, mod, layer, seq):
    n = x2.shape[0]
    part = n // MOE_PARTS
    out = None
    for p in range(MOE_PARTS):
        row0 = p * part
        gate, sel, counts = _route(logits_t, b_router, row0, part)
        cnt = counts[:, 0].astype(jnp.int32)
        padded = (cnt + EXPERT_BLOCK - 1) // EXPERT_BLOCK * EXPERT_BLOCK
        seg_end = jnp.cumsum(padded)
        seg_start = seg_end - padded
        n_blocks = part * TOP_K // EXPERT_BLOCK + N_EXPERTS
        blk_row = jnp.arange(n_blocks, dtype=jnp.int32) * EXPERT_BLOCK
        blk_expert = jnp.sum((seg_end[None, :] <= blk_row[:, None]).astype(jnp.int32), axis=1)
        blk_expert = jnp.minimum(blk_expert, N_EXPERTS - 1)
        mine = blk_expert[:, None] == jnp.arange(N_EXPERTS, dtype=jnp.int32)[None, :]
        seg_last = jnp.sum(jnp.where(mine, (seg_start + cnt)[None, :], 0), axis=1)
        blk_valid = jnp.clip(seg_last - blk_row, 0, EXPERT_BLOCK)
        dest, g8 = _dest(sel, gate, seg_start.astype(F32).reshape(-1, 1))
        dest_flat = dest.reshape(-1)
        xs = _sc_scatter_rows(hp, row0, dest_flat, n_blocks * EXPERT_BLOCK)
        ys = _experts(xs, blk_expert, blk_valid.astype(jnp.int32), wg, wu, wd, layer)
        yg = _sc_gather_rows(ys, dest_flat).reshape(TOP_K, part, ROW_WORDS)
        out = _combine(yg, g8, hp, sg, su, sd, x2, g_post, mod, layer, seq, row0, out)
    return out


def kernel(x, c, w_ada, b_ada, norm_mix_pre, norm_mix_post, norm_ffn_pre, norm_ffn_post, w_in, mu_shift, conv_w, conv_b, conv_ln_g, conv_ln_b, p_conv, w0, w2, a0, a2, g2, k_k, k_a, r_k, gn_g, gn_b, p_rwkv, w_o, w_router, b_router, we_gate, we_up, we_down, ws_gate, ws_up, ws_down):
    nb, seq, d = x.shape
    depth = w_ada.shape[0]
    n = nb * seq
    rw = RWKV_WIDTH
    mod = _ada(c, w_ada, b_ada)
    x2 = x.reshape(n, d)
    row = lambda a: a.reshape(1, -1)
    bf = lambda a: a.astype(BF16)
    we_gate_b, we_up_b, we_down_b = bf(we_gate), bf(we_up), bf(we_down)
    for l in range(depth):
        zc, zr, zg = _win(x2, row(norm_mix_pre[l]), mod, l, bf(w_in[l]), seq)
        uc = _conv(zc, conv_w[l], row(conv_b[l]), row(conv_ln_g[l]), row(conv_ln_b[l]), nb, seq)
        wa = jnp.zeros((LORA_W + LORA_A, 2 * rw), F32)
        wa = wa.at[:LORA_W, :rw].set(w2[l]).at[LORA_W:, rw:].set(a2[l])
        prep_out = _prep(zr, row(mu_shift[l]), row(w0[l]), row(a0[l]), wa, g2[l], row(k_k[l]),
                         row(k_a[l]), row(r_k[l]), nb, seq)
        ur = _scan(prep_out, row(gn_g[l]), row(gn_b[l]), nb, seq)
        x2, hp, logits_t = _merge(uc, ur, zg, x2, bf(p_conv[l]), bf(p_rwkv[l]), bf(w_o[l]),
                                  row(norm_mix_post[l]), row(norm_ffn_pre[l]), mod, l, w_router[l].T, seq)
        x2 = _moe(hp, logits_t, b_router[l].reshape(-1, 1), we_gate_b, we_up_b, we_down_b,
                  bf(ws_gate[l]), bf(ws_up[l]), bf(ws_down[l]), x2, row(norm_ffn_post[l]), mod, l, seq)
    return x2.reshape(nb, seq, d)
```

```python
import functools

import jax
import jax.numpy as jnp
from jax import lax
from jax.experimental import pallas as pl
from jax.experimental.pallas import tpu as pltpu
from jax.experimental.pallas import tpu_sc as plsc

F32 = jnp.float32
BF16 = jnp.bfloat16
HI = lax.Precision.HIGHEST

D_MODEL = 1024
CONV_WIDTH = 512
CONV_KERNEL = 31
RWKV_WIDTH = 512
HEAD = 64
N_HEADS = RWKV_WIDTH // HEAD
LORA_W = 64
LORA_A = 64
LORA_G = 128
N_EXPERTS = 64
TOP_K = 8
N_GROUPS = 8
TOPK_GROUPS = 4
GROUP_SIZE = N_EXPERTS // N_GROUPS
D_EXPERT = 256
ROUTED_SCALE = 2.5
RMS_EPS = 1e-6
LN_EPS = 1e-5
GN_EPS = 64e-5
COL_CONV = 2 * CONV_WIDTH
COL_RWKV = 3 * RWKV_WIDTH + LORA_W + LORA_A + LORA_G
COL_GATE = 2 * D_MODEL
D_IN = COL_CONV + COL_RWKV + COL_GATE

CHUNK = 64
QUAD = 4 * HEAD
N_QUADS = RWKV_WIDTH // QUAD
CUM_BLOCK = 256
CONV_HALO = 32
SUBLANES = 8
SHIFT_HALO = SUBLANES
GATE_LANES = 128
EXPERT_BLOCK = 512
EXPERT_SUB = 256
BLOCKS_PER_STEP = 2
MOE_PARTS = 1
ROW_WORDS = D_MODEL // 2
SC_CORES = 2
SC_SUBCORES = 16
SC_WORKERS = SC_CORES * SC_SUBCORES
SC_CHUNK = 128
VMEM_LIMIT = 52 * 1024 * 1024


def _params(*sem):
    return pltpu.CompilerParams(dimension_semantics=sem, vmem_limit_bytes=VMEM_LIMIT)


def _sigmoid(x):
    return 1.0 / (1.0 + jnp.exp(-x))


def _softplus(x):
    return jnp.maximum(x, 0.0) + jnp.log(1.0 + jnp.exp(-jnp.abs(x)))


def _rms(x, g):
    return x * lax.rsqrt(jnp.mean(x * x, axis=-1, keepdims=True) + RMS_EPS) * g


def _split(x):
    hi = x.astype(BF16)
    return hi, (x - hi.astype(F32)).astype(BF16)


def _dot(a, b):
    return jnp.dot(a, b, preferred_element_type=F32)


def _dot_exact_rhs(x, w):
    hi, lo = _split(x)
    return _dot(hi, w) + _dot(lo, w)


def _pack_rows(y):
    half = y.shape[1] // 2
    packed = pltpu.pack_elementwise([y[:, half:], y[:, :half]], packed_dtype=BF16)
    return lax.bitcast_convert_type(packed, jnp.int32)


def _unpack_rows(u):
    hi = pltpu.unpack_elementwise(u, index=1, packed_dtype=BF16, unpacked_dtype=F32)
    lo = pltpu.unpack_elementwise(u, index=0, packed_dtype=BF16, unpacked_dtype=F32)
    return jnp.concatenate([hi, lo], axis=1)


def _dot_exact_lhs(w, x):
    hi, lo = _split(x)
    return _dot(w, hi) + _dot(w, lo)


def _ada_kernel(c_ref, w_ref, b_ref, o_ref):
    c = c_ref[...]
    o_ref[...] = jnp.dot(c * _sigmoid(c), w_ref[...], precision=HI,
                         preferred_element_type=F32) + b_ref[...]


def _ada(c, w_ada, b_ada):
    nl, d, _ = w_ada.shape
    nb = c.shape[0]
    out = pl.pallas_call(
        _ada_kernel,
        grid=(nl, 6),
        in_specs=[
            pl.BlockSpec((nb, d), lambda l, k: (0, 0)),
            pl.BlockSpec((None, d, d), lambda l, k: (l, 0, k)),
            pl.BlockSpec((None, None, 1, d), lambda l, k: (l, k, 0, 0)),
        ],
        out_specs=pl.BlockSpec((None, None, nb, d), lambda l, k: (l, k, 0, 0)),
        out_shape=jax.ShapeDtypeStruct((nl, 6, nb, d), F32),
        compiler_params=_params("parallel", "parallel"),
        name="ada",
    )(c, w_ada, b_ada.reshape(nl, 6, 1, d))
    return out.reshape(nl, 6, nb, 1, d)


def _mod_spec(layer, piece, rows_per_batch, tm, first_tile=0):
    return pl.BlockSpec((None, None, None, 1, D_MODEL),
                        lambda i, *_: (layer, piece, ((i + first_tile) * tm) // rows_per_batch, 0, 0))


def _win_kernel(x_ref, g_ref, sc_ref, sh_ref, w_ref, zc_ref, zr_ref, zg_ref):
    h = _rms(x_ref[...], g_ref[...]) * (1.0 + sc_ref[...]) + sh_ref[...]
    hb = h.astype(BF16)
    zc_ref[...] = _dot(hb, w_ref[:, :COL_CONV]).astype(zc_ref.dtype)
    zr_ref[...] = _dot(hb, w_ref[:, COL_CONV:COL_CONV + COL_RWKV])
    zg_ref[...] = _dot(hb, w_ref[:, COL_CONV + COL_RWKV:]).astype(zg_ref.dtype)


def _win(x2, g, mod, layer, w_in_b, seq, tm=512):
    n, d = x2.shape
    row = lambda i: (i, 0)
    return pl.pallas_call(
        _win_kernel,
        grid=(n // tm,),
        in_specs=[
            pl.BlockSpec((tm, d), row),
            pl.BlockSpec((1, d), lambda i: (0, 0)),
            _mod_spec(layer, 1, seq, tm),
            _mod_spec(layer, 0, seq, tm),
            pl.BlockSpec((d, D_IN), lambda i: (0, 0)),
        ],
        out_specs=[pl.BlockSpec((tm, COL_CONV), row), pl.BlockSpec((tm, COL_RWKV), row),
                   pl.BlockSpec((tm, COL_GATE), row)],
        out_shape=[jax.ShapeDtypeStruct((n, COL_CONV), BF16), jax.ShapeDtypeStruct((n, COL_RWKV), F32),
                   jax.ShapeDtypeStruct((n, COL_GATE), BF16)],
        compiler_params=_params("parallel"),
        name="win",
    )(x2, g, mod, mod, w_in_b)


def _conv_kernel(z_ref, halo_ref, w_ref, cb_ref, g_ref, b_ref, o_ref, ubuf, *, ts):
    j = pl.program_id(1)
    z = z_ref[...].astype(F32)
    zh = halo_ref[...].astype(F32)
    uh = zh[:, :CONV_WIDTH] * _sigmoid(zh[:, CONV_WIDTH:])
    rows = CONV_HALO + ts
    u = jnp.concatenate([jnp.where(j > 0, uh, 0.0),
                         z[:, :CONV_WIDTH] * _sigmoid(z[:, CONV_WIDTH:])], axis=0)
    ubuf[0] = u
    for s in range(1, SUBLANES):
        ubuf[s] = pltpu.roll(u, rows - s, axis=0)
    sub = 64
    first = CONV_HALO - (CONV_KERNEL - 1)
    for r in range(ts // sub):
        acc = jnp.zeros((sub, CONV_WIDTH), F32) + cb_ref[...]
        for k in range(CONV_KERNEL):
            s = (first + k) % SUBLANES
            base = r * sub + first + k - s
            acc = acc + w_ref[k:k + 1, :] * ubuf[s, base:base + sub, :]
        mu = jnp.mean(acc, axis=-1, keepdims=True)
        dlt = acc - mu
        var = jnp.mean(dlt * dlt, axis=-1, keepdims=True)
        y = dlt * lax.rsqrt(var + LN_EPS) * g_ref[...] + b_ref[...]
        o_ref[r * sub:(r + 1) * sub, :] = (y * _sigmoid(y)).astype(o_ref.dtype)


def _conv(zc, conv_w, conv_b, ln_g, ln_b, nb, seq, ts=256):
    n = zc.shape[0]
    tps = seq // ts
    hpt = ts // CONV_HALO
    vec = pl.BlockSpec((1, CONV_WIDTH), lambda b, j: (0, 0))
    return pl.pallas_call(
        functools.partial(_conv_kernel, ts=ts),
        grid=(nb, tps),
        in_specs=[
            pl.BlockSpec((ts, COL_CONV), lambda b, j: (b * tps + j, 0)),
            pl.BlockSpec((CONV_HALO, COL_CONV), lambda b, j: (jnp.maximum((b * tps + j) * hpt - 1, 0), 0)),
            pl.BlockSpec((CONV_KERNEL, CONV_WIDTH), lambda b, j: (0, 0)),
            vec, vec, vec,
        ],
        out_specs=pl.BlockSpec((ts, CONV_WIDTH), lambda b, j: (b * tps + j, 0)),
        out_shape=jax.ShapeDtypeStruct((n, CONV_WIDTH), BF16),
        scratch_shapes=[pltpu.VMEM((SUBLANES, CONV_HALO + ts, CONV_WIDTH), F32)],
        compiler_params=_params("parallel", "parallel"),
        name="conv",
    )(zc, zc, conv_w, conv_b, ln_g, ln_b)


def _prep_kernel(z_ref, halo_ref, mu_ref, w0_ref, a0_ref, wah_ref, wal_ref, g2_ref, kk_ref, ka_ref, rk_ref,
                 ones_ref, tri_ref, csel_ref,
                 at_ref, bt_ref, kt_ref, rt_ref, v_ref, bts_ref, kts_ref, bonus_ref, g_ref, pc_ref):
    j = pl.program_id(1)
    z = z_ref[...]
    ts = z.shape[0]
    last = jnp.where(j > 0, halo_ref[SHIFT_HALO - 1:SHIFT_HALO, :], 0.0)
    row = lax.broadcasted_iota(jnp.int32, z.shape, 0)
    prev = jnp.where(row == 0, last, pltpu.roll(z, 1, axis=0))
    zs = z + (prev - z) * mu_ref[...]
    rw = RWKV_WIDTH
    r = zs[:, :rw]
    k = zs[:, rw:2 * rw]
    v = zs[:, 2 * rw:3 * rw]
    wad = zs[:, 3 * rw:3 * rw + LORA_W + LORA_A]
    gd = zs[:, 3 * rw + LORA_W + LORA_A:]
    lane = lax.broadcasted_iota(jnp.int32, wad.shape, 1)
    xh, xl = _split(jnp.where(lane < LORA_W, jnp.tanh(wad), wad))
    wah = wah_ref[...]
    lora = _dot(xh, wah) + _dot(xl, wah) + _dot(xh, wal_ref[...])
    w = -_softplus(-(w0_ref[...] + lora[:, :rw])) - 0.5
    lw = -jnp.exp(w)
    a = _sigmoid(a0_ref[...] + lora[:, rw:])
    g_ref[...] = _dot(_sigmoid(gd).astype(BF16), g2_ref[...]).astype(g_ref.dtype)
    kk = k * kk_ref[...]
    k2 = k * (1.0 + (a - 1.0) * ka_ref[...])
    ones = ones_ref[...]
    kk = kk * lax.rsqrt(jnp.maximum(_dot_exact_rhs(kk * kk, ones), 1e-24))
    bonus_ref[...] = (_dot_exact_rhs(r * k2 * rk_ref[...], ones) * v).astype(bonus_ref.dtype)
    tri = tri_ref[...]
    lc, tot = [], []
    for h in range(ts // CUM_BLOCK):
        cs = _dot_exact_lhs(tri, lw[h * CUM_BLOCK:(h + 1) * CUM_BLOCK])
        lc.append(cs[:CUM_BLOCK])
        tot.append(cs[CUM_BLOCK:])
    lc = jnp.concatenate(lc, axis=0)
    tot = jnp.concatenate(tot, axis=0)
    e_neg = jnp.exp(-lc)
    e_end = jnp.exp(tot - lc)
    kka = kk * a
    dt = at_ref.dtype
    at_ref[...] = (-kk * jnp.exp(lc - lw)).astype(dt)
    bt_ref[...] = (kka * e_neg).astype(dt)
    kt_ref[...] = (k2 * e_neg).astype(dt)
    rt_ref[...] = (r * jnp.exp(lc)).astype(dt)
    v_ref[...] = v.astype(dt)
    bts_ref[...] = (kka * e_end).astype(dt)
    kts_ref[...] = (k2 * e_end).astype(dt)
    pc_ref[...] = jnp.exp(_dot_exact_lhs(csel_ref[...], lw))


def _prep(zr, mu, w0, a0, wa, g2, k_k, k_a, r_k, nb, seq, ts=512):
    n = zr.shape[0]
    tps = seq // ts
    hpt = ts // SHIFT_HALO
    rw = RWKV_WIDTH
    cpt = ts // CHUNK
    ones = jnp.kron(jnp.eye(N_HEADS, dtype=F32), jnp.ones((HEAD, HEAD), F32)).astype(BF16)
    blk = jnp.kron(jnp.eye(CUM_BLOCK // CHUNK, dtype=F32), jnp.ones((CHUNK, CHUNK), F32))
    tri = jnp.concatenate([jnp.tril(blk), blk], axis=0).astype(BF16)
    csel = jnp.kron(jnp.eye(cpt, dtype=F32), jnp.ones((1, CHUNK), F32)).astype(BF16)
    wa_hi, wa_lo = _split(wa)
    vec = pl.BlockSpec((1, rw), lambda b, j: (0, 0))
    full = lambda shape: pl.BlockSpec(shape, lambda b, j: (0, 0))
    rows = pl.BlockSpec((ts, rw), lambda b, j: (b * tps + j, 0))
    return pl.pallas_call(
        _prep_kernel,
        grid=(nb, tps),
        in_specs=[
            pl.BlockSpec((ts, COL_RWKV), lambda b, j: (b * tps + j, 0)),
            pl.BlockSpec((SHIFT_HALO, COL_RWKV), lambda b, j: (jnp.maximum((b * tps + j) * hpt - 1, 0), 0)),
            full((1, COL_RWKV)), vec, vec,
            full((LORA_W + LORA_A, 2 * rw)), full((LORA_W + LORA_A, 2 * rw)), full((LORA_G, rw)),
            vec, vec, vec, full((rw, rw)), full((2 * CUM_BLOCK, CUM_BLOCK)), full((cpt, ts)),
        ],
        out_specs=[rows] * 9 + [pl.BlockSpec((cpt, rw), lambda b, j: (b * tps + j, 0))],
        out_shape=[jax.ShapeDtypeStruct((n, rw), BF16)] * 9 + [jax.ShapeDtypeStruct((n // CHUNK, rw), F32)],
        compiler_params=_params("parallel", "parallel"),
        name="prep",
    )(zr, zr, mu, w0, a0, wa_hi, wa_lo, g2.astype(BF16), k_k, k_a, r_k, ones, tri, csel)


def _mm(a, b):
    return jnp.dot(a.astype(BF16), b.astype(BF16), preferred_element_type=F32)


def _mm_nt(a, b):
    return lax.dot_general(a.astype(BF16), b.astype(BF16), (((1,), (1,)), ((), ())),
                           preferred_element_type=F32)


def _mm_tn(a, b):
    return lax.dot_general(a.astype(BF16), b.astype(BF16), (((0,), (0,)), ((), ())),
                           preferred_element_type=F32)


def _scan_kernel(at_ref, bt_ref, kt_ref, rt_ref, v_ref, bts_ref, kts_ref, bonus_ref, g_ref, pc_ref,
                 gng_ref, gnb_ref, ones_ref, o_ref, s_ref, y_ref, *, ts, gb):
    @pl.when(pl.program_id(1) == 0)
    def _():
        s_ref[...] = jnp.zeros_like(s_ref)

    row_q = lax.broadcasted_iota(jnp.int32, (QUAD, QUAD), 0)
    col_q = lax.broadcasted_iota(jnp.int32, (QUAD, QUAD), 1)
    bdmask = (row_q // HEAD) == (col_q // HEAD)
    row_c = lax.broadcasted_iota(jnp.int32, (CHUNK, QUAD), 0)
    col_c = lax.broadcasted_iota(jnp.int32, (CHUNK, QUAD), 1) % CHUNK
    strict = col_c < row_c
    incl = col_c <= row_c
    eye = jnp.where(col_c == row_c, 1.0, 0.0).astype(F32)

    def bd(x):
        xb = x.astype(BF16)
        return jnp.where(bdmask, jnp.concatenate([xb] * (QUAD // CHUNK), axis=0), jnp.zeros((), BF16))

    chains = [(b, qd) for b in range(gb) for qd in range(N_QUADS)]

    def chunk(c, carry):
        sl = pl.ds(pl.multiple_of(c * CHUNK, CHUNK), CHUNK)
        lanes = [slice(qd * QUAD, (qd + 1) * QUAD) for _, qd in chains]
        ld = lambda ref: [ref[b, sl, ln] for (b, _), ln in zip(chains, lanes)]
        at, bt, kt, rt, v, bts, kts = (ld(r) for r in (at_ref, bt_ref, kt_ref, rt_ref, v_ref, bts_ref, kts_ref))
        each = range(len(chains))
        ar = [jnp.concatenate([at[i], rt[i]], axis=0) for i in each]
        pb = [_mm_nt(ar[i], bd(bt[i])) for i in each]
        pk = [_mm_nt(ar[i], bd(kt[i])) for i in each]
        l_ab = [jnp.where(strict, p[:CHUNK], 0.0) for p in pb]
        a_rb = [jnp.where(incl, p[CHUNK:], 0.0) for p in pb]
        l_ak = [jnp.where(strict, p[:CHUNK], 0.0) for p in pk]
        a_rk = [jnp.where(incl, p[CHUNK:], 0.0) for p in pk]
        tm = [eye + l for l in l_ab]
        lp = [_mm(l, bd(l)) for l in l_ab]
        for _ in range(4):
            both = [_mm(jnp.concatenate([tm[i], lp[i]], axis=0), bd(lp[i])) for i in each]
            tm = [tm[i] + both[i][:CHUNK] for i in each]
            lp = [both[i][CHUNK:] for i in each]
        tm = [tm[i] + _mm(tm[i], bd(lp[i])) for i in each]
        bdv = [bd(x) for x in v]
        x0 = [_mm(l_ak[i], bdv[i]) for i in each]
        tu = [_mm(tm[i], jnp.concatenate([bd(x0[i]), bd(at[i])], axis=1)) for i in each]
        u0 = [t[:, :QUAD] for t in tu]
        w = [t[:, QUAD:] for t in tu]
        qy = [_mm(a_rb[i], jnp.concatenate([bd(w[i]), bd(u0[i])], axis=1)) for i in each]
        y0 = [qy[i][:, QUAD:] + _mm(a_rk[i], bdv[i]) for i in each]
        q = [rt[i].astype(F32) + qy[i][:, :QUAD] for i in each]
        s = [s_ref[i] for i in each]
        qw_s = [_mm_nt(jnp.concatenate([q[i], w[i]], axis=0), s[i]) for i in each]
        for i, ((b, _), ln) in enumerate(zip(chains, lanes)):
            y_ref[b, sl, ln] = y0[i] + qw_s[i][:CHUNK]
        u = [(u0[i] + qw_s[i][CHUNK:]).astype(BF16) for i in each]
        n_p = [jnp.where(bdmask, _mm_tn(jnp.concatenate([u[i], v[i]], axis=0),
                                        jnp.concatenate([bts[i], kts[i]], axis=0)), 0.0) for i in each]
        for i, ((b, _), ln) in enumerate(zip(chains, lanes)):
            s_ref[i] = s[i] * pc_ref[b, pl.ds(c, 1), ln] + n_p[i]
        return carry

    lax.fori_loop(0, ts // CHUNK, chunk, 0)

    ones = ones_ref[...]
    for b in range(gb):
        for qd in range(N_QUADS):
            ln = slice(qd * QUAD, (qd + 1) * QUAD)
            y = y_ref[b, :, ln]
            dlt = y - _dot_exact_rhs(y, ones) * (1.0 / HEAD)
            var = _dot_exact_rhs(dlt * dlt, ones) * (1.0 / HEAD)
            yn = dlt * lax.rsqrt(var + GN_EPS) * gng_ref[:, ln] + gnb_ref[:, ln]
            o_ref[b, :, ln] = ((yn + bonus_ref[b, :, ln].astype(F32)) * g_ref[b, :, ln].astype(F32)).astype(o_ref.dtype)


def _scan(prep_out, gn_g, gn_b, nb, seq, ts=128, gb=8):
    n = prep_out[0].shape[0]
    rw = RWKV_WIDTH
    cpt = ts // CHUNK
    ones = jnp.kron(jnp.eye(QUAD // HEAD, dtype=F32), jnp.ones((HEAD, HEAD), F32)).astype(BF16)
    rows = pl.BlockSpec((gb, ts, rw), lambda i, t: (i, t, 0))
    vec = pl.BlockSpec((1, rw), lambda i, t: (0, 0))
    args = [a.reshape(nb, seq, rw) for a in prep_out[:9]] + [prep_out[9].reshape(nb, seq // ts, cpt, rw)]
    out = pl.pallas_call(
        functools.partial(_scan_kernel, ts=ts, gb=gb),
        grid=(nb // gb, seq // ts),
        in_specs=[rows] * 9 + [pl.BlockSpec((gb, None, cpt, rw), lambda i, t: (i, t, 0, 0)), vec, vec,
                               pl.BlockSpec((QUAD, QUAD), lambda i, t: (0, 0))],
        out_specs=rows,
        out_shape=jax.ShapeDtypeStruct((nb, seq, rw), BF16),
        scratch_shapes=[pltpu.VMEM((gb * N_QUADS, QUAD, QUAD), F32), pltpu.VMEM((gb, ts, rw), F32)],
        compiler_params=_params("parallel", "arbitrary"),
        name="scan",
    )(*args, gn_g, gn_b, ones)
    return out.reshape(n, rw)


def _merge_kernel(uc_ref, ur_ref, zg_ref, x_ref, pc_ref, pr_ref, wo_ref, gpost_ref, gt_ref,
                  gpre_ref, sc_ref, sh_ref, wrh_ref, wrl_ref, xo_ref, h_ref, lg_ref):
    yc = _dot(uc_ref[...], pc_ref[...])
    yr = _dot(ur_ref[...], pr_ref[...])
    zg = zg_ref[...].astype(F32)
    m = _sigmoid(zg[:, :D_MODEL]) * yc + _sigmoid(zg[:, D_MODEL:]) * yr
    y = _dot(m.astype(BF16), wo_ref[...])
    xn = x_ref[...] + gt_ref[...] * _rms(y, gpost_ref[...])
    xo_ref[...] = xn
    h = _rms(xn, gpre_ref[...]) * (1.0 + sc_ref[...]) + sh_ref[...]
    h_ref[...] = _pack_rows(h)
    h_hi, h_lo = _split(h)
    nt = lambda a, b: lax.dot_general(a, b, (((1,), (1,)), ((), ())), preferred_element_type=F32)
    w_hi = wrh_ref[...]
    lg_ref[...] = nt(w_hi, h_hi) + nt(w_hi, h_lo) + nt(wrl_ref[...], h_hi)


def _merge(uc, ur, zg, x2, p_conv_b, p_rwkv_b, w_o_b, g_post, g_pre, mod, layer, w_router_t, seq, tm=512):
    n, d = x2.shape
    row = lambda i: (i, 0)
    full = lambda shape: pl.BlockSpec(shape, lambda i: (0, 0))
    return pl.pallas_call(
        _merge_kernel,
        grid=(n // tm,),
        in_specs=[
            pl.BlockSpec((tm, CONV_WIDTH), row), pl.BlockSpec((tm, RWKV_WIDTH), row),
            pl.BlockSpec((tm, COL_GATE), row), pl.BlockSpec((tm, d), row),
            full((CONV_WIDTH, d)), full((RWKV_WIDTH, d)), full((d, d)),
            full((1, d)), _mod_spec(layer, 2, seq, tm),
            full((1, d)), _mod_spec(layer, 4, seq, tm), _mod_spec(layer, 3, seq, tm),
            full((N_EXPERTS, d)), full((N_EXPERTS, d)),
        ],
        out_specs=[pl.BlockSpec((tm, d), row), pl.BlockSpec((tm, ROW_WORDS), row),
                   pl.BlockSpec((N_EXPERTS, tm), lambda i: (0, i))],
        out_shape=[jax.ShapeDtypeStruct((n, d), F32), jax.ShapeDtypeStruct((n, ROW_WORDS), jnp.int32),
                   jax.ShapeDtypeStruct((N_EXPERTS, n), F32)],
        compiler_params=_params("parallel"),
        name="merge",
    )(uc, ur, zg, x2, p_conv_b, p_rwkv_b, w_o_b, g_post, mod, g_pre, mod, mod, *_split(w_router_t))


def _route_kernel(lg_ref, b_ref, gate_ref, sel_ref, cnt_ref):
    s = _sigmoid(lg_ref[...])
    biased = s + b_ref[...]
    t = s.shape[1]
    member = lax.broadcasted_iota(jnp.int32, (GROUP_SIZE, t), 0)
    grp = []
    for g in range(N_GROUPS):
        bg = biased[g * GROUP_SIZE:(g + 1) * GROUP_SIZE, :]
        m1 = jnp.max(bg, axis=0, keepdims=True)
        first = jnp.min(jnp.where(bg == m1, member, GROUP_SIZE), axis=0, keepdims=True)
        m2 = jnp.max(jnp.where(member == first, -jnp.inf, bg), axis=0, keepdims=True)
        grp.append(m1 + m2)
    masked = []
    for g in range(N_GROUPS):
        rank = jnp.zeros((1, t), jnp.int32)
        for o in range(N_GROUPS):
            if o == g:
                continue
            ahead = (grp[o] > grp[g]) if o > g else (grp[o] >= grp[g])
            rank = rank + jnp.where(ahead, 1, 0)
        keep = rank < TOPK_GROUPS
        masked.append(jnp.where(keep, biased[g * GROUP_SIZE:(g + 1) * GROUP_SIZE, :], -jnp.inf))
    masked = jnp.concatenate(masked, axis=0)
    eidx = lax.broadcasted_iota(jnp.int32, masked.shape, 0)
    rank = jnp.zeros(masked.shape, jnp.int32)
    for o in range(N_EXPERTS):
        ro = masked[o:o + 1, :]
        rank = rank + jnp.where(ro > masked, 1, jnp.where((ro == masked) & (eidx > o), 1, 0))
    chosen = rank < TOP_K
    sel = jnp.where(chosen, s, 0.0)
    gate_ref[...] = sel / jnp.sum(sel, axis=0, keepdims=True) * ROUTED_SCALE
    mask = jnp.where(chosen, 1.0, 0.0)
    sel_ref[...] = mask.astype(sel_ref.dtype)

    @pl.when(pl.program_id(0) == 0)
    def _():
        cnt_ref[...] = jnp.zeros_like(cnt_ref)

    cnt_ref[...] += jnp.sum(mask, axis=1, keepdims=True)


def _route(logits_t, b_router, row0, n, tt=512):
    e = logits_t.shape[0]
    first = row0 // tt
    return pl.pallas_call(
        _route_kernel,
        grid=(n // tt,),
        in_specs=[pl.BlockSpec((e, tt), lambda i: (0, i + first)), pl.BlockSpec((e, 1), lambda i: (0, 0))],
        out_specs=[pl.BlockSpec((e, tt), lambda i: (0, i)), pl.BlockSpec((e, tt), lambda i: (0, i)),
                   pl.BlockSpec((e, GATE_LANES), lambda i: (0, 0))],
        out_shape=[jax.ShapeDtypeStruct((e, n), F32), jax.ShapeDtypeStruct((e, n), BF16),
                   jax.ShapeDtypeStruct((e, GATE_LANES), F32)],
        compiler_params=_params("arbitrary"),
        name="route",
    )(logits_t, b_router)


def _dest_kernel(sel_ref, gate_ref, start_ref, triu_ref, below_ref, dest_ref, g8_ref, run_ref):
    @pl.when(pl.program_id(0) == 0)
    def _():
        run_ref[...] = jnp.zeros_like(run_ref)

    sel = sel_ref[...]
    t = sel.shape[1]
    self32 = sel.astype(F32)
    incl = _dot(sel, triu_ref[...])
    pos = start_ref[...] + run_ref[...] + incl - self32
    run_ref[...] += incl[:, t - 1:t]
    slot = _dot(below_ref[...], sel)
    gate = gate_ref[...]
    dst, gts = [], []
    for k in range(TOP_K):
        mine = (self32 > 0.0) & (slot == float(k))
        dst.append(jnp.sum(jnp.where(mine, pos, 0.0), axis=0, keepdims=True))
        gts.append(jnp.sum(jnp.where(mine, gate, 0.0), axis=0, keepdims=True))
    dest_ref[...] = jnp.concatenate(dst, axis=0).astype(jnp.int32)
    g8_ref[...] = jnp.concatenate(gts + [jnp.zeros((GATE_LANES - TOP_K, t), F32)], axis=0).T


def _dest(sel, gate, seg_start, tt=512):
    e, n = sel.shape
    triu = jnp.triu(jnp.ones((tt, tt), F32)).astype(BF16)
    below = jnp.tril(jnp.ones((e, e), F32), -1).astype(BF16)
    return pl.pallas_call(
        _dest_kernel,
        grid=(n // tt,),
        in_specs=[pl.BlockSpec((e, tt), lambda i: (0, i)), pl.BlockSpec((e, tt), lambda i: (0, i)),
                  pl.BlockSpec((e, 1), lambda i: (0, 0)), pl.BlockSpec((tt, tt), lambda i: (0, 0)),
                  pl.BlockSpec((e, e), lambda i: (0, 0))],
        out_specs=[pl.BlockSpec((TOP_K, tt), lambda i: (0, i)), pl.BlockSpec((tt, GATE_LANES), lambda i: (i, 0))],
        out_shape=[jax.ShapeDtypeStruct((TOP_K, n), jnp.int32), jax.ShapeDtypeStruct((n, GATE_LANES), F32)],
        scratch_shapes=[pltpu.VMEM((e, 1), F32)],
        compiler_params=_params("arbitrary"),
        name="dest",
    )(sel, gate, seg_start, triu, below)


def _sc_mesh():
    return plsc.VectorSubcoreMesh(core_axis_name="c", subcore_axis_name="s",
                                  num_cores=SC_CORES, num_subcores=SC_SUBCORES)


def _worker_id():
    return lax.axis_index("s") * SC_CORES + lax.axis_index("c")


def _sc_scatter_rows(rows, row0, dest_flat, n_out):
    n = dest_flat.shape[0] // TOP_K
    w = rows.shape[1]
    per_worker = n // SC_WORKERS

    def body(rows_hbm, dest_hbm, out_hbm, idx_v, rows_v):
        base = _worker_id() * per_worker

        @pl.loop(0, per_worker // SC_CHUNK)
        def _(c):
            t0 = base + c * SC_CHUNK
            pltpu.sync_copy(rows_hbm.at[pl.ds(row0 + t0, SC_CHUNK)], rows_v)
            for k in range(TOP_K):
                pltpu.sync_copy(dest_hbm.at[pl.ds(k * n + t0, SC_CHUNK)], idx_v)
                pltpu.sync_copy(rows_v, out_hbm.at[idx_v])

    return pl.kernel(
        body, out_type=jax.ShapeDtypeStruct((n_out, w), rows.dtype), mesh=_sc_mesh(),
        scratch_types=[pltpu.VMEM((SC_CHUNK,), jnp.int32), pltpu.VMEM((SC_CHUNK, w), rows.dtype)],
        name="sc_scatter",
    )(rows, dest_flat)


def _sc_gather_rows(table, idx_flat):
    m = idx_flat.shape[0]
    w = table.shape[1]
    per_worker = m // SC_WORKERS

    def body(table_hbm, idx_hbm, out_hbm, idx_v, rows_v):
        base = _worker_id() * per_worker

        @pl.loop(0, per_worker // SC_CHUNK)
        def _(c):
            j0 = base + c * SC_CHUNK
            pltpu.sync_copy(idx_hbm.at[pl.ds(j0, SC_CHUNK)], idx_v)
            pltpu.sync_copy(table_hbm.at[idx_v], rows_v)
            pltpu.sync_copy(rows_v, out_hbm.at[pl.ds(j0, SC_CHUNK)])

    return pl.kernel(
        body, out_type=jax.ShapeDtypeStruct((m, w), table.dtype), mesh=_sc_mesh(),
        scratch_types=[pltpu.VMEM((SC_CHUNK,), jnp.int32), pltpu.VMEM((SC_CHUNK, w), table.dtype)],
        name="sc_gather",
    )(table, idx_flat)


def _swiglu(h, wg, wu):
    gte = _dot(h, wg)
    return gte * _sigmoid(gte) * _dot(h, wu)


def _experts_kernel(be_ref, va_ref, xs_ref, *refs):
    nb = BLOCKS_PER_STEP
    wg, wu, wd, ys_ref = refs[:nb], refs[nb:2 * nb], refs[2 * nb:3 * nb], refs[3 * nb]
    first = pl.program_id(0) * nb

    @pl.when(va_ref[first] > 0)
    def _():
        row = lax.broadcasted_iota(jnp.int32, (EXPERT_SUB, D_MODEL), 0)
        subs = [(b, r0) for b in range(nb) for r0 in range(0, EXPERT_BLOCK, EXPERT_SUB)]
        x = []
        for b, r0 in subs:
            lo = b * EXPERT_BLOCK + r0
            xi = _unpack_rows(xs_ref[lo:lo + EXPERT_SUB, :])
            x.append(jnp.where(row < va_ref[first + b] - r0, xi, 0.0).astype(BF16))
        gte = [_dot(xi, wg[b][...]) for xi, (b, _) in zip(x, subs)]
        up = [_dot(xi, wu[b][...]) for xi, (b, _) in zip(x, subs)]
        act = [(g * _sigmoid(g) * u).astype(BF16) for g, u in zip(gte, up)]
        y = [_dot(a, wd[b][...]) for a, (b, _) in zip(act, subs)]
        for yi, (b, r0) in zip(y, subs):
            lo = b * EXPERT_BLOCK + r0
            ys_ref[lo:lo + EXPERT_SUB, :] = _pack_rows(yi)


def _experts(xs, blk_expert, blk_valid, wg, wu, wd, layer):
    n_rows, w = xs.shape
    d = wg.shape[2]
    nb = BLOCKS_PER_STEP
    rows = pl.BlockSpec((nb * EXPERT_BLOCK, w), lambda s, be, va: (s, 0))
    up_spec = lambda i: pl.BlockSpec((None, None, d, D_EXPERT), lambda s, be, va: (layer, be[s * nb + i], 0, 0))
    down_spec = lambda i: pl.BlockSpec((None, None, D_EXPERT, d), lambda s, be, va: (layer, be[s * nb + i], 0, 0))
    return pl.pallas_call(
        _experts_kernel,
        grid_spec=pltpu.PrefetchScalarGridSpec(
            num_scalar_prefetch=2,
            grid=(n_rows // (nb * EXPERT_BLOCK),),
            in_specs=[rows] + [up_spec(i) for i in range(nb)] * 2 + [down_spec(i) for i in range(nb)],
            out_specs=rows,
        ),
        out_shape=jax.ShapeDtypeStruct((n_rows, w), xs.dtype),
        compiler_params=_params("arbitrary"),
        name="experts",
    )(blk_expert, blk_valid, xs, *([wg] * nb), *([wu] * nb), *([wd] * nb))


def _combine_kernel(yg_ref, g8_ref, hp_ref, sg_ref, su_ref, sd_ref, x_ref, gpost_ref, gt_ref, *rest):
    o_ref = rest[-1]
    h = _unpack_rows(hp_ref[...]).astype(BF16)
    acc = _dot(_swiglu(h, sg_ref[...], su_ref[...]).astype(BF16), sd_ref[...])
    g8 = g8_ref[...]
    for k in range(TOP_K):
        acc = acc + g8[:, k:k + 1] * _unpack_rows(yg_ref[k])
    o_ref[...] = x_ref[...] + gt_ref[...] * _rms(acc, gpost_ref[...])


def _combine(yg, g8, hp, sg, su, sd, x2, g_post, mod, layer, seq, row0, earlier, tm=512):
    n, d = x2.shape
    part = g8.shape[0]
    first = row0 // tm
    local = lambda i: (i, 0)
    glob = lambda i: (i + first, 0)
    full = lambda shape: pl.BlockSpec(shape, lambda i: (0, 0))
    in_specs = [
        pl.BlockSpec((TOP_K, tm, ROW_WORDS), lambda i: (0, i, 0)),
        pl.BlockSpec((tm, GATE_LANES), local),
        pl.BlockSpec((tm, ROW_WORDS), glob),
        full((d, D_EXPERT)), full((d, D_EXPERT)), full((D_EXPERT, d)),
        pl.BlockSpec((tm, d), glob),
        full((1, d)),
        _mod_spec(layer, 5, seq, tm, first),
    ]
    args = [yg, g8, hp, sg, su, sd, x2, g_post, mod]
    aliases = {}
    if earlier is not None:
        in_specs.append(pl.BlockSpec(memory_space=pl.ANY))
        args.append(earlier)
        aliases = {len(args) - 1: 0}
    return pl.pallas_call(
        _combine_kernel,
        grid=(part // tm,),
        in_specs=in_specs,
        out_specs=pl.BlockSpec((tm, d), glob),
        out_shape=jax.ShapeDtypeStruct((n, d), F32),
        input_output_aliases=aliases,
        compiler_params=_params("parallel"),
        name="combine",
    )(*args)


def _moe(hp, logits_t, b_router, wg, wu, wd, sg, su, sd, x2, g_post, mod, layer, seq):
    n = x2.shape[0]
    part = n // MOE_PARTS
    out = None
    for p in range(MOE_PARTS):
        row0 = p * part
        gate, sel, counts = _route(logits_t, b_router, row0, part)
        cnt = counts[:, 0].astype(jnp.int32)
        padded = (cnt + EXPERT_BLOCK - 1) // EXPERT_BLOCK * EXPERT_BLOCK
        seg_end = jnp.cumsum(padded)
        seg_start = seg_end - padded
        n_blocks = part * TOP_K // EXPERT_BLOCK + N_EXPERTS
        blk_row = jnp.arange(n_blocks, dtype=jnp.int32) * EXPERT_BLOCK
        blk_expert = jnp.sum((seg_end[None, :] <= blk_row[:, None]).astype(jnp.int32), axis=1)
        blk_expert = jnp.minimum(blk_expert, N_EXPERTS - 1)
        mine = blk_expert[:, None] == jnp.arange(N_EXPERTS, dtype=jnp.int32)[None, :]
        seg_last = jnp.sum(jnp.where(mine, (seg_start + cnt)[None, :], 0), axis=1)
        blk_valid = jnp.clip(seg_last - blk_row, 0, EXPERT_BLOCK)
        dest, g8 = _dest(sel, gate, seg_start.astype(F32).reshape(-1, 1))
        dest_flat = dest.reshape(-1)
        xs = _sc_scatter_rows(hp, row0, dest_flat, n_blocks * EXPERT_BLOCK)
        ys = _experts(xs, blk_expert, blk_valid.astype(jnp.int32), wg, wu, wd, layer)
        yg = _sc_gather_rows(ys, dest_flat).reshape(TOP_K, part, ROW_WORDS)
        out = _combine(yg, g8, hp, sg, su, sd, x2, g_post, mod, layer, seq, row0, out)
    return out


def kernel(x, c, w_ada, b_ada, norm_mix_pre, norm_mix_post, norm_ffn_pre, norm_ffn_post, w_in, mu_shift, conv_w, conv_b, conv_ln_g, conv_ln_b, p_conv, w0, w2, a0, a2, g2, k_k, k_a, r_k, gn_g, gn_b, p_rwkv, w_o, w_router, b_router, we_gate, we_up, we_down, ws_gate, ws_up, ws_down):
    nb, seq, d = x.shape
    depth = w_ada.shape[0]
    n = nb * seq
    rw = RWKV_WIDTH
    mod = _ada(c, w_ada, b_ada)
    x2 = x.reshape(n, d)
    row = lambda a: a.reshape(1, -1)
    bf = lambda a: a.astype(BF16)
    we_gate_b, we_up_b, we_down_b = bf(we_gate), bf(we_up), bf(we_down)
    for l in range(depth):
        zc, zr, zg = _win(x2, row(norm_mix_pre[l]), mod, l, bf(w_in[l]), seq)
        uc = _conv(zc, conv_w[l], row(conv_b[l]), row(conv_ln_g[l]), row(conv_ln_b[l]), nb, seq)
        wa = jnp.zeros((LORA_W + LORA_A, 2 * rw), F32)
        wa = wa.at[:LORA_W, :rw].set(w2[l]).at[LORA_W:, rw:].set(a2[l])
        prep_out = _prep(zr, row(mu_shift[l]), row(w0[l]), row(a0[l]), wa, g2[l], row(k_k[l]),
                         row(k_a[l]), row(r_k[l]), nb, seq)
        ur = _scan(prep_out, row(gn_g[l]), row(gn_b[l]), nb, seq)
        x2, hp, logits_t = _merge(uc, ur, zg, x2, bf(p_conv[l]), bf(p_rwkv[l]), bf(w_o[l]),
                                  row(norm_mix_post[l]), row(norm_ffn_pre[l]), mod, l, w_router[l].T, seq)
        x2 = _moe(hp, logits_t, b_router[l].reshape(-1, 1), we_gate_b, we_up_b, we_down_b,
                  bf(ws_gate[l]), bf(ws_up[l]), bf(ws_down[l]), x2, row(norm_ffn_post[l]), mod, l, seq)
    return x2.reshape(nb, seq, d)
```

```python
import functools

import jax
import jax.numpy as jnp
from jax import lax
from jax.experimental import pallas as pl
from jax.experimental.pallas import tpu as pltpu
from jax.experimental.pallas import tpu_sc as plsc

F32 = jnp.float32
BF16 = jnp.bfloat16
HI = lax.Precision.HIGHEST

D_MODEL = 1024
CONV_WIDTH = 512
CONV_KERNEL = 31
RWKV_WIDTH = 512
HEAD = 64
N_HEADS = RWKV_WIDTH // HEAD
LORA_W = 64
LORA_A = 64
LORA_G = 128
N_EXPERTS = 64
TOP_K = 8
N_GROUPS = 8
TOPK_GROUPS = 4
GROUP_SIZE = N_EXPERTS // N_GROUPS
D_EXPERT = 256
ROUTED_SCALE = 2.5
RMS_EPS = 1e-6
LN_EPS = 1e-5
GN_EPS = 64e-5
COL_CONV = 2 * CONV_WIDTH
COL_RWKV = 3 * RWKV_WIDTH + LORA_W + LORA_A + LORA_G
COL_GATE = 2 * D_MODEL
D_IN = COL_CONV + COL_RWKV + COL_GATE

CHUNK = 64
QUAD = 4 * HEAD
N_QUADS = RWKV_WIDTH // QUAD
CUM_BLOCK = 256
CONV_HALO = 32
SUBLANES = 8
SHIFT_HALO = SUBLANES
GATE_LANES = 128
EXPERT_BLOCK = 512
EXPERT_SUB = 256
BLOCKS_PER_STEP = 2
MOE_PARTS = 2
ROW_WORDS = D_MODEL // 2
SC_CORES = 2
SC_SUBCORES = 16
SC_WORKERS = SC_CORES * SC_SUBCORES
SC_CHUNK = 128
VMEM_LIMIT = 52 * 1024 * 1024


def _params(*sem):
    return pltpu.CompilerParams(dimension_semantics=sem, vmem_limit_bytes=VMEM_LIMIT)


def _sigmoid(x):
    return 1.0 / (1.0 + jnp.exp(-x))


def _softplus(x):
    return jnp.maximum(x, 0.0) + jnp.log(1.0 + jnp.exp(-jnp.abs(x)))


def _rms(x, g):
    return x * lax.rsqrt(jnp.mean(x * x, axis=-1, keepdims=True) + RMS_EPS) * g


def _split(x):
    hi = x.astype(BF16)
    return hi, (x - hi.astype(F32)).astype(BF16)


def _dot(a, b):
    return jnp.dot(a, b, preferred_element_type=F32)


def _dot_exact_rhs(x, w):
    hi, lo = _split(x)
    return _dot(hi, w) + _dot(lo, w)


def _pack_rows(y):
    half = y.shape[1] // 2
    packed = pltpu.pack_elementwise([y[:, half:], y[:, :half]], packed_dtype=BF16)
    return lax.bitcast_convert_type(packed, jnp.int32)


def _unpack_rows(u):
    hi = pltpu.unpack_elementwise(u, index=1, packed_dtype=BF16, unpacked_dtype=F32)
    lo = pltpu.unpack_elementwise(u, index=0, packed_dtype=BF16, unpacked_dtype=F32)
    return jnp.concatenate([hi, lo], axis=1)


def _dot_exact_lhs(w, x):
    hi, lo = _split(x)
    return _dot(w, hi) + _dot(w, lo)


def _ada_kernel(c_ref, w_ref, b_ref, o_ref):
    c = c_ref[...]
    o_ref[...] = jnp.dot(c * _sigmoid(c), w_ref[...], precision=HI,
                         preferred_element_type=F32) + b_ref[...]


def _ada(c, w_ada, b_ada):
    nl, d, _ = w_ada.shape
    nb = c.shape[0]
    out = pl.pallas_call(
        _ada_kernel,
        grid=(nl, 6),
        in_specs=[
            pl.BlockSpec((nb, d), lambda l, k: (0, 0)),
            pl.BlockSpec((None, d, d), lambda l, k: (l, 0, k)),
            pl.BlockSpec((None, None, 1, d), lambda l, k: (l, k, 0, 0)),
        ],
        out_specs=pl.BlockSpec((None, None, nb, d), lambda l, k: (l, k, 0, 0)),
        out_shape=jax.ShapeDtypeStruct((nl, 6, nb, d), F32),
        compiler_params=_params("parallel", "parallel"),
        name="ada",
    )(c, w_ada, b_ada.reshape(nl, 6, 1, d))
    return out.reshape(nl, 6, nb, 1, d)


def _mod_spec(layer, piece, rows_per_batch, tm, first_tile=0):
    return pl.BlockSpec((None, None, None, 1, D_MODEL),
                        lambda i, *_: (layer, piece, ((i + first_tile) * tm) // rows_per_batch, 0, 0))


def _win_kernel(x_ref, g_ref, sc_ref, sh_ref, w_ref, zc_ref, zr_ref, zg_ref):
    h = _rms(x_ref[...], g_ref[...]) * (1.0 + sc_ref[...]) + sh_ref[...]
    hb = h.astype(BF16)
    zc_ref[...] = _dot(hb, w_ref[:, :COL_CONV]).astype(zc_ref.dtype)
    zr_ref[...] = _dot(hb, w_ref[:, COL_CONV:COL_CONV + COL_RWKV])
    zg_ref[...] = _dot(hb, w_ref[:, COL_CONV + COL_RWKV:]).astype(zg_ref.dtype)


def _win(x2, g, mod, layer, w_in_b, seq, tm=512):
    n, d = x2.shape
    row = lambda i: (i, 0)
    return pl.pallas_call(
        _win_kernel,
        grid=(n // tm,),
        in_specs=[
            pl.BlockSpec((tm, d), row),
            pl.BlockSpec((1, d), lambda i: (0, 0)),
            _mod_spec(layer, 1, seq, tm),
            _mod_spec(layer, 0, seq, tm),
            pl.BlockSpec((d, D_IN), lambda i: (0, 0)),
        ],
        out_specs=[pl.BlockSpec((tm, COL_CONV), row), pl.BlockSpec((tm, COL_RWKV), row),
                   pl.BlockSpec((tm, COL_GATE), row)],
        out_shape=[jax.ShapeDtypeStruct((n, COL_CONV), BF16), jax.ShapeDtypeStruct((n, COL_RWKV), F32),
                   jax.ShapeDtypeStruct((n, COL_GATE), BF16)],
        compiler_params=_params("parallel"),
        name="win",
    )(x2, g, mod, mod, w_in_b)


def _conv_kernel(z_ref, halo_ref, w_ref, cb_ref, g_ref, b_ref, o_ref, ubuf, *, ts):
    j = pl.program_id(1)
    z = z_ref[...].astype(F32)
    zh = halo_ref[...].astype(F32)
    uh = zh[:, :CONV_WIDTH] * _sigmoid(zh[:, CONV_WIDTH:])
    rows = CONV_HALO + ts
    u = jnp.concatenate([jnp.where(j > 0, uh, 0.0),
                         z[:, :CONV_WIDTH] * _sigmoid(z[:, CONV_WIDTH:])], axis=0)
    ubuf[0] = u
    for s in range(1, SUBLANES):
        ubuf[s] = pltpu.roll(u, rows - s, axis=0)
    sub = 64
    first = CONV_HALO - (CONV_KERNEL - 1)
    for r in range(ts // sub):
        acc = jnp.zeros((sub, CONV_WIDTH), F32) + cb_ref[...]
        for k in range(CONV_KERNEL):
            s = (first + k) % SUBLANES
            base = r * sub + first + k - s
            acc = acc + w_ref[k:k + 1, :] * ubuf[s, base:base + sub, :]
        mu = jnp.mean(acc, axis=-1, keepdims=True)
        dlt = acc - mu
        var = jnp.mean(dlt * dlt, axis=-1, keepdims=True)
        y = dlt * lax.rsqrt(var + LN_EPS) * g_ref[...] + b_ref[...]
        o_ref[r * sub:(r + 1) * sub, :] = (y * _sigmoid(y)).astype(o_ref.dtype)


def _conv(zc, conv_w, conv_b, ln_g, ln_b, nb, seq, ts=256):
    n = zc.shape[0]
    tps = seq // ts
    hpt = ts // CONV_HALO
    vec = pl.BlockSpec((1, CONV_WIDTH), lambda b, j: (0, 0))
    return pl.pallas_call(
        functools.partial(_conv_kernel, ts=ts),
        grid=(nb, tps),
        in_specs=[
            pl.BlockSpec((ts, COL_CONV), lambda b, j: (b * tps + j, 0)),
            pl.BlockSpec((CONV_HALO, COL_CONV), lambda b, j: (jnp.maximum((b * tps + j) * hpt - 1, 0), 0)),
            pl.BlockSpec((CONV_KERNEL, CONV_WIDTH), lambda b, j: (0, 0)),
            vec, vec, vec,
        ],
        out_specs=pl.BlockSpec((ts, CONV_WIDTH), lambda b, j: (b * tps + j, 0)),
        out_shape=jax.ShapeDtypeStruct((n, CONV_WIDTH), BF16),
        scratch_shapes=[pltpu.VMEM((SUBLANES, CONV_HALO + ts, CONV_WIDTH), F32)],
        compiler_params=_params("parallel", "parallel"),
        name="conv",
    )(zc, zc, conv_w, conv_b, ln_g, ln_b)


def _prep_kernel(z_ref, halo_ref, mu_ref, w0_ref, a0_ref, wah_ref, wal_ref, g2_ref, kk_ref, ka_ref, rk_ref,
                 ones_ref, tri_ref, csel_ref,
                 at_ref, bt_ref, kt_ref, rt_ref, v_ref, bts_ref, kts_ref, bonus_ref, g_ref, pc_ref):
    j = pl.program_id(1)
    z = z_ref[...]
    ts = z.shape[0]
    last = jnp.where(j > 0, halo_ref[SHIFT_HALO - 1:SHIFT_HALO, :], 0.0)
    row = lax.broadcasted_iota(jnp.int32, z.shape, 0)
    prev = jnp.where(row == 0, last, pltpu.roll(z, 1, axis=0))
    zs = z + (prev - z) * mu_ref[...]
    rw = RWKV_WIDTH
    r = zs[:, :rw]
    k = zs[:, rw:2 * rw]
    v = zs[:, 2 * rw:3 * rw]
    wad = zs[:, 3 * rw:3 * rw + LORA_W + LORA_A]
    gd = zs[:, 3 * rw + LORA_W + LORA_A:]
    lane = lax.broadcasted_iota(jnp.int32, wad.shape, 1)
    xh, xl = _split(jnp.where(lane < LORA_W, jnp.tanh(wad), wad))
    wah = wah_ref[...]
    lora = _dot(xh, wah) + _dot(xl, wah) + _dot(xh, wal_ref[...])
    w = -_softplus(-(w0_ref[...] + lora[:, :rw])) - 0.5
    lw = -jnp.exp(w)
    a = _sigmoid(a0_ref[...] + lora[:, rw:])
    g_ref[...] = _dot(_sigmoid(gd).astype(BF16), g2_ref[...]).astype(g_ref.dtype)
    kk = k * kk_ref[...]
    k2 = k * (1.0 + (a - 1.0) * ka_ref[...])
    ones = ones_ref[...]
    kk = kk * lax.rsqrt(jnp.maximum(_dot_exact_rhs(kk * kk, ones), 1e-24))
    bonus_ref[...] = (_dot_exact_rhs(r * k2 * rk_ref[...], ones) * v).astype(bonus_ref.dtype)
    tri = tri_ref[...]
    lc, tot = [], []
    for h in range(ts // CUM_BLOCK):
        cs = _dot_exact_lhs(tri, lw[h * CUM_BLOCK:(h + 1) * CUM_BLOCK])
        lc.append(cs[:CUM_BLOCK])
        tot.append(cs[CUM_BLOCK:])
    lc = jnp.concatenate(lc, axis=0)
    tot = jnp.concatenate(tot, axis=0)
    e_neg = jnp.exp(-lc)
    e_end = jnp.exp(tot - lc)
    kka = kk * a
    dt = at_ref.dtype
    at_ref[...] = (-kk * jnp.exp(lc - lw)).astype(dt)
    bt_ref[...] = (kka * e_neg).astype(dt)
    kt_ref[...] = (k2 * e_neg).astype(dt)
    rt_ref[...] = (r * jnp.exp(lc)).astype(dt)
    v_ref[...] = v.astype(dt)
    bts_ref[...] = (kka * e_end).astype(dt)
    kts_ref[...] = (k2 * e_end).astype(dt)
    pc_ref[...] = jnp.exp(_dot_exact_lhs(csel_ref[...], lw))


def _prep(zr, mu, w0, a0, wa, g2, k_k, k_a, r_k, nb, seq, ts=512):
    n = zr.shape[0]
    tps = seq // ts
    hpt = ts // SHIFT_HALO
    rw = RWKV_WIDTH
    cpt = ts // CHUNK
    ones = jnp.kron(jnp.eye(N_HEADS, dtype=F32), jnp.ones((HEAD, HEAD), F32)).astype(BF16)
    blk = jnp.kron(jnp.eye(CUM_BLOCK // CHUNK, dtype=F32), jnp.ones((CHUNK, CHUNK), F32))
    tri = jnp.concatenate([jnp.tril(blk), blk], axis=0).astype(BF16)
    csel = jnp.kron(jnp.eye(cpt, dtype=F32), jnp.ones((1, CHUNK), F32)).astype(BF16)
    wa_hi, wa_lo = _split(wa)
    vec = pl.BlockSpec((1, rw), lambda b, j: (0, 0))
    full = lambda shape: pl.BlockSpec(shape, lambda b, j: (0, 0))
    rows = pl.BlockSpec((ts, rw), lambda b, j: (b * tps + j, 0))
    return pl.pallas_call(
        _prep_kernel,
        grid=(nb, tps),
        in_specs=[
            pl.BlockSpec((ts, COL_RWKV), lambda b, j: (b * tps + j, 0)),
            pl.BlockSpec((SHIFT_HALO, COL_RWKV), lambda b, j: (jnp.maximum((b * tps + j) * hpt - 1, 0), 0)),
            full((1, COL_RWKV)), vec, vec,
            full((LORA_W + LORA_A, 2 * rw)), full((LORA_W + LORA_A, 2 * rw)), full((LORA_G, rw)),
            vec, vec, vec, full((rw, rw)), full((2 * CUM_BLOCK, CUM_BLOCK)), full((cpt, ts)),
        ],
        out_specs=[rows] * 9 + [pl.BlockSpec((cpt, rw), lambda b, j: (b * tps + j, 0))],
        out_shape=[jax.ShapeDtypeStruct((n, rw), BF16)] * 9 + [jax.ShapeDtypeStruct((n // CHUNK, rw), F32)],
        compiler_params=_params("parallel", "parallel"),
        name="prep",
    )(zr, zr, mu, w0, a0, wa_hi, wa_lo, g2.astype(BF16), k_k, k_a, r_k, ones, tri, csel)


def _mm(a, b):
    return jnp.dot(a.astype(BF16), b.astype(BF16), preferred_element_type=F32)


def _mm_nt(a, b):
    return lax.dot_general(a.astype(BF16), b.astype(BF16), (((1,), (1,)), ((), ())),
                           preferred_element_type=F32)


def _mm_tn(a, b):
    return lax.dot_general(a.astype(BF16), b.astype(BF16), (((0,), (0,)), ((), ())),
                           preferred_element_type=F32)


def _scan_kernel(at_ref, bt_ref, kt_ref, rt_ref, v_ref, bts_ref, kts_ref, bonus_ref, g_ref, pc_ref,
                 gng_ref, gnb_ref, ones_ref, o_ref, s_ref, y_ref, *, ts, gb):
    @pl.when(pl.program_id(1) == 0)
    def _():
        s_ref[...] = jnp.zeros_like(s_ref)

    row_q = lax.broadcasted_iota(jnp.int32, (QUAD, QUAD), 0)
    col_q = lax.broadcasted_iota(jnp.int32, (QUAD, QUAD), 1)
    bdmask = (row_q // HEAD) == (col_q // HEAD)
    row_c = lax.broadcasted_iota(jnp.int32, (CHUNK, QUAD), 0)
    col_c = lax.broadcasted_iota(jnp.int32, (CHUNK, QUAD), 1) % CHUNK
    strict = col_c < row_c
    incl = col_c <= row_c
    eye = jnp.where(col_c == row_c, 1.0, 0.0).astype(F32)

    def bd(x):
        xb = x.astype(BF16)
        return jnp.where(bdmask, jnp.concatenate([xb] * (QUAD // CHUNK), axis=0), jnp.zeros((), BF16))

    chains = [(b, qd) for b in range(gb) for qd in range(N_QUADS)]

    def chunk(c, carry):
        sl = pl.ds(pl.multiple_of(c * CHUNK, CHUNK), CHUNK)
        lanes = [slice(qd * QUAD, (qd + 1) * QUAD) for _, qd in chains]
        ld = lambda ref: [ref[b, sl, ln] for (b, _), ln in zip(chains, lanes)]
        at, bt, kt, rt, v, bts, kts = (ld(r) for r in (at_ref, bt_ref, kt_ref, rt_ref, v_ref, bts_ref, kts_ref))
        each = range(len(chains))
        ar = [jnp.concatenate([at[i], rt[i]], axis=0) for i in each]
        pb = [_mm_nt(ar[i], bd(bt[i])) for i in each]
        pk = [_mm_nt(ar[i], bd(kt[i])) for i in each]
        l_ab = [jnp.where(strict, p[:CHUNK], 0.0) for p in pb]
        a_rb = [jnp.where(incl, p[CHUNK:], 0.0) for p in pb]
        l_ak = [jnp.where(strict, p[:CHUNK], 0.0) for p in pk]
        a_rk = [jnp.where(incl, p[CHUNK:], 0.0) for p in pk]
        tm = [eye + l for l in l_ab]
        lp = [_mm(l, bd(l)) for l in l_ab]
        for _ in range(4):
            both = [_mm(jnp.concatenate([tm[i], lp[i]], axis=0), bd(lp[i])) for i in each]
            tm = [tm[i] + both[i][:CHUNK] for i in each]
            lp = [both[i][CHUNK:] for i in each]
        tm = [tm[i] + _mm(tm[i], bd(lp[i])) for i in each]
        bdv = [bd(x) for x in v]
        x0 = [_mm(l_ak[i], bdv[i]) for i in each]
        tu = [_mm(tm[i], jnp.concatenate([bd(x0[i]), bd(at[i])], axis=1)) for i in each]
        u0 = [t[:, :QUAD] for t in tu]
        w = [t[:, QUAD:] for t in tu]
        qy = [_mm(a_rb[i], jnp.concatenate([bd(w[i]), bd(u0[i])], axis=1)) for i in each]
        y0 = [qy[i][:, QUAD:] + _mm(a_rk[i], bdv[i]) for i in each]
        q = [rt[i].astype(F32) + qy[i][:, :QUAD] for i in each]
        s = [s_ref[i] for i in each]
        qw_s = [_mm_nt(jnp.concatenate([q[i], w[i]], axis=0), s[i]) for i in each]
        for i, ((b, _), ln) in enumerate(zip(chains, lanes)):
            y_ref[b, sl, ln] = y0[i] + qw_s[i][:CHUNK]
        u = [(u0[i] + qw_s[i][CHUNK:]).astype(BF16) for i in each]
        n_p = [jnp.where(bdmask, _mm_tn(jnp.concatenate([u[i], v[i]], axis=0),
                                        jnp.concatenate([bts[i], kts[i]], axis=0)), 0.0) for i in each]
        for i, ((b, _), ln) in enumerate(zip(chains, lanes)):
            s_ref[i] = s[i] * pc_ref[b, pl.ds(c, 1), ln] + n_p[i]
        return carry

    lax.fori_loop(0, ts // CHUNK, chunk, 0)

    ones = ones_ref[...]
    for b in range(gb):
        for qd in range(N_QUADS):
            ln = slice(qd * QUAD, (qd + 1) * QUAD)
            y = y_ref[b, :, ln]
            dlt = y - _dot_exact_rhs(y, ones) * (1.0 / HEAD)
            var = _dot_exact_rhs(dlt * dlt, ones) * (1.0 / HEAD)
            yn = dlt * lax.rsqrt(var + GN_EPS) * gng_ref[:, ln] + gnb_ref[:, ln]
            o_ref[b, :, ln] = ((yn + bonus_ref[b, :, ln].astype(F32)) * g_ref[b, :, ln].astype(F32)).astype(o_ref.dtype)


def _scan(prep_out, gn_g, gn_b, nb, seq, ts=128, gb=8):
    n = prep_out[0].shape[0]
    rw = RWKV_WIDTH
    cpt = ts // CHUNK
    ones = jnp.kron(jnp.eye(QUAD // HEAD, dtype=F32), jnp.ones((HEAD, HEAD), F32)).astype(BF16)
    rows = pl.BlockSpec((gb, ts, rw), lambda i, t: (i, t, 0))
    vec = pl.BlockSpec((1, rw), lambda i, t: (0, 0))
    args = [a.reshape(nb, seq, rw) for a in prep_out[:9]] + [prep_out[9].reshape(nb, seq // ts, cpt, rw)]
    out = pl.pallas_call(
        functools.partial(_scan_kernel, ts=ts, gb=gb),
        grid=(nb // gb, seq // ts),
        in_specs=[rows] * 9 + [pl.BlockSpec((gb, None, cpt, rw), lambda i, t: (i, t, 0, 0)), vec, vec,
                               pl.BlockSpec((QUAD, QUAD), lambda i, t: (0, 0))],
        out_specs=rows,
        out_shape=jax.ShapeDtypeStruct((nb, seq, rw), BF16),
        scratch_shapes=[pltpu.VMEM((gb * N_QUADS, QUAD, QUAD), F32), pltpu.VMEM((gb, ts, rw), F32)],
        compiler_params=_params("parallel", "arbitrary"),
        name="scan",
    )(*args, gn_g, gn_b, ones)
    return out.reshape(n, rw)


def _merge_kernel(uc_ref, ur_ref, zg_ref, x_ref, pc_ref, pr_ref, wo_ref, gpost_ref, gt_ref,
                  gpre_ref, sc_ref, sh_ref, wrh_ref, wrl_ref, xo_ref, h_ref, lg_ref):
    yc = _dot(uc_ref[...], pc_ref[...])
    yr = _dot(ur_ref[...], pr_ref[...])
    zg = zg_ref[...].astype(F32)
    m = _sigmoid(zg[:, :D_MODEL]) * yc + _sigmoid(zg[:, D_MODEL:]) * yr
    y = _dot(m.astype(BF16), wo_ref[...])
    xn = x_ref[...] + gt_ref[...] * _rms(y, gpost_ref[...])
    xo_ref[...] = xn
    h = _rms(xn, gpre_ref[...]) * (1.0 + sc_ref[...]) + sh_ref[...]
    h_ref[...] = _pack_rows(h)
    h_hi, h_lo = _split(h)
    nt = lambda a, b: lax.dot_general(a, b, (((1,), (1,)), ((), ())), preferred_element_type=F32)
    w_hi = wrh_ref[...]
    lg_ref[...] = nt(w_hi, h_hi) + nt(w_hi, h_lo) + nt(wrl_ref[...], h_hi)


def _merge(uc, ur, zg, x2, p_conv_b, p_rwkv_b, w_o_b, g_post, g_pre, mod, layer, w_router_t, seq, tm=512):
    n, d = x2.shape
    row = lambda i: (i, 0)
    full = lambda shape: pl.BlockSpec(shape, lambda i: (0, 0))
    return pl.pallas_call(
        _merge_kernel,
        grid=(n // tm,),
        in_specs=[
            pl.BlockSpec((tm, CONV_WIDTH), row), pl.BlockSpec((tm, RWKV_WIDTH), row),
            pl.BlockSpec((tm, COL_GATE), row), pl.BlockSpec((tm, d), row),
            full((CONV_WIDTH, d)), full((RWKV_WIDTH, d)), full((d, d)),
            full((1, d)), _mod_spec(layer, 2, seq, tm),
            full((1, d)), _mod_spec(layer, 4, seq, tm), _mod_spec(layer, 3, seq, tm),
            full((N_EXPERTS, d)), full((N_EXPERTS, d)),
        ],
        out_specs=[pl.BlockSpec((tm, d), row), pl.BlockSpec((tm, ROW_WORDS), row),
                   pl.BlockSpec((N_EXPERTS, tm), lambda i: (0, i))],
        out_shape=[jax.ShapeDtypeStruct((n, d), F32), jax.ShapeDtypeStruct((n, ROW_WORDS), jnp.int32),
                   jax.ShapeDtypeStruct((N_EXPERTS, n), F32)],
        compiler_params=_params("parallel"),
        name="merge",
    )(uc, ur, zg, x2, p_conv_b, p_rwkv_b, w_o_b, g_post, mod, g_pre, mod, mod, *_split(w_router_t))


def _route_kernel(lg_ref, b_ref, gate_ref, sel_ref, cnt_ref):
    s = _sigmoid(lg_ref[...])
    biased = s + b_ref[...]
    t = s.shape[1]
    member = lax.broadcasted_iota(jnp.int32, (GROUP_SIZE, t), 0)
    grp = []
    for g in range(N_GROUPS):
        bg = biased[g * GROUP_SIZE:(g + 1) * GROUP_SIZE, :]
        m1 = jnp.max(bg, axis=0, keepdims=True)
        first = jnp.min(jnp.where(bg == m1, member, GROUP_SIZE), axis=0, keepdims=True)
        m2 = jnp.max(jnp.where(member == first, -jnp.inf, bg), axis=0, keepdims=True)
        grp.append(m1 + m2)
    masked = []
    for g in range(N_GROUPS):
        rank = jnp.zeros((1, t), jnp.int32)
        for o in range(N_GROUPS):
            if o == g:
                continue
            ahead = (grp[o] > grp[g]) if o > g else (grp[o] >= grp[g])
            rank = rank + jnp.where(ahead, 1, 0)
        keep = rank < TOPK_GROUPS
        masked.append(jnp.where(keep, biased[g * GROUP_SIZE:(g + 1) * GROUP_SIZE, :], -jnp.inf))
    masked = jnp.concatenate(masked, axis=0)
    eidx = lax.broadcasted_iota(jnp.int32, masked.shape, 0)
    rank = jnp.zeros(masked.shape, jnp.int32)
    for o in range(N_EXPERTS):
        ro = masked[o:o + 1, :]
        rank = rank + jnp.where(ro > masked, 1, jnp.where((ro == masked) & (eidx > o), 1, 0))
    chosen = rank < TOP_K
    sel = jnp.where(chosen, s, 0.0)
    gate_ref[...] = sel / jnp.sum(sel, axis=0, keepdims=True) * ROUTED_SCALE
    mask = jnp.where(chosen, 1.0, 0.0)
    sel_ref[...] = mask.astype(sel_ref.dtype)

    @pl.when(pl.program_id(0) == 0)
    def _():
        cnt_ref[...] = jnp.zeros_like(cnt_ref)

    cnt_ref[...] += jnp.sum(mask, axis=1, keepdims=True)


def _route(logits_t, b_router, row0, n, tt=512):
    e = logits_t.shape[0]
    first = row0 // tt
    return pl.pallas_call(
        _route_kernel,
        grid=(n // tt,),
        in_specs=[pl.BlockSpec((e, tt), lambda i: (0, i + first)), pl.BlockSpec((e, 1), lambda i: (0, 0))],
        out_specs=[pl.BlockSpec((e, tt), lambda i: (0, i)), pl.BlockSpec((e, tt), lambda i: (0, i)),
                   pl.BlockSpec((e, GATE_LANES), lambda i: (0, 0))],
        out_shape=[jax.ShapeDtypeStruct((e, n), F32), jax.ShapeDtypeStruct((e, n), BF16),
                   jax.ShapeDtypeStruct((e, GATE_LANES), F32)],
        compiler_params=_params("arbitrary"),
        name="route",
    )(logits_t, b_router)


def _dest_kernel(sel_ref, gate_ref, start_ref, triu_ref, below_ref, dest_ref, g8_ref, run_ref):
    @pl.when(pl.program_id(0) == 0)
    def _():
        run_ref[...] = jnp.zeros_like(run_ref)

    sel = sel_ref[...]
    t = sel.shape[1]
    self32 = sel.astype(F32)
    incl = _dot(sel, triu_ref[...])
    pos = start_ref[...] + run_ref[...] + incl - self32
    run_ref[...] += incl[:, t - 1:t]
    slot = _dot(below_ref[...], sel)
    gate = gate_ref[...]
    dst, gts = [], []
    for k in range(TOP_K):
        mine = (self32 > 0.0) & (slot == float(k))
        dst.append(jnp.sum(jnp.where(mine, pos, 0.0), axis=0, keepdims=True))
        gts.append(jnp.sum(jnp.where(mine, gate, 0.0), axis=0, keepdims=True))
    dest_ref[...] = jnp.concatenate(dst, axis=0).astype(jnp.int32)
    g8_ref[...] = jnp.concatenate(gts + [jnp.zeros((GATE_LANES - TOP_K, t), F32)], axis=0).T


def _dest(sel, gate, seg_start, tt=512):
    e, n = sel.shape
    triu = jnp.triu(jnp.ones((tt, tt), F32)).astype(BF16)
    below = jnp.tril(jnp.ones((e, e), F32), -1).astype(BF16)
    return pl.pallas_call(
        _dest_kernel,
        grid=(n // tt,),
        in_specs=[pl.BlockSpec((e, tt), lambda i: (0, i)), pl.BlockSpec((e, tt), lambda i: (0, i)),
                  pl.BlockSpec((e, 1), lambda i: (0, 0)), pl.BlockSpec((tt, tt), lambda i: (0, 0)),
                  pl.BlockSpec((e, e), lambda i: (0, 0))],
        out_specs=[pl.BlockSpec((TOP_K, tt), lambda i: (0, i)), pl.BlockSpec((tt, GATE_LANES), lambda i: (i, 0))],
        out_shape=[jax.ShapeDtypeStruct((TOP_K, n), jnp.int32), jax.ShapeDtypeStruct((n, GATE_LANES), F32)],
        scratch_shapes=[pltpu.VMEM((e, 1), F32)],
        compiler_params=_params("arbitrary"),
        name="dest",
    )(sel, gate, seg_start, triu, below)


def _sc_mesh():
    return plsc.VectorSubcoreMesh(core_axis_name="c", subcore_axis_name="s",
                                  num_cores=SC_CORES, num_subcores=SC_SUBCORES)


def _worker_id():
    return lax.axis_index("s") * SC_CORES + lax.axis_index("c")


def _sc_scatter_rows(rows, row0, dest_flat, n_out):
    n = dest_flat.shape[0] // TOP_K
    w = rows.shape[1]
    per_worker = n // SC_WORKERS

    def body(rows_hbm, dest_hbm, out_hbm, idx_v, rows_v):
        base = _worker_id() * per_worker

        @pl.loop(0, per_worker // SC_CHUNK)
        def _(c):
            t0 = base + c * SC_CHUNK
            pltpu.sync_copy(rows_hbm.at[pl.ds(row0 + t0, SC_CHUNK)], rows_v)
            for k in range(TOP_K):
                pltpu.sync_copy(dest_hbm.at[pl.ds(k * n + t0, SC_CHUNK)], idx_v)
                pltpu.sync_copy(rows_v, out_hbm.at[idx_v])

    return pl.kernel(
        body, out_type=jax.ShapeDtypeStruct((n_out, w), rows.dtype), mesh=_sc_mesh(),
        scratch_types=[pltpu.VMEM((SC_CHUNK,), jnp.int32), pltpu.VMEM((SC_CHUNK, w), rows.dtype)],
        name="sc_scatter",
    )(rows, dest_flat)


def _sc_gather_rows(table, idx_flat):
    m = idx_flat.shape[0]
    w = table.shape[1]
    per_worker = m // SC_WORKERS

    def body(table_hbm, idx_hbm, out_hbm, idx_v, rows_v):
        base = _worker_id() * per_worker

        @pl.loop(0, per_worker // SC_CHUNK)
        def _(c):
            j0 = base + c * SC_CHUNK
            pltpu.sync_copy(idx_hbm.at[pl.ds(j0, SC_CHUNK)], idx_v)
            pltpu.sync_copy(table_hbm.at[idx_v], rows_v)
            pltpu.sync_copy(rows_v, out_hbm.at[pl.ds(j0, SC_CHUNK)])

    return pl.kernel(
        body, out_type=jax.ShapeDtypeStruct((m, w), table.dtype), mesh=_sc_mesh(),
        scratch_types=[pltpu.VMEM((SC_CHUNK,), jnp.int32), pltpu.VMEM((SC_CHUNK, w), table.dtype)],
        name="sc_gather",
    )(table, idx_flat)


def _swiglu(h, wg, wu):
    gte = _dot(h, wg)
    return gte * _sigmoid(gte) * _dot(h, wu)


def _experts_kernel(be_ref, va_ref, xs_ref, *refs):
    nb = BLOCKS_PER_STEP
    wg, wu, wd, ys_ref = refs[:nb], refs[nb:2 * nb], refs[2 * nb:3 * nb], refs[3 * nb]
    first = pl.program_id(0) * nb

    @pl.when(va_ref[first] > 0)
    def _():
        row = lax.broadcasted_iota(jnp.int32, (EXPERT_SUB, D_MODEL), 0)
        subs = [(b, r0) for b in range(nb) for r0 in range(0, EXPERT_BLOCK, EXPERT_SUB)]
        x = []
        for b, r0 in subs:
            lo = b * EXPERT_BLOCK + r0
            xi = _unpack_rows(xs_ref[lo:lo + EXPERT_SUB, :])
            x.append(jnp.where(row < va_ref[first + b] - r0, xi, 0.0).astype(BF16))
        gte = [_dot(xi, wg[b][...]) for xi, (b, _) in zip(x, subs)]
        up = [_dot(xi, wu[b][...]) for xi, (b, _) in zip(x, subs)]
        act = [(g * _sigmoid(g) * u).astype(BF16) for g, u in zip(gte, up)]
        y = [_dot(a, wd[b][...]) for a, (b, _) in zip(act, subs)]
        for yi, (b, r0) in zip(y, subs):
            lo = b * EXPERT_BLOCK + r0
            ys_ref[lo:lo + EXPERT_SUB, :] = _pack_rows(yi)


def _experts(xs, blk_expert, blk_valid, wg, wu, wd, layer):
    n_rows, w = xs.shape
    d = wg.shape[2]
    nb = BLOCKS_PER_STEP
    rows = pl.BlockSpec((nb * EXPERT_BLOCK, w), lambda s, be, va: (s, 0))
    up_spec = lambda i: pl.BlockSpec((None, None, d, D_EXPERT), lambda s, be, va: (layer, be[s * nb + i], 0, 0))
    down_spec = lambda i: pl.BlockSpec((None, None, D_EXPERT, d), lambda s, be, va: (layer, be[s * nb + i], 0, 0))
    return pl.pallas_call(
        _experts_kernel,
        grid_spec=pltpu.PrefetchScalarGridSpec(
            num_scalar_prefetch=2,
            grid=(n_rows // (nb * EXPERT_BLOCK),),
            in_specs=[rows] + [up_spec(i) for i in range(nb)] * 2 + [down_spec(i) for i in range(nb)],
            out_specs=rows,
        ),
        out_shape=jax.ShapeDtypeStruct((n_rows, w), xs.dtype),
        compiler_params=_params("arbitrary"),
        name="experts",
    )(blk_expert, blk_valid, xs, *([wg] * nb), *([wu] * nb), *([wd] * nb))


def _combine_kernel(yg_ref, g8_ref, hp_ref, sg_ref, su_ref, sd_ref, x_ref, gpost_ref, gt_ref, *rest):
    o_ref = rest[-1]
    h = _unpack_rows(hp_ref[...]).astype(BF16)
    acc = _dot(_swiglu(h, sg_ref[...], su_ref[...]).astype(BF16), sd_ref[...])
    g8 = g8_ref[...]
    for k in range(TOP_K):
        acc = acc + g8[:, k:k + 1] * _unpack_rows(yg_ref[k])
    o_ref[...] = x_ref[...] + gt_ref[...] * _rms(acc, gpost_ref[...])


def _combine(yg, g8, hp, sg, su, sd, x2, g_post, mod, layer, seq, row0, earlier, tm=512):
    n, d = x2.shape
    part = g8.shape[0]
    first = row0 // tm
    local = lambda i: (i, 0)
    glob = lambda i: (i + first, 0)
    full = lambda shape: pl.BlockSpec(shape, lambda i: (0, 0))
    in_specs = [
        pl.BlockSpec((TOP_K, tm, ROW_WORDS), lambda i: (0, i, 0)),
        pl.BlockSpec((tm, GATE_LANES), local),
        pl.BlockSpec((tm, ROW_WORDS), glob),
        full((d, D_EXPERT)), full((d, D_EXPERT)), full((D_EXPERT, d)),
        pl.BlockSpec((tm, d), glob),
        full((1, d)),
        _mod_spec(layer, 5, seq, tm, first),
    ]
    args = [yg, g8, hp, sg, su, sd, x2, g_post, mod]
    aliases = {}
    if earlier is not None:
        in_specs.append(pl.BlockSpec(memory_space=pl.ANY))
        args.append(earlier)
        aliases = {len(args) - 1: 0}
    return pl.pallas_call(
        _combine_kernel,
        grid=(part // tm,),
        in_specs=in_specs,
        out_specs=pl.BlockSpec((tm, d), glob),
        out_shape=jax.ShapeDtypeStruct((n, d), F32),
        input_output_aliases=aliases,
        compiler_params=_params("parallel"),
        name="combine",
    )(*args)


def _moe(hp, logits_t, b_router, wg, wu, wd, sg, su, sd, x2, g_post, mod, layer, seq):
    n = x2.shape[0]
    part = n // MOE_PARTS
    out = None
    for p in range(MOE_PARTS):
        row0 = p * part
        gate, sel, counts = _route(logits_t, b_router, row0, part)
        cnt = counts[:, 0].astype(jnp.int32)
        padded = (cnt + EXPERT_BLOCK - 1) // EXPERT_BLOCK * EXPERT_BLOCK
        seg_end = jnp.cumsum(padded)
        seg_start = seg_end - padded
        n_blocks = part * TOP_K // EXPERT_BLOCK + N_EXPERTS
        blk_row = jnp.arange(n_blocks, dtype=jnp.int32) * EXPERT_BLOCK
        blk_expert = jnp.sum((seg_end[None, :] <= blk_row[:, None]).astype(jnp.int32), axis=1)
        blk_expert = jnp.minimum(blk_expert, N_EXPERTS - 1)
        mine = blk_expert[:, None] == jnp.arange(N_EXPERTS, dtype=jnp.int32)[None, :]
        seg_last = jnp.sum(jnp.where(mine, (seg_start + cnt)[None, :], 0), axis=1)
        blk_valid = jnp.clip(seg_last - blk_row, 0, EXPERT_BLOCK)
        dest, g8 = _dest(sel, gate, seg_start.astype(F32).reshape(-1, 1))
        dest_flat = dest.reshape(-1)
        xs = _sc_scatter_rows(hp, row0, dest_flat, n_blocks * EXPERT_BLOCK)
        ys = _experts(xs, blk_expert, blk_valid.astype(jnp.int32), wg, wu, wd, layer)
        yg = _sc_gather_rows(ys, dest_flat).reshape(TOP_K, part, ROW_WORDS)
        out = _combine(yg, g8, hp, sg, su, sd, x2, g_post, mod, layer, seq, row0, out)
    return out


def kernel(x, c, w_ada, b_ada, norm_mix_pre, norm_mix_post, norm_ffn_pre, norm_ffn_post, w_in, mu_shift, conv_w, conv_b, conv_ln_g, conv_ln_b, p_conv, w0, w2, a0, a2, g2, k_k, k_a, r_k, gn_g, gn_b, p_rwkv, w_o, w_router, b_router, we_gate, we_up, we_down, ws_gate, ws_up, ws_down):
    nb, seq, d = x.shape
    depth = w_ada.shape[0]
    n = nb * seq
    rw = RWKV_WIDTH
    mod = _ada(c, w_ada, b_ada)
    x2 = x.reshape(n, d)
    row = lambda a: a.reshape(1, -1)
    bf = lambda a: a.astype(BF16)
    we_gate_b, we_up_b, we_down_b = bf(we_gate), bf(we_up), bf(we_down)
    for l in range(depth):
        zc, zr, zg = _win(x2, row(norm_mix_pre[l]), mod, l, bf(w_in[l]), seq)
        uc = _conv(zc, conv_w[l], row(conv_b[l]), row(conv_ln_g[l]), row(conv_ln_b[l]), nb, seq)
        wa = jnp.zeros((LORA_W + LORA_A, 2 * rw), F32)
        wa = wa.at[:LORA_W, :rw].set(w2[l]).at[LORA_W:, rw:].set(a2[l])
        prep_out = _prep(zr, row(mu_shift[l]), row(w0[l]), row(a0[l]), wa, g2[l], row(k_k[l]),
                         row(k_a[l]), row(r_k[l]), nb, seq)
        ur = _scan(prep_out, row(gn_g[l]), row(gn_b[l]), nb, seq)
        x2, hp, logits_t = _merge(uc, ur, zg, x2, bf(p_conv[l]), bf(p_rwkv[l]), bf(w_o[l]),
                                  row(norm_mix_post[l]), row(norm_ffn_pre[l]), mod, l, w_router[l].T, seq)
        x2 = _moe(hp, logits_t, b_router[l].reshape(-1, 1), we_gate_b, we_up_b, we_down_b,
                  bf(ws_gate[l]), bf(ws_up[l]), bf(ws_down[l]), x2, row(norm_ffn_post[l]), mod, l, seq)
    return x2.reshape(nb, seq, d)
```

```python
import functools

import jax
import jax.numpy as jnp
from jax import lax
from jax.experimental import pallas as pl
from jax.experimental.pallas import tpu as pltpu
from jax.experimental.pallas import tpu_sc as plsc

F32 = jnp.float32
BF16 = jnp.bfloat16
HI = lax.Precision.HIGHEST

D_MODEL = 1024
CONV_WIDTH = 512
CONV_KERNEL = 31
RWKV_WIDTH = 512
HEAD = 64
N_HEADS = RWKV_WIDTH // HEAD
LORA_W = 64
LORA_A = 64
LORA_G = 128
N_EXPERTS = 64
TOP_K = 8
N_GROUPS = 8
TOPK_GROUPS = 4
GROUP_SIZE = N_EXPERTS // N_GROUPS
D_EXPERT = 256
ROUTED_SCALE = 2.5
RMS_EPS = 1e-6
LN_EPS = 1e-5
GN_EPS = 64e-5
COL_CONV = 2 * CONV_WIDTH
COL_RWKV = 3 * RWKV_WIDTH + LORA_W + LORA_A + LORA_G
COL_GATE = 2 * D_MODEL
D_IN = COL_CONV + COL_RWKV + COL_GATE

CHUNK = 64
QUAD = 4 * HEAD
N_QUADS = RWKV_WIDTH // QUAD
CUM_BLOCK = 256
CONV_HALO = 32
SUBLANES = 8
SHIFT_HALO = SUBLANES
GATE_LANES = 128
EXPERT_BLOCK = 512
EXPERT_SUB = 256
BLOCKS_PER_STEP = 2
MOE_PARTS = 1
ROW_WORDS = D_MODEL // 2
SC_CORES = 2
SC_SUBCORES = 16
SC_WORKERS = SC_CORES * SC_SUBCORES
SC_CHUNK = 128
VMEM_LIMIT = 52 * 1024 * 1024


def _params(*sem):
    return pltpu.CompilerParams(dimension_semantics=sem, vmem_limit_bytes=VMEM_LIMIT)


def _sigmoid(x):
    return 1.0 / (1.0 + jnp.exp(-x))


def _softplus(x):
    return jnp.maximum(x, 0.0) + jnp.log(1.0 + jnp.exp(-jnp.abs(x)))


def _rms(x, g):
    return x * lax.rsqrt(jnp.mean(x * x, axis=-1, keepdims=True) + RMS_EPS) * g


def _split(x):
    hi = x.astype(BF16)
    return hi, (x - hi.astype(F32)).astype(BF16)


def _dot(a, b):
    return jnp.dot(a, b, preferred_element_type=F32)


def _dot_exact_rhs(x, w):
    hi, lo = _split(x)
    return _dot(hi, w) + _dot(lo, w)


def _pack_rows(y):
    half = y.shape[1] // 2
    packed = pltpu.pack_elementwise([y[:, half:], y[:, :half]], packed_dtype=BF16)
    return lax.bitcast_convert_type(packed, jnp.int32)


def _unpack_rows(u):
    hi = pltpu.unpack_elementwise(u, index=1, packed_dtype=BF16, unpacked_dtype=F32)
    lo = pltpu.unpack_elementwise(u, index=0, packed_dtype=BF16, unpacked_dtype=F32)
    return jnp.concatenate([hi, lo], axis=1)


def _dot_exact_lhs(w, x):
    hi, lo = _split(x)
    return _dot(w, hi) + _dot(w, lo)


def _ada_kernel(c_ref, w_ref, b_ref, o_ref):
    c = c_ref[...]
    o_ref[...] = jnp.dot(c * _sigmoid(c), w_ref[...], precision=HI,
                         preferred_element_type=F32) + b_ref[...]


def _ada(c, w_ada, b_ada):
    nl, d, _ = w_ada.shape
    nb = c.shape[0]
    out = pl.pallas_call(
        _ada_kernel,
        grid=(nl, 6),
        in_specs=[
            pl.BlockSpec((nb, d), lambda l, k: (0, 0)),
            pl.BlockSpec((None, d, d), lambda l, k: (l, 0, k)),
            pl.BlockSpec((None, None, 1, d), lambda l, k: (l, k, 0, 0)),
        ],
        out_specs=pl.BlockSpec((None, None, nb, d), lambda l, k: (l, k, 0, 0)),
        out_shape=jax.ShapeDtypeStruct((nl, 6, nb, d), F32),
        compiler_params=_params("parallel", "parallel"),
        name="ada",
    )(c, w_ada, b_ada.reshape(nl, 6, 1, d))
    return out.reshape(nl, 6, nb, 1, d)


def _mod_spec(layer, piece, rows_per_batch, tm, first_tile=0):
    return pl.BlockSpec((None, None, None, 1, D_MODEL),
                        lambda i, *_: (layer, piece, ((i + first_tile) * tm) // rows_per_batch, 0, 0))


def _win_kernel(x_ref, g_ref, sc_ref, sh_ref, w_ref, zc_ref, zr_ref, zg_ref):
    h = _rms(x_ref[...], g_ref[...]) * (1.0 + sc_ref[...]) + sh_ref[...]
    hb = h.astype(BF16)
    zc_ref[...] = _dot(hb, w_ref[:, :COL_CONV]).astype(zc_ref.dtype)
    zr_ref[...] = _dot(hb, w_ref[:, COL_CONV:COL_CONV + COL_RWKV])
    zg_ref[...] = _dot(hb, w_ref[:, COL_CONV + COL_RWKV:]).astype(zg_ref.dtype)


def _win(x2, g, mod, layer, w_in_b, seq, tm=512):
    n, d = x2.shape
    row = lambda i: (i, 0)
    return pl.pallas_call(
        _win_kernel,
        grid=(n // tm,),
        in_specs=[
            pl.BlockSpec((tm, d), row),
            pl.BlockSpec((1, d), lambda i: (0, 0)),
            _mod_spec(layer, 1, seq, tm),
            _mod_spec(layer, 0, seq, tm),
            pl.BlockSpec((d, D_IN), lambda i: (0, 0)),
        ],
        out_specs=[pl.BlockSpec((tm, COL_CONV), row), pl.BlockSpec((tm, COL_RWKV), row),
                   pl.BlockSpec((tm, COL_GATE), row)],
        out_shape=[jax.ShapeDtypeStruct((n, COL_CONV), BF16), jax.ShapeDtypeStruct((n, COL_RWKV), F32),
                   jax.ShapeDtypeStruct((n, COL_GATE), BF16)],
        compiler_params=_params("parallel"),
        name="win",
    )(x2, g, mod, mod, w_in_b)


def _conv_kernel(z_ref, halo_ref, w_ref, cb_ref, g_ref, b_ref, o_ref, ubuf, *, ts):
    j = pl.program_id(1)
    z = z_ref[...].astype(F32)
    zh = halo_ref[...].astype(F32)
    uh = zh[:, :CONV_WIDTH] * _sigmoid(zh[:, CONV_WIDTH:])
    rows = CONV_HALO + ts
    u = jnp.concatenate([jnp.where(j > 0, uh, 0.0),
                         z[:, :CONV_WIDTH] * _sigmoid(z[:, CONV_WIDTH:])], axis=0)
    ubuf[0] = u
    for s in range(1, SUBLANES):
        ubuf[s] = pltpu.roll(u, rows - s, axis=0)
    sub = 64
    first = CONV_HALO - (CONV_KERNEL - 1)
    for r in range(ts // sub):
        acc = jnp.zeros((sub, CONV_WIDTH), F32) + cb_ref[...]
        for k in range(CONV_KERNEL):
            s = (first + k) % SUBLANES
            base = r * sub + first + k - s
            acc = acc + w_ref[k:k + 1, :] * ubuf[s, base:base + sub, :]
        mu = jnp.mean(acc, axis=-1, keepdims=True)
        dlt = acc - mu
        var = jnp.mean(dlt * dlt, axis=-1, keepdims=True)
        y = dlt * lax.rsqrt(var + LN_EPS) * g_ref[...] + b_ref[...]
        o_ref[r * sub:(r + 1) * sub, :] = (y * _sigmoid(y)).astype(o_ref.dtype)


def _conv(zc, conv_w, conv_b, ln_g, ln_b, nb, seq, ts=512):
    n = zc.shape[0]
    tps = seq // ts
    hpt = ts // CONV_HALO
    vec = pl.BlockSpec((1, CONV_WIDTH), lambda b, j: (0, 0))
    return pl.pallas_call(
        functools.partial(_conv_kernel, ts=ts),
        grid=(nb, tps),
        in_specs=[
            pl.BlockSpec((ts, COL_CONV), lambda b, j: (b * tps + j, 0)),
            pl.BlockSpec((CONV_HALO, COL_CONV), lambda b, j: (jnp.maximum((b * tps + j) * hpt - 1, 0), 0)),
            pl.BlockSpec((CONV_KERNEL, CONV_WIDTH), lambda b, j: (0, 0)),
            vec, vec, vec,
        ],
        out_specs=pl.BlockSpec((ts, CONV_WIDTH), lambda b, j: (b * tps + j, 0)),
        out_shape=jax.ShapeDtypeStruct((n, CONV_WIDTH), BF16),
        scratch_shapes=[pltpu.VMEM((SUBLANES, CONV_HALO + ts, CONV_WIDTH), F32)],
        compiler_params=_params("parallel", "parallel"),
        name="conv",
    )(zc, zc, conv_w, conv_b, ln_g, ln_b)


def _prep_kernel(z_ref, halo_ref, mu_ref, w0_ref, a0_ref, wah_ref, wal_ref, g2_ref, kk_ref, ka_ref, rk_ref,
                 ones_ref, tri_ref, csel_ref,
                 at_ref, bt_ref, kt_ref, rt_ref, v_ref, bts_ref, kts_ref, bonus_ref, g_ref, pc_ref):
    j = pl.program_id(1)
    z = z_ref[...]
    ts = z.shape[0]
    last = jnp.where(j > 0, halo_ref[SHIFT_HALO - 1:SHIFT_HALO, :], 0.0)
    row = lax.broadcasted_iota(jnp.int32, z.shape, 0)
    prev = jnp.where(row == 0, last, pltpu.roll(z, 1, axis=0))
    zs = z + (prev - z) * mu_ref[...]
    rw = RWKV_WIDTH
    r = zs[:, :rw]
    k = zs[:, rw:2 * rw]
    v = zs[:, 2 * rw:3 * rw]
    wad = zs[:, 3 * rw:3 * rw + LORA_W + LORA_A]
    gd = zs[:, 3 * rw + LORA_W + LORA_A:]
    lane = lax.broadcasted_iota(jnp.int32, wad.shape, 1)
    xh, xl = _split(jnp.where(lane < LORA_W, jnp.tanh(wad), wad))
    wah = wah_ref[...]
    lora = _dot(xh, wah) + _dot(xl, wah) + _dot(xh, wal_ref[...])
    w = -_softplus(-(w0_ref[...] + lora[:, :rw])) - 0.5
    lw = -jnp.exp(w)
    a = _sigmoid(a0_ref[...] + lora[:, rw:])
    g_ref[...] = _dot(_sigmoid(gd).astype(BF16), g2_ref[...]).astype(g_ref.dtype)
    kk = k * kk_ref[...]
    k2 = k * (1.0 + (a - 1.0) * ka_ref[...])
    ones = ones_ref[...]
    kk = kk * lax.rsqrt(jnp.maximum(_dot_exact_rhs(kk * kk, ones), 1e-24))
    bonus_ref[...] = (_dot_exact_rhs(r * k2 * rk_ref[...], ones) * v).astype(bonus_ref.dtype)
    tri = tri_ref[...]
    lc, tot = [], []
    for h in range(ts // CUM_BLOCK):
        cs = _dot_exact_lhs(tri, lw[h * CUM_BLOCK:(h + 1) * CUM_BLOCK])
        lc.append(cs[:CUM_BLOCK])
        tot.append(cs[CUM_BLOCK:])
    lc = jnp.concatenate(lc, axis=0)
    tot = jnp.concatenate(tot, axis=0)
    e_neg = jnp.exp(-lc)
    e_end = jnp.exp(tot - lc)
    kka = kk * a
    dt = at_ref.dtype
    at_ref[...] = (-kk * jnp.exp(lc - lw)).astype(dt)
    bt_ref[...] = (kka * e_neg).astype(dt)
    kt_ref[...] = (k2 * e_neg).astype(dt)
    rt_ref[...] = (r * jnp.exp(lc)).astype(dt)
    v_ref[...] = v.astype(dt)
    bts_ref[...] = (kka * e_end).astype(dt)
    kts_ref[...] = (k2 * e_end).astype(dt)
    pc_ref[...] = jnp.exp(_dot_exact_lhs(csel_ref[...], lw))


def _prep(zr, mu, w0, a0, wa, g2, k_k, k_a, r_k, nb, seq, ts=512):
    n = zr.shape[0]
    tps = seq // ts
    hpt = ts // SHIFT_HALO
    rw = RWKV_WIDTH
    cpt = ts // CHUNK
    ones = jnp.kron(jnp.eye(N_HEADS, dtype=F32), jnp.ones((HEAD, HEAD), F32)).astype(BF16)
    blk = jnp.kron(jnp.eye(CUM_BLOCK // CHUNK, dtype=F32), jnp.ones((CHUNK, CHUNK), F32))
    tri = jnp.concatenate([jnp.tril(blk), blk], axis=0).astype(BF16)
    csel = jnp.kron(jnp.eye(cpt, dtype=F32), jnp.ones((1, CHUNK), F32)).astype(BF16)
    wa_hi, wa_lo = _split(wa)
    vec = pl.BlockSpec((1, rw), lambda b, j: (0, 0))
    full = lambda shape: pl.BlockSpec(shape, lambda b, j: (0, 0))
    rows = pl.BlockSpec((ts, rw), lambda b, j: (b * tps + j, 0))
    return pl.pallas_call(
        _prep_kernel,
        grid=(nb, tps),
        in_specs=[
            pl.BlockSpec((ts, COL_RWKV), lambda b, j: (b * tps + j, 0)),
            pl.BlockSpec((SHIFT_HALO, COL_RWKV), lambda b, j: (jnp.maximum((b * tps + j) * hpt - 1, 0), 0)),
            full((1, COL_RWKV)), vec, vec,
            full((LORA_W + LORA_A, 2 * rw)), full((LORA_W + LORA_A, 2 * rw)), full((LORA_G, rw)),
            vec, vec, vec, full((rw, rw)), full((2 * CUM_BLOCK, CUM_BLOCK)), full((cpt, ts)),
        ],
        out_specs=[rows] * 9 + [pl.BlockSpec((cpt, rw), lambda b, j: (b * tps + j, 0))],
        out_shape=[jax.ShapeDtypeStruct((n, rw), BF16)] * 9 + [jax.ShapeDtypeStruct((n // CHUNK, rw), F32)],
        compiler_params=_params("parallel", "parallel"),
        name="prep",
    )(zr, zr, mu, w0, a0, wa_hi, wa_lo, g2.astype(BF16), k_k, k_a, r_k, ones, tri, csel)


def _mm(a, b):
    return jnp.dot(a.astype(BF16), b.astype(BF16), preferred_element_type=F32)


def _mm_nt(a, b):
    return lax.dot_general(a.astype(BF16), b.astype(BF16), (((1,), (1,)), ((), ())),
                           preferred_element_type=F32)


def _mm_tn(a, b):
    return lax.dot_general(a.astype(BF16), b.astype(BF16), (((0,), (0,)), ((), ())),
                           preferred_element_type=F32)


def _scan_kernel(at_ref, bt_ref, kt_ref, rt_ref, v_ref, bts_ref, kts_ref, bonus_ref, g_ref, pc_ref,
                 gng_ref, gnb_ref, ones_ref, o_ref, s_ref, y_ref, *, ts, gb):
    @pl.when(pl.program_id(1) == 0)
    def _():
        s_ref[...] = jnp.zeros_like(s_ref)

    row_q = lax.broadcasted_iota(jnp.int32, (QUAD, QUAD), 0)
    col_q = lax.broadcasted_iota(jnp.int32, (QUAD, QUAD), 1)
    bdmask = (row_q // HEAD) == (col_q // HEAD)
    row_c = lax.broadcasted_iota(jnp.int32, (CHUNK, QUAD), 0)
    col_c = lax.broadcasted_iota(jnp.int32, (CHUNK, QUAD), 1) % CHUNK
    strict = col_c < row_c
    incl = col_c <= row_c
    eye = jnp.where(col_c == row_c, 1.0, 0.0).astype(F32)

    def bd(x):
        xb = x.astype(BF16)
        return jnp.where(bdmask, jnp.concatenate([xb] * (QUAD // CHUNK), axis=0), jnp.zeros((), BF16))

    chains = [(b, qd) for b in range(gb) for qd in range(N_QUADS)]

    def chunk(c, carry):
        sl = pl.ds(pl.multiple_of(c * CHUNK, CHUNK), CHUNK)
        lanes = [slice(qd * QUAD, (qd + 1) * QUAD) for _, qd in chains]
        ld = lambda ref: [ref[b, sl, ln] for (b, _), ln in zip(chains, lanes)]
        at, bt, kt, rt, v, bts, kts = (ld(r) for r in (at_ref, bt_ref, kt_ref, rt_ref, v_ref, bts_ref, kts_ref))
        each = range(len(chains))
        ar = [jnp.concatenate([at[i], rt[i]], axis=0) for i in each]
        pb = [_mm_nt(ar[i], bd(bt[i])) for i in each]
        pk = [_mm_nt(ar[i], bd(kt[i])) for i in each]
        l_ab = [jnp.where(strict, p[:CHUNK], 0.0) for p in pb]
        a_rb = [jnp.where(incl, p[CHUNK:], 0.0) for p in pb]
        l_ak = [jnp.where(strict, p[:CHUNK], 0.0) for p in pk]
        a_rk = [jnp.where(incl, p[CHUNK:], 0.0) for p in pk]
        tm = [eye + l for l in l_ab]
        lp = [_mm(l, bd(l)) for l in l_ab]
        for _ in range(4):
            both = [_mm(jnp.concatenate([tm[i], lp[i]], axis=0), bd(lp[i])) for i in each]
            tm = [tm[i] + both[i][:CHUNK] for i in each]
            lp = [both[i][CHUNK:] for i in each]
        tm = [tm[i] + _mm(tm[i], bd(lp[i])) for i in each]
        bdv = [bd(x) for x in v]
        x0 = [_mm(l_ak[i], bdv[i]) for i in each]
        tu = [_mm(tm[i], jnp.concatenate([bd(x0[i]), bd(at[i])], axis=1)) for i in each]
        u0 = [t[:, :QUAD] for t in tu]
        w = [t[:, QUAD:] for t in tu]
        qy = [_mm(a_rb[i], jnp.concatenate([bd(w[i]), bd(u0[i])], axis=1)) for i in each]
        y0 = [qy[i][:, QUAD:] + _mm(a_rk[i], bdv[i]) for i in each]
        q = [rt[i].astype(F32) + qy[i][:, :QUAD] for i in each]
        s = [s_ref[i] for i in each]
        qw_s = [_mm_nt(jnp.concatenate([q[i], w[i]], axis=0), s[i]) for i in each]
        for i, ((b, _), ln) in enumerate(zip(chains, lanes)):
            y_ref[b, sl, ln] = y0[i] + qw_s[i][:CHUNK]
        u = [(u0[i] + qw_s[i][CHUNK:]).astype(BF16) for i in each]
        n_p = [jnp.where(bdmask, _mm_tn(jnp.concatenate([u[i], v[i]], axis=0),
                                        jnp.concatenate([bts[i], kts[i]], axis=0)), 0.0) for i in each]
        for i, ((b, _), ln) in enumerate(zip(chains, lanes)):
            s_ref[i] = s[i] * pc_ref[b, pl.ds(c, 1), ln] + n_p[i]
        return carry

    lax.fori_loop(0, ts // CHUNK, chunk, 0)

    ones = ones_ref[...]
    for b in range(gb):
        for qd in range(N_QUADS):
            ln = slice(qd * QUAD, (qd + 1) * QUAD)
            y = y_ref[b, :, ln]
            dlt = y - _dot_exact_rhs(y, ones) * (1.0 / HEAD)
            var = _dot_exact_rhs(dlt * dlt, ones) * (1.0 / HEAD)
            yn = dlt * lax.rsqrt(var + GN_EPS) * gng_ref[:, ln] + gnb_ref[:, ln]
            o_ref[b, :, ln] = ((yn + bonus_ref[b, :, ln].astype(F32)) * g_ref[b, :, ln].astype(F32)).astype(o_ref.dtype)


def _scan(prep_out, gn_g, gn_b, nb, seq, ts=128, gb=8):
    n = prep_out[0].shape[0]
    rw = RWKV_WIDTH
    cpt = ts // CHUNK
    ones = jnp.kron(jnp.eye(QUAD // HEAD, dtype=F32), jnp.ones((HEAD, HEAD), F32)).astype(BF16)
    rows = pl.BlockSpec((gb, ts, rw), lambda i, t: (i, t, 0))
    vec = pl.BlockSpec((1, rw), lambda i, t: (0, 0))
    args = [a.reshape(nb, seq, rw) for a in prep_out[:9]] + [prep_out[9].reshape(nb, seq // ts, cpt, rw)]
    out = pl.pallas_call(
        functools.partial(_scan_kernel, ts=ts, gb=gb),
        grid=(nb // gb, seq // ts),
        in_specs=[rows] * 9 + [pl.BlockSpec((gb, None, cpt, rw), lambda i, t: (i, t, 0, 0)), vec, vec,
                               pl.BlockSpec((QUAD, QUAD), lambda i, t: (0, 0))],
        out_specs=rows,
        out_shape=jax.ShapeDtypeStruct((nb, seq, rw), BF16),
        scratch_shapes=[pltpu.VMEM((gb * N_QUADS, QUAD, QUAD), F32), pltpu.VMEM((gb, ts, rw), F32)],
        compiler_params=_params("parallel", "arbitrary"),
        name="scan",
    )(*args, gn_g, gn_b, ones)
    return out.reshape(n, rw)


def _merge_kernel(uc_ref, ur_ref, zg_ref, x_ref, pc_ref, pr_ref, wo_ref, gpost_ref, gt_ref,
                  gpre_ref, sc_ref, sh_ref, wrh_ref, wrl_ref, xo_ref, h_ref, lg_ref):
    yc = _dot(uc_ref[...], pc_ref[...])
    yr = _dot(ur_ref[...], pr_ref[...])
    zg = zg_ref[...].astype(F32)
    m = _sigmoid(zg[:, :D_MODEL]) * yc + _sigmoid(zg[:, D_MODEL:]) * yr
    y = _dot(m.astype(BF16), wo_ref[...])
    xn = x_ref[...] + gt_ref[...] * _rms(y, gpost_ref[...])
    xo_ref[...] = xn
    h = _rms(xn, gpre_ref[...]) * (1.0 + sc_ref[...]) + sh_ref[...]
    h_ref[...] = _pack_rows(h)
    h_hi, h_lo = _split(h)
    nt = lambda a, b: lax.dot_general(a, b, (((1,), (1,)), ((), ())), preferred_element_type=F32)
    w_hi = wrh_ref[...]
    lg_ref[...] = nt(w_hi, h_hi) + nt(w_hi, h_lo) + nt(wrl_ref[...], h_hi)


def _merge(uc, ur, zg, x2, p_conv_b, p_rwkv_b, w_o_b, g_post, g_pre, mod, layer, w_router_t, seq, tm=512):
    n, d = x2.shape
    row = lambda i: (i, 0)
    full = lambda shape: pl.BlockSpec(shape, lambda i: (0, 0))
    return pl.pallas_call(
        _merge_kernel,
        grid=(n // tm,),
        in_specs=[
            pl.BlockSpec((tm, CONV_WIDTH), row), pl.BlockSpec((tm, RWKV_WIDTH), row),
            pl.BlockSpec((tm, COL_GATE), row), pl.BlockSpec((tm, d), row),
            full((CONV_WIDTH, d)), full((RWKV_WIDTH, d)), full((d, d)),
            full((1, d)), _mod_spec(layer, 2, seq, tm),
            full((1, d)), _mod_spec(layer, 4, seq, tm), _mod_spec(layer, 3, seq, tm),
            full((N_EXPERTS, d)), full((N_EXPERTS, d)),
        ],
        out_specs=[pl.BlockSpec((tm, d), row), pl.BlockSpec((tm, ROW_WORDS), row),
                   pl.BlockSpec((N_EXPERTS, tm), lambda i: (0, i))],
        out_shape=[jax.ShapeDtypeStruct((n, d), F32), jax.ShapeDtypeStruct((n, ROW_WORDS), jnp.int32),
                   jax.ShapeDtypeStruct((N_EXPERTS, n), F32)],
        compiler_params=_params("parallel"),
        name="merge",
    )(uc, ur, zg, x2, p_conv_b, p_rwkv_b, w_o_b, g_post, mod, g_pre, mod, mod, *_split(w_router_t))


def _route_kernel(lg_ref, b_ref, gate_ref, sel_ref, cnt_ref):
    s = _sigmoid(lg_ref[...])
    biased = s + b_ref[...]
    t = s.shape[1]
    member = lax.broadcasted_iota(jnp.int32, (GROUP_SIZE, t), 0)
    grp = []
    for g in range(N_GROUPS):
        bg = biased[g * GROUP_SIZE:(g + 1) * GROUP_SIZE, :]
        m1 = jnp.max(bg, axis=0, keepdims=True)
        first = jnp.min(jnp.where(bg == m1, member, GROUP_SIZE), axis=0, keepdims=True)
        m2 = jnp.max(jnp.where(member == first, -jnp.inf, bg), axis=0, keepdims=True)
        grp.append(m1 + m2)
    masked = []
    for g in range(N_GROUPS):
        rank = jnp.zeros((1, t), jnp.int32)
        for o in range(N_GROUPS):
            if o == g:
                continue
            ahead = (grp[o] > grp[g]) if o > g else (grp[o] >= grp[g])
            rank = rank + jnp.where(ahead, 1, 0)
        keep = rank < TOPK_GROUPS
        masked.append(jnp.where(keep, biased[g * GROUP_SIZE:(g + 1) * GROUP_SIZE, :], -jnp.inf))
    masked = jnp.concatenate(masked, axis=0)
    eidx = lax.broadcasted_iota(jnp.int32, masked.shape, 0)
    mask = jnp.zeros(masked.shape, F32)
    for _ in range(TOP_K):
        top = jnp.max(masked, axis=0, keepdims=True)
        hit = eidx == jnp.min(jnp.where(masked == top, eidx, N_EXPERTS), axis=0, keepdims=True)
        mask = jnp.where(hit, 1.0, mask)
        masked = jnp.where(hit, -jnp.inf, masked)
    sel = s * mask
    gate_ref[...] = sel / jnp.sum(sel, axis=0, keepdims=True) * ROUTED_SCALE
    sel_ref[...] = mask.astype(sel_ref.dtype)

    @pl.when(pl.program_id(0) == 0)
    def _():
        cnt_ref[...] = jnp.zeros_like(cnt_ref)

    cnt_ref[...] += jnp.sum(mask, axis=1, keepdims=True)


def _route(logits_t, b_router, row0, n, tt=512):
    e = logits_t.shape[0]
    first = row0 // tt
    return pl.pallas_call(
        _route_kernel,
        grid=(n // tt,),
        in_specs=[pl.BlockSpec((e, tt), lambda i: (0, i + first)), pl.BlockSpec((e, 1), lambda i: (0, 0))],
        out_specs=[pl.BlockSpec((e, tt), lambda i: (0, i)), pl.BlockSpec((e, tt), lambda i: (0, i)),
                   pl.BlockSpec((e, GATE_LANES), lambda i: (0, 0))],
        out_shape=[jax.ShapeDtypeStruct((e, n), F32), jax.ShapeDtypeStruct((e, n), BF16),
                   jax.ShapeDtypeStruct((e, GATE_LANES), F32)],
        compiler_params=_params("arbitrary"),
        name="route",
    )(logits_t, b_router)


def _dest_kernel(sel_ref, gate_ref, start_ref, triu_ref, below_ref, dest_ref, g8_ref, run_ref):
    @pl.when(pl.program_id(0) == 0)
    def _():
        run_ref[...] = jnp.zeros_like(run_ref)

    sel = sel_ref[...]
    t = sel.shape[1]
    self32 = sel.astype(F32)
    incl = _dot(sel, triu_ref[...])
    pos = start_ref[...] + run_ref[...] + incl - self32
    run_ref[...] += incl[:, t - 1:t]
    slot = _dot(below_ref[...], sel)
    gate = gate_ref[...]
    dst, gts = [], []
    for k in range(TOP_K):
        mine = (self32 > 0.0) & (slot == float(k))
        dst.append(jnp.sum(jnp.where(mine, pos, 0.0), axis=0, keepdims=True))
        gts.append(jnp.sum(jnp.where(mine, gate, 0.0), axis=0, keepdims=True))
    dest_ref[...] = jnp.concatenate(dst, axis=0).astype(jnp.int32)
    g8_ref[...] = jnp.concatenate(gts + [jnp.zeros((GATE_LANES - TOP_K, t), F32)], axis=0).T


def _dest(sel, gate, seg_start, tt=512):
    e, n = sel.shape
    triu = jnp.triu(jnp.ones((tt, tt), F32)).astype(BF16)
    below = jnp.tril(jnp.ones((e, e), F32), -1).astype(BF16)
    return pl.pallas_call(
        _dest_kernel,
        grid=(n // tt,),
        in_specs=[pl.BlockSpec((e, tt), lambda i: (0, i)), pl.BlockSpec((e, tt), lambda i: (0, i)),
                  pl.BlockSpec((e, 1), lambda i: (0, 0)), pl.BlockSpec((tt, tt), lambda i: (0, 0)),
                  pl.BlockSpec((e, e), lambda i: (0, 0))],
        out_specs=[pl.BlockSpec((TOP_K, tt), lambda i: (0, i)), pl.BlockSpec((tt, GATE_LANES), lambda i: (i, 0))],
        out_shape=[jax.ShapeDtypeStruct((TOP_K, n), jnp.int32), jax.ShapeDtypeStruct((n, GATE_LANES), F32)],
        scratch_shapes=[pltpu.VMEM((e, 1), F32)],
        compiler_params=_params("arbitrary"),
        name="dest",
    )(sel, gate, seg_start, triu, below)


def _sc_mesh():
    return plsc.VectorSubcoreMesh(core_axis_name="c", subcore_axis_name="s",
                                  num_cores=SC_CORES, num_subcores=SC_SUBCORES)


def _worker_id():
    return lax.axis_index("s") * SC_CORES + lax.axis_index("c")


def _sc_scatter_rows(rows, row0, dest_flat, n_out):
    n = dest_flat.shape[0] // TOP_K
    w = rows.shape[1]
    per_worker = n // SC_WORKERS

    def body(rows_hbm, dest_hbm, out_hbm, idx_v, rows_v):
        base = _worker_id() * per_worker

        @pl.loop(0, per_worker // SC_CHUNK)
        def _(c):
            t0 = base + c * SC_CHUNK
            pltpu.sync_copy(rows_hbm.at[pl.ds(row0 + t0, SC_CHUNK)], rows_v)
            for k in range(TOP_K):
                pltpu.sync_copy(dest_hbm.at[pl.ds(k * n + t0, SC_CHUNK)], idx_v)
                pltpu.sync_copy(rows_v, out_hbm.at[idx_v])

    return pl.kernel(
        body, out_type=jax.ShapeDtypeStruct((n_out, w), rows.dtype), mesh=_sc_mesh(),
        scratch_types=[pltpu.VMEM((SC_CHUNK,), jnp.int32), pltpu.VMEM((SC_CHUNK, w), rows.dtype)],
        name="sc_scatter",
    )(rows, dest_flat)


def _sc_gather_rows(table, idx_flat):
    m = idx_flat.shape[0]
    w = table.shape[1]
    per_worker = m // SC_WORKERS

    def body(table_hbm, idx_hbm, out_hbm, idx_v, rows_v):
        base = _worker_id() * per_worker

        @pl.loop(0, per_worker // SC_CHUNK)
        def _(c):
            j0 = base + c * SC_CHUNK
            pltpu.sync_copy(idx_hbm.at[pl.ds(j0, SC_CHUNK)], idx_v)
            pltpu.sync_copy(table_hbm.at[idx_v], rows_v)
            pltpu.sync_copy(rows_v, out_hbm.at[pl.ds(j0, SC_CHUNK)])

    return pl.kernel(
        body, out_type=jax.ShapeDtypeStruct((m, w), table.dtype), mesh=_sc_mesh(),
        scratch_types=[pltpu.VMEM((SC_CHUNK,), jnp.int32), pltpu.VMEM((SC_CHUNK, w), table.dtype)],
        name="sc_gather",
    )(table, idx_flat)


def _swiglu(h, wg, wu):
    gte = _dot(h, wg)
    return gte * _sigmoid(gte) * _dot(h, wu)


def _experts_kernel(be_ref, va_ref, xs_ref, *refs):
    nb = BLOCKS_PER_STEP
    wg, wu, wd, ys_ref = refs[:nb], refs[nb:2 * nb], refs[2 * nb:3 * nb], refs[3 * nb]
    first = pl.program_id(0) * nb

    @pl.when(va_ref[first] > 0)
    def _():
        row = lax.broadcasted_iota(jnp.int32, (EXPERT_SUB, D_MODEL), 0)
        subs = [(b, r0) for b in range(nb) for r0 in range(0, EXPERT_BLOCK, EXPERT_SUB)]
        x = []
        for b, r0 in subs:
            lo = b * EXPERT_BLOCK + r0
            xi = _unpack_rows(xs_ref[lo:lo + EXPERT_SUB, :])
            x.append(jnp.where(row < va_ref[first + b] - r0, xi, 0.0).astype(BF16))
        gte = [_dot(xi, wg[b][...]) for xi, (b, _) in zip(x, subs)]
        up = [_dot(xi, wu[b][...]) for xi, (b, _) in zip(x, subs)]
        act = [(g * _sigmoid(g) * u).astype(BF16) for g, u in zip(gte, up)]
        y = [_dot(a, wd[b][...]) for a, (b, _) in zip(act, subs)]
        for yi, (b, r0) in zip(y, subs):
            lo = b * EXPERT_BLOCK + r0
            ys_ref[lo:lo + EXPERT_SUB, :] = _pack_rows(yi)


def _experts(xs, blk_expert, blk_valid, wg, wu, wd, layer):
    n_rows, w = xs.shape
    d = wg.shape[2]
    nb = BLOCKS_PER_STEP
    rows = pl.BlockSpec((nb * EXPERT_BLOCK, w), lambda s, be, va: (s, 0))
    up_spec = lambda i: pl.BlockSpec((None, None, d, D_EXPERT), lambda s, be, va: (layer, be[s * nb + i], 0, 0))
    down_spec = lambda i: pl.BlockSpec((None, None, D_EXPERT, d), lambda s, be, va: (layer, be[s * nb + i], 0, 0))
    return pl.pallas_call(
        _experts_kernel,
        grid_spec=pltpu.PrefetchScalarGridSpec(
            num_scalar_prefetch=2,
            grid=(n_rows // (nb * EXPERT_BLOCK),),
            in_specs=[rows] + [up_spec(i) for i in range(nb)] * 2 + [down_spec(i) for i in range(nb)],
            out_specs=rows,
        ),
        out_shape=jax.ShapeDtypeStruct((n_rows, w), xs.dtype),
        compiler_params=_params("arbitrary"),
        name="experts",
    )(blk_expert, blk_valid, xs, *([wg] * nb), *([wu] * nb), *([wd] * nb))


def _combine_kernel(yg_ref, g8_ref, hp_ref, sg_ref, su_ref, sd_ref, x_ref, gpost_ref, gt_ref, *rest):
    o_ref = rest[-1]
    h = _unpack_rows(hp_ref[...]).astype(BF16)
    acc = _dot(_swiglu(h, sg_ref[...], su_ref[...]).astype(BF16), sd_ref[...])
    g8 = g8_ref[...]
    for k in range(TOP_K):
        acc = acc + g8[:, k:k + 1] * _unpack_rows(yg_ref[k])
    o_ref[...] = x_ref[...] + gt_ref[...] * _rms(acc, gpost_ref[...])


def _combine(yg, g8, hp, sg, su, sd, x2, g_post, mod, layer, seq, row0, earlier, tm=512):
    n, d = x2.shape
    part = g8.shape[0]
    first = row0 // tm
    local = lambda i: (i, 0)
    glob = lambda i: (i + first, 0)
    full = lambda shape: pl.BlockSpec(shape, lambda i: (0, 0))
    in_specs = [
        pl.BlockSpec((TOP_K, tm, ROW_WORDS), lambda i: (0, i, 0)),
        pl.BlockSpec((tm, GATE_LANES), local),
        pl.BlockSpec((tm, ROW_WORDS), glob),
        full((d, D_EXPERT)), full((d, D_EXPERT)), full((D_EXPERT, d)),
        pl.BlockSpec((tm, d), glob),
        full((1, d)),
        _mod_spec(layer, 5, seq, tm, first),
    ]
    args = [yg, g8, hp, sg, su, sd, x2, g_post, mod]
    aliases = {}
    if earlier is not None:
        in_specs.append(pl.BlockSpec(memory_space=pl.ANY))
        args.append(earlier)
        aliases = {len(args) - 1: 0}
    return pl.pallas_call(
        _combine_kernel,
        grid=(part // tm,),
        in_specs=in_specs,
        out_specs=pl.BlockSpec((tm, d), glob),
        out_shape=jax.ShapeDtypeStruct((n, d), F32),
        input_output_aliases=aliases,
        compiler_params=_params("parallel"),
        name="combine",
    )(*args)


def _moe(hp, logits_t, b_router, wg, wu, wd, sg, su, sd, x2, g_post, mod, layer, seq):
    n = x2.shape[0]
    part = n // MOE_PARTS
    out = None
    for p in range(MOE_PARTS):
        row0 = p * part
        gate, sel, counts = _route(logits_t, b_router, row0, part)
        cnt = counts[:, 0].astype(jnp.int32)
        padded = (cnt + EXPERT_BLOCK - 1) // EXPERT_BLOCK * EXPERT_BLOCK
        seg_end = jnp.cumsum(padded)
        seg_start = seg_end - padded
        n_blocks = part * TOP_K // EXPERT_BLOCK + N_EXPERTS
        blk_row = jnp.arange(n_blocks, dtype=jnp.int32) * EXPERT_BLOCK
        blk_expert = jnp.sum((seg_end[None, :] <= blk_row[:, None]).astype(jnp.int32), axis=1)
        blk_expert = jnp.minimum(blk_expert, N_EXPERTS - 1)
        mine = blk_expert[:, None] == jnp.arange(N_EXPERTS, dtype=jnp.int32)[None, :]
        seg_last = jnp.sum(jnp.where(mine, (seg_start + cnt)[None, :], 0), axis=1)
        blk_valid = jnp.clip(seg_last - blk_row, 0, EXPERT_BLOCK)
        dest, g8 = _dest(sel, gate, seg_start.astype(F32).reshape(-1, 1))
        dest_flat = dest.reshape(-1)
        xs = _sc_scatter_rows(hp, row0, dest_flat, n_blocks * EXPERT_BLOCK)
        ys = _experts(xs, blk_expert, blk_valid.astype(jnp.int32), wg, wu, wd, layer)
        yg = _sc_gather_rows(ys, dest_flat).reshape(TOP_K, part, ROW_WORDS)
        out = _combine(yg, g8, hp, sg, su, sd, x2, g_post, mod, layer, seq, row0, out)
    return out


def kernel(x, c, w_ada, b_ada, norm_mix_pre, norm_mix_post, norm_ffn_pre, norm_ffn_post, w_in, mu_shift, conv_w, conv_b, conv_ln_g, conv_ln_b, p_conv, w0, w2, a0, a2, g2, k_k, k_a, r_k, gn_g, gn_b, p_rwkv, w_o, w_router, b_router, we_gate, we_up, we_down, ws_gate, ws_up, ws_down):
    nb, seq, d = x.shape
    depth = w_ada.shape[0]
    n = nb * seq
    rw = RWKV_WIDTH
    mod = _ada(c, w_ada, b_ada)
    x2 = x.reshape(n, d)
    row = lambda a: a.reshape(1, -1)
    bf = lambda a: a.astype(BF16)
    we_gate_b, we_up_b, we_down_b = bf(we_gate), bf(we_up), bf(we_down)
    for l in range(depth):
        zc, zr, zg = _win(x2, row(norm_mix_pre[l]), mod, l, bf(w_in[l]), seq)
        uc = _conv(zc, conv_w[l], row(conv_b[l]), row(conv_ln_g[l]), row(conv_ln_b[l]), nb, seq)
        wa = jnp.zeros((LORA_W + LORA_A, 2 * rw), F32)
        wa = wa.at[:LORA_W, :rw].set(w2[l]).at[LORA_W:, rw:].set(a2[l])
        prep_out = _prep(zr, row(mu_shift[l]), row(w0[l]), row(a0[l]), wa, g2[l], row(k_k[l]),
                         row(k_a[l]), row(r_k[l]), nb, seq)
        ur = _scan(prep_out, row(gn_g[l]), row(gn_b[l]), nb, seq)
        x2, hp, logits_t = _merge(uc, ur, zg, x2, bf(p_conv[l]), bf(p_rwkv[l]), bf(w_o[l]),
                                  row(norm_mix_post[l]), row(norm_ffn_pre[l]), mod, l, w_router[l].T, seq)
        x2 = _moe(hp, logits_t, b_router[l].reshape(-1, 1), we_gate_b, we_up_b, we_down_b,
                  bf(ws_gate[l]), bf(ws_up[l]), bf(ws_down[l]), x2, row(norm_ffn_post[l]), mod, l, seq)
    return x2.reshape(nb, seq, d)
```

```python
import functools

import jax
import jax.numpy as jnp
from jax import lax
from jax.experimental import pallas as pl
from jax.experimental.pallas import tpu as pltpu
from jax.experimental.pallas import tpu_sc as plsc

F32 = jnp.float32
BF16 = jnp.bfloat16
HI = lax.Precision.HIGHEST

D_MODEL = 1024
CONV_WIDTH = 512
CONV_KERNEL = 31
RWKV_WIDTH = 512
HEAD = 64
N_HEADS = RWKV_WIDTH // HEAD
LORA_W = 64
LORA_A = 64
LORA_G = 128
N_EXPERTS = 64
TOP_K = 8
N_GROUPS = 8
TOPK_GROUPS = 4
GROUP_SIZE = N_EXPERTS // N_GROUPS
D_EXPERT = 256
ROUTED_SCALE = 2.5
RMS_EPS = 1e-6
LN_EPS = 1e-5
GN_EPS = 64e-5
COL_CONV = 2 * CONV_WIDTH
COL_RWKV = 3 * RWKV_WIDTH + LORA_W + LORA_A + LORA_G
COL_GATE = 2 * D_MODEL
D_IN = COL_CONV + COL_RWKV + COL_GATE

CHUNK = 64
QUAD = 4 * HEAD
N_QUADS = RWKV_WIDTH // QUAD
CUM_BLOCK = 256
CONV_HALO = 32
SUBLANES = 8
SHIFT_HALO = SUBLANES
GATE_LANES = 128
EXPERT_BLOCK = 512
EXPERT_SUB = 256
BLOCKS_PER_STEP = 2
MOE_PARTS = 1
ROW_WORDS = D_MODEL // 2
SC_CORES = 2
SC_SUBCORES = 16
SC_WORKERS = SC_CORES * SC_SUBCORES
SC_CHUNK = 128
VMEM_LIMIT = 52 * 1024 * 1024


def _params(*sem):
    return pltpu.CompilerParams(dimension_semantics=sem, vmem_limit_bytes=VMEM_LIMIT)


def _sigmoid(x):
    return 1.0 / (1.0 + jnp.exp(-x))


def _softplus(x):
    return jnp.maximum(x, 0.0) + jnp.log(1.0 + jnp.exp(-jnp.abs(x)))


def _rms(x, g):
    return x * lax.rsqrt(jnp.mean(x * x, axis=-1, keepdims=True) + RMS_EPS) * g


def _split(x):
    hi = x.astype(BF16)
    return hi, (x - hi.astype(F32)).astype(BF16)


def _dot(a, b):
    return jnp.dot(a, b, preferred_element_type=F32)


def _dot_exact_rhs(x, w):
    hi, lo = _split(x)
    return _dot(hi, w) + _dot(lo, w)


def _pack_rows(y):
    half = y.shape[1] // 2
    packed = pltpu.pack_elementwise([y[:, half:], y[:, :half]], packed_dtype=BF16)
    return lax.bitcast_convert_type(packed, jnp.int32)


def _unpack_rows(u):
    hi = pltpu.unpack_elementwise(u, index=1, packed_dtype=BF16, unpacked_dtype=F32)
    lo = pltpu.unpack_elementwise(u, index=0, packed_dtype=BF16, unpacked_dtype=F32)
    return jnp.concatenate([hi, lo], axis=1)


def _dot_exact_lhs(w, x):
    hi, lo = _split(x)
    return _dot(w, hi) + _dot(w, lo)


def _ada_kernel(c_ref, w_ref, b_ref, o_ref):
    c = c_ref[...]
    o_ref[...] = jnp.dot(c * _sigmoid(c), w_ref[...], precision=HI,
                         preferred_element_type=F32) + b_ref[...]


def _ada(c, w_ada, b_ada):
    nl, d, _ = w_ada.shape
    nb = c.shape[0]
    out = pl.pallas_call(
        _ada_kernel,
        grid=(nl, 6),
        in_specs=[
            pl.BlockSpec((nb, d), lambda l, k: (0, 0)),
            pl.BlockSpec((None, d, d), lambda l, k: (l, 0, k)),
            pl.BlockSpec((None, None, 1, d), lambda l, k: (l, k, 0, 0)),
        ],
        out_specs=pl.BlockSpec((None, None, nb, d), lambda l, k: (l, k, 0, 0)),
        out_shape=jax.ShapeDtypeStruct((nl, 6, nb, d), F32),
        compiler_params=_params("parallel", "parallel"),
        name="ada",
    )(c, w_ada, b_ada.reshape(nl, 6, 1, d))
    return out.reshape(nl, 6, nb, 1, d)


def _mod_spec(layer, piece, rows_per_batch, tm, first_tile=0):
    return pl.BlockSpec((None, None, None, 1, D_MODEL),
                        lambda i, *_: (layer, piece, ((i + first_tile) * tm) // rows_per_batch, 0, 0))


def _win_kernel(x_ref, g_ref, sc_ref, sh_ref, w_ref, zc_ref, zr_ref, zg_ref):
    h = _rms(x_ref[...], g_ref[...]) * (1.0 + sc_ref[...]) + sh_ref[...]
    hb = h.astype(BF16)
    zc_ref[...] = _dot(hb, w_ref[:, :COL_CONV]).astype(zc_ref.dtype)
    zr_ref[...] = _dot(hb, w_ref[:, COL_CONV:COL_CONV + COL_RWKV])
    zg_ref[...] = _dot(hb, w_ref[:, COL_CONV + COL_RWKV:]).astype(zg_ref.dtype)


def _win(x2, g, mod, layer, w_in_b, seq, tm=512):
    n, d = x2.shape
    row = lambda i: (i, 0)
    return pl.pallas_call(
        _win_kernel,
        grid=(n // tm,),
        in_specs=[
            pl.BlockSpec((tm, d), row),
            pl.BlockSpec((1, d), lambda i: (0, 0)),
            _mod_spec(layer, 1, seq, tm),
            _mod_spec(layer, 0, seq, tm),
            pl.BlockSpec((d, D_IN), lambda i: (0, 0)),
        ],
        out_specs=[pl.BlockSpec((tm, COL_CONV), row), pl.BlockSpec((tm, COL_RWKV), row),
                   pl.BlockSpec((tm, COL_GATE), row)],
        out_shape=[jax.ShapeDtypeStruct((n, COL_CONV), BF16), jax.ShapeDtypeStruct((n, COL_RWKV), F32),
                   jax.ShapeDtypeStruct((n, COL_GATE), BF16)],
        compiler_params=_params("parallel"),
        name="win",
    )(x2, g, mod, mod, w_in_b)


def _conv_kernel(z_ref, halo_ref, w_ref, cb_ref, g_ref, b_ref, o_ref, ubuf, *, ts):
    j = pl.program_id(1)
    z = z_ref[...].astype(F32)
    zh = halo_ref[...].astype(F32)
    uh = zh[:, :CONV_WIDTH] * _sigmoid(zh[:, CONV_WIDTH:])
    rows = CONV_HALO + ts
    u = jnp.concatenate([jnp.where(j > 0, uh, 0.0),
                         z[:, :CONV_WIDTH] * _sigmoid(z[:, CONV_WIDTH:])], axis=0)
    ubuf[0] = u
    for s in range(1, SUBLANES):
        ubuf[s] = pltpu.roll(u, rows - s, axis=0)
    sub = 64
    first = CONV_HALO - (CONV_KERNEL - 1)
    for r in range(ts // sub):
        acc = jnp.zeros((sub, CONV_WIDTH), F32) + cb_ref[...]
        for k in range(CONV_KERNEL):
            s = (first + k) % SUBLANES
            base = r * sub + first + k - s
            acc = acc + w_ref[k:k + 1, :] * ubuf[s, base:base + sub, :]
        mu = jnp.mean(acc, axis=-1, keepdims=True)
        dlt = acc - mu
        var = jnp.mean(dlt * dlt, axis=-1, keepdims=True)
        y = dlt * lax.rsqrt(var + LN_EPS) * g_ref[...] + b_ref[...]
        o_ref[r * sub:(r + 1) * sub, :] = (y * _sigmoid(y)).astype(o_ref.dtype)


def _conv(zc, conv_w, conv_b, ln_g, ln_b, nb, seq, ts=512):
    n = zc.shape[0]
    tps = seq // ts
    hpt = ts // CONV_HALO
    vec = pl.BlockSpec((1, CONV_WIDTH), lambda b, j: (0, 0))
    return pl.pallas_call(
        functools.partial(_conv_kernel, ts=ts),
        grid=(nb, tps),
        in_specs=[
            pl.BlockSpec((ts, COL_CONV), lambda b, j: (b * tps + j, 0)),
            pl.BlockSpec((CONV_HALO, COL_CONV), lambda b, j: (jnp.maximum((b * tps + j) * hpt - 1, 0), 0)),
            pl.BlockSpec((CONV_KERNEL, CONV_WIDTH), lambda b, j: (0, 0)),
            vec, vec, vec,
        ],
        out_specs=pl.BlockSpec((ts, CONV_WIDTH), lambda b, j: (b * tps + j, 0)),
        out_shape=jax.ShapeDtypeStruct((n, CONV_WIDTH), BF16),
        scratch_shapes=[pltpu.VMEM((SUBLANES, CONV_HALO + ts, CONV_WIDTH), F32)],
        compiler_params=_params("parallel", "parallel"),
        name="conv",
    )(zc, zc, conv_w, conv_b, ln_g, ln_b)


def _prep_kernel(z_ref, halo_ref, mu_ref, w0_ref, a0_ref, wah_ref, wal_ref, g2_ref, kk_ref, ka_ref, rk_ref,
                 ones_ref, tri_ref, csel_ref,
                 at_ref, bt_ref, kt_ref, rt_ref, v_ref, bts_ref, kts_ref, bonus_ref, g_ref, pc_ref):
    j = pl.program_id(1)
    z = z_ref[...]
    ts = z.shape[0]
    last = jnp.where(j > 0, halo_ref[SHIFT_HALO - 1:SHIFT_HALO, :], 0.0)
    row = lax.broadcasted_iota(jnp.int32, z.shape, 0)
    prev = jnp.where(row == 0, last, pltpu.roll(z, 1, axis=0))
    zs = z + (prev - z) * mu_ref[...]
    rw = RWKV_WIDTH
    r = zs[:, :rw]
    k = zs[:, rw:2 * rw]
    v = zs[:, 2 * rw:3 * rw]
    wad = zs[:, 3 * rw:3 * rw + LORA_W + LORA_A]
    gd = zs[:, 3 * rw + LORA_W + LORA_A:]
    lane = lax.broadcasted_iota(jnp.int32, wad.shape, 1)
    xh, xl = _split(jnp.where(lane < LORA_W, jnp.tanh(wad), wad))
    wah = wah_ref[...]
    lora = _dot(xh, wah) + _dot(xl, wah) + _dot(xh, wal_ref[...])
    w = -_softplus(-(w0_ref[...] + lora[:, :rw])) - 0.5
    lw = -jnp.exp(w)
    a = _sigmoid(a0_ref[...] + lora[:, rw:])
    g_ref[...] = _dot(_sigmoid(gd).astype(BF16), g2_ref[...]).astype(g_ref.dtype)
    kk = k * kk_ref[...]
    k2 = k * (1.0 + (a - 1.0) * ka_ref[...])
    ones = ones_ref[...]
    kk = kk * lax.rsqrt(jnp.maximum(_dot_exact_rhs(kk * kk, ones), 1e-24))
    bonus_ref[...] = (_dot_exact_rhs(r * k2 * rk_ref[...], ones) * v).astype(bonus_ref.dtype)
    tri = tri_ref[...]
    lc, tot = [], []
    for h in range(ts // CUM_BLOCK):
        cs = _dot_exact_lhs(tri, lw[h * CUM_BLOCK:(h + 1) * CUM_BLOCK])
        lc.append(cs[:CUM_BLOCK])
        tot.append(cs[CUM_BLOCK:])
    lc = jnp.concatenate(lc, axis=0)
    tot = jnp.concatenate(tot, axis=0)
    e_neg = jnp.exp(-lc)
    e_end = jnp.exp(tot - lc)
    kka = kk * a
    dt = at_ref.dtype
    at_ref[...] = (-kk * jnp.exp(lc - lw)).astype(dt)
    bt_ref[...] = (kka * e_neg).astype(dt)
    kt_ref[...] = (k2 * e_neg).astype(dt)
    rt_ref[...] = (r * jnp.exp(lc)).astype(dt)
    v_ref[...] = v.astype(dt)
    bts_ref[...] = (kka * e_end).astype(dt)
    kts_ref[...] = (k2 * e_end).astype(dt)
    pc_ref[...] = jnp.exp(_dot_exact_lhs(csel_ref[...], lw))


def _prep(zr, mu, w0, a0, wa, g2, k_k, k_a, r_k, nb, seq, ts=512):
    n = zr.shape[0]
    tps = seq // ts
    hpt = ts // SHIFT_HALO
    rw = RWKV_WIDTH
    cpt = ts // CHUNK
    ones = jnp.kron(jnp.eye(N_HEADS, dtype=F32), jnp.ones((HEAD, HEAD), F32)).astype(BF16)
    blk = jnp.kron(jnp.eye(CUM_BLOCK // CHUNK, dtype=F32), jnp.ones((CHUNK, CHUNK), F32))
    tri = jnp.concatenate([jnp.tril(blk), blk], axis=0).astype(BF16)
    csel = jnp.kron(jnp.eye(cpt, dtype=F32), jnp.ones((1, CHUNK), F32)).astype(BF16)
    wa_hi, wa_lo = _split(wa)
    vec = pl.BlockSpec((1, rw), lambda b, j: (0, 0))
    full = lambda shape: pl.BlockSpec(shape, lambda b, j: (0, 0))
    rows = pl.BlockSpec((ts, rw), lambda b, j: (b * tps + j, 0))
    return pl.pallas_call(
        _prep_kernel,
        grid=(nb, tps),
        in_specs=[
            pl.BlockSpec((ts, COL_RWKV), lambda b, j: (b * tps + j, 0)),
            pl.BlockSpec((SHIFT_HALO, COL_RWKV), lambda b, j: (jnp.maximum((b * tps + j) * hpt - 1, 0), 0)),
            full((1, COL_RWKV)), vec, vec,
            full((LORA_W + LORA_A, 2 * rw)), full((LORA_W + LORA_A, 2 * rw)), full((LORA_G, rw)),
            vec, vec, vec, full((rw, rw)), full((2 * CUM_BLOCK, CUM_BLOCK)), full((cpt, ts)),
        ],
        out_specs=[rows] * 9 + [pl.BlockSpec((cpt, rw), lambda b, j: (b * tps + j, 0))],
        out_shape=[jax.ShapeDtypeStruct((n, rw), BF16)] * 9 + [jax.ShapeDtypeStruct((n // CHUNK, rw), F32)],
        compiler_params=_params("parallel", "parallel"),
        name="prep",
    )(zr, zr, mu, w0, a0, wa_hi, wa_lo, g2.astype(BF16), k_k, k_a, r_k, ones, tri, csel)


def _mm(a, b):
    return jnp.dot(a.astype(BF16), b.astype(BF16), preferred_element_type=F32)


def _mm_nt(a, b):
    return lax.dot_general(a.astype(BF16), b.astype(BF16), (((1,), (1,)), ((), ())),
                           preferred_element_type=F32)


def _mm_tn(a, b):
    return lax.dot_general(a.astype(BF16), b.astype(BF16), (((0,), (0,)), ((), ())),
                           preferred_element_type=F32)


def _scan_kernel(at_ref, bt_ref, kt_ref, rt_ref, v_ref, bts_ref, kts_ref, bonus_ref, g_ref, pc_ref,
                 gng_ref, gnb_ref, ones_ref, o_ref, s_ref, y_ref, *, ts, gb):
    @pl.when(pl.program_id(1) == 0)
    def _():
        s_ref[...] = jnp.zeros_like(s_ref)

    row_q = lax.broadcasted_iota(jnp.int32, (QUAD, QUAD), 0)
    col_q = lax.broadcasted_iota(jnp.int32, (QUAD, QUAD), 1)
    bdmask = (row_q // HEAD) == (col_q // HEAD)
    row_c = lax.broadcasted_iota(jnp.int32, (CHUNK, QUAD), 0)
    col_c = lax.broadcasted_iota(jnp.int32, (CHUNK, QUAD), 1) % CHUNK
    strict = col_c < row_c
    incl = col_c <= row_c
    eye = jnp.where(col_c == row_c, 1.0, 0.0).astype(F32)

    def bd(x):
        xb = x.astype(BF16)
        return jnp.where(bdmask, jnp.concatenate([xb] * (QUAD // CHUNK), axis=0), jnp.zeros((), BF16))

    chains = [(b, qd) for b in range(gb) for qd in range(N_QUADS)]

    def chunk(c, carry):
        sl = pl.ds(pl.multiple_of(c * CHUNK, CHUNK), CHUNK)
        lanes = [slice(qd * QUAD, (qd + 1) * QUAD) for _, qd in chains]
        ld = lambda ref: [ref[b, sl, ln] for (b, _), ln in zip(chains, lanes)]
        at, bt, kt, rt, v, bts, kts = (ld(r) for r in (at_ref, bt_ref, kt_ref, rt_ref, v_ref, bts_ref, kts_ref))
        each = range(len(chains))
        ar = [jnp.concatenate([at[i], rt[i]], axis=0) for i in each]
        pb = [_mm_nt(ar[i], bd(bt[i])) for i in each]
        pk = [_mm_nt(ar[i], bd(kt[i])) for i in each]
        l_ab = [jnp.where(strict, p[:CHUNK], 0.0) for p in pb]
        a_rb = [jnp.where(incl, p[CHUNK:], 0.0) for p in pb]
        l_ak = [jnp.where(strict, p[:CHUNK], 0.0) for p in pk]
        a_rk = [jnp.where(incl, p[CHUNK:], 0.0) for p in pk]
        tm = [eye + l for l in l_ab]
        lp = [_mm(l, bd(l)) for l in l_ab]
        for _ in range(4):
            both = [_mm(jnp.concatenate([tm[i], lp[i]], axis=0), bd(lp[i])) for i in each]
            tm = [tm[i] + both[i][:CHUNK] for i in each]
            lp = [both[i][CHUNK:] for i in each]
        tm = [tm[i] + _mm(tm[i], bd(lp[i])) for i in each]
        bdv = [bd(x) for x in v]
        x0 = [_mm(l_ak[i], bdv[i]) for i in each]
        tu = [_mm(tm[i], jnp.concatenate([bd(x0[i]), bd(at[i])], axis=1)) for i in each]
        u0 = [t[:, :QUAD] for t in tu]
        w = [t[:, QUAD:] for t in tu]
        qy = [_mm(a_rb[i], jnp.concatenate([bd(w[i]), bd(u0[i])], axis=1)) for i in each]
        y0 = [qy[i][:, QUAD:] + _mm(a_rk[i], bdv[i]) for i in each]
        q = [rt[i].astype(F32) + qy[i][:, :QUAD] for i in each]
        s = [s_ref[i] for i in each]
        qw_s = [_mm_nt(jnp.concatenate([q[i], w[i]], axis=0), s[i]) for i in each]
        for i, ((b, _), ln) in enumerate(zip(chains, lanes)):
            y_ref[b, sl, ln] = y0[i] + qw_s[i][:CHUNK]
        u = [(u0[i] + qw_s[i][CHUNK:]).astype(BF16) for i in each]
        n_p = [jnp.where(bdmask, _mm_tn(jnp.concatenate([u[i], v[i]], axis=0),
                                        jnp.concatenate([bts[i], kts[i]], axis=0)), 0.0) for i in each]
        for i, ((b, _), ln) in enumerate(zip(chains, lanes)):
            s_ref[i] = s[i] * pc_ref[b, pl.ds(c, 1), ln] + n_p[i]
        return carry

    lax.fori_loop(0, ts // CHUNK, chunk, 0)

    ones = ones_ref[...]
    for b in range(gb):
        for qd in range(N_QUADS):
            ln = slice(qd * QUAD, (qd + 1) * QUAD)
            y = y_ref[b, :, ln]
            dlt = y - _dot_exact_rhs(y, ones) * (1.0 / HEAD)
            var = _dot_exact_rhs(dlt * dlt, ones) * (1.0 / HEAD)
            yn = dlt * lax.rsqrt(var + GN_EPS) * gng_ref[:, ln] + gnb_ref[:, ln]
            o_ref[b, :, ln] = ((yn + bonus_ref[b, :, ln].astype(F32)) * g_ref[b, :, ln].astype(F32)).astype(o_ref.dtype)


def _scan(prep_out, gn_g, gn_b, nb, seq, ts=128, gb=8):
    n = prep_out[0].shape[0]
    rw = RWKV_WIDTH
    cpt = ts // CHUNK
    ones = jnp.kron(jnp.eye(QUAD // HEAD, dtype=F32), jnp.ones((HEAD, HEAD), F32)).astype(BF16)
    rows = pl.BlockSpec((gb, ts, rw), lambda i, t: (i, t, 0))
    vec = pl.BlockSpec((1, rw), lambda i, t: (0, 0))
    args = [a.reshape(nb, seq, rw) for a in prep_out[:9]] + [prep_out[9].reshape(nb, seq // ts, cpt, rw)]
    out = pl.pallas_call(
        functools.partial(_scan_kernel, ts=ts, gb=gb),
        grid=(nb // gb, seq // ts),
        in_specs=[rows] * 9 + [pl.BlockSpec((gb, None, cpt, rw), lambda i, t: (i, t, 0, 0)), vec, vec,
                               pl.BlockSpec((QUAD, QUAD), lambda i, t: (0, 0))],
        out_specs=rows,
        out_shape=jax.ShapeDtypeStruct((nb, seq, rw), BF16),
        scratch_shapes=[pltpu.VMEM((gb * N_QUADS, QUAD, QUAD), F32), pltpu.VMEM((gb, ts, rw), F32)],
        compiler_params=_params("parallel", "arbitrary"),
        name="scan",
    )(*args, gn_g, gn_b, ones)
    return out.reshape(n, rw)


def _merge_kernel(uc_ref, ur_ref, zg_ref, x_ref, pc_ref, pr_ref, wo_ref, gpost_ref, gt_ref,
                  gpre_ref, sc_ref, sh_ref, wrh_ref, wrl_ref, xo_ref, h_ref, lg_ref):
    yc = _dot(uc_ref[...], pc_ref[...])
    yr = _dot(ur_ref[...], pr_ref[...])
    zg = zg_ref[...].astype(F32)
    m = _sigmoid(zg[:, :D_MODEL]) * yc + _sigmoid(zg[:, D_MODEL:]) * yr
    y = _dot(m.astype(BF16), wo_ref[...])
    xn = x_ref[...] + gt_ref[...] * _rms(y, gpost_ref[...])
    xo_ref[...] = xn
    h = _rms(xn, gpre_ref[...]) * (1.0 + sc_ref[...]) + sh_ref[...]
    h_ref[...] = _pack_rows(h)
    h_hi, h_lo = _split(h)
    nt = lambda a, b: lax.dot_general(a, b, (((1,), (1,)), ((), ())), preferred_element_type=F32)
    w_hi = wrh_ref[...]
    lg_ref[...] = nt(w_hi, h_hi) + nt(w_hi, h_lo) + nt(wrl_ref[...], h_hi)


def _merge(uc, ur, zg, x2, p_conv_b, p_rwkv_b, w_o_b, g_post, g_pre, mod, layer, w_router_t, seq, tm=512):
    n, d = x2.shape
    row = lambda i: (i, 0)
    full = lambda shape: pl.BlockSpec(shape, lambda i: (0, 0))
    return pl.pallas_call(
        _merge_kernel,
        grid=(n // tm,),
        in_specs=[
            pl.BlockSpec((tm, CONV_WIDTH), row), pl.BlockSpec((tm, RWKV_WIDTH), row),
            pl.BlockSpec((tm, COL_GATE), row), pl.BlockSpec((tm, d), row),
            full((CONV_WIDTH, d)), full((RWKV_WIDTH, d)), full((d, d)),
            full((1, d)), _mod_spec(layer, 2, seq, tm),
            full((1, d)), _mod_spec(layer, 4, seq, tm), _mod_spec(layer, 3, seq, tm),
            full((N_EXPERTS, d)), full((N_EXPERTS, d)),
        ],
        out_specs=[pl.BlockSpec((tm, d), row), pl.BlockSpec((tm, ROW_WORDS), row),
                   pl.BlockSpec((N_EXPERTS, tm), lambda i: (0, i))],
        out_shape=[jax.ShapeDtypeStruct((n, d), F32), jax.ShapeDtypeStruct((n, ROW_WORDS), jnp.int32),
                   jax.ShapeDtypeStruct((N_EXPERTS, n), F32)],
        compiler_params=_params("parallel"),
        name="merge",
    )(uc, ur, zg, x2, p_conv_b, p_rwkv_b, w_o_b, g_post, mod, g_pre, mod, mod, *_split(w_router_t))


def _route_kernel(lg_ref, b_ref, gate_ref, sel_ref, cnt_ref):
    s = _sigmoid(lg_ref[...])
    biased = s + b_ref[...]
    t = s.shape[1]
    member = lax.broadcasted_iota(jnp.int32, (GROUP_SIZE, t), 0)
    grp = []
    for g in range(N_GROUPS):
        bg = biased[g * GROUP_SIZE:(g + 1) * GROUP_SIZE, :]
        m1 = jnp.max(bg, axis=0, keepdims=True)
        first = jnp.min(jnp.where(bg == m1, member, GROUP_SIZE), axis=0, keepdims=True)
        m2 = jnp.max(jnp.where(member == first, -jnp.inf, bg), axis=0, keepdims=True)
        grp.append(m1 + m2)
    masked = []
    for g in range(N_GROUPS):
        rank = jnp.zeros((1, t), jnp.int32)
        for o in range(N_GROUPS):
            if o == g:
                continue
            ahead = (grp[o] > grp[g]) if o > g else (grp[o] >= grp[g])
            rank = rank + jnp.where(ahead, 1, 0)
        keep = rank < TOPK_GROUPS
        masked.append(jnp.where(keep, biased[g * GROUP_SIZE:(g + 1) * GROUP_SIZE, :], -jnp.inf))
    masked = jnp.concatenate(masked, axis=0)
    eidx = lax.broadcasted_iota(jnp.int32, masked.shape, 0)
    mask = jnp.zeros(masked.shape, F32)
    for _ in range(TOP_K):
        top = jnp.max(masked, axis=0, keepdims=True)
        hit = eidx == jnp.min(jnp.where(masked == top, eidx, N_EXPERTS), axis=0, keepdims=True)
        mask = jnp.where(hit, 1.0, mask)
        masked = jnp.where(hit, -jnp.inf, masked)
    sel = s * mask
    gate_ref[...] = sel / jnp.sum(sel, axis=0, keepdims=True) * ROUTED_SCALE
    sel_ref[...] = mask.astype(sel_ref.dtype)

    @pl.when(pl.program_id(0) == 0)
    def _():
        cnt_ref[...] = jnp.zeros_like(cnt_ref)

    cnt_ref[...] += jnp.sum(mask, axis=1, keepdims=True)


def _route(logits_t, b_router, row0, n, tt=512):
    e = logits_t.shape[0]
    first = row0 // tt
    return pl.pallas_call(
        _route_kernel,
        grid=(n // tt,),
        in_specs=[pl.BlockSpec((e, tt), lambda i: (0, i + first)), pl.BlockSpec((e, 1), lambda i: (0, 0))],
        out_specs=[pl.BlockSpec((e, tt), lambda i: (0, i)), pl.BlockSpec((e, tt), lambda i: (0, i)),
                   pl.BlockSpec((e, GATE_LANES), lambda i: (0, 0))],
        out_shape=[jax.ShapeDtypeStruct((e, n), F32), jax.ShapeDtypeStruct((e, n), BF16),
                   jax.ShapeDtypeStruct((e, GATE_LANES), F32)],
        compiler_params=_params("arbitrary"),
        name="route",
    )(logits_t, b_router)


def _dest_kernel(sel_ref, gate_ref, start_ref, triu_ref, below_ref, dest_ref, g8_ref, run_ref):
    @pl.when(pl.program_id(0) == 0)
    def _():
        run_ref[...] = jnp.zeros_like(run_ref)

    sel = sel_ref[...]
    t = sel.shape[1]
    self32 = sel.astype(F32)
    incl = _dot(sel, triu_ref[...])
    pos = start_ref[...] + run_ref[...] + incl - self32
    run_ref[...] += incl[:, t - 1:t]
    slot = _dot(below_ref[...], sel)
    gate = gate_ref[...]
    dst, gts = [], []
    for k in range(TOP_K):
        mine = (self32 > 0.0) & (slot == float(k))
        dst.append(jnp.sum(jnp.where(mine, pos, 0.0), axis=0, keepdims=True))
        gts.append(jnp.sum(jnp.where(mine, gate, 0.0), axis=0, keepdims=True))
    dest_ref[...] = jnp.concatenate(dst, axis=0).astype(jnp.int32)
    g8_ref[...] = jnp.concatenate(gts + [jnp.zeros((GATE_LANES - TOP_K, t), F32)], axis=0).T


def _dest(sel, gate, seg_start, tt=512):
    e, n = sel.shape
    triu = jnp.triu(jnp.ones((tt, tt), F32)).astype(BF16)
    below = jnp.tril(jnp.ones((e, e), F32), -1).astype(BF16)
    return pl.pallas_call(
        _dest_kernel,
        grid=(n // tt,),
        in_specs=[pl.BlockSpec((e, tt), lambda i: (0, i)), pl.BlockSpec((e, tt), lambda i: (0, i)),
                  pl.BlockSpec((e, 1), lambda i: (0, 0)), pl.BlockSpec((tt, tt), lambda i: (0, 0)),
                  pl.BlockSpec((e, e), lambda i: (0, 0))],
        out_specs=[pl.BlockSpec((TOP_K, tt), lambda i: (0, i)), pl.BlockSpec((tt, GATE_LANES), lambda i: (i, 0))],
        out_shape=[jax.ShapeDtypeStruct((TOP_K, n), jnp.int32), jax.ShapeDtypeStruct((n, GATE_LANES), F32)],
        scratch_shapes=[pltpu.VMEM((e, 1), F32)],
        compiler_params=_params("arbitrary"),
        name="dest",
    )(sel, gate, seg_start, triu, below)


def _sc_mesh():
    return plsc.VectorSubcoreMesh(core_axis_name="c", subcore_axis_name="s",
                                  num_cores=SC_CORES, num_subcores=SC_SUBCORES)


def _worker_id():
    return lax.axis_index("s") * SC_CORES + lax.axis_index("c")


def _sc_scatter_rows(rows, row0, dest_flat, n_out):
    n = dest_flat.shape[0] // TOP_K
    w = rows.shape[1]
    per_worker = n // SC_WORKERS

    def body(rows_hbm, dest_hbm, out_hbm, idx_v, rows_v):
        base = _worker_id() * per_worker

        @pl.loop(0, per_worker // SC_CHUNK)
        def _(c):
            t0 = base + c * SC_CHUNK
            pltpu.sync_copy(rows_hbm.at[pl.ds(row0 + t0, SC_CHUNK)], rows_v)
            for k in range(TOP_K):
                pltpu.sync_copy(dest_hbm.at[pl.ds(k * n + t0, SC_CHUNK)], idx_v)
                pltpu.sync_copy(rows_v, out_hbm.at[idx_v])

    return pl.kernel(
        body, out_type=jax.ShapeDtypeStruct((n_out, w), rows.dtype), mesh=_sc_mesh(),
        scratch_types=[pltpu.VMEM((SC_CHUNK,), jnp.int32), pltpu.VMEM((SC_CHUNK, w), rows.dtype)],
        name="sc_scatter",
    )(rows, dest_flat)


def _sc_gather_rows(table, idx_flat):
    m = idx_flat.shape[0]
    w = table.shape[1]
    per_worker = m // SC_WORKERS

    def body(table_hbm, idx_hbm, out_hbm, idx_v, rows_v):
        base = _worker_id() * per_worker

        @pl.loop(0, per_worker // SC_CHUNK)
        def _(c):
            j0 = base + c * SC_CHUNK
            pltpu.sync_copy(idx_hbm.at[pl.ds(j0, SC_CHUNK)], idx_v)
            pltpu.sync_copy(table_hbm.at[idx_v], rows_v)
            pltpu.sync_copy(rows_v, out_hbm.at[pl.ds(j0, SC_CHUNK)])

    return pl.kernel(
        body, out_type=jax.ShapeDtypeStruct((m, w), table.dtype), mesh=_sc_mesh(),
        scratch_types=[pltpu.VMEM((SC_CHUNK,), jnp.int32), pltpu.VMEM((SC_CHUNK, w), table.dtype)],
        name="sc_gather",
    )(table, idx_flat)


def _swiglu(h, wg, wu):
    gte = _dot(h, wg)
    return gte * _sigmoid(gte) * _dot(h, wu)


def _experts_kernel(be_ref, va_ref, xs_ref, *refs):
    nb = BLOCKS_PER_STEP
    wg, wu, wd, ys_ref = refs[:nb], refs[nb:2 * nb], refs[2 * nb:3 * nb], refs[3 * nb]
    first = pl.program_id(0) * nb

    @pl.when(va_ref[first] > 0)
    def _():
        row = lax.broadcasted_iota(jnp.int32, (EXPERT_SUB, D_MODEL), 0)
        subs = [(b, r0) for b in range(nb) for r0 in range(0, EXPERT_BLOCK, EXPERT_SUB)]
        x = []
        for b, r0 in subs:
            lo = b * EXPERT_BLOCK + r0
            xi = _unpack_rows(xs_ref[lo:lo + EXPERT_SUB, :])
            x.append(jnp.where(row < va_ref[first + b] - r0, xi, 0.0).astype(BF16))
        gte = [_dot(xi, wg[b][...]) for xi, (b, _) in zip(x, subs)]
        up = [_dot(xi, wu[b][...]) for xi, (b, _) in zip(x, subs)]
        act = [(g * _sigmoid(g) * u).astype(BF16) for g, u in zip(gte, up)]
        y = [_dot(a, wd[b][...]) for a, (b, _) in zip(act, subs)]
        for yi, (b, r0) in zip(y, subs):
            lo = b * EXPERT_BLOCK + r0
            ys_ref[lo:lo + EXPERT_SUB, :] = _pack_rows(yi)


def _cast_kernel(w_ref, o_ref):
    o_ref[...] = w_ref[...].astype(o_ref.dtype)


def _cast_layer(w, layer, per_step=4):
    _, ne, a, b = w.shape
    return pl.pallas_call(
        _cast_kernel,
        grid=(ne // per_step,),
        in_specs=[pl.BlockSpec((None, per_step, a, b), lambda e: (layer, e, 0, 0))],
        out_specs=pl.BlockSpec((per_step, a, b), lambda e: (e, 0, 0)),
        out_shape=jax.ShapeDtypeStruct((ne, a, b), BF16),
        compiler_params=_params("parallel"),
        name="cast",
    )(w)


def _experts(xs, blk_expert, blk_valid, wg, wu, wd):
    n_rows, w = xs.shape
    d = wg.shape[1]
    nb = BLOCKS_PER_STEP
    rows = pl.BlockSpec((nb * EXPERT_BLOCK, w), lambda s, be, va: (s, 0))
    up_spec = lambda i: pl.BlockSpec((None, d, D_EXPERT), lambda s, be, va: (be[s * nb + i], 0, 0))
    down_spec = lambda i: pl.BlockSpec((None, D_EXPERT, d), lambda s, be, va: (be[s * nb + i], 0, 0))
    return pl.pallas_call(
        _experts_kernel,
        grid_spec=pltpu.PrefetchScalarGridSpec(
            num_scalar_prefetch=2,
            grid=(n_rows // (nb * EXPERT_BLOCK),),
            in_specs=[rows] + [up_spec(i) for i in range(nb)] * 2 + [down_spec(i) for i in range(nb)],
            out_specs=rows,
        ),
        out_shape=jax.ShapeDtypeStruct((n_rows, w), xs.dtype),
        compiler_params=_params("arbitrary"),
        name="experts",
    )(blk_expert, blk_valid, xs, *([wg] * nb), *([wu] * nb), *([wd] * nb))


def _combine_kernel(yg_ref, g8_ref, hp_ref, sg_ref, su_ref, sd_ref, x_ref, gpost_ref, gt_ref, *rest):
    o_ref = rest[-1]
    h = _unpack_rows(hp_ref[...]).astype(BF16)
    acc = _dot(_swiglu(h, sg_ref[...], su_ref[...]).astype(BF16), sd_ref[...])
    g8 = g8_ref[...]
    for k in range(TOP_K):
        acc = acc + g8[:, k:k + 1] * _unpack_rows(yg_ref[k])
    o_ref[...] = x_ref[...] + gt_ref[...] * _rms(acc, gpost_ref[...])


def _combine(yg, g8, hp, sg, su, sd, x2, g_post, mod, layer, seq, row0, earlier, tm=512):
    n, d = x2.shape
    part = g8.shape[0]
    first = row0 // tm
    local = lambda i: (i, 0)
    glob = lambda i: (i + first, 0)
    full = lambda shape: pl.BlockSpec(shape, lambda i: (0, 0))
    in_specs = [
        pl.BlockSpec((TOP_K, tm, ROW_WORDS), lambda i: (0, i, 0)),
        pl.BlockSpec((tm, GATE_LANES), local),
        pl.BlockSpec((tm, ROW_WORDS), glob),
        full((d, D_EXPERT)), full((d, D_EXPERT)), full((D_EXPERT, d)),
        pl.BlockSpec((tm, d), glob),
        full((1, d)),
        _mod_spec(layer, 5, seq, tm, first),
    ]
    args = [yg, g8, hp, sg, su, sd, x2, g_post, mod]
    aliases = {}
    if earlier is not None:
        in_specs.append(pl.BlockSpec(memory_space=pl.ANY))
        args.append(earlier)
        aliases = {len(args) - 1: 0}
    return pl.pallas_call(
        _combine_kernel,
        grid=(part // tm,),
        in_specs=in_specs,
        out_specs=pl.BlockSpec((tm, d), glob),
        out_shape=jax.ShapeDtypeStruct((n, d), F32),
        input_output_aliases=aliases,
        compiler_params=_params("parallel"),
        name="combine",
    )(*args)


def _moe(hp, logits_t, b_router, wg, wu, wd, sg, su, sd, x2, g_post, mod, layer, seq):
    n = x2.shape[0]
    part = n // MOE_PARTS
    out = None
    for p in range(MOE_PARTS):
        row0 = p * part
        gate, sel, counts = _route(logits_t, b_router, row0, part)
        cnt = counts[:, 0].astype(jnp.int32)
        padded = (cnt + EXPERT_BLOCK - 1) // EXPERT_BLOCK * EXPERT_BLOCK
        seg_end = jnp.cumsum(padded)
        seg_start = seg_end - padded
        n_blocks = part * TOP_K // EXPERT_BLOCK + N_EXPERTS
        blk_row = jnp.arange(n_blocks, dtype=jnp.int32) * EXPERT_BLOCK
        blk_expert = jnp.sum((seg_end[None, :] <= blk_row[:, None]).astype(jnp.int32), axis=1)
        blk_expert = jnp.minimum(blk_expert, N_EXPERTS - 1)
        mine = blk_expert[:, None] == jnp.arange(N_EXPERTS, dtype=jnp.int32)[None, :]
        seg_last = jnp.sum(jnp.where(mine, (seg_start + cnt)[None, :], 0), axis=1)
        blk_valid = jnp.clip(seg_last - blk_row, 0, EXPERT_BLOCK)
        dest, g8 = _dest(sel, gate, seg_start.astype(F32).reshape(-1, 1))
        dest_flat = dest.reshape(-1)
        xs = _sc_scatter_rows(hp, row0, dest_flat, n_blocks * EXPERT_BLOCK)
        ys = _experts(xs, blk_expert, blk_valid.astype(jnp.int32), wg, wu, wd)
        yg = _sc_gather_rows(ys, dest_flat).reshape(TOP_K, part, ROW_WORDS)
        out = _combine(yg, g8, hp, sg, su, sd, x2, g_post, mod, layer, seq, row0, out)
    return out


def kernel(x, c, w_ada, b_ada, norm_mix_pre, norm_mix_post, norm_ffn_pre, norm_ffn_post, w_in, mu_shift, conv_w, conv_b, conv_ln_g, conv_ln_b, p_conv, w0, w2, a0, a2, g2, k_k, k_a, r_k, gn_g, gn_b, p_rwkv, w_o, w_router, b_router, we_gate, we_up, we_down, ws_gate, ws_up, ws_down):
    nb, seq, d = x.shape
    depth = w_ada.shape[0]
    n = nb * seq
    rw = RWKV_WIDTH
    mod = _ada(c, w_ada, b_ada)
    x2 = x.reshape(n, d)
    row = lambda a: a.reshape(1, -1)
    bf = lambda a: a.astype(BF16)
    for l in range(depth):
        zc, zr, zg = _win(x2, row(norm_mix_pre[l]), mod, l, bf(w_in[l]), seq)
        uc = _conv(zc, conv_w[l], row(conv_b[l]), row(conv_ln_g[l]), row(conv_ln_b[l]), nb, seq)
        wa = jnp.zeros((LORA_W + LORA_A, 2 * rw), F32)
        wa = wa.at[:LORA_W, :rw].set(w2[l]).at[LORA_W:, rw:].set(a2[l])
        prep_out = _prep(zr, row(mu_shift[l]), row(w0[l]), row(a0[l]), wa, g2[l], row(k_k[l]),
                         row(k_a[l]), row(r_k[l]), nb, seq)
        ur = _scan(prep_out, row(gn_g[l]), row(gn_b[l]), nb, seq)
        x2, hp, logits_t = _merge(uc, ur, zg, x2, bf(p_conv[l]), bf(p_rwkv[l]), bf(w_o[l]),
                                  row(norm_mix_post[l]), row(norm_ffn_pre[l]), mod, l, w_router[l].T, seq)
        x2 = _moe(hp, logits_t, b_router[l].reshape(-1, 1), _cast_layer(we_gate, l), _cast_layer(we_up, l),
                  _cast_layer(we_down, l),
                  bf(ws_gate[l]), bf(ws_up[l]), bf(ws_down[l]), x2, row(norm_ffn_post[l]), mod, l, seq)
    return x2.reshape(nb, seq, d)
```

```python
import functools

import jax
import jax.numpy as jnp
from jax import lax
from jax.experimental import pallas as pl
from jax.experimental.pallas import tpu as pltpu
from jax.experimental.pallas import tpu_sc as plsc

F32 = jnp.float32
BF16 = jnp.bfloat16
HI = lax.Precision.HIGHEST

D_MODEL = 1024
CONV_WIDTH = 512
CONV_KERNEL = 31
RWKV_WIDTH = 512
HEAD = 64
N_HEADS = RWKV_WIDTH // HEAD
LORA_W = 64
LORA_A = 64
LORA_G = 128
N_EXPERTS = 64
TOP_K = 8
N_GROUPS = 8
TOPK_GROUPS = 4
GROUP_SIZE = N_EXPERTS // N_GROUPS
D_EXPERT = 256
ROUTED_SCALE = 2.5
RMS_EPS = 1e-6
LN_EPS = 1e-5
GN_EPS = 64e-5
COL_CONV = 2 * CONV_WIDTH
COL_RWKV = 3 * RWKV_WIDTH + LORA_W + LORA_A + LORA_G
COL_GATE = 2 * D_MODEL
D_IN = COL_CONV + COL_RWKV + COL_GATE

CHUNK = 64
QUAD = 4 * HEAD
N_QUADS = RWKV_WIDTH // QUAD
CUM_BLOCK = 256
CONV_HALO = 32
SUBLANES = 8
SHIFT_HALO = SUBLANES
GATE_LANES = 128
EXPERT_BLOCK = 512
EXPERT_SUB = 256
BLOCKS_PER_STEP = 2
ROW_WORDS = D_MODEL // 2
SC_CORES = 2
SC_SUBCORES = 16
SC_WORKERS = SC_CORES * SC_SUBCORES
SC_CHUNK = 128
VMEM_LIMIT = 52 * 1024 * 1024


def _params(*sem):
    return pltpu.CompilerParams(dimension_semantics=sem, vmem_limit_bytes=VMEM_LIMIT)


def _sigmoid(x):
    return 1.0 / (1.0 + jnp.exp(-x))


def _softplus(x):
    return jnp.maximum(x, 0.0) + jnp.log(1.0 + jnp.exp(-jnp.abs(x)))


def _rms(x, g):
    return x * lax.rsqrt(jnp.mean(x * x, axis=-1, keepdims=True) + RMS_EPS) * g


def _split(x):
    hi = x.astype(BF16)
    return hi, (x - hi.astype(F32)).astype(BF16)


def _dot(a, b):
    return jnp.dot(a, b, preferred_element_type=F32)


def _dot_exact_rhs(x, w):
    hi, lo = _split(x)
    return _dot(hi, w) + _dot(lo, w)


def _pack_rows(y):
    half = y.shape[1] // 2
    packed = pltpu.pack_elementwise([y[:, half:], y[:, :half]], packed_dtype=BF16)
    return lax.bitcast_convert_type(packed, jnp.int32)


def _unpack_rows(u):
    hi = pltpu.unpack_elementwise(u, index=1, packed_dtype=BF16, unpacked_dtype=F32)
    lo = pltpu.unpack_elementwise(u, index=0, packed_dtype=BF16, unpacked_dtype=F32)
    return jnp.concatenate([hi, lo], axis=1)


def _dot_exact_lhs(w, x):
    hi, lo = _split(x)
    return _dot(w, hi) + _dot(w, lo)


def _ada_kernel(c_ref, w_ref, b_ref, o_ref):
    c = c_ref[...]
    o_ref[...] = jnp.dot(c * _sigmoid(c), w_ref[...], precision=HI,
                         preferred_element_type=F32) + b_ref[...]


def _ada(c, w_ada, b_ada):
    nl, d, _ = w_ada.shape
    nb = c.shape[0]
    out = pl.pallas_call(
        _ada_kernel,
        grid=(nl, 6),
        in_specs=[
            pl.BlockSpec((nb, d), lambda l, k: (0, 0)),
            pl.BlockSpec((None, d, d), lambda l, k: (l, 0, k)),
            pl.BlockSpec((None, None, 1, d), lambda l, k: (l, k, 0, 0)),
        ],
        out_specs=pl.BlockSpec((None, None, nb, d), lambda l, k: (l, k, 0, 0)),
        out_shape=jax.ShapeDtypeStruct((nl, 6, nb, d), F32),
        compiler_params=_params("parallel", "parallel"),
        name="ada",
    )(c, w_ada, b_ada.reshape(nl, 6, 1, d))
    return out.reshape(nl, 6, nb, 1, d)


def _mod_spec(layer, piece, rows_per_batch, tm):
    return pl.BlockSpec((None, None, None, 1, D_MODEL),
                        lambda i, *_: (layer, piece, (i * tm) // rows_per_batch, 0, 0))


def _win_kernel(x_ref, g_ref, sc_ref, sh_ref, w_ref, zc_ref, zr_ref, zg_ref):
    h = _rms(x_ref[...], g_ref[...]) * (1.0 + sc_ref[...]) + sh_ref[...]
    hb = h.astype(BF16)
    zc_ref[...] = _dot(hb, w_ref[:, :COL_CONV]).astype(zc_ref.dtype)
    zr_ref[...] = _dot(hb, w_ref[:, COL_CONV:COL_CONV + COL_RWKV])
    zg_ref[...] = _dot(hb, w_ref[:, COL_CONV + COL_RWKV:]).astype(zg_ref.dtype)


def _win(x2, g, mod, layer, w_in_b, seq, tm=512):
    n, d = x2.shape
    row = lambda i: (i, 0)
    return pl.pallas_call(
        _win_kernel,
        grid=(n // tm,),
        in_specs=[
            pl.BlockSpec((tm, d), row),
            pl.BlockSpec((1, d), lambda i: (0, 0)),
            _mod_spec(layer, 1, seq, tm),
            _mod_spec(layer, 0, seq, tm),
            pl.BlockSpec((d, D_IN), lambda i: (0, 0)),
        ],
        out_specs=[pl.BlockSpec((tm, COL_CONV), row), pl.BlockSpec((tm, COL_RWKV), row),
                   pl.BlockSpec((tm, COL_GATE), row)],
        out_shape=[jax.ShapeDtypeStruct((n, COL_CONV), BF16), jax.ShapeDtypeStruct((n, COL_RWKV), F32),
                   jax.ShapeDtypeStruct((n, COL_GATE), BF16)],
        compiler_params=_params("parallel"),
        name="win",
    )(x2, g, mod, mod, w_in_b)


def _conv_kernel(z_ref, halo_ref, w_ref, cb_ref, g_ref, b_ref, o_ref, ubuf, *, ts):
    j = pl.program_id(1)
    z = z_ref[...].astype(F32)
    zh = halo_ref[...].astype(F32)
    uh = zh[:, :CONV_WIDTH] * _sigmoid(zh[:, CONV_WIDTH:])
    rows = CONV_HALO + ts
    u = jnp.concatenate([jnp.where(j > 0, uh, 0.0),
                         z[:, :CONV_WIDTH] * _sigmoid(z[:, CONV_WIDTH:])], axis=0)
    ubuf[0] = u
    for s in range(1, SUBLANES):
        ubuf[s] = pltpu.roll(u, rows - s, axis=0)
    sub = 64
    first = CONV_HALO - (CONV_KERNEL - 1)
    for r in range(ts // sub):
        acc = jnp.zeros((sub, CONV_WIDTH), F32) + cb_ref[...]
        for k in range(CONV_KERNEL):
            s = (first + k) % SUBLANES
            base = r * sub + first + k - s
            acc = acc + w_ref[k:k + 1, :] * ubuf[s, base:base + sub, :]
        mu = jnp.mean(acc, axis=-1, keepdims=True)
        dlt = acc - mu
        var = jnp.mean(dlt * dlt, axis=-1, keepdims=True)
        y = dlt * lax.rsqrt(var + LN_EPS) * g_ref[...] + b_ref[...]
        o_ref[r * sub:(r + 1) * sub, :] = (y * _sigmoid(y)).astype(o_ref.dtype)


def _conv(zc, conv_w, conv_b, ln_g, ln_b, nb, seq, ts=512):
    n = zc.shape[0]
    tps = seq // ts
    hpt = ts // CONV_HALO
    vec = pl.BlockSpec((1, CONV_WIDTH), lambda b, j: (0, 0))
    return pl.pallas_call(
        functools.partial(_conv_kernel, ts=ts),
        grid=(nb, tps),
        in_specs=[
            pl.BlockSpec((ts, COL_CONV), lambda b, j: (b * tps + j, 0)),
            pl.BlockSpec((CONV_HALO, COL_CONV), lambda b, j: (jnp.maximum((b * tps + j) * hpt - 1, 0), 0)),
            pl.BlockSpec((CONV_KERNEL, CONV_WIDTH), lambda b, j: (0, 0)),
            vec, vec, vec,
        ],
        out_specs=pl.BlockSpec((ts, CONV_WIDTH), lambda b, j: (b * tps + j, 0)),
        out_shape=jax.ShapeDtypeStruct((n, CONV_WIDTH), BF16),
        scratch_shapes=[pltpu.VMEM((SUBLANES, CONV_HALO + ts, CONV_WIDTH), F32)],
        compiler_params=_params("parallel", "parallel"),
        name="conv",
    )(zc, zc, conv_w, conv_b, ln_g, ln_b)


def _prep_kernel(z_ref, halo_ref, mu_ref, w0_ref, a0_ref, wah_ref, wal_ref, g2_ref, kk_ref, ka_ref, rk_ref,
                 ones_ref, tri_ref, csel_ref,
                 at_ref, bt_ref, kt_ref, rt_ref, v_ref, bts_ref, kts_ref, bonus_ref, g_ref, pc_ref):
    j = pl.program_id(1)
    z = z_ref[...]
    ts = z.shape[0]
    last = jnp.where(j > 0, halo_ref[SHIFT_HALO - 1:SHIFT_HALO, :], 0.0)
    row = lax.broadcasted_iota(jnp.int32, z.shape, 0)
    prev = jnp.where(row == 0, last, pltpu.roll(z, 1, axis=0))
    zs = z + (prev - z) * mu_ref[...]
    rw = RWKV_WIDTH
    r = zs[:, :rw]
    k = zs[:, rw:2 * rw]
    v = zs[:, 2 * rw:3 * rw]
    wad = zs[:, 3 * rw:3 * rw + LORA_W + LORA_A]
    gd = zs[:, 3 * rw + LORA_W + LORA_A:]
    lane = lax.broadcasted_iota(jnp.int32, wad.shape, 1)
    xh, xl = _split(jnp.where(lane < LORA_W, jnp.tanh(wad), wad))
    wah = wah_ref[...]
    lora = _dot(xh, wah) + _dot(xl, wah) + _dot(xh, wal_ref[...])
    w = -_softplus(-(w0_ref[...] + lora[:, :rw])) - 0.5
    lw = -jnp.exp(w)
    a = _sigmoid(a0_ref[...] + lora[:, rw:])
    g_ref[...] = _dot(_sigmoid(gd).astype(BF16), g2_ref[...]).astype(g_ref.dtype)
    kk = k * kk_ref[...]
    k2 = k * (1.0 + (a - 1.0) * ka_ref[...])
    ones = ones_ref[...]
    kk = kk * lax.rsqrt(jnp.maximum(_dot_exact_rhs(kk * kk, ones), 1e-24))
    bonus_ref[...] = (_dot_exact_rhs(r * k2 * rk_ref[...], ones) * v).astype(bonus_ref.dtype)
    tri = tri_ref[...]
    lc, tot = [], []
    for h in range(ts // CUM_BLOCK):
        cs = _dot_exact_lhs(tri, lw[h * CUM_BLOCK:(h + 1) * CUM_BLOCK])
        lc.append(cs[:CUM_BLOCK])
        tot.append(cs[CUM_BLOCK:])
    lc = jnp.concatenate(lc, axis=0)
    tot = jnp.concatenate(tot, axis=0)
    e_neg = jnp.exp(-lc)
    e_end = jnp.exp(tot - lc)
    kka = kk * a
    dt = at_ref.dtype
    at_ref[...] = (-kk * jnp.exp(lc - lw)).astype(dt)
    bt_ref[...] = (kka * e_neg).astype(dt)
    kt_ref[...] = (k2 * e_neg).astype(dt)
    rt_ref[...] = (r * jnp.exp(lc)).astype(dt)
    v_ref[...] = v.astype(dt)
    bts_ref[...] = (kka * e_end).astype(dt)
    kts_ref[...] = (k2 * e_end).astype(dt)
    pc_ref[...] = jnp.exp(_dot_exact_lhs(csel_ref[...], lw))


def _prep(zr, mu, w0, a0, wa, g2, k_k, k_a, r_k, nb, seq, ts=512):
    n = zr.shape[0]
    tps = seq // ts
    hpt = ts // SHIFT_HALO
    rw = RWKV_WIDTH
    cpt = ts // CHUNK
    ones = jnp.kron(jnp.eye(N_HEADS, dtype=F32), jnp.ones((HEAD, HEAD), F32)).astype(BF16)
    blk = jnp.kron(jnp.eye(CUM_BLOCK // CHUNK, dtype=F32), jnp.ones((CHUNK, CHUNK), F32))
    tri = jnp.concatenate([jnp.tril(blk), blk], axis=0).astype(BF16)
    csel = jnp.kron(jnp.eye(cpt, dtype=F32), jnp.ones((1, CHUNK), F32)).astype(BF16)
    wa_hi, wa_lo = _split(wa)
    vec = pl.BlockSpec((1, rw), lambda b, j: (0, 0))
    full = lambda shape: pl.BlockSpec(shape, lambda b, j: (0, 0))
    rows = pl.BlockSpec((ts, rw), lambda b, j: (b * tps + j, 0))
    return pl.pallas_call(
        _prep_kernel,
        grid=(nb, tps),
        in_specs=[
            pl.BlockSpec((ts, COL_RWKV), lambda b, j: (b * tps + j, 0)),
            pl.BlockSpec((SHIFT_HALO, COL_RWKV), lambda b, j: (jnp.maximum((b * tps + j) * hpt - 1, 0), 0)),
            full((1, COL_RWKV)), vec, vec,
            full((LORA_W + LORA_A, 2 * rw)), full((LORA_W + LORA_A, 2 * rw)), full((LORA_G, rw)),
            vec, vec, vec, full((rw, rw)), full((2 * CUM_BLOCK, CUM_BLOCK)), full((cpt, ts)),
        ],
        out_specs=[rows] * 9 + [pl.BlockSpec((cpt, rw), lambda b, j: (b * tps + j, 0))],
        out_shape=[jax.ShapeDtypeStruct((n, rw), BF16)] * 9 + [jax.ShapeDtypeStruct((n // CHUNK, rw), F32)],
        compiler_params=_params("parallel", "parallel"),
        name="prep",
    )(zr, zr, mu, w0, a0, wa_hi, wa_lo, g2.astype(BF16), k_k, k_a, r_k, ones, tri, csel)


def _mm(a, b):
    return jnp.dot(a.astype(BF16), b.astype(BF16), preferred_element_type=F32)


def _mm_nt(a, b):
    return lax.dot_general(a.astype(BF16), b.astype(BF16), (((1,), (1,)), ((), ())),
                           preferred_element_type=F32)


def _mm_tn(a, b):
    return lax.dot_general(a.astype(BF16), b.astype(BF16), (((0,), (0,)), ((), ())),
                           preferred_element_type=F32)


def _scan_kernel(at_ref, bt_ref, kt_ref, rt_ref, v_ref, bts_ref, kts_ref, bonus_ref, g_ref, pc_ref,
                 gng_ref, gnb_ref, ones_ref, o_ref, s_ref, y_ref, *, ts, gb):
    @pl.when(pl.program_id(1) == 0)
    def _():
        s_ref[...] = jnp.zeros_like(s_ref)

    row_q = lax.broadcasted_iota(jnp.int32, (QUAD, QUAD), 0)
    col_q = lax.broadcasted_iota(jnp.int32, (QUAD, QUAD), 1)
    bdmask = (row_q // HEAD) == (col_q // HEAD)
    row_c = lax.broadcasted_iota(jnp.int32, (CHUNK, QUAD), 0)
    col_c = lax.broadcasted_iota(jnp.int32, (CHUNK, QUAD), 1) % CHUNK
    strict = col_c < row_c
    incl = col_c <= row_c
    eye = jnp.where(col_c == row_c, 1.0, 0.0).astype(F32)

    def bd(x):
        xb = x.astype(BF16)
        return jnp.where(bdmask, jnp.concatenate([xb] * (QUAD // CHUNK), axis=0), jnp.zeros((), BF16))

    chains = [(b, qd) for b in range(gb) for qd in range(N_QUADS)]

    def chunk(c, carry):
        sl = pl.ds(pl.multiple_of(c * CHUNK, CHUNK), CHUNK)
        lanes = [slice(qd * QUAD, (qd + 1) * QUAD) for _, qd in chains]
        ld = lambda ref: [ref[b, sl, ln] for (b, _), ln in zip(chains, lanes)]
        at, bt, kt, rt, v, bts, kts = (ld(r) for r in (at_ref, bt_ref, kt_ref, rt_ref, v_ref, bts_ref, kts_ref))
        each = range(len(chains))
        ar = [jnp.concatenate([at[i], rt[i]], axis=0) for i in each]
        pb = [_mm_nt(ar[i], bd(bt[i])) for i in each]
        pk = [_mm_nt(ar[i], bd(kt[i])) for i in each]
        l_ab = [jnp.where(strict, p[:CHUNK], 0.0) for p in pb]
        a_rb = [jnp.where(incl, p[CHUNK:], 0.0) for p in pb]
        l_ak = [jnp.where(strict, p[:CHUNK], 0.0) for p in pk]
        a_rk = [jnp.where(incl, p[CHUNK:], 0.0) for p in pk]
        tm = [eye + l for l in l_ab]
        lp = [_mm(l, bd(l)) for l in l_ab]
        for _ in range(4):
            both = [_mm(jnp.concatenate([tm[i], lp[i]], axis=0), bd(lp[i])) for i in each]
            tm = [tm[i] + both[i][:CHUNK] for i in each]
            lp = [both[i][CHUNK:] for i in each]
        tm = [tm[i] + _mm(tm[i], bd(lp[i])) for i in each]
        bdv = [bd(x) for x in v]
        x0 = [_mm(l_ak[i], bdv[i]) for i in each]
        tu = [_mm(tm[i], jnp.concatenate([bd(x0[i]), bd(at[i])], axis=1)) for i in each]
        u0 = [t[:, :QUAD] for t in tu]
        w = [t[:, QUAD:] for t in tu]
        qy = [_mm(a_rb[i], jnp.concatenate([bd(w[i]), bd(u0[i])], axis=1)) for i in each]
        y0 = [qy[i][:, QUAD:] + _mm(a_rk[i], bdv[i]) for i in each]
        q = [rt[i].astype(F32) + qy[i][:, :QUAD] for i in each]
        s = [s_ref[i] for i in each]
        qw_s = [_mm_nt(jnp.concatenate([q[i], w[i]], axis=0), s[i]) for i in each]
        for i, ((b, _), ln) in enumerate(zip(chains, lanes)):
            y_ref[b, sl, ln] = y0[i] + qw_s[i][:CHUNK]
        u = [(u0[i] + qw_s[i][CHUNK:]).astype(BF16) for i in each]
        n_p = [jnp.where(bdmask, _mm_tn(jnp.concatenate([u[i], v[i]], axis=0),
                                        jnp.concatenate([bts[i], kts[i]], axis=0)), 0.0) for i in each]
        for i, ((b, _), ln) in enumerate(zip(chains, lanes)):
            s_ref[i] = s[i] * pc_ref[b, pl.ds(c, 1), ln] + n_p[i]
        return carry

    lax.fori_loop(0, ts // CHUNK, chunk, 0)

    ones = ones_ref[...]
    for b in range(gb):
        for qd in range(N_QUADS):
            ln = slice(qd * QUAD, (qd + 1) * QUAD)
            y = y_ref[b, :, ln]
            dlt = y - _dot_exact_rhs(y, ones) * (1.0 / HEAD)
            var = _dot_exact_rhs(dlt * dlt, ones) * (1.0 / HEAD)
            yn = dlt * lax.rsqrt(var + GN_EPS) * gng_ref[:, ln] + gnb_ref[:, ln]
            o_ref[b, :, ln] = ((yn + bonus_ref[b, :, ln].astype(F32)) * g_ref[b, :, ln].astype(F32)).astype(o_ref.dtype)


def _scan(prep_out, gn_g, gn_b, nb, seq, ts=128, gb=8):
    n = prep_out[0].shape[0]
    rw = RWKV_WIDTH
    cpt = ts // CHUNK
    ones = jnp.kron(jnp.eye(QUAD // HEAD, dtype=F32), jnp.ones((HEAD, HEAD), F32)).astype(BF16)
    rows = pl.BlockSpec((gb, ts, rw), lambda i, t: (i, t, 0))
    vec = pl.BlockSpec((1, rw), lambda i, t: (0, 0))
    args = [a.reshape(nb, seq, rw) for a in prep_out[:9]] + [prep_out[9].reshape(nb, seq // ts, cpt, rw)]
    out = pl.pallas_call(
        functools.partial(_scan_kernel, ts=ts, gb=gb),
        grid=(nb // gb, seq // ts),
        in_specs=[rows] * 9 + [pl.BlockSpec((gb, None, cpt, rw), lambda i, t: (i, t, 0, 0)), vec, vec,
                               pl.BlockSpec((QUAD, QUAD), lambda i, t: (0, 0))],
        out_specs=rows,
        out_shape=jax.ShapeDtypeStruct((nb, seq, rw), BF16),
        scratch_shapes=[pltpu.VMEM((gb * N_QUADS, QUAD, QUAD), F32), pltpu.VMEM((gb, ts, rw), F32)],
        compiler_params=_params("parallel", "arbitrary"),
        name="scan",
    )(*args, gn_g, gn_b, ones)
    return out.reshape(n, rw)


def _merge_kernel(uc_ref, ur_ref, zg_ref, x_ref, pc_ref, pr_ref, wo_ref, gpost_ref, gt_ref,
                  gpre_ref, sc_ref, sh_ref, wrh_ref, wrl_ref, xo_ref, h_ref, lg_ref):
    yc = _dot(uc_ref[...], pc_ref[...])
    yr = _dot(ur_ref[...], pr_ref[...])
    zg = zg_ref[...].astype(F32)
    m = _sigmoid(zg[:, :D_MODEL]) * yc + _sigmoid(zg[:, D_MODEL:]) * yr
    y = _dot(m.astype(BF16), wo_ref[...])
    xn = x_ref[...] + gt_ref[...] * _rms(y, gpost_ref[...])
    xo_ref[...] = xn
    h = _rms(xn, gpre_ref[...]) * (1.0 + sc_ref[...]) + sh_ref[...]
    h_ref[...] = _pack_rows(h)
    h_hi, h_lo = _split(h)
    nt = lambda a, b: lax.dot_general(a, b, (((1,), (1,)), ((), ())), preferred_element_type=F32)
    w_hi = wrh_ref[...]
    lg_ref[...] = nt(w_hi, h_hi) + nt(w_hi, h_lo) + nt(wrl_ref[...], h_hi)


def _merge(uc, ur, zg, x2, p_conv_b, p_rwkv_b, w_o_b, g_post, g_pre, mod, layer, w_router_t, seq, tm=512):
    n, d = x2.shape
    row = lambda i: (i, 0)
    full = lambda shape: pl.BlockSpec(shape, lambda i: (0, 0))
    return pl.pallas_call(
        _merge_kernel,
        grid=(n // tm,),
        in_specs=[
            pl.BlockSpec((tm, CONV_WIDTH), row), pl.BlockSpec((tm, RWKV_WIDTH), row),
            pl.BlockSpec((tm, COL_GATE), row), pl.BlockSpec((tm, d), row),
            full((CONV_WIDTH, d)), full((RWKV_WIDTH, d)), full((d, d)),
            full((1, d)), _mod_spec(layer, 2, seq, tm),
            full((1, d)), _mod_spec(layer, 4, seq, tm), _mod_spec(layer, 3, seq, tm),
            full((N_EXPERTS, d)), full((N_EXPERTS, d)),
        ],
        out_specs=[pl.BlockSpec((tm, d), row), pl.BlockSpec((tm, ROW_WORDS), row),
                   pl.BlockSpec((N_EXPERTS, tm), lambda i: (0, i))],
        out_shape=[jax.ShapeDtypeStruct((n, d), F32), jax.ShapeDtypeStruct((n, ROW_WORDS), jnp.int32),
                   jax.ShapeDtypeStruct((N_EXPERTS, n), F32)],
        compiler_params=_params("parallel"),
        name="merge",
    )(uc, ur, zg, x2, p_conv_b, p_rwkv_b, w_o_b, g_post, mod, g_pre, mod, mod, *_split(w_router_t))


def _route_kernel(lg_ref, b_ref, gate_ref, sel_ref, cnt_ref):
    s = _sigmoid(lg_ref[...])
    biased = s + b_ref[...]
    t = s.shape[1]
    member = lax.broadcasted_iota(jnp.int32, (GROUP_SIZE, t), 0)
    grp = []
    for g in range(N_GROUPS):
        bg = biased[g * GROUP_SIZE:(g + 1) * GROUP_SIZE, :]
        m1 = jnp.max(bg, axis=0, keepdims=True)
        first = jnp.min(jnp.where(bg == m1, member, GROUP_SIZE), axis=0, keepdims=True)
        m2 = jnp.max(jnp.where(member == first, -jnp.inf, bg), axis=0, keepdims=True)
        grp.append(m1 + m2)
    masked = []
    for g in range(N_GROUPS):
        rank = jnp.zeros((1, t), jnp.int32)
        for o in range(N_GROUPS):
            if o == g:
                continue
            ahead = (grp[o] > grp[g]) if o > g else (grp[o] >= grp[g])
            rank = rank + jnp.where(ahead, 1, 0)
        keep = rank < TOPK_GROUPS
        masked.append(jnp.where(keep, biased[g * GROUP_SIZE:(g + 1) * GROUP_SIZE, :], -jnp.inf))
    masked = jnp.concatenate(masked, axis=0)
    eidx = lax.broadcasted_iota(jnp.int32, masked.shape, 0)
    mask = jnp.zeros(masked.shape, F32)
    for _ in range(TOP_K):
        top = jnp.max(masked, axis=0, keepdims=True)
        hit = eidx == jnp.min(jnp.where(masked == top, eidx, N_EXPERTS), axis=0, keepdims=True)
        mask = jnp.where(hit, 1.0, mask)
        masked = jnp.where(hit, -jnp.inf, masked)
    sel = s * mask
    gate_ref[...] = sel / jnp.sum(sel, axis=0, keepdims=True) * ROUTED_SCALE
    sel_ref[...] = mask.astype(sel_ref.dtype)

    @pl.when(pl.program_id(0) == 0)
    def _():
        cnt_ref[...] = jnp.zeros_like(cnt_ref)

    cnt_ref[...] += jnp.sum(mask, axis=1, keepdims=True)


def _route(logits_t, b_router, tt=512):
    e, n = logits_t.shape
    return pl.pallas_call(
        _route_kernel,
        grid=(n // tt,),
        in_specs=[pl.BlockSpec((e, tt), lambda i: (0, i)), pl.BlockSpec((e, 1), lambda i: (0, 0))],
        out_specs=[pl.BlockSpec((e, tt), lambda i: (0, i)), pl.BlockSpec((e, tt), lambda i: (0, i)),
                   pl.BlockSpec((e, GATE_LANES), lambda i: (0, 0))],
        out_shape=[jax.ShapeDtypeStruct((e, n), F32), jax.ShapeDtypeStruct((e, n), BF16),
                   jax.ShapeDtypeStruct((e, GATE_LANES), F32)],
        compiler_params=_params("arbitrary"),
        name="route",
    )(logits_t, b_router)


def _dest_kernel(sel_ref, gate_ref, start_ref, triu_ref, below_ref, dest_ref, g8_ref, run_ref):
    @pl.when(pl.program_id(0) == 0)
    def _():
        run_ref[...] = jnp.zeros_like(run_ref)

    sel = sel_ref[...]
    t = sel.shape[1]
    self32 = sel.astype(F32)
    incl = _dot(sel, triu_ref[...])
    pos = start_ref[...] + run_ref[...] + incl - self32
    run_ref[...] += incl[:, t - 1:t]
    slot = _dot(below_ref[...], sel)
    gate = gate_ref[...]
    dst, gts = [], []
    for k in range(TOP_K):
        mine = (self32 > 0.0) & (slot == float(k))
        dst.append(jnp.sum(jnp.where(mine, pos, 0.0), axis=0, keepdims=True))
        gts.append(jnp.sum(jnp.where(mine, gate, 0.0), axis=0, keepdims=True))
    dest_ref[...] = jnp.concatenate(dst, axis=0).astype(jnp.int32)
    g8_ref[...] = jnp.concatenate(gts + [jnp.zeros((GATE_LANES - TOP_K, t), F32)], axis=0).T


def _dest(sel, gate, seg_start, tt=512):
    e, n = sel.shape
    triu = jnp.triu(jnp.ones((tt, tt), F32)).astype(BF16)
    below = jnp.tril(jnp.ones((e, e), F32), -1).astype(BF16)
    return pl.pallas_call(
        _dest_kernel,
        grid=(n // tt,),
        in_specs=[pl.BlockSpec((e, tt), lambda i: (0, i)), pl.BlockSpec((e, tt), lambda i: (0, i)),
                  pl.BlockSpec((e, 1), lambda i: (0, 0)), pl.BlockSpec((tt, tt), lambda i: (0, 0)),
                  pl.BlockSpec((e, e), lambda i: (0, 0))],
        out_specs=[pl.BlockSpec((TOP_K, tt), lambda i: (0, i)), pl.BlockSpec((tt, GATE_LANES), lambda i: (i, 0))],
        out_shape=[jax.ShapeDtypeStruct((TOP_K, n), jnp.int32), jax.ShapeDtypeStruct((n, GATE_LANES), F32)],
        scratch_shapes=[pltpu.VMEM((e, 1), F32)],
        compiler_params=_params("arbitrary"),
        name="dest",
    )(sel, gate, seg_start, triu, below)


def _sc_mesh():
    return plsc.VectorSubcoreMesh(core_axis_name="c", subcore_axis_name="s",
                                  num_cores=SC_CORES, num_subcores=SC_SUBCORES)


def _worker_id():
    return lax.axis_index("s") * SC_CORES + lax.axis_index("c")


def _sc_scatter_rows(rows, dest_flat, n_out):
    n, w = rows.shape
    per_worker = n // SC_WORKERS

    def body(rows_hbm, dest_hbm, out_hbm, idx_v, rows_v):
        base = _worker_id() * per_worker

        @pl.loop(0, per_worker // SC_CHUNK)
        def _(c):
            t0 = base + c * SC_CHUNK
            pltpu.sync_copy(rows_hbm.at[pl.ds(t0, SC_CHUNK)], rows_v)
            for k in range(TOP_K):
                pltpu.sync_copy(dest_hbm.at[pl.ds(k * n + t0, SC_CHUNK)], idx_v)
                pltpu.sync_copy(rows_v, out_hbm.at[idx_v])

    return pl.kernel(
        body, out_type=jax.ShapeDtypeStruct((n_out, w), rows.dtype), mesh=_sc_mesh(),
        scratch_types=[pltpu.VMEM((SC_CHUNK,), jnp.int32), pltpu.VMEM((SC_CHUNK, w), rows.dtype)],
        name="sc_scatter",
    )(rows, dest_flat)


def _sc_gather_rows(table, idx_flat):
    m = idx_flat.shape[0]
    w = table.shape[1]
    per_worker = m // SC_WORKERS

    def body(table_hbm, idx_hbm, out_hbm, idx_v, rows_v):
        base = _worker_id() * per_worker

        @pl.loop(0, per_worker // SC_CHUNK)
        def _(c):
            j0 = base + c * SC_CHUNK
            pltpu.sync_copy(idx_hbm.at[pl.ds(j0, SC_CHUNK)], idx_v)
            pltpu.sync_copy(table_hbm.at[idx_v], rows_v)
            pltpu.sync_copy(rows_v, out_hbm.at[pl.ds(j0, SC_CHUNK)])

    return pl.kernel(
        body, out_type=jax.ShapeDtypeStruct((m, w), table.dtype), mesh=_sc_mesh(),
        scratch_types=[pltpu.VMEM((SC_CHUNK,), jnp.int32), pltpu.VMEM((SC_CHUNK, w), table.dtype)],
        name="sc_gather",
    )(table, idx_flat)


def _swiglu(h, wg, wu):
    gte = _dot(h, wg)
    return gte * _sigmoid(gte) * _dot(h, wu)


def _experts_kernel(be_ref, va_ref, xs_ref, *refs):
    nb = BLOCKS_PER_STEP
    wg, wu, wd, ys_ref = refs[:nb], refs[nb:2 * nb], refs[2 * nb:3 * nb], refs[3 * nb]
    first = pl.program_id(0) * nb

    @pl.when(va_ref[first] > 0)
    def _():
        row = lax.broadcasted_iota(jnp.int32, (EXPERT_SUB, D_MODEL), 0)
        subs = [(b, r0) for b in range(nb) for r0 in range(0, EXPERT_BLOCK, EXPERT_SUB)]
        x = []
        for b, r0 in subs:
            lo = b * EXPERT_BLOCK + r0
            xi = _unpack_rows(xs_ref[lo:lo + EXPERT_SUB, :])
            x.append(jnp.where(row < va_ref[first + b] - r0, xi, 0.0).astype(BF16))
        gte = [_dot(xi, wg[b][...]) for xi, (b, _) in zip(x, subs)]
        up = [_dot(xi, wu[b][...]) for xi, (b, _) in zip(x, subs)]
        act = [(g * _sigmoid(g) * u).astype(BF16) for g, u in zip(gte, up)]
        y = [_dot(a, wd[b][...]) for a, (b, _) in zip(act, subs)]
        for yi, (b, r0) in zip(y, subs):
            lo = b * EXPERT_BLOCK + r0
            ys_ref[lo:lo + EXPERT_SUB, :] = _pack_rows(yi)


def _cast_kernel(w_ref, o_ref):
    o_ref[...] = w_ref[...].astype(o_ref.dtype)


def _cast_layer(w, layer, per_step=4):
    _, ne, a, b = w.shape
    return pl.pallas_call(
        _cast_kernel,
        grid=(ne // per_step,),
        in_specs=[pl.BlockSpec((None, per_step, a, b), lambda e: (layer, e, 0, 0))],
        out_specs=pl.BlockSpec((per_step, a, b), lambda e: (e, 0, 0)),
        out_shape=jax.ShapeDtypeStruct((ne, a, b), BF16),
        compiler_params=_params("parallel"),
        name="cast",
    )(w)


def _experts(xs, blk_expert, blk_valid, wg, wu, wd):
    n_rows, w = xs.shape
    d = wg.shape[1]
    nb = BLOCKS_PER_STEP
    rows = pl.BlockSpec((nb * EXPERT_BLOCK, w), lambda s, be, va: (s, 0))
    up_spec = lambda i: pl.BlockSpec((None, d, D_EXPERT), lambda s, be, va: (be[s * nb + i], 0, 0))
    down_spec = lambda i: pl.BlockSpec((None, D_EXPERT, d), lambda s, be, va: (be[s * nb + i], 0, 0))
    return pl.pallas_call(
        _experts_kernel,
        grid_spec=pltpu.PrefetchScalarGridSpec(
            num_scalar_prefetch=2,
            grid=(n_rows // (nb * EXPERT_BLOCK),),
            in_specs=[rows] + [up_spec(i) for i in range(nb)] * 2 + [down_spec(i) for i in range(nb)],
            out_specs=rows,
        ),
        out_shape=jax.ShapeDtypeStruct((n_rows, w), xs.dtype),
        compiler_params=_params("arbitrary"),
        name="experts",
    )(blk_expert, blk_valid, xs, *([wg] * nb), *([wu] * nb), *([wd] * nb))


def _combine_kernel(yg_ref, g8_ref, hp_ref, sg_ref, su_ref, sd_ref, x_ref, gpost_ref, gt_ref, *rest):
    o_ref = rest[-1]
    h = _unpack_rows(hp_ref[...]).astype(BF16)
    acc = _dot(_swiglu(h, sg_ref[...], su_ref[...]).astype(BF16), sd_ref[...])
    g8 = g8_ref[...]
    for k in range(TOP_K):
        acc = acc + g8[:, k:k + 1] * _unpack_rows(yg_ref[k])
    o_ref[...] = x_ref[...] + gt_ref[...] * _rms(acc, gpost_ref[...])


def _combine(yg, g8, hp, sg, su, sd, x2, g_post, mod, layer, seq, run_before, tm=512):
    n, d = x2.shape
    row = lambda i: (i, 0)
    full = lambda shape: pl.BlockSpec(shape, lambda i: (0, 0))
    in_specs = [
        pl.BlockSpec((TOP_K, tm, ROW_WORDS), lambda i: (0, i, 0)),
        pl.BlockSpec((tm, GATE_LANES), row),
        pl.BlockSpec((tm, ROW_WORDS), row),
        full((d, D_EXPERT)), full((d, D_EXPERT)), full((D_EXPERT, d)),
        pl.BlockSpec((tm, d), row),
        full((1, d)),
        _mod_spec(layer, 5, seq, tm),
    ] + [pl.BlockSpec(memory_space=pl.ANY)] * len(run_before)
    return pl.pallas_call(
        _combine_kernel,
        grid=(n // tm,),
        in_specs=in_specs,
        out_specs=pl.BlockSpec((tm, d), row),
        out_shape=jax.ShapeDtypeStruct((n, d), F32),
        compiler_params=_params("parallel"),
        name="combine",
    )(yg, g8, hp, sg, su, sd, x2, g_post, mod, *run_before)


def _moe(hp, logits_t, b_router, wg, wu, wd, sg, su, sd, x2, g_post, mod, layer, seq, run_before):
    n = x2.shape[0]
    gate, sel, counts = _route(logits_t, b_router)
    cnt = counts[:, 0].astype(jnp.int32)
    padded = (cnt + EXPERT_BLOCK - 1) // EXPERT_BLOCK * EXPERT_BLOCK
    seg_end = jnp.cumsum(padded)
    seg_start = seg_end - padded
    n_blocks = n * TOP_K // EXPERT_BLOCK + N_EXPERTS
    blk_row = jnp.arange(n_blocks, dtype=jnp.int32) * EXPERT_BLOCK
    blk_expert = jnp.sum((seg_end[None, :] <= blk_row[:, None]).astype(jnp.int32), axis=1)
    blk_expert = jnp.minimum(blk_expert, N_EXPERTS - 1)
    mine = blk_expert[:, None] == jnp.arange(N_EXPERTS, dtype=jnp.int32)[None, :]
    seg_last = jnp.sum(jnp.where(mine, (seg_start + cnt)[None, :], 0), axis=1)
    blk_valid = jnp.clip(seg_last - blk_row, 0, EXPERT_BLOCK).astype(jnp.int32)
    dest, g8 = _dest(sel, gate, seg_start.astype(F32).reshape(-1, 1))
    dest_flat = dest.reshape(-1)
    xs = _sc_scatter_rows(hp, dest_flat, n_blocks * EXPERT_BLOCK)
    ys = _experts(xs, blk_expert, blk_valid, wg, wu, wd)
    yg = _sc_gather_rows(ys, dest_flat).reshape(TOP_K, n, ROW_WORDS)
    return _combine(yg, g8, hp, sg, su, sd, x2, g_post, mod, layer, seq, run_before)


def kernel(x, c, w_ada, b_ada, norm_mix_pre, norm_mix_post, norm_ffn_pre, norm_ffn_post, w_in, mu_shift, conv_w, conv_b, conv_ln_g, conv_ln_b, p_conv, w0, w2, a0, a2, g2, k_k, k_a, r_k, gn_g, gn_b, p_rwkv, w_o, w_router, b_router, we_gate, we_up, we_down, ws_gate, ws_up, ws_down):
    nb, seq, d = x.shape
    depth = w_ada.shape[0]
    n = nb * seq
    rw = RWKV_WIDTH
    mod = _ada(c, w_ada, b_ada)
    x2 = x.reshape(n, d)
    row = lambda a: a.reshape(1, -1)
    bf = lambda a: a.astype(BF16)
    expert_w = [tuple(_cast_layer(w, l) for w in (we_gate, we_up, we_down)) for l in range(depth)]
    for l in range(depth):
        zc, zr, zg = _win(x2, row(norm_mix_pre[l]), mod, l, bf(w_in[l]), seq)
        uc = _conv(zc, conv_w[l], row(conv_b[l]), row(conv_ln_g[l]), row(conv_ln_b[l]), nb, seq)
        wa = jnp.zeros((LORA_W + LORA_A, 2 * rw), F32)
        wa = wa.at[:LORA_W, :rw].set(w2[l]).at[LORA_W:, rw:].set(a2[l])
        prep_out = _prep(zr, row(mu_shift[l]), row(w0[l]), row(a0[l]), wa, g2[l], row(k_k[l]),
                         row(k_a[l]), row(r_k[l]), nb, seq)
        ur = _scan(prep_out, row(gn_g[l]), row(gn_b[l]), nb, seq)
        x2, hp, logits_t = _merge(uc, ur, zg, x2, bf(p_conv[l]), bf(p_rwkv[l]), bf(w_o[l]),
                                  row(norm_mix_post[l]), row(norm_ffn_pre[l]), mod, l, w_router[l].T, seq)
        next_w = expert_w[l + 1] if l + 1 < depth else ()
        x2 = _moe(hp, logits_t, b_router[l].reshape(-1, 1), *expert_w[l], bf(ws_gate[l]), bf(ws_up[l]),
                  bf(ws_down[l]), x2, row(norm_ffn_post[l]), mod, l, seq, next_w)
    return x2.reshape(nb, seq, d)
```

```python
import functools

import jax
import jax.numpy as jnp
from jax import lax
from jax.experimental import pallas as pl
from jax.experimental.pallas import tpu as pltpu
from jax.experimental.pallas import tpu_sc as plsc

F32 = jnp.float32
BF16 = jnp.bfloat16
HI = lax.Precision.HIGHEST

D_MODEL = 1024
CONV_WIDTH = 512
CONV_KERNEL = 31
RWKV_WIDTH = 512
HEAD = 64
N_HEADS = RWKV_WIDTH // HEAD
LORA_W = 64
LORA_A = 64
LORA_G = 128
N_EXPERTS = 64
TOP_K = 8
N_GROUPS = 8
TOPK_GROUPS = 4
GROUP_SIZE = N_EXPERTS // N_GROUPS
D_EXPERT = 256
ROUTED_SCALE = 2.5
RMS_EPS = 1e-6
LN_EPS = 1e-5
GN_EPS = 64e-5
COL_CONV = 2 * CONV_WIDTH
COL_RWKV = 3 * RWKV_WIDTH + LORA_W + LORA_A + LORA_G
COL_GATE = 2 * D_MODEL
D_IN = COL_CONV + COL_RWKV + COL_GATE

CHUNK = 64
QUAD = 4 * HEAD
N_QUADS = RWKV_WIDTH // QUAD
CUM_BLOCK = 256
CONV_HALO = 32
SUBLANES = 8
SHIFT_HALO = SUBLANES
GATE_LANES = 128
EXPERT_BLOCK = 512
EXPERT_SUB = 256
BLOCKS_PER_STEP = 2
ROW_WORDS = D_MODEL // 2
SC_CORES = 2
SC_SUBCORES = 16
SC_WORKERS = SC_CORES * SC_SUBCORES
SC_CHUNK = 128
VMEM_LIMIT = 52 * 1024 * 1024


def _params(*sem):
    return pltpu.CompilerParams(dimension_semantics=sem, vmem_limit_bytes=VMEM_LIMIT)


def _sigmoid(x):
    return 1.0 / (1.0 + jnp.exp(-x))


def _softplus(x):
    return jnp.maximum(x, 0.0) + jnp.log(1.0 + jnp.exp(-jnp.abs(x)))


def _rms(x, g):
    return x * lax.rsqrt(jnp.mean(x * x, axis=-1, keepdims=True) + RMS_EPS) * g


def _split(x):
    hi = x.astype(BF16)
    return hi, (x - hi.astype(F32)).astype(BF16)


def _dot(a, b):
    return jnp.dot(a, b, preferred_element_type=F32)


def _dot_exact_rhs(x, w):
    hi, lo = _split(x)
    return _dot(hi, w) + _dot(lo, w)


def _pack_rows(y):
    half = y.shape[1] // 2
    packed = pltpu.pack_elementwise([y[:, half:], y[:, :half]], packed_dtype=BF16)
    return lax.bitcast_convert_type(packed, jnp.int32)


def _unpack_rows(u):
    hi = pltpu.unpack_elementwise(u, index=1, packed_dtype=BF16, unpacked_dtype=F32)
    lo = pltpu.unpack_elementwise(u, index=0, packed_dtype=BF16, unpacked_dtype=F32)
    return jnp.concatenate([hi, lo], axis=1)


def _dot_exact_lhs(w, x):
    hi, lo = _split(x)
    return _dot(w, hi) + _dot(w, lo)


def _ada_kernel(c_ref, w_ref, b_ref, o_ref):
    c = c_ref[...]
    o_ref[...] = jnp.dot(c * _sigmoid(c), w_ref[...], precision=HI,
                         preferred_element_type=F32) + b_ref[...]


def _ada(c, w_ada, b_ada):
    nl, d, _ = w_ada.shape
    nb = c.shape[0]
    out = pl.pallas_call(
        _ada_kernel,
        grid=(nl, 6),
        in_specs=[
            pl.BlockSpec((nb, d), lambda l, k: (0, 0)),
            pl.BlockSpec((None, d, d), lambda l, k: (l, 0, k)),
            pl.BlockSpec((None, None, 1, d), lambda l, k: (l, k, 0, 0)),
        ],
        out_specs=pl.BlockSpec((None, None, nb, d), lambda l, k: (l, k, 0, 0)),
        out_shape=jax.ShapeDtypeStruct((nl, 6, nb, d), F32),
        compiler_params=_params("parallel", "parallel"),
        name="ada",
    )(c, w_ada, b_ada.reshape(nl, 6, 1, d))
    return out.reshape(nl, 6, nb, 1, d)


def _mod_spec(layer, piece, rows_per_batch, tm):
    return pl.BlockSpec((None, None, None, 1, D_MODEL),
                        lambda i, *_: (layer, piece, (i * tm) // rows_per_batch, 0, 0))


def _win_kernel(x_ref, g_ref, sc_ref, sh_ref, w_ref, zc_ref, zr_ref, zg_ref):
    h = _rms(x_ref[...], g_ref[...]) * (1.0 + sc_ref[...]) + sh_ref[...]
    hb = h.astype(BF16)
    zc_ref[...] = _dot(hb, w_ref[:, :COL_CONV]).astype(zc_ref.dtype)
    zr_ref[...] = _dot(hb, w_ref[:, COL_CONV:COL_CONV + COL_RWKV])
    zg_ref[...] = _dot(hb, w_ref[:, COL_CONV + COL_RWKV:]).astype(zg_ref.dtype)


def _win(x2, g, mod, layer, w_in_b, seq, tm=512):
    n, d = x2.shape
    row = lambda i: (i, 0)
    return pl.pallas_call(
        _win_kernel,
        grid=(n // tm,),
        in_specs=[
            pl.BlockSpec((tm, d), row),
            pl.BlockSpec((1, d), lambda i: (0, 0)),
            _mod_spec(layer, 1, seq, tm),
            _mod_spec(layer, 0, seq, tm),
            pl.BlockSpec((d, D_IN), lambda i: (0, 0)),
        ],
        out_specs=[pl.BlockSpec((tm, COL_CONV), row), pl.BlockSpec((tm, COL_RWKV), row),
                   pl.BlockSpec((tm, COL_GATE), row)],
        out_shape=[jax.ShapeDtypeStruct((n, COL_CONV), BF16), jax.ShapeDtypeStruct((n, COL_RWKV), F32),
                   jax.ShapeDtypeStruct((n, COL_GATE), BF16)],
        compiler_params=_params("parallel"),
        name="win",
    )(x2, g, mod, mod, w_in_b)


def _conv_kernel(z_ref, halo_ref, w_ref, cb_ref, g_ref, b_ref, o_ref, ubuf, *, ts):
    j = pl.program_id(1)
    z = z_ref[...].astype(F32)
    zh = halo_ref[...].astype(F32)
    uh = zh[:, :CONV_WIDTH] * _sigmoid(zh[:, CONV_WIDTH:])
    rows = CONV_HALO + ts
    u = jnp.concatenate([jnp.where(j > 0, uh, 0.0),
                         z[:, :CONV_WIDTH] * _sigmoid(z[:, CONV_WIDTH:])], axis=0)
    ubuf[0] = u
    for s in range(1, SUBLANES):
        ubuf[s] = pltpu.roll(u, rows - s, axis=0)
    sub = 64
    first = CONV_HALO - (CONV_KERNEL - 1)
    for r in range(ts // sub):
        acc = jnp.zeros((sub, CONV_WIDTH), F32) + cb_ref[...]
        for k in range(CONV_KERNEL):
            s = (first + k) % SUBLANES
            base = r * sub + first + k - s
            acc = acc + w_ref[k:k + 1, :] * ubuf[s, base:base + sub, :]
        mu = jnp.mean(acc, axis=-1, keepdims=True)
        dlt = acc - mu
        var = jnp.mean(dlt * dlt, axis=-1, keepdims=True)
        y = dlt * lax.rsqrt(var + LN_EPS) * g_ref[...] + b_ref[...]
        o_ref[r * sub:(r + 1) * sub, :] = (y * _sigmoid(y)).astype(o_ref.dtype)


def _conv(zc, conv_w, conv_b, ln_g, ln_b, nb, seq, ts=512):
    n = zc.shape[0]
    tps = seq // ts
    hpt = ts // CONV_HALO
    vec = pl.BlockSpec((1, CONV_WIDTH), lambda b, j: (0, 0))
    return pl.pallas_call(
        functools.partial(_conv_kernel, ts=ts),
        grid=(nb, tps),
        in_specs=[
            pl.BlockSpec((ts, COL_CONV), lambda b, j: (b * tps + j, 0)),
            pl.BlockSpec((CONV_HALO, COL_CONV), lambda b, j: (jnp.maximum((b * tps + j) * hpt - 1, 0), 0)),
            pl.BlockSpec((CONV_KERNEL, CONV_WIDTH), lambda b, j: (0, 0)),
            vec, vec, vec,
        ],
        out_specs=pl.BlockSpec((ts, CONV_WIDTH), lambda b, j: (b * tps + j, 0)),
        out_shape=jax.ShapeDtypeStruct((n, CONV_WIDTH), BF16),
        scratch_shapes=[pltpu.VMEM((SUBLANES, CONV_HALO + ts, CONV_WIDTH), F32)],
        compiler_params=_params("parallel", "parallel"),
        name="conv",
    )(zc, zc, conv_w, conv_b, ln_g, ln_b)


def _prep_kernel(z_ref, halo_ref, mu_ref, w0_ref, a0_ref, wah_ref, wal_ref, g2_ref, kk_ref, ka_ref, rk_ref,
                 ones_ref, tri_ref, csel_ref,
                 at_ref, bt_ref, kt_ref, rt_ref, v_ref, bts_ref, kts_ref, bonus_ref, g_ref, pc_ref):
    j = pl.program_id(1)
    z = z_ref[...]
    ts = z.shape[0]
    last = jnp.where(j > 0, halo_ref[SHIFT_HALO - 1:SHIFT_HALO, :], 0.0)
    row = lax.broadcasted_iota(jnp.int32, z.shape, 0)
    prev = jnp.where(row == 0, last, pltpu.roll(z, 1, axis=0))
    zs = z + (prev - z) * mu_ref[...]
    rw = RWKV_WIDTH
    r = zs[:, :rw]
    k = zs[:, rw:2 * rw]
    v = zs[:, 2 * rw:3 * rw]
    wad = zs[:, 3 * rw:3 * rw + LORA_W + LORA_A]
    gd = zs[:, 3 * rw + LORA_W + LORA_A:]
    lane = lax.broadcasted_iota(jnp.int32, wad.shape, 1)
    xh, xl = _split(jnp.where(lane < LORA_W, jnp.tanh(wad), wad))
    wah = wah_ref[...]
    lora = _dot(xh, wah) + _dot(xl, wah) + _dot(xh, wal_ref[...])
    w = -_softplus(-(w0_ref[...] + lora[:, :rw])) - 0.5
    lw = -jnp.exp(w)
    a = _sigmoid(a0_ref[...] + lora[:, rw:])
    g_ref[...] = _dot(_sigmoid(gd).astype(BF16), g2_ref[...]).astype(g_ref.dtype)
    kk = k * kk_ref[...]
    k2 = k * (1.0 + (a - 1.0) * ka_ref[...])
    ones = ones_ref[...]
    kk = kk * lax.rsqrt(jnp.maximum(_dot_exact_rhs(kk * kk, ones), 1e-24))
    bonus_ref[...] = (_dot_exact_rhs(r * k2 * rk_ref[...], ones) * v).astype(bonus_ref.dtype)
    tri = tri_ref[...]
    lc, tot = [], []
    for h in range(ts // CUM_BLOCK):
        cs = _dot_exact_lhs(tri, lw[h * CUM_BLOCK:(h + 1) * CUM_BLOCK])
        lc.append(cs[:CUM_BLOCK])
        tot.append(cs[CUM_BLOCK:])
    lc = jnp.concatenate(lc, axis=0)
    tot = jnp.concatenate(tot, axis=0)
    e_neg = jnp.exp(-lc)
    e_end = jnp.exp(tot - lc)
    kka = kk * a
    dt = at_ref.dtype
    at_ref[...] = (-kk * jnp.exp(lc - lw)).astype(dt)
    bt_ref[...] = (kka * e_neg).astype(dt)
    kt_ref[...] = (k2 * e_neg).astype(dt)
    rt_ref[...] = (r * jnp.exp(lc)).astype(dt)
    v_ref[...] = v.astype(dt)
    bts_ref[...] = (kka * e_end).astype(dt)
    kts_ref[...] = (k2 * e_end).astype(dt)
    pc_ref[...] = jnp.exp(_dot_exact_lhs(csel_ref[...], lw))


def _prep(zr, mu, w0, a0, wa, g2, k_k, k_a, r_k, nb, seq, ts=512):
    n = zr.shape[0]
    tps = seq // ts
    hpt = ts // SHIFT_HALO
    rw = RWKV_WIDTH
    cpt = ts // CHUNK
    ones = jnp.kron(jnp.eye(N_HEADS, dtype=F32), jnp.ones((HEAD, HEAD), F32)).astype(BF16)
    blk = jnp.kron(jnp.eye(CUM_BLOCK // CHUNK, dtype=F32), jnp.ones((CHUNK, CHUNK), F32))
    tri = jnp.concatenate([jnp.tril(blk), blk], axis=0).astype(BF16)
    csel = jnp.kron(jnp.eye(cpt, dtype=F32), jnp.ones((1, CHUNK), F32)).astype(BF16)
    wa_hi, wa_lo = _split(wa)
    vec = pl.BlockSpec((1, rw), lambda b, j: (0, 0))
    full = lambda shape: pl.BlockSpec(shape, lambda b, j: (0, 0))
    rows = pl.BlockSpec((ts, rw), lambda b, j: (b * tps + j, 0))
    return pl.pallas_call(
        _prep_kernel,
        grid=(nb, tps),
        in_specs=[
            pl.BlockSpec((ts, COL_RWKV), lambda b, j: (b * tps + j, 0)),
            pl.BlockSpec((SHIFT_HALO, COL_RWKV), lambda b, j: (jnp.maximum((b * tps + j) * hpt - 1, 0), 0)),
            full((1, COL_RWKV)), vec, vec,
            full((LORA_W + LORA_A, 2 * rw)), full((LORA_W + LORA_A, 2 * rw)), full((LORA_G, rw)),
            vec, vec, vec, full((rw, rw)), full((2 * CUM_BLOCK, CUM_BLOCK)), full((cpt, ts)),
        ],
        out_specs=[rows] * 9 + [pl.BlockSpec((cpt, rw), lambda b, j: (b * tps + j, 0))],
        out_shape=[jax.ShapeDtypeStruct((n, rw), BF16)] * 9 + [jax.ShapeDtypeStruct((n // CHUNK, rw), F32)],
        compiler_params=_params("parallel", "parallel"),
        name="prep",
    )(zr, zr, mu, w0, a0, wa_hi, wa_lo, g2.astype(BF16), k_k, k_a, r_k, ones, tri, csel)


def _mm(a, b):
    return jnp.dot(a.astype(BF16), b.astype(BF16), preferred_element_type=F32)


def _mm_nt(a, b):
    return lax.dot_general(a.astype(BF16), b.astype(BF16), (((1,), (1,)), ((), ())),
                           preferred_element_type=F32)


def _mm_tn(a, b):
    return lax.dot_general(a.astype(BF16), b.astype(BF16), (((0,), (0,)), ((), ())),
                           preferred_element_type=F32)


def _scan_kernel(at_ref, bt_ref, kt_ref, rt_ref, v_ref, bts_ref, kts_ref, bonus_ref, g_ref, pc_ref,
                 gng_ref, gnb_ref, ones_ref, o_ref, s_ref, y_ref, *, ts, gb):
    @pl.when(pl.program_id(1) == 0)
    def _():
        s_ref[...] = jnp.zeros_like(s_ref)

    row_q = lax.broadcasted_iota(jnp.int32, (QUAD, QUAD), 0)
    col_q = lax.broadcasted_iota(jnp.int32, (QUAD, QUAD), 1)
    bdmask = (row_q // HEAD) == (col_q // HEAD)
    row_c = lax.broadcasted_iota(jnp.int32, (CHUNK, QUAD), 0)
    col_c = lax.broadcasted_iota(jnp.int32, (CHUNK, QUAD), 1) % CHUNK
    strict = col_c < row_c
    incl = col_c <= row_c
    eye = jnp.where(col_c == row_c, 1.0, 0.0).astype(F32)

    def bd(x):
        xb = x.astype(BF16)
        return jnp.where(bdmask, jnp.concatenate([xb] * (QUAD // CHUNK), axis=0), jnp.zeros((), BF16))

    chains = [(b, qd) for b in range(gb) for qd in range(N_QUADS)]

    def chunk(c, carry):
        sl = pl.ds(pl.multiple_of(c * CHUNK, CHUNK), CHUNK)
        lanes = [slice(qd * QUAD, (qd + 1) * QUAD) for _, qd in chains]
        ld = lambda ref: [ref[b, sl, ln] for (b, _), ln in zip(chains, lanes)]
        at, bt, kt, rt, v, bts, kts = (ld(r) for r in (at_ref, bt_ref, kt_ref, rt_ref, v_ref, bts_ref, kts_ref))
        each = range(len(chains))
        ar = [jnp.concatenate([at[i], rt[i]], axis=0) for i in each]
        pb = [_mm_nt(ar[i], bd(bt[i])) for i in each]
        pk = [_mm_nt(ar[i], bd(kt[i])) for i in each]
        l_ab = [jnp.where(strict, p[:CHUNK], 0.0) for p in pb]
        a_rb = [jnp.where(incl, p[CHUNK:], 0.0) for p in pb]
        l_ak = [jnp.where(strict, p[:CHUNK], 0.0) for p in pk]
        a_rk = [jnp.where(incl, p[CHUNK:], 0.0) for p in pk]
        tm = [eye + l for l in l_ab]
        lp = [_mm(l, bd(l)) for l in l_ab]
        for _ in range(4):
            both = [_mm(jnp.concatenate([tm[i], lp[i]], axis=0), bd(lp[i])) for i in each]
            tm = [tm[i] + both[i][:CHUNK] for i in each]
            lp = [both[i][CHUNK:] for i in each]
        tm = [tm[i] + _mm(tm[i], bd(lp[i])) for i in each]
        bdv = [bd(x) for x in v]
        x0 = [_mm(l_ak[i], bdv[i]) for i in each]
        tu = [_mm(tm[i], jnp.concatenate([bd(x0[i]), bd(at[i])], axis=1)) for i in each]
        u0 = [t[:, :QUAD] for t in tu]
        w = [t[:, QUAD:] for t in tu]
        qy = [_mm(a_rb[i], jnp.concatenate([bd(w[i]), bd(u0[i])], axis=1)) for i in each]
        y0 = [qy[i][:, QUAD:] + _mm(a_rk[i], bdv[i]) for i in each]
        q = [rt[i].astype(F32) + qy[i][:, :QUAD] for i in each]
        s = [s_ref[i] for i in each]
        qw_s = [_mm_nt(jnp.concatenate([q[i], w[i]], axis=0), s[i]) for i in each]
        for i, ((b, _), ln) in enumerate(zip(chains, lanes)):
            y_ref[b, sl, ln] = y0[i] + qw_s[i][:CHUNK]
        u = [(u0[i] + qw_s[i][CHUNK:]).astype(BF16) for i in each]
        n_p = [jnp.where(bdmask, _mm_tn(jnp.concatenate([u[i], v[i]], axis=0),
                                        jnp.concatenate([bts[i], kts[i]], axis=0)), 0.0) for i in each]
        for i, ((b, _), ln) in enumerate(zip(chains, lanes)):
            s_ref[i] = s[i] * pc_ref[b, pl.ds(c, 1), ln] + n_p[i]
        return carry

    lax.fori_loop(0, ts // CHUNK, chunk, 0)

    ones = ones_ref[...]
    for b in range(gb):
        for qd in range(N_QUADS):
            ln = slice(qd * QUAD, (qd + 1) * QUAD)
            y = y_ref[b, :, ln]
            dlt = y - _dot_exact_rhs(y, ones) * (1.0 / HEAD)
            var = _dot_exact_rhs(dlt * dlt, ones) * (1.0 / HEAD)
            yn = dlt * lax.rsqrt(var + GN_EPS) * gng_ref[:, ln] + gnb_ref[:, ln]
            o_ref[b, :, ln] = ((yn + bonus_ref[b, :, ln].astype(F32)) * g_ref[b, :, ln].astype(F32)).astype(o_ref.dtype)


def _scan(prep_out, gn_g, gn_b, nb, seq, ts=128, gb=8):
    n = prep_out[0].shape[0]
    rw = RWKV_WIDTH
    cpt = ts // CHUNK
    ones = jnp.kron(jnp.eye(QUAD // HEAD, dtype=F32), jnp.ones((HEAD, HEAD), F32)).astype(BF16)
    rows = pl.BlockSpec((gb, ts, rw), lambda i, t: (i, t, 0))
    vec = pl.BlockSpec((1, rw), lambda i, t: (0, 0))
    args = [a.reshape(nb, seq, rw) for a in prep_out[:9]] + [prep_out[9].reshape(nb, seq // ts, cpt, rw)]
    out = pl.pallas_call(
        functools.partial(_scan_kernel, ts=ts, gb=gb),
        grid=(nb // gb, seq // ts),
        in_specs=[rows] * 9 + [pl.BlockSpec((gb, None, cpt, rw), lambda i, t: (i, t, 0, 0)), vec, vec,
                               pl.BlockSpec((QUAD, QUAD), lambda i, t: (0, 0))],
        out_specs=rows,
        out_shape=jax.ShapeDtypeStruct((nb, seq, rw), BF16),
        scratch_shapes=[pltpu.VMEM((gb * N_QUADS, QUAD, QUAD), F32), pltpu.VMEM((gb, ts, rw), F32)],
        compiler_params=_params("parallel", "arbitrary"),
        name="scan",
    )(*args, gn_g, gn_b, ones)
    return out.reshape(n, rw)


def _merge_kernel(uc_ref, ur_ref, zg_ref, x_ref, pc_ref, pr_ref, wo_ref, gpost_ref, gt_ref,
                  gpre_ref, sc_ref, sh_ref, wrh_ref, wrl_ref, xo_ref, h_ref, lg_ref):
    yc = _dot(uc_ref[...], pc_ref[...])
    yr = _dot(ur_ref[...], pr_ref[...])
    zg = zg_ref[...].astype(F32)
    m = _sigmoid(zg[:, :D_MODEL]) * yc + _sigmoid(zg[:, D_MODEL:]) * yr
    y = _dot(m.astype(BF16), wo_ref[...])
    xn = x_ref[...] + gt_ref[...] * _rms(y, gpost_ref[...])
    xo_ref[...] = xn
    h = _rms(xn, gpre_ref[...]) * (1.0 + sc_ref[...]) + sh_ref[...]
    h_ref[...] = _pack_rows(h)
    h_hi, h_lo = _split(h)
    nt = lambda a, b: lax.dot_general(a, b, (((1,), (1,)), ((), ())), preferred_element_type=F32)
    w_hi = wrh_ref[...]
    lg_ref[...] = nt(w_hi, h_hi) + nt(w_hi, h_lo) + nt(wrl_ref[...], h_hi)


def _merge(uc, ur, zg, x2, p_conv_b, p_rwkv_b, w_o_b, g_post, g_pre, mod, layer, w_router_t, seq, tm=512):
    n, d = x2.shape
    row = lambda i: (i, 0)
    full = lambda shape: pl.BlockSpec(shape, lambda i: (0, 0))
    return pl.pallas_call(
        _merge_kernel,
        grid=(n // tm,),
        in_specs=[
            pl.BlockSpec((tm, CONV_WIDTH), row), pl.BlockSpec((tm, RWKV_WIDTH), row),
            pl.BlockSpec((tm, COL_GATE), row), pl.BlockSpec((tm, d), row),
            full((CONV_WIDTH, d)), full((RWKV_WIDTH, d)), full((d, d)),
            full((1, d)), _mod_spec(layer, 2, seq, tm),
            full((1, d)), _mod_spec(layer, 4, seq, tm), _mod_spec(layer, 3, seq, tm),
            full((N_EXPERTS, d)), full((N_EXPERTS, d)),
        ],
        out_specs=[pl.BlockSpec((tm, d), row), pl.BlockSpec((tm, ROW_WORDS), row),
                   pl.BlockSpec((N_EXPERTS, tm), lambda i: (0, i))],
        out_shape=[jax.ShapeDtypeStruct((n, d), F32), jax.ShapeDtypeStruct((n, ROW_WORDS), jnp.int32),
                   jax.ShapeDtypeStruct((N_EXPERTS, n), F32)],
        compiler_params=_params("parallel"),
        name="merge",
    )(uc, ur, zg, x2, p_conv_b, p_rwkv_b, w_o_b, g_post, mod, g_pre, mod, mod, *_split(w_router_t))


def _route_kernel(lg_ref, b_ref, gate_ref, sel_ref, cnt_ref):
    s = _sigmoid(lg_ref[...])
    biased = s + b_ref[...]
    t = s.shape[1]
    member = lax.broadcasted_iota(jnp.int32, (GROUP_SIZE, t), 0)
    grp = []
    for g in range(N_GROUPS):
        bg = biased[g * GROUP_SIZE:(g + 1) * GROUP_SIZE, :]
        m1 = jnp.max(bg, axis=0, keepdims=True)
        first = jnp.min(jnp.where(bg == m1, member, GROUP_SIZE), axis=0, keepdims=True)
        m2 = jnp.max(jnp.where(member == first, -jnp.inf, bg), axis=0, keepdims=True)
        grp.append(m1 + m2)
    masked = []
    for g in range(N_GROUPS):
        rank = jnp.zeros((1, t), jnp.int32)
        for o in range(N_GROUPS):
            if o == g:
                continue
            ahead = (grp[o] > grp[g]) if o > g else (grp[o] >= grp[g])
            rank = rank + jnp.where(ahead, 1, 0)
        keep = rank < TOPK_GROUPS
        masked.append(jnp.where(keep, biased[g * GROUP_SIZE:(g + 1) * GROUP_SIZE, :], -jnp.inf))
    masked = jnp.concatenate(masked, axis=0)
    eidx = lax.broadcasted_iota(jnp.int32, masked.shape, 0)
    mask = jnp.zeros(masked.shape, F32)
    for _ in range(TOP_K):
        top = jnp.max(masked, axis=0, keepdims=True)
        hit = eidx == jnp.min(jnp.where(masked == top, eidx, N_EXPERTS), axis=0, keepdims=True)
        mask = jnp.where(hit, 1.0, mask)
        masked = jnp.where(hit, -jnp.inf, masked)
    sel = s * mask
    gate_ref[...] = sel / jnp.sum(sel, axis=0, keepdims=True) * ROUTED_SCALE
    sel_ref[...] = mask.astype(sel_ref.dtype)

    @pl.when(pl.program_id(0) == 0)
    def _():
        cnt_ref[...] = jnp.zeros_like(cnt_ref)

    cnt_ref[...] += jnp.sum(mask, axis=1, keepdims=True)


def _route(logits_t, b_router, tt=512):
    e, n = logits_t.shape
    return pl.pallas_call(
        _route_kernel,
        grid=(n // tt,),
        in_specs=[pl.BlockSpec((e, tt), lambda i: (0, i)), pl.BlockSpec((e, 1), lambda i: (0, 0))],
        out_specs=[pl.BlockSpec((e, tt), lambda i: (0, i)), pl.BlockSpec((e, tt), lambda i: (0, i)),
                   pl.BlockSpec((e, GATE_LANES), lambda i: (0, 0))],
        out_shape=[jax.ShapeDtypeStruct((e, n), F32), jax.ShapeDtypeStruct((e, n), BF16),
                   jax.ShapeDtypeStruct((e, GATE_LANES), F32)],
        compiler_params=_params("arbitrary"),
        name="route",
    )(logits_t, b_router)


def _dest_kernel(sel_ref, gate_ref, start_ref, triu_ref, below_ref, dest_ref, g8_ref, run_ref):
    @pl.when(pl.program_id(0) == 0)
    def _():
        run_ref[...] = jnp.zeros_like(run_ref)

    sel = sel_ref[...]
    t = sel.shape[1]
    self32 = sel.astype(F32)
    incl = _dot(sel, triu_ref[...])
    pos = start_ref[...] + run_ref[...] + incl - self32
    run_ref[...] += incl[:, t - 1:t]
    slot = _dot(below_ref[...], sel)
    gate = gate_ref[...]
    dst, gts = [], []
    for k in range(TOP_K):
        mine = (self32 > 0.0) & (slot == float(k))
        dst.append(jnp.sum(jnp.where(mine, pos, 0.0), axis=0, keepdims=True))
        gts.append(jnp.sum(jnp.where(mine, gate, 0.0), axis=0, keepdims=True))
    dest_ref[...] = jnp.concatenate(dst, axis=0).astype(jnp.int32)
    g8_ref[...] = jnp.concatenate(gts + [jnp.zeros((GATE_LANES - TOP_K, t), F32)], axis=0).T


def _dest(sel, gate, seg_start, tt=512):
    e, n = sel.shape
    triu = jnp.triu(jnp.ones((tt, tt), F32)).astype(BF16)
    below = jnp.tril(jnp.ones((e, e), F32), -1).astype(BF16)
    return pl.pallas_call(
        _dest_kernel,
        grid=(n // tt,),
        in_specs=[pl.BlockSpec((e, tt), lambda i: (0, i)), pl.BlockSpec((e, tt), lambda i: (0, i)),
                  pl.BlockSpec((e, 1), lambda i: (0, 0)), pl.BlockSpec((tt, tt), lambda i: (0, 0)),
                  pl.BlockSpec((e, e), lambda i: (0, 0))],
        out_specs=[pl.BlockSpec((TOP_K, tt), lambda i: (0, i)), pl.BlockSpec((tt, GATE_LANES), lambda i: (i, 0))],
        out_shape=[jax.ShapeDtypeStruct((TOP_K, n), jnp.int32), jax.ShapeDtypeStruct((n, GATE_LANES), F32)],
        scratch_shapes=[pltpu.VMEM((e, 1), F32)],
        compiler_params=_params("arbitrary"),
        name="dest",
    )(sel, gate, seg_start, triu, below)


def _sc_mesh():
    return plsc.VectorSubcoreMesh(core_axis_name="c", subcore_axis_name="s",
                                  num_cores=SC_CORES, num_subcores=SC_SUBCORES)


def _worker_id():
    return lax.axis_index("s") * SC_CORES + lax.axis_index("c")


def _sc_scatter_rows(rows, dest_flat, n_out):
    n, w = rows.shape
    per_worker = n // SC_WORKERS

    def body(rows_hbm, dest_hbm, out_hbm, idx_v, rows_v):
        base = _worker_id() * per_worker

        @pl.loop(0, per_worker // SC_CHUNK)
        def _(c):
            t0 = base + c * SC_CHUNK
            pltpu.sync_copy(rows_hbm.at[pl.ds(t0, SC_CHUNK)], rows_v)
            for k in range(TOP_K):
                pltpu.sync_copy(dest_hbm.at[pl.ds(k * n + t0, SC_CHUNK)], idx_v)
                pltpu.sync_copy(rows_v, out_hbm.at[idx_v])

    return pl.kernel(
        body, out_type=jax.ShapeDtypeStruct((n_out, w), rows.dtype), mesh=_sc_mesh(),
        scratch_types=[pltpu.VMEM((SC_CHUNK,), jnp.int32), pltpu.VMEM((SC_CHUNK, w), rows.dtype)],
        name="sc_scatter",
    )(rows, dest_flat)


def _sc_gather_rows(table, idx_flat):
    m = idx_flat.shape[0]
    w = table.shape[1]
    per_worker = m // SC_WORKERS

    def body(table_hbm, idx_hbm, out_hbm, idx_v, rows_v):
        base = _worker_id() * per_worker

        @pl.loop(0, per_worker // SC_CHUNK)
        def _(c):
            j0 = base + c * SC_CHUNK
            pltpu.sync_copy(idx_hbm.at[pl.ds(j0, SC_CHUNK)], idx_v)
            pltpu.sync_copy(table_hbm.at[idx_v], rows_v)
            pltpu.sync_copy(rows_v, out_hbm.at[pl.ds(j0, SC_CHUNK)])

    return pl.kernel(
        body, out_type=jax.ShapeDtypeStruct((m, w), table.dtype), mesh=_sc_mesh(),
        scratch_types=[pltpu.VMEM((SC_CHUNK,), jnp.int32), pltpu.VMEM((SC_CHUNK, w), table.dtype)],
        name="sc_gather",
    )(table, idx_flat)


def _swiglu(h, wg, wu):
    gte = _dot(h, wg)
    return gte * _sigmoid(gte) * _dot(h, wu)


def _experts_kernel(be_ref, va_ref, xs_ref, *refs):
    nb = BLOCKS_PER_STEP
    wg32, wu32, wd32, ys_ref = refs[:nb], refs[nb:2 * nb], refs[2 * nb:3 * nb], refs[3 * nb]
    wg, wu, wd = (refs[3 * nb + 1 + i * nb:3 * nb + 1 + (i + 1) * nb] for i in range(3))
    step = pl.program_id(0)
    first = step * nb
    live = va_ref[first] > 0

    for b in range(nb):
        changed = jnp.logical_or(step == 0, be_ref[first + b] != be_ref[jnp.maximum(first + b - nb, 0)])

        @pl.when(jnp.logical_and(live, changed))
        def _(b=b):
            wg[b][...] = wg32[b][...].astype(BF16)
            wu[b][...] = wu32[b][...].astype(BF16)
            wd[b][...] = wd32[b][...].astype(BF16)

    @pl.when(live)
    def _():
        row = lax.broadcasted_iota(jnp.int32, (EXPERT_SUB, D_MODEL), 0)
        subs = [(b, r0) for b in range(nb) for r0 in range(0, EXPERT_BLOCK, EXPERT_SUB)]
        x = []
        for b, r0 in subs:
            lo = b * EXPERT_BLOCK + r0
            xi = _unpack_rows(xs_ref[lo:lo + EXPERT_SUB, :])
            x.append(jnp.where(row < va_ref[first + b] - r0, xi, 0.0).astype(BF16))
        gte = [_dot(xi, wg[b][...]) for xi, (b, _) in zip(x, subs)]
        up = [_dot(xi, wu[b][...]) for xi, (b, _) in zip(x, subs)]
        act = [(g * _sigmoid(g) * u).astype(BF16) for g, u in zip(gte, up)]
        y = [_dot(a, wd[b][...]) for a, (b, _) in zip(act, subs)]
        for yi, (b, r0) in zip(y, subs):
            lo = b * EXPERT_BLOCK + r0
            ys_ref[lo:lo + EXPERT_SUB, :] = _pack_rows(yi)


def _experts(xs, blk_expert, blk_valid, wg, wu, wd, layer):
    n_rows, w = xs.shape
    d = wg.shape[2]
    nb = BLOCKS_PER_STEP
    rows = pl.BlockSpec((nb * EXPERT_BLOCK, w), lambda s, be, va: (s, 0))
    up_spec = lambda i: pl.BlockSpec((None, None, d, D_EXPERT), lambda s, be, va: (layer, be[s * nb + i], 0, 0))
    down_spec = lambda i: pl.BlockSpec((None, None, D_EXPERT, d), lambda s, be, va: (layer, be[s * nb + i], 0, 0))
    return pl.pallas_call(
        _experts_kernel,
        grid_spec=pltpu.PrefetchScalarGridSpec(
            num_scalar_prefetch=2,
            grid=(n_rows // (nb * EXPERT_BLOCK),),
            in_specs=[rows] + [up_spec(i) for i in range(nb)] * 2 + [down_spec(i) for i in range(nb)],
            out_specs=rows,
            scratch_shapes=([pltpu.VMEM((d, D_EXPERT), BF16)] * (2 * nb) + [pltpu.VMEM((D_EXPERT, d), BF16)] * nb),
        ),
        out_shape=jax.ShapeDtypeStruct((n_rows, w), xs.dtype),
        compiler_params=_params("arbitrary"),
        name="experts",
    )(blk_expert, blk_valid, xs, *([wg] * nb), *([wu] * nb), *([wd] * nb))


def _combine_kernel(yg_ref, g8_ref, hp_ref, sg_ref, su_ref, sd_ref, x_ref, gpost_ref, gt_ref, o_ref):
    h = _unpack_rows(hp_ref[...]).astype(BF16)
    acc = _dot(_swiglu(h, sg_ref[...], su_ref[...]).astype(BF16), sd_ref[...])
    g8 = g8_ref[...]
    for k in range(TOP_K):
        acc = acc + g8[:, k:k + 1] * _unpack_rows(yg_ref[k])
    o_ref[...] = x_ref[...] + gt_ref[...] * _rms(acc, gpost_ref[...])


def _combine(yg, g8, hp, sg, su, sd, x2, g_post, mod, layer, seq, tm=512):
    n, d = x2.shape
    row = lambda i: (i, 0)
    full = lambda shape: pl.BlockSpec(shape, lambda i: (0, 0))
    in_specs = [
        pl.BlockSpec((TOP_K, tm, ROW_WORDS), lambda i: (0, i, 0)),
        pl.BlockSpec((tm, GATE_LANES), row),
        pl.BlockSpec((tm, ROW_WORDS), row),
        full((d, D_EXPERT)), full((d, D_EXPERT)), full((D_EXPERT, d)),
        pl.BlockSpec((tm, d), row),
        full((1, d)),
        _mod_spec(layer, 5, seq, tm),
    ]
    return pl.pallas_call(
        _combine_kernel,
        grid=(n // tm,),
        in_specs=in_specs,
        out_specs=pl.BlockSpec((tm, d), row),
        out_shape=jax.ShapeDtypeStruct((n, d), F32),
        compiler_params=_params("parallel"),
        name="combine",
    )(yg, g8, hp, sg, su, sd, x2, g_post, mod)


def _moe(hp, logits_t, b_router, wg, wu, wd, sg, su, sd, x2, g_post, mod, layer, seq):
    n = x2.shape[0]
    gate, sel, counts = _route(logits_t, b_router)
    cnt = counts[:, 0].astype(jnp.int32)
    padded = (cnt + EXPERT_BLOCK - 1) // EXPERT_BLOCK * EXPERT_BLOCK
    seg_end = jnp.cumsum(padded)
    seg_start = seg_end - padded
    n_blocks = n * TOP_K // EXPERT_BLOCK + N_EXPERTS
    blk_row = jnp.arange(n_blocks, dtype=jnp.int32) * EXPERT_BLOCK
    blk_expert = jnp.sum((seg_end[None, :] <= blk_row[:, None]).astype(jnp.int32), axis=1)
    blk_expert = jnp.minimum(blk_expert, N_EXPERTS - 1)
    mine = blk_expert[:, None] == jnp.arange(N_EXPERTS, dtype=jnp.int32)[None, :]
    seg_last = jnp.sum(jnp.where(mine, (seg_start + cnt)[None, :], 0), axis=1)
    blk_valid = jnp.clip(seg_last - blk_row, 0, EXPERT_BLOCK).astype(jnp.int32)
    dest, g8 = _dest(sel, gate, seg_start.astype(F32).reshape(-1, 1))
    dest_flat = dest.reshape(-1)
    xs = _sc_scatter_rows(hp, dest_flat, n_blocks * EXPERT_BLOCK)
    ys = _experts(xs, blk_expert, blk_valid, wg, wu, wd, layer)
    yg = _sc_gather_rows(ys, dest_flat).reshape(TOP_K, n, ROW_WORDS)
    return _combine(yg, g8, hp, sg, su, sd, x2, g_post, mod, layer, seq)


def kernel(x, c, w_ada, b_ada, norm_mix_pre, norm_mix_post, norm_ffn_pre, norm_ffn_post, w_in, mu_shift, conv_w, conv_b, conv_ln_g, conv_ln_b, p_conv, w0, w2, a0, a2, g2, k_k, k_a, r_k, gn_g, gn_b, p_rwkv, w_o, w_router, b_router, we_gate, we_up, we_down, ws_gate, ws_up, ws_down):
    nb, seq, d = x.shape
    depth = w_ada.shape[0]
    n = nb * seq
    rw = RWKV_WIDTH
    mod = _ada(c, w_ada, b_ada)
    x2 = x.reshape(n, d)
    row = lambda a: a.reshape(1, -1)
    bf = lambda a: a.astype(BF16)
    for l in range(depth):
        zc, zr, zg = _win(x2, row(norm_mix_pre[l]), mod, l, bf(w_in[l]), seq)
        uc = _conv(zc, conv_w[l], row(conv_b[l]), row(conv_ln_g[l]), row(conv_ln_b[l]), nb, seq)
        wa = jnp.zeros((LORA_W + LORA_A, 2 * rw), F32)
        wa = wa.at[:LORA_W, :rw].set(w2[l]).at[LORA_W:, rw:].set(a2[l])
        prep_out = _prep(zr, row(mu_shift[l]), row(w0[l]), row(a0[l]), wa, g2[l], row(k_k[l]),
                         row(k_a[l]), row(r_k[l]), nb, seq)
        ur = _scan(prep_out, row(gn_g[l]), row(gn_b[l]), nb, seq)
        x2, hp, logits_t = _merge(uc, ur, zg, x2, bf(p_conv[l]), bf(p_rwkv[l]), bf(w_o[l]),
                                  row(norm_mix_post[l]), row(norm_ffn_pre[l]), mod, l, w_router[l].T, seq)
        x2 = _moe(hp, logits_t, b_router[l].reshape(-1, 1), we_gate, we_up, we_down, bf(ws_gate[l]), bf(ws_up[l]),
                  bf(ws_down[l]), x2, row(norm_ffn_post[l]), mod, l, seq)
    return x2.reshape(nb, seq, d)
```

```python
import functools

import jax
import jax.numpy as jnp
from jax import lax
from jax.experimental import pallas as pl
from jax.experimental.pallas import tpu as pltpu
from jax.experimental.pallas import tpu_sc as plsc

F32 = jnp.float32
BF16 = jnp.bfloat16
HI = lax.Precision.HIGHEST

D_MODEL = 1024
CONV_WIDTH = 512
CONV_KERNEL = 31
RWKV_WIDTH = 512
HEAD = 64
N_HEADS = RWKV_WIDTH // HEAD
LORA_W = 64
LORA_A = 64
LORA_G = 128
N_EXPERTS = 64
TOP_K = 8
N_GROUPS = 8
TOPK_GROUPS = 4
GROUP_SIZE = N_EXPERTS // N_GROUPS
D_EXPERT = 256
ROUTED_SCALE = 2.5
RMS_EPS = 1e-6
LN_EPS = 1e-5
GN_EPS = 64e-5
COL_CONV = 2 * CONV_WIDTH
COL_RWKV = 3 * RWKV_WIDTH + LORA_W + LORA_A + LORA_G
COL_GATE = 2 * D_MODEL
D_IN = COL_CONV + COL_RWKV + COL_GATE

CHUNK = 64
QUAD = 4 * HEAD
N_QUADS = RWKV_WIDTH // QUAD
CUM_BLOCK = 256
CONV_HALO = 32
SUBLANES = 8
SHIFT_HALO = SUBLANES
GATE_LANES = 128
EXPERT_BLOCK = 512
EXPERT_SUB = 256
BLOCKS_PER_STEP = 2
ROW_WORDS = D_MODEL // 2
SC_CORES = 2
SC_SUBCORES = 16
SC_WORKERS = SC_CORES * SC_SUBCORES
SC_CHUNK = 128
VMEM_LIMIT = 52 * 1024 * 1024


def _params(*sem):
    return pltpu.CompilerParams(dimension_semantics=sem, vmem_limit_bytes=VMEM_LIMIT)


def _sigmoid(x):
    return 1.0 / (1.0 + jnp.exp(-x))


def _softplus(x):
    return jnp.maximum(x, 0.0) + jnp.log(1.0 + jnp.exp(-jnp.abs(x)))


def _rms(x, g):
    return x * lax.rsqrt(jnp.mean(x * x, axis=-1, keepdims=True) + RMS_EPS) * g


def _split(x):
    hi = x.astype(BF16)
    return hi, (x - hi.astype(F32)).astype(BF16)


def _dot(a, b):
    return jnp.dot(a, b, preferred_element_type=F32)


def _dot_exact_rhs(x, w):
    hi, lo = _split(x)
    return _dot(hi, w) + _dot(lo, w)


def _pack_rows(y):
    half = y.shape[1] // 2
    packed = pltpu.pack_elementwise([y[:, half:], y[:, :half]], packed_dtype=BF16)
    return lax.bitcast_convert_type(packed, jnp.int32)


def _unpack_rows(u):
    hi = pltpu.unpack_elementwise(u, index=1, packed_dtype=BF16, unpacked_dtype=F32)
    lo = pltpu.unpack_elementwise(u, index=0, packed_dtype=BF16, unpacked_dtype=F32)
    return jnp.concatenate([hi, lo], axis=1)


def _dot_exact_lhs(w, x):
    hi, lo = _split(x)
    return _dot(w, hi) + _dot(w, lo)


def _ada_kernel(c_ref, w_ref, b_ref, o_ref):
    c = c_ref[...]
    o_ref[...] = jnp.dot(c * _sigmoid(c), w_ref[...], precision=HI,
                         preferred_element_type=F32) + b_ref[...]


def _ada(c, w_ada, b_ada):
    nl, d, _ = w_ada.shape
    nb = c.shape[0]
    out = pl.pallas_call(
        _ada_kernel,
        grid=(nl, 6),
        in_specs=[
            pl.BlockSpec((nb, d), lambda l, k: (0, 0)),
            pl.BlockSpec((None, d, d), lambda l, k: (l, 0, k)),
            pl.BlockSpec((None, None, 1, d), lambda l, k: (l, k, 0, 0)),
        ],
        out_specs=pl.BlockSpec((None, None, nb, d), lambda l, k: (l, k, 0, 0)),
        out_shape=jax.ShapeDtypeStruct((nl, 6, nb, d), F32),
        compiler_params=_params("parallel", "parallel"),
        name="ada",
    )(c, w_ada, b_ada.reshape(nl, 6, 1, d))
    return out.reshape(nl, 6, nb, 1, d)


def _mod_spec(layer, piece, rows_per_batch, tm):
    return pl.BlockSpec((None, None, None, 1, D_MODEL),
                        lambda i, *_: (layer, piece, (i * tm) // rows_per_batch, 0, 0))


def _win_kernel(x_ref, g_ref, sc_ref, sh_ref, w_ref, zc_ref, zr_ref, zg_ref):
    h = _rms(x_ref[...], g_ref[...]) * (1.0 + sc_ref[...]) + sh_ref[...]
    hb = h.astype(BF16)
    zc_ref[...] = _dot(hb, w_ref[:, :COL_CONV]).astype(zc_ref.dtype)
    zr_ref[...] = _dot(hb, w_ref[:, COL_CONV:COL_CONV + COL_RWKV])
    zg_ref[...] = _dot(hb, w_ref[:, COL_CONV + COL_RWKV:]).astype(zg_ref.dtype)


def _win(x2, g, mod, layer, w_in_b, seq, tm=512):
    n, d = x2.shape
    row = lambda i: (i, 0)
    return pl.pallas_call(
        _win_kernel,
        grid=(n // tm,),
        in_specs=[
            pl.BlockSpec((tm, d), row),
            pl.BlockSpec((1, d), lambda i: (0, 0)),
            _mod_spec(layer, 1, seq, tm),
            _mod_spec(layer, 0, seq, tm),
            pl.BlockSpec((d, D_IN), lambda i: (0, 0)),
        ],
        out_specs=[pl.BlockSpec((tm, COL_CONV), row), pl.BlockSpec((tm, COL_RWKV), row),
                   pl.BlockSpec((tm, COL_GATE), row)],
        out_shape=[jax.ShapeDtypeStruct((n, COL_CONV), BF16), jax.ShapeDtypeStruct((n, COL_RWKV), F32),
                   jax.ShapeDtypeStruct((n, COL_GATE), BF16)],
        compiler_params=_params("parallel"),
        name="win",
    )(x2, g, mod, mod, w_in_b)


def _conv_kernel(z_ref, halo_ref, w_ref, cb_ref, g_ref, b_ref, o_ref, ubuf, *, ts):
    j = pl.program_id(1)
    z = z_ref[...].astype(F32)
    zh = halo_ref[...].astype(F32)
    uh = zh[:, :CONV_WIDTH] * _sigmoid(zh[:, CONV_WIDTH:])
    rows = CONV_HALO + ts
    u = jnp.concatenate([jnp.where(j > 0, uh, 0.0),
                         z[:, :CONV_WIDTH] * _sigmoid(z[:, CONV_WIDTH:])], axis=0)
    ubuf[0] = u
    for s in range(1, SUBLANES):
        ubuf[s] = pltpu.roll(u, rows - s, axis=0)
    sub = 64
    first = CONV_HALO - (CONV_KERNEL - 1)
    for r in range(ts // sub):
        acc = jnp.zeros((sub, CONV_WIDTH), F32) + cb_ref[...]
        for k in range(CONV_KERNEL):
            s = (first + k) % SUBLANES
            base = r * sub + first + k - s
            acc = acc + w_ref[k:k + 1, :] * ubuf[s, base:base + sub, :]
        mu = jnp.mean(acc, axis=-1, keepdims=True)
        dlt = acc - mu
        var = jnp.mean(dlt * dlt, axis=-1, keepdims=True)
        y = dlt * lax.rsqrt(var + LN_EPS) * g_ref[...] + b_ref[...]
        o_ref[r * sub:(r + 1) * sub, :] = (y * _sigmoid(y)).astype(o_ref.dtype)


def _conv(zc, conv_w, conv_b, ln_g, ln_b, nb, seq, ts=512):
    n = zc.shape[0]
    tps = seq // ts
    hpt = ts // CONV_HALO
    vec = pl.BlockSpec((1, CONV_WIDTH), lambda b, j: (0, 0))
    return pl.pallas_call(
        functools.partial(_conv_kernel, ts=ts),
        grid=(nb, tps),
        in_specs=[
            pl.BlockSpec((ts, COL_CONV), lambda b, j: (b * tps + j, 0)),
            pl.BlockSpec((CONV_HALO, COL_CONV), lambda b, j: (jnp.maximum((b * tps + j) * hpt - 1, 0), 0)),
            pl.BlockSpec((CONV_KERNEL, CONV_WIDTH), lambda b, j: (0, 0)),
            vec, vec, vec,
        ],
        out_specs=pl.BlockSpec((ts, CONV_WIDTH), lambda b, j: (b * tps + j, 0)),
        out_shape=jax.ShapeDtypeStruct((n, CONV_WIDTH), BF16),
        scratch_shapes=[pltpu.VMEM((SUBLANES, CONV_HALO + ts, CONV_WIDTH), F32)],
        compiler_params=_params("parallel", "parallel"),
        name="conv",
    )(zc, zc, conv_w, conv_b, ln_g, ln_b)


def _prep_kernel(z_ref, halo_ref, mu_ref, w0_ref, a0_ref, wah_ref, wal_ref, g2_ref, kk_ref, ka_ref, rk_ref,
                 ones_ref, tri_ref, csel_ref,
                 at_ref, bt_ref, kt_ref, rt_ref, v_ref, bts_ref, kts_ref, bonus_ref, g_ref, pc_ref):
    j = pl.program_id(1)
    z = z_ref[...]
    ts = z.shape[0]
    last = jnp.where(j > 0, halo_ref[SHIFT_HALO - 1:SHIFT_HALO, :], 0.0)
    row = lax.broadcasted_iota(jnp.int32, z.shape, 0)
    prev = jnp.where(row == 0, last, pltpu.roll(z, 1, axis=0))
    zs = z + (prev - z) * mu_ref[...]
    rw = RWKV_WIDTH
    r = zs[:, :rw]
    k = zs[:, rw:2 * rw]
    v = zs[:, 2 * rw:3 * rw]
    wad = zs[:, 3 * rw:3 * rw + LORA_W + LORA_A]
    gd = zs[:, 3 * rw + LORA_W + LORA_A:]
    lane = lax.broadcasted_iota(jnp.int32, wad.shape, 1)
    xh, xl = _split(jnp.where(lane < LORA_W, jnp.tanh(wad), wad))
    wah = wah_ref[...]
    lora = _dot(xh, wah) + _dot(xl, wah) + _dot(xh, wal_ref[...])
    w = -_softplus(-(w0_ref[...] + lora[:, :rw])) - 0.5
    lw = -jnp.exp(w)
    a = _sigmoid(a0_ref[...] + lora[:, rw:])
    g_ref[...] = _dot(_sigmoid(gd).astype(BF16), g2_ref[...]).astype(g_ref.dtype)
    kk = k * kk_ref[...]
    k2 = k * (1.0 + (a - 1.0) * ka_ref[...])
    ones = ones_ref[...]
    kk = kk * lax.rsqrt(jnp.maximum(_dot_exact_rhs(kk * kk, ones), 1e-24))
    bonus_ref[...] = (_dot_exact_rhs(r * k2 * rk_ref[...], ones) * v).astype(bonus_ref.dtype)
    tri = tri_ref[...]
    lc, tot = [], []
    for h in range(ts // CUM_BLOCK):
        cs = _dot_exact_lhs(tri, lw[h * CUM_BLOCK:(h + 1) * CUM_BLOCK])
        lc.append(cs[:CUM_BLOCK])
        tot.append(cs[CUM_BLOCK:])
    lc = jnp.concatenate(lc, axis=0)
    tot = jnp.concatenate(tot, axis=0)
    e_neg = jnp.exp(-lc)
    e_end = jnp.exp(tot - lc)
    kka = kk * a
    dt = at_ref.dtype
    at_ref[...] = (-kk * jnp.exp(lc - lw)).astype(dt)
    bt_ref[...] = (kka * e_neg).astype(dt)
    kt_ref[...] = (k2 * e_neg).astype(dt)
    rt_ref[...] = (r * jnp.exp(lc)).astype(dt)
    v_ref[...] = v.astype(dt)
    bts_ref[...] = (kka * e_end).astype(dt)
    kts_ref[...] = (k2 * e_end).astype(dt)
    pc_ref[...] = jnp.exp(_dot_exact_lhs(csel_ref[...], lw))


def _prep(zr, mu, w0, a0, wa, g2, k_k, k_a, r_k, nb, seq, ts=512):
    n = zr.shape[0]
    tps = seq // ts
    hpt = ts // SHIFT_HALO
    rw = RWKV_WIDTH
    cpt = ts // CHUNK
    ones = jnp.kron(jnp.eye(N_HEADS, dtype=F32), jnp.ones((HEAD, HEAD), F32)).astype(BF16)
    blk = jnp.kron(jnp.eye(CUM_BLOCK // CHUNK, dtype=F32), jnp.ones((CHUNK, CHUNK), F32))
    tri = jnp.concatenate([jnp.tril(blk), blk], axis=0).astype(BF16)
    csel = jnp.kron(jnp.eye(cpt, dtype=F32), jnp.ones((1, CHUNK), F32)).astype(BF16)
    wa_hi, wa_lo = _split(wa)
    vec = pl.BlockSpec((1, rw), lambda b, j: (0, 0))
    full = lambda shape: pl.BlockSpec(shape, lambda b, j: (0, 0))
    rows = pl.BlockSpec((ts, rw), lambda b, j: (b * tps + j, 0))
    return pl.pallas_call(
        _prep_kernel,
        grid=(nb, tps),
        in_specs=[
            pl.BlockSpec((ts, COL_RWKV), lambda b, j: (b * tps + j, 0)),
            pl.BlockSpec((SHIFT_HALO, COL_RWKV), lambda b, j: (jnp.maximum((b * tps + j) * hpt - 1, 0), 0)),
            full((1, COL_RWKV)), vec, vec,
            full((LORA_W + LORA_A, 2 * rw)), full((LORA_W + LORA_A, 2 * rw)), full((LORA_G, rw)),
            vec, vec, vec, full((rw, rw)), full((2 * CUM_BLOCK, CUM_BLOCK)), full((cpt, ts)),
        ],
        out_specs=[rows] * 9 + [pl.BlockSpec((cpt, rw), lambda b, j: (b * tps + j, 0))],
        out_shape=[jax.ShapeDtypeStruct((n, rw), BF16)] * 9 + [jax.ShapeDtypeStruct((n // CHUNK, rw), F32)],
        compiler_params=_params("parallel", "parallel"),
        name="prep",
    )(zr, zr, mu, w0, a0, wa_hi, wa_lo, g2.astype(BF16), k_k, k_a, r_k, ones, tri, csel)


def _mm(a, b):
    return jnp.dot(a.astype(BF16), b.astype(BF16), preferred_element_type=F32)


def _mm_nt(a, b):
    return lax.dot_general(a.astype(BF16), b.astype(BF16), (((1,), (1,)), ((), ())),
                           preferred_element_type=F32)


def _mm_tn(a, b):
    return lax.dot_general(a.astype(BF16), b.astype(BF16), (((0,), (0,)), ((), ())),
                           preferred_element_type=F32)


def _scan_kernel(at_ref, bt_ref, kt_ref, rt_ref, v_ref, bts_ref, kts_ref, bonus_ref, g_ref, pc_ref,
                 gng_ref, gnb_ref, ones_ref, o_ref, s_ref, y_ref, *, ts, gb):
    @pl.when(pl.program_id(1) == 0)
    def _():
        s_ref[...] = jnp.zeros_like(s_ref)

    row_q = lax.broadcasted_iota(jnp.int32, (QUAD, QUAD), 0)
    col_q = lax.broadcasted_iota(jnp.int32, (QUAD, QUAD), 1)
    bdmask = (row_q // HEAD) == (col_q // HEAD)
    row_c = lax.broadcasted_iota(jnp.int32, (CHUNK, QUAD), 0)
    col_c = lax.broadcasted_iota(jnp.int32, (CHUNK, QUAD), 1) % CHUNK
    strict = col_c < row_c
    incl = col_c <= row_c
    eye = jnp.where(col_c == row_c, 1.0, 0.0).astype(F32)

    def bd(x):
        xb = x.astype(BF16)
        return jnp.where(bdmask, jnp.concatenate([xb] * (QUAD // CHUNK), axis=0), jnp.zeros((), BF16))

    chains = [(b, qd) for b in range(gb) for qd in range(N_QUADS)]

    def chunk(c, carry):
        sl = pl.ds(pl.multiple_of(c * CHUNK, CHUNK), CHUNK)
        lanes = [slice(qd * QUAD, (qd + 1) * QUAD) for _, qd in chains]
        ld = lambda ref: [ref[b, sl, ln] for (b, _), ln in zip(chains, lanes)]
        at, bt, kt, rt, v, bts, kts = (ld(r) for r in (at_ref, bt_ref, kt_ref, rt_ref, v_ref, bts_ref, kts_ref))
        each = range(len(chains))
        ar = [jnp.concatenate([at[i], rt[i]], axis=0) for i in each]
        pb = [_mm_nt(ar[i], bd(bt[i])) for i in each]
        pk = [_mm_nt(ar[i], bd(kt[i])) for i in each]
        l_ab = [jnp.where(strict, p[:CHUNK], 0.0) for p in pb]
        a_rb = [jnp.where(incl, p[CHUNK:], 0.0) for p in pb]
        l_ak = [jnp.where(strict, p[:CHUNK], 0.0) for p in pk]
        a_rk = [jnp.where(incl, p[CHUNK:], 0.0) for p in pk]
        tm = [eye + l for l in l_ab]
        lp = [_mm(l, bd(l)) for l in l_ab]
        for _ in range(4):
            both = [_mm(jnp.concatenate([tm[i], lp[i]], axis=0), bd(lp[i])) for i in each]
            tm = [tm[i] + both[i][:CHUNK] for i in each]
            lp = [both[i][CHUNK:] for i in each]
        tm = [tm[i] + _mm(tm[i], bd(lp[i])) for i in each]
        bdv = [bd(x) for x in v]
        x0 = [_mm(l_ak[i], bdv[i]) for i in each]
        tu = [_mm(tm[i], jnp.concatenate([bd(x0[i]), bd(at[i])], axis=1)) for i in each]
        u0 = [t[:, :QUAD] for t in tu]
        w = [t[:, QUAD:] for t in tu]
        qy = [_mm(a_rb[i], jnp.concatenate([bd(w[i]), bd(u0[i])], axis=1)) for i in each]
        y0 = [qy[i][:, QUAD:] + _mm(a_rk[i], bdv[i]) for i in each]
        q = [rt[i].astype(F32) + qy[i][:, :QUAD] for i in each]
        s = [s_ref[i] for i in each]
        qw_s = [_mm_nt(jnp.concatenate([q[i], w[i]], axis=0), s[i]) for i in each]
        for i, ((b, _), ln) in enumerate(zip(chains, lanes)):
            y_ref[b, sl, ln] = y0[i] + qw_s[i][:CHUNK]
        u = [(u0[i] + qw_s[i][CHUNK:]).astype(BF16) for i in each]
        n_p = [jnp.where(bdmask, _mm_tn(jnp.concatenate([u[i], v[i]], axis=0),
                                        jnp.concatenate([bts[i], kts[i]], axis=0)), 0.0) for i in each]
        for i, ((b, _), ln) in enumerate(zip(chains, lanes)):
            s_ref[i] = s[i] * pc_ref[b, pl.ds(c, 1), ln] + n_p[i]
        return carry

    lax.fori_loop(0, ts // CHUNK, chunk, 0)

    ones = ones_ref[...]
    for b in range(gb):
        for qd in range(N_QUADS):
            ln = slice(qd * QUAD, (qd + 1) * QUAD)
            y = y_ref[b, :, ln]
            dlt = y - _dot_exact_rhs(y, ones) * (1.0 / HEAD)
            var = _dot_exact_rhs(dlt * dlt, ones) * (1.0 / HEAD)
            yn = dlt * lax.rsqrt(var + GN_EPS) * gng_ref[:, ln] + gnb_ref[:, ln]
            o_ref[b, :, ln] = ((yn + bonus_ref[b, :, ln].astype(F32)) * g_ref[b, :, ln].astype(F32)).astype(o_ref.dtype)


def _scan(prep_out, gn_g, gn_b, nb, seq, ts=128, gb=8):
    n = prep_out[0].shape[0]
    rw = RWKV_WIDTH
    cpt = ts // CHUNK
    ones = jnp.kron(jnp.eye(QUAD // HEAD, dtype=F32), jnp.ones((HEAD, HEAD), F32)).astype(BF16)
    rows = pl.BlockSpec((gb, ts, rw), lambda i, t: (i, t, 0))
    vec = pl.BlockSpec((1, rw), lambda i, t: (0, 0))
    args = [a.reshape(nb, seq, rw) for a in prep_out[:9]] + [prep_out[9].reshape(nb, seq // ts, cpt, rw)]
    out = pl.pallas_call(
        functools.partial(_scan_kernel, ts=ts, gb=gb),
        grid=(nb // gb, seq // ts),
        in_specs=[rows] * 9 + [pl.BlockSpec((gb, None, cpt, rw), lambda i, t: (i, t, 0, 0)), vec, vec,
                               pl.BlockSpec((QUAD, QUAD), lambda i, t: (0, 0))],
        out_specs=rows,
        out_shape=jax.ShapeDtypeStruct((nb, seq, rw), BF16),
        scratch_shapes=[pltpu.VMEM((gb * N_QUADS, QUAD, QUAD), F32), pltpu.VMEM((gb, ts, rw), F32)],
        compiler_params=_params("parallel", "arbitrary"),
        name="scan",
    )(*args, gn_g, gn_b, ones)
    return out.reshape(n, rw)


def _merge_kernel(uc_ref, ur_ref, zg_ref, x_ref, pc_ref, pr_ref, wo_ref, gpost_ref, gt_ref,
                  gpre_ref, sc_ref, sh_ref, wrh_ref, wrl_ref, xo_ref, h_ref, lg_ref):
    yc = _dot(uc_ref[...], pc_ref[...])
    yr = _dot(ur_ref[...], pr_ref[...])
    zg = zg_ref[...].astype(F32)
    m = _sigmoid(zg[:, :D_MODEL]) * yc + _sigmoid(zg[:, D_MODEL:]) * yr
    y = _dot(m.astype(BF16), wo_ref[...])
    xn = x_ref[...] + gt_ref[...] * _rms(y, gpost_ref[...])
    xo_ref[...] = xn
    h = _rms(xn, gpre_ref[...]) * (1.0 + sc_ref[...]) + sh_ref[...]
    h_ref[...] = _pack_rows(h)
    h_hi, h_lo = _split(h)
    nt = lambda a, b: lax.dot_general(a, b, (((1,), (1,)), ((), ())), preferred_element_type=F32)
    w_hi = wrh_ref[...]
    lg_ref[...] = nt(w_hi, h_hi) + nt(w_hi, h_lo) + nt(wrl_ref[...], h_hi)


def _merge(uc, ur, zg, x2, p_conv_b, p_rwkv_b, w_o_b, g_post, g_pre, mod, layer, w_router_t, seq, tm=512):
    n, d = x2.shape
    row = lambda i: (i, 0)
    full = lambda shape: pl.BlockSpec(shape, lambda i: (0, 0))
    return pl.pallas_call(
        _merge_kernel,
        grid=(n // tm,),
        in_specs=[
            pl.BlockSpec((tm, CONV_WIDTH), row), pl.BlockSpec((tm, RWKV_WIDTH), row),
            pl.BlockSpec((tm, COL_GATE), row), pl.BlockSpec((tm, d), row),
            full((CONV_WIDTH, d)), full((RWKV_WIDTH, d)), full((d, d)),
            full((1, d)), _mod_spec(layer, 2, seq, tm),
            full((1, d)), _mod_spec(layer, 4, seq, tm), _mod_spec(layer, 3, seq, tm),
            full((N_EXPERTS, d)), full((N_EXPERTS, d)),
        ],
        out_specs=[pl.BlockSpec((tm, d), row), pl.BlockSpec((tm, ROW_WORDS), row),
                   pl.BlockSpec((N_EXPERTS, tm), lambda i: (0, i))],
        out_shape=[jax.ShapeDtypeStruct((n, d), F32), jax.ShapeDtypeStruct((n, ROW_WORDS), jnp.int32),
                   jax.ShapeDtypeStruct((N_EXPERTS, n), F32)],
        compiler_params=_params("parallel"),
        name="merge",
    )(uc, ur, zg, x2, p_conv_b, p_rwkv_b, w_o_b, g_post, mod, g_pre, mod, mod, *_split(w_router_t))


def _route_kernel(lg_ref, b_ref, gate_ref, sel_ref, cnt_ref):
    s = _sigmoid(lg_ref[...])
    biased = s + b_ref[...]
    t = s.shape[1]
    member = lax.broadcasted_iota(jnp.int32, (GROUP_SIZE, t), 0)
    grp = []
    for g in range(N_GROUPS):
        bg = biased[g * GROUP_SIZE:(g + 1) * GROUP_SIZE, :]
        m1 = jnp.max(bg, axis=0, keepdims=True)
        first = jnp.min(jnp.where(bg == m1, member, GROUP_SIZE), axis=0, keepdims=True)
        m2 = jnp.max(jnp.where(member == first, -jnp.inf, bg), axis=0, keepdims=True)
        grp.append(m1 + m2)
    masked = []
    for g in range(N_GROUPS):
        rank = jnp.zeros((1, t), jnp.int32)
        for o in range(N_GROUPS):
            if o == g:
                continue
            ahead = (grp[o] > grp[g]) if o > g else (grp[o] >= grp[g])
            rank = rank + jnp.where(ahead, 1, 0)
        keep = rank < TOPK_GROUPS
        masked.append(jnp.where(keep, biased[g * GROUP_SIZE:(g + 1) * GROUP_SIZE, :], -jnp.inf))
    masked = jnp.concatenate(masked, axis=0)
    eidx = lax.broadcasted_iota(jnp.int32, masked.shape, 0)
    mask = jnp.zeros(masked.shape, F32)
    for _ in range(TOP_K):
        top = jnp.max(masked, axis=0, keepdims=True)
        hit = eidx == jnp.min(jnp.where(masked == top, eidx, N_EXPERTS), axis=0, keepdims=True)
        mask = jnp.where(hit, 1.0, mask)
        masked = jnp.where(hit, -jnp.inf, masked)
    sel = s * mask
    gate_ref[...] = sel / jnp.sum(sel, axis=0, keepdims=True) * ROUTED_SCALE
    sel_ref[...] = mask.astype(sel_ref.dtype)

    @pl.when(pl.program_id(0) == 0)
    def _():
        cnt_ref[...] = jnp.zeros_like(cnt_ref)

    cnt_ref[...] += jnp.sum(mask, axis=1, keepdims=True)


def _route(logits_t, b_router, tt=512):
    e, n = logits_t.shape
    return pl.pallas_call(
        _route_kernel,
        grid=(n // tt,),
        in_specs=[pl.BlockSpec((e, tt), lambda i: (0, i)), pl.BlockSpec((e, 1), lambda i: (0, 0))],
        out_specs=[pl.BlockSpec((e, tt), lambda i: (0, i)), pl.BlockSpec((e, tt), lambda i: (0, i)),
                   pl.BlockSpec((e, GATE_LANES), lambda i: (0, 0))],
        out_shape=[jax.ShapeDtypeStruct((e, n), F32), jax.ShapeDtypeStruct((e, n), BF16),
                   jax.ShapeDtypeStruct((e, GATE_LANES), F32)],
        compiler_params=_params("arbitrary"),
        name="route",
    )(logits_t, b_router)


def _dest_kernel(sel_ref, gate_ref, start_ref, triu_ref, below_ref, dest_ref, g8_ref, run_ref):
    @pl.when(pl.program_id(0) == 0)
    def _():
        run_ref[...] = jnp.zeros_like(run_ref)

    sel = sel_ref[...]
    t = sel.shape[1]
    self32 = sel.astype(F32)
    incl = _dot(sel, triu_ref[...])
    pos = start_ref[...] + run_ref[...] + incl - self32
    run_ref[...] += incl[:, t - 1:t]
    slot = _dot(below_ref[...], sel)
    gate = gate_ref[...]
    dst, gts = [], []
    for k in range(TOP_K):
        mine = (self32 > 0.0) & (slot == float(k))
        dst.append(jnp.sum(jnp.where(mine, pos, 0.0), axis=0, keepdims=True))
        gts.append(jnp.sum(jnp.where(mine, gate, 0.0), axis=0, keepdims=True))
    dest_ref[...] = jnp.concatenate(dst, axis=0).astype(jnp.int32)
    g8_ref[...] = jnp.concatenate(gts + [jnp.zeros((GATE_LANES - TOP_K, t), F32)], axis=0).T


def _dest(sel, gate, seg_start, tt=512):
    e, n = sel.shape
    triu = jnp.triu(jnp.ones((tt, tt), F32)).astype(BF16)
    below = jnp.tril(jnp.ones((e, e), F32), -1).astype(BF16)
    return pl.pallas_call(
        _dest_kernel,
        grid=(n // tt,),
        in_specs=[pl.BlockSpec((e, tt), lambda i: (0, i)), pl.BlockSpec((e, tt), lambda i: (0, i)),
                  pl.BlockSpec((e, 1), lambda i: (0, 0)), pl.BlockSpec((tt, tt), lambda i: (0, 0)),
                  pl.BlockSpec((e, e), lambda i: (0, 0))],
        out_specs=[pl.BlockSpec((TOP_K, tt), lambda i: (0, i)), pl.BlockSpec((tt, GATE_LANES), lambda i: (i, 0))],
        out_shape=[jax.ShapeDtypeStruct((TOP_K, n), jnp.int32), jax.ShapeDtypeStruct((n, GATE_LANES), F32)],
        scratch_shapes=[pltpu.VMEM((e, 1), F32)],
        compiler_params=_params("arbitrary"),
        name="dest",
    )(sel, gate, seg_start, triu, below)


def _sc_mesh():
    return plsc.VectorSubcoreMesh(core_axis_name="c", subcore_axis_name="s",
                                  num_cores=SC_CORES, num_subcores=SC_SUBCORES)


def _worker_id():
    return lax.axis_index("s") * SC_CORES + lax.axis_index("c")


def _sc_scatter_rows(rows, dest, n_out):
    n, w = rows.shape
    per_worker = n // SC_WORKERS
    chunks_per_worker = per_worker // SC_CHUNK
    idx_rows = dest.reshape(TOP_K, n // SC_CHUNK, SC_CHUNK).transpose(1, 0, 2).reshape(-1, SC_CHUNK)

    def body(rows_hbm, idx_hbm, out_hbm, idx_v, rows_v):
        wid = _worker_id()

        @pl.loop(0, chunks_per_worker)
        def _(c):
            chunk = wid * chunks_per_worker + c
            pltpu.sync_copy(rows_hbm.at[pl.ds(chunk * SC_CHUNK, SC_CHUNK)], rows_v)
            pltpu.sync_copy(idx_hbm.at[pl.ds(chunk * TOP_K, TOP_K)], idx_v)
            for k in range(TOP_K):
                pltpu.sync_copy(rows_v, out_hbm.at[idx_v.at[k]])

    return pl.kernel(
        body, out_type=jax.ShapeDtypeStruct((n_out, w), rows.dtype), mesh=_sc_mesh(),
        scratch_types=[pltpu.VMEM((TOP_K, SC_CHUNK), jnp.int32), pltpu.VMEM((SC_CHUNK, w), rows.dtype)],
        name="sc_scatter",
    )(rows, idx_rows)


def _sc_gather_rows(table, idx_flat):
    m = idx_flat.shape[0]
    w = table.shape[1]
    per_worker = m // SC_WORKERS

    def body(table_hbm, idx_hbm, out_hbm, idx_v, rows_v):
        base = _worker_id() * per_worker

        @pl.loop(0, per_worker // SC_CHUNK)
        def _(c):
            j0 = base + c * SC_CHUNK
            pltpu.sync_copy(idx_hbm.at[pl.ds(j0, SC_CHUNK)], idx_v)
            pltpu.sync_copy(table_hbm.at[idx_v], rows_v)
            pltpu.sync_copy(rows_v, out_hbm.at[pl.ds(j0, SC_CHUNK)])

    return pl.kernel(
        body, out_type=jax.ShapeDtypeStruct((m, w), table.dtype), mesh=_sc_mesh(),
        scratch_types=[pltpu.VMEM((SC_CHUNK,), jnp.int32), pltpu.VMEM((SC_CHUNK, w), table.dtype)],
        name="sc_gather",
    )(table, idx_flat)


def _swiglu(h, wg, wu):
    gte = _dot(h, wg)
    return gte * _sigmoid(gte) * _dot(h, wu)


def _experts_kernel(be_ref, va_ref, xs_ref, *refs):
    nb = BLOCKS_PER_STEP
    wg32, wu32, wd32, ys_ref = refs[:nb], refs[nb:2 * nb], refs[2 * nb:3 * nb], refs[3 * nb]
    wg, wu, wd = (refs[3 * nb + 1 + i * nb:3 * nb + 1 + (i + 1) * nb] for i in range(3))
    step = pl.program_id(0)
    first = step * nb
    live = va_ref[first] > 0

    for b in range(nb):
        changed = jnp.logical_or(step == 0, be_ref[first + b] != be_ref[jnp.maximum(first + b - nb, 0)])

        @pl.when(jnp.logical_and(live, changed))
        def _(b=b):
            wg[b][...] = wg32[b][...].astype(BF16)
            wu[b][...] = wu32[b][...].astype(BF16)
            wd[b][...] = wd32[b][...].astype(BF16)

    @pl.when(live)
    def _():
        row = lax.broadcasted_iota(jnp.int32, (EXPERT_SUB, D_MODEL), 0)
        subs = [(b, r0) for b in range(nb) for r0 in range(0, EXPERT_BLOCK, EXPERT_SUB)]
        x = []
        for b, r0 in subs:
            lo = b * EXPERT_BLOCK + r0
            xi = _unpack_rows(xs_ref[lo:lo + EXPERT_SUB, :])
            x.append(jnp.where(row < va_ref[first + b] - r0, xi, 0.0).astype(BF16))
        gte = [_dot(xi, wg[b][...]) for xi, (b, _) in zip(x, subs)]
        up = [_dot(xi, wu[b][...]) for xi, (b, _) in zip(x, subs)]
        act = [(g * _sigmoid(g) * u).astype(BF16) for g, u in zip(gte, up)]
        y = [_dot(a, wd[b][...]) for a, (b, _) in zip(act, subs)]
        for yi, (b, r0) in zip(y, subs):
            lo = b * EXPERT_BLOCK + r0
            ys_ref[lo:lo + EXPERT_SUB, :] = _pack_rows(yi)


def _experts(xs, blk_expert, blk_valid, wg, wu, wd, layer):
    n_rows, w = xs.shape
    d = wg.shape[2]
    nb = BLOCKS_PER_STEP
    rows = pl.BlockSpec((nb * EXPERT_BLOCK, w), lambda s, be, va: (s, 0))
    up_spec = lambda i: pl.BlockSpec((None, None, d, D_EXPERT), lambda s, be, va: (layer, be[s * nb + i], 0, 0))
    down_spec = lambda i: pl.BlockSpec((None, None, D_EXPERT, d), lambda s, be, va: (layer, be[s * nb + i], 0, 0))
    return pl.pallas_call(
        _experts_kernel,
        grid_spec=pltpu.PrefetchScalarGridSpec(
            num_scalar_prefetch=2,
            grid=(n_rows // (nb * EXPERT_BLOCK),),
            in_specs=[rows] + [up_spec(i) for i in range(nb)] * 2 + [down_spec(i) for i in range(nb)],
            out_specs=rows,
            scratch_shapes=([pltpu.VMEM((d, D_EXPERT), BF16)] * (2 * nb) + [pltpu.VMEM((D_EXPERT, d), BF16)] * nb),
        ),
        out_shape=jax.ShapeDtypeStruct((n_rows, w), xs.dtype),
        compiler_params=_params("arbitrary"),
        name="experts",
    )(blk_expert, blk_valid, xs, *([wg] * nb), *([wu] * nb), *([wd] * nb))


def _combine_kernel(yg_ref, g8_ref, hp_ref, sg_ref, su_ref, sd_ref, x_ref, gpost_ref, gt_ref, o_ref):
    h = _unpack_rows(hp_ref[...]).astype(BF16)
    acc = _dot(_swiglu(h, sg_ref[...], su_ref[...]).astype(BF16), sd_ref[...])
    g8 = g8_ref[...]
    for k in range(TOP_K):
        acc = acc + g8[:, k:k + 1] * _unpack_rows(yg_ref[k])
    o_ref[...] = x_ref[...] + gt_ref[...] * _rms(acc, gpost_ref[...])


def _combine(yg, g8, hp, sg, su, sd, x2, g_post, mod, layer, seq, tm=512):
    n, d = x2.shape
    row = lambda i: (i, 0)
    full = lambda shape: pl.BlockSpec(shape, lambda i: (0, 0))
    in_specs = [
        pl.BlockSpec((TOP_K, tm, ROW_WORDS), lambda i: (0, i, 0)),
        pl.BlockSpec((tm, GATE_LANES), row),
        pl.BlockSpec((tm, ROW_WORDS), row),
        full((d, D_EXPERT)), full((d, D_EXPERT)), full((D_EXPERT, d)),
        pl.BlockSpec((tm, d), row),
        full((1, d)),
        _mod_spec(layer, 5, seq, tm),
    ]
    return pl.pallas_call(
        _combine_kernel,
        grid=(n // tm,),
        in_specs=in_specs,
        out_specs=pl.BlockSpec((tm, d), row),
        out_shape=jax.ShapeDtypeStruct((n, d), F32),
        compiler_params=_params("parallel"),
        name="combine",
    )(yg, g8, hp, sg, su, sd, x2, g_post, mod)


def _moe(hp, logits_t, b_router, wg, wu, wd, sg, su, sd, x2, g_post, mod, layer, seq):
    n = x2.shape[0]
    gate, sel, counts = _route(logits_t, b_router)
    cnt = counts[:, 0].astype(jnp.int32)
    padded = (cnt + EXPERT_BLOCK - 1) // EXPERT_BLOCK * EXPERT_BLOCK
    seg_end = jnp.cumsum(padded)
    seg_start = seg_end - padded
    n_blocks = n * TOP_K // EXPERT_BLOCK + N_EXPERTS
    blk_row = jnp.arange(n_blocks, dtype=jnp.int32) * EXPERT_BLOCK
    blk_expert = jnp.sum((seg_end[None, :] <= blk_row[:, None]).astype(jnp.int32), axis=1)
    blk_expert = jnp.minimum(blk_expert, N_EXPERTS - 1)
    mine = blk_expert[:, None] == jnp.arange(N_EXPERTS, dtype=jnp.int32)[None, :]
    seg_last = jnp.sum(jnp.where(mine, (seg_start + cnt)[None, :], 0), axis=1)
    blk_valid = jnp.clip(seg_last - blk_row, 0, EXPERT_BLOCK).astype(jnp.int32)
    dest, g8 = _dest(sel, gate, seg_start.astype(F32).reshape(-1, 1))
    dest_flat = dest.reshape(-1)
    xs = _sc_scatter_rows(hp, dest, n_blocks * EXPERT_BLOCK)
    ys = _experts(xs, blk_expert, blk_valid, wg, wu, wd, layer)
    yg = _sc_gather_rows(ys, dest_flat).reshape(TOP_K, n, ROW_WORDS)
    return _combine(yg, g8, hp, sg, su, sd, x2, g_post, mod, layer, seq)


def kernel(x, c, w_ada, b_ada, norm_mix_pre, norm_mix_post, norm_ffn_pre, norm_ffn_post, w_in, mu_shift, conv_w, conv_b, conv_ln_g, conv_ln_b, p_conv, w0, w2, a0, a2, g2, k_k, k_a, r_k, gn_g, gn_b, p_rwkv, w_o, w_router, b_router, we_gate, we_up, we_down, ws_gate, ws_up, ws_down):
    nb, seq, d = x.shape
    depth = w_ada.shape[0]
    n = nb * seq
    rw = RWKV_WIDTH
    mod = _ada(c, w_ada, b_ada)
    x2 = x.reshape(n, d)
    row = lambda a: a.reshape(1, -1)
    bf = lambda a: a.astype(BF16)
    for l in range(depth):
        zc, zr, zg = _win(x2, row(norm_mix_pre[l]), mod, l, bf(w_in[l]), seq)
        uc = _conv(zc, conv_w[l], row(conv_b[l]), row(conv_ln_g[l]), row(conv_ln_b[l]), nb, seq)
        wa = jnp.zeros((LORA_W + LORA_A, 2 * rw), F32)
        wa = wa.at[:LORA_W, :rw].set(w2[l]).at[LORA_W:, rw:].set(a2[l])
        prep_out = _prep(zr, row(mu_shift[l]), row(w0[l]), row(a0[l]), wa, g2[l], row(k_k[l]),
                         row(k_a[l]), row(r_k[l]), nb, seq)
        ur = _scan(prep_out, row(gn_g[l]), row(gn_b[l]), nb, seq)
        x2, hp, logits_t = _merge(uc, ur, zg, x2, bf(p_conv[l]), bf(p_rwkv[l]), bf(w_o[l]),
                                  row(norm_mix_post[l]), row(norm_ffn_pre[l]), mod, l, w_router[l].T, seq)
        x2 = _moe(hp, logits_t, b_router[l].reshape(-1, 1), we_gate, we_up, we_down, bf(ws_gate[l]), bf(ws_up[l]),
                  bf(ws_down[l]), x2, row(norm_ffn_post[l]), mod, l, seq)
    return x2.reshape(nb, seq, d)
```

```python
import functools

import jax
import jax.numpy as jnp
from jax import lax
from jax.experimental import pallas as pl
from jax.experimental.pallas import tpu as pltpu
from jax.experimental.pallas import tpu_sc as plsc

F32 = jnp.float32
BF16 = jnp.bfloat16
HI = lax.Precision.HIGHEST

D_MODEL = 1024
CONV_WIDTH = 512
CONV_KERNEL = 31
RWKV_WIDTH = 512
HEAD = 64
N_HEADS = RWKV_WIDTH // HEAD
LORA_W = 64
LORA_A = 64
LORA_G = 128
N_EXPERTS = 64
TOP_K = 8
N_GROUPS = 8
TOPK_GROUPS = 4
GROUP_SIZE = N_EXPERTS // N_GROUPS
D_EXPERT = 256
ROUTED_SCALE = 2.5
RMS_EPS = 1e-6
LN_EPS = 1e-5
GN_EPS = 64e-5
COL_CONV = 2 * CONV_WIDTH
COL_RWKV = 3 * RWKV_WIDTH + LORA_W + LORA_A + LORA_G
COL_GATE = 2 * D_MODEL
D_IN = COL_CONV + COL_RWKV + COL_GATE

CHUNK = 64
QUAD = 4 * HEAD
N_QUADS = RWKV_WIDTH // QUAD
CUM_BLOCK = 256
CONV_HALO = 32
SUBLANES = 8
SHIFT_HALO = SUBLANES
GATE_LANES = 128
EXPERT_BLOCK = 512
EXPERT_SUB = 256
BLOCKS_PER_STEP = 2
ROW_WORDS = D_MODEL // 2
SC_CORES = 2
SC_SUBCORES = 16
SC_WORKERS = SC_CORES * SC_SUBCORES
SC_CHUNK = 128
SC_IDX_ROWS = 8
VMEM_LIMIT = 52 * 1024 * 1024


def _params(*sem):
    return pltpu.CompilerParams(dimension_semantics=sem, vmem_limit_bytes=VMEM_LIMIT)


def _sigmoid(x):
    return 1.0 / (1.0 + jnp.exp(-x))


def _softplus(x):
    return jnp.maximum(x, 0.0) + jnp.log(1.0 + jnp.exp(-jnp.abs(x)))


def _rms(x, g):
    return x * lax.rsqrt(jnp.mean(x * x, axis=-1, keepdims=True) + RMS_EPS) * g


def _split(x):
    hi = x.astype(BF16)
    return hi, (x - hi.astype(F32)).astype(BF16)


def _dot(a, b):
    return jnp.dot(a, b, preferred_element_type=F32)


def _dot_exact_rhs(x, w):
    hi, lo = _split(x)
    return _dot(hi, w) + _dot(lo, w)


def _pack_rows(y):
    half = y.shape[1] // 2
    packed = pltpu.pack_elementwise([y[:, half:], y[:, :half]], packed_dtype=BF16)
    return lax.bitcast_convert_type(packed, jnp.int32)


def _unpack_rows(u):
    hi = pltpu.unpack_elementwise(u, index=1, packed_dtype=BF16, unpacked_dtype=F32)
    lo = pltpu.unpack_elementwise(u, index=0, packed_dtype=BF16, unpacked_dtype=F32)
    return jnp.concatenate([hi, lo], axis=1)


def _dot_exact_lhs(w, x):
    hi, lo = _split(x)
    return _dot(w, hi) + _dot(w, lo)


def _ada_kernel(c_ref, w_ref, b_ref, o_ref):
    c = c_ref[...]
    o_ref[...] = jnp.dot(c * _sigmoid(c), w_ref[...], precision=HI,
                         preferred_element_type=F32) + b_ref[...]


def _ada(c, w_ada, b_ada):
    nl, d, _ = w_ada.shape
    nb = c.shape[0]
    out = pl.pallas_call(
        _ada_kernel,
        grid=(nl, 6),
        in_specs=[
            pl.BlockSpec((nb, d), lambda l, k: (0, 0)),
            pl.BlockSpec((None, d, d), lambda l, k: (l, 0, k)),
            pl.BlockSpec((None, None, 1, d), lambda l, k: (l, k, 0, 0)),
        ],
        out_specs=pl.BlockSpec((None, None, nb, d), lambda l, k: (l, k, 0, 0)),
        out_shape=jax.ShapeDtypeStruct((nl, 6, nb, d), F32),
        compiler_params=_params("parallel", "parallel"),
        name="ada",
    )(c, w_ada, b_ada.reshape(nl, 6, 1, d))
    return out.reshape(nl, 6, nb, 1, d)


def _mod_spec(layer, piece, rows_per_batch, tm):
    return pl.BlockSpec((None, None, None, 1, D_MODEL),
                        lambda i, *_: (layer, piece, (i * tm) // rows_per_batch, 0, 0))


def _win_kernel(x_ref, g_ref, sc_ref, sh_ref, w_ref, zc_ref, zr_ref, zg_ref):
    h = _rms(x_ref[...], g_ref[...]) * (1.0 + sc_ref[...]) + sh_ref[...]
    hb = h.astype(BF16)
    zc_ref[...] = _dot(hb, w_ref[:, :COL_CONV]).astype(zc_ref.dtype)
    zr_ref[...] = _dot(hb, w_ref[:, COL_CONV:COL_CONV + COL_RWKV])
    zg_ref[...] = _dot(hb, w_ref[:, COL_CONV + COL_RWKV:]).astype(zg_ref.dtype)


def _win(x2, g, mod, layer, w_in_b, seq, tm=512):
    n, d = x2.shape
    row = lambda i: (i, 0)
    return pl.pallas_call(
        _win_kernel,
        grid=(n // tm,),
        in_specs=[
            pl.BlockSpec((tm, d), row),
            pl.BlockSpec((1, d), lambda i: (0, 0)),
            _mod_spec(layer, 1, seq, tm),
            _mod_spec(layer, 0, seq, tm),
            pl.BlockSpec((d, D_IN), lambda i: (0, 0)),
        ],
        out_specs=[pl.BlockSpec((tm, COL_CONV), row), pl.BlockSpec((tm, COL_RWKV), row),
                   pl.BlockSpec((tm, COL_GATE), row)],
        out_shape=[jax.ShapeDtypeStruct((n, COL_CONV), BF16), jax.ShapeDtypeStruct((n, COL_RWKV), F32),
                   jax.ShapeDtypeStruct((n, COL_GATE), BF16)],
        compiler_params=_params("parallel"),
        name="win",
    )(x2, g, mod, mod, w_in_b)


def _conv_kernel(z_ref, halo_ref, w_ref, cb_ref, g_ref, b_ref, o_ref, ubuf, *, ts):
    j = pl.program_id(1)
    z = z_ref[...].astype(F32)
    zh = halo_ref[...].astype(F32)
    uh = zh[:, :CONV_WIDTH] * _sigmoid(zh[:, CONV_WIDTH:])
    rows = CONV_HALO + ts
    u = jnp.concatenate([jnp.where(j > 0, uh, 0.0),
                         z[:, :CONV_WIDTH] * _sigmoid(z[:, CONV_WIDTH:])], axis=0)
    ubuf[0] = u
    for s in range(1, SUBLANES):
        ubuf[s] = pltpu.roll(u, rows - s, axis=0)
    sub = 64
    first = CONV_HALO - (CONV_KERNEL - 1)
    for r in range(ts // sub):
        acc = jnp.zeros((sub, CONV_WIDTH), F32) + cb_ref[...]
        for k in range(CONV_KERNEL):
            s = (first + k) % SUBLANES
            base = r * sub + first + k - s
            acc = acc + w_ref[k:k + 1, :] * ubuf[s, base:base + sub, :]
        mu = jnp.mean(acc, axis=-1, keepdims=True)
        dlt = acc - mu
        var = jnp.mean(dlt * dlt, axis=-1, keepdims=True)
        y = dlt * lax.rsqrt(var + LN_EPS) * g_ref[...] + b_ref[...]
        o_ref[r * sub:(r + 1) * sub, :] = (y * _sigmoid(y)).astype(o_ref.dtype)


def _conv(zc, conv_w, conv_b, ln_g, ln_b, nb, seq, ts=512):
    n = zc.shape[0]
    tps = seq // ts
    hpt = ts // CONV_HALO
    vec = pl.BlockSpec((1, CONV_WIDTH), lambda b, j: (0, 0))
    return pl.pallas_call(
        functools.partial(_conv_kernel, ts=ts),
        grid=(nb, tps),
        in_specs=[
            pl.BlockSpec((ts, COL_CONV), lambda b, j: (b * tps + j, 0)),
            pl.BlockSpec((CONV_HALO, COL_CONV), lambda b, j: (jnp.maximum((b * tps + j) * hpt - 1, 0), 0)),
            pl.BlockSpec((CONV_KERNEL, CONV_WIDTH), lambda b, j: (0, 0)),
            vec, vec, vec,
        ],
        out_specs=pl.BlockSpec((ts, CONV_WIDTH), lambda b, j: (b * tps + j, 0)),
        out_shape=jax.ShapeDtypeStruct((n, CONV_WIDTH), BF16),
        scratch_shapes=[pltpu.VMEM((SUBLANES, CONV_HALO + ts, CONV_WIDTH), F32)],
        compiler_params=_params("parallel", "parallel"),
        name="conv",
    )(zc, zc, conv_w, conv_b, ln_g, ln_b)


def _prep_kernel(z_ref, halo_ref, mu_ref, w0_ref, a0_ref, wah_ref, wal_ref, g2_ref, kk_ref, ka_ref, rk_ref,
                 ones_ref, tri_ref, csel_ref,
                 at_ref, bt_ref, kt_ref, rt_ref, v_ref, bts_ref, kts_ref, bonus_ref, g_ref, pc_ref):
    j = pl.program_id(1)
    z = z_ref[...]
    ts = z.shape[0]
    last = jnp.where(j > 0, halo_ref[SHIFT_HALO - 1:SHIFT_HALO, :], 0.0)
    row = lax.broadcasted_iota(jnp.int32, z.shape, 0)
    prev = jnp.where(row == 0, last, pltpu.roll(z, 1, axis=0))
    zs = z + (prev - z) * mu_ref[...]
    rw = RWKV_WIDTH
    r = zs[:, :rw]
    k = zs[:, rw:2 * rw]
    v = zs[:, 2 * rw:3 * rw]
    wad = zs[:, 3 * rw:3 * rw + LORA_W + LORA_A]
    gd = zs[:, 3 * rw + LORA_W + LORA_A:]
    lane = lax.broadcasted_iota(jnp.int32, wad.shape, 1)
    xh, xl = _split(jnp.where(lane < LORA_W, jnp.tanh(wad), wad))
    wah = wah_ref[...]
    lora = _dot(xh, wah) + _dot(xl, wah) + _dot(xh, wal_ref[...])
    w = -_softplus(-(w0_ref[...] + lora[:, :rw])) - 0.5
    lw = -jnp.exp(w)
    a = _sigmoid(a0_ref[...] + lora[:, rw:])
    g_ref[...] = _dot(_sigmoid(gd).astype(BF16), g2_ref[...]).astype(g_ref.dtype)
    kk = k * kk_ref[...]
    k2 = k * (1.0 + (a - 1.0) * ka_ref[...])
    ones = ones_ref[...]
    kk = kk * lax.rsqrt(jnp.maximum(_dot_exact_rhs(kk * kk, ones), 1e-24))
    bonus_ref[...] = (_dot_exact_rhs(r * k2 * rk_ref[...], ones) * v).astype(bonus_ref.dtype)
    tri = tri_ref[...]
    lc, tot = [], []
    for h in range(ts // CUM_BLOCK):
        cs = _dot_exact_lhs(tri, lw[h * CUM_BLOCK:(h + 1) * CUM_BLOCK])
        lc.append(cs[:CUM_BLOCK])
        tot.append(cs[CUM_BLOCK:])
    lc = jnp.concatenate(lc, axis=0)
    tot = jnp.concatenate(tot, axis=0)
    e_neg = jnp.exp(-lc)
    e_end = jnp.exp(tot - lc)
    kka = kk * a
    dt = at_ref.dtype
    at_ref[...] = (-kk * jnp.exp(lc - lw)).astype(dt)
    bt_ref[...] = (kka * e_neg).astype(dt)
    kt_ref[...] = (k2 * e_neg).astype(dt)
    rt_ref[...] = (r * jnp.exp(lc)).astype(dt)
    v_ref[...] = v.astype(dt)
    bts_ref[...] = (kka * e_end).astype(dt)
    kts_ref[...] = (k2 * e_end).astype(dt)
    pc_ref[...] = jnp.exp(_dot_exact_lhs(csel_ref[...], lw))


def _prep(zr, mu, w0, a0, wa, g2, k_k, k_a, r_k, nb, seq, ts=512):
    n = zr.shape[0]
    tps = seq // ts
    hpt = ts // SHIFT_HALO
    rw = RWKV_WIDTH
    cpt = ts // CHUNK
    ones = jnp.kron(jnp.eye(N_HEADS, dtype=F32), jnp.ones((HEAD, HEAD), F32)).astype(BF16)
    blk = jnp.kron(jnp.eye(CUM_BLOCK // CHUNK, dtype=F32), jnp.ones((CHUNK, CHUNK), F32))
    tri = jnp.concatenate([jnp.tril(blk), blk], axis=0).astype(BF16)
    csel = jnp.kron(jnp.eye(cpt, dtype=F32), jnp.ones((1, CHUNK), F32)).astype(BF16)
    wa_hi, wa_lo = _split(wa)
    vec = pl.BlockSpec((1, rw), lambda b, j: (0, 0))
    full = lambda shape: pl.BlockSpec(shape, lambda b, j: (0, 0))
    rows = pl.BlockSpec((ts, rw), lambda b, j: (b * tps + j, 0))
    return pl.pallas_call(
        _prep_kernel,
        grid=(nb, tps),
        in_specs=[
            pl.BlockSpec((ts, COL_RWKV), lambda b, j: (b * tps + j, 0)),
            pl.BlockSpec((SHIFT_HALO, COL_RWKV), lambda b, j: (jnp.maximum((b * tps + j) * hpt - 1, 0), 0)),
            full((1, COL_RWKV)), vec, vec,
            full((LORA_W + LORA_A, 2 * rw)), full((LORA_W + LORA_A, 2 * rw)), full((LORA_G, rw)),
            vec, vec, vec, full((rw, rw)), full((2 * CUM_BLOCK, CUM_BLOCK)), full((cpt, ts)),
        ],
        out_specs=[rows] * 9 + [pl.BlockSpec((cpt, rw), lambda b, j: (b * tps + j, 0))],
        out_shape=[jax.ShapeDtypeStruct((n, rw), BF16)] * 9 + [jax.ShapeDtypeStruct((n // CHUNK, rw), F32)],
        compiler_params=_params("parallel", "parallel"),
        name="prep",
    )(zr, zr, mu, w0, a0, wa_hi, wa_lo, g2.astype(BF16), k_k, k_a, r_k, ones, tri, csel)


def _mm(a, b):
    return jnp.dot(a.astype(BF16), b.astype(BF16), preferred_element_type=F32)


def _mm_nt(a, b):
    return lax.dot_general(a.astype(BF16), b.astype(BF16), (((1,), (1,)), ((), ())),
                           preferred_element_type=F32)


def _mm_tn(a, b):
    return lax.dot_general(a.astype(BF16), b.astype(BF16), (((0,), (0,)), ((), ())),
                           preferred_element_type=F32)


def _scan_kernel(at_ref, bt_ref, kt_ref, rt_ref, v_ref, bts_ref, kts_ref, bonus_ref, g_ref, pc_ref,
                 gng_ref, gnb_ref, ones_ref, o_ref, s_ref, y_ref, *, ts, gb):
    @pl.when(pl.program_id(1) == 0)
    def _():
        s_ref[...] = jnp.zeros_like(s_ref)

    row_q = lax.broadcasted_iota(jnp.int32, (QUAD, QUAD), 0)
    col_q = lax.broadcasted_iota(jnp.int32, (QUAD, QUAD), 1)
    bdmask = (row_q // HEAD) == (col_q // HEAD)
    row_c = lax.broadcasted_iota(jnp.int32, (CHUNK, QUAD), 0)
    col_c = lax.broadcasted_iota(jnp.int32, (CHUNK, QUAD), 1) % CHUNK
    strict = col_c < row_c
    incl = col_c <= row_c
    eye = jnp.where(col_c == row_c, 1.0, 0.0).astype(F32)

    def bd(x):
        xb = x.astype(BF16)
        return jnp.where(bdmask, jnp.concatenate([xb] * (QUAD // CHUNK), axis=0), jnp.zeros((), BF16))

    chains = [(b, qd) for b in range(gb) for qd in range(N_QUADS)]

    def chunk(c, carry):
        sl = pl.ds(pl.multiple_of(c * CHUNK, CHUNK), CHUNK)
        lanes = [slice(qd * QUAD, (qd + 1) * QUAD) for _, qd in chains]
        ld = lambda ref: [ref[b, sl, ln] for (b, _), ln in zip(chains, lanes)]
        at, bt, kt, rt, v, bts, kts = (ld(r) for r in (at_ref, bt_ref, kt_ref, rt_ref, v_ref, bts_ref, kts_ref))
        each = range(len(chains))
        ar = [jnp.concatenate([at[i], rt[i]], axis=0) for i in each]
        pb = [_mm_nt(ar[i], bd(bt[i])) for i in each]
        pk = [_mm_nt(ar[i], bd(kt[i])) for i in each]
        l_ab = [jnp.where(strict, p[:CHUNK], 0.0) for p in pb]
        a_rb = [jnp.where(incl, p[CHUNK:], 0.0) for p in pb]
        l_ak = [jnp.where(strict, p[:CHUNK], 0.0) for p in pk]
        a_rk = [jnp.where(incl, p[CHUNK:], 0.0) for p in pk]
        tm = [eye + l for l in l_ab]
        lp = [_mm(l, bd(l)) for l in l_ab]
        for _ in range(4):
            both = [_mm(jnp.concatenate([tm[i], lp[i]], axis=0), bd(lp[i])) for i in each]
            tm = [tm[i] + both[i][:CHUNK] for i in each]
            lp = [both[i][CHUNK:] for i in each]
        tm = [tm[i] + _mm(tm[i], bd(lp[i])) for i in each]
        bdv = [bd(x) for x in v]
        x0 = [_mm(l_ak[i], bdv[i]) for i in each]
        tu = [_mm(tm[i], jnp.concatenate([bd(x0[i]), bd(at[i])], axis=1)) for i in each]
        u0 = [t[:, :QUAD] for t in tu]
        w = [t[:, QUAD:] for t in tu]
        qy = [_mm(a_rb[i], jnp.concatenate([bd(w[i]), bd(u0[i])], axis=1)) for i in each]
        y0 = [qy[i][:, QUAD:] + _mm(a_rk[i], bdv[i]) for i in each]
        q = [rt[i].astype(F32) + qy[i][:, :QUAD] for i in each]
        s = [s_ref[i] for i in each]
        qw_s = [_mm_nt(jnp.concatenate([q[i], w[i]], axis=0), s[i]) for i in each]
        for i, ((b, _), ln) in enumerate(zip(chains, lanes)):
            y_ref[b, sl, ln] = y0[i] + qw_s[i][:CHUNK]
        u = [(u0[i] + qw_s[i][CHUNK:]).astype(BF16) for i in each]
        n_p = [jnp.where(bdmask, _mm_tn(jnp.concatenate([u[i], v[i]], axis=0),
                                        jnp.concatenate([bts[i], kts[i]], axis=0)), 0.0) for i in each]
        for i, ((b, _), ln) in enumerate(zip(chains, lanes)):
            s_ref[i] = s[i] * pc_ref[b, pl.ds(c, 1), ln] + n_p[i]
        return carry

    lax.fori_loop(0, ts // CHUNK, chunk, 0)

    ones = ones_ref[...]
    for b in range(gb):
        for qd in range(N_QUADS):
            ln = slice(qd * QUAD, (qd + 1) * QUAD)
            y = y_ref[b, :, ln]
            dlt = y - _dot_exact_rhs(y, ones) * (1.0 / HEAD)
            var = _dot_exact_rhs(dlt * dlt, ones) * (1.0 / HEAD)
            yn = dlt * lax.rsqrt(var + GN_EPS) * gng_ref[:, ln] + gnb_ref[:, ln]
            o_ref[b, :, ln] = ((yn + bonus_ref[b, :, ln].astype(F32)) * g_ref[b, :, ln].astype(F32)).astype(o_ref.dtype)


def _scan(prep_out, gn_g, gn_b, nb, seq, ts=128, gb=8):
    n = prep_out[0].shape[0]
    rw = RWKV_WIDTH
    cpt = ts // CHUNK
    ones = jnp.kron(jnp.eye(QUAD // HEAD, dtype=F32), jnp.ones((HEAD, HEAD), F32)).astype(BF16)
    rows = pl.BlockSpec((gb, ts, rw), lambda i, t: (i, t, 0))
    vec = pl.BlockSpec((1, rw), lambda i, t: (0, 0))
    args = [a.reshape(nb, seq, rw) for a in prep_out[:9]] + [prep_out[9].reshape(nb, seq // ts, cpt, rw)]
    out = pl.pallas_call(
        functools.partial(_scan_kernel, ts=ts, gb=gb),
        grid=(nb // gb, seq // ts),
        in_specs=[rows] * 9 + [pl.BlockSpec((gb, None, cpt, rw), lambda i, t: (i, t, 0, 0)), vec, vec,
                               pl.BlockSpec((QUAD, QUAD), lambda i, t: (0, 0))],
        out_specs=rows,
        out_shape=jax.ShapeDtypeStruct((nb, seq, rw), BF16),
        scratch_shapes=[pltpu.VMEM((gb * N_QUADS, QUAD, QUAD), F32), pltpu.VMEM((gb, ts, rw), F32)],
        compiler_params=_params("parallel", "arbitrary"),
        name="scan",
    )(*args, gn_g, gn_b, ones)
    return out.reshape(n, rw)


def _merge_kernel(uc_ref, ur_ref, zg_ref, x_ref, pc_ref, pr_ref, wo_ref, gpost_ref, gt_ref,
                  gpre_ref, sc_ref, sh_ref, wrh_ref, wrl_ref, xo_ref, h_ref, lg_ref):
    yc = _dot(uc_ref[...], pc_ref[...])
    yr = _dot(ur_ref[...], pr_ref[...])
    zg = zg_ref[...].astype(F32)
    m = _sigmoid(zg[:, :D_MODEL]) * yc + _sigmoid(zg[:, D_MODEL:]) * yr
    y = _dot(m.astype(BF16), wo_ref[...])
    xn = x_ref[...] + gt_ref[...] * _rms(y, gpost_ref[...])
    xo_ref[...] = xn
    h = _rms(xn, gpre_ref[...]) * (1.0 + sc_ref[...]) + sh_ref[...]
    h_ref[...] = _pack_rows(h)
    h_hi, h_lo = _split(h)
    nt = lambda a, b: lax.dot_general(a, b, (((1,), (1,)), ((), ())), preferred_element_type=F32)
    w_hi = wrh_ref[...]
    lg_ref[...] = nt(w_hi, h_hi) + nt(w_hi, h_lo) + nt(wrl_ref[...], h_hi)


def _merge(uc, ur, zg, x2, p_conv_b, p_rwkv_b, w_o_b, g_post, g_pre, mod, layer, w_router_t, seq, tm=512):
    n, d = x2.shape
    row = lambda i: (i, 0)
    full = lambda shape: pl.BlockSpec(shape, lambda i: (0, 0))
    return pl.pallas_call(
        _merge_kernel,
        grid=(n // tm,),
        in_specs=[
            pl.BlockSpec((tm, CONV_WIDTH), row), pl.BlockSpec((tm, RWKV_WIDTH), row),
            pl.BlockSpec((tm, COL_GATE), row), pl.BlockSpec((tm, d), row),
            full((CONV_WIDTH, d)), full((RWKV_WIDTH, d)), full((d, d)),
            full((1, d)), _mod_spec(layer, 2, seq, tm),
            full((1, d)), _mod_spec(layer, 4, seq, tm), _mod_spec(layer, 3, seq, tm),
            full((N_EXPERTS, d)), full((N_EXPERTS, d)),
        ],
        out_specs=[pl.BlockSpec((tm, d), row), pl.BlockSpec((tm, ROW_WORDS), row),
                   pl.BlockSpec((N_EXPERTS, tm), lambda i: (0, i))],
        out_shape=[jax.ShapeDtypeStruct((n, d), F32), jax.ShapeDtypeStruct((n, ROW_WORDS), jnp.int32),
                   jax.ShapeDtypeStruct((N_EXPERTS, n), F32)],
        compiler_params=_params("parallel"),
        name="merge",
    )(uc, ur, zg, x2, p_conv_b, p_rwkv_b, w_o_b, g_post, mod, g_pre, mod, mod, *_split(w_router_t))


def _route_kernel(lg_ref, b_ref, gate_ref, sel_ref, cnt_ref):
    s = _sigmoid(lg_ref[...])
    biased = s + b_ref[...]
    t = s.shape[1]
    member = lax.broadcasted_iota(jnp.int32, (GROUP_SIZE, t), 0)
    grp = []
    for g in range(N_GROUPS):
        bg = biased[g * GROUP_SIZE:(g + 1) * GROUP_SIZE, :]
        m1 = jnp.max(bg, axis=0, keepdims=True)
        first = jnp.min(jnp.where(bg == m1, member, GROUP_SIZE), axis=0, keepdims=True)
        m2 = jnp.max(jnp.where(member == first, -jnp.inf, bg), axis=0, keepdims=True)
        grp.append(m1 + m2)
    masked = []
    for g in range(N_GROUPS):
        rank = jnp.zeros((1, t), jnp.int32)
        for o in range(N_GROUPS):
            if o == g:
                continue
            ahead = (grp[o] > grp[g]) if o > g else (grp[o] >= grp[g])
            rank = rank + jnp.where(ahead, 1, 0)
        keep = rank < TOPK_GROUPS
        masked.append(jnp.where(keep, biased[g * GROUP_SIZE:(g + 1) * GROUP_SIZE, :], -jnp.inf))
    masked = jnp.concatenate(masked, axis=0)
    eidx = lax.broadcasted_iota(jnp.int32, masked.shape, 0)
    mask = jnp.zeros(masked.shape, F32)
    for _ in range(TOP_K):
        top = jnp.max(masked, axis=0, keepdims=True)
        hit = eidx == jnp.min(jnp.where(masked == top, eidx, N_EXPERTS), axis=0, keepdims=True)
        mask = jnp.where(hit, 1.0, mask)
        masked = jnp.where(hit, -jnp.inf, masked)
    sel = s * mask
    gate_ref[...] = sel / jnp.sum(sel, axis=0, keepdims=True) * ROUTED_SCALE
    sel_ref[...] = mask.astype(sel_ref.dtype)

    @pl.when(pl.program_id(0) == 0)
    def _():
        cnt_ref[...] = jnp.zeros_like(cnt_ref)

    cnt_ref[...] += jnp.sum(mask, axis=1, keepdims=True)


def _route(logits_t, b_router, tt=512):
    e, n = logits_t.shape
    return pl.pallas_call(
        _route_kernel,
        grid=(n // tt,),
        in_specs=[pl.BlockSpec((e, tt), lambda i: (0, i)), pl.BlockSpec((e, 1), lambda i: (0, 0))],
        out_specs=[pl.BlockSpec((e, tt), lambda i: (0, i)), pl.BlockSpec((e, tt), lambda i: (0, i)),
                   pl.BlockSpec((e, GATE_LANES), lambda i: (0, 0))],
        out_shape=[jax.ShapeDtypeStruct((e, n), F32), jax.ShapeDtypeStruct((e, n), BF16),
                   jax.ShapeDtypeStruct((e, GATE_LANES), F32)],
        compiler_params=_params("arbitrary"),
        name="route",
    )(logits_t, b_router)


def _dest_kernel(sel_ref, gate_ref, start_ref, triu_ref, below_ref, dest_ref, g8_ref, run_ref):
    @pl.when(pl.program_id(0) == 0)
    def _():
        run_ref[...] = jnp.zeros_like(run_ref)

    sel = sel_ref[...]
    t = sel.shape[1]
    self32 = sel.astype(F32)
    incl = _dot(sel, triu_ref[...])
    pos = start_ref[...] + run_ref[...] + incl - self32
    run_ref[...] += incl[:, t - 1:t]
    slot = _dot(below_ref[...], sel)
    gate = gate_ref[...]
    dst, gts = [], []
    for k in range(TOP_K):
        mine = (self32 > 0.0) & (slot == float(k))
        dst.append(jnp.sum(jnp.where(mine, pos, 0.0), axis=0, keepdims=True))
        gts.append(jnp.sum(jnp.where(mine, gate, 0.0), axis=0, keepdims=True))
    dest_ref[...] = jnp.concatenate(dst, axis=0).astype(jnp.int32)
    g8_ref[...] = jnp.concatenate(gts + [jnp.zeros((GATE_LANES - TOP_K, t), F32)], axis=0).T


def _dest(sel, gate, seg_start, tt=512):
    e, n = sel.shape
    triu = jnp.triu(jnp.ones((tt, tt), F32)).astype(BF16)
    below = jnp.tril(jnp.ones((e, e), F32), -1).astype(BF16)
    return pl.pallas_call(
        _dest_kernel,
        grid=(n // tt,),
        in_specs=[pl.BlockSpec((e, tt), lambda i: (0, i)), pl.BlockSpec((e, tt), lambda i: (0, i)),
                  pl.BlockSpec((e, 1), lambda i: (0, 0)), pl.BlockSpec((tt, tt), lambda i: (0, 0)),
                  pl.BlockSpec((e, e), lambda i: (0, 0))],
        out_specs=[pl.BlockSpec((TOP_K, tt), lambda i: (0, i)), pl.BlockSpec((tt, GATE_LANES), lambda i: (i, 0))],
        out_shape=[jax.ShapeDtypeStruct((TOP_K, n), jnp.int32), jax.ShapeDtypeStruct((n, GATE_LANES), F32)],
        scratch_shapes=[pltpu.VMEM((e, 1), F32)],
        compiler_params=_params("arbitrary"),
        name="dest",
    )(sel, gate, seg_start, triu, below)


def _sc_mesh():
    return plsc.VectorSubcoreMesh(core_axis_name="c", subcore_axis_name="s",
                                  num_cores=SC_CORES, num_subcores=SC_SUBCORES)


def _worker_id():
    return lax.axis_index("s") * SC_CORES + lax.axis_index("c")


def _sc_scatter_rows(rows, dest, n_out):
    n, w = rows.shape
    per_worker = n // SC_WORKERS
    chunks_per_worker = per_worker // SC_CHUNK
    idx_rows = dest.reshape(TOP_K, n // SC_CHUNK, SC_CHUNK).transpose(1, 0, 2).reshape(-1, SC_CHUNK)

    def body(rows_hbm, idx_hbm, out_hbm, idx_v, rows_v):
        wid = _worker_id()

        @pl.loop(0, chunks_per_worker)
        def _(c):
            chunk = wid * chunks_per_worker + c
            pltpu.sync_copy(rows_hbm.at[pl.ds(chunk * SC_CHUNK, SC_CHUNK)], rows_v)
            pltpu.sync_copy(idx_hbm.at[pl.ds(chunk * TOP_K, TOP_K)], idx_v)
            for k in range(TOP_K):
                pltpu.sync_copy(rows_v, out_hbm.at[idx_v.at[k]])

    return pl.kernel(
        body, out_type=jax.ShapeDtypeStruct((n_out, w), rows.dtype), mesh=_sc_mesh(),
        scratch_types=[pltpu.VMEM((TOP_K, SC_CHUNK), jnp.int32), pltpu.VMEM((SC_CHUNK, w), rows.dtype)],
        name="sc_scatter",
    )(rows, idx_rows)


def _sc_gather_rows(table, idx_flat):
    m = idx_flat.shape[0]
    w = table.shape[1]
    chunks_per_worker = m // SC_WORKERS // SC_CHUNK
    idx_rows = idx_flat.reshape(-1, SC_CHUNK)

    def body(table_hbm, idx_hbm, out_hbm, idx_v, rows_v):
        wid = _worker_id()

        @pl.loop(0, chunks_per_worker // SC_IDX_ROWS)
        def _(g):
            first = wid * chunks_per_worker + g * SC_IDX_ROWS
            pltpu.sync_copy(idx_hbm.at[pl.ds(first, SC_IDX_ROWS)], idx_v)
            for j in range(SC_IDX_ROWS):
                pltpu.sync_copy(table_hbm.at[idx_v.at[j]], rows_v)
                pltpu.sync_copy(rows_v, out_hbm.at[pl.ds((first + j) * SC_CHUNK, SC_CHUNK)])

    return pl.kernel(
        body, out_type=jax.ShapeDtypeStruct((m, w), table.dtype), mesh=_sc_mesh(),
        scratch_types=[pltpu.VMEM((SC_IDX_ROWS, SC_CHUNK), jnp.int32), pltpu.VMEM((SC_CHUNK, w), table.dtype)],
        name="sc_gather",
    )(table, idx_rows)


def _swiglu(h, wg, wu):
    gte = _dot(h, wg)
    return gte * _sigmoid(gte) * _dot(h, wu)


def _experts_kernel(be_ref, va_ref, xs_ref, *refs):
    nb = BLOCKS_PER_STEP
    wg32, wu32, wd32, ys_ref = refs[:nb], refs[nb:2 * nb], refs[2 * nb:3 * nb], refs[3 * nb]
    wg, wu, wd = (refs[3 * nb + 1 + i * nb:3 * nb + 1 + (i + 1) * nb] for i in range(3))
    step = pl.program_id(0)
    first = step * nb
    live = va_ref[first] > 0

    for b in range(nb):
        changed = jnp.logical_or(step == 0, be_ref[first + b] != be_ref[jnp.maximum(first + b - nb, 0)])

        @pl.when(jnp.logical_and(live, changed))
        def _(b=b):
            wg[b][...] = wg32[b][...].astype(BF16)
            wu[b][...] = wu32[b][...].astype(BF16)
            wd[b][...] = wd32[b][...].astype(BF16)

    @pl.when(live)
    def _():
        row = lax.broadcasted_iota(jnp.int32, (EXPERT_SUB, D_MODEL), 0)
        subs = [(b, r0) for b in range(nb) for r0 in range(0, EXPERT_BLOCK, EXPERT_SUB)]
        x = []
        for b, r0 in subs:
            lo = b * EXPERT_BLOCK + r0
            xi = _unpack_rows(xs_ref[lo:lo + EXPERT_SUB, :])
            x.append(jnp.where(row < va_ref[first + b] - r0, xi, 0.0).astype(BF16))
        gte = [_dot(xi, wg[b][...]) for xi, (b, _) in zip(x, subs)]
        up = [_dot(xi, wu[b][...]) for xi, (b, _) in zip(x, subs)]
        act = [(g * _sigmoid(g) * u).astype(BF16) for g, u in zip(gte, up)]
        y = [_dot(a, wd[b][...]) for a, (b, _) in zip(act, subs)]
        for yi, (b, r0) in zip(y, subs):
            lo = b * EXPERT_BLOCK + r0
            ys_ref[lo:lo + EXPERT_SUB, :] = _pack_rows(yi)


def _experts(xs, blk_expert, blk_valid, wg, wu, wd, layer):
    n_rows, w = xs.shape
    d = wg.shape[2]
    nb = BLOCKS_PER_STEP
    rows = pl.BlockSpec((nb * EXPERT_BLOCK, w), lambda s, be, va: (s, 0))
    up_spec = lambda i: pl.BlockSpec((None, None, d, D_EXPERT), lambda s, be, va: (layer, be[s * nb + i], 0, 0))
    down_spec = lambda i: pl.BlockSpec((None, None, D_EXPERT, d), lambda s, be, va: (layer, be[s * nb + i], 0, 0))
    return pl.pallas_call(
        _experts_kernel,
        grid_spec=pltpu.PrefetchScalarGridSpec(
            num_scalar_prefetch=2,
            grid=(n_rows // (nb * EXPERT_BLOCK),),
            in_specs=[rows] + [up_spec(i) for i in range(nb)] * 2 + [down_spec(i) for i in range(nb)],
            out_specs=rows,
            scratch_shapes=([pltpu.VMEM((d, D_EXPERT), BF16)] * (2 * nb) + [pltpu.VMEM((D_EXPERT, d), BF16)] * nb),
        ),
        out_shape=jax.ShapeDtypeStruct((n_rows, w), xs.dtype),
        compiler_params=_params("arbitrary"),
        name="experts",
    )(blk_expert, blk_valid, xs, *([wg] * nb), *([wu] * nb), *([wd] * nb))


def _combine_kernel(yg_ref, g8_ref, hp_ref, sg_ref, su_ref, sd_ref, x_ref, gpost_ref, gt_ref, o_ref):
    h = _unpack_rows(hp_ref[...]).astype(BF16)
    acc = _dot(_swiglu(h, sg_ref[...], su_ref[...]).astype(BF16), sd_ref[...])
    g8 = g8_ref[...]
    for k in range(TOP_K):
        acc = acc + g8[:, k:k + 1] * _unpack_rows(yg_ref[k])
    o_ref[...] = x_ref[...] + gt_ref[...] * _rms(acc, gpost_ref[...])


def _combine(yg, g8, hp, sg, su, sd, x2, g_post, mod, layer, seq, tm=512):
    n, d = x2.shape
    row = lambda i: (i, 0)
    full = lambda shape: pl.BlockSpec(shape, lambda i: (0, 0))
    in_specs = [
        pl.BlockSpec((TOP_K, tm, ROW_WORDS), lambda i: (0, i, 0)),
        pl.BlockSpec((tm, GATE_LANES), row),
        pl.BlockSpec((tm, ROW_WORDS), row),
        full((d, D_EXPERT)), full((d, D_EXPERT)), full((D_EXPERT, d)),
        pl.BlockSpec((tm, d), row),
        full((1, d)),
        _mod_spec(layer, 5, seq, tm),
    ]
    return pl.pallas_call(
        _combine_kernel,
        grid=(n // tm,),
        in_specs=in_specs,
        out_specs=pl.BlockSpec((tm, d), row),
        out_shape=jax.ShapeDtypeStruct((n, d), F32),
        compiler_params=_params("parallel"),
        name="combine",
    )(yg, g8, hp, sg, su, sd, x2, g_post, mod)


def _moe(hp, logits_t, b_router, wg, wu, wd, sg, su, sd, x2, g_post, mod, layer, seq):
    n = x2.shape[0]
    gate, sel, counts = _route(logits_t, b_router)
    cnt = counts[:, 0].astype(jnp.int32)
    padded = (cnt + EXPERT_BLOCK - 1) // EXPERT_BLOCK * EXPERT_BLOCK
    seg_end = jnp.cumsum(padded)
    seg_start = seg_end - padded
    n_blocks = n * TOP_K // EXPERT_BLOCK + N_EXPERTS
    blk_row = jnp.arange(n_blocks, dtype=jnp.int32) * EXPERT_BLOCK
    blk_expert = jnp.sum((seg_end[None, :] <= blk_row[:, None]).astype(jnp.int32), axis=1)
    blk_expert = jnp.minimum(blk_expert, N_EXPERTS - 1)
    mine = blk_expert[:, None] == jnp.arange(N_EXPERTS, dtype=jnp.int32)[None, :]
    seg_last = jnp.sum(jnp.where(mine, (seg_start + cnt)[None, :], 0), axis=1)
    blk_valid = jnp.clip(seg_last - blk_row, 0, EXPERT_BLOCK).astype(jnp.int32)
    dest, g8 = _dest(sel, gate, seg_start.astype(F32).reshape(-1, 1))
    dest_flat = dest.reshape(-1)
    xs = _sc_scatter_rows(hp, dest, n_blocks * EXPERT_BLOCK)
    ys = _experts(xs, blk_expert, blk_valid, wg, wu, wd, layer)
    yg = _sc_gather_rows(ys, dest_flat).reshape(TOP_K, n, ROW_WORDS)
    return _combine(yg, g8, hp, sg, su, sd, x2, g_post, mod, layer, seq)


def kernel(x, c, w_ada, b_ada, norm_mix_pre, norm_mix_post, norm_ffn_pre, norm_ffn_post, w_in, mu_shift, conv_w, conv_b, conv_ln_g, conv_ln_b, p_conv, w0, w2, a0, a2, g2, k_k, k_a, r_k, gn_g, gn_b, p_rwkv, w_o, w_router, b_router, we_gate, we_up, we_down, ws_gate, ws_up, ws_down):
    nb, seq, d = x.shape
    depth = w_ada.shape[0]
    n = nb * seq
    rw = RWKV_WIDTH
    mod = _ada(c, w_ada, b_ada)
    x2 = x.reshape(n, d)
    row = lambda a: a.reshape(1, -1)
    bf = lambda a: a.astype(BF16)
    for l in range(depth):
        zc, zr, zg = _win(x2, row(norm_mix_pre[l]), mod, l, bf(w_in[l]), seq)
        uc = _conv(zc, conv_w[l], row(conv_b[l]), row(conv_ln_g[l]), row(conv_ln_b[l]), nb, seq)
        wa = jnp.zeros((LORA_W + LORA_A, 2 * rw), F32)
        wa = wa.at[:LORA_W, :rw].set(w2[l]).at[LORA_W:, rw:].set(a2[l])
        prep_out = _prep(zr, row(mu_shift[l]), row(w0[l]), row(a0[l]), wa, g2[l], row(k_k[l]),
                         row(k_a[l]), row(r_k[l]), nb, seq)
        ur = _scan(prep_out, row(gn_g[l]), row(gn_b[l]), nb, seq)
        x2, hp, logits_t = _merge(uc, ur, zg, x2, bf(p_conv[l]), bf(p_rwkv[l]), bf(w_o[l]),
                                  row(norm_mix_post[l]), row(norm_ffn_pre[l]), mod, l, w_router[l].T, seq)
        x2 = _moe(hp, logits_t, b_router[l].reshape(-1, 1), we_gate, we_up, we_down, bf(ws_gate[l]), bf(ws_up[l]),
                  bf(ws_down[l]), x2, row(norm_ffn_post[l]), mod, l, seq)
    return x2.reshape(nb, seq, d)
```

```python
import functools

import jax
import jax.numpy as jnp
from jax import lax
from jax.experimental import pallas as pl
from jax.experimental.pallas import tpu as pltpu
from jax.experimental.pallas import tpu_sc as plsc

F32 = jnp.float32
BF16 = jnp.bfloat16
HI = lax.Precision.HIGHEST

D_MODEL = 1024
CONV_WIDTH = 512
CONV_KERNEL = 31
RWKV_WIDTH = 512
HEAD = 64
N_HEADS = RWKV_WIDTH // HEAD
LORA_W = 64
LORA_A = 64
LORA_G = 128
N_EXPERTS = 64
TOP_K = 8
N_GROUPS = 8
TOPK_GROUPS = 4
GROUP_SIZE = N_EXPERTS // N_GROUPS
D_EXPERT = 256
ROUTED_SCALE = 2.5
RMS_EPS = 1e-6
LN_EPS = 1e-5
GN_EPS = 64e-5
COL_CONV = 2 * CONV_WIDTH
COL_RWKV = 3 * RWKV_WIDTH + LORA_W + LORA_A + LORA_G
COL_GATE = 2 * D_MODEL
D_IN = COL_CONV + COL_RWKV + COL_GATE

CHUNK = 64
QUAD = 4 * HEAD
N_QUADS = RWKV_WIDTH // QUAD
CUM_BLOCK = 256
CONV_HALO = 32
SUBLANES = 8
SHIFT_HALO = SUBLANES
GATE_LANES = 128
EXPERT_BLOCK = 512
EXPERT_SUB = 256
BLOCKS_PER_STEP = 2
ROW_WORDS = D_MODEL // 2
SC_CORES = 2
SC_SUBCORES = 16
SC_WORKERS = SC_CORES * SC_SUBCORES
SC_CHUNK = 128
SC_IDX_ROWS = 8
VMEM_LIMIT = 52 * 1024 * 1024


def _params(*sem):
    return pltpu.CompilerParams(dimension_semantics=sem, vmem_limit_bytes=VMEM_LIMIT)


def _sigmoid(x):
    return 1.0 / (1.0 + jnp.exp(-x))


def _softplus(x):
    return jnp.maximum(x, 0.0) + jnp.log(1.0 + jnp.exp(-jnp.abs(x)))


def _rms(x, g):
    return x * lax.rsqrt(jnp.mean(x * x, axis=-1, keepdims=True) + RMS_EPS) * g


def _split(x):
    hi = x.astype(BF16)
    return hi, (x - hi.astype(F32)).astype(BF16)


def _dot(a, b):
    return jnp.dot(a, b, preferred_element_type=F32)


def _dot_exact_rhs(x, w):
    hi, lo = _split(x)
    return _dot(hi, w) + _dot(lo, w)


def _pack_rows(y):
    half = y.shape[1] // 2
    packed = pltpu.pack_elementwise([y[:, half:], y[:, :half]], packed_dtype=BF16)
    return lax.bitcast_convert_type(packed, jnp.int32)


def _unpack_rows(u):
    hi = pltpu.unpack_elementwise(u, index=1, packed_dtype=BF16, unpacked_dtype=F32)
    lo = pltpu.unpack_elementwise(u, index=0, packed_dtype=BF16, unpacked_dtype=F32)
    return jnp.concatenate([hi, lo], axis=1)


def _dot_exact_lhs(w, x):
    hi, lo = _split(x)
    return _dot(w, hi) + _dot(w, lo)


def _ada_kernel(c_ref, w_ref, b_ref, o_ref):
    c = c_ref[...]
    o_ref[...] = jnp.dot(c * _sigmoid(c), w_ref[...], precision=HI,
                         preferred_element_type=F32) + b_ref[...]


def _ada(c, w_ada, b_ada):
    nl, d, _ = w_ada.shape
    nb = c.shape[0]
    out = pl.pallas_call(
        _ada_kernel,
        grid=(nl, 6),
        in_specs=[
            pl.BlockSpec((nb, d), lambda l, k: (0, 0)),
            pl.BlockSpec((None, d, d), lambda l, k: (l, 0, k)),
            pl.BlockSpec((None, None, 1, d), lambda l, k: (l, k, 0, 0)),
        ],
        out_specs=pl.BlockSpec((None, None, nb, d), lambda l, k: (l, k, 0, 0)),
        out_shape=jax.ShapeDtypeStruct((nl, 6, nb, d), F32),
        compiler_params=_params("parallel", "parallel"),
        name="ada",
    )(c, w_ada, b_ada.reshape(nl, 6, 1, d))
    return out.reshape(nl, 6, nb, 1, d)


def _mod_spec(layer, piece, rows_per_batch, tm):
    return pl.BlockSpec((None, None, None, 1, D_MODEL),
                        lambda i, *_: (layer, piece, (i * tm) // rows_per_batch, 0, 0))


def _win_kernel(x_ref, g_ref, sc_ref, sh_ref, w_ref, zc_ref, zr_ref, zg_ref):
    h = _rms(x_ref[...], g_ref[...]) * (1.0 + sc_ref[...]) + sh_ref[...]
    hb = h.astype(BF16)
    zc_ref[...] = _dot(hb, w_ref[:, :COL_CONV]).astype(zc_ref.dtype)
    zr_ref[...] = _dot(hb, w_ref[:, COL_CONV:COL_CONV + COL_RWKV])
    zg_ref[...] = _dot(hb, w_ref[:, COL_CONV + COL_RWKV:]).astype(zg_ref.dtype)


def _win(x2, g, mod, layer, w_in_b, seq, tm=512):
    n, d = x2.shape
    row = lambda i: (i, 0)
    return pl.pallas_call(
        _win_kernel,
        grid=(n // tm,),
        in_specs=[
            pl.BlockSpec((tm, d), row),
            pl.BlockSpec((1, d), lambda i: (0, 0)),
            _mod_spec(layer, 1, seq, tm),
            _mod_spec(layer, 0, seq, tm),
            pl.BlockSpec((d, D_IN), lambda i: (0, 0)),
        ],
        out_specs=[pl.BlockSpec((tm, COL_CONV), row), pl.BlockSpec((tm, COL_RWKV), row),
                   pl.BlockSpec((tm, COL_GATE), row)],
        out_shape=[jax.ShapeDtypeStruct((n, COL_CONV), BF16), jax.ShapeDtypeStruct((n, COL_RWKV), F32),
                   jax.ShapeDtypeStruct((n, COL_GATE), BF16)],
        compiler_params=_params("parallel"),
        name="win",
    )(x2, g, mod, mod, w_in_b)


def _conv_kernel(z_ref, halo_ref, w_ref, cb_ref, g_ref, b_ref, o_ref, ubuf, *, ts):
    j = pl.program_id(1)
    z = z_ref[...].astype(F32)
    zh = halo_ref[...].astype(F32)
    uh = zh[:, :CONV_WIDTH] * _sigmoid(zh[:, CONV_WIDTH:])
    rows = CONV_HALO + ts
    u = jnp.concatenate([jnp.where(j > 0, uh, 0.0),
                         z[:, :CONV_WIDTH] * _sigmoid(z[:, CONV_WIDTH:])], axis=0)
    ubuf[0] = u
    for s in range(1, SUBLANES):
        ubuf[s] = pltpu.roll(u, rows - s, axis=0)
    sub = 64
    first = CONV_HALO - (CONV_KERNEL - 1)
    for r in range(ts // sub):
        acc = jnp.zeros((sub, CONV_WIDTH), F32) + cb_ref[...]
        for k in range(CONV_KERNEL):
            s = (first + k) % SUBLANES
            base = r * sub + first + k - s
            acc = acc + w_ref[k:k + 1, :] * ubuf[s, base:base + sub, :]
        mu = jnp.mean(acc, axis=-1, keepdims=True)
        dlt = acc - mu
        var = jnp.mean(dlt * dlt, axis=-1, keepdims=True)
        y = dlt * lax.rsqrt(var + LN_EPS) * g_ref[...] + b_ref[...]
        o_ref[r * sub:(r + 1) * sub, :] = (y * _sigmoid(y)).astype(o_ref.dtype)


def _conv(zc, conv_w, conv_b, ln_g, ln_b, nb, seq, ts=512):
    n = zc.shape[0]
    tps = seq // ts
    hpt = ts // CONV_HALO
    vec = pl.BlockSpec((1, CONV_WIDTH), lambda b, j: (0, 0))
    return pl.pallas_call(
        functools.partial(_conv_kernel, ts=ts),
        grid=(nb, tps),
        in_specs=[
            pl.BlockSpec((ts, COL_CONV), lambda b, j: (b * tps + j, 0)),
            pl.BlockSpec((CONV_HALO, COL_CONV), lambda b, j: (jnp.maximum((b * tps + j) * hpt - 1, 0), 0)),
            pl.BlockSpec((CONV_KERNEL, CONV_WIDTH), lambda b, j: (0, 0)),
            vec, vec, vec,
        ],
        out_specs=pl.BlockSpec((ts, CONV_WIDTH), lambda b, j: (b * tps + j, 0)),
        out_shape=jax.ShapeDtypeStruct((n, CONV_WIDTH), BF16),
        scratch_shapes=[pltpu.VMEM((SUBLANES, CONV_HALO + ts, CONV_WIDTH), F32)],
        compiler_params=_params("parallel", "parallel"),
        name="conv",
    )(zc, zc, conv_w, conv_b, ln_g, ln_b)


def _prep_kernel(z_ref, halo_ref, mu_ref, w0_ref, a0_ref, wah_ref, wal_ref, g2_ref, kk_ref, ka_ref, rk_ref,
                 ones_ref, tri_ref, csel_ref,
                 at_ref, bt_ref, kt_ref, rt_ref, v_ref, bts_ref, kts_ref, bonus_ref, g_ref, pc_ref):
    j = pl.program_id(1)
    z = z_ref[...]
    ts = z.shape[0]
    last = jnp.where(j > 0, halo_ref[SHIFT_HALO - 1:SHIFT_HALO, :], 0.0)
    row = lax.broadcasted_iota(jnp.int32, z.shape, 0)
    prev = jnp.where(row == 0, last, pltpu.roll(z, 1, axis=0))
    zs = z + (prev - z) * mu_ref[...]
    rw = RWKV_WIDTH
    r = zs[:, :rw]
    k = zs[:, rw:2 * rw]
    v = zs[:, 2 * rw:3 * rw]
    wad = zs[:, 3 * rw:3 * rw + LORA_W + LORA_A]
    gd = zs[:, 3 * rw + LORA_W + LORA_A:]
    lane = lax.broadcasted_iota(jnp.int32, wad.shape, 1)
    xh, xl = _split(jnp.where(lane < LORA_W, jnp.tanh(wad), wad))
    wah = wah_ref[...]
    lora = _dot(xh, wah) + _dot(xl, wah) + _dot(xh, wal_ref[...])
    w = -_softplus(-(w0_ref[...] + lora[:, :rw])) - 0.5
    lw = -jnp.exp(w)
    a = _sigmoid(a0_ref[...] + lora[:, rw:])
    g_ref[...] = _dot(_sigmoid(gd).astype(BF16), g2_ref[...]).astype(g_ref.dtype)
    kk = k * kk_ref[...]
    k2 = k * (1.0 + (a - 1.0) * ka_ref[...])
    ones = ones_ref[...]
    kk = kk * lax.rsqrt(jnp.maximum(_dot_exact_rhs(kk * kk, ones), 1e-24))
    bonus_ref[...] = (_dot_exact_rhs(r * k2 * rk_ref[...], ones) * v).astype(bonus_ref.dtype)
    tri = tri_ref[...]
    lc, tot = [], []
    for h in range(ts // CUM_BLOCK):
        cs = _dot_exact_lhs(tri, lw[h * CUM_BLOCK:(h + 1) * CUM_BLOCK])
        lc.append(cs[:CUM_BLOCK])
        tot.append(cs[CUM_BLOCK:])
    lc = jnp.concatenate(lc, axis=0)
    tot = jnp.concatenate(tot, axis=0)
    e_neg = jnp.exp(-lc)
    e_end = jnp.exp(tot - lc)
    kka = kk * a
    dt = at_ref.dtype
    at_ref[...] = (-kk * jnp.exp(lc - lw)).astype(dt)
    bt_ref[...] = (kka * e_neg).astype(dt)
    kt_ref[...] = (k2 * e_neg).astype(dt)
    rt_ref[...] = (r * jnp.exp(lc)).astype(dt)
    v_ref[...] = v.astype(dt)
    bts_ref[...] = (kka * e_end).astype(dt)
    kts_ref[...] = (k2 * e_end).astype(dt)
    pc_ref[...] = jnp.exp(_dot_exact_lhs(csel_ref[...], lw))


def _prep(zr, mu, w0, a0, wa, g2, k_k, k_a, r_k, nb, seq, ts=512):
    n = zr.shape[0]
    tps = seq // ts
    hpt = ts // SHIFT_HALO
    rw = RWKV_WIDTH
    cpt = ts // CHUNK
    ones = jnp.kron(jnp.eye(N_HEADS, dtype=F32), jnp.ones((HEAD, HEAD), F32)).astype(BF16)
    blk = jnp.kron(jnp.eye(CUM_BLOCK // CHUNK, dtype=F32), jnp.ones((CHUNK, CHUNK), F32))
    tri = jnp.concatenate([jnp.tril(blk), blk], axis=0).astype(BF16)
    csel = jnp.kron(jnp.eye(cpt, dtype=F32), jnp.ones((1, CHUNK), F32)).astype(BF16)
    wa_hi, wa_lo = _split(wa)
    vec = pl.BlockSpec((1, rw), lambda b, j: (0, 0))
    full = lambda shape: pl.BlockSpec(shape, lambda b, j: (0, 0))
    rows = pl.BlockSpec((ts, rw), lambda b, j: (b * tps + j, 0))
    return pl.pallas_call(
        _prep_kernel,
        grid=(nb, tps),
        in_specs=[
            pl.BlockSpec((ts, COL_RWKV), lambda b, j: (b * tps + j, 0)),
            pl.BlockSpec((SHIFT_HALO, COL_RWKV), lambda b, j: (jnp.maximum((b * tps + j) * hpt - 1, 0), 0)),
            full((1, COL_RWKV)), vec, vec,
            full((LORA_W + LORA_A, 2 * rw)), full((LORA_W + LORA_A, 2 * rw)), full((LORA_G, rw)),
            vec, vec, vec, full((rw, rw)), full((2 * CUM_BLOCK, CUM_BLOCK)), full((cpt, ts)),
        ],
        out_specs=[rows] * 9 + [pl.BlockSpec((cpt, rw), lambda b, j: (b * tps + j, 0))],
        out_shape=[jax.ShapeDtypeStruct((n, rw), BF16)] * 9 + [jax.ShapeDtypeStruct((n // CHUNK, rw), F32)],
        compiler_params=_params("parallel", "parallel"),
        name="prep",
    )(zr, zr, mu, w0, a0, wa_hi, wa_lo, g2.astype(BF16), k_k, k_a, r_k, ones, tri, csel)


def _mm(a, b):
    return jnp.dot(a.astype(BF16), b.astype(BF16), preferred_element_type=F32)


def _mm_nt(a, b):
    return lax.dot_general(a.astype(BF16), b.astype(BF16), (((1,), (1,)), ((), ())),
                           preferred_element_type=F32)


def _mm_tn(a, b):
    return lax.dot_general(a.astype(BF16), b.astype(BF16), (((0,), (0,)), ((), ())),
                           preferred_element_type=F32)


def _scan_kernel(at_ref, bt_ref, kt_ref, rt_ref, v_ref, bts_ref, kts_ref, bonus_ref, g_ref, pc_ref,
                 gng_ref, gnb_ref, ones_ref, o_ref, s_ref, y_ref, *, ts, gb):
    @pl.when(pl.program_id(1) == 0)
    def _():
        s_ref[...] = jnp.zeros_like(s_ref)

    row_q = lax.broadcasted_iota(jnp.int32, (QUAD, QUAD), 0)
    col_q = lax.broadcasted_iota(jnp.int32, (QUAD, QUAD), 1)
    bdmask = (row_q // HEAD) == (col_q // HEAD)
    row_c = lax.broadcasted_iota(jnp.int32, (CHUNK, QUAD), 0)
    col_c = lax.broadcasted_iota(jnp.int32, (CHUNK, QUAD), 1) % CHUNK
    strict = col_c < row_c
    incl = col_c <= row_c
    eye = jnp.where(col_c == row_c, 1.0, 0.0).astype(F32)

    def bd(x):
        xb = x.astype(BF16)
        return jnp.where(bdmask, jnp.concatenate([xb] * (QUAD // CHUNK), axis=0), jnp.zeros((), BF16))

    chains = [(b, qd) for b in range(gb) for qd in range(N_QUADS)]

    def chunk(c, carry):
        sl = pl.ds(pl.multiple_of(c * CHUNK, CHUNK), CHUNK)
        lanes = [slice(qd * QUAD, (qd + 1) * QUAD) for _, qd in chains]
        ld = lambda ref: [ref[b, sl, ln] for (b, _), ln in zip(chains, lanes)]
        at, bt, kt, rt, v, bts, kts = (ld(r) for r in (at_ref, bt_ref, kt_ref, rt_ref, v_ref, bts_ref, kts_ref))
        each = range(len(chains))
        ar = [jnp.concatenate([at[i], rt[i]], axis=0) for i in each]
        pb = [_mm_nt(ar[i], bd(bt[i])) for i in each]
        pk = [_mm_nt(ar[i], bd(kt[i])) for i in each]
        l_ab = [jnp.where(strict, p[:CHUNK], 0.0) for p in pb]
        a_rb = [jnp.where(incl, p[CHUNK:], 0.0) for p in pb]
        l_ak = [jnp.where(strict, p[:CHUNK], 0.0) for p in pk]
        a_rk = [jnp.where(incl, p[CHUNK:], 0.0) for p in pk]
        tm = [eye + l for l in l_ab]
        lp = [_mm(l, bd(l)) for l in l_ab]
        for _ in range(4):
            both = [_mm(jnp.concatenate([tm[i], lp[i]], axis=0), bd(lp[i])) for i in each]
            tm = [tm[i] + both[i][:CHUNK] for i in each]
            lp = [both[i][CHUNK:] for i in each]
        tm = [tm[i] + _mm(tm[i], bd(lp[i])) for i in each]
        bdv = [bd(x) for x in v]
        x0 = [_mm(l_ak[i], bdv[i]) for i in each]
        tu = [_mm(tm[i], jnp.concatenate([bd(x0[i]), bd(at[i])], axis=1)) for i in each]
        u0 = [t[:, :QUAD] for t in tu]
        w = [t[:, QUAD:] for t in tu]
        qy = [_mm(a_rb[i], jnp.concatenate([bd(w[i]), bd(u0[i])], axis=1)) for i in each]
        y0 = [qy[i][:, QUAD:] + _mm(a_rk[i], bdv[i]) for i in each]
        q = [rt[i].astype(F32) + qy[i][:, :QUAD] for i in each]
        s = [s_ref[i] for i in each]
        qw_s = [_mm_nt(jnp.concatenate([q[i], w[i]], axis=0), s[i]) for i in each]
        for i, ((b, _), ln) in enumerate(zip(chains, lanes)):
            y_ref[b, sl, ln] = y0[i] + qw_s[i][:CHUNK]
        u = [(u0[i] + qw_s[i][CHUNK:]).astype(BF16) for i in each]
        n_p = [jnp.where(bdmask, _mm_tn(jnp.concatenate([u[i], v[i]], axis=0),
                                        jnp.concatenate([bts[i], kts[i]], axis=0)), 0.0) for i in each]
        for i, ((b, _), ln) in enumerate(zip(chains, lanes)):
            s_ref[i] = s[i] * pc_ref[b, pl.ds(c, 1), ln] + n_p[i]
        return carry

    lax.fori_loop(0, ts // CHUNK, chunk, 0)

    ones = ones_ref[...]
    for b in range(gb):
        for qd in range(N_QUADS):
            ln = slice(qd * QUAD, (qd + 1) * QUAD)
            y = y_ref[b, :, ln]
            dlt = y - _dot_exact_rhs(y, ones) * (1.0 / HEAD)
            var = _dot_exact_rhs(dlt * dlt, ones) * (1.0 / HEAD)
            yn = dlt * lax.rsqrt(var + GN_EPS) * gng_ref[:, ln] + gnb_ref[:, ln]
            o_ref[b, :, ln] = ((yn + bonus_ref[b, :, ln].astype(F32)) * g_ref[b, :, ln].astype(F32)).astype(o_ref.dtype)


def _scan(prep_out, gn_g, gn_b, nb, seq, ts=128, gb=8):
    n = prep_out[0].shape[0]
    rw = RWKV_WIDTH
    cpt = ts // CHUNK
    ones = jnp.kron(jnp.eye(QUAD // HEAD, dtype=F32), jnp.ones((HEAD, HEAD), F32)).astype(BF16)
    rows = pl.BlockSpec((gb, ts, rw), lambda i, t: (i, t, 0))
    vec = pl.BlockSpec((1, rw), lambda i, t: (0, 0))
    args = [a.reshape(nb, seq, rw) for a in prep_out[:9]] + [prep_out[9].reshape(nb, seq // ts, cpt, rw)]
    out = pl.pallas_call(
        functools.partial(_scan_kernel, ts=ts, gb=gb),
        grid=(nb // gb, seq // ts),
        in_specs=[rows] * 9 + [pl.BlockSpec((gb, None, cpt, rw), lambda i, t: (i, t, 0, 0)), vec, vec,
                               pl.BlockSpec((QUAD, QUAD), lambda i, t: (0, 0))],
        out_specs=rows,
        out_shape=jax.ShapeDtypeStruct((nb, seq, rw), BF16),
        scratch_shapes=[pltpu.VMEM((gb * N_QUADS, QUAD, QUAD), F32), pltpu.VMEM((gb, ts, rw), F32)],
        compiler_params=_params("parallel", "arbitrary"),
        name="scan",
    )(*args, gn_g, gn_b, ones)
    return out.reshape(n, rw)


def _merge_kernel(uc_ref, ur_ref, zg_ref, x_ref, pc_ref, pr_ref, wo_ref, gpost_ref, gt_ref,
                  gpre_ref, sc_ref, sh_ref, wrh_ref, wrl_ref, xo_ref, h_ref, lg_ref):
    yc = _dot(uc_ref[...], pc_ref[...])
    yr = _dot(ur_ref[...], pr_ref[...])
    zg = zg_ref[...].astype(F32)
    m = _sigmoid(zg[:, :D_MODEL]) * yc + _sigmoid(zg[:, D_MODEL:]) * yr
    y = _dot(m.astype(BF16), wo_ref[...])
    xn = x_ref[...] + gt_ref[...] * _rms(y, gpost_ref[...])
    xo_ref[...] = xn
    h = _rms(xn, gpre_ref[...]) * (1.0 + sc_ref[...]) + sh_ref[...]
    h_ref[...] = _pack_rows(h)
    h_hi, h_lo = _split(h)
    nt = lambda a, b: lax.dot_general(a, b, (((1,), (1,)), ((), ())), preferred_element_type=F32)
    w_hi = wrh_ref[...]
    lg_ref[...] = nt(w_hi, h_hi) + nt(w_hi, h_lo) + nt(wrl_ref[...], h_hi)


def _merge(uc, ur, zg, x2, p_conv_b, p_rwkv_b, w_o_b, g_post, g_pre, mod, layer, w_router_t, seq, tm=512):
    n, d = x2.shape
    row = lambda i: (i, 0)
    full = lambda shape: pl.BlockSpec(shape, lambda i: (0, 0))
    return pl.pallas_call(
        _merge_kernel,
        grid=(n // tm,),
        in_specs=[
            pl.BlockSpec((tm, CONV_WIDTH), row), pl.BlockSpec((tm, RWKV_WIDTH), row),
            pl.BlockSpec((tm, COL_GATE), row), pl.BlockSpec((tm, d), row),
            full((CONV_WIDTH, d)), full((RWKV_WIDTH, d)), full((d, d)),
            full((1, d)), _mod_spec(layer, 2, seq, tm),
            full((1, d)), _mod_spec(layer, 4, seq, tm), _mod_spec(layer, 3, seq, tm),
            full((N_EXPERTS, d)), full((N_EXPERTS, d)),
        ],
        out_specs=[pl.BlockSpec((tm, d), row), pl.BlockSpec((tm, ROW_WORDS), row),
                   pl.BlockSpec((N_EXPERTS, tm), lambda i: (0, i))],
        out_shape=[jax.ShapeDtypeStruct((n, d), F32), jax.ShapeDtypeStruct((n, ROW_WORDS), jnp.int32),
                   jax.ShapeDtypeStruct((N_EXPERTS, n), F32)],
        compiler_params=_params("parallel"),
        name="merge",
    )(uc, ur, zg, x2, p_conv_b, p_rwkv_b, w_o_b, g_post, mod, g_pre, mod, mod, *_split(w_router_t))


def _route_kernel(lg_ref, b_ref, gate_ref, sel_ref, cnt_ref):
    s = _sigmoid(lg_ref[...])
    biased = s + b_ref[...]
    t = s.shape[1]
    member = lax.broadcasted_iota(jnp.int32, (GROUP_SIZE, t), 0)
    grp = []
    for g in range(N_GROUPS):
        bg = biased[g * GROUP_SIZE:(g + 1) * GROUP_SIZE, :]
        m1 = jnp.max(bg, axis=0, keepdims=True)
        first = jnp.min(jnp.where(bg == m1, member, GROUP_SIZE), axis=0, keepdims=True)
        m2 = jnp.max(jnp.where(member == first, -jnp.inf, bg), axis=0, keepdims=True)
        grp.append(m1 + m2)
    masked = []
    for g in range(N_GROUPS):
        rank = jnp.zeros((1, t), jnp.int32)
        for o in range(N_GROUPS):
            if o == g:
                continue
            ahead = (grp[o] > grp[g]) if o > g else (grp[o] >= grp[g])
            rank = rank + jnp.where(ahead, 1, 0)
        keep = rank < TOPK_GROUPS
        masked.append(jnp.where(keep, biased[g * GROUP_SIZE:(g + 1) * GROUP_SIZE, :], -jnp.inf))
    masked = jnp.concatenate(masked, axis=0)
    eidx = lax.broadcasted_iota(jnp.int32, masked.shape, 0)
    mask = jnp.zeros(masked.shape, F32)
    for _ in range(TOP_K):
        top = jnp.max(masked, axis=0, keepdims=True)
        hit = eidx == jnp.min(jnp.where(masked == top, eidx, N_EXPERTS), axis=0, keepdims=True)
        mask = jnp.where(hit, 1.0, mask)
        masked = jnp.where(hit, -jnp.inf, masked)
    sel = s * mask
    gate_ref[...] = sel / jnp.sum(sel, axis=0, keepdims=True) * ROUTED_SCALE
    sel_ref[...] = mask.astype(sel_ref.dtype)

    @pl.when(pl.program_id(0) == 0)
    def _():
        cnt_ref[...] = jnp.zeros_like(cnt_ref)

    cnt_ref[...] += jnp.sum(mask, axis=1, keepdims=True)


def _route(logits_t, b_router, tt=512):
    e, n = logits_t.shape
    return pl.pallas_call(
        _route_kernel,
        grid=(n // tt,),
        in_specs=[pl.BlockSpec((e, tt), lambda i: (0, i)), pl.BlockSpec((e, 1), lambda i: (0, 0))],
        out_specs=[pl.BlockSpec((e, tt), lambda i: (0, i)), pl.BlockSpec((e, tt), lambda i: (0, i)),
                   pl.BlockSpec((e, GATE_LANES), lambda i: (0, 0))],
        out_shape=[jax.ShapeDtypeStruct((e, n), F32), jax.ShapeDtypeStruct((e, n), BF16),
                   jax.ShapeDtypeStruct((e, GATE_LANES), F32)],
        compiler_params=_params("arbitrary"),
        name="route",
    )(logits_t, b_router)


def _dest_kernel(sel_ref, gate_ref, start_ref, triu_ref, below_ref, dest_ref, g8_ref, run_ref):
    @pl.when(pl.program_id(0) == 0)
    def _():
        run_ref[...] = jnp.zeros_like(run_ref)

    sel = sel_ref[...]
    t = sel.shape[1]
    self32 = sel.astype(F32)
    incl = _dot(sel, triu_ref[...])
    pos = start_ref[...] + run_ref[...] + incl - self32
    run_ref[...] += incl[:, t - 1:t]
    slot = _dot(below_ref[...], sel)
    gate = gate_ref[...]
    dst, gts = [], []
    for k in range(TOP_K):
        mine = (self32 > 0.0) & (slot == float(k))
        dst.append(jnp.sum(jnp.where(mine, pos, 0.0), axis=0, keepdims=True))
        gts.append(jnp.sum(jnp.where(mine, gate, 0.0), axis=0, keepdims=True))
    dest_ref[...] = jnp.concatenate(dst, axis=0).astype(jnp.int32)
    g8_ref[...] = jnp.concatenate(gts + [jnp.zeros((GATE_LANES - TOP_K, t), F32)], axis=0).T


def _dest(sel, gate, seg_start, tt=512):
    e, n = sel.shape
    triu = jnp.triu(jnp.ones((tt, tt), F32)).astype(BF16)
    below = jnp.tril(jnp.ones((e, e), F32), -1).astype(BF16)
    return pl.pallas_call(
        _dest_kernel,
        grid=(n // tt,),
        in_specs=[pl.BlockSpec((e, tt), lambda i: (0, i)), pl.BlockSpec((e, tt), lambda i: (0, i)),
                  pl.BlockSpec((e, 1), lambda i: (0, 0)), pl.BlockSpec((tt, tt), lambda i: (0, 0)),
                  pl.BlockSpec((e, e), lambda i: (0, 0))],
        out_specs=[pl.BlockSpec((TOP_K, tt), lambda i: (0, i)), pl.BlockSpec((tt, GATE_LANES), lambda i: (i, 0))],
        out_shape=[jax.ShapeDtypeStruct((TOP_K, n), jnp.int32), jax.ShapeDtypeStruct((n, GATE_LANES), F32)],
        scratch_shapes=[pltpu.VMEM((e, 1), F32)],
        compiler_params=_params("arbitrary"),
        name="dest",
    )(sel, gate, seg_start, triu, below)


def _sc_mesh():
    return plsc.VectorSubcoreMesh(core_axis_name="c", subcore_axis_name="s",
                                  num_cores=SC_CORES, num_subcores=SC_SUBCORES)


def _worker_id():
    return lax.axis_index("s") * SC_CORES + lax.axis_index("c")


def _sc_scatter_rows(rows, dest, n_out):
    n, w = rows.shape
    per_worker = n // SC_WORKERS
    chunks_per_worker = per_worker // SC_CHUNK
    idx_rows = dest.reshape(TOP_K, n // SC_CHUNK, SC_CHUNK).transpose(1, 0, 2).reshape(-1, SC_CHUNK)

    def body(rows_hbm, idx_hbm, out_hbm, idx_v, rows_v):
        wid = _worker_id()

        @pl.loop(0, chunks_per_worker)
        def _(c):
            chunk = wid * chunks_per_worker + c
            pltpu.sync_copy(rows_hbm.at[pl.ds(chunk * SC_CHUNK, SC_CHUNK)], rows_v)
            pltpu.sync_copy(idx_hbm.at[pl.ds(chunk * TOP_K, TOP_K)], idx_v)
            for k in range(TOP_K):
                pltpu.sync_copy(rows_v, out_hbm.at[idx_v.at[k]])

    return pl.kernel(
        body, out_type=jax.ShapeDtypeStruct((n_out, w), rows.dtype), mesh=_sc_mesh(),
        scratch_types=[pltpu.VMEM((TOP_K, SC_CHUNK), jnp.int32), pltpu.VMEM((SC_CHUNK, w), rows.dtype)],
        name="sc_scatter",
    )(rows, idx_rows)


def _sc_gather_rows(table, idx_flat):
    m = idx_flat.shape[0]
    w = table.shape[1]
    ch = SC_CHUNK // 2
    chunks_per_worker = m // SC_WORKERS // ch
    idx_rows = idx_flat.reshape(-1, ch)
    ng = SC_IDX_ROWS

    def body(table_hbm, idx_hbm, out_hbm, idx_v, rows_a, rows_b, gat_a, gat_b, put_a, put_b):
        wid = _worker_id()
        bufs = ((rows_a, gat_a, put_a), (rows_b, gat_b, put_b))

        @pl.loop(0, chunks_per_worker // ng)
        def _(g):
            first = wid * chunks_per_worker + g * ng
            pltpu.sync_copy(idx_hbm.at[pl.ds(first, ng)], idx_v)
            gathers, stores = [None] * ng, [None] * ng
            gathers[0] = pltpu.async_copy(table_hbm.at[idx_v.at[0]], rows_a, gat_a)
            for j in range(ng):
                buf, _, put = bufs[j % 2]
                if j + 1 < ng:
                    if j >= 1:
                        stores[j - 1].wait()
                    nxt, gat, _ = bufs[(j + 1) % 2]
                    gathers[j + 1] = pltpu.async_copy(table_hbm.at[idx_v.at[j + 1]], nxt, gat)
                gathers[j].wait()
                stores[j] = pltpu.async_copy(buf, out_hbm.at[pl.ds((first + j) * ch, ch)], put)
            stores[ng - 2].wait()
            stores[ng - 1].wait()

    return pl.kernel(
        body, out_type=jax.ShapeDtypeStruct((m, w), table.dtype), mesh=_sc_mesh(),
        scratch_types=[pltpu.VMEM((ng, ch), jnp.int32), pltpu.VMEM((ch, w), table.dtype),
                       pltpu.VMEM((ch, w), table.dtype)] + [pltpu.SemaphoreType.DMA] * 4,
        name="sc_gather",
    )(table, idx_rows)


def _swiglu(h, wg, wu):
    gte = _dot(h, wg)
    return gte * _sigmoid(gte) * _dot(h, wu)


def _experts_kernel(be_ref, va_ref, xs_ref, *refs):
    nb = BLOCKS_PER_STEP
    wg32, wu32, wd32, ys_ref = refs[:nb], refs[nb:2 * nb], refs[2 * nb:3 * nb], refs[3 * nb]
    wg, wu, wd = (refs[3 * nb + 1 + i * nb:3 * nb + 1 + (i + 1) * nb] for i in range(3))
    step = pl.program_id(0)
    first = step * nb
    live = va_ref[first] > 0

    for b in range(nb):
        changed = jnp.logical_or(step == 0, be_ref[first + b] != be_ref[jnp.maximum(first + b - nb, 0)])

        @pl.when(jnp.logical_and(live, changed))
        def _(b=b):
            wg[b][...] = wg32[b][...].astype(BF16)
            wu[b][...] = wu32[b][...].astype(BF16)
            wd[b][...] = wd32[b][...].astype(BF16)

    @pl.when(live)
    def _():
        row = lax.broadcasted_iota(jnp.int32, (EXPERT_SUB, D_MODEL), 0)
        subs = [(b, r0) for b in range(nb) for r0 in range(0, EXPERT_BLOCK, EXPERT_SUB)]
        x = []
        for b, r0 in subs:
            lo = b * EXPERT_BLOCK + r0
            xi = _unpack_rows(xs_ref[lo:lo + EXPERT_SUB, :])
            x.append(jnp.where(row < va_ref[first + b] - r0, xi, 0.0).astype(BF16))
        gte = [_dot(xi, wg[b][...]) for xi, (b, _) in zip(x, subs)]
        up = [_dot(xi, wu[b][...]) for xi, (b, _) in zip(x, subs)]
        act = [(g * _sigmoid(g) * u).astype(BF16) for g, u in zip(gte, up)]
        y = [_dot(a, wd[b][...]) for a, (b, _) in zip(act, subs)]
        for yi, (b, r0) in zip(y, subs):
            lo = b * EXPERT_BLOCK + r0
            ys_ref[lo:lo + EXPERT_SUB, :] = _pack_rows(yi)


def _experts(xs, blk_expert, blk_valid, wg, wu, wd, layer):
    n_rows, w = xs.shape
    d = wg.shape[2]
    nb = BLOCKS_PER_STEP
    rows = pl.BlockSpec((nb * EXPERT_BLOCK, w), lambda s, be, va: (s, 0))
    up_spec = lambda i: pl.BlockSpec((None, None, d, D_EXPERT), lambda s, be, va: (layer, be[s * nb + i], 0, 0))
    down_spec = lambda i: pl.BlockSpec((None, None, D_EXPERT, d), lambda s, be, va: (layer, be[s * nb + i], 0, 0))
    return pl.pallas_call(
        _experts_kernel,
        grid_spec=pltpu.PrefetchScalarGridSpec(
            num_scalar_prefetch=2,
            grid=(n_rows // (nb * EXPERT_BLOCK),),
            in_specs=[rows] + [up_spec(i) for i in range(nb)] * 2 + [down_spec(i) for i in range(nb)],
            out_specs=rows,
            scratch_shapes=([pltpu.VMEM((d, D_EXPERT), BF16)] * (2 * nb) + [pltpu.VMEM((D_EXPERT, d), BF16)] * nb),
        ),
        out_shape=jax.ShapeDtypeStruct((n_rows, w), xs.dtype),
        compiler_params=_params("arbitrary"),
        name="experts",
    )(blk_expert, blk_valid, xs, *([wg] * nb), *([wu] * nb), *([wd] * nb))


def _combine_kernel(yg_ref, g8_ref, hp_ref, sg_ref, su_ref, sd_ref, x_ref, gpost_ref, gt_ref, o_ref):
    h = _unpack_rows(hp_ref[...]).astype(BF16)
    acc = _dot(_swiglu(h, sg_ref[...], su_ref[...]).astype(BF16), sd_ref[...])
    g8 = g8_ref[...]
    for k in range(TOP_K):
        acc = acc + g8[:, k:k + 1] * _unpack_rows(yg_ref[k])
    o_ref[...] = x_ref[...] + gt_ref[...] * _rms(acc, gpost_ref[...])


def _combine(yg, g8, hp, sg, su, sd, x2, g_post, mod, layer, seq, tm=512):
    n, d = x2.shape
    row = lambda i: (i, 0)
    full = lambda shape: pl.BlockSpec(shape, lambda i: (0, 0))
    in_specs = [
        pl.BlockSpec((TOP_K, tm, ROW_WORDS), lambda i: (0, i, 0)),
        pl.BlockSpec((tm, GATE_LANES), row),
        pl.BlockSpec((tm, ROW_WORDS), row),
        full((d, D_EXPERT)), full((d, D_EXPERT)), full((D_EXPERT, d)),
        pl.BlockSpec((tm, d), row),
        full((1, d)),
        _mod_spec(layer, 5, seq, tm),
    ]
    return pl.pallas_call(
        _combine_kernel,
        grid=(n // tm,),
        in_specs=in_specs,
        out_specs=pl.BlockSpec((tm, d), row),
        out_shape=jax.ShapeDtypeStruct((n, d), F32),
        compiler_params=_params("parallel"),
        name="combine",
    )(yg, g8, hp, sg, su, sd, x2, g_post, mod)


def _moe(hp, logits_t, b_router, wg, wu, wd, sg, su, sd, x2, g_post, mod, layer, seq):
    n = x2.shape[0]
    gate, sel, counts = _route(logits_t, b_router)
    cnt = counts[:, 0].astype(jnp.int32)
    padded = (cnt + EXPERT_BLOCK - 1) // EXPERT_BLOCK * EXPERT_BLOCK
    seg_end = jnp.cumsum(padded)
    seg_start = seg_end - padded
    n_blocks = n * TOP_K // EXPERT_BLOCK + N_EXPERTS
    blk_row = jnp.arange(n_blocks, dtype=jnp.int32) * EXPERT_BLOCK
    blk_expert = jnp.sum((seg_end[None, :] <= blk_row[:, None]).astype(jnp.int32), axis=1)
    blk_expert = jnp.minimum(blk_expert, N_EXPERTS - 1)
    mine = blk_expert[:, None] == jnp.arange(N_EXPERTS, dtype=jnp.int32)[None, :]
    seg_last = jnp.sum(jnp.where(mine, (seg_start + cnt)[None, :], 0), axis=1)
    blk_valid = jnp.clip(seg_last - blk_row, 0, EXPERT_BLOCK).astype(jnp.int32)
    dest, g8 = _dest(sel, gate, seg_start.astype(F32).reshape(-1, 1))
    dest_flat = dest.reshape(-1)
    xs = _sc_scatter_rows(hp, dest, n_blocks * EXPERT_BLOCK)
    ys = _experts(xs, blk_expert, blk_valid, wg, wu, wd, layer)
    yg = _sc_gather_rows(ys, dest_flat).reshape(TOP_K, n, ROW_WORDS)
    return _combine(yg, g8, hp, sg, su, sd, x2, g_post, mod, layer, seq)


def kernel(x, c, w_ada, b_ada, norm_mix_pre, norm_mix_post, norm_ffn_pre, norm_ffn_post, w_in, mu_shift, conv_w, conv_b, conv_ln_g, conv_ln_b, p_conv, w0, w2, a0, a2, g2, k_k, k_a, r_k, gn_g, gn_b, p_rwkv, w_o, w_router, b_router, we_gate, we_up, we_down, ws_gate, ws_up, ws_down):
    nb, seq, d = x.shape
    depth = w_ada.shape[0]
    n = nb * seq
    rw = RWKV_WIDTH
    mod = _ada(c, w_ada, b_ada)
    x2 = x.reshape(n, d)
    row = lambda a: a.reshape(1, -1)
    bf = lambda a: a.astype(BF16)
    for l in range(depth):
        zc, zr, zg = _win(x2, row(norm_mix_pre[l]), mod, l, bf(w_in[l]), seq)
        uc = _conv(zc, conv_w[l], row(conv_b[l]), row(conv_ln_g[l]), row(conv_ln_b[l]), nb, seq)
        wa = jnp.zeros((LORA_W + LORA_A, 2 * rw), F32)
        wa = wa.at[:LORA_W, :rw].set(w2[l]).at[LORA_W:, rw:].set(a2[l])
        prep_out = _prep(zr, row(mu_shift[l]), row(w0[l]), row(a0[l]), wa, g2[l], row(k_k[l]),
                         row(k_a[l]), row(r_k[l]), nb, seq)
        ur = _scan(prep_out, row(gn_g[l]), row(gn_b[l]), nb, seq)
        x2, hp, logits_t = _merge(uc, ur, zg, x2, bf(p_conv[l]), bf(p_rwkv[l]), bf(w_o[l]),
                                  row(norm_mix_post[l]), row(norm_ffn_pre[l]), mod, l, w_router[l].T, seq)
        x2 = _moe(hp, logits_t, b_router[l].reshape(-1, 1), we_gate, we_up, we_down, bf(ws_gate[l]), bf(ws_up[l]),
                  bf(ws_down[l]), x2, row(norm_ffn_post[l]), mod, l, seq)
    return x2.reshape(nb, seq, d)
```

```python
import functools

import jax
import jax.numpy as jnp
from jax import lax
from jax.experimental import pallas as pl
from jax.experimental.pallas import tpu as pltpu
from jax.experimental.pallas import tpu_sc as plsc

F32 = jnp.float32
BF16 = jnp.bfloat16

D_MODEL = 1024
CONV_WIDTH = 512
CONV_KERNEL = 31
RWKV_WIDTH = 512
HEAD = 64
N_HEADS = RWKV_WIDTH // HEAD
LORA_W = 64
LORA_A = 64
LORA_G = 128
N_EXPERTS = 64
TOP_K = 8
N_GROUPS = 8
TOPK_GROUPS = 4
GROUP_SIZE = N_EXPERTS // N_GROUPS
D_EXPERT = 256
ROUTED_SCALE = 2.5
RMS_EPS = 1e-6
LN_EPS = 1e-5
GN_EPS = 64e-5
COL_CONV = 2 * CONV_WIDTH
COL_RWKV = 3 * RWKV_WIDTH + LORA_W + LORA_A + LORA_G
COL_GATE = 2 * D_MODEL
D_IN = COL_CONV + COL_RWKV + COL_GATE

CHUNK = 64
QUAD = 4 * HEAD
N_QUADS = RWKV_WIDTH // QUAD
CUM_BLOCK = 256
CONV_HALO = 32
SUBLANES = 8
SHIFT_HALO = SUBLANES
GATE_LANES = 128
EXPERT_BLOCK = 512
EXPERT_SUB = 256
BLOCKS_PER_STEP = 2
ROW_WORDS = D_MODEL // 2
SC_CORES = 2
SC_SUBCORES = 16
SC_WORKERS = SC_CORES * SC_SUBCORES
SC_CHUNK = 128
SC_IDX_ROWS = 8
VMEM_LIMIT = 52 * 1024 * 1024


def _params(*sem):
    return pltpu.CompilerParams(dimension_semantics=sem, vmem_limit_bytes=VMEM_LIMIT)


def _sigmoid(x):
    return 1.0 / (1.0 + jnp.exp(-x))


def _softplus(x):
    return jnp.maximum(x, 0.0) + jnp.log(1.0 + jnp.exp(-jnp.abs(x)))


def _rms(x, g):
    return x * lax.rsqrt(jnp.mean(x * x, axis=-1, keepdims=True) + RMS_EPS) * g


def _split(x):
    hi = x.astype(BF16)
    return hi, (x - hi.astype(F32)).astype(BF16)


def _dot(a, b):
    return jnp.dot(a, b, preferred_element_type=F32)


def _dot_exact_rhs(x, w):
    hi, lo = _split(x)
    return _dot(hi, w) + _dot(lo, w)


def _pack_rows(y):
    half = y.shape[1] // 2
    packed = pltpu.pack_elementwise([y[:, half:], y[:, :half]], packed_dtype=BF16)
    return lax.bitcast_convert_type(packed, jnp.int32)


def _unpack_rows(u):
    hi = pltpu.unpack_elementwise(u, index=1, packed_dtype=BF16, unpacked_dtype=F32)
    lo = pltpu.unpack_elementwise(u, index=0, packed_dtype=BF16, unpacked_dtype=F32)
    return jnp.concatenate([hi, lo], axis=1)


def _dot_exact_lhs(w, x):
    hi, lo = _split(x)
    return _dot(w, hi) + _dot(w, lo)


def _ada_kernel(c_ref, w_ref, b_ref, o_ref):
    c = c_ref[...]
    a_hi, a_lo = _split(c * _sigmoid(c))
    w_hi, w_lo = _split(w_ref[...])
    o_ref[...] = _dot(a_hi, w_hi) + _dot(a_lo, w_hi) + _dot(a_hi, w_lo) + b_ref[...]


def _ada(c, w_ada, b_ada):
    nl, d, _ = w_ada.shape
    nb = c.shape[0]
    out = pl.pallas_call(
        _ada_kernel,
        grid=(nl, 6),
        in_specs=[
            pl.BlockSpec((nb, d), lambda l, k: (0, 0)),
            pl.BlockSpec((None, d, d), lambda l, k: (l, 0, k)),
            pl.BlockSpec((None, None, 1, d), lambda l, k: (l, k, 0, 0)),
        ],
        out_specs=pl.BlockSpec((None, None, nb, d), lambda l, k: (l, k, 0, 0)),
        out_shape=jax.ShapeDtypeStruct((nl, 6, nb, d), F32),
        compiler_params=_params("parallel", "parallel"),
        name="ada",
    )(c, w_ada, b_ada.reshape(nl, 6, 1, d))
    return out.reshape(nl, 6, nb, 1, d)


def _mod_spec(layer, piece, rows_per_batch, tm):
    return pl.BlockSpec((None, None, None, 1, D_MODEL),
                        lambda i, *_: (layer, piece, (i * tm) // rows_per_batch, 0, 0))


def _win_kernel(x_ref, g_ref, sc_ref, sh_ref, w_ref, zc_ref, zr_ref, zg_ref):
    h = _rms(x_ref[...], g_ref[...]) * (1.0 + sc_ref[...]) + sh_ref[...]
    hb = h.astype(BF16)
    zc_ref[...] = _dot(hb, w_ref[:, :COL_CONV]).astype(zc_ref.dtype)
    zr_ref[...] = _dot(hb, w_ref[:, COL_CONV:COL_CONV + COL_RWKV])
    zg_ref[...] = _dot(hb, w_ref[:, COL_CONV + COL_RWKV:]).astype(zg_ref.dtype)


def _win(x2, g, mod, layer, w_in_b, seq, tm=512):
    n, d = x2.shape
    row = lambda i: (i, 0)
    return pl.pallas_call(
        _win_kernel,
        grid=(n // tm,),
        in_specs=[
            pl.BlockSpec((tm, d), row),
            pl.BlockSpec((1, d), lambda i: (0, 0)),
            _mod_spec(layer, 1, seq, tm),
            _mod_spec(layer, 0, seq, tm),
            pl.BlockSpec((d, D_IN), lambda i: (0, 0)),
        ],
        out_specs=[pl.BlockSpec((tm, COL_CONV), row), pl.BlockSpec((tm, COL_RWKV), row),
                   pl.BlockSpec((tm, COL_GATE), row)],
        out_shape=[jax.ShapeDtypeStruct((n, COL_CONV), BF16), jax.ShapeDtypeStruct((n, COL_RWKV), F32),
                   jax.ShapeDtypeStruct((n, COL_GATE), BF16)],
        compiler_params=_params("parallel"),
        name="win",
    )(x2, g, mod, mod, w_in_b)


def _conv_kernel(z_ref, halo_ref, w_ref, cb_ref, g_ref, b_ref, o_ref, ubuf, *, ts):
    j = pl.program_id(1)
    z = z_ref[...].astype(F32)
    zh = halo_ref[...].astype(F32)
    uh = zh[:, :CONV_WIDTH] * _sigmoid(zh[:, CONV_WIDTH:])
    rows = CONV_HALO + ts
    u = jnp.concatenate([jnp.where(j > 0, uh, 0.0),
                         z[:, :CONV_WIDTH] * _sigmoid(z[:, CONV_WIDTH:])], axis=0)
    ubuf[0] = u
    for s in range(1, SUBLANES):
        ubuf[s] = pltpu.roll(u, rows - s, axis=0)
    sub = 64
    first = CONV_HALO - (CONV_KERNEL - 1)
    for r in range(ts // sub):
        acc = jnp.zeros((sub, CONV_WIDTH), F32) + cb_ref[...]
        for k in range(CONV_KERNEL):
            s = (first + k) % SUBLANES
            base = r * sub + first + k - s
            acc = acc + w_ref[k:k + 1, :] * ubuf[s, base:base + sub, :]
        mu = jnp.mean(acc, axis=-1, keepdims=True)
        dlt = acc - mu
        var = jnp.mean(dlt * dlt, axis=-1, keepdims=True)
        y = dlt * lax.rsqrt(var + LN_EPS) * g_ref[...] + b_ref[...]
        o_ref[r * sub:(r + 1) * sub, :] = (y * _sigmoid(y)).astype(o_ref.dtype)


def _conv(zc, conv_w, conv_b, ln_g, ln_b, nb, seq, ts=512):
    n = zc.shape[0]
    tps = seq // ts
    hpt = ts // CONV_HALO
    vec = pl.BlockSpec((1, CONV_WIDTH), lambda b, j: (0, 0))
    return pl.pallas_call(
        functools.partial(_conv_kernel, ts=ts),
        grid=(nb, tps),
        in_specs=[
            pl.BlockSpec((ts, COL_CONV), lambda b, j: (b * tps + j, 0)),
            pl.BlockSpec((CONV_HALO, COL_CONV), lambda b, j: (jnp.maximum((b * tps + j) * hpt - 1, 0), 0)),
            pl.BlockSpec((CONV_KERNEL, CONV_WIDTH), lambda b, j: (0, 0)),
            vec, vec, vec,
        ],
        out_specs=pl.BlockSpec((ts, CONV_WIDTH), lambda b, j: (b * tps + j, 0)),
        out_shape=jax.ShapeDtypeStruct((n, CONV_WIDTH), BF16),
        scratch_shapes=[pltpu.VMEM((SUBLANES, CONV_HALO + ts, CONV_WIDTH), F32)],
        compiler_params=_params("parallel", "parallel"),
        name="conv",
    )(zc, zc, conv_w, conv_b, ln_g, ln_b)


def _prep_kernel(z_ref, halo_ref, mu_ref, w0_ref, a0_ref, wah_ref, wal_ref, g2_ref, kk_ref, ka_ref, rk_ref,
                 ones_ref, tri_ref, csel_ref,
                 at_ref, bt_ref, kt_ref, rt_ref, v_ref, bts_ref, kts_ref, bonus_ref, g_ref, pc_ref):
    j = pl.program_id(1)
    z = z_ref[...]
    ts = z.shape[0]
    last = jnp.where(j > 0, halo_ref[SHIFT_HALO - 1:SHIFT_HALO, :], 0.0)
    row = lax.broadcasted_iota(jnp.int32, z.shape, 0)
    prev = jnp.where(row == 0, last, pltpu.roll(z, 1, axis=0))
    zs = z + (prev - z) * mu_ref[...]
    rw = RWKV_WIDTH
    r = zs[:, :rw]
    k = zs[:, rw:2 * rw]
    v = zs[:, 2 * rw:3 * rw]
    wad = zs[:, 3 * rw:3 * rw + LORA_W + LORA_A]
    gd = zs[:, 3 * rw + LORA_W + LORA_A:]
    lane = lax.broadcasted_iota(jnp.int32, wad.shape, 1)
    xh, xl = _split(jnp.where(lane < LORA_W, jnp.tanh(wad), wad))
    wah = wah_ref[...]
    lora = _dot(xh, wah) + _dot(xl, wah) + _dot(xh, wal_ref[...])
    w = -_softplus(-(w0_ref[...] + lora[:, :rw])) - 0.5
    lw = -jnp.exp(w)
    a = _sigmoid(a0_ref[...] + lora[:, rw:])
    g_ref[...] = _dot(_sigmoid(gd).astype(BF16), g2_ref[...]).astype(g_ref.dtype)
    kk = k * kk_ref[...]
    k2 = k * (1.0 + (a - 1.0) * ka_ref[...])
    ones = ones_ref[...]
    kk = kk * lax.rsqrt(jnp.maximum(_dot_exact_rhs(kk * kk, ones), 1e-24))
    bonus_ref[...] = (_dot_exact_rhs(r * k2 * rk_ref[...], ones) * v).astype(bonus_ref.dtype)
    tri = tri_ref[...]
    lc, tot = [], []
    for h in range(ts // CUM_BLOCK):
        cs = _dot_exact_lhs(tri, lw[h * CUM_BLOCK:(h + 1) * CUM_BLOCK])
        lc.append(cs[:CUM_BLOCK])
        tot.append(cs[CUM_BLOCK:])
    lc = jnp.concatenate(lc, axis=0)
    tot = jnp.concatenate(tot, axis=0)
    e_neg = jnp.exp(-lc)
    e_end = jnp.exp(tot - lc)
    kka = kk * a
    dt = at_ref.dtype
    at_ref[...] = (-kk * jnp.exp(lc - lw)).astype(dt)
    bt_ref[...] = (kka * e_neg).astype(dt)
    kt_ref[...] = (k2 * e_neg).astype(dt)
    rt_ref[...] = (r * jnp.exp(lc)).astype(dt)
    v_ref[...] = v.astype(dt)
    bts_ref[...] = (kka * e_end).astype(dt)
    kts_ref[...] = (k2 * e_end).astype(dt)
    pc_ref[...] = jnp.exp(_dot_exact_lhs(csel_ref[...], lw))


def _prep(zr, mu, w0, a0, wa, g2, k_k, k_a, r_k, nb, seq, ts=512):
    n = zr.shape[0]
    tps = seq // ts
    hpt = ts // SHIFT_HALO
    rw = RWKV_WIDTH
    cpt = ts // CHUNK
    ones = jnp.kron(jnp.eye(N_HEADS, dtype=F32), jnp.ones((HEAD, HEAD), F32)).astype(BF16)
    blk = jnp.kron(jnp.eye(CUM_BLOCK // CHUNK, dtype=F32), jnp.ones((CHUNK, CHUNK), F32))
    tri = jnp.concatenate([jnp.tril(blk), blk], axis=0).astype(BF16)
    csel = jnp.kron(jnp.eye(cpt, dtype=F32), jnp.ones((1, CHUNK), F32)).astype(BF16)
    wa_hi, wa_lo = _split(wa)
    vec = pl.BlockSpec((1, rw), lambda b, j: (0, 0))
    full = lambda shape: pl.BlockSpec(shape, lambda b, j: (0, 0))
    rows = pl.BlockSpec((ts, rw), lambda b, j: (b * tps + j, 0))
    return pl.pallas_call(
        _prep_kernel,
        grid=(nb, tps),
        in_specs=[
            pl.BlockSpec((ts, COL_RWKV), lambda b, j: (b * tps + j, 0)),
            pl.BlockSpec((SHIFT_HALO, COL_RWKV), lambda b, j: (jnp.maximum((b * tps + j) * hpt - 1, 0), 0)),
            full((1, COL_RWKV)), vec, vec,
            full((LORA_W + LORA_A, 2 * rw)), full((LORA_W + LORA_A, 2 * rw)), full((LORA_G, rw)),
            vec, vec, vec, full((rw, rw)), full((2 * CUM_BLOCK, CUM_BLOCK)), full((cpt, ts)),
        ],
        out_specs=[rows] * 9 + [pl.BlockSpec((cpt, rw), lambda b, j: (b * tps + j, 0))],
        out_shape=[jax.ShapeDtypeStruct((n, rw), BF16)] * 9 + [jax.ShapeDtypeStruct((n // CHUNK, rw), F32)],
        compiler_params=_params("parallel", "parallel"),
        name="prep",
    )(zr, zr, mu, w0, a0, wa_hi, wa_lo, g2.astype(BF16), k_k, k_a, r_k, ones, tri, csel)


def _mm(a, b):
    return jnp.dot(a.astype(BF16), b.astype(BF16), preferred_element_type=F32)


def _mm_nt(a, b):
    return lax.dot_general(a.astype(BF16), b.astype(BF16), (((1,), (1,)), ((), ())),
                           preferred_element_type=F32)


def _mm_tn(a, b):
    return lax.dot_general(a.astype(BF16), b.astype(BF16), (((0,), (0,)), ((), ())),
                           preferred_element_type=F32)


def _scan_kernel(at_ref, bt_ref, kt_ref, rt_ref, v_ref, bts_ref, kts_ref, bonus_ref, g_ref, pc_ref,
                 gng_ref, gnb_ref, ones_ref, o_ref, s_ref, y_ref, *, ts, gb):
    @pl.when(pl.program_id(1) == 0)
    def _():
        s_ref[...] = jnp.zeros_like(s_ref)

    row_q = lax.broadcasted_iota(jnp.int32, (QUAD, QUAD), 0)
    col_q = lax.broadcasted_iota(jnp.int32, (QUAD, QUAD), 1)
    bdmask = (row_q // HEAD) == (col_q // HEAD)
    row_c = lax.broadcasted_iota(jnp.int32, (CHUNK, QUAD), 0)
    col_c = lax.broadcasted_iota(jnp.int32, (CHUNK, QUAD), 1) % CHUNK
    strict = col_c < row_c
    incl = col_c <= row_c
    eye = jnp.where(col_c == row_c, 1.0, 0.0).astype(F32)

    def bd(x):
        xb = x.astype(BF16)
        return jnp.where(bdmask, jnp.concatenate([xb] * (QUAD // CHUNK), axis=0), jnp.zeros((), BF16))

    chains = [(b, qd) for b in range(gb) for qd in range(N_QUADS)]

    def chunk(c, carry):
        sl = pl.ds(pl.multiple_of(c * CHUNK, CHUNK), CHUNK)
        lanes = [slice(qd * QUAD, (qd + 1) * QUAD) for _, qd in chains]
        ld = lambda ref: [ref[b, sl, ln] for (b, _), ln in zip(chains, lanes)]
        at, bt, kt, rt, v, bts, kts = (ld(r) for r in (at_ref, bt_ref, kt_ref, rt_ref, v_ref, bts_ref, kts_ref))
        each = range(len(chains))
        ar = [jnp.concatenate([at[i], rt[i]], axis=0) for i in each]
        pb = [_mm_nt(ar[i], bd(bt[i])) for i in each]
        pk = [_mm_nt(ar[i], bd(kt[i])) for i in each]
        l_ab = [jnp.where(strict, p[:CHUNK], 0.0) for p in pb]
        a_rb = [jnp.where(incl, p[CHUNK:], 0.0) for p in pb]
        l_ak = [jnp.where(strict, p[:CHUNK], 0.0) for p in pk]
        a_rk = [jnp.where(incl, p[CHUNK:], 0.0) for p in pk]
        tm = [eye + l for l in l_ab]
        lp = [_mm(l, bd(l)) for l in l_ab]
        for _ in range(4):
            both = [_mm(jnp.concatenate([tm[i], lp[i]], axis=0), bd(lp[i])) for i in each]
            tm = [tm[i] + both[i][:CHUNK] for i in each]
            lp = [both[i][CHUNK:] for i in each]
        tm = [tm[i] + _mm(tm[i], bd(lp[i])) for i in each]
        bdv = [bd(x) for x in v]
        x0 = [_mm(l_ak[i], bdv[i]) for i in each]
        tu = [_mm(tm[i], jnp.concatenate([bd(x0[i]), bd(at[i])], axis=1)) for i in each]
        u0 = [t[:, :QUAD] for t in tu]
        w = [t[:, QUAD:] for t in tu]
        qy = [_mm(a_rb[i], jnp.concatenate([bd(w[i]), bd(u0[i])], axis=1)) for i in each]
        y0 = [qy[i][:, QUAD:] + _mm(a_rk[i], bdv[i]) for i in each]
        q = [rt[i].astype(F32) + qy[i][:, :QUAD] for i in each]
        s = [s_ref[i] for i in each]
        qw_s = [_mm_nt(jnp.concatenate([q[i], w[i]], axis=0), s[i]) for i in each]
        for i, ((b, _), ln) in enumerate(zip(chains, lanes)):
            y_ref[b, sl, ln] = y0[i] + qw_s[i][:CHUNK]
        u = [(u0[i] + qw_s[i][CHUNK:]).astype(BF16) for i in each]
        n_p = [jnp.where(bdmask, _mm_tn(jnp.concatenate([u[i], v[i]], axis=0),
                                        jnp.concatenate([bts[i], kts[i]], axis=0)), 0.0) for i in each]
        for i, ((b, _), ln) in enumerate(zip(chains, lanes)):
            s_ref[i] = s[i] * pc_ref[b, pl.ds(c, 1), ln] + n_p[i]
        return carry

    lax.fori_loop(0, ts // CHUNK, chunk, 0)

    ones = ones_ref[...]
    for b in range(gb):
        for qd in range(N_QUADS):
            ln = slice(qd * QUAD, (qd + 1) * QUAD)
            y = y_ref[b, :, ln]
            dlt = y - _dot_exact_rhs(y, ones) * (1.0 / HEAD)
            var = _dot_exact_rhs(dlt * dlt, ones) * (1.0 / HEAD)
            yn = dlt * lax.rsqrt(var + GN_EPS) * gng_ref[:, ln] + gnb_ref[:, ln]
            o_ref[b, :, ln] = ((yn + bonus_ref[b, :, ln].astype(F32)) * g_ref[b, :, ln].astype(F32)).astype(o_ref.dtype)


def _scan(prep_out, gn_g, gn_b, nb, seq, ts=128, gb=8):
    n = prep_out[0].shape[0]
    rw = RWKV_WIDTH
    cpt = ts // CHUNK
    ones = jnp.kron(jnp.eye(QUAD // HEAD, dtype=F32), jnp.ones((HEAD, HEAD), F32)).astype(BF16)
    rows = pl.BlockSpec((gb, ts, rw), lambda i, t: (i, t, 0))
    vec = pl.BlockSpec((1, rw), lambda i, t: (0, 0))
    args = [a.reshape(nb, seq, rw) for a in prep_out[:9]] + [prep_out[9].reshape(nb, seq // ts, cpt, rw)]
    out = pl.pallas_call(
        functools.partial(_scan_kernel, ts=ts, gb=gb),
        grid=(nb // gb, seq // ts),
        in_specs=[rows] * 9 + [pl.BlockSpec((gb, None, cpt, rw), lambda i, t: (i, t, 0, 0)), vec, vec,
                               pl.BlockSpec((QUAD, QUAD), lambda i, t: (0, 0))],
        out_specs=rows,
        out_shape=jax.ShapeDtypeStruct((nb, seq, rw), BF16),
        scratch_shapes=[pltpu.VMEM((gb * N_QUADS, QUAD, QUAD), F32), pltpu.VMEM((gb, ts, rw), F32)],
        compiler_params=_params("parallel", "arbitrary"),
        name="scan",
    )(*args, gn_g, gn_b, ones)
    return out.reshape(n, rw)


def _merge_kernel(uc_ref, ur_ref, zg_ref, x_ref, pc_ref, pr_ref, wo_ref, gpost_ref, gt_ref,
                  gpre_ref, sc_ref, sh_ref, wrh_ref, wrl_ref, xo_ref, h_ref, lg_ref):
    yc = _dot(uc_ref[...], pc_ref[...])
    yr = _dot(ur_ref[...], pr_ref[...])
    zg = zg_ref[...].astype(F32)
    m = _sigmoid(zg[:, :D_MODEL]) * yc + _sigmoid(zg[:, D_MODEL:]) * yr
    y = _dot(m.astype(BF16), wo_ref[...])
    xn = x_ref[...] + gt_ref[...] * _rms(y, gpost_ref[...])
    xo_ref[...] = xn
    h = _rms(xn, gpre_ref[...]) * (1.0 + sc_ref[...]) + sh_ref[...]
    h_ref[...] = _pack_rows(h)
    h_hi, h_lo = _split(h)
    nt = lambda a, b: lax.dot_general(a, b, (((1,), (1,)), ((), ())), preferred_element_type=F32)
    w_hi = wrh_ref[...]
    lg_ref[...] = nt(w_hi, h_hi) + nt(w_hi, h_lo) + nt(wrl_ref[...], h_hi)


def _merge(uc, ur, zg, x2, p_conv_b, p_rwkv_b, w_o_b, g_post, g_pre, mod, layer, w_router_t, seq, tm=512):
    n, d = x2.shape
    row = lambda i: (i, 0)
    full = lambda shape: pl.BlockSpec(shape, lambda i: (0, 0))
    return pl.pallas_call(
        _merge_kernel,
        grid=(n // tm,),
        in_specs=[
            pl.BlockSpec((tm, CONV_WIDTH), row), pl.BlockSpec((tm, RWKV_WIDTH), row),
            pl.BlockSpec((tm, COL_GATE), row), pl.BlockSpec((tm, d), row),
            full((CONV_WIDTH, d)), full((RWKV_WIDTH, d)), full((d, d)),
            full((1, d)), _mod_spec(layer, 2, seq, tm),
            full((1, d)), _mod_spec(layer, 4, seq, tm), _mod_spec(layer, 3, seq, tm),
            full((N_EXPERTS, d)), full((N_EXPERTS, d)),
        ],
        out_specs=[pl.BlockSpec((tm, d), row), pl.BlockSpec((tm, ROW_WORDS), row),
                   pl.BlockSpec((N_EXPERTS, tm), lambda i: (0, i))],
        out_shape=[jax.ShapeDtypeStruct((n, d), F32), jax.ShapeDtypeStruct((n, ROW_WORDS), jnp.int32),
                   jax.ShapeDtypeStruct((N_EXPERTS, n), F32)],
        compiler_params=_params("parallel"),
        name="merge",
    )(uc, ur, zg, x2, p_conv_b, p_rwkv_b, w_o_b, g_post, mod, g_pre, mod, mod, *_split(w_router_t))


def _route_kernel(lg_ref, b_ref, gate_ref, sel_ref, cnt_ref):
    s = _sigmoid(lg_ref[...])
    biased = s + b_ref[...]
    t = s.shape[1]
    member = lax.broadcasted_iota(jnp.int32, (GROUP_SIZE, t), 0)
    grp = []
    for g in range(N_GROUPS):
        bg = biased[g * GROUP_SIZE:(g + 1) * GROUP_SIZE, :]
        m1 = jnp.max(bg, axis=0, keepdims=True)
        first = jnp.min(jnp.where(bg == m1, member, GROUP_SIZE), axis=0, keepdims=True)
        m2 = jnp.max(jnp.where(member == first, -jnp.inf, bg), axis=0, keepdims=True)
        grp.append(m1 + m2)
    masked = []
    for g in range(N_GROUPS):
        rank = jnp.zeros((1, t), jnp.int32)
        for o in range(N_GROUPS):
            if o == g:
                continue
            ahead = (grp[o] > grp[g]) if o > g else (grp[o] >= grp[g])
            rank = rank + jnp.where(ahead, 1, 0)
        keep = rank < TOPK_GROUPS
        masked.append(jnp.where(keep, biased[g * GROUP_SIZE:(g + 1) * GROUP_SIZE, :], -jnp.inf))
    masked = jnp.concatenate(masked, axis=0)
    eidx = lax.broadcasted_iota(jnp.int32, masked.shape, 0)
    mask = jnp.zeros(masked.shape, F32)
    for _ in range(TOP_K):
        top = jnp.max(masked, axis=0, keepdims=True)
        hit = eidx == jnp.min(jnp.where(masked == top, eidx, N_EXPERTS), axis=0, keepdims=True)
        mask = jnp.where(hit, 1.0, mask)
        masked = jnp.where(hit, -jnp.inf, masked)
    sel = s * mask
    gate_ref[...] = sel / jnp.sum(sel, axis=0, keepdims=True) * ROUTED_SCALE
    sel_ref[...] = mask.astype(sel_ref.dtype)

    @pl.when(pl.program_id(0) == 0)
    def _():
        cnt_ref[...] = jnp.zeros_like(cnt_ref)

    cnt_ref[...] += jnp.sum(mask, axis=1, keepdims=True)


def _route(logits_t, b_router, tt=512):
    e, n = logits_t.shape
    return pl.pallas_call(
        _route_kernel,
        grid=(n // tt,),
        in_specs=[pl.BlockSpec((e, tt), lambda i: (0, i)), pl.BlockSpec((e, 1), lambda i: (0, 0))],
        out_specs=[pl.BlockSpec((e, tt), lambda i: (0, i)), pl.BlockSpec((e, tt), lambda i: (0, i)),
                   pl.BlockSpec((e, GATE_LANES), lambda i: (0, 0))],
        out_shape=[jax.ShapeDtypeStruct((e, n), F32), jax.ShapeDtypeStruct((e, n), BF16),
                   jax.ShapeDtypeStruct((e, GATE_LANES), F32)],
        compiler_params=_params("arbitrary"),
        name="route",
    )(logits_t, b_router)


def _dest_kernel(sel_ref, gate_ref, start_ref, triu_ref, below_ref, dest_ref, g8_ref, run_ref):
    @pl.when(pl.program_id(0) == 0)
    def _():
        run_ref[...] = jnp.zeros_like(run_ref)

    sel = sel_ref[...]
    t = sel.shape[1]
    self32 = sel.astype(F32)
    incl = _dot(sel, triu_ref[...])
    pos = start_ref[...] + run_ref[...] + incl - self32
    run_ref[...] += incl[:, t - 1:t]
    slot = _dot(below_ref[...], sel)
    gate = gate_ref[...]
    dst, gts = [], []
    for k in range(TOP_K):
        mine = (self32 > 0.0) & (slot == float(k))
        dst.append(jnp.sum(jnp.where(mine, pos, 0.0), axis=0, keepdims=True))
        gts.append(jnp.sum(jnp.where(mine, gate, 0.0), axis=0, keepdims=True))
    dest_ref[...] = jnp.concatenate(dst, axis=0).astype(jnp.int32)
    g8_ref[...] = jnp.concatenate(gts + [jnp.zeros((GATE_LANES - TOP_K, t), F32)], axis=0).T


def _dest(sel, gate, seg_start, tt=512):
    e, n = sel.shape
    triu = jnp.triu(jnp.ones((tt, tt), F32)).astype(BF16)
    below = jnp.tril(jnp.ones((e, e), F32), -1).astype(BF16)
    return pl.pallas_call(
        _dest_kernel,
        grid=(n // tt,),
        in_specs=[pl.BlockSpec((e, tt), lambda i: (0, i)), pl.BlockSpec((e, tt), lambda i: (0, i)),
                  pl.BlockSpec((e, 1), lambda i: (0, 0)), pl.BlockSpec((tt, tt), lambda i: (0, 0)),
                  pl.BlockSpec((e, e), lambda i: (0, 0))],
        out_specs=[pl.BlockSpec((TOP_K, tt), lambda i: (0, i)), pl.BlockSpec((tt, GATE_LANES), lambda i: (i, 0))],
        out_shape=[jax.ShapeDtypeStruct((TOP_K, n), jnp.int32), jax.ShapeDtypeStruct((n, GATE_LANES), F32)],
        scratch_shapes=[pltpu.VMEM((e, 1), F32)],
        compiler_params=_params("arbitrary"),
        name="dest",
    )(sel, gate, seg_start, triu, below)


def _sc_mesh():
    return plsc.VectorSubcoreMesh(core_axis_name="c", subcore_axis_name="s",
                                  num_cores=SC_CORES, num_subcores=SC_SUBCORES)


def _worker_id():
    return lax.axis_index("s") * SC_CORES + lax.axis_index("c")


def _sc_scatter_rows(rows, dest, n_out):
    n, w = rows.shape
    per_worker = n // SC_WORKERS
    chunks_per_worker = per_worker // SC_CHUNK
    idx_rows = dest.reshape(TOP_K, n // SC_CHUNK, SC_CHUNK).transpose(1, 0, 2).reshape(-1, SC_CHUNK)

    def body(rows_hbm, idx_hbm, out_hbm, idx_v, rows_v):
        wid = _worker_id()

        @pl.loop(0, chunks_per_worker)
        def _(c):
            chunk = wid * chunks_per_worker + c
            pltpu.sync_copy(rows_hbm.at[pl.ds(chunk * SC_CHUNK, SC_CHUNK)], rows_v)
            pltpu.sync_copy(idx_hbm.at[pl.ds(chunk * TOP_K, TOP_K)], idx_v)
            for k in range(TOP_K):
                pltpu.sync_copy(rows_v, out_hbm.at[idx_v.at[k]])

    return pl.kernel(
        body, out_type=jax.ShapeDtypeStruct((n_out, w), rows.dtype), mesh=_sc_mesh(),
        scratch_types=[pltpu.VMEM((TOP_K, SC_CHUNK), jnp.int32), pltpu.VMEM((SC_CHUNK, w), rows.dtype)],
        name="sc_scatter",
    )(rows, idx_rows)


def _sc_gather_rows(table, idx_flat):
    m = idx_flat.shape[0]
    w = table.shape[1]
    ch = SC_CHUNK // 2
    chunks_per_worker = m // SC_WORKERS // ch
    idx_rows = idx_flat.reshape(-1, ch)
    ng = SC_IDX_ROWS

    def body(table_hbm, idx_hbm, out_hbm, idx_v, rows_a, rows_b, gat_a, gat_b, put_a, put_b):
        wid = _worker_id()
        bufs = ((rows_a, gat_a, put_a), (rows_b, gat_b, put_b))

        @pl.loop(0, chunks_per_worker // ng)
        def _(g):
            first = wid * chunks_per_worker + g * ng
            pltpu.sync_copy(idx_hbm.at[pl.ds(first, ng)], idx_v)
            gathers, stores = [None] * ng, [None] * ng
            gathers[0] = pltpu.async_copy(table_hbm.at[idx_v.at[0]], rows_a, gat_a)
            for j in range(ng):
                buf, _, put = bufs[j % 2]
                if j + 1 < ng:
                    if j >= 1:
                        stores[j - 1].wait()
                    nxt, gat, _ = bufs[(j + 1) % 2]
                    gathers[j + 1] = pltpu.async_copy(table_hbm.at[idx_v.at[j + 1]], nxt, gat)
                gathers[j].wait()
                stores[j] = pltpu.async_copy(buf, out_hbm.at[pl.ds((first + j) * ch, ch)], put)
            stores[ng - 2].wait()
            stores[ng - 1].wait()

    return pl.kernel(
        body, out_type=jax.ShapeDtypeStruct((m, w), table.dtype), mesh=_sc_mesh(),
        scratch_types=[pltpu.VMEM((ng, ch), jnp.int32), pltpu.VMEM((ch, w), table.dtype),
                       pltpu.VMEM((ch, w), table.dtype)] + [pltpu.SemaphoreType.DMA] * 4,
        name="sc_gather",
    )(table, idx_rows)


def _swiglu(h, wg, wu):
    gte = _dot(h, wg)
    return gte * _sigmoid(gte) * _dot(h, wu)


def _experts_kernel(be_ref, va_ref, xs_ref, *refs):
    nb = BLOCKS_PER_STEP
    wg32, wu32, wd32, ys_ref = refs[:nb], refs[nb:2 * nb], refs[2 * nb:3 * nb], refs[3 * nb]
    wg, wu, wd = (refs[3 * nb + 1 + i * nb:3 * nb + 1 + (i + 1) * nb] for i in range(3))
    step = pl.program_id(0)
    first = step * nb
    live = va_ref[first] > 0

    for b in range(nb):
        changed = jnp.logical_or(step == 0, be_ref[first + b] != be_ref[jnp.maximum(first + b - nb, 0)])

        @pl.when(jnp.logical_and(live, changed))
        def _(b=b):
            wg[b][...] = wg32[b][...].astype(BF16)
            wu[b][...] = wu32[b][...].astype(BF16)
            wd[b][...] = wd32[b][...].astype(BF16)

    @pl.when(live)
    def _():
        row = lax.broadcasted_iota(jnp.int32, (EXPERT_SUB, D_MODEL), 0)
        subs = [(b, r0) for b in range(nb) for r0 in range(0, EXPERT_BLOCK, EXPERT_SUB)]
        x = []
        for b, r0 in subs:
            lo = b * EXPERT_BLOCK + r0
            xi = _unpack_rows(xs_ref[lo:lo + EXPERT_SUB, :])
            x.append(jnp.where(row < va_ref[first + b] - r0, xi, 0.0).astype(BF16))
        gte = [_dot(xi, wg[b][...]) for xi, (b, _) in zip(x, subs)]
        up = [_dot(xi, wu[b][...]) for xi, (b, _) in zip(x, subs)]
        act = [(g * _sigmoid(g) * u).astype(BF16) for g, u in zip(gte, up)]
        y = [_dot(a, wd[b][...]) for a, (b, _) in zip(act, subs)]
        for yi, (b, r0) in zip(y, subs):
            lo = b * EXPERT_BLOCK + r0
            ys_ref[lo:lo + EXPERT_SUB, :] = _pack_rows(yi)


def _experts(xs, blk_expert, blk_valid, wg, wu, wd, layer):
    n_rows, w = xs.shape
    d = wg.shape[2]
    nb = BLOCKS_PER_STEP
    rows = pl.BlockSpec((nb * EXPERT_BLOCK, w), lambda s, be, va: (s, 0))
    up_spec = lambda i: pl.BlockSpec((None, None, d, D_EXPERT), lambda s, be, va: (layer, be[s * nb + i], 0, 0))
    down_spec = lambda i: pl.BlockSpec((None, None, D_EXPERT, d), lambda s, be, va: (layer, be[s * nb + i], 0, 0))
    return pl.pallas_call(
        _experts_kernel,
        grid_spec=pltpu.PrefetchScalarGridSpec(
            num_scalar_prefetch=2,
            grid=(n_rows // (nb * EXPERT_BLOCK),),
            in_specs=[rows] + [up_spec(i) for i in range(nb)] * 2 + [down_spec(i) for i in range(nb)],
            out_specs=rows,
            scratch_shapes=([pltpu.VMEM((d, D_EXPERT), BF16)] * (2 * nb) + [pltpu.VMEM((D_EXPERT, d), BF16)] * nb),
        ),
        out_shape=jax.ShapeDtypeStruct((n_rows, w), xs.dtype),
        compiler_params=_params("arbitrary"),
        name="experts",
    )(blk_expert, blk_valid, xs, *([wg] * nb), *([wu] * nb), *([wd] * nb))


def _combine_kernel(yg_ref, g8_ref, hp_ref, sg_ref, su_ref, sd_ref, x_ref, gpost_ref, gt_ref, o_ref):
    h = _unpack_rows(hp_ref[...]).astype(BF16)
    acc = _dot(_swiglu(h, sg_ref[...], su_ref[...]).astype(BF16), sd_ref[...])
    g8 = g8_ref[...]
    for k in range(TOP_K):
        acc = acc + g8[:, k:k + 1] * _unpack_rows(yg_ref[k])
    o_ref[...] = x_ref[...] + gt_ref[...] * _rms(acc, gpost_ref[...])


def _combine(yg, g8, hp, sg, su, sd, x2, g_post, mod, layer, seq, tm=512):
    n, d = x2.shape
    row = lambda i: (i, 0)
    full = lambda shape: pl.BlockSpec(shape, lambda i: (0, 0))
    in_specs = [
        pl.BlockSpec((TOP_K, tm, ROW_WORDS), lambda i: (0, i, 0)),
        pl.BlockSpec((tm, GATE_LANES), row),
        pl.BlockSpec((tm, ROW_WORDS), row),
        full((d, D_EXPERT)), full((d, D_EXPERT)), full((D_EXPERT, d)),
        pl.BlockSpec((tm, d), row),
        full((1, d)),
        _mod_spec(layer, 5, seq, tm),
    ]
    return pl.pallas_call(
        _combine_kernel,
        grid=(n // tm,),
        in_specs=in_specs,
        out_specs=pl.BlockSpec((tm, d), row),
        out_shape=jax.ShapeDtypeStruct((n, d), F32),
        compiler_params=_params("parallel"),
        name="combine",
    )(yg, g8, hp, sg, su, sd, x2, g_post, mod)


def _moe(hp, logits_t, b_router, wg, wu, wd, sg, su, sd, x2, g_post, mod, layer, seq):
    n = x2.shape[0]
    gate, sel, counts = _route(logits_t, b_router)
    cnt = counts[:, 0].astype(jnp.int32)
    padded = (cnt + EXPERT_BLOCK - 1) // EXPERT_BLOCK * EXPERT_BLOCK
    seg_end = jnp.cumsum(padded)
    seg_start = seg_end - padded
    n_blocks = n * TOP_K // EXPERT_BLOCK + N_EXPERTS
    blk_row = jnp.arange(n_blocks, dtype=jnp.int32) * EXPERT_BLOCK
    blk_expert = jnp.sum((seg_end[None, :] <= blk_row[:, None]).astype(jnp.int32), axis=1)
    blk_expert = jnp.minimum(blk_expert, N_EXPERTS - 1)
    mine = blk_expert[:, None] == jnp.arange(N_EXPERTS, dtype=jnp.int32)[None, :]
    seg_last = jnp.sum(jnp.where(mine, (seg_start + cnt)[None, :], 0), axis=1)
    blk_valid = jnp.clip(seg_last - blk_row, 0, EXPERT_BLOCK).astype(jnp.int32)
    dest, g8 = _dest(sel, gate, seg_start.astype(F32).reshape(-1, 1))
    dest_flat = dest.reshape(-1)
    xs = _sc_scatter_rows(hp, dest, n_blocks * EXPERT_BLOCK)
    ys = _experts(xs, blk_expert, blk_valid, wg, wu, wd, layer)
    yg = _sc_gather_rows(ys, dest_flat).reshape(TOP_K, n, ROW_WORDS)
    return _combine(yg, g8, hp, sg, su, sd, x2, g_post, mod, layer, seq)


def kernel(x, c, w_ada, b_ada, norm_mix_pre, norm_mix_post, norm_ffn_pre, norm_ffn_post, w_in, mu_shift, conv_w, conv_b, conv_ln_g, conv_ln_b, p_conv, w0, w2, a0, a2, g2, k_k, k_a, r_k, gn_g, gn_b, p_rwkv, w_o, w_router, b_router, we_gate, we_up, we_down, ws_gate, ws_up, ws_down):
    nb, seq, d = x.shape
    depth = w_ada.shape[0]
    n = nb * seq
    rw = RWKV_WIDTH
    mod = _ada(c, w_ada, b_ada)
    x2 = x.reshape(n, d)
    row = lambda a: a.reshape(1, -1)
    bf = lambda a: a.astype(BF16)
    for l in range(depth):
        zc, zr, zg = _win(x2, row(norm_mix_pre[l]), mod, l, bf(w_in[l]), seq)
        uc = _conv(zc, conv_w[l], row(conv_b[l]), row(conv_ln_g[l]), row(conv_ln_b[l]), nb, seq)
        wa = jnp.zeros((LORA_W + LORA_A, 2 * rw), F32)
        wa = wa.at[:LORA_W, :rw].set(w2[l]).at[LORA_W:, rw:].set(a2[l])
        prep_out = _prep(zr, row(mu_shift[l]), row(w0[l]), row(a0[l]), wa, g2[l], row(k_k[l]),
                         row(k_a[l]), row(r_k[l]), nb, seq)
        ur = _scan(prep_out, row(gn_g[l]), row(gn_b[l]), nb, seq)
        x2, hp, logits_t = _merge(uc, ur, zg, x2, bf(p_conv[l]), bf(p_rwkv[l]), bf(w_o[l]),
                                  row(norm_mix_post[l]), row(norm_ffn_pre[l]), mod, l, w_router[l].T, seq)
        x2 = _moe(hp, logits_t, b_router[l].reshape(-1, 1), we_gate, we_up, we_down, bf(ws_gate[l]), bf(ws_up[l]),
                  bf(ws_down[l]), x2, row(norm_ffn_post[l]), mod, l, seq)
    return x2.reshape(nb, seq, d)
```

```python
import functools

import jax
import jax.numpy as jnp
from jax import lax
from jax.experimental import pallas as pl
from jax.experimental.pallas import tpu as pltpu
from jax.experimental.pallas import tpu_sc as plsc

F32 = jnp.float32
BF16 = jnp.bfloat16

D_MODEL = 1024
CONV_WIDTH = 512
CONV_KERNEL = 31
RWKV_WIDTH = 512
HEAD = 64
N_HEADS = RWKV_WIDTH // HEAD
LORA_W = 64
LORA_A = 64
LORA_G = 128
N_EXPERTS = 64
TOP_K = 8
N_GROUPS = 8
TOPK_GROUPS = 4
GROUP_SIZE = N_EXPERTS // N_GROUPS
D_EXPERT = 256
ROUTED_SCALE = 2.5
RMS_EPS = 1e-6
LN_EPS = 1e-5
GN_EPS = 64e-5
COL_CONV = 2 * CONV_WIDTH
COL_RWKV = 3 * RWKV_WIDTH + LORA_W + LORA_A + LORA_G
COL_GATE = 2 * D_MODEL
D_IN = COL_CONV + COL_RWKV + COL_GATE

CHUNK = 64
QUAD = 4 * HEAD
N_QUADS = RWKV_WIDTH // QUAD
CUM_BLOCK = 256
CONV_HALO = 32
SUBLANES = 8
SHIFT_HALO = SUBLANES
GATE_LANES = 128
EXPERT_BLOCK = 512
EXPERT_SUB = 256
BLOCKS_PER_STEP = 2
ROW_WORDS = D_MODEL // 2
SC_CORES = 2
SC_SUBCORES = 16
SC_WORKERS = SC_CORES * SC_SUBCORES
SC_CHUNK = 128
SC_IDX_ROWS = 8
VMEM_LIMIT = 52 * 1024 * 1024


def _params(*sem):
    return pltpu.CompilerParams(dimension_semantics=sem, vmem_limit_bytes=VMEM_LIMIT)


def _sigmoid(x):
    return 1.0 / (1.0 + jnp.exp(-x))


def _softplus(x):
    return jnp.maximum(x, 0.0) + jnp.log(1.0 + jnp.exp(-jnp.abs(x)))


def _rms(x, g):
    return x * lax.rsqrt(jnp.mean(x * x, axis=-1, keepdims=True) + RMS_EPS) * g


def _split(x):
    hi = x.astype(BF16)
    return hi, (x - hi.astype(F32)).astype(BF16)


def _dot(a, b):
    return jnp.dot(a, b, preferred_element_type=F32)


def _dot_exact_rhs(x, w):
    hi, lo = _split(x)
    return _dot(hi, w) + _dot(lo, w)


def _pack_rows(y):
    half = y.shape[1] // 2
    packed = pltpu.pack_elementwise([y[:, half:], y[:, :half]], packed_dtype=BF16)
    return lax.bitcast_convert_type(packed, jnp.int32)


def _unpack_rows(u):
    hi = pltpu.unpack_elementwise(u, index=1, packed_dtype=BF16, unpacked_dtype=F32)
    lo = pltpu.unpack_elementwise(u, index=0, packed_dtype=BF16, unpacked_dtype=F32)
    return jnp.concatenate([hi, lo], axis=1)


def _dot_exact_lhs(w, x):
    hi, lo = _split(x)
    return _dot(w, hi) + _dot(w, lo)


def _ada_kernel(c_ref, w_ref, b_ref, o_ref):
    c = c_ref[...]
    a_hi, a_lo = _split(c * _sigmoid(c))
    w_hi, w_lo = _split(w_ref[...])
    o_ref[...] = _dot(a_hi, w_hi) + _dot(a_lo, w_hi) + _dot(a_hi, w_lo) + b_ref[...]


def _ada(c, w_ada, b_ada):
    nl, d, _ = w_ada.shape
    nb = c.shape[0]
    out = pl.pallas_call(
        _ada_kernel,
        grid=(nl, 6),
        in_specs=[
            pl.BlockSpec((nb, d), lambda l, k: (0, 0)),
            pl.BlockSpec((None, d, d), lambda l, k: (l, 0, k)),
            pl.BlockSpec((None, None, 1, d), lambda l, k: (l, k, 0, 0)),
        ],
        out_specs=pl.BlockSpec((None, None, nb, d), lambda l, k: (l, k, 0, 0)),
        out_shape=jax.ShapeDtypeStruct((nl, 6, nb, d), F32),
        compiler_params=_params("parallel", "parallel"),
        name="ada",
    )(c, w_ada, b_ada.reshape(nl, 6, 1, d))
    return out.reshape(nl, 6, nb, 1, d)


def _mod_spec(layer, piece, rows_per_batch, tm):
    return pl.BlockSpec((None, None, None, 1, D_MODEL),
                        lambda i, *_: (layer, piece, (i * tm) // rows_per_batch, 0, 0))


def _win_kernel(x_ref, g_ref, sc_ref, sh_ref, w_ref, zc_ref, zr_ref, zg_ref):
    h = _rms(x_ref[...], g_ref[...]) * (1.0 + sc_ref[...]) + sh_ref[...]
    hb = h.astype(BF16)
    zc_ref[...] = _dot(hb, w_ref[:, :COL_CONV]).astype(zc_ref.dtype)
    zr_ref[...] = _dot(hb, w_ref[:, COL_CONV:COL_CONV + COL_RWKV])
    zg_ref[...] = _dot(hb, w_ref[:, COL_CONV + COL_RWKV:]).astype(zg_ref.dtype)


def _win(x2, g, mod, layer, w_in_b, seq, tm=512):
    n, d = x2.shape
    row = lambda i: (i, 0)
    return pl.pallas_call(
        _win_kernel,
        grid=(n // tm,),
        in_specs=[
            pl.BlockSpec((tm, d), row),
            pl.BlockSpec((1, d), lambda i: (0, 0)),
            _mod_spec(layer, 1, seq, tm),
            _mod_spec(layer, 0, seq, tm),
            pl.BlockSpec((d, D_IN), lambda i: (0, 0)),
        ],
        out_specs=[pl.BlockSpec((tm, COL_CONV), row), pl.BlockSpec((tm, COL_RWKV), row),
                   pl.BlockSpec((tm, COL_GATE), row)],
        out_shape=[jax.ShapeDtypeStruct((n, COL_CONV), BF16), jax.ShapeDtypeStruct((n, COL_RWKV), F32),
                   jax.ShapeDtypeStruct((n, COL_GATE), BF16)],
        compiler_params=_params("parallel"),
        name="win",
    )(x2, g, mod, mod, w_in_b)


def _conv_kernel(z_ref, halo_ref, w_ref, cb_ref, g_ref, b_ref, eg_ref, eu_ref, ed_ref,
                 o_ref, og_ref, ou_ref, od_ref, ubuf, *, ts):
    og_ref[...] = eg_ref[...].astype(og_ref.dtype)
    ou_ref[...] = eu_ref[...].astype(ou_ref.dtype)
    od_ref[...] = ed_ref[...].astype(od_ref.dtype)
    j = pl.program_id(1)
    z = z_ref[...].astype(F32)
    zh = halo_ref[...].astype(F32)
    uh = zh[:, :CONV_WIDTH] * _sigmoid(zh[:, CONV_WIDTH:])
    rows = CONV_HALO + ts
    u = jnp.concatenate([jnp.where(j > 0, uh, 0.0),
                         z[:, :CONV_WIDTH] * _sigmoid(z[:, CONV_WIDTH:])], axis=0)
    ubuf[0] = u
    for s in range(1, SUBLANES):
        ubuf[s] = pltpu.roll(u, rows - s, axis=0)
    sub = 64
    first = CONV_HALO - (CONV_KERNEL - 1)
    for r in range(ts // sub):
        acc = jnp.zeros((sub, CONV_WIDTH), F32) + cb_ref[...]
        for k in range(CONV_KERNEL):
            s = (first + k) % SUBLANES
            base = r * sub + first + k - s
            acc = acc + w_ref[k:k + 1, :] * ubuf[s, base:base + sub, :]
        mu = jnp.mean(acc, axis=-1, keepdims=True)
        dlt = acc - mu
        var = jnp.mean(dlt * dlt, axis=-1, keepdims=True)
        y = dlt * lax.rsqrt(var + LN_EPS) * g_ref[...] + b_ref[...]
        o_ref[r * sub:(r + 1) * sub, :] = (y * _sigmoid(y)).astype(o_ref.dtype)


def _conv(zc, conv_w, conv_b, ln_g, ln_b, we_gate, we_up, we_down, layer, nb, seq, ts=512):
    n = zc.shape[0]
    tps = seq // ts
    hpt = ts // CONV_HALO
    ne, d = we_gate.shape[1], we_gate.shape[2]
    eps = ne // (nb * tps)
    assert eps * nb * tps == ne
    vec = pl.BlockSpec((1, CONV_WIDTH), lambda b, j: (0, 0))
    w32 = lambda a, c: pl.BlockSpec((None, eps, a, c), lambda b, j: (layer, b * tps + j, 0, 0))
    w16 = lambda a, c: pl.BlockSpec((eps, a, c), lambda b, j: (b * tps + j, 0, 0))
    return pl.pallas_call(
        functools.partial(_conv_kernel, ts=ts),
        grid=(nb, tps),
        in_specs=[
            pl.BlockSpec((ts, COL_CONV), lambda b, j: (b * tps + j, 0)),
            pl.BlockSpec((CONV_HALO, COL_CONV), lambda b, j: (jnp.maximum((b * tps + j) * hpt - 1, 0), 0)),
            pl.BlockSpec((CONV_KERNEL, CONV_WIDTH), lambda b, j: (0, 0)),
            vec, vec, vec,
            w32(d, D_EXPERT), w32(d, D_EXPERT), w32(D_EXPERT, d),
        ],
        out_specs=[pl.BlockSpec((ts, CONV_WIDTH), lambda b, j: (b * tps + j, 0)),
                   w16(d, D_EXPERT), w16(d, D_EXPERT), w16(D_EXPERT, d)],
        out_shape=[jax.ShapeDtypeStruct((n, CONV_WIDTH), BF16), jax.ShapeDtypeStruct((ne, d, D_EXPERT), BF16),
                   jax.ShapeDtypeStruct((ne, d, D_EXPERT), BF16), jax.ShapeDtypeStruct((ne, D_EXPERT, d), BF16)],
        scratch_shapes=[pltpu.VMEM((SUBLANES, CONV_HALO + ts, CONV_WIDTH), F32)],
        compiler_params=_params("parallel", "parallel"),
        name="conv",
    )(zc, zc, conv_w, conv_b, ln_g, ln_b, we_gate, we_up, we_down)


def _prep_kernel(z_ref, halo_ref, mu_ref, w0_ref, a0_ref, wah_ref, wal_ref, g2_ref, kk_ref, ka_ref, rk_ref,
                 ones_ref, tri_ref, csel_ref,
                 at_ref, bt_ref, kt_ref, rt_ref, v_ref, bts_ref, kts_ref, bonus_ref, g_ref, pc_ref):
    j = pl.program_id(1)
    z = z_ref[...]
    ts = z.shape[0]
    last = jnp.where(j > 0, halo_ref[SHIFT_HALO - 1:SHIFT_HALO, :], 0.0)
    row = lax.broadcasted_iota(jnp.int32, z.shape, 0)
    prev = jnp.where(row == 0, last, pltpu.roll(z, 1, axis=0))
    zs = z + (prev - z) * mu_ref[...]
    rw = RWKV_WIDTH
    r = zs[:, :rw]
    k = zs[:, rw:2 * rw]
    v = zs[:, 2 * rw:3 * rw]
    wad = zs[:, 3 * rw:3 * rw + LORA_W + LORA_A]
    gd = zs[:, 3 * rw + LORA_W + LORA_A:]
    lane = lax.broadcasted_iota(jnp.int32, wad.shape, 1)
    xh, xl = _split(jnp.where(lane < LORA_W, jnp.tanh(wad), wad))
    wah = wah_ref[...]
    lora = _dot(xh, wah) + _dot(xl, wah) + _dot(xh, wal_ref[...])
    w = -_softplus(-(w0_ref[...] + lora[:, :rw])) - 0.5
    lw = -jnp.exp(w)
    a = _sigmoid(a0_ref[...] + lora[:, rw:])
    g_ref[...] = _dot(_sigmoid(gd).astype(BF16), g2_ref[...]).astype(g_ref.dtype)
    kk = k * kk_ref[...]
    k2 = k * (1.0 + (a - 1.0) * ka_ref[...])
    ones = ones_ref[...]
    kk = kk * lax.rsqrt(jnp.maximum(_dot_exact_rhs(kk * kk, ones), 1e-24))
    bonus_ref[...] = (_dot_exact_rhs(r * k2 * rk_ref[...], ones) * v).astype(bonus_ref.dtype)
    tri = tri_ref[...]
    lc, tot = [], []
    for h in range(ts // CUM_BLOCK):
        cs = _dot_exact_lhs(tri, lw[h * CUM_BLOCK:(h + 1) * CUM_BLOCK])
        lc.append(cs[:CUM_BLOCK])
        tot.append(cs[CUM_BLOCK:])
    lc = jnp.concatenate(lc, axis=0)
    tot = jnp.concatenate(tot, axis=0)
    e_neg = jnp.exp(-lc)
    e_end = jnp.exp(tot - lc)
    kka = kk * a
    dt = at_ref.dtype
    at_ref[...] = (-kk * jnp.exp(lc - lw)).astype(dt)
    bt_ref[...] = (kka * e_neg).astype(dt)
    kt_ref[...] = (k2 * e_neg).astype(dt)
    rt_ref[...] = (r * jnp.exp(lc)).astype(dt)
    v_ref[...] = v.astype(dt)
    bts_ref[...] = (kka * e_end).astype(dt)
    kts_ref[...] = (k2 * e_end).astype(dt)
    pc_ref[...] = jnp.exp(_dot_exact_lhs(csel_ref[...], lw))


def _prep(zr, mu, w0, a0, wa, g2, k_k, k_a, r_k, nb, seq, ts=512):
    n = zr.shape[0]
    tps = seq // ts
    hpt = ts // SHIFT_HALO
    rw = RWKV_WIDTH
    cpt = ts // CHUNK
    ones = jnp.kron(jnp.eye(N_HEADS, dtype=F32), jnp.ones((HEAD, HEAD), F32)).astype(BF16)
    blk = jnp.kron(jnp.eye(CUM_BLOCK // CHUNK, dtype=F32), jnp.ones((CHUNK, CHUNK), F32))
    tri = jnp.concatenate([jnp.tril(blk), blk], axis=0).astype(BF16)
    csel = jnp.kron(jnp.eye(cpt, dtype=F32), jnp.ones((1, CHUNK), F32)).astype(BF16)
    wa_hi, wa_lo = _split(wa)
    vec = pl.BlockSpec((1, rw), lambda b, j: (0, 0))
    full = lambda shape: pl.BlockSpec(shape, lambda b, j: (0, 0))
    rows = pl.BlockSpec((ts, rw), lambda b, j: (b * tps + j, 0))
    return pl.pallas_call(
        _prep_kernel,
        grid=(nb, tps),
        in_specs=[
            pl.BlockSpec((ts, COL_RWKV), lambda b, j: (b * tps + j, 0)),
            pl.BlockSpec((SHIFT_HALO, COL_RWKV), lambda b, j: (jnp.maximum((b * tps + j) * hpt - 1, 0), 0)),
            full((1, COL_RWKV)), vec, vec,
            full((LORA_W + LORA_A, 2 * rw)), full((LORA_W + LORA_A, 2 * rw)), full((LORA_G, rw)),
            vec, vec, vec, full((rw, rw)), full((2 * CUM_BLOCK, CUM_BLOCK)), full((cpt, ts)),
        ],
        out_specs=[rows] * 9 + [pl.BlockSpec((cpt, rw), lambda b, j: (b * tps + j, 0))],
        out_shape=[jax.ShapeDtypeStruct((n, rw), BF16)] * 9 + [jax.ShapeDtypeStruct((n // CHUNK, rw), F32)],
        compiler_params=_params("parallel", "parallel"),
        name="prep",
    )(zr, zr, mu, w0, a0, wa_hi, wa_lo, g2.astype(BF16), k_k, k_a, r_k, ones, tri, csel)


def _mm(a, b):
    return jnp.dot(a.astype(BF16), b.astype(BF16), preferred_element_type=F32)


def _mm_nt(a, b):
    return lax.dot_general(a.astype(BF16), b.astype(BF16), (((1,), (1,)), ((), ())),
                           preferred_element_type=F32)


def _mm_tn(a, b):
    return lax.dot_general(a.astype(BF16), b.astype(BF16), (((0,), (0,)), ((), ())),
                           preferred_element_type=F32)


def _scan_kernel(at_ref, bt_ref, kt_ref, rt_ref, v_ref, bts_ref, kts_ref, bonus_ref, g_ref, pc_ref,
                 gng_ref, gnb_ref, ones_ref, o_ref, s_ref, y_ref, *, ts, gb):
    @pl.when(pl.program_id(1) == 0)
    def _():
        s_ref[...] = jnp.zeros_like(s_ref)

    row_q = lax.broadcasted_iota(jnp.int32, (QUAD, QUAD), 0)
    col_q = lax.broadcasted_iota(jnp.int32, (QUAD, QUAD), 1)
    bdmask = (row_q // HEAD) == (col_q // HEAD)
    row_c = lax.broadcasted_iota(jnp.int32, (CHUNK, QUAD), 0)
    col_c = lax.broadcasted_iota(jnp.int32, (CHUNK, QUAD), 1) % CHUNK
    strict = col_c < row_c
    incl = col_c <= row_c
    eye = jnp.where(col_c == row_c, 1.0, 0.0).astype(F32)

    def bd(x):
        xb = x.astype(BF16)
        return jnp.where(bdmask, jnp.concatenate([xb] * (QUAD // CHUNK), axis=0), jnp.zeros((), BF16))

    chains = [(b, qd) for b in range(gb) for qd in range(N_QUADS)]

    def chunk(c, carry):
        sl = pl.ds(pl.multiple_of(c * CHUNK, CHUNK), CHUNK)
        lanes = [slice(qd * QUAD, (qd + 1) * QUAD) for _, qd in chains]
        ld = lambda ref: [ref[b, sl, ln] for (b, _), ln in zip(chains, lanes)]
        at, bt, kt, rt, v, bts, kts = (ld(r) for r in (at_ref, bt_ref, kt_ref, rt_ref, v_ref, bts_ref, kts_ref))
        each = range(len(chains))
        ar = [jnp.concatenate([at[i], rt[i]], axis=0) for i in each]
        pb = [_mm_nt(ar[i], bd(bt[i])) for i in each]
        pk = [_mm_nt(ar[i], bd(kt[i])) for i in each]
        l_ab = [jnp.where(strict, p[:CHUNK], 0.0) for p in pb]
        a_rb = [jnp.where(incl, p[CHUNK:], 0.0) for p in pb]
        l_ak = [jnp.where(strict, p[:CHUNK], 0.0) for p in pk]
        a_rk = [jnp.where(incl, p[CHUNK:], 0.0) for p in pk]
        tm = [eye + l for l in l_ab]
        lp = [_mm(l, bd(l)) for l in l_ab]
        for _ in range(4):
            both = [_mm(jnp.concatenate([tm[i], lp[i]], axis=0), bd(lp[i])) for i in each]
            tm = [tm[i] + both[i][:CHUNK] for i in each]
            lp = [both[i][CHUNK:] for i in each]
        tm = [tm[i] + _mm(tm[i], bd(lp[i])) for i in each]
        bdv = [bd(x) for x in v]
        x0 = [_mm(l_ak[i], bdv[i]) for i in each]
        tu = [_mm(tm[i], jnp.concatenate([bd(x0[i]), bd(at[i])], axis=1)) for i in each]
        u0 = [t[:, :QUAD] for t in tu]
        w = [t[:, QUAD:] for t in tu]
        qy = [_mm(a_rb[i], jnp.concatenate([bd(w[i]), bd(u0[i])], axis=1)) for i in each]
        y0 = [qy[i][:, QUAD:] + _mm(a_rk[i], bdv[i]) for i in each]
        q = [rt[i].astype(F32) + qy[i][:, :QUAD] for i in each]
        s = [s_ref[i] for i in each]
        qw_s = [_mm_nt(jnp.concatenate([q[i], w[i]], axis=0), s[i]) for i in each]
        for i, ((b, _), ln) in enumerate(zip(chains, lanes)):
            y_ref[b, sl, ln] = y0[i] + qw_s[i][:CHUNK]
        u = [(u0[i] + qw_s[i][CHUNK:]).astype(BF16) for i in each]
        n_p = [jnp.where(bdmask, _mm_tn(jnp.concatenate([u[i], v[i]], axis=0),
                                        jnp.concatenate([bts[i], kts[i]], axis=0)), 0.0) for i in each]
        for i, ((b, _), ln) in enumerate(zip(chains, lanes)):
            s_ref[i] = s[i] * pc_ref[b, pl.ds(c, 1), ln] + n_p[i]
        return carry

    lax.fori_loop(0, ts // CHUNK, chunk, 0)

    ones = ones_ref[...]
    for b in range(gb):
        for qd in range(N_QUADS):
            ln = slice(qd * QUAD, (qd + 1) * QUAD)
            y = y_ref[b, :, ln]
            dlt = y - _dot_exact_rhs(y, ones) * (1.0 / HEAD)
            var = _dot_exact_rhs(dlt * dlt, ones) * (1.0 / HEAD)
            yn = dlt * lax.rsqrt(var + GN_EPS) * gng_ref[:, ln] + gnb_ref[:, ln]
            o_ref[b, :, ln] = ((yn + bonus_ref[b, :, ln].astype(F32)) * g_ref[b, :, ln].astype(F32)).astype(o_ref.dtype)


def _scan(prep_out, gn_g, gn_b, nb, seq, ts=128, gb=8):
    n = prep_out[0].shape[0]
    rw = RWKV_WIDTH
    cpt = ts // CHUNK
    ones = jnp.kron(jnp.eye(QUAD // HEAD, dtype=F32), jnp.ones((HEAD, HEAD), F32)).astype(BF16)
    rows = pl.BlockSpec((gb, ts, rw), lambda i, t: (i, t, 0))
    vec = pl.BlockSpec((1, rw), lambda i, t: (0, 0))
    args = [a.reshape(nb, seq, rw) for a in prep_out[:9]] + [prep_out[9].reshape(nb, seq // ts, cpt, rw)]
    out = pl.pallas_call(
        functools.partial(_scan_kernel, ts=ts, gb=gb),
        grid=(nb // gb, seq // ts),
        in_specs=[rows] * 9 + [pl.BlockSpec((gb, None, cpt, rw), lambda i, t: (i, t, 0, 0)), vec, vec,
                               pl.BlockSpec((QUAD, QUAD), lambda i, t: (0, 0))],
        out_specs=rows,
        out_shape=jax.ShapeDtypeStruct((nb, seq, rw), BF16),
        scratch_shapes=[pltpu.VMEM((gb * N_QUADS, QUAD, QUAD), F32), pltpu.VMEM((gb, ts, rw), F32)],
        compiler_params=_params("parallel", "arbitrary"),
        name="scan",
    )(*args, gn_g, gn_b, ones)
    return out.reshape(n, rw)


def _merge_kernel(uc_ref, ur_ref, zg_ref, x_ref, pc_ref, pr_ref, wo_ref, gpost_ref, gt_ref,
                  gpre_ref, sc_ref, sh_ref, wrh_ref, wrl_ref, xo_ref, h_ref, lg_ref):
    yc = _dot(uc_ref[...], pc_ref[...])
    yr = _dot(ur_ref[...], pr_ref[...])
    zg = zg_ref[...].astype(F32)
    m = _sigmoid(zg[:, :D_MODEL]) * yc + _sigmoid(zg[:, D_MODEL:]) * yr
    y = _dot(m.astype(BF16), wo_ref[...])
    xn = x_ref[...] + gt_ref[...] * _rms(y, gpost_ref[...])
    xo_ref[...] = xn
    h = _rms(xn, gpre_ref[...]) * (1.0 + sc_ref[...]) + sh_ref[...]
    h_ref[...] = _pack_rows(h)
    h_hi, h_lo = _split(h)
    nt = lambda a, b: lax.dot_general(a, b, (((1,), (1,)), ((), ())), preferred_element_type=F32)
    w_hi = wrh_ref[...]
    lg_ref[...] = nt(w_hi, h_hi) + nt(w_hi, h_lo) + nt(wrl_ref[...], h_hi)


def _merge(uc, ur, zg, x2, p_conv_b, p_rwkv_b, w_o_b, g_post, g_pre, mod, layer, w_router_t, seq, tm=512):
    n, d = x2.shape
    row = lambda i: (i, 0)
    full = lambda shape: pl.BlockSpec(shape, lambda i: (0, 0))
    return pl.pallas_call(
        _merge_kernel,
        grid=(n // tm,),
        in_specs=[
            pl.BlockSpec((tm, CONV_WIDTH), row), pl.BlockSpec((tm, RWKV_WIDTH), row),
            pl.BlockSpec((tm, COL_GATE), row), pl.BlockSpec((tm, d), row),
            full((CONV_WIDTH, d)), full((RWKV_WIDTH, d)), full((d, d)),
            full((1, d)), _mod_spec(layer, 2, seq, tm),
            full((1, d)), _mod_spec(layer, 4, seq, tm), _mod_spec(layer, 3, seq, tm),
            full((N_EXPERTS, d)), full((N_EXPERTS, d)),
        ],
        out_specs=[pl.BlockSpec((tm, d), row), pl.BlockSpec((tm, ROW_WORDS), row),
                   pl.BlockSpec((N_EXPERTS, tm), lambda i: (0, i))],
        out_shape=[jax.ShapeDtypeStruct((n, d), F32), jax.ShapeDtypeStruct((n, ROW_WORDS), jnp.int32),
                   jax.ShapeDtypeStruct((N_EXPERTS, n), F32)],
        compiler_params=_params("parallel"),
        name="merge",
    )(uc, ur, zg, x2, p_conv_b, p_rwkv_b, w_o_b, g_post, mod, g_pre, mod, mod, *_split(w_router_t))


def _route_kernel(lg_ref, b_ref, gate_ref, sel_ref, cnt_ref):
    s = _sigmoid(lg_ref[...])
    biased = s + b_ref[...]
    t = s.shape[1]
    member = lax.broadcasted_iota(jnp.int32, (GROUP_SIZE, t), 0)
    grp = []
    for g in range(N_GROUPS):
        bg = biased[g * GROUP_SIZE:(g + 1) * GROUP_SIZE, :]
        m1 = jnp.max(bg, axis=0, keepdims=True)
        first = jnp.min(jnp.where(bg == m1, member, GROUP_SIZE), axis=0, keepdims=True)
        m2 = jnp.max(jnp.where(member == first, -jnp.inf, bg), axis=0, keepdims=True)
        grp.append(m1 + m2)
    masked = []
    for g in range(N_GROUPS):
        rank = jnp.zeros((1, t), jnp.int32)
        for o in range(N_GROUPS):
            if o == g:
                continue
            ahead = (grp[o] > grp[g]) if o > g else (grp[o] >= grp[g])
            rank = rank + jnp.where(ahead, 1, 0)
        keep = rank < TOPK_GROUPS
        masked.append(jnp.where(keep, biased[g * GROUP_SIZE:(g + 1) * GROUP_SIZE, :], -jnp.inf))
    masked = jnp.concatenate(masked, axis=0)
    eidx = lax.broadcasted_iota(jnp.int32, masked.shape, 0)
    mask = jnp.zeros(masked.shape, F32)
    for _ in range(TOP_K):
        top = jnp.max(masked, axis=0, keepdims=True)
        hit = eidx == jnp.min(jnp.where(masked == top, eidx, N_EXPERTS), axis=0, keepdims=True)
        mask = jnp.where(hit, 1.0, mask)
        masked = jnp.where(hit, -jnp.inf, masked)
    sel = s * mask
    gate_ref[...] = sel / jnp.sum(sel, axis=0, keepdims=True) * ROUTED_SCALE
    sel_ref[...] = mask.astype(sel_ref.dtype)

    @pl.when(pl.program_id(0) == 0)
    def _():
        cnt_ref[...] = jnp.zeros_like(cnt_ref)

    cnt_ref[...] += jnp.sum(mask, axis=1, keepdims=True)


def _route(logits_t, b_router, tt=512):
    e, n = logits_t.shape
    return pl.pallas_call(
        _route_kernel,
        grid=(n // tt,),
        in_specs=[pl.BlockSpec((e, tt), lambda i: (0, i)), pl.BlockSpec((e, 1), lambda i: (0, 0))],
        out_specs=[pl.BlockSpec((e, tt), lambda i: (0, i)), pl.BlockSpec((e, tt), lambda i: (0, i)),
                   pl.BlockSpec((e, GATE_LANES), lambda i: (0, 0))],
        out_shape=[jax.ShapeDtypeStruct((e, n), F32), jax.ShapeDtypeStruct((e, n), BF16),
                   jax.ShapeDtypeStruct((e, GATE_LANES), F32)],
        compiler_params=_params("arbitrary"),
        name="route",
    )(logits_t, b_router)


def _dest_kernel(sel_ref, gate_ref, start_ref, triu_ref, below_ref, dest_ref, g8_ref, run_ref):
    @pl.when(pl.program_id(0) == 0)
    def _():
        run_ref[...] = jnp.zeros_like(run_ref)

    sel = sel_ref[...]
    t = sel.shape[1]
    self32 = sel.astype(F32)
    incl = _dot(sel, triu_ref[...])
    pos = start_ref[...] + run_ref[...] + incl - self32
    run_ref[...] += incl[:, t - 1:t]
    slot = _dot(below_ref[...], sel)
    gate = gate_ref[...]
    dst, gts = [], []
    for k in range(TOP_K):
        mine = (self32 > 0.0) & (slot == float(k))
        dst.append(jnp.sum(jnp.where(mine, pos, 0.0), axis=0, keepdims=True))
        gts.append(jnp.sum(jnp.where(mine, gate, 0.0), axis=0, keepdims=True))
    dest_ref[...] = jnp.concatenate(dst, axis=0).astype(jnp.int32)
    g8_ref[...] = jnp.concatenate(gts + [jnp.zeros((GATE_LANES - TOP_K, t), F32)], axis=0).T


def _dest(sel, gate, seg_start, tt=512):
    e, n = sel.shape
    triu = jnp.triu(jnp.ones((tt, tt), F32)).astype(BF16)
    below = jnp.tril(jnp.ones((e, e), F32), -1).astype(BF16)
    return pl.pallas_call(
        _dest_kernel,
        grid=(n // tt,),
        in_specs=[pl.BlockSpec((e, tt), lambda i: (0, i)), pl.BlockSpec((e, tt), lambda i: (0, i)),
                  pl.BlockSpec((e, 1), lambda i: (0, 0)), pl.BlockSpec((tt, tt), lambda i: (0, 0)),
                  pl.BlockSpec((e, e), lambda i: (0, 0))],
        out_specs=[pl.BlockSpec((TOP_K, tt), lambda i: (0, i)), pl.BlockSpec((tt, GATE_LANES), lambda i: (i, 0))],
        out_shape=[jax.ShapeDtypeStruct((TOP_K, n), jnp.int32), jax.ShapeDtypeStruct((n, GATE_LANES), F32)],
        scratch_shapes=[pltpu.VMEM((e, 1), F32)],
        compiler_params=_params("arbitrary"),
        name="dest",
    )(sel, gate, seg_start, triu, below)


def _sc_mesh():
    return plsc.VectorSubcoreMesh(core_axis_name="c", subcore_axis_name="s",
                                  num_cores=SC_CORES, num_subcores=SC_SUBCORES)


def _worker_id():
    return lax.axis_index("s") * SC_CORES + lax.axis_index("c")


def _sc_scatter_rows(rows, dest, n_out):
    n, w = rows.shape
    per_worker = n // SC_WORKERS
    chunks_per_worker = per_worker // SC_CHUNK
    idx_rows = dest.reshape(TOP_K, n // SC_CHUNK, SC_CHUNK).transpose(1, 0, 2).reshape(-1, SC_CHUNK)

    def body(rows_hbm, idx_hbm, out_hbm, idx_v, rows_v):
        wid = _worker_id()

        @pl.loop(0, chunks_per_worker)
        def _(c):
            chunk = wid * chunks_per_worker + c
            pltpu.sync_copy(rows_hbm.at[pl.ds(chunk * SC_CHUNK, SC_CHUNK)], rows_v)
            pltpu.sync_copy(idx_hbm.at[pl.ds(chunk * TOP_K, TOP_K)], idx_v)
            for k in range(TOP_K):
                pltpu.sync_copy(rows_v, out_hbm.at[idx_v.at[k]])

    return pl.kernel(
        body, out_type=jax.ShapeDtypeStruct((n_out, w), rows.dtype), mesh=_sc_mesh(),
        scratch_types=[pltpu.VMEM((TOP_K, SC_CHUNK), jnp.int32), pltpu.VMEM((SC_CHUNK, w), rows.dtype)],
        name="sc_scatter",
    )(rows, idx_rows)


def _sc_gather_rows(table, idx_flat):
    m = idx_flat.shape[0]
    w = table.shape[1]
    ch = SC_CHUNK // 2
    chunks_per_worker = m // SC_WORKERS // ch
    idx_rows = idx_flat.reshape(-1, ch)
    ng = SC_IDX_ROWS

    def body(table_hbm, idx_hbm, out_hbm, idx_v, rows_a, rows_b, gat_a, gat_b, put_a, put_b):
        wid = _worker_id()
        bufs = ((rows_a, gat_a, put_a), (rows_b, gat_b, put_b))

        @pl.loop(0, chunks_per_worker // ng)
        def _(g):
            first = wid * chunks_per_worker + g * ng
            pltpu.sync_copy(idx_hbm.at[pl.ds(first, ng)], idx_v)
            gathers, stores = [None] * ng, [None] * ng
            gathers[0] = pltpu.async_copy(table_hbm.at[idx_v.at[0]], rows_a, gat_a)
            for j in range(ng):
                buf, _, put = bufs[j % 2]
                if j + 1 < ng:
                    if j >= 1:
                        stores[j - 1].wait()
                    nxt, gat, _ = bufs[(j + 1) % 2]
                    gathers[j + 1] = pltpu.async_copy(table_hbm.at[idx_v.at[j + 1]], nxt, gat)
                gathers[j].wait()
                stores[j] = pltpu.async_copy(buf, out_hbm.at[pl.ds((first + j) * ch, ch)], put)
            stores[ng - 2].wait()
            stores[ng - 1].wait()

    return pl.kernel(
        body, out_type=jax.ShapeDtypeStruct((m, w), table.dtype), mesh=_sc_mesh(),
        scratch_types=[pltpu.VMEM((ng, ch), jnp.int32), pltpu.VMEM((ch, w), table.dtype),
                       pltpu.VMEM((ch, w), table.dtype)] + [pltpu.SemaphoreType.DMA] * 4,
        name="sc_gather",
    )(table, idx_rows)


def _swiglu(h, wg, wu):
    gte = _dot(h, wg)
    return gte * _sigmoid(gte) * _dot(h, wu)


def _experts_kernel(be_ref, va_ref, xs_ref, *refs):
    nb = BLOCKS_PER_STEP
    wg, wu, wd, ys_ref = refs[:nb], refs[nb:2 * nb], refs[2 * nb:3 * nb], refs[3 * nb]
    first = pl.program_id(0) * nb

    @pl.when(va_ref[first] > 0)
    def _():
        row = lax.broadcasted_iota(jnp.int32, (EXPERT_SUB, D_MODEL), 0)
        subs = [(b, r0) for b in range(nb) for r0 in range(0, EXPERT_BLOCK, EXPERT_SUB)]
        x = []
        for b, r0 in subs:
            lo = b * EXPERT_BLOCK + r0
            xi = _unpack_rows(xs_ref[lo:lo + EXPERT_SUB, :])
            x.append(jnp.where(row < va_ref[first + b] - r0, xi, 0.0).astype(BF16))
        gte = [_dot(xi, wg[b][...]) for xi, (b, _) in zip(x, subs)]
        up = [_dot(xi, wu[b][...]) for xi, (b, _) in zip(x, subs)]
        act = [(g * _sigmoid(g) * u).astype(BF16) for g, u in zip(gte, up)]
        y = [_dot(a, wd[b][...]) for a, (b, _) in zip(act, subs)]
        for yi, (b, r0) in zip(y, subs):
            lo = b * EXPERT_BLOCK + r0
            ys_ref[lo:lo + EXPERT_SUB, :] = _pack_rows(yi)


def _experts(xs, blk_expert, blk_valid, wg, wu, wd):
    n_rows, w = xs.shape
    d = wg.shape[1]
    nb = BLOCKS_PER_STEP
    rows = pl.BlockSpec((nb * EXPERT_BLOCK, w), lambda s, be, va: (s, 0))
    up_spec = lambda i: pl.BlockSpec((None, d, D_EXPERT), lambda s, be, va: (be[s * nb + i], 0, 0))
    down_spec = lambda i: pl.BlockSpec((None, D_EXPERT, d), lambda s, be, va: (be[s * nb + i], 0, 0))
    return pl.pallas_call(
        _experts_kernel,
        grid_spec=pltpu.PrefetchScalarGridSpec(
            num_scalar_prefetch=2,
            grid=(n_rows // (nb * EXPERT_BLOCK),),
            in_specs=[rows] + [up_spec(i) for i in range(nb)] * 2 + [down_spec(i) for i in range(nb)],
            out_specs=rows,
        ),
        out_shape=jax.ShapeDtypeStruct((n_rows, w), xs.dtype),
        compiler_params=_params("arbitrary"),
        name="experts",
    )(blk_expert, blk_valid, xs, *([wg] * nb), *([wu] * nb), *([wd] * nb))


def _combine_kernel(yg_ref, g8_ref, hp_ref, sg_ref, su_ref, sd_ref, x_ref, gpost_ref, gt_ref, o_ref):
    h = _unpack_rows(hp_ref[...]).astype(BF16)
    acc = _dot(_swiglu(h, sg_ref[...], su_ref[...]).astype(BF16), sd_ref[...])
    g8 = g8_ref[...]
    for k in range(TOP_K):
        acc = acc + g8[:, k:k + 1] * _unpack_rows(yg_ref[k])
    o_ref[...] = x_ref[...] + gt_ref[...] * _rms(acc, gpost_ref[...])


def _combine(yg, g8, hp, sg, su, sd, x2, g_post, mod, layer, seq, tm=512):
    n, d = x2.shape
    row = lambda i: (i, 0)
    full = lambda shape: pl.BlockSpec(shape, lambda i: (0, 0))
    in_specs = [
        pl.BlockSpec((TOP_K, tm, ROW_WORDS), lambda i: (0, i, 0)),
        pl.BlockSpec((tm, GATE_LANES), row),
        pl.BlockSpec((tm, ROW_WORDS), row),
        full((d, D_EXPERT)), full((d, D_EXPERT)), full((D_EXPERT, d)),
        pl.BlockSpec((tm, d), row),
        full((1, d)),
        _mod_spec(layer, 5, seq, tm),
    ]
    return pl.pallas_call(
        _combine_kernel,
        grid=(n // tm,),
        in_specs=in_specs,
        out_specs=pl.BlockSpec((tm, d), row),
        out_shape=jax.ShapeDtypeStruct((n, d), F32),
        compiler_params=_params("parallel"),
        name="combine",
    )(yg, g8, hp, sg, su, sd, x2, g_post, mod)


def _moe(hp, logits_t, b_router, wg, wu, wd, sg, su, sd, x2, g_post, mod, layer, seq):
    n = x2.shape[0]
    gate, sel, counts = _route(logits_t, b_router)
    cnt = counts[:, 0].astype(jnp.int32)
    padded = (cnt + EXPERT_BLOCK - 1) // EXPERT_BLOCK * EXPERT_BLOCK
    seg_end = jnp.cumsum(padded)
    seg_start = seg_end - padded
    n_blocks = n * TOP_K // EXPERT_BLOCK + N_EXPERTS
    blk_row = jnp.arange(n_blocks, dtype=jnp.int32) * EXPERT_BLOCK
    blk_expert = jnp.sum((seg_end[None, :] <= blk_row[:, None]).astype(jnp.int32), axis=1)
    blk_expert = jnp.minimum(blk_expert, N_EXPERTS - 1)
    mine = blk_expert[:, None] == jnp.arange(N_EXPERTS, dtype=jnp.int32)[None, :]
    seg_last = jnp.sum(jnp.where(mine, (seg_start + cnt)[None, :], 0), axis=1)
    blk_valid = jnp.clip(seg_last - blk_row, 0, EXPERT_BLOCK).astype(jnp.int32)
    dest, g8 = _dest(sel, gate, seg_start.astype(F32).reshape(-1, 1))
    dest_flat = dest.reshape(-1)
    xs = _sc_scatter_rows(hp, dest, n_blocks * EXPERT_BLOCK)
    ys = _experts(xs, blk_expert, blk_valid, wg, wu, wd)
    yg = _sc_gather_rows(ys, dest_flat).reshape(TOP_K, n, ROW_WORDS)
    return _combine(yg, g8, hp, sg, su, sd, x2, g_post, mod, layer, seq)


def kernel(x, c, w_ada, b_ada, norm_mix_pre, norm_mix_post, norm_ffn_pre, norm_ffn_post, w_in, mu_shift, conv_w, conv_b, conv_ln_g, conv_ln_b, p_conv, w0, w2, a0, a2, g2, k_k, k_a, r_k, gn_g, gn_b, p_rwkv, w_o, w_router, b_router, we_gate, we_up, we_down, ws_gate, ws_up, ws_down):
    nb, seq, d = x.shape
    depth = w_ada.shape[0]
    n = nb * seq
    rw = RWKV_WIDTH
    mod = _ada(c, w_ada, b_ada)
    x2 = x.reshape(n, d)
    row = lambda a: a.reshape(1, -1)
    bf = lambda a: a.astype(BF16)
    for l in range(depth):
        zc, zr, zg = _win(x2, row(norm_mix_pre[l]), mod, l, bf(w_in[l]), seq)
        uc, wg_b, wu_b, wd_b = _conv(zc, conv_w[l], row(conv_b[l]), row(conv_ln_g[l]), row(conv_ln_b[l]),
                                     we_gate, we_up, we_down, l, nb, seq)
        wa = jnp.zeros((LORA_W + LORA_A, 2 * rw), F32)
        wa = wa.at[:LORA_W, :rw].set(w2[l]).at[LORA_W:, rw:].set(a2[l])
        prep_out = _prep(zr, row(mu_shift[l]), row(w0[l]), row(a0[l]), wa, g2[l], row(k_k[l]),
                         row(k_a[l]), row(r_k[l]), nb, seq)
        ur = _scan(prep_out, row(gn_g[l]), row(gn_b[l]), nb, seq)
        x2, hp, logits_t = _merge(uc, ur, zg, x2, bf(p_conv[l]), bf(p_rwkv[l]), bf(w_o[l]),
                                  row(norm_mix_post[l]), row(norm_ffn_pre[l]), mod, l, w_router[l].T, seq)
        x2 = _moe(hp, logits_t, b_router[l].reshape(-1, 1), wg_b, wu_b, wd_b, bf(ws_gate[l]), bf(ws_up[l]),
                  bf(ws_down[l]), x2, row(norm_ffn_post[l]), mod, l, seq)
    return x2.reshape(nb, seq, d)
```

```python
import functools

import jax
import jax.numpy as jnp
from jax import lax
from jax.experimental import pallas as pl
from jax.experimental.pallas import tpu as pltpu
from jax.experimental.pallas import tpu_sc as plsc

F32 = jnp.float32
BF16 = jnp.bfloat16

D_MODEL = 1024
CONV_WIDTH = 512
CONV_KERNEL = 31
RWKV_WIDTH = 512
HEAD = 64
N_HEADS = RWKV_WIDTH // HEAD
LORA_W = 64
LORA_A = 64
LORA_G = 128
N_EXPERTS = 64
TOP_K = 8
N_GROUPS = 8
TOPK_GROUPS = 4
GROUP_SIZE = N_EXPERTS // N_GROUPS
D_EXPERT = 256
ROUTED_SCALE = 2.5
RMS_EPS = 1e-6
LN_EPS = 1e-5
GN_EPS = 64e-5
COL_CONV = 2 * CONV_WIDTH
COL_RWKV = 3 * RWKV_WIDTH + LORA_W + LORA_A + LORA_G
COL_GATE = 2 * D_MODEL
D_IN = COL_CONV + COL_RWKV + COL_GATE

CHUNK = 64
QUAD = 4 * HEAD
N_QUADS = RWKV_WIDTH // QUAD
CUM_BLOCK = 256
CONV_HALO = 32
SUBLANES = 8
SHIFT_HALO = SUBLANES
GATE_LANES = 128
EXPERT_BLOCK = 512
EXPERT_SUB = 256
BLOCKS_PER_STEP = 4
ROW_WORDS = D_MODEL // 2
SC_CORES = 2
SC_SUBCORES = 16
SC_WORKERS = SC_CORES * SC_SUBCORES
SC_CHUNK = 128
SC_IDX_ROWS = 8
VMEM_LIMIT = 52 * 1024 * 1024


def _params(*sem):
    return pltpu.CompilerParams(dimension_semantics=sem, vmem_limit_bytes=VMEM_LIMIT)


def _sigmoid(x):
    return 1.0 / (1.0 + jnp.exp(-x))


def _softplus(x):
    return jnp.maximum(x, 0.0) + jnp.log(1.0 + jnp.exp(-jnp.abs(x)))


def _rms(x, g):
    return x * lax.rsqrt(jnp.mean(x * x, axis=-1, keepdims=True) + RMS_EPS) * g


def _split(x):
    hi = x.astype(BF16)
    return hi, (x - hi.astype(F32)).astype(BF16)


def _dot(a, b):
    return jnp.dot(a, b, preferred_element_type=F32)


def _dot_exact_rhs(x, w):
    hi, lo = _split(x)
    return _dot(hi, w) + _dot(lo, w)


def _pack_rows(y):
    half = y.shape[1] // 2
    packed = pltpu.pack_elementwise([y[:, half:], y[:, :half]], packed_dtype=BF16)
    return lax.bitcast_convert_type(packed, jnp.int32)


def _unpack_rows(u):
    hi = pltpu.unpack_elementwise(u, index=1, packed_dtype=BF16, unpacked_dtype=F32)
    lo = pltpu.unpack_elementwise(u, index=0, packed_dtype=BF16, unpacked_dtype=F32)
    return jnp.concatenate([hi, lo], axis=1)


def _dot_exact_lhs(w, x):
    hi, lo = _split(x)
    return _dot(w, hi) + _dot(w, lo)


def _ada_kernel(c_ref, w_ref, b_ref, o_ref):
    c = c_ref[...]
    a_hi, a_lo = _split(c * _sigmoid(c))
    w_hi, w_lo = _split(w_ref[...])
    o_ref[...] = _dot(a_hi, w_hi) + _dot(a_lo, w_hi) + _dot(a_hi, w_lo) + b_ref[...]


def _ada(c, w_ada, b_ada):
    nl, d, _ = w_ada.shape
    nb = c.shape[0]
    out = pl.pallas_call(
        _ada_kernel,
        grid=(nl, 6),
        in_specs=[
            pl.BlockSpec((nb, d), lambda l, k: (0, 0)),
            pl.BlockSpec((None, d, d), lambda l, k: (l, 0, k)),
            pl.BlockSpec((None, None, 1, d), lambda l, k: (l, k, 0, 0)),
        ],
        out_specs=pl.BlockSpec((None, None, nb, d), lambda l, k: (l, k, 0, 0)),
        out_shape=jax.ShapeDtypeStruct((nl, 6, nb, d), F32),
        compiler_params=_params("parallel", "parallel"),
        name="ada",
    )(c, w_ada, b_ada.reshape(nl, 6, 1, d))
    return out.reshape(nl, 6, nb, 1, d)


def _mod_spec(layer, piece, rows_per_batch, tm):
    return pl.BlockSpec((None, None, None, 1, D_MODEL),
                        lambda i, *_: (layer, piece, (i * tm) // rows_per_batch, 0, 0))


def _win_kernel(x_ref, g_ref, sc_ref, sh_ref, w_ref, zc_ref, zr_ref, zg_ref):
    h = _rms(x_ref[...], g_ref[...]) * (1.0 + sc_ref[...]) + sh_ref[...]
    hb = h.astype(BF16)
    zc_ref[...] = _dot(hb, w_ref[:, :COL_CONV]).astype(zc_ref.dtype)
    zr_ref[...] = _dot(hb, w_ref[:, COL_CONV:COL_CONV + COL_RWKV])
    zg_ref[...] = _dot(hb, w_ref[:, COL_CONV + COL_RWKV:]).astype(zg_ref.dtype)


def _win(x2, g, mod, layer, w_in_b, seq, tm=512):
    n, d = x2.shape
    row = lambda i: (i, 0)
    return pl.pallas_call(
        _win_kernel,
        grid=(n // tm,),
        in_specs=[
            pl.BlockSpec((tm, d), row),
            pl.BlockSpec((1, d), lambda i: (0, 0)),
            _mod_spec(layer, 1, seq, tm),
            _mod_spec(layer, 0, seq, tm),
            pl.BlockSpec((d, D_IN), lambda i: (0, 0)),
        ],
        out_specs=[pl.BlockSpec((tm, COL_CONV), row), pl.BlockSpec((tm, COL_RWKV), row),
                   pl.BlockSpec((tm, COL_GATE), row)],
        out_shape=[jax.ShapeDtypeStruct((n, COL_CONV), BF16), jax.ShapeDtypeStruct((n, COL_RWKV), F32),
                   jax.ShapeDtypeStruct((n, COL_GATE), BF16)],
        compiler_params=_params("parallel"),
        name="win",
    )(x2, g, mod, mod, w_in_b)


def _conv_kernel(z_ref, halo_ref, w_ref, cb_ref, g_ref, b_ref, eg_ref, eu_ref, ed_ref,
                 o_ref, og_ref, ou_ref, od_ref, ubuf, *, ts):
    og_ref[...] = eg_ref[...].astype(og_ref.dtype)
    ou_ref[...] = eu_ref[...].astype(ou_ref.dtype)
    od_ref[...] = ed_ref[...].astype(od_ref.dtype)
    j = pl.program_id(1)
    z = z_ref[...].astype(F32)
    zh = halo_ref[...].astype(F32)
    uh = zh[:, :CONV_WIDTH] * _sigmoid(zh[:, CONV_WIDTH:])
    rows = CONV_HALO + ts
    u = jnp.concatenate([jnp.where(j > 0, uh, 0.0),
                         z[:, :CONV_WIDTH] * _sigmoid(z[:, CONV_WIDTH:])], axis=0)
    ubuf[0] = u
    for s in range(1, SUBLANES):
        ubuf[s] = pltpu.roll(u, rows - s, axis=0)
    sub = 64
    first = CONV_HALO - (CONV_KERNEL - 1)
    for r in range(ts // sub):
        acc = jnp.zeros((sub, CONV_WIDTH), F32) + cb_ref[...]
        for k in range(CONV_KERNEL):
            s = (first + k) % SUBLANES
            base = r * sub + first + k - s
            acc = acc + w_ref[k:k + 1, :] * ubuf[s, base:base + sub, :]
        mu = jnp.mean(acc, axis=-1, keepdims=True)
        dlt = acc - mu
        var = jnp.mean(dlt * dlt, axis=-1, keepdims=True)
        y = dlt * lax.rsqrt(var + LN_EPS) * g_ref[...] + b_ref[...]
        o_ref[r * sub:(r + 1) * sub, :] = (y * _sigmoid(y)).astype(o_ref.dtype)


def _conv(zc, conv_w, conv_b, ln_g, ln_b, we_gate, we_up, we_down, layer, nb, seq, ts=512):
    n = zc.shape[0]
    tps = seq // ts
    hpt = ts // CONV_HALO
    ne, d = we_gate.shape[1], we_gate.shape[2]
    eps = ne // (nb * tps)
    assert eps * nb * tps == ne
    vec = pl.BlockSpec((1, CONV_WIDTH), lambda b, j: (0, 0))
    w32 = lambda a, c: pl.BlockSpec((None, eps, a, c), lambda b, j: (layer, b * tps + j, 0, 0))
    w16 = lambda a, c: pl.BlockSpec((eps, a, c), lambda b, j: (b * tps + j, 0, 0))
    return pl.pallas_call(
        functools.partial(_conv_kernel, ts=ts),
        grid=(nb, tps),
        in_specs=[
            pl.BlockSpec((ts, COL_CONV), lambda b, j: (b * tps + j, 0)),
            pl.BlockSpec((CONV_HALO, COL_CONV), lambda b, j: (jnp.maximum((b * tps + j) * hpt - 1, 0), 0)),
            pl.BlockSpec((CONV_KERNEL, CONV_WIDTH), lambda b, j: (0, 0)),
            vec, vec, vec,
            w32(d, D_EXPERT), w32(d, D_EXPERT), w32(D_EXPERT, d),
        ],
        out_specs=[pl.BlockSpec((ts, CONV_WIDTH), lambda b, j: (b * tps + j, 0)),
                   w16(d, D_EXPERT), w16(d, D_EXPERT), w16(D_EXPERT, d)],
        out_shape=[jax.ShapeDtypeStruct((n, CONV_WIDTH), BF16), jax.ShapeDtypeStruct((ne, d, D_EXPERT), BF16),
                   jax.ShapeDtypeStruct((ne, d, D_EXPERT), BF16), jax.ShapeDtypeStruct((ne, D_EXPERT, d), BF16)],
        scratch_shapes=[pltpu.VMEM((SUBLANES, CONV_HALO + ts, CONV_WIDTH), F32)],
        compiler_params=_params("parallel", "parallel"),
        name="conv",
    )(zc, zc, conv_w, conv_b, ln_g, ln_b, we_gate, we_up, we_down)


def _prep_kernel(z_ref, halo_ref, mu_ref, w0_ref, a0_ref, wah_ref, wal_ref, g2_ref, kk_ref, ka_ref, rk_ref,
                 ones_ref, tri_ref, csel_ref,
                 at_ref, bt_ref, kt_ref, rt_ref, v_ref, bts_ref, kts_ref, bonus_ref, g_ref, pc_ref):
    j = pl.program_id(1)
    z = z_ref[...]
    ts = z.shape[0]
    last = jnp.where(j > 0, halo_ref[SHIFT_HALO - 1:SHIFT_HALO, :], 0.0)
    row = lax.broadcasted_iota(jnp.int32, z.shape, 0)
    prev = jnp.where(row == 0, last, pltpu.roll(z, 1, axis=0))
    zs = z + (prev - z) * mu_ref[...]
    rw = RWKV_WIDTH
    r = zs[:, :rw]
    k = zs[:, rw:2 * rw]
    v = zs[:, 2 * rw:3 * rw]
    wad = zs[:, 3 * rw:3 * rw + LORA_W + LORA_A]
    gd = zs[:, 3 * rw + LORA_W + LORA_A:]
    lane = lax.broadcasted_iota(jnp.int32, wad.shape, 1)
    xh, xl = _split(jnp.where(lane < LORA_W, jnp.tanh(wad), wad))
    wah = wah_ref[...]
    lora = _dot(xh, wah) + _dot(xl, wah) + _dot(xh, wal_ref[...])
    w = -_softplus(-(w0_ref[...] + lora[:, :rw])) - 0.5
    lw = -jnp.exp(w)
    a = _sigmoid(a0_ref[...] + lora[:, rw:])
    g_ref[...] = _dot(_sigmoid(gd).astype(BF16), g2_ref[...]).astype(g_ref.dtype)
    kk = k * kk_ref[...]
    k2 = k * (1.0 + (a - 1.0) * ka_ref[...])
    ones = ones_ref[...]
    kk = kk * lax.rsqrt(jnp.maximum(_dot_exact_rhs(kk * kk, ones), 1e-24))
    bonus_ref[...] = (_dot_exact_rhs(r * k2 * rk_ref[...], ones) * v).astype(bonus_ref.dtype)
    tri = tri_ref[...]
    lc, tot = [], []
    for h in range(ts // CUM_BLOCK):
        cs = _dot_exact_lhs(tri, lw[h * CUM_BLOCK:(h + 1) * CUM_BLOCK])
        lc.append(cs[:CUM_BLOCK])
        tot.append(cs[CUM_BLOCK:])
    lc = jnp.concatenate(lc, axis=0)
    tot = jnp.concatenate(tot, axis=0)
    e_neg = jnp.exp(-lc)
    e_end = jnp.exp(tot - lc)
    kka = kk * a
    dt = at_ref.dtype
    at_ref[...] = (-kk * jnp.exp(lc - lw)).astype(dt)
    bt_ref[...] = (kka * e_neg).astype(dt)
    kt_ref[...] = (k2 * e_neg).astype(dt)
    rt_ref[...] = (r * jnp.exp(lc)).astype(dt)
    v_ref[...] = v.astype(dt)
    bts_ref[...] = (kka * e_end).astype(dt)
    kts_ref[...] = (k2 * e_end).astype(dt)
    pc_ref[...] = jnp.exp(_dot_exact_lhs(csel_ref[...], lw))


def _prep(zr, mu, w0, a0, wa, g2, k_k, k_a, r_k, nb, seq, ts=512):
    n = zr.shape[0]
    tps = seq // ts
    hpt = ts // SHIFT_HALO
    rw = RWKV_WIDTH
    cpt = ts // CHUNK
    ones = jnp.kron(jnp.eye(N_HEADS, dtype=F32), jnp.ones((HEAD, HEAD), F32)).astype(BF16)
    blk = jnp.kron(jnp.eye(CUM_BLOCK // CHUNK, dtype=F32), jnp.ones((CHUNK, CHUNK), F32))
    tri = jnp.concatenate([jnp.tril(blk), blk], axis=0).astype(BF16)
    csel = jnp.kron(jnp.eye(cpt, dtype=F32), jnp.ones((1, CHUNK), F32)).astype(BF16)
    wa_hi, wa_lo = _split(wa)
    vec = pl.BlockSpec((1, rw), lambda b, j: (0, 0))
    full = lambda shape: pl.BlockSpec(shape, lambda b, j: (0, 0))
    rows = pl.BlockSpec((ts, rw), lambda b, j: (b * tps + j, 0))
    return pl.pallas_call(
        _prep_kernel,
        grid=(nb, tps),
        in_specs=[
            pl.BlockSpec((ts, COL_RWKV), lambda b, j: (b * tps + j, 0)),
            pl.BlockSpec((SHIFT_HALO, COL_RWKV), lambda b, j: (jnp.maximum((b * tps + j) * hpt - 1, 0), 0)),
            full((1, COL_RWKV)), vec, vec,
            full((LORA_W + LORA_A, 2 * rw)), full((LORA_W + LORA_A, 2 * rw)), full((LORA_G, rw)),
            vec, vec, vec, full((rw, rw)), full((2 * CUM_BLOCK, CUM_BLOCK)), full((cpt, ts)),
        ],
        out_specs=[rows] * 9 + [pl.BlockSpec((cpt, rw), lambda b, j: (b * tps + j, 0))],
        out_shape=[jax.ShapeDtypeStruct((n, rw), BF16)] * 9 + [jax.ShapeDtypeStruct((n // CHUNK, rw), F32)],
        compiler_params=_params("parallel", "parallel"),
        name="prep",
    )(zr, zr, mu, w0, a0, wa_hi, wa_lo, g2.astype(BF16), k_k, k_a, r_k, ones, tri, csel)


def _mm(a, b):
    return jnp.dot(a.astype(BF16), b.astype(BF16), preferred_element_type=F32)


def _mm_nt(a, b):
    return lax.dot_general(a.astype(BF16), b.astype(BF16), (((1,), (1,)), ((), ())),
                           preferred_element_type=F32)


def _mm_tn(a, b):
    return lax.dot_general(a.astype(BF16), b.astype(BF16), (((0,), (0,)), ((), ())),
                           preferred_element_type=F32)


def _scan_kernel(at_ref, bt_ref, kt_ref, rt_ref, v_ref, bts_ref, kts_ref, bonus_ref, g_ref, pc_ref,
                 gng_ref, gnb_ref, ones_ref, o_ref, s_ref, y_ref, *, ts, gb):
    @pl.when(pl.program_id(1) == 0)
    def _():
        s_ref[...] = jnp.zeros_like(s_ref)

    row_q = lax.broadcasted_iota(jnp.int32, (QUAD, QUAD), 0)
    col_q = lax.broadcasted_iota(jnp.int32, (QUAD, QUAD), 1)
    bdmask = (row_q // HEAD) == (col_q // HEAD)
    row_c = lax.broadcasted_iota(jnp.int32, (CHUNK, QUAD), 0)
    col_c = lax.broadcasted_iota(jnp.int32, (CHUNK, QUAD), 1) % CHUNK
    strict = col_c < row_c
    incl = col_c <= row_c
    eye = jnp.where(col_c == row_c, 1.0, 0.0).astype(F32)

    def bd(x):
        xb = x.astype(BF16)
        return jnp.where(bdmask, jnp.concatenate([xb] * (QUAD // CHUNK), axis=0), jnp.zeros((), BF16))

    chains = [(b, qd) for b in range(gb) for qd in range(N_QUADS)]

    def chunk(c, carry):
        sl = pl.ds(pl.multiple_of(c * CHUNK, CHUNK), CHUNK)
        lanes = [slice(qd * QUAD, (qd + 1) * QUAD) for _, qd in chains]
        ld = lambda ref: [ref[b, sl, ln] for (b, _), ln in zip(chains, lanes)]
        at, bt, kt, rt, v, bts, kts = (ld(r) for r in (at_ref, bt_ref, kt_ref, rt_ref, v_ref, bts_ref, kts_ref))
        each = range(len(chains))
        ar = [jnp.concatenate([at[i], rt[i]], axis=0) for i in each]
        pb = [_mm_nt(ar[i], bd(bt[i])) for i in each]
        pk = [_mm_nt(ar[i], bd(kt[i])) for i in each]
        l_ab = [jnp.where(strict, p[:CHUNK], 0.0) for p in pb]
        a_rb = [jnp.where(incl, p[CHUNK:], 0.0) for p in pb]
        l_ak = [jnp.where(strict, p[:CHUNK], 0.0) for p in pk]
        a_rk = [jnp.where(incl, p[CHUNK:], 0.0) for p in pk]
        tm = [eye + l for l in l_ab]
        lp = [_mm(l, bd(l)) for l in l_ab]
        for _ in range(4):
            both = [_mm(jnp.concatenate([tm[i], lp[i]], axis=0), bd(lp[i])) for i in each]
            tm = [tm[i] + both[i][:CHUNK] for i in each]
            lp = [both[i][CHUNK:] for i in each]
        tm = [tm[i] + _mm(tm[i], bd(lp[i])) for i in each]
        bdv = [bd(x) for x in v]
        x0 = [_mm(l_ak[i], bdv[i]) for i in each]
        tu = [_mm(tm[i], jnp.concatenate([bd(x0[i]), bd(at[i])], axis=1)) for i in each]
        u0 = [t[:, :QUAD] for t in tu]
        w = [t[:, QUAD:] for t in tu]
        qy = [_mm(a_rb[i], jnp.concatenate([bd(w[i]), bd(u0[i])], axis=1)) for i in each]
        y0 = [qy[i][:, QUAD:] + _mm(a_rk[i], bdv[i]) for i in each]
        q = [rt[i].astype(F32) + qy[i][:, :QUAD] for i in each]
        s = [s_ref[i] for i in each]
        qw_s = [_mm_nt(jnp.concatenate([q[i], w[i]], axis=0), s[i]) for i in each]
        for i, ((b, _), ln) in enumerate(zip(chains, lanes)):
            y_ref[b, sl, ln] = y0[i] + qw_s[i][:CHUNK]
        u = [(u0[i] + qw_s[i][CHUNK:]).astype(BF16) for i in each]
        n_p = [jnp.where(bdmask, _mm_tn(jnp.concatenate([u[i], v[i]], axis=0),
                                        jnp.concatenate([bts[i], kts[i]], axis=0)), 0.0) for i in each]
        for i, ((b, _), ln) in enumerate(zip(chains, lanes)):
            s_ref[i] = s[i] * pc_ref[b, pl.ds(c, 1), ln] + n_p[i]
        return carry

    lax.fori_loop(0, ts // CHUNK, chunk, 0)

    ones = ones_ref[...]
    for b in range(gb):
        for qd in range(N_QUADS):
            ln = slice(qd * QUAD, (qd + 1) * QUAD)
            y = y_ref[b, :, ln]
            dlt = y - _dot_exact_rhs(y, ones) * (1.0 / HEAD)
            var = _dot_exact_rhs(dlt * dlt, ones) * (1.0 / HEAD)
            yn = dlt * lax.rsqrt(var + GN_EPS) * gng_ref[:, ln] + gnb_ref[:, ln]
            o_ref[b, :, ln] = ((yn + bonus_ref[b, :, ln].astype(F32)) * g_ref[b, :, ln].astype(F32)).astype(o_ref.dtype)


def _scan(prep_out, gn_g, gn_b, nb, seq, ts=128, gb=8):
    n = prep_out[0].shape[0]
    rw = RWKV_WIDTH
    cpt = ts // CHUNK
    ones = jnp.kron(jnp.eye(QUAD // HEAD, dtype=F32), jnp.ones((HEAD, HEAD), F32)).astype(BF16)
    rows = pl.BlockSpec((gb, ts, rw), lambda i, t: (i, t, 0))
    vec = pl.BlockSpec((1, rw), lambda i, t: (0, 0))
    args = [a.reshape(nb, seq, rw) for a in prep_out[:9]] + [prep_out[9].reshape(nb, seq // ts, cpt, rw)]
    out = pl.pallas_call(
        functools.partial(_scan_kernel, ts=ts, gb=gb),
        grid=(nb // gb, seq // ts),
        in_specs=[rows] * 9 + [pl.BlockSpec((gb, None, cpt, rw), lambda i, t: (i, t, 0, 0)), vec, vec,
                               pl.BlockSpec((QUAD, QUAD), lambda i, t: (0, 0))],
        out_specs=rows,
        out_shape=jax.ShapeDtypeStruct((nb, seq, rw), BF16),
        scratch_shapes=[pltpu.VMEM((gb * N_QUADS, QUAD, QUAD), F32), pltpu.VMEM((gb, ts, rw), F32)],
        compiler_params=_params("parallel", "arbitrary"),
        name="scan",
    )(*args, gn_g, gn_b, ones)
    return out.reshape(n, rw)


def _merge_kernel(uc_ref, ur_ref, zg_ref, x_ref, pc_ref, pr_ref, wo_ref, gpost_ref, gt_ref,
                  gpre_ref, sc_ref, sh_ref, wrh_ref, wrl_ref, xo_ref, h_ref, lg_ref):
    yc = _dot(uc_ref[...], pc_ref[...])
    yr = _dot(ur_ref[...], pr_ref[...])
    zg = zg_ref[...].astype(F32)
    m = _sigmoid(zg[:, :D_MODEL]) * yc + _sigmoid(zg[:, D_MODEL:]) * yr
    y = _dot(m.astype(BF16), wo_ref[...])
    xn = x_ref[...] + gt_ref[...] * _rms(y, gpost_ref[...])
    xo_ref[...] = xn
    h = _rms(xn, gpre_ref[...]) * (1.0 + sc_ref[...]) + sh_ref[...]
    h_ref[...] = _pack_rows(h)
    h_hi, h_lo = _split(h)
    nt = lambda a, b: lax.dot_general(a, b, (((1,), (1,)), ((), ())), preferred_element_type=F32)
    w_hi = wrh_ref[...]
    lg_ref[...] = nt(w_hi, h_hi) + nt(w_hi, h_lo) + nt(wrl_ref[...], h_hi)


def _merge(uc, ur, zg, x2, p_conv_b, p_rwkv_b, w_o_b, g_post, g_pre, mod, layer, w_router_t, seq, tm=512):
    n, d = x2.shape
    row = lambda i: (i, 0)
    full = lambda shape: pl.BlockSpec(shape, lambda i: (0, 0))
    return pl.pallas_call(
        _merge_kernel,
        grid=(n // tm,),
        in_specs=[
            pl.BlockSpec((tm, CONV_WIDTH), row), pl.BlockSpec((tm, RWKV_WIDTH), row),
            pl.BlockSpec((tm, COL_GATE), row), pl.BlockSpec((tm, d), row),
            full((CONV_WIDTH, d)), full((RWKV_WIDTH, d)), full((d, d)),
            full((1, d)), _mod_spec(layer, 2, seq, tm),
            full((1, d)), _mod_spec(layer, 4, seq, tm), _mod_spec(layer, 3, seq, tm),
            full((N_EXPERTS, d)), full((N_EXPERTS, d)),
        ],
        out_specs=[pl.BlockSpec((tm, d), row), pl.BlockSpec((tm, ROW_WORDS), row),
                   pl.BlockSpec((N_EXPERTS, tm), lambda i: (0, i))],
        out_shape=[jax.ShapeDtypeStruct((n, d), F32), jax.ShapeDtypeStruct((n, ROW_WORDS), jnp.int32),
                   jax.ShapeDtypeStruct((N_EXPERTS, n), F32)],
        compiler_params=_params("parallel"),
        name="merge",
    )(uc, ur, zg, x2, p_conv_b, p_rwkv_b, w_o_b, g_post, mod, g_pre, mod, mod, *_split(w_router_t))


def _route_kernel(lg_ref, b_ref, gate_ref, sel_ref, cnt_ref):
    s = _sigmoid(lg_ref[...])
    biased = s + b_ref[...]
    t = s.shape[1]
    member = lax.broadcasted_iota(jnp.int32, (GROUP_SIZE, t), 0)
    grp = []
    for g in range(N_GROUPS):
        bg = biased[g * GROUP_SIZE:(g + 1) * GROUP_SIZE, :]
        m1 = jnp.max(bg, axis=0, keepdims=True)
        first = jnp.min(jnp.where(bg == m1, member, GROUP_SIZE), axis=0, keepdims=True)
        m2 = jnp.max(jnp.where(member == first, -jnp.inf, bg), axis=0, keepdims=True)
        grp.append(m1 + m2)
    masked = []
    for g in range(N_GROUPS):
        rank = jnp.zeros((1, t), jnp.int32)
        for o in range(N_GROUPS):
            if o == g:
                continue
            ahead = (grp[o] > grp[g]) if o > g else (grp[o] >= grp[g])
            rank = rank + jnp.where(ahead, 1, 0)
        keep = rank < TOPK_GROUPS
        masked.append(jnp.where(keep, biased[g * GROUP_SIZE:(g + 1) * GROUP_SIZE, :], -jnp.inf))
    masked = jnp.concatenate(masked, axis=0)
    eidx = lax.broadcasted_iota(jnp.int32, masked.shape, 0)
    mask = jnp.zeros(masked.shape, F32)
    for _ in range(TOP_K):
        top = jnp.max(masked, axis=0, keepdims=True)
        hit = eidx == jnp.min(jnp.where(masked == top, eidx, N_EXPERTS), axis=0, keepdims=True)
        mask = jnp.where(hit, 1.0, mask)
        masked = jnp.where(hit, -jnp.inf, masked)
    sel = s * mask
    gate_ref[...] = sel / jnp.sum(sel, axis=0, keepdims=True) * ROUTED_SCALE
    sel_ref[...] = mask.astype(sel_ref.dtype)

    @pl.when(pl.program_id(0) == 0)
    def _():
        cnt_ref[...] = jnp.zeros_like(cnt_ref)

    cnt_ref[...] += jnp.sum(mask, axis=1, keepdims=True)


def _route(logits_t, b_router, tt=512):
    e, n = logits_t.shape
    return pl.pallas_call(
        _route_kernel,
        grid=(n // tt,),
        in_specs=[pl.BlockSpec((e, tt), lambda i: (0, i)), pl.BlockSpec((e, 1), lambda i: (0, 0))],
        out_specs=[pl.BlockSpec((e, tt), lambda i: (0, i)), pl.BlockSpec((e, tt), lambda i: (0, i)),
                   pl.BlockSpec((e, GATE_LANES), lambda i: (0, 0))],
        out_shape=[jax.ShapeDtypeStruct((e, n), F32), jax.ShapeDtypeStruct((e, n), BF16),
                   jax.ShapeDtypeStruct((e, GATE_LANES), F32)],
        compiler_params=_params("arbitrary"),
        name="route",
    )(logits_t, b_router)


def _dest_kernel(sel_ref, gate_ref, start_ref, triu_ref, below_ref, dest_ref, g8_ref, run_ref):
    @pl.when(pl.program_id(0) == 0)
    def _():
        run_ref[...] = jnp.zeros_like(run_ref)

    sel = sel_ref[...]
    t = sel.shape[1]
    self32 = sel.astype(F32)
    incl = _dot(sel, triu_ref[...])
    pos = start_ref[...] + run_ref[...] + incl - self32
    run_ref[...] += incl[:, t - 1:t]
    slot = _dot(below_ref[...], sel)
    gate = gate_ref[...]
    dst, gts = [], []
    for k in range(TOP_K):
        mine = (self32 > 0.0) & (slot == float(k))
        dst.append(jnp.sum(jnp.where(mine, pos, 0.0), axis=0, keepdims=True))
        gts.append(jnp.sum(jnp.where(mine, gate, 0.0), axis=0, keepdims=True))
    dest_ref[...] = jnp.concatenate(dst, axis=0).astype(jnp.int32)
    g8_ref[...] = jnp.concatenate(gts + [jnp.zeros((GATE_LANES - TOP_K, t), F32)], axis=0).T


def _dest(sel, gate, seg_start, tt=512):
    e, n = sel.shape
    triu = jnp.triu(jnp.ones((tt, tt), F32)).astype(BF16)
    below = jnp.tril(jnp.ones((e, e), F32), -1).astype(BF16)
    return pl.pallas_call(
        _dest_kernel,
        grid=(n // tt,),
        in_specs=[pl.BlockSpec((e, tt), lambda i: (0, i)), pl.BlockSpec((e, tt), lambda i: (0, i)),
                  pl.BlockSpec((e, 1), lambda i: (0, 0)), pl.BlockSpec((tt, tt), lambda i: (0, 0)),
                  pl.BlockSpec((e, e), lambda i: (0, 0))],
        out_specs=[pl.BlockSpec((TOP_K, tt), lambda i: (0, i)), pl.BlockSpec((tt, GATE_LANES), lambda i: (i, 0))],
        out_shape=[jax.ShapeDtypeStruct((TOP_K, n), jnp.int32), jax.ShapeDtypeStruct((n, GATE_LANES), F32)],
        scratch_shapes=[pltpu.VMEM((e, 1), F32)],
        compiler_params=_params("arbitrary"),
        name="dest",
    )(sel, gate, seg_start, triu, below)


def _sc_mesh():
    return plsc.VectorSubcoreMesh(core_axis_name="c", subcore_axis_name="s",
                                  num_cores=SC_CORES, num_subcores=SC_SUBCORES)


def _worker_id():
    return lax.axis_index("s") * SC_CORES + lax.axis_index("c")


def _sc_scatter_rows(rows, dest, n_out):
    n, w = rows.shape
    per_worker = n // SC_WORKERS
    chunks_per_worker = per_worker // SC_CHUNK
    idx_rows = dest.reshape(TOP_K, n // SC_CHUNK, SC_CHUNK).transpose(1, 0, 2).reshape(-1, SC_CHUNK)

    def body(rows_hbm, idx_hbm, out_hbm, idx_v, rows_v):
        wid = _worker_id()

        @pl.loop(0, chunks_per_worker)
        def _(c):
            chunk = wid * chunks_per_worker + c
            pltpu.sync_copy(rows_hbm.at[pl.ds(chunk * SC_CHUNK, SC_CHUNK)], rows_v)
            pltpu.sync_copy(idx_hbm.at[pl.ds(chunk * TOP_K, TOP_K)], idx_v)
            for k in range(TOP_K):
                pltpu.sync_copy(rows_v, out_hbm.at[idx_v.at[k]])

    return pl.kernel(
        body, out_type=jax.ShapeDtypeStruct((n_out, w), rows.dtype), mesh=_sc_mesh(),
        scratch_types=[pltpu.VMEM((TOP_K, SC_CHUNK), jnp.int32), pltpu.VMEM((SC_CHUNK, w), rows.dtype)],
        name="sc_scatter",
    )(rows, idx_rows)


def _sc_gather_rows(table, idx_flat):
    m = idx_flat.shape[0]
    w = table.shape[1]
    ch = SC_CHUNK // 2
    chunks_per_worker = m // SC_WORKERS // ch
    idx_rows = idx_flat.reshape(-1, ch)
    ng = SC_IDX_ROWS

    def body(table_hbm, idx_hbm, out_hbm, idx_v, rows_a, rows_b, gat_a, gat_b, put_a, put_b):
        wid = _worker_id()
        bufs = ((rows_a, gat_a, put_a), (rows_b, gat_b, put_b))

        @pl.loop(0, chunks_per_worker // ng)
        def _(g):
            first = wid * chunks_per_worker + g * ng
            pltpu.sync_copy(idx_hbm.at[pl.ds(first, ng)], idx_v)
            gathers, stores = [None] * ng, [None] * ng
            gathers[0] = pltpu.async_copy(table_hbm.at[idx_v.at[0]], rows_a, gat_a)
            for j in range(ng):
                buf, _, put = bufs[j % 2]
                if j + 1 < ng:
                    if j >= 1:
                        stores[j - 1].wait()
                    nxt, gat, _ = bufs[(j + 1) % 2]
                    gathers[j + 1] = pltpu.async_copy(table_hbm.at[idx_v.at[j + 1]], nxt, gat)
                gathers[j].wait()
                stores[j] = pltpu.async_copy(buf, out_hbm.at[pl.ds((first + j) * ch, ch)], put)
            stores[ng - 2].wait()
            stores[ng - 1].wait()

    return pl.kernel(
        body, out_type=jax.ShapeDtypeStruct((m, w), table.dtype), mesh=_sc_mesh(),
        scratch_types=[pltpu.VMEM((ng, ch), jnp.int32), pltpu.VMEM((ch, w), table.dtype),
                       pltpu.VMEM((ch, w), table.dtype)] + [pltpu.SemaphoreType.DMA] * 4,
        name="sc_gather",
    )(table, idx_rows)


def _swiglu(h, wg, wu):
    gte = _dot(h, wg)
    return gte * _sigmoid(gte) * _dot(h, wu)


def _experts_kernel(be_ref, va_ref, xs_ref, *refs):
    nb = BLOCKS_PER_STEP
    wg, wu, wd, ys_ref = refs[:nb], refs[nb:2 * nb], refs[2 * nb:3 * nb], refs[3 * nb]
    first = pl.program_id(0) * nb

    @pl.when(va_ref[first] > 0)
    def _():
        row = lax.broadcasted_iota(jnp.int32, (EXPERT_SUB, D_MODEL), 0)
        subs = [(b, r0) for b in range(nb) for r0 in range(0, EXPERT_BLOCK, EXPERT_SUB)]
        x = []
        for b, r0 in subs:
            lo = b * EXPERT_BLOCK + r0
            xi = _unpack_rows(xs_ref[lo:lo + EXPERT_SUB, :])
            x.append(jnp.where(row < va_ref[first + b] - r0, xi, 0.0).astype(BF16))
        gte = [_dot(xi, wg[b][...]) for xi, (b, _) in zip(x, subs)]
        up = [_dot(xi, wu[b][...]) for xi, (b, _) in zip(x, subs)]
        act = [(g * _sigmoid(g) * u).astype(BF16) for g, u in zip(gte, up)]
        y = [_dot(a, wd[b][...]) for a, (b, _) in zip(act, subs)]
        for yi, (b, r0) in zip(y, subs):
            lo = b * EXPERT_BLOCK + r0
            ys_ref[lo:lo + EXPERT_SUB, :] = _pack_rows(yi)


def _experts(xs, blk_expert, blk_valid, wg, wu, wd):
    n_rows, w = xs.shape
    d = wg.shape[1]
    nb = BLOCKS_PER_STEP
    rows = pl.BlockSpec((nb * EXPERT_BLOCK, w), lambda s, be, va: (s, 0))
    up_spec = lambda i: pl.BlockSpec((None, d, D_EXPERT), lambda s, be, va: (be[s * nb + i], 0, 0))
    down_spec = lambda i: pl.BlockSpec((None, D_EXPERT, d), lambda s, be, va: (be[s * nb + i], 0, 0))
    return pl.pallas_call(
        _experts_kernel,
        grid_spec=pltpu.PrefetchScalarGridSpec(
            num_scalar_prefetch=2,
            grid=(n_rows // (nb * EXPERT_BLOCK),),
            in_specs=[rows] + [up_spec(i) for i in range(nb)] * 2 + [down_spec(i) for i in range(nb)],
            out_specs=rows,
        ),
        out_shape=jax.ShapeDtypeStruct((n_rows, w), xs.dtype),
        compiler_params=_params("arbitrary"),
        name="experts",
    )(blk_expert, blk_valid, xs, *([wg] * nb), *([wu] * nb), *([wd] * nb))


def _combine_kernel(yg_ref, g8_ref, hp_ref, sg_ref, su_ref, sd_ref, x_ref, gpost_ref, gt_ref, o_ref):
    h = _unpack_rows(hp_ref[...]).astype(BF16)
    acc = _dot(_swiglu(h, sg_ref[...], su_ref[...]).astype(BF16), sd_ref[...])
    g8 = g8_ref[...]
    for k in range(TOP_K):
        acc = acc + g8[:, k:k + 1] * _unpack_rows(yg_ref[k])
    o_ref[...] = x_ref[...] + gt_ref[...] * _rms(acc, gpost_ref[...])


def _combine(yg, g8, hp, sg, su, sd, x2, g_post, mod, layer, seq, tm=512):
    n, d = x2.shape
    row = lambda i: (i, 0)
    full = lambda shape: pl.BlockSpec(shape, lambda i: (0, 0))
    in_specs = [
        pl.BlockSpec((TOP_K, tm, ROW_WORDS), lambda i: (0, i, 0)),
        pl.BlockSpec((tm, GATE_LANES), row),
        pl.BlockSpec((tm, ROW_WORDS), row),
        full((d, D_EXPERT)), full((d, D_EXPERT)), full((D_EXPERT, d)),
        pl.BlockSpec((tm, d), row),
        full((1, d)),
        _mod_spec(layer, 5, seq, tm),
    ]
    return pl.pallas_call(
        _combine_kernel,
        grid=(n // tm,),
        in_specs=in_specs,
        out_specs=pl.BlockSpec((tm, d), row),
        out_shape=jax.ShapeDtypeStruct((n, d), F32),
        compiler_params=_params("parallel"),
        name="combine",
    )(yg, g8, hp, sg, su, sd, x2, g_post, mod)


def _moe(hp, logits_t, b_router, wg, wu, wd, sg, su, sd, x2, g_post, mod, layer, seq):
    n = x2.shape[0]
    gate, sel, counts = _route(logits_t, b_router)
    cnt = counts[:, 0].astype(jnp.int32)
    padded = (cnt + EXPERT_BLOCK - 1) // EXPERT_BLOCK * EXPERT_BLOCK
    seg_end = jnp.cumsum(padded)
    seg_start = seg_end - padded
    n_blocks = n * TOP_K // EXPERT_BLOCK + N_EXPERTS
    blk_row = jnp.arange(n_blocks, dtype=jnp.int32) * EXPERT_BLOCK
    blk_expert = jnp.sum((seg_end[None, :] <= blk_row[:, None]).astype(jnp.int32), axis=1)
    blk_expert = jnp.minimum(blk_expert, N_EXPERTS - 1)
    mine = blk_expert[:, None] == jnp.arange(N_EXPERTS, dtype=jnp.int32)[None, :]
    seg_last = jnp.sum(jnp.where(mine, (seg_start + cnt)[None, :], 0), axis=1)
    blk_valid = jnp.clip(seg_last - blk_row, 0, EXPERT_BLOCK).astype(jnp.int32)
    dest, g8 = _dest(sel, gate, seg_start.astype(F32).reshape(-1, 1))
    dest_flat = dest.reshape(-1)
    xs = _sc_scatter_rows(hp, dest, n_blocks * EXPERT_BLOCK)
    ys = _experts(xs, blk_expert, blk_valid, wg, wu, wd)
    yg = _sc_gather_rows(ys, dest_flat).reshape(TOP_K, n, ROW_WORDS)
    return _combine(yg, g8, hp, sg, su, sd, x2, g_post, mod, layer, seq)


def kernel(x, c, w_ada, b_ada, norm_mix_pre, norm_mix_post, norm_ffn_pre, norm_ffn_post, w_in, mu_shift, conv_w, conv_b, conv_ln_g, conv_ln_b, p_conv, w0, w2, a0, a2, g2, k_k, k_a, r_k, gn_g, gn_b, p_rwkv, w_o, w_router, b_router, we_gate, we_up, we_down, ws_gate, ws_up, ws_down):
    nb, seq, d = x.shape
    depth = w_ada.shape[0]
    n = nb * seq
    rw = RWKV_WIDTH
    mod = _ada(c, w_ada, b_ada)
    x2 = x.reshape(n, d)
    row = lambda a: a.reshape(1, -1)
    bf = lambda a: a.astype(BF16)
    for l in range(depth):
        zc, zr, zg = _win(x2, row(norm_mix_pre[l]), mod, l, bf(w_in[l]), seq)
        uc, wg_b, wu_b, wd_b = _conv(zc, conv_w[l], row(conv_b[l]), row(conv_ln_g[l]), row(conv_ln_b[l]),
                                     we_gate, we_up, we_down, l, nb, seq)
        wa = jnp.zeros((LORA_W + LORA_A, 2 * rw), F32)
        wa = wa.at[:LORA_W, :rw].set(w2[l]).at[LORA_W:, rw:].set(a2[l])
        prep_out = _prep(zr, row(mu_shift[l]), row(w0[l]), row(a0[l]), wa, g2[l], row(k_k[l]),
                         row(k_a[l]), row(r_k[l]), nb, seq)
        ur = _scan(prep_out, row(gn_g[l]), row(gn_b[l]), nb, seq)
        x2, hp, logits_t = _merge(uc, ur, zg, x2, bf(p_conv[l]), bf(p_rwkv[l]), bf(w_o[l]),
                                  row(norm_mix_post[l]), row(norm_ffn_pre[l]), mod, l, w_router[l].T, seq)
        x2 = _moe(hp, logits_t, b_router[l].reshape(-1, 1), wg_b, wu_b, wd_b, bf(ws_gate[l]), bf(ws_up[l]),
                  bf(ws_down[l]), x2, row(norm_ffn_post[l]), mod, l, seq)
    return x2.reshape(nb, seq, d)
```

```python
import functools

import jax
import jax.numpy as jnp
from jax import lax
from jax.experimental import pallas as pl
from jax.experimental.pallas import tpu as pltpu
from jax.experimental.pallas import tpu_sc as plsc

F32 = jnp.float32
BF16 = jnp.bfloat16

D_MODEL = 1024
CONV_WIDTH = 512
CONV_KERNEL = 31
RWKV_WIDTH = 512
HEAD = 64
N_HEADS = RWKV_WIDTH // HEAD
LORA_W = 64
LORA_A = 64
LORA_G = 128
N_EXPERTS = 64
TOP_K = 8
N_GROUPS = 8
TOPK_GROUPS = 4
GROUP_SIZE = N_EXPERTS // N_GROUPS
D_EXPERT = 256
ROUTED_SCALE = 2.5
RMS_EPS = 1e-6
LN_EPS = 1e-5
GN_EPS = 64e-5
COL_CONV = 2 * CONV_WIDTH
COL_RWKV = 3 * RWKV_WIDTH + LORA_W + LORA_A + LORA_G
COL_GATE = 2 * D_MODEL
D_IN = COL_CONV + COL_RWKV + COL_GATE

CHUNK = 64
QUAD = 4 * HEAD
N_QUADS = RWKV_WIDTH // QUAD
CUM_BLOCK = 256
CONV_HALO = 32
SUBLANES = 8
SHIFT_HALO = SUBLANES
GATE_LANES = 128
EXPERT_BLOCK = 512
EXPERT_SUB = 256
BLOCKS_PER_STEP = 4
ROW_WORDS = D_MODEL // 2
SC_CORES = 2
SC_SUBCORES = 16
SC_WORKERS = SC_CORES * SC_SUBCORES
SC_CHUNK = 128
SC_IDX_ROWS = 8
VMEM_LIMIT = 52 * 1024 * 1024


def _params(*sem):
    return pltpu.CompilerParams(dimension_semantics=sem, vmem_limit_bytes=VMEM_LIMIT)


def _sigmoid(x):
    return 1.0 / (1.0 + jnp.exp(-x))


def _softplus(x):
    return jnp.maximum(x, 0.0) + jnp.log(1.0 + jnp.exp(-jnp.abs(x)))


def _rms(x, g):
    return x * lax.rsqrt(jnp.mean(x * x, axis=-1, keepdims=True) + RMS_EPS) * g


def _split(x):
    hi = x.astype(BF16)
    return hi, (x - hi.astype(F32)).astype(BF16)


def _dot(a, b):
    return jnp.dot(a, b, preferred_element_type=F32)


def _dot_exact_rhs(x, w):
    hi, lo = _split(x)
    return _dot(hi, w) + _dot(lo, w)


def _pack_rows(y):
    half = y.shape[1] // 2
    packed = pltpu.pack_elementwise([y[:, half:], y[:, :half]], packed_dtype=BF16)
    return lax.bitcast_convert_type(packed, jnp.int32)


def _unpack_rows(u):
    hi = pltpu.unpack_elementwise(u, index=1, packed_dtype=BF16, unpacked_dtype=F32)
    lo = pltpu.unpack_elementwise(u, index=0, packed_dtype=BF16, unpacked_dtype=F32)
    return jnp.concatenate([hi, lo], axis=1)


def _dot_exact_lhs(w, x):
    hi, lo = _split(x)
    return _dot(w, hi) + _dot(w, lo)


def _ada_kernel(c_ref, w_ref, b_ref, o_ref):
    c = c_ref[...]
    a_hi, a_lo = _split(c * _sigmoid(c))
    w_hi, w_lo = _split(w_ref[...])
    o_ref[...] = _dot(a_hi, w_hi) + _dot(a_lo, w_hi) + _dot(a_hi, w_lo) + b_ref[...]


def _ada(c, w_ada, b_ada):
    nl, d, _ = w_ada.shape
    nb = c.shape[0]
    out = pl.pallas_call(
        _ada_kernel,
        grid=(nl, 6),
        in_specs=[
            pl.BlockSpec((nb, d), lambda l, k: (0, 0)),
            pl.BlockSpec((None, d, d), lambda l, k: (l, 0, k)),
            pl.BlockSpec((None, None, 1, d), lambda l, k: (l, k, 0, 0)),
        ],
        out_specs=pl.BlockSpec((None, None, nb, d), lambda l, k: (l, k, 0, 0)),
        out_shape=jax.ShapeDtypeStruct((nl, 6, nb, d), F32),
        compiler_params=_params("parallel", "parallel"),
        name="ada",
    )(c, w_ada, b_ada.reshape(nl, 6, 1, d))
    return out.reshape(nl, 6, nb, 1, d)


def _mod_spec(layer, piece, rows_per_batch, tm):
    return pl.BlockSpec((None, None, None, 1, D_MODEL),
                        lambda i, *_: (layer, piece, (i * tm) // rows_per_batch, 0, 0))


def _win_kernel(x_ref, g_ref, sc_ref, sh_ref, w_ref, zc_ref, zr_ref, zg_ref):
    h = _rms(x_ref[...], g_ref[...]) * (1.0 + sc_ref[...]) + sh_ref[...]
    hb = h.astype(BF16)
    zc_ref[...] = _dot(hb, w_ref[:, :COL_CONV]).astype(zc_ref.dtype)
    zr_ref[...] = _dot(hb, w_ref[:, COL_CONV:COL_CONV + COL_RWKV])
    zg_ref[...] = _dot(hb, w_ref[:, COL_CONV + COL_RWKV:]).astype(zg_ref.dtype)


def _win(x2, g, mod, layer, w_in_b, seq, tm=512):
    n, d = x2.shape
    row = lambda i: (i, 0)
    return pl.pallas_call(
        _win_kernel,
        grid=(n // tm,),
        in_specs=[
            pl.BlockSpec((tm, d), row),
            pl.BlockSpec((1, d), lambda i: (0, 0)),
            _mod_spec(layer, 1, seq, tm),
            _mod_spec(layer, 0, seq, tm),
            pl.BlockSpec((d, D_IN), lambda i: (0, 0)),
        ],
        out_specs=[pl.BlockSpec((tm, COL_CONV), row), pl.BlockSpec((tm, COL_RWKV), row),
                   pl.BlockSpec((tm, COL_GATE), row)],
        out_shape=[jax.ShapeDtypeStruct((n, COL_CONV), BF16), jax.ShapeDtypeStruct((n, COL_RWKV), F32),
                   jax.ShapeDtypeStruct((n, COL_GATE), BF16)],
        compiler_params=_params("parallel"),
        name="win",
    )(x2, g, mod, mod, w_in_b)


def _conv_kernel(z_ref, halo_ref, w_ref, cb_ref, g_ref, b_ref, eg_ref, eu_ref, ed_ref,
                 o_ref, og_ref, ou_ref, od_ref, ubuf, *, ts):
    og_ref[...] = eg_ref[...].astype(og_ref.dtype)
    ou_ref[...] = eu_ref[...].astype(ou_ref.dtype)
    od_ref[...] = ed_ref[...].astype(od_ref.dtype)
    j = pl.program_id(1)
    z = z_ref[...].astype(F32)
    zh = halo_ref[...].astype(F32)
    uh = zh[:, :CONV_WIDTH] * _sigmoid(zh[:, CONV_WIDTH:])
    rows = CONV_HALO + ts
    u = jnp.concatenate([jnp.where(j > 0, uh, 0.0),
                         z[:, :CONV_WIDTH] * _sigmoid(z[:, CONV_WIDTH:])], axis=0)
    ubuf[0] = u
    for s in range(1, SUBLANES):
        ubuf[s] = pltpu.roll(u, rows - s, axis=0)
    sub = 64
    first = CONV_HALO - (CONV_KERNEL - 1)
    for r in range(ts // sub):
        acc = jnp.zeros((sub, CONV_WIDTH), F32) + cb_ref[...]
        for k in range(CONV_KERNEL):
            s = (first + k) % SUBLANES
            base = r * sub + first + k - s
            acc = acc + w_ref[k:k + 1, :] * ubuf[s, base:base + sub, :]
        mu = jnp.mean(acc, axis=-1, keepdims=True)
        dlt = acc - mu
        var = jnp.mean(dlt * dlt, axis=-1, keepdims=True)
        y = dlt * lax.rsqrt(var + LN_EPS) * g_ref[...] + b_ref[...]
        o_ref[r * sub:(r + 1) * sub, :] = (y * _sigmoid(y)).astype(o_ref.dtype)


def _conv(zc, conv_w, conv_b, ln_g, ln_b, we_gate, we_up, we_down, layer, nb, seq, ts=512):
    n = zc.shape[0]
    tps = seq // ts
    hpt = ts // CONV_HALO
    ne, d = we_gate.shape[1], we_gate.shape[2]
    eps = ne // (nb * tps)
    assert eps * nb * tps == ne
    vec = pl.BlockSpec((1, CONV_WIDTH), lambda b, j: (0, 0))
    w32 = lambda a, c: pl.BlockSpec((None, eps, a, c), lambda b, j: (layer, b * tps + j, 0, 0))
    w16 = lambda a, c: pl.BlockSpec((eps, a, c), lambda b, j: (b * tps + j, 0, 0))
    return pl.pallas_call(
        functools.partial(_conv_kernel, ts=ts),
        grid=(nb, tps),
        in_specs=[
            pl.BlockSpec((ts, COL_CONV), lambda b, j: (b * tps + j, 0)),
            pl.BlockSpec((CONV_HALO, COL_CONV), lambda b, j: (jnp.maximum((b * tps + j) * hpt - 1, 0), 0)),
            pl.BlockSpec((CONV_KERNEL, CONV_WIDTH), lambda b, j: (0, 0)),
            vec, vec, vec,
            w32(d, D_EXPERT), w32(d, D_EXPERT), w32(D_EXPERT, d),
        ],
        out_specs=[pl.BlockSpec((ts, CONV_WIDTH), lambda b, j: (b * tps + j, 0)),
                   w16(d, D_EXPERT), w16(d, D_EXPERT), w16(D_EXPERT, d)],
        out_shape=[jax.ShapeDtypeStruct((n, CONV_WIDTH), BF16), jax.ShapeDtypeStruct((ne, d, D_EXPERT), BF16),
                   jax.ShapeDtypeStruct((ne, d, D_EXPERT), BF16), jax.ShapeDtypeStruct((ne, D_EXPERT, d), BF16)],
        scratch_shapes=[pltpu.VMEM((SUBLANES, CONV_HALO + ts, CONV_WIDTH), F32)],
        compiler_params=_params("parallel", "parallel"),
        name="conv",
    )(zc, zc, conv_w, conv_b, ln_g, ln_b, we_gate, we_up, we_down)


def _prep_kernel(z_ref, halo_ref, mu_ref, w0_ref, a0_ref, wah_ref, wal_ref, g2_ref, kk_ref, ka_ref, rk_ref,
                 ones_ref, tri_ref, csel_ref,
                 at_ref, bt_ref, kt_ref, rt_ref, v_ref, bts_ref, kts_ref, bonus_ref, g_ref, pc_ref):
    j = pl.program_id(1)
    z = z_ref[...]
    ts = z.shape[0]
    last = jnp.where(j > 0, halo_ref[SHIFT_HALO - 1:SHIFT_HALO, :], 0.0)
    row = lax.broadcasted_iota(jnp.int32, z.shape, 0)
    prev = jnp.where(row == 0, last, pltpu.roll(z, 1, axis=0))
    zs = z + (prev - z) * mu_ref[...]
    rw = RWKV_WIDTH
    r = zs[:, :rw]
    k = zs[:, rw:2 * rw]
    v = zs[:, 2 * rw:3 * rw]
    wad = zs[:, 3 * rw:3 * rw + LORA_W + LORA_A]
    gd = zs[:, 3 * rw + LORA_W + LORA_A:]
    lane = lax.broadcasted_iota(jnp.int32, wad.shape, 1)
    xh, xl = _split(jnp.where(lane < LORA_W, jnp.tanh(wad), wad))
    wah = wah_ref[...]
    lora = _dot(xh, wah) + _dot(xl, wah) + _dot(xh, wal_ref[...])
    w = -_softplus(-(w0_ref[...] + lora[:, :rw])) - 0.5
    lw = -jnp.exp(w)
    a = _sigmoid(a0_ref[...] + lora[:, rw:])
    g_ref[...] = _dot(_sigmoid(gd).astype(BF16), g2_ref[...]).astype(g_ref.dtype)
    kk = k * kk_ref[...]
    k2 = k * (1.0 + (a - 1.0) * ka_ref[...])
    ones = ones_ref[...]
    kk = kk * lax.rsqrt(jnp.maximum(_dot_exact_rhs(kk * kk, ones), 1e-24))
    bonus_ref[...] = (_dot_exact_rhs(r * k2 * rk_ref[...], ones) * v).astype(bonus_ref.dtype)
    tri = tri_ref[...]
    lc, tot = [], []
    for h in range(ts // CUM_BLOCK):
        cs = _dot_exact_lhs(tri, lw[h * CUM_BLOCK:(h + 1) * CUM_BLOCK])
        lc.append(cs[:CUM_BLOCK])
        tot.append(cs[CUM_BLOCK:])
    lc = jnp.concatenate(lc, axis=0)
    tot = jnp.concatenate(tot, axis=0)
    e_neg = jnp.exp(-lc)
    e_end = jnp.exp(tot - lc)
    kka = kk * a
    dt = at_ref.dtype
    at_ref[...] = (-kk * jnp.exp(lc - lw)).astype(dt)
    bt_ref[...] = (kka * e_neg).astype(dt)
    kt_ref[...] = (k2 * e_neg).astype(dt)
    rt_ref[...] = (r * jnp.exp(lc)).astype(dt)
    v_ref[...] = v.astype(dt)
    bts_ref[...] = (kka * e_end).astype(dt)
    kts_ref[...] = (k2 * e_end).astype(dt)
    pc_ref[...] = jnp.exp(_dot_exact_lhs(csel_ref[...], lw))


def _prep(zr, mu, w0, a0, wa, g2, k_k, k_a, r_k, nb, seq, ts=512):
    n = zr.shape[0]
    tps = seq // ts
    hpt = ts // SHIFT_HALO
    rw = RWKV_WIDTH
    cpt = ts // CHUNK
    ones = jnp.kron(jnp.eye(N_HEADS, dtype=F32), jnp.ones((HEAD, HEAD), F32)).astype(BF16)
    blk = jnp.kron(jnp.eye(CUM_BLOCK // CHUNK, dtype=F32), jnp.ones((CHUNK, CHUNK), F32))
    tri = jnp.concatenate([jnp.tril(blk), blk], axis=0).astype(BF16)
    csel = jnp.kron(jnp.eye(cpt, dtype=F32), jnp.ones((1, CHUNK), F32)).astype(BF16)
    wa_hi, wa_lo = _split(wa)
    vec = pl.BlockSpec((1, rw), lambda b, j: (0, 0))
    full = lambda shape: pl.BlockSpec(shape, lambda b, j: (0, 0))
    rows = pl.BlockSpec((ts, rw), lambda b, j: (b * tps + j, 0))
    return pl.pallas_call(
        _prep_kernel,
        grid=(nb, tps),
        in_specs=[
            pl.BlockSpec((ts, COL_RWKV), lambda b, j: (b * tps + j, 0)),
            pl.BlockSpec((SHIFT_HALO, COL_RWKV), lambda b, j: (jnp.maximum((b * tps + j) * hpt - 1, 0), 0)),
            full((1, COL_RWKV)), vec, vec,
            full((LORA_W + LORA_A, 2 * rw)), full((LORA_W + LORA_A, 2 * rw)), full((LORA_G, rw)),
            vec, vec, vec, full((rw, rw)), full((2 * CUM_BLOCK, CUM_BLOCK)), full((cpt, ts)),
        ],
        out_specs=[rows] * 9 + [pl.BlockSpec((cpt, rw), lambda b, j: (b * tps + j, 0))],
        out_shape=[jax.ShapeDtypeStruct((n, rw), BF16)] * 9 + [jax.ShapeDtypeStruct((n // CHUNK, rw), F32)],
        compiler_params=_params("parallel", "parallel"),
        name="prep",
    )(zr, zr, mu, w0, a0, wa_hi, wa_lo, g2.astype(BF16), k_k, k_a, r_k, ones, tri, csel)


def _mm(a, b):
    return jnp.dot(a.astype(BF16), b.astype(BF16), preferred_element_type=F32)


def _mm_nt(a, b):
    return lax.dot_general(a.astype(BF16), b.astype(BF16), (((1,), (1,)), ((), ())),
                           preferred_element_type=F32)


def _mm_tn(a, b):
    return lax.dot_general(a.astype(BF16), b.astype(BF16), (((0,), (0,)), ((), ())),
                           preferred_element_type=F32)


def _scan_kernel(at_ref, bt_ref, kt_ref, rt_ref, v_ref, bts_ref, kts_ref, bonus_ref, g_ref, pc_ref,
                 gng_ref, gnb_ref, ones_ref, o_ref, s_ref, y_ref, *, ts, gb):
    @pl.when(pl.program_id(1) == 0)
    def _():
        s_ref[...] = jnp.zeros_like(s_ref)

    row_q = lax.broadcasted_iota(jnp.int32, (QUAD, QUAD), 0)
    col_q = lax.broadcasted_iota(jnp.int32, (QUAD, QUAD), 1)
    bdmask = (row_q // HEAD) == (col_q // HEAD)
    row_c = lax.broadcasted_iota(jnp.int32, (CHUNK, QUAD), 0)
    col_c = lax.broadcasted_iota(jnp.int32, (CHUNK, QUAD), 1) % CHUNK
    strict = col_c < row_c
    incl = col_c <= row_c
    eye = jnp.where(col_c == row_c, 1.0, 0.0).astype(F32)

    def bd(x):
        xb = x.astype(BF16)
        return jnp.where(bdmask, jnp.concatenate([xb] * (QUAD // CHUNK), axis=0), jnp.zeros((), BF16))

    chains = [(b, qd) for b in range(gb) for qd in range(N_QUADS)]

    def chunk(c, carry):
        sl = pl.ds(pl.multiple_of(c * CHUNK, CHUNK), CHUNK)
        lanes = [slice(qd * QUAD, (qd + 1) * QUAD) for _, qd in chains]
        ld = lambda ref: [ref[b, sl, ln] for (b, _), ln in zip(chains, lanes)]
        at, bt, kt, rt, v, bts, kts = (ld(r) for r in (at_ref, bt_ref, kt_ref, rt_ref, v_ref, bts_ref, kts_ref))
        each = range(len(chains))
        ar = [jnp.concatenate([at[i], rt[i]], axis=0) for i in each]
        pb = [_mm_nt(ar[i], bd(bt[i])) for i in each]
        pk = [_mm_nt(ar[i], bd(kt[i])) for i in each]
        l_ab = [jnp.where(strict, p[:CHUNK], 0.0) for p in pb]
        a_rb = [jnp.where(incl, p[CHUNK:], 0.0) for p in pb]
        l_ak = [jnp.where(strict, p[:CHUNK], 0.0) for p in pk]
        a_rk = [jnp.where(incl, p[CHUNK:], 0.0) for p in pk]
        tm = [eye + l for l in l_ab]
        lp = [_mm(l, bd(l)) for l in l_ab]
        for _ in range(4):
            both = [_mm(jnp.concatenate([tm[i], lp[i]], axis=0), bd(lp[i])) for i in each]
            tm = [tm[i] + both[i][:CHUNK] for i in each]
            lp = [both[i][CHUNK:] for i in each]
        tm = [tm[i] + _mm(tm[i], bd(lp[i])) for i in each]
        bdv = [bd(x) for x in v]
        x0 = [_mm(l_ak[i], bdv[i]) for i in each]
        tu = [_mm(tm[i], jnp.concatenate([bd(x0[i]), bd(at[i])], axis=1)) for i in each]
        u0 = [t[:, :QUAD] for t in tu]
        w = [t[:, QUAD:] for t in tu]
        qy = [_mm(a_rb[i], jnp.concatenate([bd(w[i]), bd(u0[i])], axis=1)) for i in each]
        y0 = [qy[i][:, QUAD:] + _mm(a_rk[i], bdv[i]) for i in each]
        q = [rt[i].astype(F32) + qy[i][:, :QUAD] for i in each]
        s = [s_ref[i] for i in each]
        qw_s = [_mm_nt(jnp.concatenate([q[i], w[i]], axis=0), s[i]) for i in each]
        for i, ((b, _), ln) in enumerate(zip(chains, lanes)):
            y_ref[b, sl, ln] = y0[i] + qw_s[i][:CHUNK]
        u = [(u0[i] + qw_s[i][CHUNK:]).astype(BF16) for i in each]
        n_p = [jnp.where(bdmask, _mm_tn(jnp.concatenate([u[i], v[i]], axis=0),
                                        jnp.concatenate([bts[i], kts[i]], axis=0)), 0.0) for i in each]
        for i, ((b, _), ln) in enumerate(zip(chains, lanes)):
            s_ref[i] = s[i] * pc_ref[b, pl.ds(c, 1), ln] + n_p[i]
        return carry

    lax.fori_loop(0, ts // CHUNK, chunk, 0)

    ones = ones_ref[...]
    for b in range(gb):
        for qd in range(N_QUADS):
            ln = slice(qd * QUAD, (qd + 1) * QUAD)
            y = y_ref[b, :, ln]
            dlt = y - _dot_exact_rhs(y, ones) * (1.0 / HEAD)
            var = _dot_exact_rhs(dlt * dlt, ones) * (1.0 / HEAD)
            yn = dlt * lax.rsqrt(var + GN_EPS) * gng_ref[:, ln] + gnb_ref[:, ln]
            o_ref[b, :, ln] = ((yn + bonus_ref[b, :, ln].astype(F32)) * g_ref[b, :, ln].astype(F32)).astype(o_ref.dtype)


def _scan(prep_out, gn_g, gn_b, nb, seq, ts=128, gb=8):
    n = prep_out[0].shape[0]
    rw = RWKV_WIDTH
    cpt = ts // CHUNK
    ones = jnp.kron(jnp.eye(QUAD // HEAD, dtype=F32), jnp.ones((HEAD, HEAD), F32)).astype(BF16)
    rows = pl.BlockSpec((gb, ts, rw), lambda i, t: (i, t, 0))
    vec = pl.BlockSpec((1, rw), lambda i, t: (0, 0))
    args = [a.reshape(nb, seq, rw) for a in prep_out[:9]] + [prep_out[9].reshape(nb, seq // ts, cpt, rw)]
    out = pl.pallas_call(
        functools.partial(_scan_kernel, ts=ts, gb=gb),
        grid=(nb // gb, seq // ts),
        in_specs=[rows] * 9 + [pl.BlockSpec((gb, None, cpt, rw), lambda i, t: (i, t, 0, 0)), vec, vec,
                               pl.BlockSpec((QUAD, QUAD), lambda i, t: (0, 0))],
        out_specs=rows,
        out_shape=jax.ShapeDtypeStruct((nb, seq, rw), BF16),
        scratch_shapes=[pltpu.VMEM((gb * N_QUADS, QUAD, QUAD), F32), pltpu.VMEM((gb, ts, rw), F32)],
        compiler_params=_params("parallel", "arbitrary"),
        name="scan",
    )(*args, gn_g, gn_b, ones)
    return out.reshape(n, rw)


def _merge_kernel(uc_ref, ur_ref, zg_ref, x_ref, pc_ref, pr_ref, wo_ref, gpost_ref, gt_ref,
                  gpre_ref, sc_ref, sh_ref, wrh_ref, wrl_ref, xo_ref, h_ref, lg_ref):
    yc = _dot(uc_ref[...], pc_ref[...])
    yr = _dot(ur_ref[...], pr_ref[...])
    zg = zg_ref[...].astype(F32)
    m = _sigmoid(zg[:, :D_MODEL]) * yc + _sigmoid(zg[:, D_MODEL:]) * yr
    y = _dot(m.astype(BF16), wo_ref[...])
    xn = x_ref[...] + gt_ref[...] * _rms(y, gpost_ref[...])
    xo_ref[...] = xn
    h = _rms(xn, gpre_ref[...]) * (1.0 + sc_ref[...]) + sh_ref[...]
    h_ref[...] = _pack_rows(h)
    h_hi, h_lo = _split(h)
    nt = lambda a, b: lax.dot_general(a, b, (((1,), (1,)), ((), ())), preferred_element_type=F32)
    w_hi = wrh_ref[...]
    lg_ref[...] = nt(w_hi, h_hi) + nt(w_hi, h_lo) + nt(wrl_ref[...], h_hi)


def _merge(uc, ur, zg, x2, p_conv_b, p_rwkv_b, w_o_b, g_post, g_pre, mod, layer, w_router_t, seq, tm=512):
    n, d = x2.shape
    row = lambda i: (i, 0)
    full = lambda shape: pl.BlockSpec(shape, lambda i: (0, 0))
    return pl.pallas_call(
        _merge_kernel,
        grid=(n // tm,),
        in_specs=[
            pl.BlockSpec((tm, CONV_WIDTH), row), pl.BlockSpec((tm, RWKV_WIDTH), row),
            pl.BlockSpec((tm, COL_GATE), row), pl.BlockSpec((tm, d), row),
            full((CONV_WIDTH, d)), full((RWKV_WIDTH, d)), full((d, d)),
            full((1, d)), _mod_spec(layer, 2, seq, tm),
            full((1, d)), _mod_spec(layer, 4, seq, tm), _mod_spec(layer, 3, seq, tm),
            full((N_EXPERTS, d)), full((N_EXPERTS, d)),
        ],
        out_specs=[pl.BlockSpec((tm, d), row), pl.BlockSpec((tm, ROW_WORDS), row),
                   pl.BlockSpec((N_EXPERTS, tm), lambda i: (0, i))],
        out_shape=[jax.ShapeDtypeStruct((n, d), F32), jax.ShapeDtypeStruct((n, ROW_WORDS), jnp.int32),
                   jax.ShapeDtypeStruct((N_EXPERTS, n), F32)],
        compiler_params=_params("parallel"),
        name="merge",
    )(uc, ur, zg, x2, p_conv_b, p_rwkv_b, w_o_b, g_post, mod, g_pre, mod, mod, *_split(w_router_t))


def _route_kernel(lg_ref, b_ref, gate_ref, sel_ref, cnt_ref):
    s = _sigmoid(lg_ref[...])
    biased = s + b_ref[...]
    t = s.shape[1]
    member = lax.broadcasted_iota(jnp.int32, (GROUP_SIZE, t), 0)
    grp = []
    for g in range(N_GROUPS):
        bg = biased[g * GROUP_SIZE:(g + 1) * GROUP_SIZE, :]
        m1 = jnp.max(bg, axis=0, keepdims=True)
        first = jnp.min(jnp.where(bg == m1, member, GROUP_SIZE), axis=0, keepdims=True)
        m2 = jnp.max(jnp.where(member == first, -jnp.inf, bg), axis=0, keepdims=True)
        grp.append(m1 + m2)
    masked = []
    for g in range(N_GROUPS):
        rank = jnp.zeros((1, t), jnp.int32)
        for o in range(N_GROUPS):
            if o == g:
                continue
            ahead = (grp[o] > grp[g]) if o > g else (grp[o] >= grp[g])
            rank = rank + jnp.where(ahead, 1, 0)
        keep = rank < TOPK_GROUPS
        masked.append(jnp.where(keep, biased[g * GROUP_SIZE:(g + 1) * GROUP_SIZE, :], -jnp.inf))
    masked = jnp.concatenate(masked, axis=0)
    eidx = lax.broadcasted_iota(jnp.int32, masked.shape, 0)
    mask = jnp.zeros(masked.shape, F32)
    for _ in range(TOP_K):
        top = jnp.max(masked, axis=0, keepdims=True)
        hit = eidx == jnp.min(jnp.where(masked == top, eidx, N_EXPERTS), axis=0, keepdims=True)
        mask = jnp.where(hit, 1.0, mask)
        masked = jnp.where(hit, -jnp.inf, masked)
    sel = s * mask
    gate_ref[...] = sel / jnp.sum(sel, axis=0, keepdims=True) * ROUTED_SCALE
    sel_ref[...] = mask.astype(sel_ref.dtype)

    @pl.when(pl.program_id(0) == 0)
    def _():
        cnt_ref[...] = jnp.zeros_like(cnt_ref)

    cnt_ref[...] += jnp.sum(mask, axis=1, keepdims=True)


def _route(logits_t, b_router, tt=512):
    e, n = logits_t.shape
    return pl.pallas_call(
        _route_kernel,
        grid=(n // tt,),
        in_specs=[pl.BlockSpec((e, tt), lambda i: (0, i)), pl.BlockSpec((e, 1), lambda i: (0, 0))],
        out_specs=[pl.BlockSpec((e, tt), lambda i: (0, i)), pl.BlockSpec((e, tt), lambda i: (0, i)),
                   pl.BlockSpec((e, GATE_LANES), lambda i: (0, 0))],
        out_shape=[jax.ShapeDtypeStruct((e, n), F32), jax.ShapeDtypeStruct((e, n), BF16),
                   jax.ShapeDtypeStruct((e, GATE_LANES), F32)],
        compiler_params=_params("arbitrary"),
        name="route",
    )(logits_t, b_router)


def _dest_kernel(sel_ref, gate_ref, start_ref, triu_ref, below_ref, dest_ref, g8_ref, run_ref):
    @pl.when(pl.program_id(0) == 0)
    def _():
        run_ref[...] = jnp.zeros_like(run_ref)

    sel = sel_ref[...]
    t = sel.shape[1]
    self32 = sel.astype(F32)
    incl = _dot(sel, triu_ref[...])
    pos = start_ref[...] + run_ref[...] + incl - self32
    run_ref[...] += incl[:, t - 1:t]
    slot = _dot(below_ref[...], sel)
    gate = gate_ref[...]
    dst, gts = [], []
    for k in range(TOP_K):
        mine = (self32 > 0.0) & (slot == float(k))
        dst.append(jnp.sum(jnp.where(mine, pos, 0.0), axis=0, keepdims=True))
        gts.append(jnp.sum(jnp.where(mine, gate, 0.0), axis=0, keepdims=True))
    dest_ref[...] = jnp.concatenate(dst, axis=0).astype(jnp.int32)
    g8_ref[...] = jnp.concatenate(gts + [jnp.zeros((GATE_LANES - TOP_K, t), F32)], axis=0).T


def _dest(sel, gate, seg_start, tt=512):
    e, n = sel.shape
    triu = jnp.triu(jnp.ones((tt, tt), F32)).astype(BF16)
    below = jnp.tril(jnp.ones((e, e), F32), -1).astype(BF16)
    return pl.pallas_call(
        _dest_kernel,
        grid=(n // tt,),
        in_specs=[pl.BlockSpec((e, tt), lambda i: (0, i)), pl.BlockSpec((e, tt), lambda i: (0, i)),
                  pl.BlockSpec((e, 1), lambda i: (0, 0)), pl.BlockSpec((tt, tt), lambda i: (0, 0)),
                  pl.BlockSpec((e, e), lambda i: (0, 0))],
        out_specs=[pl.BlockSpec((TOP_K, tt), lambda i: (0, i)), pl.BlockSpec((tt, GATE_LANES), lambda i: (i, 0))],
        out_shape=[jax.ShapeDtypeStruct((TOP_K, n), jnp.int32), jax.ShapeDtypeStruct((n, GATE_LANES), F32)],
        scratch_shapes=[pltpu.VMEM((e, 1), F32)],
        compiler_params=_params("arbitrary"),
        name="dest",
    )(sel, gate, seg_start, triu, below)


def _sc_mesh():
    return plsc.VectorSubcoreMesh(core_axis_name="c", subcore_axis_name="s",
                                  num_cores=SC_CORES, num_subcores=SC_SUBCORES)


def _worker_id():
    return lax.axis_index("s") * SC_CORES + lax.axis_index("c")


def _sc_scatter_rows(rows, dest, n_out):
    n, w = rows.shape
    per_worker = n // SC_WORKERS
    chunks_per_worker = per_worker // SC_CHUNK
    idx_rows = dest.reshape(TOP_K, n // SC_CHUNK, SC_CHUNK).transpose(1, 0, 2).reshape(-1, SC_CHUNK)

    def body(rows_hbm, idx_hbm, out_hbm, idx_v, rows_v, *sems):
        wid = _worker_id()

        @pl.loop(0, chunks_per_worker)
        def _(c):
            chunk = wid * chunks_per_worker + c
            pltpu.sync_copy(rows_hbm.at[pl.ds(chunk * SC_CHUNK, SC_CHUNK)], rows_v)
            pltpu.sync_copy(idx_hbm.at[pl.ds(chunk * TOP_K, TOP_K)], idx_v)
            puts = [pltpu.async_copy(rows_v, out_hbm.at[idx_v.at[k]], sems[k]) for k in range(TOP_K)]
            for put in puts:
                put.wait()

    return pl.kernel(
        body, out_type=jax.ShapeDtypeStruct((n_out, w), rows.dtype), mesh=_sc_mesh(),
        scratch_types=[pltpu.VMEM((TOP_K, SC_CHUNK), jnp.int32), pltpu.VMEM((SC_CHUNK, w), rows.dtype)]
        + [pltpu.SemaphoreType.DMA] * TOP_K,
        name="sc_scatter",
    )(rows, idx_rows)


def _sc_gather_rows(table, idx_flat):
    m = idx_flat.shape[0]
    w = table.shape[1]
    ch = SC_CHUNK // 2
    chunks_per_worker = m // SC_WORKERS // ch
    idx_rows = idx_flat.reshape(-1, ch)
    ng = SC_IDX_ROWS

    def body(table_hbm, idx_hbm, out_hbm, idx_v, rows_a, rows_b, gat_a, gat_b, put_a, put_b):
        wid = _worker_id()
        bufs = ((rows_a, gat_a, put_a), (rows_b, gat_b, put_b))

        @pl.loop(0, chunks_per_worker // ng)
        def _(g):
            first = wid * chunks_per_worker + g * ng
            pltpu.sync_copy(idx_hbm.at[pl.ds(first, ng)], idx_v)
            gathers, stores = [None] * ng, [None] * ng
            gathers[0] = pltpu.async_copy(table_hbm.at[idx_v.at[0]], rows_a, gat_a)
            for j in range(ng):
                buf, _, put = bufs[j % 2]
                if j + 1 < ng:
                    if j >= 1:
                        stores[j - 1].wait()
                    nxt, gat, _ = bufs[(j + 1) % 2]
                    gathers[j + 1] = pltpu.async_copy(table_hbm.at[idx_v.at[j + 1]], nxt, gat)
                gathers[j].wait()
                stores[j] = pltpu.async_copy(buf, out_hbm.at[pl.ds((first + j) * ch, ch)], put)
            stores[ng - 2].wait()
            stores[ng - 1].wait()

    return pl.kernel(
        body, out_type=jax.ShapeDtypeStruct((m, w), table.dtype), mesh=_sc_mesh(),
        scratch_types=[pltpu.VMEM((ng, ch), jnp.int32), pltpu.VMEM((ch, w), table.dtype),
                       pltpu.VMEM((ch, w), table.dtype)] + [pltpu.SemaphoreType.DMA] * 4,
        name="sc_gather",
    )(table, idx_rows)


def _swiglu(h, wg, wu):
    gte = _dot(h, wg)
    return gte * _sigmoid(gte) * _dot(h, wu)


def _experts_kernel(be_ref, va_ref, xs_ref, *refs):
    nb = BLOCKS_PER_STEP
    wg, wu, wd, ys_ref = refs[:nb], refs[nb:2 * nb], refs[2 * nb:3 * nb], refs[3 * nb]
    first = pl.program_id(0) * nb

    @pl.when(va_ref[first] > 0)
    def _():
        row = lax.broadcasted_iota(jnp.int32, (EXPERT_SUB, D_MODEL), 0)
        subs = [(b, r0) for b in range(nb) for r0 in range(0, EXPERT_BLOCK, EXPERT_SUB)]
        x = []
        for b, r0 in subs:
            lo = b * EXPERT_BLOCK + r0
            xi = _unpack_rows(xs_ref[lo:lo + EXPERT_SUB, :])
            x.append(jnp.where(row < va_ref[first + b] - r0, xi, 0.0).astype(BF16))
        gte = [_dot(xi, wg[b][...]) for xi, (b, _) in zip(x, subs)]
        up = [_dot(xi, wu[b][...]) for xi, (b, _) in zip(x, subs)]
        act = [(g * _sigmoid(g) * u).astype(BF16) for g, u in zip(gte, up)]
        y = [_dot(a, wd[b][...]) for a, (b, _) in zip(act, subs)]
        for yi, (b, r0) in zip(y, subs):
            lo = b * EXPERT_BLOCK + r0
            ys_ref[lo:lo + EXPERT_SUB, :] = _pack_rows(yi)


def _experts(xs, blk_expert, blk_valid, wg, wu, wd):
    n_rows, w = xs.shape
    d = wg.shape[1]
    nb = BLOCKS_PER_STEP
    rows = pl.BlockSpec((nb * EXPERT_BLOCK, w), lambda s, be, va: (s, 0))
    up_spec = lambda i: pl.BlockSpec((None, d, D_EXPERT), lambda s, be, va: (be[s * nb + i], 0, 0))
    down_spec = lambda i: pl.BlockSpec((None, D_EXPERT, d), lambda s, be, va: (be[s * nb + i], 0, 0))
    return pl.pallas_call(
        _experts_kernel,
        grid_spec=pltpu.PrefetchScalarGridSpec(
            num_scalar_prefetch=2,
            grid=(n_rows // (nb * EXPERT_BLOCK),),
            in_specs=[rows] + [up_spec(i) for i in range(nb)] * 2 + [down_spec(i) for i in range(nb)],
            out_specs=rows,
        ),
        out_shape=jax.ShapeDtypeStruct((n_rows, w), xs.dtype),
        compiler_params=_params("arbitrary"),
        name="experts",
    )(blk_expert, blk_valid, xs, *([wg] * nb), *([wu] * nb), *([wd] * nb))


def _combine_kernel(yg_ref, g8_ref, hp_ref, sg_ref, su_ref, sd_ref, x_ref, gpost_ref, gt_ref, o_ref):
    h = _unpack_rows(hp_ref[...]).astype(BF16)
    acc = _dot(_swiglu(h, sg_ref[...], su_ref[...]).astype(BF16), sd_ref[...])
    g8 = g8_ref[...]
    for k in range(TOP_K):
        acc = acc + g8[:, k:k + 1] * _unpack_rows(yg_ref[k])
    o_ref[...] = x_ref[...] + gt_ref[...] * _rms(acc, gpost_ref[...])


def _combine(yg, g8, hp, sg, su, sd, x2, g_post, mod, layer, seq, tm=512):
    n, d = x2.shape
    row = lambda i: (i, 0)
    full = lambda shape: pl.BlockSpec(shape, lambda i: (0, 0))
    in_specs = [
        pl.BlockSpec((TOP_K, tm, ROW_WORDS), lambda i: (0, i, 0)),
        pl.BlockSpec((tm, GATE_LANES), row),
        pl.BlockSpec((tm, ROW_WORDS), row),
        full((d, D_EXPERT)), full((d, D_EXPERT)), full((D_EXPERT, d)),
        pl.BlockSpec((tm, d), row),
        full((1, d)),
        _mod_spec(layer, 5, seq, tm),
    ]
    return pl.pallas_call(
        _combine_kernel,
        grid=(n // tm,),
        in_specs=in_specs,
        out_specs=pl.BlockSpec((tm, d), row),
        out_shape=jax.ShapeDtypeStruct((n, d), F32),
        compiler_params=_params("parallel"),
        name="combine",
    )(yg, g8, hp, sg, su, sd, x2, g_post, mod)


def _moe(hp, logits_t, b_router, wg, wu, wd, sg, su, sd, x2, g_post, mod, layer, seq):
    n = x2.shape[0]
    gate, sel, counts = _route(logits_t, b_router)
    cnt = counts[:, 0].astype(jnp.int32)
    padded = (cnt + EXPERT_BLOCK - 1) // EXPERT_BLOCK * EXPERT_BLOCK
    seg_end = jnp.cumsum(padded)
    seg_start = seg_end - padded
    n_blocks = n * TOP_K // EXPERT_BLOCK + N_EXPERTS
    blk_row = jnp.arange(n_blocks, dtype=jnp.int32) * EXPERT_BLOCK
    blk_expert = jnp.sum((seg_end[None, :] <= blk_row[:, None]).astype(jnp.int32), axis=1)
    blk_expert = jnp.minimum(blk_expert, N_EXPERTS - 1)
    mine = blk_expert[:, None] == jnp.arange(N_EXPERTS, dtype=jnp.int32)[None, :]
    seg_last = jnp.sum(jnp.where(mine, (seg_start + cnt)[None, :], 0), axis=1)
    blk_valid = jnp.clip(seg_last - blk_row, 0, EXPERT_BLOCK).astype(jnp.int32)
    dest, g8 = _dest(sel, gate, seg_start.astype(F32).reshape(-1, 1))
    dest_flat = dest.reshape(-1)
    xs = _sc_scatter_rows(hp, dest, n_blocks * EXPERT_BLOCK)
    ys = _experts(xs, blk_expert, blk_valid, wg, wu, wd)
    yg = _sc_gather_rows(ys, dest_flat).reshape(TOP_K, n, ROW_WORDS)
    return _combine(yg, g8, hp, sg, su, sd, x2, g_post, mod, layer, seq)


def kernel(x, c, w_ada, b_ada, norm_mix_pre, norm_mix_post, norm_ffn_pre, norm_ffn_post, w_in, mu_shift, conv_w, conv_b, conv_ln_g, conv_ln_b, p_conv, w0, w2, a0, a2, g2, k_k, k_a, r_k, gn_g, gn_b, p_rwkv, w_o, w_router, b_router, we_gate, we_up, we_down, ws_gate, ws_up, ws_down):
    nb, seq, d = x.shape
    depth = w_ada.shape[0]
    n = nb * seq
    rw = RWKV_WIDTH
    mod = _ada(c, w_ada, b_ada)
    x2 = x.reshape(n, d)
    row = lambda a: a.reshape(1, -1)
    bf = lambda a: a.astype(BF16)
    for l in range(depth):
        zc, zr, zg = _win(x2, row(norm_mix_pre[l]), mod, l, bf(w_in[l]), seq)
        uc, wg_b, wu_b, wd_b = _conv(zc, conv_w[l], row(conv_b[l]), row(conv_ln_g[l]), row(conv_ln_b[l]),
                                     we_gate, we_up, we_down, l, nb, seq)
        wa = jnp.zeros((LORA_W + LORA_A, 2 * rw), F32)
        wa = wa.at[:LORA_W, :rw].set(w2[l]).at[LORA_W:, rw:].set(a2[l])
        prep_out = _prep(zr, row(mu_shift[l]), row(w0[l]), row(a0[l]), wa, g2[l], row(k_k[l]),
                         row(k_a[l]), row(r_k[l]), nb, seq)
        ur = _scan(prep_out, row(gn_g[l]), row(gn_b[l]), nb, seq)
        x2, hp, logits_t = _merge(uc, ur, zg, x2, bf(p_conv[l]), bf(p_rwkv[l]), bf(w_o[l]),
                                  row(norm_mix_post[l]), row(norm_ffn_pre[l]), mod, l, w_router[l].T, seq)
        x2 = _moe(hp, logits_t, b_router[l].reshape(-1, 1), wg_b, wu_b, wd_b, bf(ws_gate[l]), bf(ws_up[l]),
                  bf(ws_down[l]), x2, row(norm_ffn_post[l]), mod, l, seq)
    return x2.reshape(nb, seq, d)
```

```python
import functools

import jax
import jax.numpy as jnp
from jax import lax
from jax.experimental import pallas as pl
from jax.experimental.pallas import tpu as pltpu
from jax.experimental.pallas import tpu_sc as plsc

F32 = jnp.float32
BF16 = jnp.bfloat16

D_MODEL = 1024
CONV_WIDTH = 512
CONV_KERNEL = 31
RWKV_WIDTH = 512
HEAD = 64
N_HEADS = RWKV_WIDTH // HEAD
LORA_W = 64
LORA_A = 64
LORA_G = 128
N_EXPERTS = 64
TOP_K = 8
N_GROUPS = 8
TOPK_GROUPS = 4
GROUP_SIZE = N_EXPERTS // N_GROUPS
D_EXPERT = 256
ROUTED_SCALE = 2.5
RMS_EPS = 1e-6
LN_EPS = 1e-5
GN_EPS = 64e-5
COL_CONV = 2 * CONV_WIDTH
COL_RWKV = 3 * RWKV_WIDTH + LORA_W + LORA_A + LORA_G
COL_GATE = 2 * D_MODEL
D_IN = COL_CONV + COL_RWKV + COL_GATE

CHUNK = 64
QUAD = 4 * HEAD
N_QUADS = RWKV_WIDTH // QUAD
CUM_BLOCK = 256
CONV_HALO = 32
SUBLANES = 8
SHIFT_HALO = SUBLANES
GATE_LANES = 128
EXPERT_BLOCK = 512
EXPERT_SUB = 256
BLOCKS_PER_STEP = 4
XS_SLOTS = 3
ROW_WORDS = D_MODEL // 2
SC_CORES = 2
SC_SUBCORES = 16
SC_WORKERS = SC_CORES * SC_SUBCORES
SC_CHUNK = 128
SC_IDX_ROWS = 8
VMEM_LIMIT = 52 * 1024 * 1024


def _params(*sem):
    return pltpu.CompilerParams(dimension_semantics=sem, vmem_limit_bytes=VMEM_LIMIT)


def _sigmoid(x):
    return 1.0 / (1.0 + jnp.exp(-x))


def _softplus(x):
    return jnp.maximum(x, 0.0) + jnp.log(1.0 + jnp.exp(-jnp.abs(x)))


def _rms(x, g):
    return x * lax.rsqrt(jnp.mean(x * x, axis=-1, keepdims=True) + RMS_EPS) * g


def _split(x):
    hi = x.astype(BF16)
    return hi, (x - hi.astype(F32)).astype(BF16)


def _dot(a, b):
    return jnp.dot(a, b, preferred_element_type=F32)


def _dot_exact_rhs(x, w):
    hi, lo = _split(x)
    return _dot(hi, w) + _dot(lo, w)


def _pack_rows(y):
    half = y.shape[1] // 2
    packed = pltpu.pack_elementwise([y[:, half:], y[:, :half]], packed_dtype=BF16)
    return lax.bitcast_convert_type(packed, jnp.int32)


def _unpack_rows(u):
    hi = pltpu.unpack_elementwise(u, index=1, packed_dtype=BF16, unpacked_dtype=F32)
    lo = pltpu.unpack_elementwise(u, index=0, packed_dtype=BF16, unpacked_dtype=F32)
    return jnp.concatenate([hi, lo], axis=1)


def _dot_exact_lhs(w, x):
    hi, lo = _split(x)
    return _dot(w, hi) + _dot(w, lo)


def _ada_kernel(c_ref, w_ref, b_ref, o_ref):
    c = c_ref[...]
    a_hi, a_lo = _split(c * _sigmoid(c))
    w_hi, w_lo = _split(w_ref[...])
    o_ref[...] = _dot(a_hi, w_hi) + _dot(a_lo, w_hi) + _dot(a_hi, w_lo) + b_ref[...]


def _ada(c, w_ada, b_ada):
    nl, d, _ = w_ada.shape
    nb = c.shape[0]
    out = pl.pallas_call(
        _ada_kernel,
        grid=(nl, 6),
        in_specs=[
            pl.BlockSpec((nb, d), lambda l, k: (0, 0)),
            pl.BlockSpec((None, d, d), lambda l, k: (l, 0, k)),
            pl.BlockSpec((None, None, 1, d), lambda l, k: (l, k, 0, 0)),
        ],
        out_specs=pl.BlockSpec((None, None, nb, d), lambda l, k: (l, k, 0, 0)),
        out_shape=jax.ShapeDtypeStruct((nl, 6, nb, d), F32),
        compiler_params=_params("parallel", "parallel"),
        name="ada",
    )(c, w_ada, b_ada.reshape(nl, 6, 1, d))
    return out.reshape(nl, 6, nb, 1, d)


def _mod_spec(layer, piece, rows_per_batch, tm):
    return pl.BlockSpec((None, None, None, 1, D_MODEL),
                        lambda i, *_: (layer, piece, (i * tm) // rows_per_batch, 0, 0))


def _win_kernel(x_ref, g_ref, sc_ref, sh_ref, w_ref, zc_ref, zr_ref, zg_ref):
    h = _rms(x_ref[...], g_ref[...]) * (1.0 + sc_ref[...]) + sh_ref[...]
    hb = h.astype(BF16)
    zc_ref[...] = _dot(hb, w_ref[:, :COL_CONV]).astype(zc_ref.dtype)
    zr_ref[...] = _dot(hb, w_ref[:, COL_CONV:COL_CONV + COL_RWKV])
    zg_ref[...] = _dot(hb, w_ref[:, COL_CONV + COL_RWKV:]).astype(zg_ref.dtype)


def _win(x2, g, mod, layer, w_in_b, seq, tm=512):
    n, d = x2.shape
    row = lambda i: (i, 0)
    return pl.pallas_call(
        _win_kernel,
        grid=(n // tm,),
        in_specs=[
            pl.BlockSpec((tm, d), row),
            pl.BlockSpec((1, d), lambda i: (0, 0)),
            _mod_spec(layer, 1, seq, tm),
            _mod_spec(layer, 0, seq, tm),
            pl.BlockSpec((d, D_IN), lambda i: (0, 0)),
        ],
        out_specs=[pl.BlockSpec((tm, COL_CONV), row), pl.BlockSpec((tm, COL_RWKV), row),
                   pl.BlockSpec((tm, COL_GATE), row)],
        out_shape=[jax.ShapeDtypeStruct((n, COL_CONV), BF16), jax.ShapeDtypeStruct((n, COL_RWKV), F32),
                   jax.ShapeDtypeStruct((n, COL_GATE), BF16)],
        compiler_params=_params("parallel"),
        name="win",
    )(x2, g, mod, mod, w_in_b)


def _conv_kernel(z_ref, halo_ref, w_ref, cb_ref, g_ref, b_ref, eg_ref, eu_ref, ed_ref,
                 o_ref, og_ref, ou_ref, od_ref, ubuf, *, ts):
    og_ref[...] = eg_ref[...].astype(og_ref.dtype)
    ou_ref[...] = eu_ref[...].astype(ou_ref.dtype)
    od_ref[...] = ed_ref[...].astype(od_ref.dtype)
    j = pl.program_id(1)
    z = z_ref[...].astype(F32)
    zh = halo_ref[...].astype(F32)
    uh = zh[:, :CONV_WIDTH] * _sigmoid(zh[:, CONV_WIDTH:])
    rows = CONV_HALO + ts
    u = jnp.concatenate([jnp.where(j > 0, uh, 0.0),
                         z[:, :CONV_WIDTH] * _sigmoid(z[:, CONV_WIDTH:])], axis=0)
    ubuf[0] = u
    for s in range(1, SUBLANES):
        ubuf[s] = pltpu.roll(u, rows - s, axis=0)
    sub = 64
    first = CONV_HALO - (CONV_KERNEL - 1)
    for r in range(ts // sub):
        acc = jnp.zeros((sub, CONV_WIDTH), F32) + cb_ref[...]
        for k in range(CONV_KERNEL):
            s = (first + k) % SUBLANES
            base = r * sub + first + k - s
            acc = acc + w_ref[k:k + 1, :] * ubuf[s, base:base + sub, :]
        mu = jnp.mean(acc, axis=-1, keepdims=True)
        dlt = acc - mu
        var = jnp.mean(dlt * dlt, axis=-1, keepdims=True)
        y = dlt * lax.rsqrt(var + LN_EPS) * g_ref[...] + b_ref[...]
        o_ref[r * sub:(r + 1) * sub, :] = (y * _sigmoid(y)).astype(o_ref.dtype)


def _conv(zc, conv_w, conv_b, ln_g, ln_b, we_gate, we_up, we_down, layer, nb, seq, ts=512):
    n = zc.shape[0]
    tps = seq // ts
    hpt = ts // CONV_HALO
    ne, d = we_gate.shape[1], we_gate.shape[2]
    eps = ne // (nb * tps)
    assert eps * nb * tps == ne
    vec = pl.BlockSpec((1, CONV_WIDTH), lambda b, j: (0, 0))
    w32 = lambda a, c: pl.BlockSpec((None, eps, a, c), lambda b, j: (layer, b * tps + j, 0, 0))
    w16 = lambda a, c: pl.BlockSpec((eps, a, c), lambda b, j: (b * tps + j, 0, 0))
    return pl.pallas_call(
        functools.partial(_conv_kernel, ts=ts),
        grid=(nb, tps),
        in_specs=[
            pl.BlockSpec((ts, COL_CONV), lambda b, j: (b * tps + j, 0)),
            pl.BlockSpec((CONV_HALO, COL_CONV), lambda b, j: (jnp.maximum((b * tps + j) * hpt - 1, 0), 0)),
            pl.BlockSpec((CONV_KERNEL, CONV_WIDTH), lambda b, j: (0, 0)),
            vec, vec, vec,
            w32(d, D_EXPERT), w32(d, D_EXPERT), w32(D_EXPERT, d),
        ],
        out_specs=[pl.BlockSpec((ts, CONV_WIDTH), lambda b, j: (b * tps + j, 0)),
                   w16(d, D_EXPERT), w16(d, D_EXPERT), w16(D_EXPERT, d)],
        out_shape=[jax.ShapeDtypeStruct((n, CONV_WIDTH), BF16), jax.ShapeDtypeStruct((ne, d, D_EXPERT), BF16),
                   jax.ShapeDtypeStruct((ne, d, D_EXPERT), BF16), jax.ShapeDtypeStruct((ne, D_EXPERT, d), BF16)],
        scratch_shapes=[pltpu.VMEM((SUBLANES, CONV_HALO + ts, CONV_WIDTH), F32)],
        compiler_params=_params("parallel", "parallel"),
        name="conv",
    )(zc, zc, conv_w, conv_b, ln_g, ln_b, we_gate, we_up, we_down)


def _prep_kernel(z_ref, halo_ref, mu_ref, w0_ref, a0_ref, wah_ref, wal_ref, g2_ref, kk_ref, ka_ref, rk_ref,
                 ones_ref, tri_ref, csel_ref,
                 at_ref, bt_ref, kt_ref, rt_ref, v_ref, bts_ref, kts_ref, bonus_ref, g_ref, pc_ref):
    j = pl.program_id(1)
    z = z_ref[...]
    ts = z.shape[0]
    last = jnp.where(j > 0, halo_ref[SHIFT_HALO - 1:SHIFT_HALO, :], 0.0)
    row = lax.broadcasted_iota(jnp.int32, z.shape, 0)
    prev = jnp.where(row == 0, last, pltpu.roll(z, 1, axis=0))
    zs = z + (prev - z) * mu_ref[...]
    rw = RWKV_WIDTH
    r = zs[:, :rw]
    k = zs[:, rw:2 * rw]
    v = zs[:, 2 * rw:3 * rw]
    wad = zs[:, 3 * rw:3 * rw + LORA_W + LORA_A]
    gd = zs[:, 3 * rw + LORA_W + LORA_A:]
    lane = lax.broadcasted_iota(jnp.int32, wad.shape, 1)
    xh, xl = _split(jnp.where(lane < LORA_W, jnp.tanh(wad), wad))
    wah = wah_ref[...]
    lora = _dot(xh, wah) + _dot(xl, wah) + _dot(xh, wal_ref[...])
    w = -_softplus(-(w0_ref[...] + lora[:, :rw])) - 0.5
    lw = -jnp.exp(w)
    a = _sigmoid(a0_ref[...] + lora[:, rw:])
    g_ref[...] = _dot(_sigmoid(gd).astype(BF16), g2_ref[...]).astype(g_ref.dtype)
    kk = k * kk_ref[...]
    k2 = k * (1.0 + (a - 1.0) * ka_ref[...])
    ones = ones_ref[...]
    kk = kk * lax.rsqrt(jnp.maximum(_dot_exact_rhs(kk * kk, ones), 1e-24))
    bonus_ref[...] = (_dot_exact_rhs(r * k2 * rk_ref[...], ones) * v).astype(bonus_ref.dtype)
    tri = tri_ref[...]
    lc, tot = [], []
    for h in range(ts // CUM_BLOCK):
        cs = _dot_exact_lhs(tri, lw[h * CUM_BLOCK:(h + 1) * CUM_BLOCK])
        lc.append(cs[:CUM_BLOCK])
        tot.append(cs[CUM_BLOCK:])
    lc = jnp.concatenate(lc, axis=0)
    tot = jnp.concatenate(tot, axis=0)
    e_neg = jnp.exp(-lc)
    e_end = jnp.exp(tot - lc)
    kka = kk * a
    dt = at_ref.dtype
    at_ref[...] = (-kk * jnp.exp(lc - lw)).astype(dt)
    bt_ref[...] = (kka * e_neg).astype(dt)
    kt_ref[...] = (k2 * e_neg).astype(dt)
    rt_ref[...] = (r * jnp.exp(lc)).astype(dt)
    v_ref[...] = v.astype(dt)
    bts_ref[...] = (kka * e_end).astype(dt)
    kts_ref[...] = (k2 * e_end).astype(dt)
    pc_ref[...] = jnp.exp(_dot_exact_lhs(csel_ref[...], lw))


def _prep(zr, mu, w0, a0, wa, g2, k_k, k_a, r_k, nb, seq, ts=512):
    n = zr.shape[0]
    tps = seq // ts
    hpt = ts // SHIFT_HALO
    rw = RWKV_WIDTH
    cpt = ts // CHUNK
    ones = jnp.kron(jnp.eye(N_HEADS, dtype=F32), jnp.ones((HEAD, HEAD), F32)).astype(BF16)
    blk = jnp.kron(jnp.eye(CUM_BLOCK // CHUNK, dtype=F32), jnp.ones((CHUNK, CHUNK), F32))
    tri = jnp.concatenate([jnp.tril(blk), blk], axis=0).astype(BF16)
    csel = jnp.kron(jnp.eye(cpt, dtype=F32), jnp.ones((1, CHUNK), F32)).astype(BF16)
    wa_hi, wa_lo = _split(wa)
    vec = pl.BlockSpec((1, rw), lambda b, j: (0, 0))
    full = lambda shape: pl.BlockSpec(shape, lambda b, j: (0, 0))
    rows = pl.BlockSpec((ts, rw), lambda b, j: (b * tps + j, 0))
    return pl.pallas_call(
        _prep_kernel,
        grid=(nb, tps),
        in_specs=[
            pl.BlockSpec((ts, COL_RWKV), lambda b, j: (b * tps + j, 0)),
            pl.BlockSpec((SHIFT_HALO, COL_RWKV), lambda b, j: (jnp.maximum((b * tps + j) * hpt - 1, 0), 0)),
            full((1, COL_RWKV)), vec, vec,
            full((LORA_W + LORA_A, 2 * rw)), full((LORA_W + LORA_A, 2 * rw)), full((LORA_G, rw)),
            vec, vec, vec, full((rw, rw)), full((2 * CUM_BLOCK, CUM_BLOCK)), full((cpt, ts)),
        ],
        out_specs=[rows] * 9 + [pl.BlockSpec((cpt, rw), lambda b, j: (b * tps + j, 0))],
        out_shape=[jax.ShapeDtypeStruct((n, rw), BF16)] * 9 + [jax.ShapeDtypeStruct((n // CHUNK, rw), F32)],
        compiler_params=_params("parallel", "parallel"),
        name="prep",
    )(zr, zr, mu, w0, a0, wa_hi, wa_lo, g2.astype(BF16), k_k, k_a, r_k, ones, tri, csel)


def _mm(a, b):
    return jnp.dot(a.astype(BF16), b.astype(BF16), preferred_element_type=F32)


def _mm_nt(a, b):
    return lax.dot_general(a.astype(BF16), b.astype(BF16), (((1,), (1,)), ((), ())),
                           preferred_element_type=F32)


def _mm_tn(a, b):
    return lax.dot_general(a.astype(BF16), b.astype(BF16), (((0,), (0,)), ((), ())),
                           preferred_element_type=F32)


def _scan_kernel(at_ref, bt_ref, kt_ref, rt_ref, v_ref, bts_ref, kts_ref, bonus_ref, g_ref, pc_ref,
                 gng_ref, gnb_ref, ones_ref, o_ref, s_ref, y_ref, *, ts, gb):
    @pl.when(pl.program_id(1) == 0)
    def _():
        s_ref[...] = jnp.zeros_like(s_ref)

    row_q = lax.broadcasted_iota(jnp.int32, (QUAD, QUAD), 0)
    col_q = lax.broadcasted_iota(jnp.int32, (QUAD, QUAD), 1)
    bdmask = (row_q // HEAD) == (col_q // HEAD)
    row_c = lax.broadcasted_iota(jnp.int32, (CHUNK, QUAD), 0)
    col_c = lax.broadcasted_iota(jnp.int32, (CHUNK, QUAD), 1) % CHUNK
    strict = col_c < row_c
    incl = col_c <= row_c
    eye = jnp.where(col_c == row_c, 1.0, 0.0).astype(F32)

    def bd(x):
        xb = x.astype(BF16)
        return jnp.where(bdmask, jnp.concatenate([xb] * (QUAD // CHUNK), axis=0), jnp.zeros((), BF16))

    chains = [(b, qd) for b in range(gb) for qd in range(N_QUADS)]

    def chunk(c, carry):
        sl = pl.ds(pl.multiple_of(c * CHUNK, CHUNK), CHUNK)
        lanes = [slice(qd * QUAD, (qd + 1) * QUAD) for _, qd in chains]
        ld = lambda ref: [ref[b, sl, ln] for (b, _), ln in zip(chains, lanes)]
        at, bt, kt, rt, v, bts, kts = (ld(r) for r in (at_ref, bt_ref, kt_ref, rt_ref, v_ref, bts_ref, kts_ref))
        each = range(len(chains))
        ar = [jnp.concatenate([at[i], rt[i]], axis=0) for i in each]
        pb = [_mm_nt(ar[i], bd(bt[i])) for i in each]
        pk = [_mm_nt(ar[i], bd(kt[i])) for i in each]
        l_ab = [jnp.where(strict, p[:CHUNK], 0.0) for p in pb]
        a_rb = [jnp.where(incl, p[CHUNK:], 0.0) for p in pb]
        l_ak = [jnp.where(strict, p[:CHUNK], 0.0) for p in pk]
        a_rk = [jnp.where(incl, p[CHUNK:], 0.0) for p in pk]
        tm = [eye + l for l in l_ab]
        lp = [_mm(l, bd(l)) for l in l_ab]
        for _ in range(4):
            both = [_mm(jnp.concatenate([tm[i], lp[i]], axis=0), bd(lp[i])) for i in each]
            tm = [tm[i] + both[i][:CHUNK] for i in each]
            lp = [both[i][CHUNK:] for i in each]
        tm = [tm[i] + _mm(tm[i], bd(lp[i])) for i in each]
        bdv = [bd(x) for x in v]
        x0 = [_mm(l_ak[i], bdv[i]) for i in each]
        tu = [_mm(tm[i], jnp.concatenate([bd(x0[i]), bd(at[i])], axis=1)) for i in each]
        u0 = [t[:, :QUAD] for t in tu]
        w = [t[:, QUAD:] for t in tu]
        qy = [_mm(a_rb[i], jnp.concatenate([bd(w[i]), bd(u0[i])], axis=1)) for i in each]
        y0 = [qy[i][:, QUAD:] + _mm(a_rk[i], bdv[i]) for i in each]
        q = [rt[i].astype(F32) + qy[i][:, :QUAD] for i in each]
        s = [s_ref[i] for i in each]
        qw_s = [_mm_nt(jnp.concatenate([q[i], w[i]], axis=0), s[i]) for i in each]
        for i, ((b, _), ln) in enumerate(zip(chains, lanes)):
            y_ref[b, sl, ln] = y0[i] + qw_s[i][:CHUNK]
        u = [(u0[i] + qw_s[i][CHUNK:]).astype(BF16) for i in each]
        n_p = [jnp.where(bdmask, _mm_tn(jnp.concatenate([u[i], v[i]], axis=0),
                                        jnp.concatenate([bts[i], kts[i]], axis=0)), 0.0) for i in each]
        for i, ((b, _), ln) in enumerate(zip(chains, lanes)):
            s_ref[i] = s[i] * pc_ref[b, pl.ds(c, 1), ln] + n_p[i]
        return carry

    lax.fori_loop(0, ts // CHUNK, chunk, 0)

    ones = ones_ref[...]
    for b in range(gb):
        for qd in range(N_QUADS):
            ln = slice(qd * QUAD, (qd + 1) * QUAD)
            y = y_ref[b, :, ln]
            dlt = y - _dot_exact_rhs(y, ones) * (1.0 / HEAD)
            var = _dot_exact_rhs(dlt * dlt, ones) * (1.0 / HEAD)
            yn = dlt * lax.rsqrt(var + GN_EPS) * gng_ref[:, ln] + gnb_ref[:, ln]
            o_ref[b, :, ln] = ((yn + bonus_ref[b, :, ln].astype(F32)) * g_ref[b, :, ln].astype(F32)).astype(o_ref.dtype)


def _scan(prep_out, gn_g, gn_b, nb, seq, ts=128, gb=8):
    n = prep_out[0].shape[0]
    rw = RWKV_WIDTH
    cpt = ts // CHUNK
    ones = jnp.kron(jnp.eye(QUAD // HEAD, dtype=F32), jnp.ones((HEAD, HEAD), F32)).astype(BF16)
    rows = pl.BlockSpec((gb, ts, rw), lambda i, t: (i, t, 0))
    vec = pl.BlockSpec((1, rw), lambda i, t: (0, 0))
    args = [a.reshape(nb, seq, rw) for a in prep_out[:9]] + [prep_out[9].reshape(nb, seq // ts, cpt, rw)]
    out = pl.pallas_call(
        functools.partial(_scan_kernel, ts=ts, gb=gb),
        grid=(nb // gb, seq // ts),
        in_specs=[rows] * 9 + [pl.BlockSpec((gb, None, cpt, rw), lambda i, t: (i, t, 0, 0)), vec, vec,
                               pl.BlockSpec((QUAD, QUAD), lambda i, t: (0, 0))],
        out_specs=rows,
        out_shape=jax.ShapeDtypeStruct((nb, seq, rw), BF16),
        scratch_shapes=[pltpu.VMEM((gb * N_QUADS, QUAD, QUAD), F32), pltpu.VMEM((gb, ts, rw), F32)],
        compiler_params=_params("parallel", "arbitrary"),
        name="scan",
    )(*args, gn_g, gn_b, ones)
    return out.reshape(n, rw)


def _merge_kernel(uc_ref, ur_ref, zg_ref, x_ref, pc_ref, pr_ref, wo_ref, gpost_ref, gt_ref,
                  gpre_ref, sc_ref, sh_ref, wrh_ref, wrl_ref, xo_ref, h_ref, lg_ref):
    yc = _dot(uc_ref[...], pc_ref[...])
    yr = _dot(ur_ref[...], pr_ref[...])
    zg = zg_ref[...].astype(F32)
    m = _sigmoid(zg[:, :D_MODEL]) * yc + _sigmoid(zg[:, D_MODEL:]) * yr
    y = _dot(m.astype(BF16), wo_ref[...])
    xn = x_ref[...] + gt_ref[...] * _rms(y, gpost_ref[...])
    xo_ref[...] = xn
    h = _rms(xn, gpre_ref[...]) * (1.0 + sc_ref[...]) + sh_ref[...]
    h_ref[...] = _pack_rows(h)
    h_hi, h_lo = _split(h)
    nt = lambda a, b: lax.dot_general(a, b, (((1,), (1,)), ((), ())), preferred_element_type=F32)
    w_hi = wrh_ref[...]
    lg_ref[...] = nt(w_hi, h_hi) + nt(w_hi, h_lo) + nt(wrl_ref[...], h_hi)


def _merge(uc, ur, zg, x2, p_conv_b, p_rwkv_b, w_o_b, g_post, g_pre, mod, layer, w_router_t, seq, tm=512):
    n, d = x2.shape
    row = lambda i: (i, 0)
    full = lambda shape: pl.BlockSpec(shape, lambda i: (0, 0))
    return pl.pallas_call(
        _merge_kernel,
        grid=(n // tm,),
        in_specs=[
            pl.BlockSpec((tm, CONV_WIDTH), row), pl.BlockSpec((tm, RWKV_WIDTH), row),
            pl.BlockSpec((tm, COL_GATE), row), pl.BlockSpec((tm, d), row),
            full((CONV_WIDTH, d)), full((RWKV_WIDTH, d)), full((d, d)),
            full((1, d)), _mod_spec(layer, 2, seq, tm),
            full((1, d)), _mod_spec(layer, 4, seq, tm), _mod_spec(layer, 3, seq, tm),
            full((N_EXPERTS, d)), full((N_EXPERTS, d)),
        ],
        out_specs=[pl.BlockSpec((tm, d), row), pl.BlockSpec((tm, ROW_WORDS), row),
                   pl.BlockSpec((N_EXPERTS, tm), lambda i: (0, i))],
        out_shape=[jax.ShapeDtypeStruct((n, d), F32), jax.ShapeDtypeStruct((n, ROW_WORDS), jnp.int32),
                   jax.ShapeDtypeStruct((N_EXPERTS, n), F32)],
        compiler_params=_params("parallel"),
        name="merge",
    )(uc, ur, zg, x2, p_conv_b, p_rwkv_b, w_o_b, g_post, mod, g_pre, mod, mod, *_split(w_router_t))


def _route_kernel(lg_ref, b_ref, gate_ref, sel_ref, cnt_ref):
    s = _sigmoid(lg_ref[...])
    biased = s + b_ref[...]
    t = s.shape[1]
    member = lax.broadcasted_iota(jnp.int32, (GROUP_SIZE, t), 0)
    grp = []
    for g in range(N_GROUPS):
        bg = biased[g * GROUP_SIZE:(g + 1) * GROUP_SIZE, :]
        m1 = jnp.max(bg, axis=0, keepdims=True)
        first = jnp.min(jnp.where(bg == m1, member, GROUP_SIZE), axis=0, keepdims=True)
        m2 = jnp.max(jnp.where(member == first, -jnp.inf, bg), axis=0, keepdims=True)
        grp.append(m1 + m2)
    masked = []
    for g in range(N_GROUPS):
        rank = jnp.zeros((1, t), jnp.int32)
        for o in range(N_GROUPS):
            if o == g:
                continue
            ahead = (grp[o] > grp[g]) if o > g else (grp[o] >= grp[g])
            rank = rank + jnp.where(ahead, 1, 0)
        keep = rank < TOPK_GROUPS
        masked.append(jnp.where(keep, biased[g * GROUP_SIZE:(g + 1) * GROUP_SIZE, :], -jnp.inf))
    masked = jnp.concatenate(masked, axis=0)
    eidx = lax.broadcasted_iota(jnp.int32, masked.shape, 0)
    mask = jnp.zeros(masked.shape, F32)
    for _ in range(TOP_K):
        top = jnp.max(masked, axis=0, keepdims=True)
        hit = eidx == jnp.min(jnp.where(masked == top, eidx, N_EXPERTS), axis=0, keepdims=True)
        mask = jnp.where(hit, 1.0, mask)
        masked = jnp.where(hit, -jnp.inf, masked)
    sel = s * mask
    gate_ref[...] = sel / jnp.sum(sel, axis=0, keepdims=True) * ROUTED_SCALE
    sel_ref[...] = mask.astype(sel_ref.dtype)

    @pl.when(pl.program_id(0) == 0)
    def _():
        cnt_ref[...] = jnp.zeros_like(cnt_ref)

    cnt_ref[...] += jnp.sum(mask, axis=1, keepdims=True)


def _route(logits_t, b_router, tt=512):
    e, n = logits_t.shape
    return pl.pallas_call(
        _route_kernel,
        grid=(n // tt,),
        in_specs=[pl.BlockSpec((e, tt), lambda i: (0, i)), pl.BlockSpec((e, 1), lambda i: (0, 0))],
        out_specs=[pl.BlockSpec((e, tt), lambda i: (0, i)), pl.BlockSpec((e, tt), lambda i: (0, i)),
                   pl.BlockSpec((e, GATE_LANES), lambda i: (0, 0))],
        out_shape=[jax.ShapeDtypeStruct((e, n), F32), jax.ShapeDtypeStruct((e, n), BF16),
                   jax.ShapeDtypeStruct((e, GATE_LANES), F32)],
        compiler_params=_params("arbitrary"),
        name="route",
    )(logits_t, b_router)


def _dest_kernel(sel_ref, gate_ref, start_ref, triu_ref, below_ref, dest_ref, g8_ref, run_ref):
    @pl.when(pl.program_id(0) == 0)
    def _():
        run_ref[...] = jnp.zeros_like(run_ref)

    sel = sel_ref[...]
    t = sel.shape[1]
    self32 = sel.astype(F32)
    incl = _dot(sel, triu_ref[...])
    pos = start_ref[...] + run_ref[...] + incl - self32
    run_ref[...] += incl[:, t - 1:t]
    slot = _dot(below_ref[...], sel)
    gate = gate_ref[...]
    dst, gts = [], []
    for k in range(TOP_K):
        mine = (self32 > 0.0) & (slot == float(k))
        dst.append(jnp.sum(jnp.where(mine, pos, 0.0), axis=0, keepdims=True))
        gts.append(jnp.sum(jnp.where(mine, gate, 0.0), axis=0, keepdims=True))
    dest_ref[...] = jnp.concatenate(dst, axis=0).astype(jnp.int32)
    g8_ref[...] = jnp.concatenate(gts + [jnp.zeros((GATE_LANES - TOP_K, t), F32)], axis=0).T


def _dest(sel, gate, seg_start, tt=512):
    e, n = sel.shape
    triu = jnp.triu(jnp.ones((tt, tt), F32)).astype(BF16)
    below = jnp.tril(jnp.ones((e, e), F32), -1).astype(BF16)
    return pl.pallas_call(
        _dest_kernel,
        grid=(n // tt,),
        in_specs=[pl.BlockSpec((e, tt), lambda i: (0, i)), pl.BlockSpec((e, tt), lambda i: (0, i)),
                  pl.BlockSpec((e, 1), lambda i: (0, 0)), pl.BlockSpec((tt, tt), lambda i: (0, 0)),
                  pl.BlockSpec((e, e), lambda i: (0, 0))],
        out_specs=[pl.BlockSpec((TOP_K, tt), lambda i: (0, i)), pl.BlockSpec((tt, GATE_LANES), lambda i: (i, 0))],
        out_shape=[jax.ShapeDtypeStruct((TOP_K, n), jnp.int32), jax.ShapeDtypeStruct((n, GATE_LANES), F32)],
        scratch_shapes=[pltpu.VMEM((e, 1), F32)],
        compiler_params=_params("arbitrary"),
        name="dest",
    )(sel, gate, seg_start, triu, below)


def _sc_mesh():
    return plsc.VectorSubcoreMesh(core_axis_name="c", subcore_axis_name="s",
                                  num_cores=SC_CORES, num_subcores=SC_SUBCORES)


def _worker_id():
    return lax.axis_index("s") * SC_CORES + lax.axis_index("c")


def _sc_scatter_rows(rows, dest, n_out):
    n, w = rows.shape
    per_worker = n // SC_WORKERS
    chunks_per_worker = per_worker // SC_CHUNK
    idx_rows = dest.reshape(TOP_K, n // SC_CHUNK, SC_CHUNK).transpose(1, 0, 2).reshape(-1, SC_CHUNK)

    def body(rows_hbm, idx_hbm, out_hbm, idx_v, rows_v):
        wid = _worker_id()

        @pl.loop(0, chunks_per_worker)
        def _(c):
            chunk = wid * chunks_per_worker + c
            pltpu.sync_copy(rows_hbm.at[pl.ds(chunk * SC_CHUNK, SC_CHUNK)], rows_v)
            pltpu.sync_copy(idx_hbm.at[pl.ds(chunk * TOP_K, TOP_K)], idx_v)
            for k in range(TOP_K):
                pltpu.sync_copy(rows_v, out_hbm.at[idx_v.at[k]])

    return pl.kernel(
        body, out_type=jax.ShapeDtypeStruct((n_out, w), rows.dtype), mesh=_sc_mesh(),
        scratch_types=[pltpu.VMEM((TOP_K, SC_CHUNK), jnp.int32), pltpu.VMEM((SC_CHUNK, w), rows.dtype)],
        name="sc_scatter",
    )(rows, idx_rows)


def _sc_gather_rows(table, idx_flat):
    m = idx_flat.shape[0]
    w = table.shape[1]
    ch = SC_CHUNK // 2
    chunks_per_worker = m // SC_WORKERS // ch
    idx_rows = idx_flat.reshape(-1, ch)
    ng = SC_IDX_ROWS

    def body(table_hbm, idx_hbm, out_hbm, idx_v, rows_a, rows_b, gat_a, gat_b, put_a, put_b):
        wid = _worker_id()
        bufs = ((rows_a, gat_a, put_a), (rows_b, gat_b, put_b))

        @pl.loop(0, chunks_per_worker // ng)
        def _(g):
            first = wid * chunks_per_worker + g * ng
            pltpu.sync_copy(idx_hbm.at[pl.ds(first, ng)], idx_v)
            gathers, stores = [None] * ng, [None] * ng
            gathers[0] = pltpu.async_copy(table_hbm.at[idx_v.at[0]], rows_a, gat_a)
            for j in range(ng):
                buf, _, put = bufs[j % 2]
                if j + 1 < ng:
                    if j >= 1:
                        stores[j - 1].wait()
                    nxt, gat, _ = bufs[(j + 1) % 2]
                    gathers[j + 1] = pltpu.async_copy(table_hbm.at[idx_v.at[j + 1]], nxt, gat)
                gathers[j].wait()
                stores[j] = pltpu.async_copy(buf, out_hbm.at[pl.ds((first + j) * ch, ch)], put)
            stores[ng - 2].wait()
            stores[ng - 1].wait()

    return pl.kernel(
        body, out_type=jax.ShapeDtypeStruct((m, w), table.dtype), mesh=_sc_mesh(),
        scratch_types=[pltpu.VMEM((ng, ch), jnp.int32), pltpu.VMEM((ch, w), table.dtype),
                       pltpu.VMEM((ch, w), table.dtype)] + [pltpu.SemaphoreType.DMA] * 4,
        name="sc_gather",
    )(table, idx_rows)


def _swiglu(h, wg, wu):
    gte = _dot(h, wg)
    return gte * _sigmoid(gte) * _dot(h, wu)


def _experts_kernel(be_ref, va_ref, xs_hbm, *refs):
    nb = BLOCKS_PER_STEP
    wg, wu, wd, ys_ref = refs[:nb], refs[nb:2 * nb], refs[2 * nb:3 * nb], refs[3 * nb]
    xs_buf, xs_sem = refs[3 * nb + 1], refs[3 * nb + 2]
    step = pl.program_id(0)
    first = step * nb
    rows = nb * EXPERT_BLOCK

    def fetch(i):
        slot = i % XS_SLOTS
        return pltpu.make_async_copy(xs_hbm.at[pl.ds(pl.multiple_of(i * rows, rows), rows)],
                                     xs_buf.at[slot], xs_sem.at[slot])

    @pl.when(step == 0)
    def _():
        fetch(0).start()
        fetch(1).start()

    @pl.when(step + 2 < pl.num_programs(0))
    def _():
        fetch(step + 2).start()

    fetch(step).wait()
    xs_ref = xs_buf.at[step % XS_SLOTS]

    @pl.when(va_ref[first] > 0)
    def _():
        row = lax.broadcasted_iota(jnp.int32, (EXPERT_SUB, D_MODEL), 0)
        subs = [(b, r0) for b in range(nb) for r0 in range(0, EXPERT_BLOCK, EXPERT_SUB)]
        x = []
        for b, r0 in subs:
            lo = b * EXPERT_BLOCK + r0
            xi = _unpack_rows(xs_ref[lo:lo + EXPERT_SUB, :])
            x.append(jnp.where(row < va_ref[first + b] - r0, xi, 0.0).astype(BF16))
        gte = [_dot(xi, wg[b][...]) for xi, (b, _) in zip(x, subs)]
        up = [_dot(xi, wu[b][...]) for xi, (b, _) in zip(x, subs)]
        act = [(g * _sigmoid(g) * u).astype(BF16) for g, u in zip(gte, up)]
        y = [_dot(a, wd[b][...]) for a, (b, _) in zip(act, subs)]
        for yi, (b, r0) in zip(y, subs):
            lo = b * EXPERT_BLOCK + r0
            ys_ref[lo:lo + EXPERT_SUB, :] = _pack_rows(yi)


def _experts(xs, blk_expert, blk_valid, wg, wu, wd):
    n_rows, w = xs.shape
    d = wg.shape[1]
    nb = BLOCKS_PER_STEP
    rows = pl.BlockSpec((nb * EXPERT_BLOCK, w), lambda s, be, va: (s, 0))
    up_spec = lambda i: pl.BlockSpec((None, d, D_EXPERT), lambda s, be, va: (be[s * nb + i], 0, 0))
    down_spec = lambda i: pl.BlockSpec((None, D_EXPERT, d), lambda s, be, va: (be[s * nb + i], 0, 0))
    return pl.pallas_call(
        _experts_kernel,
        grid_spec=pltpu.PrefetchScalarGridSpec(
            num_scalar_prefetch=2,
            grid=(n_rows // (nb * EXPERT_BLOCK),),
            in_specs=([pl.BlockSpec(memory_space=pl.ANY)] + [up_spec(i) for i in range(nb)] * 2
                      + [down_spec(i) for i in range(nb)]),
            out_specs=rows,
            scratch_shapes=[pltpu.VMEM((XS_SLOTS, nb * EXPERT_BLOCK, w), xs.dtype),
                            pltpu.SemaphoreType.DMA((XS_SLOTS,))],
        ),
        out_shape=jax.ShapeDtypeStruct((n_rows, w), xs.dtype),
        compiler_params=_params("arbitrary"),
        name="experts",
    )(blk_expert, blk_valid, xs, *([wg] * nb), *([wu] * nb), *([wd] * nb))


def _combine_kernel(yg_ref, g8_ref, hp_ref, sg_ref, su_ref, sd_ref, x_ref, gpost_ref, gt_ref, o_ref):
    h = _unpack_rows(hp_ref[...]).astype(BF16)
    acc = _dot(_swiglu(h, sg_ref[...], su_ref[...]).astype(BF16), sd_ref[...])
    g8 = g8_ref[...]
    for k in range(TOP_K):
        acc = acc + g8[:, k:k + 1] * _unpack_rows(yg_ref[k])
    o_ref[...] = x_ref[...] + gt_ref[...] * _rms(acc, gpost_ref[...])


def _combine(yg, g8, hp, sg, su, sd, x2, g_post, mod, layer, seq, tm=512):
    n, d = x2.shape
    row = lambda i: (i, 0)
    full = lambda shape: pl.BlockSpec(shape, lambda i: (0, 0))
    in_specs = [
        pl.BlockSpec((TOP_K, tm, ROW_WORDS), lambda i: (0, i, 0)),
        pl.BlockSpec((tm, GATE_LANES), row),
        pl.BlockSpec((tm, ROW_WORDS), row),
        full((d, D_EXPERT)), full((d, D_EXPERT)), full((D_EXPERT, d)),
        pl.BlockSpec((tm, d), row),
        full((1, d)),
        _mod_spec(layer, 5, seq, tm),
    ]
    return pl.pallas_call(
        _combine_kernel,
        grid=(n // tm,),
        in_specs=in_specs,
        out_specs=pl.BlockSpec((tm, d), row),
        out_shape=jax.ShapeDtypeStruct((n, d), F32),
        compiler_params=_params("parallel"),
        name="combine",
    )(yg, g8, hp, sg, su, sd, x2, g_post, mod)


def _moe(hp, logits_t, b_router, wg, wu, wd, sg, su, sd, x2, g_post, mod, layer, seq):
    n = x2.shape[0]
    gate, sel, counts = _route(logits_t, b_router)
    cnt = counts[:, 0].astype(jnp.int32)
    padded = (cnt + EXPERT_BLOCK - 1) // EXPERT_BLOCK * EXPERT_BLOCK
    seg_end = jnp.cumsum(padded)
    seg_start = seg_end - padded
    n_blocks = n * TOP_K // EXPERT_BLOCK + N_EXPERTS
    blk_row = jnp.arange(n_blocks, dtype=jnp.int32) * EXPERT_BLOCK
    blk_expert = jnp.sum((seg_end[None, :] <= blk_row[:, None]).astype(jnp.int32), axis=1)
    blk_expert = jnp.minimum(blk_expert, N_EXPERTS - 1)
    mine = blk_expert[:, None] == jnp.arange(N_EXPERTS, dtype=jnp.int32)[None, :]
    seg_last = jnp.sum(jnp.where(mine, (seg_start + cnt)[None, :], 0), axis=1)
    blk_valid = jnp.clip(seg_last - blk_row, 0, EXPERT_BLOCK).astype(jnp.int32)
    dest, g8 = _dest(sel, gate, seg_start.astype(F32).reshape(-1, 1))
    dest_flat = dest.reshape(-1)
    xs = _sc_scatter_rows(hp, dest, n_blocks * EXPERT_BLOCK)
    ys = _experts(xs, blk_expert, blk_valid, wg, wu, wd)
    yg = _sc_gather_rows(ys, dest_flat).reshape(TOP_K, n, ROW_WORDS)
    return _combine(yg, g8, hp, sg, su, sd, x2, g_post, mod, layer, seq)


def kernel(x, c, w_ada, b_ada, norm_mix_pre, norm_mix_post, norm_ffn_pre, norm_ffn_post, w_in, mu_shift, conv_w, conv_b, conv_ln_g, conv_ln_b, p_conv, w0, w2, a0, a2, g2, k_k, k_a, r_k, gn_g, gn_b, p_rwkv, w_o, w_router, b_router, we_gate, we_up, we_down, ws_gate, ws_up, ws_down):
    nb, seq, d = x.shape
    depth = w_ada.shape[0]
    n = nb * seq
    rw = RWKV_WIDTH
    mod = _ada(c, w_ada, b_ada)
    x2 = x.reshape(n, d)
    row = lambda a: a.reshape(1, -1)
    bf = lambda a: a.astype(BF16)
    for l in range(depth):
        zc, zr, zg = _win(x2, row(norm_mix_pre[l]), mod, l, bf(w_in[l]), seq)
        uc, wg_b, wu_b, wd_b = _conv(zc, conv_w[l], row(conv_b[l]), row(conv_ln_g[l]), row(conv_ln_b[l]),
                                     we_gate, we_up, we_down, l, nb, seq)
        wa = jnp.zeros((LORA_W + LORA_A, 2 * rw), F32)
        wa = wa.at[:LORA_W, :rw].set(w2[l]).at[LORA_W:, rw:].set(a2[l])
        prep_out = _prep(zr, row(mu_shift[l]), row(w0[l]), row(a0[l]), wa, g2[l], row(k_k[l]),
                         row(k_a[l]), row(r_k[l]), nb, seq)
        ur = _scan(prep_out, row(gn_g[l]), row(gn_b[l]), nb, seq)
        x2, hp, logits_t = _merge(uc, ur, zg, x2, bf(p_conv[l]), bf(p_rwkv[l]), bf(w_o[l]),
                                  row(norm_mix_post[l]), row(norm_ffn_pre[l]), mod, l, w_router[l].T, seq)
        x2 = _moe(hp, logits_t, b_router[l].reshape(-1, 1), wg_b, wu_b, wd_b, bf(ws_gate[l]), bf(ws_up[l]),
                  bf(ws_down[l]), x2, row(norm_ffn_post[l]), mod, l, seq)
    return x2.reshape(nb, seq, d)
```
